```python
import math
import jax, jax.numpy as jnp
from jax import lax
import numpy as np

D_MODEL = 1024
BATCH = 16
SEQ = 2048
DEPTH = 2

N_META = 16
N_EVEN = (DEPTH + 1) // 2
N_ODD = DEPTH // 2

S5_WIDTH = D_MODEL // 2
S5_GROUP = 16
S5_GROUPS = S5_WIDTH // S5_GROUP
S5_STATE = 64
S5_DT_MIN = 0.001
S5_DT_MAX = 0.1

SB_HEAD_DIM = 64
SB_WIDTH = D_MODEL // 2
SB_HEADS = SB_WIDTH // SB_HEAD_DIM
SB_BLOCK = 128

HG_WIDTH = D_MODEL
HG_DK = 128
HG_HEADS = HG_WIDTH // HG_DK
HG_DV = HG_WIDTH // HG_HEADS
HG_CHUNK = 64

D_FF = 4 * D_MODEL
IN_AB = S5_WIDTH + 3 * SB_WIDTH
IN_C = 4 * HG_WIDTH

ALPHA = (2.0 * DEPTH) ** 0.25
BETA = (8.0 * DEPTH) ** -0.25
LN_EPS = 1e-5
RMS_EPS = 1e-6

kernel_name = 'hybrid_s5_stickbreak_hgrn2_deepnorm_meta'


def layer_norm(x, g, b):
    xf = x.astype(jnp.float32)
    mu = jnp.mean(xf, axis=-1, keepdims=True)
    var = jnp.mean(jnp.square(xf - mu), axis=-1, keepdims=True)
    return (xf - mu) * lax.rsqrt(var + LN_EPS) * g + b


def s5_mixer(u, lam_re, lam_im, log_dt, b_re, b_im, c_re, c_im, d, w_glu, b_glu):
    bn, seq_len, _ = u.shape
    f32 = jnp.float32
    uf = u.astype(f32).reshape(bn, seq_len, S5_GROUPS, S5_GROUP)
    lam = lax.complex(lam_re.astype(f32), lam_im.astype(f32))
    dt = jnp.exp(log_dt.astype(f32))[:, None]
    lam_bar = jnp.exp(lam * dt)
    b_mat = lax.complex(b_re.astype(f32), b_im.astype(f32))
    b_bar = ((lam_bar - 1.0) / lam)[:, :, None] * b_mat
    bu = jnp.einsum('blgh,gph->blgp', uf.astype(jnp.complex64), b_bar)
    a = jnp.broadcast_to(lam_bar, (1, seq_len) + lam_bar.shape)

    def combine(left, right):
        a_l, s_l = left
        a_r, s_r = right
        return a_r * a_l, a_r * s_l + s_r

    _, states = lax.associative_scan(combine, (a, bu), axis=1)
    c_mat = lax.complex(c_re.astype(f32), c_im.astype(f32))
    y = jnp.real(jnp.einsum('blgp,ghp->blgh', states, c_mat)) + d.astype(f32) * uf
    y = jax.nn.gelu(y.reshape(bn, seq_len, S5_WIDTH))
    return y * jax.nn.sigmoid(y @ w_glu + b_glu)


def _stick_breaking_block(qb, kb, vb, start):
    z = jnp.einsum('bhqd,bhkd->bhqk', qb, kb)
    nq = qb.shape[2]
    nk = kb.shape[2]
    visible = jnp.arange(nk)[None, :] < (start + jnp.arange(nq))[:, None]
    log_keep = jnp.where(visible, jax.nn.log_sigmoid(-z), 0.0)
    later = lax.cumsum(log_keep, axis=3, reverse=True) - log_keep
    w = jnp.where(visible, jnp.exp(jax.nn.log_sigmoid(z) + later), 0.0)
    return jnp.einsum('bhqk,bhkd->bhqd', w, vb)


def stick_breaking_attention(q, k, v):
    bn, seq_len = q.shape[0], q.shape[1]
    f32 = jnp.float32
    q = q.astype(f32).transpose(0, 2, 1, 3) * (SB_HEAD_DIM ** -0.5)
    k = k.astype(f32).transpose(0, 2, 1, 3)
    v = v.astype(f32).transpose(0, 2, 1, 3)
    blocks = [(0, N_META)] + [(N_META + i * SB_BLOCK, SB_BLOCK)
                              for i in range((seq_len - N_META) // SB_BLOCK)]
    outs = [_stick_breaking_block(q[:, :, s:s + n], k[:, :, :s + n], v[:, :, :s + n], s)
            for s, n in blocks]
    o = jnp.concatenate(outs, axis=2)
    return o.transpose(0, 2, 1, 3).reshape(bn, seq_len, SB_HEADS * SB_HEAD_DIM)


def hgrn_lower_bound(gamma, layer):
    p = jax.nn.softmax(gamma.astype(jnp.float32), axis=0)
    return (jnp.cumsum(p, axis=0) - p[0])[layer]


def _to_chunks(t, n_chunks):
    bn, _, nh, dd = t.shape
    return t.reshape(bn, n_chunks, HG_CHUNK, nh, dd).transpose(1, 0, 3, 2, 4)


def hgrn2_mixer(q, f_pre, i_in, g, lb, norm_g):
    bn, seq_len, _ = q.shape
    f32 = jnp.float32
    f = lb + (1.0 - lb) * jax.nn.sigmoid(f_pre.astype(f32))
    log_f = jnp.log(f)
    k = 1.0 - f
    pad = HG_CHUNK - N_META
    n_chunks = (seq_len + pad) // HG_CHUNK

    def heads(t, dd):
        t = t.astype(f32).reshape(bn, seq_len, HG_HEADS, dd)
        return _to_chunks(jnp.pad(t, ((0, 0), (pad, 0), (0, 0), (0, 0))), n_chunks)

    qh, kh, vh, lh = heads(q, HG_DK), heads(k, HG_DK), heads(i_in, HG_DV), heads(log_f, HG_DK)
    causal = jnp.tril(jnp.ones((HG_CHUNK, HG_CHUNK), dtype=bool))

    def step(state, xs):
        qc, kc, vc, lc = xs
        b = jnp.cumsum(lc, axis=2)
        b_last = b[:, :, -1:, :]
        q_dec = qc * jnp.exp(b)
        scores = jnp.einsum('bhtd,bhsd->bhts', q_dec, kc * jnp.exp(-b))
        scores = jnp.where(causal, scores, 0.0)
        o = (jnp.einsum('bhts,bhse->bhte', scores, vc)
             + jnp.einsum('bhtd,bhde->bhte', q_dec, state))
        state = (jnp.exp(b_last[:, :, 0, :])[..., None] * state
                 + jnp.einsum('bhsd,bhse->bhde', kc * jnp.exp(b_last - b), vc))
        return state, o

    s0 = jnp.zeros((bn, HG_HEADS, HG_DK, HG_DV), f32)
    _, o = lax.scan(step, s0, (qh, kh, vh, lh))
    o = o.transpose(1, 0, 3, 2, 4).reshape(bn, seq_len + pad, HG_HEADS, HG_DV)[:, pad:]
    o = o * lax.rsqrt(jnp.mean(jnp.square(o), axis=-1, keepdims=True) + RMS_EPS)
    o = o * norm_g.astype(f32).reshape(HG_HEADS, HG_DV)
    return o.reshape(bn, seq_len, HG_WIDTH) * jax.nn.silu(g.astype(f32))


def _fwd_setup_inputs(seed: int = 0) -> dict:
    key = jax.random.key(seed)
    ks = jax.random.split(key, 32)
    f32 = jnp.float32
    nrm = lambda k, shape, scale: jax.random.normal(k, shape, f32) * scale
    n_arange = jnp.pi * jnp.arange(S5_STATE, dtype=f32)
    return {
        'x': nrm(ks[0], (BATCH, SEQ, D_MODEL), 1.0),
        'meta': nrm(ks[1], (N_META, D_MODEL), 1.0),
        'w_in_ab': nrm(ks[2], (N_EVEN, D_MODEL, IN_AB), D_MODEL ** -0.5),
        's5_lam_re': -0.5 * jnp.exp(nrm(ks[3], (N_EVEN, S5_GROUPS, S5_STATE), 0.05)),
        's5_lam_im': n_arange + nrm(ks[4], (N_EVEN, S5_GROUPS, S5_STATE), 0.01),
        's5_log_dt': jax.random.uniform(ks[5], (N_EVEN, S5_GROUPS), f32,
                                        minval=math.log(S5_DT_MIN), maxval=math.log(S5_DT_MAX)),
        's5_b_re': nrm(ks[6], (N_EVEN, S5_GROUPS, S5_STATE, S5_GROUP), (2.0 * S5_GROUP) ** -0.5),
        's5_b_im': nrm(ks[7], (N_EVEN, S5_GROUPS, S5_STATE, S5_GROUP), (2.0 * S5_GROUP) ** -0.5),
        's5_c_re': nrm(ks[8], (N_EVEN, S5_GROUPS, S5_GROUP, S5_STATE), S5_STATE ** -0.5),
        's5_c_im': nrm(ks[9], (N_EVEN, S5_GROUPS, S5_GROUP, S5_STATE), S5_STATE ** -0.5),
        's5_d': nrm(ks[10], (N_EVEN, S5_GROUPS, S5_GROUP), 1.0),
        's5_w_glu': nrm(ks[11], (N_EVEN, S5_WIDTH, S5_WIDTH), S5_WIDTH ** -0.5),
        's5_b_glu': nrm(ks[12], (N_EVEN, S5_WIDTH), 0.01),
        'w_out_ab': nrm(ks[13], (N_EVEN, S5_WIDTH + SB_WIDTH, D_MODEL), BETA * (S5_WIDTH + SB_WIDTH) ** -0.5),
        'w_in_c': nrm(ks[14], (N_ODD, D_MODEL, IN_C), D_MODEL ** -0.5),
        'hgrn_gamma': nrm(ks[15], (DEPTH, HG_WIDTH), 0.1),
        'hgrn_norm_g': 1.0 + nrm(ks[16], (N_ODD, HG_WIDTH), 0.01),
        'w_out_c': nrm(ks[17], (N_ODD, HG_WIDTH, D_MODEL), BETA * HG_WIDTH ** -0.5),
        'ln_mix_g': 1.0 + nrm(ks[18], (DEPTH, D_MODEL), 0.01),
        'ln_mix_b': nrm(ks[19], (DEPTH, D_MODEL), 0.01),
        'mlp_w_up': nrm(ks[20], (DEPTH, D_MODEL, D_FF), D_MODEL ** -0.5),
        'mlp_b_up': nrm(ks[21], (DEPTH, D_FF), 0.01),
        'mlp_w_down': nrm(ks[22], (DEPTH, D_FF, D_MODEL), BETA * D_FF ** -0.5),
        'mlp_b_down': nrm(ks[23], (DEPTH, D_MODEL), 0.01),
        'ln_mlp_g': 1.0 + nrm(ks[24], (DEPTH, D_MODEL), 0.01),
        'ln_mlp_b': nrm(ks[25], (DEPTH, D_MODEL), 0.01),
    }


def _fwd_reference(x, meta, w_in_ab, s5_lam_re, s5_lam_im, s5_log_dt, s5_b_re, s5_b_im, s5_c_re, s5_c_im,
              s5_d, s5_w_glu, s5_b_glu, w_out_ab, w_in_c, hgrn_gamma, hgrn_norm_g, w_out_c,
              ln_mix_g, ln_mix_b, mlp_w_up, mlp_b_up, mlp_w_down, mlp_b_down, ln_mlp_g, ln_mlp_b):
    bn = x.shape[0]
    h = jnp.concatenate([jnp.broadcast_to(meta.astype(x.dtype)[None], (bn, N_META, D_MODEL)), x], axis=1)
    seq_len = h.shape[1]
    for layer in range(DEPTH):
        if layer % 2 == 0:
            e = layer // 2
            proj = h @ w_in_ab[e]
            u = proj[..., :S5_WIDTH]
            qkv = proj[..., S5_WIDTH:].reshape(bn, seq_len, 3, SB_HEADS, SB_HEAD_DIM)
            a_out = s5_mixer(u, s5_lam_re[e], s5_lam_im[e], s5_log_dt[e], s5_b_re[e], s5_b_im[e],
                             s5_c_re[e], s5_c_im[e], s5_d[e], s5_w_glu[e], s5_b_glu[e])
            b_out = stick_breaking_attention(qkv[:, :, 0], qkv[:, :, 1], qkv[:, :, 2])
            mix = jnp.concatenate([a_out, b_out], axis=-1) @ w_out_ab[e]
        else:
            o_idx = layer // 2
            proj = h @ w_in_c[o_idx]
            q, f_pre, i_in, g = jnp.split(proj, 4, axis=-1)
            lb = hgrn_lower_bound(hgrn_gamma, layer)
            mix = hgrn2_mixer(q, f_pre, i_in, g, lb, hgrn_norm_g[o_idx]) @ w_out_c[o_idx]
        h = layer_norm(ALPHA * h + mix, ln_mix_g[layer], ln_mix_b[layer])
        hid = jnp.square(jax.nn.relu(h @ mlp_w_up[layer] + mlp_b_up[layer]))
        h = layer_norm(ALPHA * h + hid @ mlp_w_down[layer] + mlp_b_down[layer],
                       ln_mlp_g[layer], ln_mlp_b[layer])
    return h[:, N_META:, :]


import jax as _jax
import jax.numpy as _jnp

TWIN_FORMAT = 'train_step'
FWD_PARAMS = ['x', 'meta', 'w_in_ab', 's5_lam_re', 's5_lam_im', 's5_log_dt', 's5_b_re', 's5_b_im', 's5_c_re', 's5_c_im', 's5_d', 's5_w_glu', 's5_b_glu', 'w_out_ab', 'w_in_c', 'hgrn_gamma', 'hgrn_norm_g', 'w_out_c', 'ln_mix_g', 'ln_mix_b', 'mlp_w_up', 'mlp_b_up', 'mlp_w_down', 'mlp_b_down', 'ln_mlp_g', 'ln_mlp_b']
TWIN_WEIGHTS = ['meta', 'w_in_ab', 's5_lam_re', 's5_lam_im', 's5_log_dt', 's5_b_re', 's5_b_im', 's5_c_re', 's5_c_im', 's5_d', 's5_w_glu', 's5_b_glu', 'w_out_ab', 'w_in_c', 'hgrn_gamma', 'hgrn_norm_g', 'w_out_c', 'ln_mix_g', 'ln_mix_b', 'mlp_w_up', 'mlp_b_up', 'mlp_w_down', 'mlp_b_down', 'ln_mlp_g', 'ln_mlp_b']
TWIN_DIFF_INPUT = 'x'
TWIN_INPUTS = ['x', 'meta', 'w_in_ab', 's5_lam_re', 's5_lam_im', 's5_log_dt', 's5_b_re', 's5_b_im', 's5_c_re', 's5_c_im', 's5_d', 's5_w_glu', 's5_b_glu', 'w_out_ab', 'w_in_c', 'hgrn_gamma', 'hgrn_norm_g', 'w_out_c', 'ln_mix_g', 'ln_mix_b', 'mlp_w_up', 'mlp_b_up', 'mlp_w_down', 'mlp_b_down', 'ln_mlp_g', 'ln_mlp_b', 'loss_target', 'm_meta', 'm_w_in_ab', 'm_s5_lam_re', 'm_s5_lam_im', 'm_s5_log_dt', 'm_s5_b_re', 'm_s5_b_im', 'm_s5_c_re', 'm_s5_c_im', 'm_s5_d', 'm_s5_w_glu', 'm_s5_b_glu', 'm_w_out_ab', 'm_w_in_c', 'm_hgrn_gamma', 'm_hgrn_norm_g', 'm_w_out_c', 'm_ln_mix_g', 'm_ln_mix_b', 'm_mlp_w_up', 'm_mlp_b_up', 'm_mlp_w_down', 'm_mlp_b_down', 'm_ln_mlp_g', 'm_ln_mlp_b', 'v_meta', 'v_w_in_ab', 'v_s5_lam_re', 'v_s5_lam_im', 'v_s5_log_dt', 'v_s5_b_re', 'v_s5_b_im', 'v_s5_c_re', 'v_s5_c_im', 'v_s5_d', 'v_s5_w_glu', 'v_s5_b_glu', 'v_w_out_ab', 'v_w_in_c', 'v_hgrn_gamma', 'v_hgrn_norm_g', 'v_w_out_c', 'v_ln_mix_g', 'v_ln_mix_b', 'v_mlp_w_up', 'v_mlp_b_up', 'v_mlp_w_down', 'v_mlp_b_down', 'v_ln_mlp_g', 'v_ln_mlp_b']
TWIN_OUTPUTS = ['loss', 'grad_x', 'grad_meta', 'grad_w_in_ab', 'grad_s5_lam_re', 'grad_s5_lam_im', 'grad_s5_log_dt', 'grad_s5_b_re', 'grad_s5_b_im', 'grad_s5_c_re', 'grad_s5_c_im', 'grad_s5_d', 'grad_s5_w_glu', 'grad_s5_b_glu', 'grad_w_out_ab', 'grad_w_in_c', 'grad_hgrn_gamma', 'grad_hgrn_norm_g', 'grad_w_out_c', 'grad_ln_mix_g', 'grad_ln_mix_b', 'grad_mlp_w_up', 'grad_mlp_b_up', 'grad_mlp_w_down', 'grad_mlp_b_down', 'grad_ln_mlp_g', 'grad_ln_mlp_b', 'delta_meta', 'delta_w_in_ab', 'delta_s5_lam_re', 'delta_s5_lam_im', 'delta_s5_log_dt', 'delta_s5_b_re', 'delta_s5_b_im', 'delta_s5_c_re', 'delta_s5_c_im', 'delta_s5_d', 'delta_s5_w_glu', 'delta_s5_b_glu', 'delta_w_out_ab', 'delta_w_in_c', 'delta_hgrn_gamma', 'delta_hgrn_norm_g', 'delta_w_out_c', 'delta_ln_mix_g', 'delta_ln_mix_b', 'delta_mlp_w_up', 'delta_mlp_b_up', 'delta_mlp_w_down', 'delta_mlp_b_down', 'delta_ln_mlp_g', 'delta_ln_mlp_b', 'new_m_meta', 'new_m_w_in_ab', 'new_m_s5_lam_re', 'new_m_s5_lam_im', 'new_m_s5_log_dt', 'new_m_s5_b_re', 'new_m_s5_b_im', 'new_m_s5_c_re', 'new_m_s5_c_im', 'new_m_s5_d', 'new_m_s5_w_glu', 'new_m_s5_b_glu', 'new_m_w_out_ab', 'new_m_w_in_c', 'new_m_hgrn_gamma', 'new_m_hgrn_norm_g', 'new_m_w_out_c', 'new_m_ln_mix_g', 'new_m_ln_mix_b', 'new_m_mlp_w_up', 'new_m_mlp_b_up', 'new_m_mlp_w_down', 'new_m_mlp_b_down', 'new_m_ln_mlp_g', 'new_m_ln_mlp_b', 'new_v_meta', 'new_v_w_in_ab', 'new_v_s5_lam_re', 'new_v_s5_lam_im', 'new_v_s5_log_dt', 'new_v_s5_b_re', 'new_v_s5_b_im', 'new_v_s5_c_re', 'new_v_s5_c_im', 'new_v_s5_d', 'new_v_s5_w_glu', 'new_v_s5_b_glu', 'new_v_w_out_ab', 'new_v_w_in_c', 'new_v_hgrn_gamma', 'new_v_hgrn_norm_g', 'new_v_w_out_c', 'new_v_ln_mix_g', 'new_v_ln_mix_b', 'new_v_mlp_w_up', 'new_v_mlp_b_up', 'new_v_mlp_w_down', 'new_v_mlp_b_down', 'new_v_ln_mlp_g', 'new_v_ln_mlp_b']
TWIN_LEAF_KINDS = {'loss': 'loss', 'grad_x': 'grad_x', 'grad_meta': 'grad_w', 'grad_w_in_ab': 'grad_w', 'grad_s5_lam_re': 'grad_w', 'grad_s5_lam_im': 'grad_w', 'grad_s5_log_dt': 'grad_w', 'grad_s5_b_re': 'grad_w', 'grad_s5_b_im': 'grad_w', 'grad_s5_c_re': 'grad_w', 'grad_s5_c_im': 'grad_w', 'grad_s5_d': 'grad_w', 'grad_s5_w_glu': 'grad_w', 'grad_s5_b_glu': 'grad_w', 'grad_w_out_ab': 'grad_w', 'grad_w_in_c': 'grad_w', 'grad_hgrn_gamma': 'grad_w', 'grad_hgrn_norm_g': 'grad_w', 'grad_w_out_c': 'grad_w', 'grad_ln_mix_g': 'grad_w', 'grad_ln_mix_b': 'grad_w', 'grad_mlp_w_up': 'grad_w', 'grad_mlp_b_up': 'grad_w', 'grad_mlp_w_down': 'grad_w', 'grad_mlp_b_down': 'grad_w', 'grad_ln_mlp_g': 'grad_w', 'grad_ln_mlp_b': 'grad_w', 'delta_meta': 'delta_w', 'delta_w_in_ab': 'delta_w', 'delta_s5_lam_re': 'delta_w', 'delta_s5_lam_im': 'delta_w', 'delta_s5_log_dt': 'delta_w', 'delta_s5_b_re': 'delta_w', 'delta_s5_b_im': 'delta_w', 'delta_s5_c_re': 'delta_w', 'delta_s5_c_im': 'delta_w', 'delta_s5_d': 'delta_w', 'delta_s5_w_glu': 'delta_w', 'delta_s5_b_glu': 'delta_w', 'delta_w_out_ab': 'delta_w', 'delta_w_in_c': 'delta_w', 'delta_hgrn_gamma': 'delta_w', 'delta_hgrn_norm_g': 'delta_w', 'delta_w_out_c': 'delta_w', 'delta_ln_mix_g': 'delta_w', 'delta_ln_mix_b': 'delta_w', 'delta_mlp_w_up': 'delta_w', 'delta_mlp_b_up': 'delta_w', 'delta_mlp_w_down': 'delta_w', 'delta_mlp_b_down': 'delta_w', 'delta_ln_mlp_g': 'delta_w', 'delta_ln_mlp_b': 'delta_w', 'new_m_meta': 'new_m', 'new_m_w_in_ab': 'new_m', 'new_m_s5_lam_re': 'new_m', 'new_m_s5_lam_im': 'new_m', 'new_m_s5_log_dt': 'new_m', 'new_m_s5_b_re': 'new_m', 'new_m_s5_b_im': 'new_m', 'new_m_s5_c_re': 'new_m', 'new_m_s5_c_im': 'new_m', 'new_m_s5_d': 'new_m', 'new_m_s5_w_glu': 'new_m', 'new_m_s5_b_glu': 'new_m', 'new_m_w_out_ab': 'new_m', 'new_m_w_in_c': 'new_m', 'new_m_hgrn_gamma': 'new_m', 'new_m_hgrn_norm_g': 'new_m', 'new_m_w_out_c': 'new_m', 'new_m_ln_mix_g': 'new_m', 'new_m_ln_mix_b': 'new_m', 'new_m_mlp_w_up': 'new_m', 'new_m_mlp_b_up': 'new_m', 'new_m_mlp_w_down': 'new_m', 'new_m_mlp_b_down': 'new_m', 'new_m_ln_mlp_g': 'new_m', 'new_m_ln_mlp_b': 'new_m', 'new_v_meta': 'new_v', 'new_v_w_in_ab': 'new_v', 'new_v_s5_lam_re': 'new_v', 'new_v_s5_lam_im': 'new_v', 'new_v_s5_log_dt': 'new_v', 'new_v_s5_b_re': 'new_v', 'new_v_s5_b_im': 'new_v', 'new_v_s5_c_re': 'new_v', 'new_v_s5_c_im': 'new_v', 'new_v_s5_d': 'new_v', 'new_v_s5_w_glu': 'new_v', 'new_v_s5_b_glu': 'new_v', 'new_v_w_out_ab': 'new_v', 'new_v_w_in_c': 'new_v', 'new_v_hgrn_gamma': 'new_v', 'new_v_hgrn_norm_g': 'new_v', 'new_v_w_out_c': 'new_v', 'new_v_ln_mix_g': 'new_v', 'new_v_ln_mix_b': 'new_v', 'new_v_mlp_w_up': 'new_v', 'new_v_mlp_b_up': 'new_v', 'new_v_mlp_w_down': 'new_v', 'new_v_mlp_b_down': 'new_v', 'new_v_ln_mlp_g': 'new_v', 'new_v_ln_mlp_b': 'new_v'}


def _forward(args):
    return _fwd_reference(*[args[k] for k in FWD_PARAMS])


def _output_shape():
    out = _jax.eval_shape(lambda: _forward(_fwd_setup_inputs(0)))
    return out.shape, out.dtype

N_MICROBATCH = 1
ADAM_LR = 0.001
ADAM_B1 = 0.9
ADAM_B2 = 0.999
ADAM_EPS = 1e-08
ADAM_WD = 0.01
ADAM_STEP = 10
PER_EXAMPLE_BATCH_AXIS = {'x': 0, 'loss_target': 0}
SHARED_INPUTS = []
_WEIGHT_DTYPES = {'meta': _jnp.float32, 'w_in_ab': _jnp.float32, 's5_lam_re': _jnp.float32, 's5_lam_im': _jnp.float32, 's5_log_dt': _jnp.float32, 's5_b_re': _jnp.float32, 's5_b_im': _jnp.float32, 's5_c_re': _jnp.float32, 's5_c_im': _jnp.float32, 's5_d': _jnp.float32, 's5_w_glu': _jnp.float32, 's5_b_glu': _jnp.float32, 'w_out_ab': _jnp.float32, 'w_in_c': _jnp.float32, 'hgrn_gamma': _jnp.float32, 'hgrn_norm_g': _jnp.float32, 'w_out_c': _jnp.float32, 'ln_mix_g': _jnp.float32, 'ln_mix_b': _jnp.float32, 'mlp_w_up': _jnp.float32, 'mlp_b_up': _jnp.float32, 'mlp_w_down': _jnp.float32, 'mlp_b_down': _jnp.float32, 'ln_mlp_g': _jnp.float32, 'ln_mlp_b': _jnp.float32}
MOMENT_SCALE = {'meta': 1.094488e-03, 'w_in_ab': 3.020733e-02, 's5_lam_re': 1.956207e-03, 's5_lam_im': 1.869794e-03, 's5_log_dt': 9.505003e-01, 's5_b_re': 1.291552e-03, 's5_b_im': 1.231164e-03, 's5_c_re': 1.803319e-03, 's5_c_im': 1.817705e-03, 's5_d': 2.751324e-02, 's5_w_glu': 7.972353e-03, 's5_b_glu': 1.227672e-02, 'w_out_ab': 7.476080e-02, 'w_in_c': 4.158466e-02, 'hgrn_gamma': 2.389228e-02, 'hgrn_norm_g': 3.863964e-02, 'w_out_c': 7.504502e-02, 'ln_mix_g': 3.522404e-01, 'ln_mix_b': 2.666476e-01, 'mlp_w_up': 4.281029e-02, 'mlp_b_up': 4.856362e-02, 'mlp_w_down': 1.615481e-01, 'mlp_b_down': 1.818198e-01, 'ln_mlp_g': 2.267936e+01, 'ln_mlp_b': 5.042114e+00}


def _to_microbatches(a, axis):
    t = _jnp.moveaxis(a, axis, 0)
    t = t.reshape((N_MICROBATCH, t.shape[0] // N_MICROBATCH) + t.shape[1:])
    return _jnp.moveaxis(t, 1, axis + 1)


def setup_inputs(seed: int = 0) -> dict:
    inp = _fwd_setup_inputs(seed)
    key = _jax.random.fold_in(_jax.random.key(seed), 7919)
    shape, _ = _output_shape()
    out = dict(inp)
    out["loss_target"] = _jax.random.normal(_jax.random.fold_in(key, 0), shape, _jnp.float32)
    for i, name in enumerate(TWIN_WEIGHTS):
        w = inp[name].astype(_jnp.float32)
        if MOMENT_SCALE is None:
            s = _jnp.sqrt(_jnp.mean(_jnp.square(w)) + 1e-30)
        else:
            s = MOMENT_SCALE[name]
        km, kv = _jax.random.split(_jax.random.fold_in(key, i + 1))
        out[name] = w
        out["m_" + name] = s * _jax.random.normal(km, w.shape, _jnp.float32)
        out["v_" + name] = (s * s) * _jax.random.uniform(kv, w.shape, _jnp.float32, 0.5, 1.5)
    if N_MICROBATCH > 1:
        for name, axis in PER_EXAMPLE_BATCH_AXIS.items():
            out[name] = _to_microbatches(out[name], axis)
    return {'x': out['x'], 'meta': out['meta'], 'w_in_ab': out['w_in_ab'], 's5_lam_re': out['s5_lam_re'], 's5_lam_im': out['s5_lam_im'], 's5_log_dt': out['s5_log_dt'], 's5_b_re': out['s5_b_re'], 's5_b_im': out['s5_b_im'], 's5_c_re': out['s5_c_re'], 's5_c_im': out['s5_c_im'], 's5_d': out['s5_d'], 's5_w_glu': out['s5_w_glu'], 's5_b_glu': out['s5_b_glu'], 'w_out_ab': out['w_out_ab'], 'w_in_c': out['w_in_c'], 'hgrn_gamma': out['hgrn_gamma'], 'hgrn_norm_g': out['hgrn_norm_g'], 'w_out_c': out['w_out_c'], 'ln_mix_g': out['ln_mix_g'], 'ln_mix_b': out['ln_mix_b'], 'mlp_w_up': out['mlp_w_up'], 'mlp_b_up': out['mlp_b_up'], 'mlp_w_down': out['mlp_w_down'], 'mlp_b_down': out['mlp_b_down'], 'ln_mlp_g': out['ln_mlp_g'], 'ln_mlp_b': out['ln_mlp_b'], 'loss_target': out['loss_target'], 'm_meta': out['m_meta'], 'm_w_in_ab': out['m_w_in_ab'], 'm_s5_lam_re': out['m_s5_lam_re'], 'm_s5_lam_im': out['m_s5_lam_im'], 'm_s5_log_dt': out['m_s5_log_dt'], 'm_s5_b_re': out['m_s5_b_re'], 'm_s5_b_im': out['m_s5_b_im'], 'm_s5_c_re': out['m_s5_c_re'], 'm_s5_c_im': out['m_s5_c_im'], 'm_s5_d': out['m_s5_d'], 'm_s5_w_glu': out['m_s5_w_glu'], 'm_s5_b_glu': out['m_s5_b_glu'], 'm_w_out_ab': out['m_w_out_ab'], 'm_w_in_c': out['m_w_in_c'], 'm_hgrn_gamma': out['m_hgrn_gamma'], 'm_hgrn_norm_g': out['m_hgrn_norm_g'], 'm_w_out_c': out['m_w_out_c'], 'm_ln_mix_g': out['m_ln_mix_g'], 'm_ln_mix_b': out['m_ln_mix_b'], 'm_mlp_w_up': out['m_mlp_w_up'], 'm_mlp_b_up': out['m_mlp_b_up'], 'm_mlp_w_down': out['m_mlp_w_down'], 'm_mlp_b_down': out['m_mlp_b_down'], 'm_ln_mlp_g': out['m_ln_mlp_g'], 'm_ln_mlp_b': out['m_ln_mlp_b'], 'v_meta': out['v_meta'], 'v_w_in_ab': out['v_w_in_ab'], 'v_s5_lam_re': out['v_s5_lam_re'], 'v_s5_lam_im': out['v_s5_lam_im'], 'v_s5_log_dt': out['v_s5_log_dt'], 'v_s5_b_re': out['v_s5_b_re'], 'v_s5_b_im': out['v_s5_b_im'], 'v_s5_c_re': out['v_s5_c_re'], 'v_s5_c_im': out['v_s5_c_im'], 'v_s5_d': out['v_s5_d'], 'v_s5_w_glu': out['v_s5_w_glu'], 'v_s5_b_glu': out['v_s5_b_glu'], 'v_w_out_ab': out['v_w_out_ab'], 'v_w_in_c': out['v_w_in_c'], 'v_hgrn_gamma': out['v_hgrn_gamma'], 'v_hgrn_norm_g': out['v_hgrn_norm_g'], 'v_w_out_c': out['v_w_out_c'], 'v_ln_mix_g': out['v_ln_mix_g'], 'v_ln_mix_b': out['v_ln_mix_b'], 'v_mlp_w_up': out['v_mlp_w_up'], 'v_mlp_b_up': out['v_mlp_b_up'], 'v_mlp_w_down': out['v_mlp_w_down'], 'v_mlp_b_down': out['v_mlp_b_down'], 'v_ln_mlp_g': out['v_ln_mlp_g'], 'v_ln_mlp_b': out['v_ln_mlp_b']}


def _loss(weights, diff, rest, loss_target):
    with _jax.named_scope("forward"):
        args = {**rest, TWIN_DIFF_INPUT: diff, **{k: w.astype(_WEIGHT_DTYPES[k]) for k, w in weights.items()}}
        y = _forward(args)
    with _jax.named_scope("loss_head"):
        err = _jnp.square(y.astype(_jnp.float32) - loss_target)
        return 0.5 * _jnp.sum(_jnp.mean(err, axis=-1)) if err.ndim else 0.5 * err


def _adamw(w, g, m, v):
    m = ADAM_B1 * m + (1.0 - ADAM_B1) * g
    v = ADAM_B2 * v + (1.0 - ADAM_B2) * _jnp.square(g)
    m_hat = m / (1.0 - ADAM_B1 ** ADAM_STEP)
    v_hat = v / (1.0 - ADAM_B2 ** ADAM_STEP)
    delta = -ADAM_LR * (m_hat / (_jnp.sqrt(v_hat) + ADAM_EPS) + ADAM_WD * w)
    return delta, m, v


def reference(x, meta, w_in_ab, s5_lam_re, s5_lam_im, s5_log_dt, s5_b_re, s5_b_im, s5_c_re, s5_c_im, s5_d, s5_w_glu, s5_b_glu, w_out_ab, w_in_c, hgrn_gamma, hgrn_norm_g, w_out_c, ln_mix_g, ln_mix_b, mlp_w_up, mlp_b_up, mlp_w_down, mlp_b_down, ln_mlp_g, ln_mlp_b, loss_target, m_meta, m_w_in_ab, m_s5_lam_re, m_s5_lam_im, m_s5_log_dt, m_s5_b_re, m_s5_b_im, m_s5_c_re, m_s5_c_im, m_s5_d, m_s5_w_glu, m_s5_b_glu, m_w_out_ab, m_w_in_c, m_hgrn_gamma, m_hgrn_norm_g, m_w_out_c, m_ln_mix_g, m_ln_mix_b, m_mlp_w_up, m_mlp_b_up, m_mlp_w_down, m_mlp_b_down, m_ln_mlp_g, m_ln_mlp_b, v_meta, v_w_in_ab, v_s5_lam_re, v_s5_lam_im, v_s5_log_dt, v_s5_b_re, v_s5_b_im, v_s5_c_re, v_s5_c_im, v_s5_d, v_s5_w_glu, v_s5_b_glu, v_w_out_ab, v_w_in_c, v_hgrn_gamma, v_hgrn_norm_g, v_w_out_c, v_ln_mix_g, v_ln_mix_b, v_mlp_w_up, v_mlp_b_up, v_mlp_w_down, v_mlp_b_down, v_ln_mlp_g, v_ln_mlp_b):
    given = dict(x=x, meta=meta, w_in_ab=w_in_ab, s5_lam_re=s5_lam_re, s5_lam_im=s5_lam_im, s5_log_dt=s5_log_dt, s5_b_re=s5_b_re, s5_b_im=s5_b_im, s5_c_re=s5_c_re, s5_c_im=s5_c_im, s5_d=s5_d, s5_w_glu=s5_w_glu, s5_b_glu=s5_b_glu, w_out_ab=w_out_ab, w_in_c=w_in_c, hgrn_gamma=hgrn_gamma, hgrn_norm_g=hgrn_norm_g, w_out_c=w_out_c, ln_mix_g=ln_mix_g, ln_mix_b=ln_mix_b, mlp_w_up=mlp_w_up, mlp_b_up=mlp_b_up, mlp_w_down=mlp_w_down, mlp_b_down=mlp_b_down, ln_mlp_g=ln_mlp_g, ln_mlp_b=ln_mlp_b, loss_target=loss_target, m_meta=m_meta, m_w_in_ab=m_w_in_ab, m_s5_lam_re=m_s5_lam_re, m_s5_lam_im=m_s5_lam_im, m_s5_log_dt=m_s5_log_dt, m_s5_b_re=m_s5_b_re, m_s5_b_im=m_s5_b_im, m_s5_c_re=m_s5_c_re, m_s5_c_im=m_s5_c_im, m_s5_d=m_s5_d, m_s5_w_glu=m_s5_w_glu, m_s5_b_glu=m_s5_b_glu, m_w_out_ab=m_w_out_ab, m_w_in_c=m_w_in_c, m_hgrn_gamma=m_hgrn_gamma, m_hgrn_norm_g=m_hgrn_norm_g, m_w_out_c=m_w_out_c, m_ln_mix_g=m_ln_mix_g, m_ln_mix_b=m_ln_mix_b, m_mlp_w_up=m_mlp_w_up, m_mlp_b_up=m_mlp_b_up, m_mlp_w_down=m_mlp_w_down, m_mlp_b_down=m_mlp_b_down, m_ln_mlp_g=m_ln_mlp_g, m_ln_mlp_b=m_ln_mlp_b, v_meta=v_meta, v_w_in_ab=v_w_in_ab, v_s5_lam_re=v_s5_lam_re, v_s5_lam_im=v_s5_lam_im, v_s5_log_dt=v_s5_log_dt, v_s5_b_re=v_s5_b_re, v_s5_b_im=v_s5_b_im, v_s5_c_re=v_s5_c_re, v_s5_c_im=v_s5_c_im, v_s5_d=v_s5_d, v_s5_w_glu=v_s5_w_glu, v_s5_b_glu=v_s5_b_glu, v_w_out_ab=v_w_out_ab, v_w_in_c=v_w_in_c, v_hgrn_gamma=v_hgrn_gamma, v_hgrn_norm_g=v_hgrn_norm_g, v_w_out_c=v_w_out_c, v_ln_mix_g=v_ln_mix_g, v_ln_mix_b=v_ln_mix_b, v_mlp_w_up=v_mlp_w_up, v_mlp_b_up=v_mlp_b_up, v_mlp_w_down=v_mlp_w_down, v_mlp_b_down=v_mlp_b_down, v_ln_mlp_g=v_ln_mlp_g, v_ln_mlp_b=v_ln_mlp_b)
    weights = {n: given[n] for n in TWIN_WEIGHTS}
    shared = {n: given[n] for n in SHARED_INPUTS}
    per_example = {n: given[n] for n in ['x']}
    grad_fn = _jax.value_and_grad(_loss, argnums=(0, 1))

    def one_microbatch(ex, loss_target):
        ex = dict(ex)
        diff = ex.pop(TWIN_DIFF_INPUT)
        return grad_fn(weights, diff, {**shared, **ex}, loss_target)

    if N_MICROBATCH == 1:
        loss, (grad_w, grad_x) = one_microbatch(per_example, given["loss_target"])
    else:
        def body(carry, xs):
            loss_sum, grad_sum = carry
            l_k, (gw_k, gx_k) = one_microbatch(xs[0], xs[1])
            with _jax.named_scope("update"):
                return (loss_sum + l_k, _jax.tree.map(_jnp.add, grad_sum, gw_k)), gx_k

        init = (_jnp.zeros((), _jnp.float32), _jax.tree.map(_jnp.zeros_like, weights))
        (loss, grad_w), grad_x = _jax.lax.scan(body, init, (per_example, given["loss_target"]))
    with _jax.named_scope("update"):
        delta_w, new_m, new_v = {}, {}, {}
        for n in TWIN_WEIGHTS:
            delta_w[n], new_m[n], new_v[n] = _adamw(weights[n], grad_w[n], given["m_" + n], given["v_" + n])
    return (loss, grad_x, *[grad_w[n] for n in TWIN_WEIGHTS], *[delta_w[n] for n in TWIN_WEIGHTS],
            *[new_m[n] for n in TWIN_WEIGHTS], *[new_v[n] for n in TWIN_WEIGHTS])
```

```python
import functools
import math

import jax
import jax.numpy as jnp
from jax import lax
from jax.experimental import pallas as pl
from jax.experimental.pallas import tpu as pltpu

F32 = jnp.float32
BF16 = jnp.bfloat16

D_MODEL = 1024
N_META = 16
DEPTH = 2
S5_WIDTH = 512
S5_GROUP = 16
S5_GROUPS = 32
S5_STATE = 64
S5_CHUNK = 16
SB_WIDTH = 512
SB_BLOCK = 128
HG_HEADS = 8
HG_DK = 128
HG_CHUNK = 64
D_FF = 4096
ALPHA = (2.0 * DEPTH) ** 0.25
LN_EPS = 1e-5
RMS_EPS = 1e-6
ADAM_LR = 0.001
ADAM_B1 = 0.9
ADAM_B2 = 0.999
ADAM_EPS = 1e-08
ADAM_WD = 0.01
ADAM_STEP = 10
N_CHIPS = 4
N_DEV = 8
LANES = 128
MESH = pl.DeviceIdType.MESH
VMEM_LIMIT = 56 * 1024 * 1024

WEIGHTS = ['meta', 'w_in_ab', 's5_lam_re', 's5_lam_im', 's5_log_dt', 's5_b_re', 's5_b_im', 's5_c_re', 's5_c_im',
           's5_d', 's5_w_glu', 's5_b_glu', 'w_out_ab', 'w_in_c', 'hgrn_gamma', 'hgrn_norm_g', 'w_out_c',
           'ln_mix_g', 'ln_mix_b', 'mlp_w_up', 'mlp_b_up', 'mlp_w_down', 'mlp_b_down', 'ln_mlp_g', 'ln_mlp_b']
BIG = ['w_in_ab', 's5_w_glu', 'w_out_ab', 'w_in_c', 'w_out_c', 'mlp_w_up', 'mlp_w_down']
SHARDED_SMALL = ['meta', 'hgrn_norm_g']
SMALL = [n for n in WEIGHTS if n not in BIG]


def _params(sem=None):
    return pltpu.CompilerParams(dimension_semantics=sem, vmem_limit_bytes=VMEM_LIMIT)


def _pick(n, cands):
    for c in cands:
        if n % c == 0:
            return c
    return n


def _dot(a, b, ca, cb):
    return lax.dot_general(a, b, (((ca,), (cb,)), ((), ())), preferred_element_type=F32)


def _split3(x):
    hi = x.astype(BF16)
    r = x - hi.astype(F32)
    mid = r.astype(BF16)
    lo = (r - mid.astype(F32)).astype(BF16)
    return hi, mid, lo


def _dot_exact_rhs(x, m, cx, cm):
    hi, mid, lo = _split3(x)
    return _dot(hi, m, cx, cm) + (_dot(mid, m, cx, cm) + _dot(lo, m, cx, cm))


def _dot3(a, b, ca, cb):
    ah = a.astype(BF16)
    al = (a - ah.astype(F32)).astype(BF16)
    bh = b.astype(BF16)
    bl = (b - bh.astype(F32)).astype(BF16)
    return _dot(ah, bh, ca, cb) + (_dot(ah, bl, ca, cb) + _dot(al, bh, ca, cb))


def rowwise(name, fn, row_ins, bc_ins, out_widths, sum_widths=(), out_dtypes=None, tm=None):
    t = row_ins[0].shape[0]
    if tm is None:
        wmax = max([a.shape[1] for a in row_ins] + list(out_widths))
        tm = _pick(t, (272, 256, 128, 64, 16, 8)) if wmax > 1024 else _pick(t, (544, 512, 256, 128, 64, 16, 8))
    nr, nb, no, ns = len(row_ins), len(bc_ins), len(out_widths), len(sum_widths)
    out_dtypes = out_dtypes or [F32] * no

    def body(*refs):
        i = pl.program_id(0)
        rows = [r[...] for r in refs[:nr]]
        bcs = [r[...] for r in refs[nr:nr + nb]]
        outs, sums = fn(i, tm, rows, bcs)
        for r, v in zip(refs[nr + nb:nr + nb + no], outs):
            r[...] = v.astype(r.dtype)
        if ns:
            srefs = refs[nr + nb + no:]

            @pl.when(i == 0)
            def _():
                for r in srefs:
                    r[...] = jnp.zeros(r.shape, r.dtype)

            for r, v in zip(srefs, sums):
                r[...] += v

    in_specs = [pl.BlockSpec((tm, a.shape[1]), lambda i: (i, 0)) for a in row_ins]
    in_specs += [pl.BlockSpec(a.shape, lambda i: (0, 0)) for a in bc_ins]
    out_specs = [pl.BlockSpec((tm, w), lambda i: (i, 0)) for w in out_widths]
    out_specs += [pl.BlockSpec((1, w), lambda i: (0, 0)) for w in sum_widths]
    out_shape = [jax.ShapeDtypeStruct((t, w), dt) for w, dt in zip(out_widths, out_dtypes)]
    out_shape += [jax.ShapeDtypeStruct((1, w), F32) for w in sum_widths]
    res = pl.pallas_call(body, name=name, grid=(t // tm,), in_specs=in_specs, out_specs=out_specs,
                         out_shape=out_shape, compiler_params=_params(("arbitrary",)))(*row_ins, *bc_ins)
    return res


def _colsum(v):
    return jnp.sum(v, axis=0, keepdims=True)


def matmul(name, a, b, mode, *, bias=None, act=None, add=None, add_scale=1.0, out_dtype=F32,
           out_slots=0, tm=None, tn=None, tk=None):
    slotted = b.ndim == 3
    if mode == 'nn':
        m, k = a.shape
        n = b.shape[-1] * (b.shape[0] if slotted else 1)
    elif mode == 'nt':
        m, k = a.shape
        n = b.shape[-2]
    else:
        k, m = a.shape
        n = b.shape[-1]
    ns = b.shape[-1] if slotted else None
    if mode == 'tn':
        tm = tm or _pick(m, (512, 256, 128))
        tn = tn or _pick(n if not out_slots else n // out_slots, (1024, 512, 256, 128))
        tk = tk or _pick(k, (1088, 1024, 768, 512, 384, 256, 128))
    else:
        tm = tm or _pick(m, (1088, 1024, 768, 512, 384, 256, 128))
        tn = tn or _pick(n if not (slotted and mode == 'nn') else ns, (512, 256, 128))
        tk = tk or _pick(k if not (slotted and mode == 'nt') else ns, (1024, 512, 256, 128))
    gm, gn, gk = m // tm, n // tn, k // tk

    if mode == 'nn':
        a_spec = pl.BlockSpec((tm, tk), lambda i, j, kk: (i, kk))
        if slotted:
            per = ns // tn
            b_spec = pl.BlockSpec((None, tk, tn), lambda i, j, kk: (j // per, kk, j % per))
        else:
            b_spec = pl.BlockSpec((tk, tn), lambda i, j, kk: (kk, j))
        ca, cb = 1, 0
    elif mode == 'nt':
        a_spec = pl.BlockSpec((tm, tk), lambda i, j, kk: (i, kk))
        if slotted:
            per = ns // tk
            b_spec = pl.BlockSpec((None, tn, tk), lambda i, j, kk: (kk // per, j, kk % per))
        else:
            b_spec = pl.BlockSpec((tn, tk), lambda i, j, kk: (j, kk))
        ca, cb = 1, 1
    else:
        a_spec = pl.BlockSpec((tk, tm), lambda i, j, kk: (kk, i))
        b_spec = pl.BlockSpec((tk, tn), lambda i, j, kk: (kk, j))
        ca, cb = 0, 0
    in_specs = [a_spec, b_spec]
    args = [a, b]
    if bias is not None:
        in_specs.append(pl.BlockSpec((1, tn), lambda i, j, kk: (0, j)))
        args.append(bias)
    if add is not None:
        in_specs.append(pl.BlockSpec((tm, tn), lambda i, j, kk: (i, j)))
        args.append(add)
    if out_slots:
        per_o = (n // out_slots) // tn
        out_spec = pl.BlockSpec((None, tm, tn), lambda i, j, kk: (j // per_o, i, j % per_o))
        out_shape = jax.ShapeDtypeStruct((out_slots, m, n // out_slots), out_dtype)
    else:
        out_spec = pl.BlockSpec((tm, tn), lambda i, j, kk: (i, j))
        out_shape = jax.ShapeDtypeStruct((m, n), out_dtype)

    def body(*refs):
        a_ref, b_ref = refs[0], refs[1]
        pos = 2
        bias_ref = add_ref = None
        if bias is not None:
            bias_ref = refs[pos]
            pos += 1
        if add is not None:
            add_ref = refs[pos]
            pos += 1
        o_ref = refs[pos]
        acc_ref = refs[pos + 1] if gk > 1 else None
        p = _dot(a_ref[...].astype(BF16), b_ref[...].astype(BF16), ca, cb)

        def finish(r):
            if bias_ref is not None:
                r = r + bias_ref[...]
            if act == 'relu2':
                r = jnp.square(jnp.maximum(r, 0.0))
            if add_ref is not None:
                r = r + add_scale * add_ref[...]
            o_ref[...] = r.astype(o_ref.dtype)

        if gk == 1:
            finish(p)
        else:
            kk = pl.program_id(2)

            @pl.when(kk == 0)
            def _():
                acc_ref[...] = p

            @pl.when(kk > 0)
            def _():
                acc_ref[...] += p

            @pl.when(kk == gk - 1)
            def _():
                finish(acc_ref[...])

    scratch = [pltpu.VMEM((tm, tn), F32)] if gk > 1 else []
    return pl.pallas_call(body, name=name, grid=(gm, gn, gk), in_specs=in_specs, out_specs=out_spec,
                          out_shape=out_shape, scratch_shapes=scratch,
                          compiler_params=_params(("parallel", "parallel", "arbitrary")))(*args)


def ln_fwd(name, h, mix, g, b):
    def fn(i, tm, rows, bcs):
        pre = ALPHA * rows[0] + rows[1]
        mu = jnp.mean(pre, axis=1, keepdims=True)
        xc = pre - mu
        var = jnp.mean(xc * xc, axis=1, keepdims=True)
        rstd = lax.rsqrt(var + LN_EPS)
        xhat = xc * rstd
        return (xhat * bcs[0] + bcs[1], xhat, rstd), ()
    return rowwise(name, fn, [h, mix], [g, b], [D_MODEL, D_MODEL, 1])


def ln_bwd(name, dy, xhat, rstd, g):
    def fn(i, tm, rows, bcs):
        dy_, xh, rs = rows
        dyg = dy_ * bcs[0]
        m1 = jnp.mean(dyg, axis=1, keepdims=True)
        m2 = jnp.mean(dyg * xh, axis=1, keepdims=True)
        dpre = rs * (dyg - m1 - xh * m2)
        return (dpre,), (_colsum(dy_ * xh), _colsum(dy_), _colsum(dpre))
    return rowwise(name, fn, [dy, xhat, rstd], [g], [D_MODEL], [D_MODEL] * 3)


def relu2_bwd(name, dhid, hid):
    def fn(i, tm, rows, bcs):
        dpre = rows[0] * (2.0 * jnp.sqrt(rows[1]))
        return (dpre,), (_colsum(dpre),)
    return rowwise(name, fn, [dhid, hid], [], [D_FF], [D_FF])


_GELU_C = math.sqrt(2.0 / math.pi)


def gelu_fwd(name, x):
    def fn(i, tm, rows, bcs):
        v = rows[0]
        u = _GELU_C * (v + 0.044715 * (v * v * v))
        return (0.5 * v * (1.0 + jnp.tanh(u)),), ()
    return rowwise(name, fn, [x], [], [x.shape[1]])[0]


def gelu_bwd(name, dy, x):
    def fn(i, tm, rows, bcs):
        v = rows[1]
        th = jnp.tanh(_GELU_C * (v + 0.044715 * (v * v * v)))
        dg = 0.5 * (1.0 + th) + 0.5 * v * (1.0 - th * th) * (_GELU_C * (1.0 + 3.0 * 0.044715 * v * v))
        return (rows[0] * dg,), ()
    return rowwise(name, fn, [dy, x], [], [x.shape[1]])[0]


def glu_gate(name, y, gp):
    def fn(i, tm, rows, bcs):
        return (rows[0] * jax.nn.sigmoid(rows[1]),), ()
    return rowwise(name, fn, [y, gp], [], [y.shape[1]])[0]


def glu_gate_bwd(name, da, y, gp):
    def fn(i, tm, rows, bcs):
        s = jax.nn.sigmoid(rows[2])
        dgp = rows[0] * rows[1] * s * (1.0 - s)
        return (dgp, rows[0] * s), (_colsum(dgp),)
    w = y.shape[1]
    return rowwise(name, fn, [da, y, gp], [], [w, w], [w])


def loss_head(name, h, tgt, lp, seq):
    def fn(i, tm, rows, bcs):
        pos = (i * tm + lax.broadcasted_iota(jnp.int32, (tm, 1), 0)) % lp
        valid = jnp.logical_and(pos >= N_META, pos < N_META + seq)
        err = jnp.where(valid, rows[0] - rows[1], 0.0)
        part = 0.5 * jnp.sum(jnp.mean(err * err, axis=1, keepdims=True), axis=0, keepdims=True)
        return (err * (1.0 / D_MODEL),), (jnp.broadcast_to(part, (1, LANES)),)
    return rowwise(name, fn, [h, tgt], [], [D_MODEL], [LANES])


def adamw(name, w, gs, m, v):
    ng = len(gs)

    def fn(i, tm, rows, bcs):
        w_ = rows[0]
        g = rows[1] if ng == 1 else rows[1] + rows[2]
        m_, v_ = rows[1 + ng], rows[2 + ng]
        m_ = ADAM_B1 * m_ + (1.0 - ADAM_B1) * g
        v_ = ADAM_B2 * v_ + (1.0 - ADAM_B2) * jnp.square(g)
        m_hat = m_ / (1.0 - ADAM_B1 ** ADAM_STEP)
        v_hat = v_ / (1.0 - ADAM_B2 ** ADAM_STEP)
        delta = -ADAM_LR * (m_hat / (jnp.sqrt(v_hat) + ADAM_EPS) + ADAM_WD * w_)
        return (g, delta, m_, v_), ()
    wd = w.shape[1]
    return rowwise(name, fn, [w] + list(gs) + [m, v], [], [wd] * 4)


def sum_rows(name, parts):
    def fn(i, tm, rows, bcs):
        s = rows[0]
        for r in rows[1:]:
            s = s + r
        return (s,), ()
    return rowwise(name, fn, list(parts), [], [parts[0].shape[1]])[0]


def cast_bf16(name, w):
    def fn(i, tm, rows, bcs):
        return (rows[0],), ()
    return rowwise(name, fn, [w], [], [w.shape[1]], out_dtypes=[BF16])[0]


def s5_matrices(lam_re, lam_im, log_dt, b_re, b_im, c_re, c_im, d):
    c = S5_CHUNK
    dt = jnp.exp(log_dt)[:, None]
    tau = jnp.arange(c + 1, dtype=F32)[:, None, None]
    mag = jnp.exp(tau * (lam_re * dt)[None])
    ang = tau * (lam_im * dt)[None]
    pw_re, pw_im = mag * jnp.cos(ang), mag * jnp.sin(ang)
    nr, ni = pw_re[1] - 1.0, pw_im[1]
    den = lam_re * lam_re + lam_im * lam_im
    cf_re = (nr * lam_re + ni * lam_im) / den
    cf_im = (ni * lam_re - nr * lam_im) / den
    bb_re = cf_re[:, :, None] * b_re - cf_im[:, :, None] * b_im
    bb_im = cf_re[:, :, None] * b_im + cf_im[:, :, None] * b_re
    cp_re = c_re[None] * pw_re[:, :, None, :] - c_im[None] * pw_im[:, :, None, :]
    cp_im = c_re[None] * pw_im[:, :, None, :] + c_im[None] * pw_re[:, :, None, :]
    hp = lax.Precision.HIGHEST
    kern = (jnp.einsum('tghp,gpk->tghk', cp_re[:c], bb_re, precision=hp)
            - jnp.einsum('tghp,gpk->tghk', cp_im[:c], bb_im, precision=hp))
    ii = jnp.arange(c)
    sel = (ii[None, :, None] - ii[None, None, :] == ii[:, None, None]).astype(F32)
    a1 = jnp.einsum('tghk,tij->gjkih', kern, sel, precision=hp)
    a1 = a1 + jnp.einsum('ij,kh,gh->gjkih', jnp.eye(c, dtype=F32), jnp.eye(S5_GROUP, dtype=F32), d)
    a1 = a1.reshape(S5_GROUPS, c * S5_GROUP, c * S5_GROUP)
    a2 = jnp.concatenate([cp_re[1:], -cp_im[1:]], axis=-1)
    a2 = a2.transpose(1, 3, 0, 2).reshape(S5_GROUPS, 2 * S5_STATE, c * S5_GROUP)
    rv_re, rv_im = pw_re[:c][::-1], pw_im[:c][::-1]
    x_re = rv_re[:, :, :, None] * bb_re[None] - rv_im[:, :, :, None] * bb_im[None]
    x_im = rv_re[:, :, :, None] * bb_im[None] + rv_im[:, :, :, None] * bb_re[None]
    a3 = jnp.concatenate([x_re, x_im], axis=2)
    a3 = a3.transpose(1, 0, 3, 2).reshape(S5_GROUPS, c * S5_GROUP, 2 * S5_STATE)
    are = jnp.concatenate([pw_re[c], pw_re[c]], axis=-1)[:, None, :]
    aim = jnp.concatenate([-pw_im[c], pw_im[c]], axis=-1)[:, None, :]
    return a1, a2, a3, are, aim


def s5_fwd(ug, a1, a2, a3, are, aim, nseq):
    g, nc, cw = ug.shape
    per = nc // nseq
    sw = 2 * S5_STATE

    def body(u_ref, a1_ref, a2_ref, a3_ref, are_ref, aim_ref, y_ref, s_ref, x_ref):
        u = u_ref[...]
        x_ref[...] = _dot3(u, a3_ref[...], 1, 0)
        ar, ai = are_ref[...], aim_ref[...]

        def step(cix, carry):
            new = []
            for b in range(nseq):
                s = carry[b]
                s_ref[pl.ds(b * per + cix, 1), :] = s
                new.append(ar * s + ai * pltpu.roll(s, S5_STATE, 1) + x_ref[pl.ds(b * per + cix, 1), :])
            return tuple(new)

        lax.fori_loop(0, per, step, tuple(jnp.zeros((1, sw), F32) for _ in range(nseq)))
        y_ref[...] = _dot3(u, a1_ref[...], 1, 0) + _dot3(s_ref[...], a2_ref[...], 1, 0)

    def spec(shape):
        return pl.BlockSpec((None,) + shape, lambda i: (i, 0, 0))
    return pl.pallas_call(
        body, name="s5_fwd", grid=(g,),
        in_specs=[spec((nc, cw)), spec((cw, cw)), spec((sw, cw)), spec((cw, sw)), spec((1, sw)), spec((1, sw))],
        out_specs=[spec((nc, cw)), spec((nc, sw))],
        out_shape=[jax.ShapeDtypeStruct((g, nc, cw), F32), jax.ShapeDtypeStruct((g, nc, sw), F32)],
        scratch_shapes=[pltpu.VMEM((nc, sw), F32)],
        compiler_params=_params(("parallel",)))(ug, a1, a2, a3, are, aim)


def s5_bwd(dyg, ug, sst, a1, a2, a3, are, aim, nseq):
    g, nc, cw = ug.shape
    per = nc // nseq
    sw = 2 * S5_STATE

    def body(dy_ref, u_ref, s_ref, a1_ref, a2_ref, a3_ref, are_ref, aim_ref,
             du_ref, da1_ref, da2_ref, da3_ref, dare_ref, daim_ref, gd_ref, dx_ref):
        dy = dy_ref[...]
        u = u_ref[...]
        gd_ref[...] = _dot3(dy, a2_ref[...], 1, 1)
        da2_ref[...] = _dot3(s_ref[...], dy, 0, 0)
        ar, ai = are_ref[...], aim_ref[...]

        def step(t, carry):
            cix = per - 1 - t
            new = []
            for b in range(nseq):
                gn, dar, dai = carry[3 * b:3 * b + 3]
                row = pl.ds(b * per + cix, 1)
                dx_ref[row, :] = gn
                s = s_ref[row, :]
                dar = dar + gn * s
                dai = dai + gn * pltpu.roll(s, S5_STATE, 1)
                gcur = gd_ref[row, :] + ar * gn + pltpu.roll(ai * gn, S5_STATE, 1)
                new += [gcur, dar, dai]
            return tuple(new)

        z = jnp.zeros((1, sw), F32)
        fin = lax.fori_loop(0, per, step, tuple(z for _ in range(3 * nseq)))
        dar, dai = fin[1], fin[2]
        for b in range(1, nseq):
            dar = dar + fin[3 * b + 1]
            dai = dai + fin[3 * b + 2]
        dare_ref[...] = dar
        daim_ref[...] = dai
        dx = dx_ref[...]
        du_ref[...] = _dot3(dy, a1_ref[...], 1, 1) + _dot3(dx, a3_ref[...], 1, 1)
        da1_ref[...] = _dot3(u, dy, 0, 0)
        da3_ref[...] = _dot3(u, dx, 0, 0)

    def spec(shape):
        return pl.BlockSpec((None,) + shape, lambda i: (i, 0, 0))
    sds = jax.ShapeDtypeStruct
    return pl.pallas_call(
        body, name="s5_bwd", grid=(g,),
        in_specs=[spec((nc, cw)), spec((nc, cw)), spec((nc, sw)), spec((cw, cw)), spec((sw, cw)), spec((cw, sw)),
                  spec((1, sw)), spec((1, sw))],
        out_specs=[spec((nc, cw)), spec((cw, cw)), spec((sw, cw)), spec((cw, sw)), spec((1, sw)), spec((1, sw))],
        out_shape=[sds((g, nc, cw), F32), sds((g, cw, cw), F32), sds((g, sw, cw), F32), sds((g, cw, sw), F32),
                   sds((g, 1, sw), F32), sds((g, 1, sw), F32)],
        scratch_shapes=[pltpu.VMEM((nc, sw), F32), pltpu.VMEM((nc, sw), F32)],
        compiler_params=_params(("parallel",)))(dyg, ug, sst, a1, a2, a3, are, aim)


def _to_groups(u):
    t = u.shape[0]
    nc = t // S5_CHUNK
    return u.reshape(nc, S5_CHUNK, S5_GROUPS, S5_GROUP).transpose(2, 0, 1, 3).reshape(S5_GROUPS, nc, -1)


def _from_groups(yg):
    g, nc, _ = yg.shape
    return yg.reshape(g, nc, S5_CHUNK, S5_GROUP).transpose(1, 2, 0, 3).reshape(nc * S5_CHUNK, g * S5_GROUP)


def _sb_masks():
    row = lax.broadcasted_iota(jnp.int32, (SB_BLOCK, SB_BLOCK), 0)
    col = lax.broadcasted_iota(jnp.int32, (SB_BLOCK, SB_BLOCK), 1)
    lane = lax.broadcasted_iota(jnp.int32, (1, LANES), 1)
    return row, col, [lane < 64, lane >= 64]


def _sb_weights(z, vis, later_of, after):
    sp = jnp.maximum(z, 0.0) + jnp.log1p(jnp.exp(-jnp.abs(z)))
    lk = jnp.where(vis, -sp, 0.0)
    rs = jnp.sum(lk, axis=1, keepdims=True)
    later = _dot_exact_rhs(lk, after, 1, 0) + later_of(rs)
    lb = z - sp
    w = jnp.where(vis, jnp.exp(lb + later), 0.0)
    return w, rs, lb


def attn_fwd(proj, nseq, lp):
    nqb = lp // SB_BLOCK
    t = proj.shape[0]

    def body(q_ref, k_ref, v_ref, o_ref, tot_ref):
        row, col, hmask = _sb_masks()
        after = (row > col).astype(BF16)
        tri = col < row

        def qloop(qi, _):
            q0 = pl.multiple_of(qi * SB_BLOCK, SB_BLOCK)
            q = q_ref[pl.ds(q0, SB_BLOCK), :] * 0.125
            qm = [jnp.where(hm, q, 0.0).astype(BF16) for hm in hmask]

            def kloop(s, carry):
                acc, lat0, lat1 = carry
                lats = [lat0, lat1]
                kj = qi - s
                k0 = pl.multiple_of(kj * SB_BLOCK, SB_BLOCK)
                k = k_ref[pl.ds(k0, SB_BLOCK), :].astype(BF16)
                v = v_ref[pl.ds(k0, SB_BLOCK), :]
                vis = jnp.logical_or(kj < qi, tri)
                for hd in range(2):
                    z = _dot(qm[hd], k, 1, 1)
                    w, rs, _ = _sb_weights(z, vis, functools.partial(lambda rs, lat: lat, lat=lats[hd]), after)
                    acc = acc + _dot(w.astype(BF16), jnp.where(hmask[hd], v, 0.0).astype(BF16), 1, 0)
                    lats[hd] = lats[hd] + rs
                return acc, lats[0], lats[1]

            z1 = jnp.zeros((SB_BLOCK, 1), F32)
            acc, lat0, lat1 = lax.fori_loop(0, qi + 1, kloop, (jnp.zeros((SB_BLOCK, LANES), F32), z1, z1))
            o_ref[pl.ds(q0, SB_BLOCK), :] = acc
            tot_ref[pl.ds(q0, SB_BLOCK), :] = jnp.where(hmask[0], lat0, lat1)
            return 0

        lax.fori_loop(0, nqb, qloop, 0)

    nhp = SB_WIDTH // LANES
    qoff, koff, voff = S5_WIDTH // LANES, (S5_WIDTH + SB_WIDTH) // LANES, (S5_WIDTH + 2 * SB_WIDTH) // LANES
    return pl.pallas_call(
        body, name="attn_fwd", grid=(nseq, nhp),
        in_specs=[pl.BlockSpec((lp, LANES), lambda b, h: (b, qoff + h)),
                  pl.BlockSpec((lp, LANES), lambda b, h: (b, koff + h)),
                  pl.BlockSpec((lp, LANES), lambda b, h: (b, voff + h))],
        out_specs=[pl.BlockSpec((lp, LANES), lambda b, h: (b, h))] * 2,
        out_shape=[jax.ShapeDtypeStruct((t, SB_WIDTH), F32)] * 2,
        compiler_params=_params(("parallel", "parallel")))(proj, proj, proj)


def attn_bwd(proj, tot, do, nseq, lp):
    nqb = lp // SB_BLOCK
    t = proj.shape[0]

    def body(q_ref, k_ref, v_ref, tot_ref, do_ref, dq_ref, dk_ref, dv_ref):
        row, col, hmask = _sb_masks()
        after = (row > col).astype(BF16)
        before = (row < col).astype(BF16)
        tri = col < row
        dk_ref[...] = jnp.zeros(dk_ref.shape, F32)
        dv_ref[...] = jnp.zeros(dv_ref.shape, F32)

        def qloop(qi, _):
            q0 = pl.multiple_of(qi * SB_BLOCK, SB_BLOCK)
            q = q_ref[pl.ds(q0, SB_BLOCK), :] * 0.125
            do_ = do_ref[pl.ds(q0, SB_BLOCK), :]
            tot_ = tot_ref[pl.ds(q0, SB_BLOCK), :]
            qm = [jnp.where(hm, q, 0.0).astype(BF16) for hm in hmask]
            dom = [jnp.where(hm, do_, 0.0).astype(BF16) for hm in hmask]
            tot = [jnp.max(jnp.where(hm, tot_, -jnp.inf), axis=1, keepdims=True) for hm in hmask]

            def kloop(kj, carry):
                dq = carry[0]
                pre = [carry[1], carry[2]]
                gpre = [carry[3], carry[4]]
                k0 = pl.multiple_of(kj * SB_BLOCK, SB_BLOCK)
                kf = k_ref[pl.ds(k0, SB_BLOCK), :]
                k = kf.astype(BF16)
                v = v_ref[pl.ds(k0, SB_BLOCK), :].astype(BF16)
                vis = jnp.logical_or(kj < qi, tri)
                dk_acc = jnp.zeros((SB_BLOCK, LANES), F32)
                dv_acc = jnp.zeros((SB_BLOCK, LANES), F32)
                for hd in range(2):
                    z = _dot(qm[hd], k, 1, 1)
                    later_of = functools.partial(lambda rs, t_, p_: t_ - p_ - rs, t_=tot[hd], p_=pre[hd])
                    w, rs, lb = _sb_weights(z, vis, later_of, after)
                    dw = _dot(dom[hd], v, 1, 1)
                    g = dw * w
                    dlk = _dot_exact_rhs(g, before, 1, 0) + gpre[hd]
                    sig = jnp.exp(lb)
                    dz = jnp.where(vis, g * (1.0 - sig) - dlk * sig, 0.0).astype(BF16)
                    dq = dq + _dot(dz, jnp.where(hmask[hd], kf, 0.0).astype(BF16), 1, 0)
                    dk_acc = dk_acc + _dot(dz, qm[hd], 0, 0)
                    dv_acc = dv_acc + _dot(w.astype(BF16), dom[hd], 0, 0)
                    pre[hd] = pre[hd] + rs
                    gpre[hd] = gpre[hd] + jnp.sum(g, axis=1, keepdims=True)
                dk_ref[pl.ds(k0, SB_BLOCK), :] += dk_acc
                dv_ref[pl.ds(k0, SB_BLOCK), :] += dv_acc
                return dq, pre[0], pre[1], gpre[0], gpre[1]

            z1 = jnp.zeros((SB_BLOCK, 1), F32)
            res = lax.fori_loop(0, qi + 1, kloop, (jnp.zeros((SB_BLOCK, LANES), F32), z1, z1, z1, z1))
            dq_ref[pl.ds(q0, SB_BLOCK), :] = res[0] * 0.125
            return 0

        lax.fori_loop(0, nqb, qloop, 0)

    nhp = SB_WIDTH // LANES
    qoff, koff, voff = S5_WIDTH // LANES, (S5_WIDTH + SB_WIDTH) // LANES, (S5_WIDTH + 2 * SB_WIDTH) // LANES
    blk = (lp, LANES)
    return pl.pallas_call(
        body, name="attn_bwd", grid=(nseq, nhp),
        in_specs=[pl.BlockSpec(blk, lambda b, h: (b, qoff + h)),
                  pl.BlockSpec(blk, lambda b, h: (b, koff + h)),
                  pl.BlockSpec(blk, lambda b, h: (b, voff + h)),
                  pl.BlockSpec(blk, lambda b, h: (b, h)),
                  pl.BlockSpec(blk, lambda b, h: (b, h))],
        out_specs=[pl.BlockSpec(blk, lambda b, h: (b, h))] * 3,
        out_shape=[jax.ShapeDtypeStruct((t, SB_WIDTH), F32)] * 3,
        compiler_params=_params(("parallel", "parallel")))(proj, proj, proj, tot, do)


def _hg_gates(fc, lb):
    sg = jax.nn.sigmoid(fc)
    f = lb + (1.0 - lb) * sg
    return sg, f


def _hg_decay(f, incl):
    bcum = _dot_exact_lhs(incl, jnp.log(f))
    return bcum, bcum[HG_CHUNK - 1:HG_CHUNK, :]


def _dot_exact_lhs(m, x):
    hi, mid, lo = _split3(x)
    return _dot(m, hi, 1, 0) + (_dot(m, mid, 1, 0) + _dot(m, lo, 1, 0))


def hgrn_fwd(proj, lb, ng, nseq, lp):
    nch = lp // HG_CHUNK
    t = proj.shape[0]
    c = HG_CHUNK

    def body(q_ref, f_ref, v_ref, g_ref, lb_ref, ng_ref, o_ref, og_ref, st_ref):
        row = lax.broadcasted_iota(jnp.int32, (c, c), 0)
        col = lax.broadcasted_iota(jnp.int32, (c, c), 1)
        causal = col <= row
        incl = causal.astype(BF16)
        lbv, ngv = lb_ref[...], ng_ref[...]

        def step(ci, st):
            r0 = pl.multiple_of(ci * c, c)
            rows = pl.ds(r0, c)
            st_ref[ci] = st
            _, f = _hg_gates(f_ref[rows, :], lbv)
            bcum, blast = _hg_decay(f, incl)
            kk = 1.0 - f
            vc = v_ref[rows, :].astype(BF16)
            qd = (q_ref[rows, :] * jnp.exp(bcum)).astype(BF16)
            kd = (kk * jnp.exp(-bcum)).astype(BF16)
            a = jnp.where(causal, _dot(qd, kd, 1, 1), 0.0)
            o = _dot(a.astype(BF16), vc, 1, 0) + _dot(qd, st.astype(BF16), 1, 1)
            kend = (kk * jnp.exp(blast - bcum)).astype(BF16)
            st = st * jnp.exp(blast) + _dot(vc, kend, 0, 0)
            o_ref[rows, :] = o
            r = lax.rsqrt(jnp.mean(o * o, axis=1, keepdims=True) + RMS_EPS)
            gc = g_ref[rows, :]
            og_ref[rows, :] = o * r * ngv * (gc * jax.nn.sigmoid(gc))
            return st

        lax.fori_loop(0, nch, step, jnp.zeros((HG_DK, HG_DK), F32))

    blk = (lp, LANES)
    h = HG_HEADS
    return pl.pallas_call(
        body, name="hgrn_fwd", grid=(nseq, h),
        in_specs=[pl.BlockSpec(blk, lambda b, hh: (b, hh)),
                  pl.BlockSpec(blk, lambda b, hh: (b, h + hh)),
                  pl.BlockSpec(blk, lambda b, hh: (b, 2 * h + hh)),
                  pl.BlockSpec(blk, lambda b, hh: (b, 3 * h + hh)),
                  pl.BlockSpec((1, LANES), lambda b, hh: (0, hh)),
                  pl.BlockSpec((1, LANES), lambda b, hh: (0, hh))],
        out_specs=[pl.BlockSpec(blk, lambda b, hh: (b, hh)),
                   pl.BlockSpec(blk, lambda b, hh: (b, hh)),
                   pl.BlockSpec((None, None, nch, HG_DK, HG_DK), lambda b, hh: (b, hh, 0, 0, 0))],
        out_shape=[jax.ShapeDtypeStruct((t, D_MODEL), F32), jax.ShapeDtypeStruct((t, D_MODEL), F32),
                   jax.ShapeDtypeStruct((nseq, h, nch, HG_DK, HG_DK), F32)],
        compiler_params=_params(("parallel", "parallel")))(proj, proj, proj, proj, lb, ng)


def hgrn_bwd(proj, o, dog, states, lb, ng, nseq, lp):
    nch = lp // HG_CHUNK
    t = proj.shape[0]
    c = HG_CHUNK

    def body(q_ref, f_ref, v_ref, g_ref, o_ref, dog_ref, st_ref, lb_ref, ng_ref,
             dq_ref, df_ref, dv_ref, dg_ref, dlb_ref, dng_ref):
        row = lax.broadcasted_iota(jnp.int32, (c, c), 0)
        col = lax.broadcasted_iota(jnp.int32, (c, c), 1)
        causal = col <= row
        incl = causal.astype(BF16)
        from_here = (col >= row).astype(BF16)
        last = lax.broadcasted_iota(jnp.int32, (c, 1), 0) == c - 1
        lbv, ngv = lb_ref[...], ng_ref[...]

        def step(s, carry):
            dst, dlb, dng = carry
            ci = nch - 1 - s
            r0 = pl.multiple_of(ci * c, c)
            rows = pl.ds(r0, c)
            oc, gc, dogc = o_ref[rows, :], g_ref[rows, :], dog_ref[rows, :]
            r = lax.rsqrt(jnp.mean(oc * oc, axis=1, keepdims=True) + RMS_EPS)
            sgg = jax.nn.sigmoid(gc)
            silu = gc * sgg
            dg_ref[rows, :] = dogc * (oc * r * ngv) * (sgg * (1.0 + gc * (1.0 - sgg)))
            don = dogc * silu
            dng = dng + _colsum(don * oc * r)
            tt = don * ngv
            do = r * (tt - oc * (r * r) * jnp.mean(tt * oc, axis=1, keepdims=True))
            st = st_ref[ci]
            fc = f_ref[rows, :]
            sg, f = _hg_gates(fc, lbv)
            bcum, blast = _hg_decay(f, incl)
            kk = 1.0 - f
            qc, vf = q_ref[rows, :], v_ref[rows, :]
            pdec = jnp.exp(bcum)
            ndec = jnp.exp(-bcum)
            edec = jnp.exp(blast - bcum)
            eb = jnp.exp(blast)
            qd_f, kd_f, kend_f = qc * pdec, kk * ndec, kk * edec
            qd, kd, kend = qd_f.astype(BF16), kd_f.astype(BF16), kend_f.astype(BF16)
            vc, dob, stb, dstb = vf.astype(BF16), do.astype(BF16), st.astype(BF16), dst.astype(BF16)
            a = jnp.where(causal, _dot(qd, kd, 1, 1), 0.0).astype(BF16)
            da = jnp.where(causal, _dot(dob, vc, 1, 1), 0.0).astype(BF16)
            dv_ref[rows, :] = _dot(a, dob, 0, 0) + _dot(kend, dstb, 1, 1)
            dqd = _dot(da, kd, 1, 0) + _dot(dob, stb, 1, 0)
            dkd = _dot(da, qd, 0, 0)
            dkend = _dot(vc, dstb, 1, 0)
            dq_ref[rows, :] = dqd * pdec
            dblast = _colsum(dkend * kend_f) + eb * _colsum(dst * st)
            dbcum = dqd * qd_f - dkd * kd_f - dkend * kend_f
            dbcum = dbcum + jnp.where(last, dblast, 0.0)
            dlf = _dot_exact_lhs(from_here, dbcum)
            df = dlf / f - (dkd * ndec + dkend * edec)
            df_ref[rows, :] = df * (1.0 - lbv) * sg * (1.0 - sg)
            dlb = dlb + _colsum(df * (1.0 - sg))
            dst = dst * eb + _dot(dob, qd, 0, 0)
            return dst, dlb, dng

        z = jnp.zeros((1, LANES), F32)
        _, dlb, dng = lax.fori_loop(0, nch, step, (jnp.zeros((HG_DK, HG_DK), F32), z, z))
        dlb_ref[...] = dlb
        dng_ref[...] = dng

    blk = (lp, LANES)
    h = HG_HEADS
    vec = pl.BlockSpec((1, LANES), lambda b, hh: (0, hh))
    svec = pl.BlockSpec((None, 1, LANES), lambda b, hh: (b, 0, hh))
    return pl.pallas_call(
        body, name="hgrn_bwd", grid=(nseq, h),
        in_specs=[pl.BlockSpec(blk, lambda b, hh: (b, hh)),
                  pl.BlockSpec(blk, lambda b, hh: (b, h + hh)),
                  pl.BlockSpec(blk, lambda b, hh: (b, 2 * h + hh)),
                  pl.BlockSpec(blk, lambda b, hh: (b, 3 * h + hh)),
                  pl.BlockSpec(blk, lambda b, hh: (b, hh)),
                  pl.BlockSpec(blk, lambda b, hh: (b, hh)),
                  pl.BlockSpec((None, None, nch, HG_DK, HG_DK), lambda b, hh: (b, hh, 0, 0, 0)),
                  vec, vec],
        out_specs=[pl.BlockSpec(blk, lambda b, hh: (b, hh))] * 4 + [svec, svec],
        out_shape=[jax.ShapeDtypeStruct((t, D_MODEL), F32)] * 4
        + [jax.ShapeDtypeStruct((nseq, 1, D_MODEL), F32)] * 2,
        compiler_params=_params(("parallel", "parallel")))(proj, proj, proj, proj, o, dog, states, lb, ng)


def _lower_bound(gamma):
    p = jax.nn.softmax(gamma, axis=0)
    return (jnp.cumsum(p, axis=0) - p[0])[1][None, :]


def local_step(x, tgt, w):
    nseq, seq, _ = x.shape
    lp = -(-(N_META + seq) // SB_BLOCK) * SB_BLOCK
    t = nseq * lp
    pad = lp - N_META - seq
    h0 = jnp.concatenate([jnp.broadcast_to(w['meta'][None], (nseq, N_META, D_MODEL)), x,
                          jnp.zeros((nseq, pad, D_MODEL), F32)], axis=1).reshape(t, D_MODEL)
    tgt_p = jnp.pad(tgt, ((0, 0), (N_META, pad), (0, 0))).reshape(t, D_MODEL)
    row = lambda v: v.reshape(1, -1)
    g = {}

    proj = matmul("in_ab", h0, w['w_in_ab'], 'nn')
    s5_names = ['s5_lam_re', 's5_lam_im', 's5_log_dt', 's5_b_re', 's5_b_im', 's5_c_re', 's5_c_im', 's5_d']
    mats, mats_vjp = jax.vjp(s5_matrices, *[w[n][0] for n in s5_names])
    ug = _to_groups(proj[:, :S5_WIDTH])
    yg, sst = s5_fwd(ug, *mats, nseq)
    ys5 = _from_groups(yg)
    y = gelu_fwd("gelu", ys5)
    gp = matmul("glu", y, w['s5_w_glu'], 'nn', bias=w['s5_b_glu'])
    a_out = glu_gate("glu_gate", y, gp)
    b_out, sb_tot = attn_fwd(proj, nseq, lp)
    cat = jnp.concatenate([a_out, b_out], axis=1)
    mix0 = matmul("out_ab", cat, w['w_out_ab'], 'nn')
    h1, xh1, rs1 = ln_fwd("ln_mix0", h0, mix0, row(w['ln_mix_g'][0]), row(w['ln_mix_b'][0]))
    hid0 = matmul("up0", h1, w['mlp_w_up'][0], 'nn', bias=row(w['mlp_b_up'][0]), act='relu2')
    dn0 = matmul("down0", hid0, w['mlp_w_down'][0], 'nn', bias=row(w['mlp_b_down'][0]))
    h2, xh2, rs2 = ln_fwd("ln_mlp0", h1, dn0, row(w['ln_mlp_g'][0]), row(w['ln_mlp_b'][0]))

    projc = matmul("in_c", h2, w['w_in_c'], 'nn')
    lb, lb_vjp = jax.vjp(_lower_bound, w['hgrn_gamma'])
    ng = w['hgrn_norm_g']
    o, og, states = hgrn_fwd(projc, lb, ng, nseq, lp)
    mix1 = matmul("out_c", og, w['w_out_c'], 'nn')
    h3, xh3, rs3 = ln_fwd("ln_mix1", h2, mix1, row(w['ln_mix_g'][1]), row(w['ln_mix_b'][1]))
    hid1 = matmul("up1", h3, w['mlp_w_up'][1], 'nn', bias=row(w['mlp_b_up'][1]), act='relu2')
    dn1 = matmul("down1", hid1, w['mlp_w_down'][1], 'nn', bias=row(w['mlp_b_down'][1]))
    h4, xh4, rs4 = ln_fwd("ln_mlp1", h3, dn1, row(w['ln_mlp_g'][1]), row(w['ln_mlp_b'][1]))
    dy, loss = loss_head("loss", h4, tgt_p, lp, seq)

    def mlp_bwd(l, dh_out, xh_out, rs_out, hid, h_in):
        dpre, dg_, db_, dbd = ln_bwd(f"ln_mlp{l}_b", dh_out, xh_out, rs_out, row(w['ln_mlp_g'][l]))
        dhid = matmul(f"down{l}_dx", dpre, w['mlp_w_down'][l], 'nt')
        dwd = matmul(f"down{l}_dw", hid, dpre, 'tn')
        dpu, dbu = relu2_bwd(f"relu2_{l}_b", dhid, hid)
        dh_in = matmul(f"up{l}_dx", dpu, w['mlp_w_up'][l], 'nt', add=dpre, add_scale=ALPHA)
        dwu = matmul(f"up{l}_dw", h_in, dpu, 'tn', out_slots=N_CHIPS)
        return dh_in, dict(ln_mlp_g=dg_, ln_mlp_b=db_, mlp_b_down=dbd, mlp_b_up=dbu, mlp_w_up=dwu, mlp_w_down=dwd)

    dh3, gm1 = mlp_bwd(1, dy, xh4, rs4, hid1, h3)
    dpre, dg1, db1, _ = ln_bwd("ln_mix1_b", dh3, xh3, rs3, row(w['ln_mix_g'][1]))
    g['w_out_c'] = matmul("out_c_dw", og, dpre, 'tn')
    dog = matmul("out_c_dx", dpre, w['w_out_c'], 'nt')
    dq, df, di, dgg, dlb, dng = hgrn_bwd(projc, o, dog, states, lb, ng, nseq, lp)
    dprojc = jnp.concatenate([dq, df, di, dgg], axis=1)
    dh2 = matmul("in_c_dx", dprojc, w['w_in_c'], 'nt', add=dpre, add_scale=ALPHA)
    g['w_in_c'] = matmul("in_c_dw", h2, dprojc, 'tn', out_slots=N_CHIPS)
    g['hgrn_gamma'] = lb_vjp(jnp.sum(dlb, axis=0))[0]
    g['hgrn_norm_g'] = jnp.sum(dng, axis=0)

    dh1, gm0 = mlp_bwd(0, dh2, xh2, rs2, hid0, h1)
    dpre, dg0, db0, _ = ln_bwd("ln_mix0_b", dh1, xh1, rs1, row(w['ln_mix_g'][0]))
    g['w_out_ab'] = matmul("out_ab_dw", cat, dpre, 'tn')
    dcat = matmul("out_ab_dx", dpre, w['w_out_ab'], 'nt')
    dgp, da_s, dbglu = glu_gate_bwd("glu_gate_b", dcat[:, :S5_WIDTH], y, gp)
    g['s5_w_glu'] = matmul("glu_dw", y, dgp, 'tn')
    g['s5_b_glu'] = dbglu
    dy5 = matmul("glu_dx", dgp, w['s5_w_glu'], 'nt', add=da_s)
    dys5 = gelu_bwd("gelu_b", dy5, ys5)
    dug, da1, da2, da3, dare, daim = s5_bwd(_to_groups(dys5), ug, sst, *mats, nseq)
    for n, v in zip(s5_names, mats_vjp((da1, da2, da3, dare, daim))):
        g[n] = v[None]
    dqa, dka, dva = attn_bwd(proj, sb_tot, dcat[:, S5_WIDTH:], nseq, lp)
    dproj = jnp.concatenate([_from_groups(dug), dqa, dka, dva], axis=1)
    dh0 = matmul("in_ab_dx", dproj, w['w_in_ab'], 'nt', add=dpre, add_scale=ALPHA)
    g['w_in_ab'] = matmul("in_ab_dw", h0, dproj, 'tn', out_slots=N_CHIPS)

    dh0 = dh0.reshape(nseq, lp, D_MODEL)
    grad_x = dh0[:, N_META:N_META + seq]
    g['meta'] = jnp.sum(dh0[:, :N_META], axis=0)
    g['ln_mix_g'] = jnp.concatenate([dg0, dg1], axis=0)
    g['ln_mix_b'] = jnp.concatenate([db0, db1], axis=0)
    for n in gm0:
        g[n] = jnp.stack([gm0[n], gm1[n]], axis=1 if n == 'mlp_w_up' else 0)
    for n in ('ln_mlp_g', 'ln_mlp_b', 'mlp_b_down', 'mlp_b_up'):
        g[n] = g[n].reshape(DEPTH, -1)
    return loss, grad_x, g


def _exchange(name, ins, out_shapes, n_local, n_remote, build):
    ni, no = len(ins), len(out_shapes)

    def body(*refs):
        in_refs, out_refs = refs[:ni], refs[ni:ni + no]
        lsem, ssem, rsem = refs[ni + no:]
        me = (lax.axis_index("x"), lax.axis_index("y"), lax.axis_index("c"))
        loc, rem = build(in_refs, out_refs, me)
        copies = [pltpu.make_async_copy(s, d, lsem.at[i]) for i, (s, d) in enumerate(loc)]
        copies += [pltpu.make_async_remote_copy(src_ref=s, dst_ref=d, send_sem=ssem.at[k], recv_sem=rsem.at[k],
                                                device_id=p, device_id_type=MESH)
                   for k, (s, d, p) in enumerate(rem)]
        for cp in copies:
            cp.start()
        for cp in copies:
            cp.wait()

    anyspec = pl.BlockSpec(memory_space=pl.ANY)
    return pl.pallas_call(
        body, name=name, in_specs=[anyspec] * ni, out_specs=[anyspec] * no, out_shape=out_shapes,
        scratch_shapes=[pltpu.SemaphoreType.DMA((n_local,)), pltpu.SemaphoreType.DMA((n_remote,)),
                        pltpu.SemaphoreType.DMA((n_remote,))])(*ins)


def _chip_peers(me):
    x, y, c = me
    return [(x ^ (m >> 1), y ^ (m & 1), c) for m in (1, 2, 3)]


def gather_over_chips(name, shards):
    def build(in_refs, out_refs, me):
        j = 2 * me[0] + me[1]
        loc = [(s, o.at[j]) for s, o in zip(in_refs, out_refs)]
        rem = [(s, o.at[j], p) for s, o in zip(in_refs, out_refs) for p in _chip_peers(me)]
        return loc, rem
    shapes = [jax.ShapeDtypeStruct((N_CHIPS,) + s.shape, s.dtype) for s in shards]
    return _exchange(name, shards, shapes, len(shards), 3 * len(shards), build)


def scatter_over_chips(name, fulls):
    def build(in_refs, out_refs, me):
        j = 2 * me[0] + me[1]
        loc = [(s.at[j], o.at[j]) for s, o in zip(in_refs, out_refs)]
        rem = []
        for s, o in zip(in_refs, out_refs):
            for p in _chip_peers(me):
                rem.append((s.at[2 * p[0] + p[1]], o.at[j], p))
        return loc, rem
    shapes = [jax.ShapeDtypeStruct(f.shape, f.dtype) for f in fulls]
    return _exchange(name, fulls, shapes, len(fulls), 3 * len(fulls), build)


def swap_with_sibling(name, parts):
    def build(in_refs, out_refs, me):
        sib = (me[0], me[1], 1 - me[2])
        return [], [(s, o, sib) for s, o in zip(in_refs, out_refs)]
    shapes = [jax.ShapeDtypeStruct(p.shape, p.dtype) for p in parts]
    return _exchange(name, parts, shapes, 1, len(parts), build)


def gather_over_devices(name, block):
    def build(in_refs, out_refs, me):
        x, y, c = me
        i = 4 * x + 2 * y + c
        src, out = in_refs[0], out_refs[0]
        rem = [(src, out.at[i], (x ^ (m >> 2), y ^ ((m >> 1) & 1), c ^ (m & 1))) for m in range(1, N_DEV)]
        return [(src, out.at[i])], rem
    shape = jax.ShapeDtypeStruct((N_DEV,) + block.shape, block.dtype)
    return _exchange(name, [block], [shape], 1, N_DEV - 1, build)[0]


def sum_slots(name, arr):
    s, r, wd = arr.shape
    tm = _pick(r, (512, 256, 128, 64, 32, 16, 8))

    def body(*refs):
        acc = refs[0][...]
        for ref in refs[1:s]:
            acc = acc + ref[...]
        refs[s][...] = acc

    specs = [pl.BlockSpec((None, tm, wd), functools.partial(lambda i, k: (k, i, 0), k=k)) for k in range(s)]
    return pl.pallas_call(body, name=name, grid=(r // tm,), in_specs=specs,
                          out_specs=pl.BlockSpec((tm, wd), lambda i: (i, 0)),
                          out_shape=jax.ShapeDtypeStruct((r, wd), arr.dtype),
                          compiler_params=_params(("parallel",)))(*([arr] * s))


def _pack(arrs):
    flat = jnp.concatenate([a.reshape(-1) for a in arrs])
    n = flat.shape[0]
    padded = -(-n // (8 * LANES)) * (8 * LANES)
    return jnp.pad(flat, (0, padded - n)).reshape(-1, LANES)


def _unpack(packed, shapes):
    flat = packed.reshape(-1)
    out, pos = [], 0
    for s in shapes:
        n = math.prod(s)
        out.append(flat[pos:pos + n].reshape(s))
        pos += n
    return out


def _as2d(a):
    return a.reshape(-1, a.shape[-1])


def kernel(x, meta, w_in_ab, s5_lam_re, s5_lam_im, s5_log_dt, s5_b_re, s5_b_im, s5_c_re, s5_c_im, s5_d, s5_w_glu, s5_b_glu, w_out_ab, w_in_c, hgrn_gamma, hgrn_norm_g, w_out_c, ln_mix_g, ln_mix_b, mlp_w_up, mlp_b_up, mlp_w_down, mlp_b_down, ln_mlp_g, ln_mlp_b, loss_target, m_meta, m_w_in_ab, m_s5_lam_re, m_s5_lam_im, m_s5_log_dt, m_s5_b_re, m_s5_b_im, m_s5_c_re, m_s5_c_im, m_s5_d, m_s5_w_glu, m_s5_b_glu, m_w_out_ab, m_w_in_c, m_hgrn_gamma, m_hgrn_norm_g, m_w_out_c, m_ln_mix_g, m_ln_mix_b, m_mlp_w_up, m_mlp_b_up, m_mlp_w_down, m_mlp_b_down, m_ln_mlp_g, m_ln_mlp_b, v_meta, v_w_in_ab, v_s5_lam_re, v_s5_lam_im, v_s5_log_dt, v_s5_b_re, v_s5_b_im, v_s5_c_re, v_s5_c_im, v_s5_d, v_s5_w_glu, v_s5_b_glu, v_w_out_ab, v_w_in_c, v_hgrn_gamma, v_hgrn_norm_g, v_w_out_c, v_ln_mix_g, v_ln_mix_b, v_mlp_w_up, v_mlp_b_up, v_mlp_w_down, v_mlp_b_down, v_ln_mlp_g, v_ln_mlp_b):
    given = dict(locals())
    wts = {n: given[n] for n in WEIGHTS}
    ms = {n: given['m_' + n] for n in WEIGHTS}
    vs = {n: given['v_' + n] for n in WEIGHTS}
    chip = 2 * lax.axis_index("x") + lax.axis_index("y")

    shards = [cast_bf16("cast_" + n, _as2d(wts[n])) for n in BIG]
    small_sh = jnp.concatenate([meta, hgrn_norm_g, jnp.zeros((7, meta.shape[1]), F32)], axis=0)
    full = gather_over_chips("gather_weights", shards + [small_sh])
    fw = dict(zip(BIG, full[:len(BIG)]))
    sm = full[len(BIG)]
    w = {n: wts[n] for n in SMALL}
    w['meta'] = sm[:, :N_META].transpose(1, 0, 2).reshape(N_META, D_MODEL)
    w['hgrn_norm_g'] = sm[:, N_META:N_META + 1].transpose(1, 0, 2).reshape(1, D_MODEL)
    w['w_in_ab'] = fw['w_in_ab']
    w['s5_w_glu'] = fw['s5_w_glu'].reshape(S5_WIDTH, S5_WIDTH)
    w['w_out_ab'] = fw['w_out_ab'].reshape(D_MODEL, D_MODEL)
    w['w_in_c'] = fw['w_in_c']
    w['w_out_c'] = fw['w_out_c'].reshape(D_MODEL, D_MODEL)
    up = fw['mlp_w_up'].reshape(N_CHIPS, DEPTH, D_MODEL, D_FF // N_CHIPS)
    dn = fw['mlp_w_down'].reshape(N_CHIPS, DEPTH, D_FF // N_CHIPS, D_MODEL)
    w['mlp_w_up'] = [up[:, l] for l in range(DEPTH)]
    w['mlp_w_down'] = [dn[:, l].reshape(D_FF, D_MODEL) for l in range(DEPTH)]

    loss, grad_x, g = local_step(x, loss_target, w)

    gfull = [g['w_in_ab'],
             g['s5_w_glu'].reshape(N_CHIPS, -1, S5_WIDTH),
             g['w_out_ab'].reshape(N_CHIPS, -1, D_MODEL),
             g['w_in_c'],
             g['w_out_c'].reshape(N_CHIPS, -1, D_MODEL),
             g['mlp_w_up'].reshape(N_CHIPS, DEPTH * D_MODEL, D_FF // N_CHIPS),
             g['mlp_w_down'].reshape(DEPTH, N_CHIPS, D_FF // N_CHIPS, D_MODEL).transpose(1, 0, 2, 3)
             .reshape(N_CHIPS, DEPTH * D_FF // N_CHIPS, D_MODEL)]
    recv = scatter_over_chips("scatter_grads", gfull)
    mine = [sum_slots("sum_chips_" + n, r) for n, r in zip(BIG, recv)]
    theirs = swap_with_sibling("swap_grads", mine)

    small_shapes = [(LANES,)] + [g[n].shape for n in SMALL]
    packed = _pack([loss.reshape(-1)] + [g[n] for n in SMALL])
    red = sum_slots("sum_small", gather_over_devices("gather_small", packed))
    red = _unpack(red, small_shapes)
    loss_out = red[0][0]
    gs = dict(zip(SMALL, red[1:]))
    gs['meta'] = lax.dynamic_slice_in_dim(gs['meta'], chip * meta.shape[1], meta.shape[1], axis=1)
    gs['hgrn_norm_g'] = lax.dynamic_slice_in_dim(gs['hgrn_norm_g'].reshape(1, -1), chip * meta.shape[1],
                                                 meta.shape[1], axis=1)

    grads, deltas, new_m, new_v = {}, {}, {}, {}
    for n, a, b in zip(BIG, mine, theirs):
        res = adamw("adamw_" + n, _as2d(wts[n]), [a, b], _as2d(ms[n]), _as2d(vs[n]))
        grads[n], deltas[n], new_m[n], new_v[n] = [r.reshape(wts[n].shape) for r in res]
    shapes = [wts[n].shape for n in SMALL]
    res = adamw("adamw_small", _pack([wts[n] for n in SMALL]), [_pack([gs[n] for n in SMALL])],
                _pack([ms[n] for n in SMALL]), _pack([vs[n] for n in SMALL]))
    for d, r in zip((grads, deltas, new_m, new_v), res):
        d.update(zip(SMALL, _unpack(r, shapes)))
    return (loss_out, grad_x, *[grads[n] for n in WEIGHTS], *[deltas[n] for n in WEIGHTS],
            *[new_m[n] for n in WEIGHTS], *[new_v[n] for n in WEIGHTS])
```

```python
import functools
import math

import jax
import jax.numpy as jnp
from jax import lax
from jax.experimental import pallas as pl
from jax.experimental.pallas import tpu as pltpu

F32 = jnp.float32
BF16 = jnp.bfloat16

D_MODEL = 1024
N_META = 16
DEPTH = 2
S5_WIDTH = 512
S5_GROUP = 16
S5_GROUPS = 32
S5_STATE = 64
S5_CHUNK = 16
SB_WIDTH = 512
SB_BLOCK = 128
SB_DEAD = -110.0
HG_HEADS = 8
HG_DK = 128
HG_CHUNK = 64
D_FF = 4096
ALPHA = (2.0 * DEPTH) ** 0.25
LN_EPS = 1e-5
RMS_EPS = 1e-6
ADAM_LR = 0.001
ADAM_B1 = 0.9
ADAM_B2 = 0.999
ADAM_EPS = 1e-08
ADAM_WD = 0.01
ADAM_STEP = 10
N_CHIPS = 4
N_DEV = 8
LANES = 128
MESH = pl.DeviceIdType.MESH
VMEM_LIMIT = 56 * 1024 * 1024

WEIGHTS = ['meta', 'w_in_ab', 's5_lam_re', 's5_lam_im', 's5_log_dt', 's5_b_re', 's5_b_im', 's5_c_re', 's5_c_im',
           's5_d', 's5_w_glu', 's5_b_glu', 'w_out_ab', 'w_in_c', 'hgrn_gamma', 'hgrn_norm_g', 'w_out_c',
           'ln_mix_g', 'ln_mix_b', 'mlp_w_up', 'mlp_b_up', 'mlp_w_down', 'mlp_b_down', 'ln_mlp_g', 'ln_mlp_b']
BIG = ['w_in_ab', 's5_w_glu', 'w_out_ab', 'w_in_c', 'w_out_c', 'mlp_w_up', 'mlp_w_down']
SHARDED_SMALL = ['meta', 'hgrn_norm_g']
SMALL = [n for n in WEIGHTS if n not in BIG]


def _params(sem=None):
    return pltpu.CompilerParams(dimension_semantics=sem, vmem_limit_bytes=VMEM_LIMIT)


def _pick(n, cands):
    for c in cands:
        if n % c == 0:
            return c
    return n


def _dot(a, b, ca, cb):
    return lax.dot_general(a, b, (((ca,), (cb,)), ((), ())), preferred_element_type=F32)


def _split3(x):
    hi = x.astype(BF16)
    r = x - hi.astype(F32)
    mid = r.astype(BF16)
    lo = (r - mid.astype(F32)).astype(BF16)
    return hi, mid, lo


def _dot_exact_rhs(x, m, cx, cm):
    hi = x.astype(BF16)
    lo = (x - hi.astype(F32)).astype(BF16)
    return _dot(hi, m, cx, cm) + _dot(lo, m, cx, cm)


def _dot3(a, b, ca, cb):
    ah = a.astype(BF16)
    al = (a - ah.astype(F32)).astype(BF16)
    bh = b.astype(BF16)
    bl = (b - bh.astype(F32)).astype(BF16)
    return _dot(ah, bh, ca, cb) + (_dot(ah, bl, ca, cb) + _dot(al, bh, ca, cb))


def rowwise(name, fn, row_ins, bc_ins, out_widths, sum_widths=(), out_dtypes=None, tm=None):
    t = row_ins[0].shape[0]
    if tm is None:
        wmax = max([a.shape[1] for a in row_ins] + list(out_widths))
        tm = _pick(t, (272, 256, 128, 64, 16, 8)) if wmax > 1024 else _pick(t, (544, 512, 256, 128, 64, 16, 8))
    nr, nb, no, ns = len(row_ins), len(bc_ins), len(out_widths), len(sum_widths)
    out_dtypes = out_dtypes or [F32] * no

    def body(*refs):
        i = pl.program_id(0)
        rows = [r[...] for r in refs[:nr]]
        bcs = [r[...] for r in refs[nr:nr + nb]]
        outs, sums = fn(i, tm, rows, bcs)
        for r, v in zip(refs[nr + nb:nr + nb + no], outs):
            r[...] = v.astype(r.dtype)
        if ns:
            srefs = refs[nr + nb + no:]

            @pl.when(i == 0)
            def _():
                for r in srefs:
                    r[...] = jnp.zeros(r.shape, r.dtype)

            for r, v in zip(srefs, sums):
                r[...] += v

    in_specs = [pl.BlockSpec((tm, a.shape[1]), lambda i: (i, 0)) for a in row_ins]
    in_specs += [pl.BlockSpec(a.shape, lambda i: (0, 0)) for a in bc_ins]
    out_specs = [pl.BlockSpec((tm, w), lambda i: (i, 0)) for w in out_widths]
    out_specs += [pl.BlockSpec((1, w), lambda i: (0, 0)) for w in sum_widths]
    out_shape = [jax.ShapeDtypeStruct((t, w), dt) for w, dt in zip(out_widths, out_dtypes)]
    out_shape += [jax.ShapeDtypeStruct((1, w), F32) for w in sum_widths]
    res = pl.pallas_call(body, name=name, grid=(t // tm,), in_specs=in_specs, out_specs=out_specs,
                         out_shape=out_shape, compiler_params=_params(("arbitrary",)))(*row_ins, *bc_ins)
    return res


def _colsum(v):
    return jnp.sum(v, axis=0, keepdims=True)


def matmul(name, a, b, mode, *, bias=None, act=None, add=None, add_scale=1.0, out_dtype=F32,
           out_slots=0, tm=None, tn=None, tk=None):
    slotted = b.ndim == 3
    if mode == 'nn':
        m, k = a.shape
        n = b.shape[-1] * (b.shape[0] if slotted else 1)
    elif mode == 'nt':
        m, k = a.shape
        n = b.shape[-2]
    else:
        k, m = a.shape
        n = b.shape[-1]
    ns = b.shape[-1] if slotted else None
    if mode == 'tn':
        tm = tm or _pick(m, (512, 256, 128))
        tn = tn or _pick(n if not out_slots else n // out_slots, (1024, 512, 256, 128))
        tk = tk or _pick(k, (1088, 1024, 768, 512, 384, 256, 128))
    else:
        tm = tm or _pick(m, (1088, 1024, 768, 512, 384, 256, 128))
        tn = tn or _pick(n if not (slotted and mode == 'nn') else ns, (512, 256, 128))
        tk = tk or _pick(k if not (slotted and mode == 'nt') else ns, (1024, 512, 256, 128))
    gm, gn, gk = m // tm, n // tn, k // tk

    if mode == 'nn':
        a_spec = pl.BlockSpec((tm, tk), lambda i, j, kk: (i, kk))
        if slotted:
            per = ns // tn
            b_spec = pl.BlockSpec((None, tk, tn), lambda i, j, kk: (j // per, kk, j % per))
        else:
            b_spec = pl.BlockSpec((tk, tn), lambda i, j, kk: (kk, j))
        ca, cb = 1, 0
    elif mode == 'nt':
        a_spec = pl.BlockSpec((tm, tk), lambda i, j, kk: (i, kk))
        if slotted:
            per = ns // tk
            b_spec = pl.BlockSpec((None, tn, tk), lambda i, j, kk: (kk // per, j, kk % per))
        else:
            b_spec = pl.BlockSpec((tn, tk), lambda i, j, kk: (j, kk))
        ca, cb = 1, 1
    else:
        a_spec = pl.BlockSpec((tk, tm), lambda i, j, kk: (kk, i))
        b_spec = pl.BlockSpec((tk, tn), lambda i, j, kk: (kk, j))
        ca, cb = 0, 0
    in_specs = [a_spec, b_spec]
    args = [a, b]
    if bias is not None:
        in_specs.append(pl.BlockSpec((1, tn), lambda i, j, kk: (0, j)))
        args.append(bias)
    if add is not None:
        in_specs.append(pl.BlockSpec((tm, tn), lambda i, j, kk: (i, j)))
        args.append(add)
    if out_slots:
        per_o = (n // out_slots) // tn
        out_spec = pl.BlockSpec((None, tm, tn), lambda i, j, kk: (j // per_o, i, j % per_o))
        out_shape = jax.ShapeDtypeStruct((out_slots, m, n // out_slots), out_dtype)
    else:
        out_spec = pl.BlockSpec((tm, tn), lambda i, j, kk: (i, j))
        out_shape = jax.ShapeDtypeStruct((m, n), out_dtype)

    def body(*refs):
        a_ref, b_ref = refs[0], refs[1]
        pos = 2
        bias_ref = add_ref = None
        if bias is not None:
            bias_ref = refs[pos]
            pos += 1
        if add is not None:
            add_ref = refs[pos]
            pos += 1
        o_ref = refs[pos]
        acc_ref = refs[pos + 1] if gk > 1 else None
        p = _dot(a_ref[...].astype(BF16), b_ref[...].astype(BF16), ca, cb)

        def finish(r):
            if bias_ref is not None:
                r = r + bias_ref[...]
            if act == 'relu2':
                r = jnp.square(jnp.maximum(r, 0.0))
            if add_ref is not None:
                r = r + add_scale * add_ref[...]
            o_ref[...] = r.astype(o_ref.dtype)

        if gk == 1:
            finish(p)
        else:
            kk = pl.program_id(2)

            @pl.when(kk == 0)
            def _():
                acc_ref[...] = p

            @pl.when(kk > 0)
            def _():
                acc_ref[...] += p

            @pl.when(kk == gk - 1)
            def _():
                finish(acc_ref[...])

    scratch = [pltpu.VMEM((tm, tn), F32)] if gk > 1 else []
    return pl.pallas_call(body, name=name, grid=(gm, gn, gk), in_specs=in_specs, out_specs=out_spec,
                          out_shape=out_shape, scratch_shapes=scratch,
                          compiler_params=_params(("parallel", "parallel", "arbitrary")))(*args)


def ln_fwd(name, h, mix, g, b):
    def fn(i, tm, rows, bcs):
        pre = ALPHA * rows[0] + rows[1]
        mu = jnp.mean(pre, axis=1, keepdims=True)
        xc = pre - mu
        var = jnp.mean(xc * xc, axis=1, keepdims=True)
        rstd = lax.rsqrt(var + LN_EPS)
        xhat = xc * rstd
        return (xhat * bcs[0] + bcs[1], xhat, rstd), ()
    return rowwise(name, fn, [h, mix], [g, b], [D_MODEL, D_MODEL, 1])


def ln_bwd(name, dy, xhat, rstd, g):
    def fn(i, tm, rows, bcs):
        dy_, xh, rs = rows
        dyg = dy_ * bcs[0]
        m1 = jnp.mean(dyg, axis=1, keepdims=True)
        m2 = jnp.mean(dyg * xh, axis=1, keepdims=True)
        dpre = rs * (dyg - m1 - xh * m2)
        return (dpre,), (_colsum(dy_ * xh), _colsum(dy_), _colsum(dpre))
    return rowwise(name, fn, [dy, xhat, rstd], [g], [D_MODEL], [D_MODEL] * 3)


def relu2_bwd(name, dhid, hid):
    def fn(i, tm, rows, bcs):
        dpre = rows[0] * (2.0 * jnp.sqrt(rows[1]))
        return (dpre,), (_colsum(dpre),)
    return rowwise(name, fn, [dhid, hid], [], [D_FF], [D_FF])


_GELU_C = math.sqrt(2.0 / math.pi)


def gelu_fwd(name, x):
    def fn(i, tm, rows, bcs):
        v = rows[0]
        u = _GELU_C * (v + 0.044715 * (v * v * v))
        return (0.5 * v * (1.0 + jnp.tanh(u)),), ()
    return rowwise(name, fn, [x], [], [x.shape[1]])[0]


def gelu_bwd(name, dy, x):
    def fn(i, tm, rows, bcs):
        v = rows[1]
        th = jnp.tanh(_GELU_C * (v + 0.044715 * (v * v * v)))
        dg = 0.5 * (1.0 + th) + 0.5 * v * (1.0 - th * th) * (_GELU_C * (1.0 + 3.0 * 0.044715 * v * v))
        return (rows[0] * dg,), ()
    return rowwise(name, fn, [dy, x], [], [x.shape[1]])[0]


def glu_gate(name, y, gp):
    def fn(i, tm, rows, bcs):
        return (rows[0] * jax.nn.sigmoid(rows[1]),), ()
    return rowwise(name, fn, [y, gp], [], [y.shape[1]])[0]


def glu_gate_bwd(name, da, y, gp):
    def fn(i, tm, rows, bcs):
        s = jax.nn.sigmoid(rows[2])
        dgp = rows[0] * rows[1] * s * (1.0 - s)
        return (dgp, rows[0] * s), (_colsum(dgp),)
    w = y.shape[1]
    return rowwise(name, fn, [da, y, gp], [], [w, w], [w])


def loss_head(name, h, tgt, lp, seq):
    def fn(i, tm, rows, bcs):
        pos = (i * tm + lax.broadcasted_iota(jnp.int32, (tm, 1), 0)) % lp
        valid = jnp.logical_and(pos >= N_META, pos < N_META + seq)
        err = jnp.where(valid, rows[0] - rows[1], 0.0)
        part = 0.5 * jnp.sum(jnp.mean(err * err, axis=1, keepdims=True), axis=0, keepdims=True)
        return (err * (1.0 / D_MODEL),), (jnp.broadcast_to(part, (1, LANES)),)
    return rowwise(name, fn, [h, tgt], [], [D_MODEL], [LANES])


def adamw(name, w, gs, m, v):
    ng = len(gs)

    def fn(i, tm, rows, bcs):
        w_ = rows[0]
        g = rows[1] if ng == 1 else rows[1] + rows[2]
        m_, v_ = rows[1 + ng], rows[2 + ng]
        m_ = ADAM_B1 * m_ + (1.0 - ADAM_B1) * g
        v_ = ADAM_B2 * v_ + (1.0 - ADAM_B2) * jnp.square(g)
        m_hat = m_ / (1.0 - ADAM_B1 ** ADAM_STEP)
        v_hat = v_ / (1.0 - ADAM_B2 ** ADAM_STEP)
        delta = -ADAM_LR * (m_hat / (jnp.sqrt(v_hat) + ADAM_EPS) + ADAM_WD * w_)
        return (g, delta, m_, v_), ()
    wd = w.shape[1]
    return rowwise(name, fn, [w] + list(gs) + [m, v], [], [wd] * 4)


def sum_rows(name, parts):
    def fn(i, tm, rows, bcs):
        s = rows[0]
        for r in rows[1:]:
            s = s + r
        return (s,), ()
    return rowwise(name, fn, list(parts), [], [parts[0].shape[1]])[0]


def sum_cores_bf16(name, a, b):
    def fn(i, tm, rows, bcs):
        return (rows[0] + rows[1],), ()
    return rowwise(name, fn, [_as2d(a), _as2d(b)], [], [a.shape[-1]], out_dtypes=[BF16])[0].reshape(a.shape)


def cast_bf16(name, w):
    def fn(i, tm, rows, bcs):
        return (rows[0],), ()
    return rowwise(name, fn, [w], [], [w.shape[1]], out_dtypes=[BF16])[0]


def s5_matrices(lam_re, lam_im, log_dt, b_re, b_im, c_re, c_im, d):
    c = S5_CHUNK
    dt = jnp.exp(log_dt)[:, None]
    tau = jnp.arange(c + 1, dtype=F32)[:, None, None]
    mag = jnp.exp(tau * (lam_re * dt)[None])
    ang = tau * (lam_im * dt)[None]
    pw_re, pw_im = mag * jnp.cos(ang), mag * jnp.sin(ang)
    nr, ni = pw_re[1] - 1.0, pw_im[1]
    den = lam_re * lam_re + lam_im * lam_im
    cf_re = (nr * lam_re + ni * lam_im) / den
    cf_im = (ni * lam_re - nr * lam_im) / den
    bb_re = cf_re[:, :, None] * b_re - cf_im[:, :, None] * b_im
    bb_im = cf_re[:, :, None] * b_im + cf_im[:, :, None] * b_re
    cp_re = c_re[None] * pw_re[:, :, None, :] - c_im[None] * pw_im[:, :, None, :]
    cp_im = c_re[None] * pw_im[:, :, None, :] + c_im[None] * pw_re[:, :, None, :]
    hp = lax.Precision.HIGHEST
    kern = (jnp.einsum('tghp,gpk->tghk', cp_re[:c], bb_re, precision=hp)
            - jnp.einsum('tghp,gpk->tghk', cp_im[:c], bb_im, precision=hp))
    ii = jnp.arange(c)
    sel = (ii[None, :, None] - ii[None, None, :] == ii[:, None, None]).astype(F32)
    a1 = jnp.einsum('tghk,tij->gjkih', kern, sel, precision=hp)
    a1 = a1 + jnp.einsum('ij,kh,gh->gjkih', jnp.eye(c, dtype=F32), jnp.eye(S5_GROUP, dtype=F32), d)
    a1 = a1.reshape(S5_GROUPS, c * S5_GROUP, c * S5_GROUP)
    a2 = jnp.concatenate([cp_re[1:], -cp_im[1:]], axis=-1)
    a2 = a2.transpose(1, 3, 0, 2).reshape(S5_GROUPS, 2 * S5_STATE, c * S5_GROUP)
    rv_re, rv_im = pw_re[:c][::-1], pw_im[:c][::-1]
    x_re = rv_re[:, :, :, None] * bb_re[None] - rv_im[:, :, :, None] * bb_im[None]
    x_im = rv_re[:, :, :, None] * bb_im[None] + rv_im[:, :, :, None] * bb_re[None]
    a3 = jnp.concatenate([x_re, x_im], axis=2)
    a3 = a3.transpose(1, 0, 3, 2).reshape(S5_GROUPS, c * S5_GROUP, 2 * S5_STATE)
    are = jnp.concatenate([pw_re[c], pw_re[c]], axis=-1)[:, None, :]
    aim = jnp.concatenate([-pw_im[c], pw_im[c]], axis=-1)[:, None, :]
    return a1, a2, a3, are, aim


def s5_fwd(ug, a1, a2, a3, are, aim, nseq):
    g, nc, cw = ug.shape
    per = nc // nseq
    sw = 2 * S5_STATE

    def body(u_ref, a1_ref, a2_ref, a3_ref, are_ref, aim_ref, y_ref, s_ref, x_ref):
        u = u_ref[...]
        x_ref[...] = _dot3(u, a3_ref[...], 1, 0)
        ar, ai = are_ref[...], aim_ref[...]

        def step(cix, carry):
            new = []
            for b in range(nseq):
                s = carry[b]
                s_ref[pl.ds(b * per + cix, 1), :] = s
                new.append(ar * s + ai * pltpu.roll(s, S5_STATE, 1) + x_ref[pl.ds(b * per + cix, 1), :])
            return tuple(new)

        lax.fori_loop(0, per, step, tuple(jnp.zeros((1, sw), F32) for _ in range(nseq)))
        y_ref[...] = _dot3(u, a1_ref[...], 1, 0) + _dot3(s_ref[...], a2_ref[...], 1, 0)

    def spec(shape):
        return pl.BlockSpec((None,) + shape, lambda i: (i, 0, 0))
    return pl.pallas_call(
        body, name="s5_fwd", grid=(g,),
        in_specs=[spec((nc, cw)), spec((cw, cw)), spec((sw, cw)), spec((cw, sw)), spec((1, sw)), spec((1, sw))],
        out_specs=[spec((nc, cw)), spec((nc, sw))],
        out_shape=[jax.ShapeDtypeStruct((g, nc, cw), F32), jax.ShapeDtypeStruct((g, nc, sw), F32)],
        scratch_shapes=[pltpu.VMEM((nc, sw), F32)],
        compiler_params=_params(("parallel",)))(ug, a1, a2, a3, are, aim)


def s5_bwd(dyg, ug, sst, a1, a2, a3, are, aim, nseq):
    g, nc, cw = ug.shape
    per = nc // nseq
    sw = 2 * S5_STATE

    def body(dy_ref, u_ref, s_ref, a1_ref, a2_ref, a3_ref, are_ref, aim_ref,
             du_ref, da1_ref, da2_ref, da3_ref, dare_ref, daim_ref, gd_ref, dx_ref):
        dy = dy_ref[...]
        u = u_ref[...]
        gd_ref[...] = _dot3(dy, a2_ref[...], 1, 1)
        da2_ref[...] = _dot3(s_ref[...], dy, 0, 0)
        ar, ai = are_ref[...], aim_ref[...]

        def step(t, carry):
            cix = per - 1 - t
            new = []
            for b in range(nseq):
                gn, dar, dai = carry[3 * b:3 * b + 3]
                row = pl.ds(b * per + cix, 1)
                dx_ref[row, :] = gn
                s = s_ref[row, :]
                dar = dar + gn * s
                dai = dai + gn * pltpu.roll(s, S5_STATE, 1)
                gcur = gd_ref[row, :] + ar * gn + pltpu.roll(ai * gn, S5_STATE, 1)
                new += [gcur, dar, dai]
            return tuple(new)

        z = jnp.zeros((1, sw), F32)
        fin = lax.fori_loop(0, per, step, tuple(z for _ in range(3 * nseq)))
        dar, dai = fin[1], fin[2]
        for b in range(1, nseq):
            dar = dar + fin[3 * b + 1]
            dai = dai + fin[3 * b + 2]
        dare_ref[...] = dar
        daim_ref[...] = dai
        dx = dx_ref[...]
        du_ref[...] = _dot3(dy, a1_ref[...], 1, 1) + _dot3(dx, a3_ref[...], 1, 1)
        da1_ref[...] = _dot3(u, dy, 0, 0)
        da3_ref[...] = _dot3(u, dx, 0, 0)

    def spec(shape):
        return pl.BlockSpec((None,) + shape, lambda i: (i, 0, 0))
    sds = jax.ShapeDtypeStruct
    return pl.pallas_call(
        body, name="s5_bwd", grid=(g,),
        in_specs=[spec((nc, cw)), spec((nc, cw)), spec((nc, sw)), spec((cw, cw)), spec((sw, cw)), spec((cw, sw)),
                  spec((1, sw)), spec((1, sw))],
        out_specs=[spec((nc, cw)), spec((cw, cw)), spec((sw, cw)), spec((cw, sw)), spec((1, sw)), spec((1, sw))],
        out_shape=[sds((g, nc, cw), F32), sds((g, cw, cw), F32), sds((g, sw, cw), F32), sds((g, cw, sw), F32),
                   sds((g, 1, sw), F32), sds((g, 1, sw), F32)],
        scratch_shapes=[pltpu.VMEM((nc, sw), F32), pltpu.VMEM((nc, sw), F32)],
        compiler_params=_params(("parallel",)))(dyg, ug, sst, a1, a2, a3, are, aim)


def _to_groups(u):
    t = u.shape[0]
    nc = t // S5_CHUNK
    return u.reshape(nc, S5_CHUNK, S5_GROUPS, S5_GROUP).transpose(2, 0, 1, 3).reshape(S5_GROUPS, nc, -1)


def _from_groups(yg):
    g, nc, _ = yg.shape
    return yg.reshape(g, nc, S5_CHUNK, S5_GROUP).transpose(1, 2, 0, 3).reshape(nc * S5_CHUNK, g * S5_GROUP)


def _sb_masks():
    row = lax.broadcasted_iota(jnp.int32, (SB_BLOCK, SB_BLOCK), 0)
    col = lax.broadcasted_iota(jnp.int32, (SB_BLOCK, SB_BLOCK), 1)
    lane = lax.broadcasted_iota(jnp.int32, (1, LANES), 1)
    return row, col, [lane < 64, lane >= 64]


def _sb_weights(z, vis, later_of, after):
    sp = jnp.maximum(z, 0.0) + jnp.log1p(jnp.exp(-jnp.abs(z)))
    lk = jnp.where(vis, -sp, 0.0)
    rs = jnp.sum(lk, axis=1, keepdims=True)
    later = _dot_exact_rhs(lk, after, 1, 0) + later_of(rs)
    lb = z - sp
    w = jnp.where(vis, jnp.exp(lb + later), 0.0)
    return w, rs, lb


def attn_fwd(proj, nseq, lp):
    nqb = lp // SB_BLOCK
    t = proj.shape[0]

    def body(q_ref, k_ref, v_ref, o_ref, tot_ref, first_ref):
        row, col, hmask = _sb_masks()
        after = (row > col).astype(BF16)
        tri = col < row
        lane8 = lax.broadcasted_iota(jnp.int32, (8, LANES), 1)

        def qloop(qi, firsts):
            q0 = pl.multiple_of(qi * SB_BLOCK, SB_BLOCK)
            q = q_ref[pl.ds(q0, SB_BLOCK), :] * 0.125
            qm = [jnp.where(hm, q, 0.0).astype(BF16) for hm in hmask]

            def alive(carry):
                return jnp.logical_and(carry[0] <= qi, carry[1] > 0)

            def kstep(carry):
                s, _, acc, lat0, lat1 = carry
                lats = [lat0, lat1]
                kj = qi - s
                k0 = pl.multiple_of(kj * SB_BLOCK, SB_BLOCK)
                k = k_ref[pl.ds(k0, SB_BLOCK), :].astype(BF16)
                v = v_ref[pl.ds(k0, SB_BLOCK), :]
                vis = jnp.logical_or(kj < qi, tri)
                for hd in range(2):
                    z = _dot(qm[hd], k, 1, 1)
                    w, rs, _ = _sb_weights(z, vis, functools.partial(lambda rs, lat: lat, lat=lats[hd]), after)
                    acc = acc + _dot(w.astype(BF16), jnp.where(hmask[hd], v, 0.0).astype(BF16), 1, 0)
                    lats[hd] = lats[hd] + rs
                go = (jnp.max(jnp.maximum(lats[0], lats[1])) > SB_DEAD).astype(jnp.int32)
                return s + 1, go, acc, lats[0], lats[1]

            z1 = jnp.zeros((SB_BLOCK, 1), F32)
            s, _, acc, lat0, lat1 = lax.while_loop(
                alive, kstep, (jnp.int32(0), jnp.int32(1), jnp.zeros((SB_BLOCK, LANES), F32), z1, z1))
            o_ref[pl.ds(q0, SB_BLOCK), :] = acc
            tot_ref[pl.ds(q0, SB_BLOCK), :] = jnp.where(hmask[0], lat0, lat1)
            return jnp.where(lane8 == qi, (qi - s + 1).astype(F32), firsts)

        first_ref[...] = lax.fori_loop(0, nqb, qloop, jnp.zeros((8, LANES), F32))

    nhp = SB_WIDTH // LANES
    qoff, koff, voff = S5_WIDTH // LANES, (S5_WIDTH + SB_WIDTH) // LANES, (S5_WIDTH + 2 * SB_WIDTH) // LANES
    return pl.pallas_call(
        body, name="attn_fwd", grid=(nseq, nhp),
        in_specs=[pl.BlockSpec((lp, LANES), lambda b, h: (b, qoff + h)),
                  pl.BlockSpec((lp, LANES), lambda b, h: (b, koff + h)),
                  pl.BlockSpec((lp, LANES), lambda b, h: (b, voff + h))],
        out_specs=[pl.BlockSpec((lp, LANES), lambda b, h: (b, h))] * 2
        + [pl.BlockSpec((None, None, 8, LANES), lambda b, h: (b, h, 0, 0))],
        out_shape=[jax.ShapeDtypeStruct((t, SB_WIDTH), F32)] * 2
        + [jax.ShapeDtypeStruct((nseq, nhp, 8, LANES), F32)],
        compiler_params=_params(("parallel", "parallel")))(proj, proj, proj)


def attn_bwd(proj, tot, first, do, nseq, lp):
    nqb = lp // SB_BLOCK
    t = proj.shape[0]

    def body(q_ref, k_ref, v_ref, tot_ref, first_ref, do_ref, dq_ref, dk_ref, dv_ref):
        row, col, hmask = _sb_masks()
        after = (row > col).astype(BF16)
        before = (row < col).astype(BF16)
        tri = col < row
        lane8 = lax.broadcasted_iota(jnp.int32, (8, LANES), 1)
        firsts = first_ref[...]
        dk_ref[...] = jnp.zeros(dk_ref.shape, F32)
        dv_ref[...] = jnp.zeros(dv_ref.shape, F32)

        def qloop(qi, _):
            first = jnp.max(jnp.where(lane8 == qi, firsts, 0.0)).astype(jnp.int32)
            first = jnp.clip(first, 0, qi)
            q0 = pl.multiple_of(qi * SB_BLOCK, SB_BLOCK)
            q = q_ref[pl.ds(q0, SB_BLOCK), :] * 0.125
            do_ = do_ref[pl.ds(q0, SB_BLOCK), :]
            tot_ = tot_ref[pl.ds(q0, SB_BLOCK), :]
            qm = [jnp.where(hm, q, 0.0).astype(BF16) for hm in hmask]
            dom = [jnp.where(hm, do_, 0.0).astype(BF16) for hm in hmask]
            tot = [jnp.max(jnp.where(hm, tot_, -jnp.inf), axis=1, keepdims=True) for hm in hmask]

            def kloop(kj, carry):
                dq = carry[0]
                pre = [carry[1], carry[2]]
                gpre = [carry[3], carry[4]]
                k0 = pl.multiple_of(kj * SB_BLOCK, SB_BLOCK)
                kf = k_ref[pl.ds(k0, SB_BLOCK), :]
                k = kf.astype(BF16)
                v = v_ref[pl.ds(k0, SB_BLOCK), :].astype(BF16)
                vis = jnp.logical_or(kj < qi, tri)
                dk_acc = jnp.zeros((SB_BLOCK, LANES), F32)
                dv_acc = jnp.zeros((SB_BLOCK, LANES), F32)
                for hd in range(2):
                    z = _dot(qm[hd], k, 1, 1)
                    later_of = functools.partial(lambda rs, t_, p_: t_ - p_ - rs, t_=tot[hd], p_=pre[hd])
                    w, rs, lb = _sb_weights(z, vis, later_of, after)
                    dw = _dot(dom[hd], v, 1, 1)
                    g = dw * w
                    dlk = _dot_exact_rhs(g, before, 1, 0) + gpre[hd]
                    sig = jnp.exp(lb)
                    dz = jnp.where(vis, g * (1.0 - sig) - dlk * sig, 0.0).astype(BF16)
                    dq = dq + _dot(dz, jnp.where(hmask[hd], kf, 0.0).astype(BF16), 1, 0)
                    dk_acc = dk_acc + _dot(dz, qm[hd], 0, 0)
                    dv_acc = dv_acc + _dot(w.astype(BF16), dom[hd], 0, 0)
                    pre[hd] = pre[hd] + rs
                    gpre[hd] = gpre[hd] + jnp.sum(g, axis=1, keepdims=True)
                dk_ref[pl.ds(k0, SB_BLOCK), :] += dk_acc
                dv_ref[pl.ds(k0, SB_BLOCK), :] += dv_acc
                return dq, pre[0], pre[1], gpre[0], gpre[1]

            z1 = jnp.zeros((SB_BLOCK, 1), F32)
            res = lax.fori_loop(first, qi + 1, kloop, (jnp.zeros((SB_BLOCK, LANES), F32), z1, z1, z1, z1))
            dq_ref[pl.ds(q0, SB_BLOCK), :] = res[0] * 0.125
            return 0

        lax.fori_loop(0, nqb, qloop, 0)

    nhp = SB_WIDTH // LANES
    qoff, koff, voff = S5_WIDTH // LANES, (S5_WIDTH + SB_WIDTH) // LANES, (S5_WIDTH + 2 * SB_WIDTH) // LANES
    blk = (lp, LANES)
    return pl.pallas_call(
        body, name="attn_bwd", grid=(nseq, nhp),
        in_specs=[pl.BlockSpec(blk, lambda b, h: (b, qoff + h)),
                  pl.BlockSpec(blk, lambda b, h: (b, koff + h)),
                  pl.BlockSpec(blk, lambda b, h: (b, voff + h)),
                  pl.BlockSpec(blk, lambda b, h: (b, h)),
                  pl.BlockSpec((None, None, 8, LANES), lambda b, h: (b, h, 0, 0)),
                  pl.BlockSpec(blk, lambda b, h: (b, h))],
        out_specs=[pl.BlockSpec(blk, lambda b, h: (b, h))] * 3,
        out_shape=[jax.ShapeDtypeStruct((t, SB_WIDTH), F32)] * 3,
        compiler_params=_params(("parallel", "parallel")))(proj, proj, proj, tot, first, do)


def _hg_gates(fc, lb):
    sg = jax.nn.sigmoid(fc)
    f = lb + (1.0 - lb) * sg
    return sg, f


def _hg_decay(f, incl):
    bcum = _dot_exact_lhs(incl, jnp.log(f))
    return bcum, bcum[HG_CHUNK - 1:HG_CHUNK, :]


def _dot_exact_lhs(m, x):
    hi, mid, lo = _split3(x)
    return _dot(m, hi, 1, 0) + (_dot(m, mid, 1, 0) + _dot(m, lo, 1, 0))


def hgrn_fwd(proj, lb, ng, nseq, lp):
    nch = lp // HG_CHUNK
    t = proj.shape[0]
    c = HG_CHUNK

    def body(q_ref, f_ref, v_ref, g_ref, lb_ref, ng_ref, o_ref, og_ref, st_ref):
        row = lax.broadcasted_iota(jnp.int32, (c, c), 0)
        col = lax.broadcasted_iota(jnp.int32, (c, c), 1)
        causal = col <= row
        incl = causal.astype(BF16)
        lbv, ngv = lb_ref[...], ng_ref[...]

        def step(ci, st):
            r0 = pl.multiple_of(ci * c, c)
            rows = pl.ds(r0, c)
            st_ref[ci] = st
            _, f = _hg_gates(f_ref[rows, :], lbv)
            bcum, blast = _hg_decay(f, incl)
            kk = 1.0 - f
            vc = v_ref[rows, :].astype(BF16)
            qd = (q_ref[rows, :] * jnp.exp(bcum)).astype(BF16)
            kd = (kk * jnp.exp(-bcum)).astype(BF16)
            a = jnp.where(causal, _dot(qd, kd, 1, 1), 0.0)
            o = _dot(a.astype(BF16), vc, 1, 0) + _dot(qd, st.astype(BF16), 1, 1)
            kend = (kk * jnp.exp(blast - bcum)).astype(BF16)
            st = st * jnp.exp(blast) + _dot(vc, kend, 0, 0)
            o_ref[rows, :] = o
            r = lax.rsqrt(jnp.mean(o * o, axis=1, keepdims=True) + RMS_EPS)
            gc = g_ref[rows, :]
            og_ref[rows, :] = o * r * ngv * (gc * jax.nn.sigmoid(gc))
            return st

        lax.fori_loop(0, nch, step, jnp.zeros((HG_DK, HG_DK), F32))

    blk = (lp, LANES)
    h = HG_HEADS
    return pl.pallas_call(
        body, name="hgrn_fwd", grid=(nseq, h),
        in_specs=[pl.BlockSpec(blk, lambda b, hh: (b, hh)),
                  pl.BlockSpec(blk, lambda b, hh: (b, h + hh)),
                  pl.BlockSpec(blk, lambda b, hh: (b, 2 * h + hh)),
                  pl.BlockSpec(blk, lambda b, hh: (b, 3 * h + hh)),
                  pl.BlockSpec((1, LANES), lambda b, hh: (0, hh)),
                  pl.BlockSpec((1, LANES), lambda b, hh: (0, hh))],
        out_specs=[pl.BlockSpec(blk, lambda b, hh: (b, hh)),
                   pl.BlockSpec(blk, lambda b, hh: (b, hh)),
                   pl.BlockSpec((None, None, nch, HG_DK, HG_DK), lambda b, hh: (b, hh, 0, 0, 0))],
        out_shape=[jax.ShapeDtypeStruct((t, D_MODEL), F32), jax.ShapeDtypeStruct((t, D_MODEL), F32),
                   jax.ShapeDtypeStruct((nseq, h, nch, HG_DK, HG_DK), F32)],
        compiler_params=_params(("parallel", "parallel")))(proj, proj, proj, proj, lb, ng)


def hgrn_bwd(proj, o, dog, states, lb, ng, nseq, lp):
    nch = lp // HG_CHUNK
    t = proj.shape[0]
    c = HG_CHUNK

    def body(q_ref, f_ref, v_ref, g_ref, o_ref, dog_ref, st_ref, lb_ref, ng_ref,
             dq_ref, df_ref, dv_ref, dg_ref, dlb_ref, dng_ref):
        row = lax.broadcasted_iota(jnp.int32, (c, c), 0)
        col = lax.broadcasted_iota(jnp.int32, (c, c), 1)
        causal = col <= row
        incl = causal.astype(BF16)
        from_here = (col >= row).astype(BF16)
        last = lax.broadcasted_iota(jnp.int32, (c, 1), 0) == c - 1
        lbv, ngv = lb_ref[...], ng_ref[...]

        def step(s, carry):
            dst, dlb, dng = carry
            ci = nch - 1 - s
            r0 = pl.multiple_of(ci * c, c)
            rows = pl.ds(r0, c)
            oc, gc, dogc = o_ref[rows, :], g_ref[rows, :], dog_ref[rows, :]
            r = lax.rsqrt(jnp.mean(oc * oc, axis=1, keepdims=True) + RMS_EPS)
            sgg = jax.nn.sigmoid(gc)
            silu = gc * sgg
            dg_ref[rows, :] = dogc * (oc * r * ngv) * (sgg * (1.0 + gc * (1.0 - sgg)))
            don = dogc * silu
            dng = dng + _colsum(don * oc * r)
            tt = don * ngv
            do = r * (tt - oc * (r * r) * jnp.mean(tt * oc, axis=1, keepdims=True))
            st = st_ref[ci]
            fc = f_ref[rows, :]
            sg, f = _hg_gates(fc, lbv)
            bcum, blast = _hg_decay(f, incl)
            kk = 1.0 - f
            qc, vf = q_ref[rows, :], v_ref[rows, :]
            pdec = jnp.exp(bcum)
            ndec = jnp.exp(-bcum)
            edec = jnp.exp(blast - bcum)
            eb = jnp.exp(blast)
            qd_f, kd_f, kend_f = qc * pdec, kk * ndec, kk * edec
            qd, kd, kend = qd_f.astype(BF16), kd_f.astype(BF16), kend_f.astype(BF16)
            vc, dob, stb, dstb = vf.astype(BF16), do.astype(BF16), st.astype(BF16), dst.astype(BF16)
            a = jnp.where(causal, _dot(qd, kd, 1, 1), 0.0).astype(BF16)
            da = jnp.where(causal, _dot(dob, vc, 1, 1), 0.0).astype(BF16)
            dv_ref[rows, :] = _dot(a, dob, 0, 0) + _dot(kend, dstb, 1, 1)
            dqd = _dot(da, kd, 1, 0) + _dot(dob, stb, 1, 0)
            dkd = _dot(da, qd, 0, 0)
            dkend = _dot(vc, dstb, 1, 0)
            dq_ref[rows, :] = dqd * pdec
            dblast = _colsum(dkend * kend_f) + eb * _colsum(dst * st)
            dbcum = dqd * qd_f - dkd * kd_f - dkend * kend_f
            dbcum = dbcum + jnp.where(last, dblast, 0.0)
            dlf = _dot_exact_lhs(from_here, dbcum)
            df = dlf / f - (dkd * ndec + dkend * edec)
            df_ref[rows, :] = df * (1.0 - lbv) * sg * (1.0 - sg)
            dlb = dlb + _colsum(df * (1.0 - sg))
            dst = dst * eb + _dot(dob, qd, 0, 0)
            return dst, dlb, dng

        z = jnp.zeros((1, LANES), F32)
        _, dlb, dng = lax.fori_loop(0, nch, step, (jnp.zeros((HG_DK, HG_DK), F32), z, z))
        dlb_ref[...] = dlb
        dng_ref[...] = dng

    blk = (lp, LANES)
    h = HG_HEADS
    vec = pl.BlockSpec((1, LANES), lambda b, hh: (0, hh))
    svec = pl.BlockSpec((None, 1, LANES), lambda b, hh: (b, 0, hh))
    return pl.pallas_call(
        body, name="hgrn_bwd", grid=(nseq, h),
        in_specs=[pl.BlockSpec(blk, lambda b, hh: (b, hh)),
                  pl.BlockSpec(blk, lambda b, hh: (b, h + hh)),
                  pl.BlockSpec(blk, lambda b, hh: (b, 2 * h + hh)),
                  pl.BlockSpec(blk, lambda b, hh: (b, 3 * h + hh)),
                  pl.BlockSpec(blk, lambda b, hh: (b, hh)),
                  pl.BlockSpec(blk, lambda b, hh: (b, hh)),
                  pl.BlockSpec((None, None, nch, HG_DK, HG_DK), lambda b, hh: (b, hh, 0, 0, 0)),
                  vec, vec],
        out_specs=[pl.BlockSpec(blk, lambda b, hh: (b, hh))] * 4 + [svec, svec],
        out_shape=[jax.ShapeDtypeStruct((t, D_MODEL), F32)] * 4
        + [jax.ShapeDtypeStruct((nseq, 1, D_MODEL), F32)] * 2,
        compiler_params=_params(("parallel", "parallel")))(proj, proj, proj, proj, o, dog, states, lb, ng)


def _lower_bound(gamma):
    p = jax.nn.softmax(gamma, axis=0)
    return (jnp.cumsum(p, axis=0) - p[0])[1][None, :]


def local_step(x, tgt, w):
    nseq, seq, _ = x.shape
    lp = -(-(N_META + seq) // SB_BLOCK) * SB_BLOCK
    t = nseq * lp
    pad = lp - N_META - seq
    h0 = jnp.concatenate([jnp.broadcast_to(w['meta'][None], (nseq, N_META, D_MODEL)), x,
                          jnp.zeros((nseq, pad, D_MODEL), F32)], axis=1).reshape(t, D_MODEL)
    tgt_p = jnp.pad(tgt, ((0, 0), (N_META, pad), (0, 0))).reshape(t, D_MODEL)
    row = lambda v: v.reshape(1, -1)
    g = {}

    proj = matmul("in_ab", h0, w['w_in_ab'], 'nn')
    s5_names = ['s5_lam_re', 's5_lam_im', 's5_log_dt', 's5_b_re', 's5_b_im', 's5_c_re', 's5_c_im', 's5_d']
    mats, mats_vjp = jax.vjp(s5_matrices, *[w[n][0] for n in s5_names])
    ug = _to_groups(proj[:, :S5_WIDTH])
    yg, sst = s5_fwd(ug, *mats, nseq)
    ys5 = _from_groups(yg)
    y = gelu_fwd("gelu", ys5)
    gp = matmul("glu", y, w['s5_w_glu'], 'nn', bias=w['s5_b_glu'])
    a_out = glu_gate("glu_gate", y, gp)
    b_out, sb_tot, sb_first = attn_fwd(proj, nseq, lp)
    cat = jnp.concatenate([a_out, b_out], axis=1)
    mix0 = matmul("out_ab", cat, w['w_out_ab'], 'nn')
    h1, xh1, rs1 = ln_fwd("ln_mix0", h0, mix0, row(w['ln_mix_g'][0]), row(w['ln_mix_b'][0]))
    hid0 = matmul("up0", h1, w['mlp_w_up'][0], 'nn', bias=row(w['mlp_b_up'][0]), act='relu2')
    dn0 = matmul("down0", hid0, w['mlp_w_down'][0], 'nn', bias=row(w['mlp_b_down'][0]))
    h2, xh2, rs2 = ln_fwd("ln_mlp0", h1, dn0, row(w['ln_mlp_g'][0]), row(w['ln_mlp_b'][0]))

    projc = matmul("in_c", h2, w['w_in_c'], 'nn')
    lb, lb_vjp = jax.vjp(_lower_bound, w['hgrn_gamma'])
    ng = w['hgrn_norm_g']
    o, og, states = hgrn_fwd(projc, lb, ng, nseq, lp)
    mix1 = matmul("out_c", og, w['w_out_c'], 'nn')
    h3, xh3, rs3 = ln_fwd("ln_mix1", h2, mix1, row(w['ln_mix_g'][1]), row(w['ln_mix_b'][1]))
    hid1 = matmul("up1", h3, w['mlp_w_up'][1], 'nn', bias=row(w['mlp_b_up'][1]), act='relu2')
    dn1 = matmul("down1", hid1, w['mlp_w_down'][1], 'nn', bias=row(w['mlp_b_down'][1]))
    h4, xh4, rs4 = ln_fwd("ln_mlp1", h3, dn1, row(w['ln_mlp_g'][1]), row(w['ln_mlp_b'][1]))
    dy, loss = loss_head("loss", h4, tgt_p, lp, seq)

    def mlp_bwd(l, dh_out, xh_out, rs_out, hid, h_in):
        dpre, dg_, db_, dbd = ln_bwd(f"ln_mlp{l}_b", dh_out, xh_out, rs_out, row(w['ln_mlp_g'][l]))
        dhid = matmul(f"down{l}_dx", dpre, w['mlp_w_down'][l], 'nt')
        dwd = matmul(f"down{l}_dw", hid, dpre, 'tn')
        dpu, dbu = relu2_bwd(f"relu2_{l}_b", dhid, hid)
        dh_in = matmul(f"up{l}_dx", dpu, w['mlp_w_up'][l], 'nt', add=dpre, add_scale=ALPHA)
        dwu = matmul(f"up{l}_dw", h_in, dpu, 'tn', out_slots=N_CHIPS)
        return dh_in, dict(ln_mlp_g=dg_, ln_mlp_b=db_, mlp_b_down=dbd, mlp_b_up=dbu, mlp_w_up=dwu, mlp_w_down=dwd)

    dh3, gm1 = mlp_bwd(1, dy, xh4, rs4, hid1, h3)
    dpre, dg1, db1, _ = ln_bwd("ln_mix1_b", dh3, xh3, rs3, row(w['ln_mix_g'][1]))
    g['w_out_c'] = matmul("out_c_dw", og, dpre, 'tn')
    dog = matmul("out_c_dx", dpre, w['w_out_c'], 'nt')
    dq, df, di, dgg, dlb, dng = hgrn_bwd(projc, o, dog, states, lb, ng, nseq, lp)
    dprojc = jnp.concatenate([dq, df, di, dgg], axis=1)
    dh2 = matmul("in_c_dx", dprojc, w['w_in_c'], 'nt', add=dpre, add_scale=ALPHA)
    g['w_in_c'] = matmul("in_c_dw", h2, dprojc, 'tn', out_slots=N_CHIPS)
    g['hgrn_gamma'] = lb_vjp(jnp.sum(dlb, axis=0))[0]
    g['hgrn_norm_g'] = jnp.sum(dng, axis=0)

    dh1, gm0 = mlp_bwd(0, dh2, xh2, rs2, hid0, h1)
    dpre, dg0, db0, _ = ln_bwd("ln_mix0_b", dh1, xh1, rs1, row(w['ln_mix_g'][0]))
    g['w_out_ab'] = matmul("out_ab_dw", cat, dpre, 'tn')
    dcat = matmul("out_ab_dx", dpre, w['w_out_ab'], 'nt')
    dgp, da_s, dbglu = glu_gate_bwd("glu_gate_b", dcat[:, :S5_WIDTH], y, gp)
    g['s5_w_glu'] = matmul("glu_dw", y, dgp, 'tn')
    g['s5_b_glu'] = dbglu
    dy5 = matmul("glu_dx", dgp, w['s5_w_glu'], 'nt', add=da_s)
    dys5 = gelu_bwd("gelu_b", dy5, ys5)
    dug, da1, da2, da3, dare, daim = s5_bwd(_to_groups(dys5), ug, sst, *mats, nseq)
    for n, v in zip(s5_names, mats_vjp((da1, da2, da3, dare, daim))):
        g[n] = v[None]
    dqa, dka, dva = attn_bwd(proj, sb_tot, sb_first, dcat[:, S5_WIDTH:], nseq, lp)
    dproj = jnp.concatenate([_from_groups(dug), dqa, dka, dva], axis=1)
    dh0 = matmul("in_ab_dx", dproj, w['w_in_ab'], 'nt', add=dpre, add_scale=ALPHA)
    g['w_in_ab'] = matmul("in_ab_dw", h0, dproj, 'tn', out_slots=N_CHIPS)

    dh0 = dh0.reshape(nseq, lp, D_MODEL)
    grad_x = dh0[:, N_META:N_META + seq]
    g['meta'] = jnp.sum(dh0[:, :N_META], axis=0)
    g['ln_mix_g'] = jnp.concatenate([dg0, dg1], axis=0)
    g['ln_mix_b'] = jnp.concatenate([db0, db1], axis=0)
    for n in gm0:
        g[n] = jnp.stack([gm0[n], gm1[n]], axis=1 if n == 'mlp_w_up' else 0)
    for n in ('ln_mlp_g', 'ln_mlp_b', 'mlp_b_down', 'mlp_b_up'):
        g[n] = g[n].reshape(DEPTH, -1)
    return loss, grad_x, g


def _exchange(name, ins, out_shapes, n_local, n_remote, build):
    ni, no = len(ins), len(out_shapes)

    def body(*refs):
        in_refs, out_refs = refs[:ni], refs[ni:ni + no]
        lsem, ssem, rsem = refs[ni + no:]
        me = (lax.axis_index("x"), lax.axis_index("y"), lax.axis_index("c"))
        nl = nr = 0
        for loc, rem in build(in_refs, out_refs, me):
            copies = [pltpu.make_async_copy(s, d, lsem.at[nl + i]) for i, (s, d) in enumerate(loc)]
            copies += [pltpu.make_async_remote_copy(src_ref=s, dst_ref=d, send_sem=ssem.at[nr + k],
                                                    recv_sem=rsem.at[nr + k], device_id=p, device_id_type=MESH)
                       for k, (s, d, p) in enumerate(rem)]
            nl, nr = nl + len(loc), nr + len(rem)
            for cp in copies:
                cp.start()
            for cp in copies:
                cp.wait()

    anyspec = pl.BlockSpec(memory_space=pl.ANY)
    return pl.pallas_call(
        body, name=name, in_specs=[anyspec] * ni, out_specs=[anyspec] * no, out_shape=out_shapes,
        scratch_shapes=[pltpu.SemaphoreType.DMA((n_local,)), pltpu.SemaphoreType.DMA((n_remote,)),
                        pltpu.SemaphoreType.DMA((n_remote,))])(*ins)


def _chip_peers(me):
    x, y, c = me
    return [(x ^ (m >> 1), y ^ (m & 1), c) for m in (1, 2, 3)]


def _half(ref, c, axis):
    h = ref.shape[axis] // 2
    idx = [slice(None)] * axis + [pl.ds(c * h, h)]
    return ref.at[tuple(idx)]


def gather_over_chips(name, shards):
    def build(in_refs, out_refs, me):
        x, y, c = me
        j = 2 * x + y
        sib = (x, y, 1 - c)
        loc = [(s, o.at[j]) for s, o in zip(in_refs, out_refs)]
        ici = [(_half(s, c, 0), _half(o.at[j], c, 0), p)
               for s, o in zip(in_refs, out_refs) for p in _chip_peers(me)]
        d2d = [(_half(o.at[2 * p[0] + p[1]], c, 0), _half(o.at[2 * p[0] + p[1]], c, 0), sib)
               for o in out_refs for p in _chip_peers(me)]
        return [(loc, ici), ([], d2d)]
    shapes = [jax.ShapeDtypeStruct((N_CHIPS,) + s.shape, s.dtype) for s in shards]
    return _exchange(name, shards, shapes, len(shards), 6 * len(shards), build)


def fold_cores(name, fulls):
    def build(in_refs, out_refs, me):
        x, y, c = me
        n = len(in_refs)
        loc = [(_half(s, c, 1), o) for s, o in zip(in_refs, out_refs[:n])]
        rem = [(_half(s, 1 - c, 1), o, (x, y, 1 - c)) for s, o in zip(in_refs, out_refs[n:])]
        return [(loc, rem)]
    shapes = [jax.ShapeDtypeStruct((f.shape[0], f.shape[1] // 2, f.shape[2]), f.dtype) for f in fulls]
    res = _exchange(name, fulls, shapes + shapes, len(fulls), len(fulls), build)
    return res[:len(fulls)], res[len(fulls):]


def scatter_over_chips(name, parts):
    def build(in_refs, out_refs, me):
        j = 2 * me[0] + me[1]
        loc = [(s.at[j], o.at[j]) for s, o in zip(in_refs, out_refs)]
        rem = [(s.at[2 * p[0] + p[1]], o.at[j], p) for s, o in zip(in_refs, out_refs) for p in _chip_peers(me)]
        return [(loc, rem)]
    shapes = [jax.ShapeDtypeStruct(f.shape, f.dtype) for f in parts]
    return _exchange(name, parts, shapes, len(parts), 3 * len(parts), build)


def join_cores(name, halves):
    def build(in_refs, out_refs, me):
        x, y, c = me
        loc = [(s, o.at[c]) for s, o in zip(in_refs, out_refs)]
        rem = [(s, o.at[c], (x, y, 1 - c)) for s, o in zip(in_refs, out_refs)]
        return [(loc, rem)]
    shapes = [jax.ShapeDtypeStruct((2,) + h.shape, h.dtype) for h in halves]
    return _exchange(name, halves, shapes, len(halves), len(halves), build)


def gather_over_devices(name, block):
    def build(in_refs, out_refs, me):
        x, y, c = me
        i = 4 * x + 2 * y + c
        src, out = in_refs[0], out_refs[0]
        rem = [(src, out.at[i], (x ^ (m >> 2), y ^ ((m >> 1) & 1), c ^ (m & 1))) for m in range(1, N_DEV)]
        return [([(src, out.at[i])], rem)]
    shape = jax.ShapeDtypeStruct((N_DEV,) + block.shape, block.dtype)
    return _exchange(name, [block], [shape], 1, N_DEV - 1, build)[0]


def sum_slots(name, arr):
    s, r, wd = arr.shape
    tm = _pick(r, (512, 256, 128, 64, 32, 16, 8))

    def body(*refs):
        acc = refs[0][...].astype(F32)
        for ref in refs[1:s]:
            acc = acc + ref[...].astype(F32)
        refs[s][...] = acc

    specs = [pl.BlockSpec((None, tm, wd), functools.partial(lambda i, k: (k, i, 0), k=k)) for k in range(s)]
    return pl.pallas_call(body, name=name, grid=(r // tm,), in_specs=specs,
                          out_specs=pl.BlockSpec((tm, wd), lambda i: (i, 0)),
                          out_shape=jax.ShapeDtypeStruct((r, wd), F32),
                          compiler_params=_params(("parallel",)))(*([arr] * s))


def _pack(arrs):
    flat = jnp.concatenate([a.reshape(-1) for a in arrs])
    n = flat.shape[0]
    padded = -(-n // (8 * LANES)) * (8 * LANES)
    return jnp.pad(flat, (0, padded - n)).reshape(-1, LANES)


def _unpack(packed, shapes):
    flat = packed.reshape(-1)
    out, pos = [], 0
    for s in shapes:
        n = math.prod(s)
        out.append(flat[pos:pos + n].reshape(s))
        pos += n
    return out


def _as2d(a):
    return a.reshape(-1, a.shape[-1])


def kernel(x, meta, w_in_ab, s5_lam_re, s5_lam_im, s5_log_dt, s5_b_re, s5_b_im, s5_c_re, s5_c_im, s5_d, s5_w_glu, s5_b_glu, w_out_ab, w_in_c, hgrn_gamma, hgrn_norm_g, w_out_c, ln_mix_g, ln_mix_b, mlp_w_up, mlp_b_up, mlp_w_down, mlp_b_down, ln_mlp_g, ln_mlp_b, loss_target, m_meta, m_w_in_ab, m_s5_lam_re, m_s5_lam_im, m_s5_log_dt, m_s5_b_re, m_s5_b_im, m_s5_c_re, m_s5_c_im, m_s5_d, m_s5_w_glu, m_s5_b_glu, m_w_out_ab, m_w_in_c, m_hgrn_gamma, m_hgrn_norm_g, m_w_out_c, m_ln_mix_g, m_ln_mix_b, m_mlp_w_up, m_mlp_b_up, m_mlp_w_down, m_mlp_b_down, m_ln_mlp_g, m_ln_mlp_b, v_meta, v_w_in_ab, v_s5_lam_re, v_s5_lam_im, v_s5_log_dt, v_s5_b_re, v_s5_b_im, v_s5_c_re, v_s5_c_im, v_s5_d, v_s5_w_glu, v_s5_b_glu, v_w_out_ab, v_w_in_c, v_hgrn_gamma, v_hgrn_norm_g, v_w_out_c, v_ln_mix_g, v_ln_mix_b, v_mlp_w_up, v_mlp_b_up, v_mlp_w_down, v_mlp_b_down, v_ln_mlp_g, v_ln_mlp_b):
    given = dict(locals())
    wts = {n: given[n] for n in WEIGHTS}
    ms = {n: given['m_' + n] for n in WEIGHTS}
    vs = {n: given['v_' + n] for n in WEIGHTS}
    chip = 2 * lax.axis_index("x") + lax.axis_index("y")

    shards = [cast_bf16("cast_" + n, _as2d(wts[n])) for n in BIG]
    small_sh = jnp.concatenate([meta, hgrn_norm_g, jnp.zeros((15, meta.shape[1]), F32)], axis=0)
    full = gather_over_chips("gather_weights", shards + [small_sh])
    fw = dict(zip(BIG, full[:len(BIG)]))
    sm = full[len(BIG)]
    w = {n: wts[n] for n in SMALL}
    w['meta'] = sm[:, :N_META].transpose(1, 0, 2).reshape(N_META, D_MODEL)
    w['hgrn_norm_g'] = sm[:, N_META:N_META + 1].transpose(1, 0, 2).reshape(1, D_MODEL)
    w['w_in_ab'] = fw['w_in_ab']
    w['s5_w_glu'] = fw['s5_w_glu'].reshape(S5_WIDTH, S5_WIDTH)
    w['w_out_ab'] = fw['w_out_ab'].reshape(D_MODEL, D_MODEL)
    w['w_in_c'] = fw['w_in_c']
    w['w_out_c'] = fw['w_out_c'].reshape(D_MODEL, D_MODEL)
    up = fw['mlp_w_up'].reshape(N_CHIPS, DEPTH, D_MODEL, D_FF // N_CHIPS)
    dn = fw['mlp_w_down'].reshape(N_CHIPS, DEPTH, D_FF // N_CHIPS, D_MODEL)
    w['mlp_w_up'] = [up[:, l] for l in range(DEPTH)]
    w['mlp_w_down'] = [dn[:, l].reshape(D_FF, D_MODEL) for l in range(DEPTH)]

    loss, grad_x, g = local_step(x, loss_target, w)

    gfull = [g['w_in_ab'],
             g['s5_w_glu'].reshape(N_CHIPS, -1, S5_WIDTH),
             g['w_out_ab'].reshape(N_CHIPS, -1, D_MODEL),
             g['w_in_c'],
             g['w_out_c'].reshape(N_CHIPS, -1, D_MODEL),
             g['mlp_w_up'].reshape(N_CHIPS, DEPTH * D_MODEL, D_FF // N_CHIPS),
             g['mlp_w_down'].reshape(DEPTH, N_CHIPS, D_FF // N_CHIPS, D_MODEL).transpose(1, 0, 2, 3)
             .reshape(N_CHIPS, DEPTH * D_FF // N_CHIPS, D_MODEL)]
    mine, theirs = fold_cores("fold_grads", gfull)
    chip_sums = [sum_cores_bf16("sum_cores_" + n, a, b) for n, a, b in zip(BIG, mine, theirs)]
    recv = scatter_over_chips("scatter_grads", chip_sums)
    reduced = join_cores("join_grads", [sum_slots("sum_chips_" + n, r) for n, r in zip(BIG, recv)])

    small_shapes = [(LANES,)] + [g[n].shape for n in SMALL]
    packed = _pack([loss.reshape(-1)] + [g[n] for n in SMALL])
    red = sum_slots("sum_small", gather_over_devices("gather_small", packed))
    red = _unpack(red, small_shapes)
    loss_out = red[0][0]
    gs = dict(zip(SMALL, red[1:]))
    gs['meta'] = lax.dynamic_slice_in_dim(gs['meta'], chip * meta.shape[1], meta.shape[1], axis=1)
    gs['hgrn_norm_g'] = lax.dynamic_slice_in_dim(gs['hgrn_norm_g'].reshape(1, -1), chip * meta.shape[1],
                                                 meta.shape[1], axis=1)

    grads, deltas, new_m, new_v = {}, {}, {}, {}
    for n, r in zip(BIG, reduced):
        res = adamw("adamw_" + n, _as2d(wts[n]), [_as2d(r)], _as2d(ms[n]), _as2d(vs[n]))
        grads[n], deltas[n], new_m[n], new_v[n] = [r.reshape(wts[n].shape) for r in res]
    shapes = [wts[n].shape for n in SMALL]
    res = adamw("adamw_small", _pack([wts[n] for n in SMALL]), [_pack([gs[n] for n in SMALL])],
                _pack([ms[n] for n in SMALL]), _pack([vs[n] for n in SMALL]))
    for d, r in zip((grads, deltas, new_m, new_v), res):
        d.update(zip(SMALL, _unpack(r, shapes)))
    return (loss_out, grad_x, *[grads[n] for n in WEIGHTS], *[deltas[n] for n in WEIGHTS],
            *[new_m[n] for n in WEIGHTS], *[new_v[n] for n in WEIGHTS])
```

```python
import functools
import math

import jax
import jax.numpy as jnp
from jax import lax
from jax.experimental import pallas as pl
from jax.experimental.pallas import tpu as pltpu

F32 = jnp.float32
BF16 = jnp.bfloat16

D_MODEL = 1024
N_META = 16
DEPTH = 2
S5_WIDTH = 512
S5_GROUP = 16
S5_GROUPS = 32
S5_STATE = 64
S5_CHUNK = 16
SB_WIDTH = 512
SB_BLOCK = 128
SB_DEAD = -110.0
HG_HEADS = 8
HG_DK = 128
HG_CHUNK = 64
D_FF = 4096
ALPHA = (2.0 * DEPTH) ** 0.25
LN_EPS = 1e-5
RMS_EPS = 1e-6
ADAM_LR = 0.001
ADAM_B1 = 0.9
ADAM_B2 = 0.999
ADAM_EPS = 1e-08
ADAM_WD = 0.01
ADAM_STEP = 10
N_CHIPS = 4
N_DEV = 8
LANES = 128
MESH = pl.DeviceIdType.MESH
VMEM_LIMIT = 56 * 1024 * 1024

WEIGHTS = ['meta', 'w_in_ab', 's5_lam_re', 's5_lam_im', 's5_log_dt', 's5_b_re', 's5_b_im', 's5_c_re', 's5_c_im',
           's5_d', 's5_w_glu', 's5_b_glu', 'w_out_ab', 'w_in_c', 'hgrn_gamma', 'hgrn_norm_g', 'w_out_c',
           'ln_mix_g', 'ln_mix_b', 'mlp_w_up', 'mlp_b_up', 'mlp_w_down', 'mlp_b_down', 'ln_mlp_g', 'ln_mlp_b']
BIG = ['w_in_ab', 's5_w_glu', 'w_out_ab', 'w_in_c', 'w_out_c', 'mlp_w_up', 'mlp_w_down']
SHARDED_SMALL = ['meta', 'hgrn_norm_g']
SMALL = [n for n in WEIGHTS if n not in BIG]


def _params(sem=None):
    return pltpu.CompilerParams(dimension_semantics=sem, vmem_limit_bytes=VMEM_LIMIT)


def _pick(n, cands):
    for c in cands:
        if n % c == 0:
            return c
    return n


def _dot(a, b, ca, cb):
    return lax.dot_general(a, b, (((ca,), (cb,)), ((), ())), preferred_element_type=F32)


def _split3(x):
    hi = x.astype(BF16)
    r = x - hi.astype(F32)
    mid = r.astype(BF16)
    lo = (r - mid.astype(F32)).astype(BF16)
    return hi, mid, lo


def _dot_exact_rhs(x, m, cx, cm):
    hi = x.astype(BF16)
    lo = (x - hi.astype(F32)).astype(BF16)
    return _dot(hi, m, cx, cm) + _dot(lo, m, cx, cm)


def _dot3(a, b, ca, cb):
    ah = a.astype(BF16)
    al = (a - ah.astype(F32)).astype(BF16)
    bh = b.astype(BF16)
    bl = (b - bh.astype(F32)).astype(BF16)
    return _dot(ah, bh, ca, cb) + (_dot(ah, bl, ca, cb) + _dot(al, bh, ca, cb))


def rowwise(name, fn, row_ins, bc_ins, out_widths, sum_widths=(), out_dtypes=None, tm=None):
    t = row_ins[0].shape[0]
    if tm is None:
        wmax = max([a.shape[1] for a in row_ins] + list(out_widths))
        tm = _pick(t, (272, 256, 128, 64, 16, 8)) if wmax > 1024 else _pick(t, (544, 512, 256, 128, 64, 16, 8))
    nr, nb, no, ns = len(row_ins), len(bc_ins), len(out_widths), len(sum_widths)
    out_dtypes = out_dtypes or [F32] * no

    def body(*refs):
        i = pl.program_id(0)
        rows = [r[...] for r in refs[:nr]]
        bcs = [r[...] for r in refs[nr:nr + nb]]
        outs, sums = fn(i, tm, rows, bcs)
        for r, v in zip(refs[nr + nb:nr + nb + no], outs):
            r[...] = v.astype(r.dtype)
        if ns:
            srefs = refs[nr + nb + no:]

            @pl.when(i == 0)
            def _():
                for r in srefs:
                    r[...] = jnp.zeros(r.shape, r.dtype)

            for r, v in zip(srefs, sums):
                r[...] += v

    in_specs = [pl.BlockSpec((tm, a.shape[1]), lambda i: (i, 0)) for a in row_ins]
    in_specs += [pl.BlockSpec(a.shape, lambda i: (0, 0)) for a in bc_ins]
    out_specs = [pl.BlockSpec((tm, w), lambda i: (i, 0)) for w in out_widths]
    out_specs += [pl.BlockSpec((1, w), lambda i: (0, 0)) for w in sum_widths]
    out_shape = [jax.ShapeDtypeStruct((t, w), dt) for w, dt in zip(out_widths, out_dtypes)]
    out_shape += [jax.ShapeDtypeStruct((1, w), F32) for w in sum_widths]
    res = pl.pallas_call(body, name=name, grid=(t // tm,), in_specs=in_specs, out_specs=out_specs,
                         out_shape=out_shape, compiler_params=_params(("arbitrary",)))(*row_ins, *bc_ins)
    return res


def _colsum(v):
    return jnp.sum(v, axis=0, keepdims=True)


def matmul(name, a, b, mode, *, bias=None, act=None, add=None, add_scale=1.0, out_dtype=F32,
           out_slots=0, tm=None, tn=None, tk=None):
    slotted = b.ndim == 3
    if mode == 'nn':
        m, k = a.shape
        n = b.shape[-1] * (b.shape[0] if slotted else 1)
    elif mode == 'nt':
        m, k = a.shape
        n = b.shape[-2]
    else:
        k, m = a.shape
        n = b.shape[-1]
    ns = b.shape[-1] if slotted else None
    if mode == 'tn':
        tm = tm or _pick(m, (512, 256, 128))
        tn = tn or _pick(n if not out_slots else n // out_slots, (1024, 512, 256, 128))
        tk = tk or _pick(k, (1088, 1024, 768, 512, 384, 256, 128))
    else:
        tm = tm or _pick(m, (1088, 1024, 768, 512, 384, 256, 128))
        tn = tn or _pick(n if not (slotted and mode == 'nn') else ns, (512, 256, 128))
        tk = tk or _pick(k if not (slotted and mode == 'nt') else ns, (1024, 512, 256, 128))
    gm, gn, gk = m // tm, n // tn, k // tk

    if mode == 'nn':
        a_spec = pl.BlockSpec((tm, tk), lambda i, j, kk: (i, kk))
        if slotted:
            per = ns // tn
            b_spec = pl.BlockSpec((None, tk, tn), lambda i, j, kk: (j // per, kk, j % per))
        else:
            b_spec = pl.BlockSpec((tk, tn), lambda i, j, kk: (kk, j))
        ca, cb = 1, 0
    elif mode == 'nt':
        a_spec = pl.BlockSpec((tm, tk), lambda i, j, kk: (i, kk))
        if slotted:
            per = ns // tk
            b_spec = pl.BlockSpec((None, tn, tk), lambda i, j, kk: (kk // per, j, kk % per))
        else:
            b_spec = pl.BlockSpec((tn, tk), lambda i, j, kk: (j, kk))
        ca, cb = 1, 1
    else:
        a_spec = pl.BlockSpec((tk, tm), lambda i, j, kk: (kk, i))
        b_spec = pl.BlockSpec((tk, tn), lambda i, j, kk: (kk, j))
        ca, cb = 0, 0
    in_specs = [a_spec, b_spec]
    args = [a, b]
    if bias is not None:
        in_specs.append(pl.BlockSpec((1, tn), lambda i, j, kk: (0, j)))
        args.append(bias)
    if add is not None:
        in_specs.append(pl.BlockSpec((tm, tn), lambda i, j, kk: (i, j)))
        args.append(add)
    if out_slots:
        per_o = (n // out_slots) // tn
        out_spec = pl.BlockSpec((None, tm, tn), lambda i, j, kk: (j // per_o, i, j % per_o))
        out_shape = jax.ShapeDtypeStruct((out_slots, m, n // out_slots), out_dtype)
    else:
        out_spec = pl.BlockSpec((tm, tn), lambda i, j, kk: (i, j))
        out_shape = jax.ShapeDtypeStruct((m, n), out_dtype)

    def body(*refs):
        a_ref, b_ref = refs[0], refs[1]
        pos = 2
        bias_ref = add_ref = None
        if bias is not None:
            bias_ref = refs[pos]
            pos += 1
        if add is not None:
            add_ref = refs[pos]
            pos += 1
        o_ref = refs[pos]
        acc_ref = refs[pos + 1] if gk > 1 else None
        p = _dot(a_ref[...].astype(BF16), b_ref[...].astype(BF16), ca, cb)

        def finish(r):
            if bias_ref is not None:
                r = r + bias_ref[...]
            if act == 'relu2':
                r = jnp.square(jnp.maximum(r, 0.0))
            if add_ref is not None:
                r = r + add_scale * add_ref[...]
            o_ref[...] = r.astype(o_ref.dtype)

        if gk == 1:
            finish(p)
        else:
            kk = pl.program_id(2)

            @pl.when(kk == 0)
            def _():
                acc_ref[...] = p

            @pl.when(kk > 0)
            def _():
                acc_ref[...] += p

            @pl.when(kk == gk - 1)
            def _():
                finish(acc_ref[...])

    scratch = [pltpu.VMEM((tm, tn), F32)] if gk > 1 else []
    return pl.pallas_call(body, name=name, grid=(gm, gn, gk), in_specs=in_specs, out_specs=out_spec,
                          out_shape=out_shape, scratch_shapes=scratch,
                          compiler_params=_params(("parallel", "parallel", "arbitrary")))(*args)


def ln_fwd(name, h, mix, g, b):
    def fn(i, tm, rows, bcs):
        pre = ALPHA * rows[0] + rows[1]
        mu = jnp.mean(pre, axis=1, keepdims=True)
        xc = pre - mu
        var = jnp.mean(xc * xc, axis=1, keepdims=True)
        rstd = lax.rsqrt(var + LN_EPS)
        xhat = xc * rstd
        return (xhat * bcs[0] + bcs[1], xhat, rstd), ()
    return rowwise(name, fn, [h, mix], [g, b], [D_MODEL, D_MODEL, 1])


def ln_bwd(name, dy, xhat, rstd, g):
    def fn(i, tm, rows, bcs):
        dy_, xh, rs = rows
        dyg = dy_ * bcs[0]
        m1 = jnp.mean(dyg, axis=1, keepdims=True)
        m2 = jnp.mean(dyg * xh, axis=1, keepdims=True)
        dpre = rs * (dyg - m1 - xh * m2)
        return (dpre,), (_colsum(dy_ * xh), _colsum(dy_), _colsum(dpre))
    return rowwise(name, fn, [dy, xhat, rstd], [g], [D_MODEL], [D_MODEL] * 3)


def relu2_bwd(name, dhid, hid):
    def fn(i, tm, rows, bcs):
        dpre = rows[0] * (2.0 * jnp.sqrt(rows[1]))
        return (dpre,), (_colsum(dpre),)
    return rowwise(name, fn, [dhid, hid], [], [D_FF], [D_FF])


_GELU_C = math.sqrt(2.0 / math.pi)


def gelu_fwd(name, x):
    def fn(i, tm, rows, bcs):
        v = rows[0]
        u = _GELU_C * (v + 0.044715 * (v * v * v))
        return (0.5 * v * (1.0 + jnp.tanh(u)),), ()
    return rowwise(name, fn, [x], [], [x.shape[1]])[0]


def gelu_bwd(name, dy, x):
    def fn(i, tm, rows, bcs):
        v = rows[1]
        th = jnp.tanh(_GELU_C * (v + 0.044715 * (v * v * v)))
        dg = 0.5 * (1.0 + th) + 0.5 * v * (1.0 - th * th) * (_GELU_C * (1.0 + 3.0 * 0.044715 * v * v))
        return (rows[0] * dg,), ()
    return rowwise(name, fn, [dy, x], [], [x.shape[1]])[0]


def glu_gate(name, y, gp):
    def fn(i, tm, rows, bcs):
        return (rows[0] * jax.nn.sigmoid(rows[1]),), ()
    return rowwise(name, fn, [y, gp], [], [y.shape[1]])[0]


def glu_gate_bwd(name, da, y, gp):
    def fn(i, tm, rows, bcs):
        s = jax.nn.sigmoid(rows[2])
        dgp = rows[0] * rows[1] * s * (1.0 - s)
        return (dgp, rows[0] * s), (_colsum(dgp),)
    w = y.shape[1]
    return rowwise(name, fn, [da, y, gp], [], [w, w], [w])


def loss_head(name, h, tgt, lp, seq):
    def fn(i, tm, rows, bcs):
        pos = (i * tm + lax.broadcasted_iota(jnp.int32, (tm, 1), 0)) % lp
        valid = jnp.logical_and(pos >= N_META, pos < N_META + seq)
        err = jnp.where(valid, rows[0] - rows[1], 0.0)
        part = 0.5 * jnp.sum(jnp.mean(err * err, axis=1, keepdims=True), axis=0, keepdims=True)
        return (err * (1.0 / D_MODEL),), (jnp.broadcast_to(part, (1, LANES)),)
    return rowwise(name, fn, [h, tgt], [], [D_MODEL], [LANES])


def adamw(name, w, gs, m, v):
    ng = len(gs)

    def fn(i, tm, rows, bcs):
        w_ = rows[0]
        g = rows[1] if ng == 1 else rows[1] + rows[2]
        m_, v_ = rows[1 + ng], rows[2 + ng]
        m_ = ADAM_B1 * m_ + (1.0 - ADAM_B1) * g
        v_ = ADAM_B2 * v_ + (1.0 - ADAM_B2) * jnp.square(g)
        m_hat = m_ / (1.0 - ADAM_B1 ** ADAM_STEP)
        v_hat = v_ / (1.0 - ADAM_B2 ** ADAM_STEP)
        delta = -ADAM_LR * (m_hat / (jnp.sqrt(v_hat) + ADAM_EPS) + ADAM_WD * w_)
        return (g, delta, m_, v_), ()
    wd = w.shape[1]
    return rowwise(name, fn, [w] + list(gs) + [m, v], [], [wd] * 4)


def s5_matrices(lam_re, lam_im, log_dt, b_re, b_im, c_re, c_im, d):
    c = S5_CHUNK
    dt = jnp.exp(log_dt)[:, None]
    tau = jnp.arange(c + 1, dtype=F32)[:, None, None]
    mag = jnp.exp(tau * (lam_re * dt)[None])
    ang = tau * (lam_im * dt)[None]
    pw_re, pw_im = mag * jnp.cos(ang), mag * jnp.sin(ang)
    nr, ni = pw_re[1] - 1.0, pw_im[1]
    den = lam_re * lam_re + lam_im * lam_im
    cf_re = (nr * lam_re + ni * lam_im) / den
    cf_im = (ni * lam_re - nr * lam_im) / den
    bb_re = cf_re[:, :, None] * b_re - cf_im[:, :, None] * b_im
    bb_im = cf_re[:, :, None] * b_im + cf_im[:, :, None] * b_re
    cp_re = c_re[None] * pw_re[:, :, None, :] - c_im[None] * pw_im[:, :, None, :]
    cp_im = c_re[None] * pw_im[:, :, None, :] + c_im[None] * pw_re[:, :, None, :]
    hp = lax.Precision.HIGHEST
    kern = (jnp.einsum('tghp,gpk->tghk', cp_re[:c], bb_re, precision=hp)
            - jnp.einsum('tghp,gpk->tghk', cp_im[:c], bb_im, precision=hp))
    ii = jnp.arange(c)
    sel = (ii[None, :, None] - ii[None, None, :] == ii[:, None, None]).astype(F32)
    a1 = jnp.einsum('tghk,tij->gjkih', kern, sel, precision=hp)
    a1 = a1 + jnp.einsum('ij,kh,gh->gjkih', jnp.eye(c, dtype=F32), jnp.eye(S5_GROUP, dtype=F32), d)
    a1 = a1.reshape(S5_GROUPS, c * S5_GROUP, c * S5_GROUP)
    a2 = jnp.concatenate([cp_re[1:], -cp_im[1:]], axis=-1)
    a2 = a2.transpose(1, 3, 0, 2).reshape(S5_GROUPS, 2 * S5_STATE, c * S5_GROUP)
    rv_re, rv_im = pw_re[:c][::-1], pw_im[:c][::-1]
    x_re = rv_re[:, :, :, None] * bb_re[None] - rv_im[:, :, :, None] * bb_im[None]
    x_im = rv_re[:, :, :, None] * bb_im[None] + rv_im[:, :, :, None] * bb_re[None]
    a3 = jnp.concatenate([x_re, x_im], axis=2)
    a3 = a3.transpose(1, 0, 3, 2).reshape(S5_GROUPS, c * S5_GROUP, 2 * S5_STATE)
    are = jnp.concatenate([pw_re[c], pw_re[c]], axis=-1)[:, None, :]
    aim = jnp.concatenate([-pw_im[c], pw_im[c]], axis=-1)[:, None, :]
    return a1, a2, a3, are, aim


S5_LEVELS = 8


def s5_scan_powers(lam_re, lam_im, log_dt):
    dt = jnp.exp(log_dt)[:, None]
    e = (S5_CHUNK * 2.0 ** jnp.arange(S5_LEVELS, dtype=F32))[:, None, None]
    mag = jnp.exp(e * (lam_re * dt)[None])
    ang = e * (lam_im * dt)[None]
    pr, pi = mag * jnp.cos(ang), mag * jnp.sin(ang)
    pre = jnp.concatenate([pr, pr], axis=-1).transpose(1, 0, 2)
    pim = jnp.concatenate([-pi, pi], axis=-1).transpose(1, 0, 2)
    return pre, pim


def _s5_scan(x, pre, pim, reverse):
    n = x.shape[0]
    rowi = lax.broadcasted_iota(jnp.int32, (n, 1), 0)
    t = x
    for k in range(S5_LEVELS):
        shift = 2 ** k
        if shift >= n:
            break
        p_re, p_im = pre[k:k + 1, :], pim[k:k + 1, :]
        if reverse:
            far = jnp.where(rowi < n - shift, pltpu.roll(t, n - shift, 0), 0.0)
            t = t + (p_re * far + pltpu.roll(p_im * far, S5_STATE, 1))
        else:
            far = jnp.where(rowi >= shift, pltpu.roll(t, shift, 0), 0.0)
            t = t + (p_re * far + p_im * pltpu.roll(far, S5_STATE, 1))
    return t


def s5_fwd(ug, a1, a2, a3, pre, pim, nseq):
    g, nc, cw = ug.shape
    per = nc // nseq
    sw = 2 * S5_STATE
    assert per <= 2 ** S5_LEVELS

    def body(u_ref, a1_ref, a2_ref, a3_ref, pre_ref, pim_ref, y_ref, s_ref):
        u = u_ref[...]
        x = _dot3(u, a3_ref[...], 1, 0)
        first = lax.broadcasted_iota(jnp.int32, (per, 1), 0) == 0
        for b in range(nseq):
            t = _s5_scan(x[b * per:(b + 1) * per], pre_ref[...], pim_ref[...], False)
            s_ref[pl.ds(b * per, per), :] = jnp.where(first, 0.0, pltpu.roll(t, 1, 0))
        y_ref[...] = _dot3(u, a1_ref[...], 1, 0) + _dot3(s_ref[...], a2_ref[...], 1, 0)

    def spec(shape):
        return pl.BlockSpec((None,) + shape, lambda i: (i, 0, 0))
    lv = (S5_LEVELS, sw)
    return pl.pallas_call(
        body, name="s5_fwd", grid=(g,),
        in_specs=[spec((nc, cw)), spec((cw, cw)), spec((sw, cw)), spec((cw, sw)), spec(lv), spec(lv)],
        out_specs=[spec((nc, cw)), spec((nc, sw))],
        out_shape=[jax.ShapeDtypeStruct((g, nc, cw), F32), jax.ShapeDtypeStruct((g, nc, sw), F32)],
        compiler_params=_params(("parallel",)))(ug, a1, a2, a3, pre, pim)


def s5_bwd(dyg, ug, sst, a1, a2, a3, pre, pim, nseq):
    g, nc, cw = ug.shape
    per = nc // nseq
    sw = 2 * S5_STATE

    def body(dy_ref, u_ref, s_ref, a1_ref, a2_ref, a3_ref, pre_ref, pim_ref,
             du_ref, da1_ref, da2_ref, da3_ref, dare_ref, daim_ref, dx_ref):
        dy = dy_ref[...]
        u = u_ref[...]
        gd = _dot3(dy, a2_ref[...], 1, 1)
        s_all = s_ref[...]
        da2_ref[...] = _dot3(s_all, dy, 0, 0)
        last = lax.broadcasted_iota(jnp.int32, (per, 1), 0) == per - 1
        for b in range(nseq):
            gs = _s5_scan(gd[b * per:(b + 1) * per], pre_ref[...], pim_ref[...], True)
            dx_ref[pl.ds(b * per, per), :] = jnp.where(last, 0.0, pltpu.roll(gs, per - 1, 0))
        dx = dx_ref[...]
        dare_ref[...] = _colsum(dx * s_all)
        daim_ref[...] = _colsum(dx * pltpu.roll(s_all, S5_STATE, 1))
        du_ref[...] = _dot3(dy, a1_ref[...], 1, 1) + _dot3(dx, a3_ref[...], 1, 1)
        da1_ref[...] = _dot3(u, dy, 0, 0)
        da3_ref[...] = _dot3(u, dx, 0, 0)

    def spec(shape):
        return pl.BlockSpec((None,) + shape, lambda i: (i, 0, 0))
    sds = jax.ShapeDtypeStruct
    return pl.pallas_call(
        body, name="s5_bwd", grid=(g,),
        in_specs=[spec((nc, cw)), spec((nc, cw)), spec((nc, sw)), spec((cw, cw)), spec((sw, cw)), spec((cw, sw)),
                  spec((S5_LEVELS, sw)), spec((S5_LEVELS, sw))],
        out_specs=[spec((nc, cw)), spec((cw, cw)), spec((sw, cw)), spec((cw, sw)), spec((1, sw)), spec((1, sw))],
        out_shape=[sds((g, nc, cw), F32), sds((g, cw, cw), F32), sds((g, sw, cw), F32), sds((g, cw, sw), F32),
                   sds((g, 1, sw), F32), sds((g, 1, sw), F32)],
        scratch_shapes=[pltpu.VMEM((nc, sw), F32)],
        compiler_params=_params(("parallel",)))(dyg, ug, sst, a1, a2, a3, pre, pim)


def _to_groups(u):
    t = u.shape[0]
    nc = t // S5_CHUNK
    return u.reshape(nc, S5_CHUNK, S5_GROUPS, S5_GROUP).transpose(2, 0, 1, 3).reshape(S5_GROUPS, nc, -1)


def _from_groups(yg):
    g, nc, _ = yg.shape
    return yg.reshape(g, nc, S5_CHUNK, S5_GROUP).transpose(1, 2, 0, 3).reshape(nc * S5_CHUNK, g * S5_GROUP)


def _sb_masks():
    row = lax.broadcasted_iota(jnp.int32, (SB_BLOCK, SB_BLOCK), 0)
    col = lax.broadcasted_iota(jnp.int32, (SB_BLOCK, SB_BLOCK), 1)
    lane = lax.broadcasted_iota(jnp.int32, (1, LANES), 1)
    return row, col, [lane < 64, lane >= 64]


def _sb_weights(z, vis, later_of, after):
    sp = jnp.maximum(z, 0.0) + jnp.log1p(jnp.exp(-jnp.abs(z)))
    lk = jnp.where(vis, -sp, 0.0)
    rs = jnp.sum(lk, axis=1, keepdims=True)
    later = _dot_exact_rhs(lk, after, 1, 0) + later_of(rs)
    lb = z - sp
    w = jnp.where(vis, jnp.exp(lb + later), 0.0)
    return w, rs, lb


def attn_fwd(proj, nseq, lp):
    nqb = lp // SB_BLOCK
    t = proj.shape[0]

    def body(q_ref, k_ref, v_ref, o_ref, tot_ref, first_ref):
        row, col, hmask = _sb_masks()
        after = (row > col).astype(BF16)
        tri = col < row
        lane8 = lax.broadcasted_iota(jnp.int32, (8, LANES), 1)

        def qloop(qi, firsts):
            q0 = pl.multiple_of(qi * SB_BLOCK, SB_BLOCK)
            q = q_ref[pl.ds(q0, SB_BLOCK), :] * 0.125
            qm = [jnp.where(hm, q, 0.0).astype(BF16) for hm in hmask]

            def alive(carry):
                return jnp.logical_and(carry[0] <= qi, carry[1] > 0)

            def kstep(carry):
                s, _, acc, lat0, lat1 = carry
                lats = [lat0, lat1]
                kj = qi - s
                k0 = pl.multiple_of(kj * SB_BLOCK, SB_BLOCK)
                k = k_ref[pl.ds(k0, SB_BLOCK), :].astype(BF16)
                v = v_ref[pl.ds(k0, SB_BLOCK), :]
                vis = jnp.logical_or(kj < qi, tri)
                for hd in range(2):
                    z = _dot(qm[hd], k, 1, 1)
                    w, rs, _ = _sb_weights(z, vis, functools.partial(lambda rs, lat: lat, lat=lats[hd]), after)
                    acc = acc + _dot(w.astype(BF16), jnp.where(hmask[hd], v, 0.0).astype(BF16), 1, 0)
                    lats[hd] = lats[hd] + rs
                go = (jnp.max(jnp.maximum(lats[0], lats[1])) > SB_DEAD).astype(jnp.int32)
                return s + 1, go, acc, lats[0], lats[1]

            z1 = jnp.zeros((SB_BLOCK, 1), F32)
            s, _, acc, lat0, lat1 = lax.while_loop(
                alive, kstep, (jnp.int32(0), jnp.int32(1), jnp.zeros((SB_BLOCK, LANES), F32), z1, z1))
            o_ref[pl.ds(q0, SB_BLOCK), :] = acc
            tot_ref[pl.ds(q0, SB_BLOCK), :] = jnp.where(hmask[0], lat0, lat1)
            return jnp.where(lane8 == qi, (qi - s + 1).astype(F32), firsts)

        first_ref[...] = lax.fori_loop(0, nqb, qloop, jnp.zeros((8, LANES), F32))

    nhp = SB_WIDTH // LANES
    qoff, koff, voff = S5_WIDTH // LANES, (S5_WIDTH + SB_WIDTH) // LANES, (S5_WIDTH + 2 * SB_WIDTH) // LANES
    return pl.pallas_call(
        body, name="attn_fwd", grid=(nseq, nhp),
        in_specs=[pl.BlockSpec((lp, LANES), lambda b, h: (b, qoff + h)),
                  pl.BlockSpec((lp, LANES), lambda b, h: (b, koff + h)),
                  pl.BlockSpec((lp, LANES), lambda b, h: (b, voff + h))],
        out_specs=[pl.BlockSpec((lp, LANES), lambda b, h: (b, h))] * 2
        + [pl.BlockSpec((None, None, 8, LANES), lambda b, h: (b, h, 0, 0))],
        out_shape=[jax.ShapeDtypeStruct((t, SB_WIDTH), F32)] * 2
        + [jax.ShapeDtypeStruct((nseq, nhp, 8, LANES), F32)],
        compiler_params=_params(("parallel", "parallel")))(proj, proj, proj)


def attn_bwd(proj, tot, first, do, nseq, lp):
    nqb = lp // SB_BLOCK
    t = proj.shape[0]

    def body(q_ref, k_ref, v_ref, tot_ref, first_ref, do_ref, dq_ref, dk_ref, dv_ref):
        row, col, hmask = _sb_masks()
        after = (row > col).astype(BF16)
        before = (row < col).astype(BF16)
        tri = col < row
        lane8 = lax.broadcasted_iota(jnp.int32, (8, LANES), 1)
        firsts = first_ref[...]
        dk_ref[...] = jnp.zeros(dk_ref.shape, F32)
        dv_ref[...] = jnp.zeros(dv_ref.shape, F32)

        def qloop(qi, _):
            first = jnp.max(jnp.where(lane8 == qi, firsts, 0.0)).astype(jnp.int32)
            first = jnp.clip(first, 0, qi)
            q0 = pl.multiple_of(qi * SB_BLOCK, SB_BLOCK)
            q = q_ref[pl.ds(q0, SB_BLOCK), :] * 0.125
            do_ = do_ref[pl.ds(q0, SB_BLOCK), :]
            tot_ = tot_ref[pl.ds(q0, SB_BLOCK), :]
            qm = [jnp.where(hm, q, 0.0).astype(BF16) for hm in hmask]
            dom = [jnp.where(hm, do_, 0.0).astype(BF16) for hm in hmask]
            tot = [jnp.max(jnp.where(hm, tot_, -jnp.inf), axis=1, keepdims=True) for hm in hmask]

            def kloop(kj, carry):
                dq = carry[0]
                pre = [carry[1], carry[2]]
                gpre = [carry[3], carry[4]]
                k0 = pl.multiple_of(kj * SB_BLOCK, SB_BLOCK)
                kf = k_ref[pl.ds(k0, SB_BLOCK), :]
                k = kf.astype(BF16)
                v = v_ref[pl.ds(k0, SB_BLOCK), :].astype(BF16)
                vis = jnp.logical_or(kj < qi, tri)
                dk_acc = jnp.zeros((SB_BLOCK, LANES), F32)
                dv_acc = jnp.zeros((SB_BLOCK, LANES), F32)
                for hd in range(2):
                    z = _dot(qm[hd], k, 1, 1)
                    later_of = functools.partial(lambda rs, t_, p_: t_ - p_ - rs, t_=tot[hd], p_=pre[hd])
                    w, rs, lb = _sb_weights(z, vis, later_of, after)
                    dw = _dot(dom[hd], v, 1, 1)
                    g = dw * w
                    dlk = _dot_exact_rhs(g, before, 1, 0) + gpre[hd]
                    sig = jnp.exp(lb)
                    dz = jnp.where(vis, g * (1.0 - sig) - dlk * sig, 0.0).astype(BF16)
                    dq = dq + _dot(dz, jnp.where(hmask[hd], kf, 0.0).astype(BF16), 1, 0)
                    dk_acc = dk_acc + _dot(dz, qm[hd], 0, 0)
                    dv_acc = dv_acc + _dot(w.astype(BF16), dom[hd], 0, 0)
                    pre[hd] = pre[hd] + rs
                    gpre[hd] = gpre[hd] + jnp.sum(g, axis=1, keepdims=True)
                dk_ref[pl.ds(k0, SB_BLOCK), :] += dk_acc
                dv_ref[pl.ds(k0, SB_BLOCK), :] += dv_acc
                return dq, pre[0], pre[1], gpre[0], gpre[1]

            z1 = jnp.zeros((SB_BLOCK, 1), F32)
            res = lax.fori_loop(first, qi + 1, kloop, (jnp.zeros((SB_BLOCK, LANES), F32), z1, z1, z1, z1))
            dq_ref[pl.ds(q0, SB_BLOCK), :] = res[0] * 0.125
            return 0

        lax.fori_loop(0, nqb, qloop, 0)

    nhp = SB_WIDTH // LANES
    qoff, koff, voff = S5_WIDTH // LANES, (S5_WIDTH + SB_WIDTH) // LANES, (S5_WIDTH + 2 * SB_WIDTH) // LANES
    blk = (lp, LANES)
    return pl.pallas_call(
        body, name="attn_bwd", grid=(nseq, nhp),
        in_specs=[pl.BlockSpec(blk, lambda b, h: (b, qoff + h)),
                  pl.BlockSpec(blk, lambda b, h: (b, koff + h)),
                  pl.BlockSpec(blk, lambda b, h: (b, voff + h)),
                  pl.BlockSpec(blk, lambda b, h: (b, h)),
                  pl.BlockSpec((None, None, 8, LANES), lambda b, h: (b, h, 0, 0)),
                  pl.BlockSpec(blk, lambda b, h: (b, h))],
        out_specs=[pl.BlockSpec(blk, lambda b, h: (b, h))] * 3,
        out_shape=[jax.ShapeDtypeStruct((t, SB_WIDTH), F32)] * 3,
        compiler_params=_params(("parallel", "parallel")))(proj, proj, proj, tot, first, do)


def _hg_gates(fc, lb):
    sg = jax.nn.sigmoid(fc)
    f = lb + (1.0 - lb) * sg
    return sg, f


def _hg_decay(f, incl):
    bcum = _dot_exact_lhs(incl, jnp.log(f))
    return bcum, bcum[HG_CHUNK - 1:HG_CHUNK, :]


def _dot_exact_lhs(m, x):
    hi, mid, lo = _split3(x)
    return _dot(m, hi, 1, 0) + (_dot(m, mid, 1, 0) + _dot(m, lo, 1, 0))


def hgrn_fwd(proj, lb, ng, nseq, lp):
    nch = lp // HG_CHUNK
    t = proj.shape[0]
    c = HG_CHUNK

    def body(q_ref, f_ref, v_ref, g_ref, lb_ref, ng_ref, o_ref, og_ref, st_ref):
        row = lax.broadcasted_iota(jnp.int32, (c, c), 0)
        col = lax.broadcasted_iota(jnp.int32, (c, c), 1)
        causal = col <= row
        incl = causal.astype(BF16)
        lbv, ngv = lb_ref[...], ng_ref[...]

        def step(ci, st):
            r0 = pl.multiple_of(ci * c, c)
            rows = pl.ds(r0, c)
            st_ref[ci] = st
            _, f = _hg_gates(f_ref[rows, :], lbv)
            bcum, blast = _hg_decay(f, incl)
            kk = 1.0 - f
            vc = v_ref[rows, :].astype(BF16)
            qd = (q_ref[rows, :] * jnp.exp(bcum)).astype(BF16)
            kd = (kk * jnp.exp(-bcum)).astype(BF16)
            a = jnp.where(causal, _dot(qd, kd, 1, 1), 0.0)
            o = _dot(a.astype(BF16), vc, 1, 0) + _dot(qd, st.astype(BF16), 1, 1)
            kend = (kk * jnp.exp(blast - bcum)).astype(BF16)
            st = st * jnp.exp(blast) + _dot(vc, kend, 0, 0)
            o_ref[rows, :] = o
            r = lax.rsqrt(jnp.mean(o * o, axis=1, keepdims=True) + RMS_EPS)
            gc = g_ref[rows, :]
            og_ref[rows, :] = o * r * ngv * (gc * jax.nn.sigmoid(gc))
            return st

        lax.fori_loop(0, nch, step, jnp.zeros((HG_DK, HG_DK), F32))

    blk = (lp, LANES)
    h = HG_HEADS
    return pl.pallas_call(
        body, name="hgrn_fwd", grid=(nseq, h),
        in_specs=[pl.BlockSpec(blk, lambda b, hh: (b, hh)),
                  pl.BlockSpec(blk, lambda b, hh: (b, h + hh)),
                  pl.BlockSpec(blk, lambda b, hh: (b, 2 * h + hh)),
                  pl.BlockSpec(blk, lambda b, hh: (b, 3 * h + hh)),
                  pl.BlockSpec((1, LANES), lambda b, hh: (0, hh)),
                  pl.BlockSpec((1, LANES), lambda b, hh: (0, hh))],
        out_specs=[pl.BlockSpec(blk, lambda b, hh: (b, hh)),
                   pl.BlockSpec(blk, lambda b, hh: (b, hh)),
                   pl.BlockSpec((None, None, nch, HG_DK, HG_DK), lambda b, hh: (b, hh, 0, 0, 0))],
        out_shape=[jax.ShapeDtypeStruct((t, D_MODEL), F32), jax.ShapeDtypeStruct((t, D_MODEL), F32),
                   jax.ShapeDtypeStruct((nseq, h, nch, HG_DK, HG_DK), F32)],
        compiler_params=_params(("parallel", "parallel")))(proj, proj, proj, proj, lb, ng)


def hgrn_bwd(proj, o, dog, states, lb, ng, nseq, lp):
    nch = lp // HG_CHUNK
    t = proj.shape[0]
    c = HG_CHUNK

    def body(q_ref, f_ref, v_ref, g_ref, o_ref, dog_ref, st_ref, lb_ref, ng_ref,
             dq_ref, df_ref, dv_ref, dg_ref, dlb_ref, dng_ref):
        row = lax.broadcasted_iota(jnp.int32, (c, c), 0)
        col = lax.broadcasted_iota(jnp.int32, (c, c), 1)
        causal = col <= row
        incl = causal.astype(BF16)
        from_here = (col >= row).astype(BF16)
        last = lax.broadcasted_iota(jnp.int32, (c, 1), 0) == c - 1
        lbv, ngv = lb_ref[...], ng_ref[...]

        def step(s, carry):
            dst, dlb, dng = carry
            ci = nch - 1 - s
            r0 = pl.multiple_of(ci * c, c)
            rows = pl.ds(r0, c)
            oc, gc, dogc = o_ref[rows, :], g_ref[rows, :], dog_ref[rows, :]
            r = lax.rsqrt(jnp.mean(oc * oc, axis=1, keepdims=True) + RMS_EPS)
            sgg = jax.nn.sigmoid(gc)
            silu = gc * sgg
            dg_ref[rows, :] = dogc * (oc * r * ngv) * (sgg * (1.0 + gc * (1.0 - sgg)))
            don = dogc * silu
            dng = dng + _colsum(don * oc * r)
            tt = don * ngv
            do = r * (tt - oc * (r * r) * jnp.mean(tt * oc, axis=1, keepdims=True))
            st = st_ref[ci]
            fc = f_ref[rows, :]
            sg, f = _hg_gates(fc, lbv)
            bcum, blast = _hg_decay(f, incl)
            kk = 1.0 - f
            qc, vf = q_ref[rows, :], v_ref[rows, :]
            pdec = jnp.exp(bcum)
            ndec = jnp.exp(-bcum)
            edec = jnp.exp(blast - bcum)
            eb = jnp.exp(blast)
            qd_f, kd_f, kend_f = qc * pdec, kk * ndec, kk * edec
            qd, kd, kend = qd_f.astype(BF16), kd_f.astype(BF16), kend_f.astype(BF16)
            vc, dob, stb, dstb = vf.astype(BF16), do.astype(BF16), st.astype(BF16), dst.astype(BF16)
            a = jnp.where(causal, _dot(qd, kd, 1, 1), 0.0).astype(BF16)
            da = jnp.where(causal, _dot(dob, vc, 1, 1), 0.0).astype(BF16)
            dv_ref[rows, :] = _dot(a, dob, 0, 0) + _dot(kend, dstb, 1, 1)
            dqd = _dot(da, kd, 1, 0) + _dot(dob, stb, 1, 0)
            dkd = _dot(da, qd, 0, 0)
            dkend = _dot(vc, dstb, 1, 0)
            dq_ref[rows, :] = dqd * pdec
            dblast = _colsum(dkend * kend_f) + eb * _colsum(dst * st)
            dbcum = dqd * qd_f - dkd * kd_f - dkend * kend_f
            dbcum = dbcum + jnp.where(last, dblast, 0.0)
            dlf = _dot_exact_lhs(from_here, dbcum)
            df = dlf / f - (dkd * ndec + dkend * edec)
            df_ref[rows, :] = df * (1.0 - lbv) * sg * (1.0 - sg)
            dlb = dlb + _colsum(df * (1.0 - sg))
            dst = dst * eb + _dot(dob, qd, 0, 0)
            return dst, dlb, dng

        z = jnp.zeros((1, LANES), F32)
        _, dlb, dng = lax.fori_loop(0, nch, step, (jnp.zeros((HG_DK, HG_DK), F32), z, z))
        dlb_ref[...] = dlb
        dng_ref[...] = dng

    blk = (lp, LANES)
    h = HG_HEADS
    vec = pl.BlockSpec((1, LANES), lambda b, hh: (0, hh))
    svec = pl.BlockSpec((None, 1, LANES), lambda b, hh: (b, 0, hh))
    return pl.pallas_call(
        body, name="hgrn_bwd", grid=(nseq, h),
        in_specs=[pl.BlockSpec(blk, lambda b, hh: (b, hh)),
                  pl.BlockSpec(blk, lambda b, hh: (b, h + hh)),
                  pl.BlockSpec(blk, lambda b, hh: (b, 2 * h + hh)),
                  pl.BlockSpec(blk, lambda b, hh: (b, 3 * h + hh)),
                  pl.BlockSpec(blk, lambda b, hh: (b, hh)),
                  pl.BlockSpec(blk, lambda b, hh: (b, hh)),
                  pl.BlockSpec((None, None, nch, HG_DK, HG_DK), lambda b, hh: (b, hh, 0, 0, 0)),
                  vec, vec],
        out_specs=[pl.BlockSpec(blk, lambda b, hh: (b, hh))] * 4 + [svec, svec],
        out_shape=[jax.ShapeDtypeStruct((t, D_MODEL), F32)] * 4
        + [jax.ShapeDtypeStruct((nseq, 1, D_MODEL), F32)] * 2,
        compiler_params=_params(("parallel", "parallel")))(proj, proj, proj, proj, o, dog, states, lb, ng)


def _lower_bound(gamma):
    p = jax.nn.softmax(gamma, axis=0)
    return (jnp.cumsum(p, axis=0) - p[0])[1][None, :]


def local_step(x, tgt, w):
    nseq, seq, _ = x.shape
    lp = -(-(N_META + seq) // SB_BLOCK) * SB_BLOCK
    t = nseq * lp
    pad = lp - N_META - seq
    h0 = jnp.concatenate([jnp.broadcast_to(w['meta'][None], (nseq, N_META, D_MODEL)), x,
                          jnp.zeros((nseq, pad, D_MODEL), F32)], axis=1).reshape(t, D_MODEL)
    tgt_p = jnp.pad(tgt, ((0, 0), (N_META, pad), (0, 0))).reshape(t, D_MODEL)
    row = lambda v: v.reshape(1, -1)
    g = {}

    proj = matmul("in_ab", h0, w['w_in_ab'], 'nn')
    s5_names = ['s5_lam_re', 's5_lam_im', 's5_log_dt', 's5_b_re', 's5_b_im', 's5_c_re', 's5_c_im', 's5_d']
    mats, mats_vjp = jax.vjp(s5_matrices, *[w[n][0] for n in s5_names])
    ug = _to_groups(proj[:, :S5_WIDTH])
    s5_pows = s5_scan_powers(*[w[n][0] for n in s5_names[:3]])
    yg, sst = s5_fwd(ug, *mats[:3], *s5_pows, nseq)
    ys5 = _from_groups(yg)
    y = gelu_fwd("gelu", ys5)
    gp = matmul("glu", y, w['s5_w_glu'], 'nn', bias=w['s5_b_glu'])
    a_out = glu_gate("glu_gate", y, gp)
    b_out, sb_tot, sb_first = attn_fwd(proj, nseq, lp)
    cat = jnp.concatenate([a_out, b_out], axis=1)
    mix0 = matmul("out_ab", cat, w['w_out_ab'], 'nn')
    h1, xh1, rs1 = ln_fwd("ln_mix0", h0, mix0, row(w['ln_mix_g'][0]), row(w['ln_mix_b'][0]))
    hid0 = matmul("up0", h1, w['mlp_w_up'][0], 'nn', bias=row(w['mlp_b_up'][0]), act='relu2')
    dn0 = matmul("down0", hid0, w['mlp_w_down'][0], 'nn', bias=row(w['mlp_b_down'][0]))
    h2, xh2, rs2 = ln_fwd("ln_mlp0", h1, dn0, row(w['ln_mlp_g'][0]), row(w['ln_mlp_b'][0]))

    projc = matmul("in_c", h2, w['w_in_c'], 'nn')
    lb, lb_vjp = jax.vjp(_lower_bound, w['hgrn_gamma'])
    ng = w['hgrn_norm_g']
    o, og, states = hgrn_fwd(projc, lb, ng, nseq, lp)
    mix1 = matmul("out_c", og, w['w_out_c'], 'nn')
    h3, xh3, rs3 = ln_fwd("ln_mix1", h2, mix1, row(w['ln_mix_g'][1]), row(w['ln_mix_b'][1]))
    hid1 = matmul("up1", h3, w['mlp_w_up'][1], 'nn', bias=row(w['mlp_b_up'][1]), act='relu2')
    dn1 = matmul("down1", hid1, w['mlp_w_down'][1], 'nn', bias=row(w['mlp_b_down'][1]))
    h4, xh4, rs4 = ln_fwd("ln_mlp1", h3, dn1, row(w['ln_mlp_g'][1]), row(w['ln_mlp_b'][1]))
    dy, loss = loss_head("loss", h4, tgt_p, lp, seq)

    def mlp_bwd(l, dh_out, xh_out, rs_out, hid, h_in):
        dpre, dg_, db_, dbd = ln_bwd(f"ln_mlp{l}_b", dh_out, xh_out, rs_out, row(w['ln_mlp_g'][l]))
        dhid = matmul(f"down{l}_dx", dpre, w['mlp_w_down'][l], 'nt')
        dwd = matmul(f"down{l}_dw", hid, dpre, 'tn')
        dpu, dbu = relu2_bwd(f"relu2_{l}_b", dhid, hid)
        dh_in = matmul(f"up{l}_dx", dpu, w['mlp_w_up'][l], 'nt', add=dpre, add_scale=ALPHA)
        dwu = matmul(f"up{l}_dw", h_in, dpu, 'tn', out_slots=N_CHIPS)
        return dh_in, dict(ln_mlp_g=dg_, ln_mlp_b=db_, mlp_b_down=dbd, mlp_b_up=dbu, mlp_w_up=dwu, mlp_w_down=dwd)

    dh3, gm1 = mlp_bwd(1, dy, xh4, rs4, hid1, h3)
    dpre, dg1, db1, _ = ln_bwd("ln_mix1_b", dh3, xh3, rs3, row(w['ln_mix_g'][1]))
    g['w_out_c'] = matmul("out_c_dw", og, dpre, 'tn')
    dog = matmul("out_c_dx", dpre, w['w_out_c'], 'nt')
    dq, df, di, dgg, dlb, dng = hgrn_bwd(projc, o, dog, states, lb, ng, nseq, lp)
    dprojc = jnp.concatenate([dq, df, di, dgg], axis=1)
    dh2 = matmul("in_c_dx", dprojc, w['w_in_c'], 'nt', add=dpre, add_scale=ALPHA)
    g['w_in_c'] = matmul("in_c_dw", h2, dprojc, 'tn', out_slots=N_CHIPS)
    g['hgrn_gamma'] = lb_vjp(jnp.sum(dlb, axis=0))[0]
    g['hgrn_norm_g'] = jnp.sum(dng, axis=0)

    dh1, gm0 = mlp_bwd(0, dh2, xh2, rs2, hid0, h1)
    dpre, dg0, db0, _ = ln_bwd("ln_mix0_b", dh1, xh1, rs1, row(w['ln_mix_g'][0]))
    g['w_out_ab'] = matmul("out_ab_dw", cat, dpre, 'tn')
    dcat = matmul("out_ab_dx", dpre, w['w_out_ab'], 'nt')
    dgp, da_s, dbglu = glu_gate_bwd("glu_gate_b", dcat[:, :S5_WIDTH], y, gp)
    g['s5_w_glu'] = matmul("glu_dw", y, dgp, 'tn')
    g['s5_b_glu'] = dbglu
    dy5 = matmul("glu_dx", dgp, w['s5_w_glu'], 'nt', add=da_s)
    dys5 = gelu_bwd("gelu_b", dy5, ys5)
    dug, da1, da2, da3, dare, daim = s5_bwd(_to_groups(dys5), ug, sst, *mats[:3], *s5_pows, nseq)
    for n, v in zip(s5_names, mats_vjp((da1, da2, da3, dare, daim))):
        g[n] = v[None]
    dqa, dka, dva = attn_bwd(proj, sb_tot, sb_first, dcat[:, S5_WIDTH:], nseq, lp)
    dproj = jnp.concatenate([_from_groups(dug), dqa, dka, dva], axis=1)
    dh0 = matmul("in_ab_dx", dproj, w['w_in_ab'], 'nt', add=dpre, add_scale=ALPHA)
    g['w_in_ab'] = matmul("in_ab_dw", h0, dproj, 'tn', out_slots=N_CHIPS)

    dh0 = dh0.reshape(nseq, lp, D_MODEL)
    grad_x = dh0[:, N_META:N_META + seq]
    g['meta'] = jnp.sum(dh0[:, :N_META], axis=0)
    g['ln_mix_g'] = jnp.concatenate([dg0, dg1], axis=0)
    g['ln_mix_b'] = jnp.concatenate([db0, db1], axis=0)
    for n in gm0:
        g[n] = jnp.stack([gm0[n], gm1[n]], axis=1 if n == 'mlp_w_up' else 0)
    for n in ('ln_mlp_g', 'ln_mlp_b', 'mlp_b_down', 'mlp_b_up'):
        g[n] = g[n].reshape(DEPTH, -1)
    return loss, grad_x, g


def _exchange(name, ins, out_shapes, n_local, n_remote, build, aliased=0):
    ni, no = len(ins), len(out_shapes)

    def body(*refs):
        in_refs, out_refs = refs[:ni], refs[ni:ni + no]
        lsem, ssem, rsem = refs[ni + no:]
        me = (lax.axis_index("x"), lax.axis_index("y"), lax.axis_index("c"))
        nl = nr = 0
        for loc, rem in build(in_refs, out_refs, me):
            copies = [pltpu.make_async_copy(s, d, lsem.at[nl + i]) for i, (s, d) in enumerate(loc)]
            copies += [pltpu.make_async_remote_copy(src_ref=s, dst_ref=d, send_sem=ssem.at[nr + k],
                                                    recv_sem=rsem.at[nr + k], device_id=p, device_id_type=MESH)
                       for k, (s, d, p) in enumerate(rem)]
            nl, nr = nl + len(loc), nr + len(rem)
            for cp in copies:
                cp.start()
            for cp in copies:
                cp.wait()

    anyspec = pl.BlockSpec(memory_space=pl.ANY)
    return pl.pallas_call(
        body, name=name, in_specs=[anyspec] * ni, out_specs=[anyspec] * no, out_shape=out_shapes,
        input_output_aliases={i: i for i in range(aliased)},
        scratch_shapes=[pltpu.SemaphoreType.DMA((max(n_local, 1),)), pltpu.SemaphoreType.DMA((n_remote,)),
                        pltpu.SemaphoreType.DMA((n_remote,))])(*ins)


def _chip_peers(me):
    x, y, c = me
    return [(x ^ (m >> 1), y ^ (m & 1), c) for m in (1, 2, 3)]


def _half(ref, c, axis):
    h = ref.shape[axis] // 2
    idx = [slice(None)] * axis + [pl.ds(c * h, h)]
    return ref.at[tuple(idx)]


def gather_over_chips(name, bufs):
    def build(in_refs, out_refs, me):
        x, y, c = me
        j = 2 * x + y
        sib = (x, y, 1 - c)
        ici = [(_half(o.at[j], c, 0), _half(o.at[j], c, 0), p) for o in out_refs for p in _chip_peers(me)]
        d2d = [(_half(o.at[2 * p[0] + p[1]], c, 0), _half(o.at[2 * p[0] + p[1]], c, 0), sib)
               for o in out_refs for p in _chip_peers(me)]
        return [([], ici), ([], d2d)]
    shapes = [jax.ShapeDtypeStruct(b.shape, b.dtype) for b in bufs]
    return _exchange(name, bufs, shapes, 0, 6 * len(bufs), build, aliased=len(bufs))


def fold_cores(name, fulls):
    def build(in_refs, out_refs, me):
        x, y, c = me
        return [([], [(_half(s, 1 - c, 1), o, (x, y, 1 - c)) for s, o in zip(in_refs, out_refs)])]
    shapes = [jax.ShapeDtypeStruct((f.shape[0], f.shape[1] // 2, f.shape[2]), f.dtype) for f in fulls]
    return _exchange(name, fulls, shapes, 0, len(fulls), build)


def scatter_over_chips(name, parts):
    def build(in_refs, out_refs, me):
        j = 2 * me[0] + me[1]
        rem = [(s.at[2 * p[0] + p[1]], o.at[j], p) for s, o in zip(in_refs, out_refs) for p in _chip_peers(me)]
        return [([], rem)]
    shapes = [jax.ShapeDtypeStruct(f.shape, f.dtype) for f in parts]
    return _exchange(name, parts, shapes, 0, 3 * len(parts), build)


def join_cores(name, bufs):
    def build(in_refs, out_refs, me):
        x, y, c = me
        return [([], [(o.at[c], o.at[c], (x, y, 1 - c)) for o in out_refs])]
    shapes = [jax.ShapeDtypeStruct(b.shape, b.dtype) for b in bufs]
    return _exchange(name, bufs, shapes, 0, len(bufs), build, aliased=len(bufs))


def gather_over_devices(name, buf):
    def build(in_refs, out_refs, me):
        x, y, c = me
        i = 4 * x + 2 * y + c
        out = out_refs[0]
        rem = [(out.at[i], out.at[i], (x ^ (m >> 2), y ^ ((m >> 1) & 1), c ^ (m & 1))) for m in range(1, N_DEV)]
        return [([], rem)]
    return _exchange(name, [buf], [jax.ShapeDtypeStruct(buf.shape, buf.dtype)], 0, N_DEV - 1, build, aliased=1)[0]


def _slot_call(name, body, scalars, ins, in_maps, out_shape, out_map, blk, grid):
    gs = pltpu.PrefetchScalarGridSpec(
        num_scalar_prefetch=1, grid=grid,
        in_specs=[pl.BlockSpec((None,) + blk, m) for m in in_maps],
        out_specs=pl.BlockSpec((None,) + blk, out_map))
    return pl.pallas_call(body, name=name, grid_spec=gs, out_shape=out_shape,
                          compiler_params=_params(("arbitrary",) * len(grid)))(scalars, *ins)


def cast_into_slot(name, w2d, chip, dtype):
    r, wd = w2d.shape
    tm = _pick(r, (512, 256, 128, 64, 32, 16))

    def body(s_ref, w_ref, o_ref):
        o_ref[...] = w_ref[...].astype(o_ref.dtype)

    gs = pltpu.PrefetchScalarGridSpec(
        num_scalar_prefetch=1, grid=(r // tm,),
        in_specs=[pl.BlockSpec((tm, wd), lambda i, s: (i, 0))],
        out_specs=pl.BlockSpec((None, tm, wd), lambda i, s: (s[0], i, 0)))
    return pl.pallas_call(body, name=name, grid_spec=gs,
                          out_shape=jax.ShapeDtypeStruct((N_CHIPS, r, wd), dtype),
                          compiler_params=_params(("arbitrary",)))(chip.reshape(1), w2d)


def sum_cores(name, full, theirs, core):
    s, r, wd = full.shape
    h = r // 2
    tm = _pick(h, (512, 256, 128, 64, 32, 16))
    nt = h // tm

    def body(s_ref, a_ref, b_ref, o_ref):
        o_ref[...] = (a_ref[...] + b_ref[...]).astype(o_ref.dtype)

    return _slot_call(name, body, core.reshape(1), [full, theirs],
                      [lambda k, i, s_: (k, s_[0] * nt + i, 0), lambda k, i, s_: (k, i, 0)],
                      jax.ShapeDtypeStruct((s, h, wd), BF16), lambda k, i, s_: (k, i, 0), (tm, wd), (s, nt))


def sum_chips(name, own, recv, chip, core):
    s, r, wd = own.shape
    tm = _pick(r, (512, 256, 128, 64, 32, 16))

    def body(s_ref, a_ref, b1_ref, b2_ref, b3_ref, o_ref):
        acc = a_ref[...].astype(F32)
        for ref in (b1_ref, b2_ref, b3_ref):
            acc = acc + ref[...].astype(F32)
        o_ref[...] = acc

    maps = [lambda i, s_: (s_[0], i, 0)]
    maps += [functools.partial(lambda i, s_, m: (s_[0] ^ m, i, 0), m=m) for m in (1, 2, 3)]
    return _slot_call(name, body, jnp.stack([chip, core]), [own, recv, recv, recv], maps,
                      jax.ShapeDtypeStruct((2, r, wd), F32), lambda i, s_: (s_[1], i, 0), (tm, wd), (r // tm,))


def sum_slots(name, arr):
    s, r, wd = arr.shape
    tm = _pick(r, (512, 256, 128, 64, 32, 16, 8))

    def body(*refs):
        acc = refs[0][...].astype(F32)
        for ref in refs[1:s]:
            acc = acc + ref[...].astype(F32)
        refs[s][...] = acc

    specs = [pl.BlockSpec((None, tm, wd), functools.partial(lambda i, k: (k, i, 0), k=k)) for k in range(s)]
    return pl.pallas_call(body, name=name, grid=(r // tm,), in_specs=specs,
                          out_specs=pl.BlockSpec((tm, wd), lambda i: (i, 0)),
                          out_shape=jax.ShapeDtypeStruct((r, wd), F32),
                          compiler_params=_params(("parallel",)))(*([arr] * s))


def _pack(arrs):
    flat = jnp.concatenate([a.reshape(-1) for a in arrs])
    n = flat.shape[0]
    padded = -(-n // (8 * LANES)) * (8 * LANES)
    return jnp.pad(flat, (0, padded - n)).reshape(-1, LANES)


def _unpack(packed, shapes):
    flat = packed.reshape(-1)
    out, pos = [], 0
    for s in shapes:
        n = math.prod(s)
        out.append(flat[pos:pos + n].reshape(s))
        pos += n
    return out


def _as2d(a):
    return a.reshape(-1, a.shape[-1])


def kernel(x, meta, w_in_ab, s5_lam_re, s5_lam_im, s5_log_dt, s5_b_re, s5_b_im, s5_c_re, s5_c_im, s5_d, s5_w_glu, s5_b_glu, w_out_ab, w_in_c, hgrn_gamma, hgrn_norm_g, w_out_c, ln_mix_g, ln_mix_b, mlp_w_up, mlp_b_up, mlp_w_down, mlp_b_down, ln_mlp_g, ln_mlp_b, loss_target, m_meta, m_w_in_ab, m_s5_lam_re, m_s5_lam_im, m_s5_log_dt, m_s5_b_re, m_s5_b_im, m_s5_c_re, m_s5_c_im, m_s5_d, m_s5_w_glu, m_s5_b_glu, m_w_out_ab, m_w_in_c, m_hgrn_gamma, m_hgrn_norm_g, m_w_out_c, m_ln_mix_g, m_ln_mix_b, m_mlp_w_up, m_mlp_b_up, m_mlp_w_down, m_mlp_b_down, m_ln_mlp_g, m_ln_mlp_b, v_meta, v_w_in_ab, v_s5_lam_re, v_s5_lam_im, v_s5_log_dt, v_s5_b_re, v_s5_b_im, v_s5_c_re, v_s5_c_im, v_s5_d, v_s5_w_glu, v_s5_b_glu, v_w_out_ab, v_w_in_c, v_hgrn_gamma, v_hgrn_norm_g, v_w_out_c, v_ln_mix_g, v_ln_mix_b, v_mlp_w_up, v_mlp_b_up, v_mlp_w_down, v_mlp_b_down, v_ln_mlp_g, v_ln_mlp_b):
    given = dict(locals())
    wts = {n: given[n] for n in WEIGHTS}
    ms = {n: given['m_' + n] for n in WEIGHTS}
    vs = {n: given['v_' + n] for n in WEIGHTS}
    chip = 2 * lax.axis_index("x") + lax.axis_index("y")

    core = lax.axis_index("c")
    shards = [cast_into_slot("cast_" + n, _as2d(wts[n]), chip, BF16) for n in BIG]
    small_sh = jnp.concatenate([meta, hgrn_norm_g, jnp.zeros((15, meta.shape[1]), F32)], axis=0)
    full = gather_over_chips("gather_weights", shards + [cast_into_slot("cast_small", small_sh, chip, F32)])
    fw = dict(zip(BIG, full[:len(BIG)]))
    sm = full[len(BIG)]
    w = {n: wts[n] for n in SMALL}
    w['meta'] = sm[:, :N_META].transpose(1, 0, 2).reshape(N_META, D_MODEL)
    w['hgrn_norm_g'] = sm[:, N_META:N_META + 1].transpose(1, 0, 2).reshape(1, D_MODEL)
    w['w_in_ab'] = fw['w_in_ab']
    w['s5_w_glu'] = fw['s5_w_glu'].reshape(S5_WIDTH, S5_WIDTH)
    w['w_out_ab'] = fw['w_out_ab'].reshape(D_MODEL, D_MODEL)
    w['w_in_c'] = fw['w_in_c']
    w['w_out_c'] = fw['w_out_c'].reshape(D_MODEL, D_MODEL)
    up = fw['mlp_w_up'].reshape(N_CHIPS, DEPTH, D_MODEL, D_FF // N_CHIPS)
    dn = fw['mlp_w_down'].reshape(N_CHIPS, DEPTH, D_FF // N_CHIPS, D_MODEL)
    w['mlp_w_up'] = [up[:, l] for l in range(DEPTH)]
    w['mlp_w_down'] = [dn[:, l].reshape(D_FF, D_MODEL) for l in range(DEPTH)]

    loss, grad_x, g = local_step(x, loss_target, w)

    gfull = [g['w_in_ab'],
             g['s5_w_glu'].reshape(N_CHIPS, -1, S5_WIDTH),
             g['w_out_ab'].reshape(N_CHIPS, -1, D_MODEL),
             g['w_in_c'],
             g['w_out_c'].reshape(N_CHIPS, -1, D_MODEL),
             g['mlp_w_up'].reshape(N_CHIPS, DEPTH * D_MODEL, D_FF // N_CHIPS),
             g['mlp_w_down'].reshape(DEPTH, N_CHIPS, D_FF // N_CHIPS, D_MODEL).transpose(1, 0, 2, 3)
             .reshape(N_CHIPS, DEPTH * D_FF // N_CHIPS, D_MODEL)]
    theirs = fold_cores("fold_grads", gfull)
    chip_sums = [sum_cores("sum_cores_" + n, a, b, core) for n, a, b in zip(BIG, gfull, theirs)]
    recv = scatter_over_chips("scatter_grads", chip_sums)
    reduced = join_cores("join_grads", [sum_chips("sum_chips_" + n, a, b, chip, core)
                                        for n, a, b in zip(BIG, chip_sums, recv)])

    small_shapes = [(LANES,)] + [g[n].shape for n in SMALL]
    packed = _pack([loss.reshape(-1)] + [g[n] for n in SMALL])
    slots = lax.dynamic_update_slice(jnp.zeros((N_DEV,) + packed.shape, F32), packed[None],
                                     (2 * chip + core, 0, 0))
    red = sum_slots("sum_small", gather_over_devices("gather_small", slots))
    red = _unpack(red, small_shapes)
    loss_out = red[0][0]
    gs = dict(zip(SMALL, red[1:]))
    gs['meta'] = lax.dynamic_slice_in_dim(gs['meta'], chip * meta.shape[1], meta.shape[1], axis=1)
    gs['hgrn_norm_g'] = lax.dynamic_slice_in_dim(gs['hgrn_norm_g'].reshape(1, -1), chip * meta.shape[1],
                                                 meta.shape[1], axis=1)

    grads, deltas, new_m, new_v = {}, {}, {}, {}
    for n, r in zip(BIG, reduced):
        res = adamw("adamw_" + n, _as2d(wts[n]), [_as2d(r)], _as2d(ms[n]), _as2d(vs[n]))
        grads[n], deltas[n], new_m[n], new_v[n] = [r.reshape(wts[n].shape) for r in res]
    shapes = [wts[n].shape for n in SMALL]
    res = adamw("adamw_small", _pack([wts[n] for n in SMALL]), [_pack([gs[n] for n in SMALL])],
                _pack([ms[n] for n in SMALL]), _pack([vs[n] for n in SMALL]))
    for d, r in zip((grads, deltas, new_m, new_v), res):
        d.update(zip(SMALL, _unpack(r, shapes)))
    return (loss_out, grad_x, *[grads[n] for n in WEIGHTS], *[deltas[n] for n in WEIGHTS],
            *[new_m[n] for n in WEIGHTS], *[new_v[n] for n in WEIGHTS])
```

```python
import functools
import math

import jax
import jax.numpy as jnp
from jax import lax
from jax.experimental import pallas as pl
from jax.experimental.pallas import tpu as pltpu

F32 = jnp.float32
BF16 = jnp.bfloat16

D_MODEL = 1024
N_META = 16
DEPTH = 2
S5_WIDTH = 512
S5_GROUP = 16
S5_GROUPS = 32
S5_STATE = 64
S5_CHUNK = 16
SB_WIDTH = 512
SB_BLOCK = 128
SB_DEAD = -110.0
HG_HEADS = 8
HG_DK = 128
HG_CHUNK = 64
D_FF = 4096
ALPHA = (2.0 * DEPTH) ** 0.25
LN_EPS = 1e-5
RMS_EPS = 1e-6
ADAM_LR = 0.001
ADAM_B1 = 0.9
ADAM_B2 = 0.999
ADAM_EPS = 1e-08
ADAM_WD = 0.01
ADAM_STEP = 10
N_CHIPS = 4
N_DEV = 8
LANES = 128
MESH = pl.DeviceIdType.MESH
VMEM_LIMIT = 56 * 1024 * 1024

WEIGHTS = ['meta', 'w_in_ab', 's5_lam_re', 's5_lam_im', 's5_log_dt', 's5_b_re', 's5_b_im', 's5_c_re', 's5_c_im',
           's5_d', 's5_w_glu', 's5_b_glu', 'w_out_ab', 'w_in_c', 'hgrn_gamma', 'hgrn_norm_g', 'w_out_c',
           'ln_mix_g', 'ln_mix_b', 'mlp_w_up', 'mlp_b_up', 'mlp_w_down', 'mlp_b_down', 'ln_mlp_g', 'ln_mlp_b']
BIG = ['w_in_ab', 's5_w_glu', 'w_out_ab', 'w_in_c', 'w_out_c', 'mlp_w_up', 'mlp_w_down']
SHARDED_SMALL = ['meta', 'hgrn_norm_g']
SMALL = [n for n in WEIGHTS if n not in BIG]


def _params(sem=None):
    return pltpu.CompilerParams(dimension_semantics=sem, vmem_limit_bytes=VMEM_LIMIT)


def _pick(n, cands):
    for c in cands:
        if n % c == 0:
            return c
    return n


def _dot(a, b, ca, cb):
    return lax.dot_general(a, b, (((ca,), (cb,)), ((), ())), preferred_element_type=F32)


def _split3(x):
    hi = x.astype(BF16)
    r = x - hi.astype(F32)
    mid = r.astype(BF16)
    lo = (r - mid.astype(F32)).astype(BF16)
    return hi, mid, lo


def _dot_exact_rhs(x, m, cx, cm):
    hi = x.astype(BF16)
    lo = (x - hi.astype(F32)).astype(BF16)
    return _dot(hi, m, cx, cm) + _dot(lo, m, cx, cm)


def _dot3(a, b, ca, cb):
    ah = a.astype(BF16)
    al = (a - ah.astype(F32)).astype(BF16)
    bh = b.astype(BF16)
    bl = (b - bh.astype(F32)).astype(BF16)
    return _dot(ah, bh, ca, cb) + (_dot(ah, bl, ca, cb) + _dot(al, bh, ca, cb))


def rowwise(name, fn, row_ins, bc_ins, out_widths, sum_widths=(), out_dtypes=None, tm=None):
    t = row_ins[0].shape[0]
    if tm is None:
        wmax = max([a.shape[1] for a in row_ins] + list(out_widths))
        tm = _pick(t, (272, 256, 128, 64, 16, 8)) if wmax > 1024 else _pick(t, (544, 512, 256, 128, 64, 16, 8))
    nr, nb, no, ns = len(row_ins), len(bc_ins), len(out_widths), len(sum_widths)
    out_dtypes = out_dtypes or [F32] * no

    def body(*refs):
        i = pl.program_id(0)
        rows = [r[...] for r in refs[:nr]]
        bcs = [r[...] for r in refs[nr:nr + nb]]
        outs, sums = fn(i, tm, rows, bcs)
        for r, v in zip(refs[nr + nb:nr + nb + no], outs):
            r[...] = v.astype(r.dtype)
        if ns:
            srefs = refs[nr + nb + no:]

            @pl.when(i == 0)
            def _():
                for r in srefs:
                    r[...] = jnp.zeros(r.shape, r.dtype)

            for r, v in zip(srefs, sums):
                r[...] += v

    in_specs = [pl.BlockSpec((tm, a.shape[1]), lambda i: (i, 0)) for a in row_ins]
    in_specs += [pl.BlockSpec(a.shape, lambda i: (0, 0)) for a in bc_ins]
    out_specs = [pl.BlockSpec((tm, w), lambda i: (i, 0)) for w in out_widths]
    out_specs += [pl.BlockSpec((1, w), lambda i: (0, 0)) for w in sum_widths]
    out_shape = [jax.ShapeDtypeStruct((t, w), dt) for w, dt in zip(out_widths, out_dtypes)]
    out_shape += [jax.ShapeDtypeStruct((1, w), F32) for w in sum_widths]
    res = pl.pallas_call(body, name=name, grid=(t // tm,), in_specs=in_specs, out_specs=out_specs,
                         out_shape=out_shape, compiler_params=_params(("arbitrary",)))(*row_ins, *bc_ins)
    return res


def _colsum(v):
    return jnp.sum(v, axis=0, keepdims=True)


def matmul(name, a, b, mode, *, bias=None, act=None, add=None, add_scale=1.0, relu2_of=None, colsum=False,
           out_dtype=F32, out_slots=0, tm=None, tn=None, tk=None):
    slotted = b.ndim == 3
    if mode == 'nn':
        m, k = a.shape
        n = b.shape[-1] * (b.shape[0] if slotted else 1)
    elif mode == 'nt':
        m, k = a.shape
        n = b.shape[-2]
    else:
        k, m = a.shape
        n = b.shape[-1]
    ns = b.shape[-1] if slotted else None
    if mode == 'tn':
        tm = tm or _pick(m, (512, 256, 128))
        tn = tn or _pick(n if not out_slots else n // out_slots, (1024, 512, 256, 128))
        tk = tk or _pick(k, (1088, 1024, 768, 512, 384, 256, 128))
    else:
        tm = tm or _pick(m, (1088, 1024, 768, 512, 384, 256, 128))
        tn = tn or _pick(n if not (slotted and mode == 'nn') else ns, (512, 256, 128))
        tk = tk or _pick(k if not (slotted and mode == 'nt') else ns, (1024, 512, 256, 128))
    gm, gn, gk = m // tm, n // tn, k // tk

    if mode == 'nn':
        a_spec = pl.BlockSpec((tm, tk), lambda i, j, kk: (i, kk))
        if slotted:
            per = ns // tn
            b_spec = pl.BlockSpec((None, tk, tn), lambda i, j, kk: (j // per, kk, j % per))
        else:
            b_spec = pl.BlockSpec((tk, tn), lambda i, j, kk: (kk, j))
        ca, cb = 1, 0
    elif mode == 'nt':
        a_spec = pl.BlockSpec((tm, tk), lambda i, j, kk: (i, kk))
        if slotted:
            per = ns // tk
            b_spec = pl.BlockSpec((None, tn, tk), lambda i, j, kk: (kk // per, j, kk % per))
        else:
            b_spec = pl.BlockSpec((tn, tk), lambda i, j, kk: (j, kk))
        ca, cb = 1, 1
    else:
        a_spec = pl.BlockSpec((tk, tm), lambda i, j, kk: (kk, i))
        b_spec = pl.BlockSpec((tk, tn), lambda i, j, kk: (kk, j))
        ca, cb = 0, 0
    in_specs = [a_spec, b_spec]
    args = [a, b]
    if bias is not None:
        in_specs.append(pl.BlockSpec((1, tn), lambda i, j, kk: (0, j)))
        args.append(bias)
    if add is not None:
        in_specs.append(pl.BlockSpec((tm, tn), lambda i, j, kk: (i, j)))
        args.append(add)
    if relu2_of is not None:
        in_specs.append(pl.BlockSpec((tm, tn), lambda i, j, kk: (i, j)))
        args.append(relu2_of)
    if out_slots:
        per_o = (n // out_slots) // tn
        out_spec = pl.BlockSpec((None, tm, tn), lambda i, j, kk: (j // per_o, i, j % per_o))
        out_shape = jax.ShapeDtypeStruct((out_slots, m, n // out_slots), out_dtype)
    else:
        out_spec = pl.BlockSpec((tm, tn), lambda i, j, kk: (i, j))
        out_shape = jax.ShapeDtypeStruct((m, n), out_dtype)
    grid = (gm, gn, gk)
    if colsum:
        assert mode == 'tn'
        out_spec = [out_spec, pl.BlockSpec((1, tn), lambda i, j, kk: (0, j))]
        out_shape = [out_shape, jax.ShapeDtypeStruct((1, n), F32)]

        def swapped(spec):
            return pl.BlockSpec(spec.block_shape, functools.partial(lambda j, i, kk, f: f(i, j, kk), f=spec.index_map))
        in_specs = [swapped(sp) for sp in in_specs]
        out_spec = [swapped(sp) for sp in out_spec]
        grid = (gn, gm, gk)

    def body(*refs):
        a_ref, b_ref = refs[0], refs[1]
        pos = 2
        bias_ref = add_ref = hid_ref = cs_ref = None
        if bias is not None:
            bias_ref = refs[pos]
            pos += 1
        if add is not None:
            add_ref = refs[pos]
            pos += 1
        if relu2_of is not None:
            hid_ref = refs[pos]
            pos += 1
        o_ref = refs[pos]
        pos += 1
        if colsum:
            cs_ref = refs[pos]
            pos += 1
        acc_ref = refs[pos] if gk > 1 else None
        bt = b_ref[...]
        p = _dot(a_ref[...].astype(BF16), bt.astype(BF16), ca, cb)
        kk = pl.program_id(2)

        if colsum:
            @pl.when(jnp.logical_and(pl.program_id(1) == 0, kk == 0))
            def _():
                cs_ref[...] = _colsum(bt.astype(F32))

            @pl.when(jnp.logical_and(pl.program_id(1) == 0, kk > 0))
            def _():
                cs_ref[...] += _colsum(bt.astype(F32))

        def finish(r):
            if bias_ref is not None:
                r = r + bias_ref[...]
            if act == 'relu2':
                r = jnp.square(jnp.maximum(r, 0.0))
            if hid_ref is not None:
                r = r * (2.0 * jnp.sqrt(hid_ref[...]))
            if add_ref is not None:
                r = r + add_scale * add_ref[...]
            o_ref[...] = r.astype(o_ref.dtype)

        if gk == 1:
            finish(p)
        else:
            @pl.when(kk == 0)
            def _():
                acc_ref[...] = p

            @pl.when(kk > 0)
            def _():
                acc_ref[...] += p

            @pl.when(kk == gk - 1)
            def _():
                finish(acc_ref[...])

    scratch = [pltpu.VMEM((tm, tn), F32)] if gk > 1 else []
    sem = ("arbitrary",) * 3 if colsum else ("parallel", "parallel", "arbitrary")
    return pl.pallas_call(body, name=name, grid=grid, in_specs=in_specs, out_specs=out_spec,
                          out_shape=out_shape, scratch_shapes=scratch, compiler_params=_params(sem))(*args)


def ln_fwd(name, h, mix, g, b):
    def fn(i, tm, rows, bcs):
        pre = ALPHA * rows[0] + rows[1]
        mu = jnp.mean(pre, axis=1, keepdims=True)
        xc = pre - mu
        var = jnp.mean(xc * xc, axis=1, keepdims=True)
        rstd = lax.rsqrt(var + LN_EPS)
        xhat = xc * rstd
        return (xhat * bcs[0] + bcs[1], xhat, rstd), ()
    return rowwise(name, fn, [h, mix], [g, b], [D_MODEL, D_MODEL, 1])


def ln_bwd(name, dy, xhat, rstd, g):
    def fn(i, tm, rows, bcs):
        dy_, xh, rs = rows
        dyg = dy_ * bcs[0]
        m1 = jnp.mean(dyg, axis=1, keepdims=True)
        m2 = jnp.mean(dyg * xh, axis=1, keepdims=True)
        dpre = rs * (dyg - m1 - xh * m2)
        return (dpre,), (_colsum(dy_ * xh), _colsum(dy_), _colsum(dpre))
    return rowwise(name, fn, [dy, xhat, rstd], [g], [D_MODEL], [D_MODEL] * 3)


_GELU_C = math.sqrt(2.0 / math.pi)


def gelu_fwd(name, x):
    def fn(i, tm, rows, bcs):
        v = rows[0]
        u = _GELU_C * (v + 0.044715 * (v * v * v))
        return (0.5 * v * (1.0 + jnp.tanh(u)),), ()
    return rowwise(name, fn, [x], [], [x.shape[1]])[0]


def gelu_bwd(name, dy, x):
    def fn(i, tm, rows, bcs):
        v = rows[1]
        th = jnp.tanh(_GELU_C * (v + 0.044715 * (v * v * v)))
        dg = 0.5 * (1.0 + th) + 0.5 * v * (1.0 - th * th) * (_GELU_C * (1.0 + 3.0 * 0.044715 * v * v))
        return (rows[0] * dg,), ()
    return rowwise(name, fn, [dy, x], [], [x.shape[1]])[0]


def glu_gate(name, y, gp):
    def fn(i, tm, rows, bcs):
        return (rows[0] * jax.nn.sigmoid(rows[1]),), ()
    return rowwise(name, fn, [y, gp], [], [y.shape[1]])[0]


def glu_gate_bwd(name, da, y, gp):
    def fn(i, tm, rows, bcs):
        s = jax.nn.sigmoid(rows[2])
        dgp = rows[0] * rows[1] * s * (1.0 - s)
        return (dgp, rows[0] * s), (_colsum(dgp),)
    w = y.shape[1]
    return rowwise(name, fn, [da, y, gp], [], [w, w], [w])


def loss_head(name, h, tgt, lp, seq):
    def fn(i, tm, rows, bcs):
        pos = (i * tm + lax.broadcasted_iota(jnp.int32, (tm, 1), 0)) % lp
        valid = jnp.logical_and(pos >= N_META, pos < N_META + seq)
        err = jnp.where(valid, rows[0] - rows[1], 0.0)
        part = 0.5 * jnp.sum(jnp.mean(err * err, axis=1, keepdims=True), axis=0, keepdims=True)
        return (err * (1.0 / D_MODEL),), (jnp.broadcast_to(part, (1, LANES)),)
    return rowwise(name, fn, [h, tgt], [], [D_MODEL], [LANES])


def adamw(name, w, gs, m, v):
    ng = len(gs)

    def fn(i, tm, rows, bcs):
        w_ = rows[0]
        g = rows[1] if ng == 1 else rows[1] + rows[2]
        m_, v_ = rows[1 + ng], rows[2 + ng]
        m_ = ADAM_B1 * m_ + (1.0 - ADAM_B1) * g
        v_ = ADAM_B2 * v_ + (1.0 - ADAM_B2) * jnp.square(g)
        m_hat = m_ / (1.0 - ADAM_B1 ** ADAM_STEP)
        v_hat = v_ / (1.0 - ADAM_B2 ** ADAM_STEP)
        delta = -ADAM_LR * (m_hat / (jnp.sqrt(v_hat) + ADAM_EPS) + ADAM_WD * w_)
        return (g, delta, m_, v_), ()
    wd = w.shape[1]
    return rowwise(name, fn, [w] + list(gs) + [m, v], [], [wd] * 4)


def s5_matrices(lam_re, lam_im, log_dt, b_re, b_im, c_re, c_im, d):
    c = S5_CHUNK
    dt = jnp.exp(log_dt)[:, None]
    tau = jnp.arange(c + 1, dtype=F32)[:, None, None]
    mag = jnp.exp(tau * (lam_re * dt)[None])
    ang = tau * (lam_im * dt)[None]
    pw_re, pw_im = mag * jnp.cos(ang), mag * jnp.sin(ang)
    nr, ni = pw_re[1] - 1.0, pw_im[1]
    den = lam_re * lam_re + lam_im * lam_im
    cf_re = (nr * lam_re + ni * lam_im) / den
    cf_im = (ni * lam_re - nr * lam_im) / den
    bb_re = cf_re[:, :, None] * b_re - cf_im[:, :, None] * b_im
    bb_im = cf_re[:, :, None] * b_im + cf_im[:, :, None] * b_re
    cp_re = c_re[None] * pw_re[:, :, None, :] - c_im[None] * pw_im[:, :, None, :]
    cp_im = c_re[None] * pw_im[:, :, None, :] + c_im[None] * pw_re[:, :, None, :]
    hp = lax.Precision.HIGHEST
    kern = (jnp.einsum('tghp,gpk->tghk', cp_re[:c], bb_re, precision=hp)
            - jnp.einsum('tghp,gpk->tghk', cp_im[:c], bb_im, precision=hp))
    ii = jnp.arange(c)
    sel = (ii[None, :, None] - ii[None, None, :] == ii[:, None, None]).astype(F32)
    a1 = jnp.einsum('tghk,tij->gjkih', kern, sel, precision=hp)
    a1 = a1 + jnp.einsum('ij,kh,gh->gjkih', jnp.eye(c, dtype=F32), jnp.eye(S5_GROUP, dtype=F32), d)
    a1 = a1.reshape(S5_GROUPS, c * S5_GROUP, c * S5_GROUP)
    a2 = jnp.concatenate([cp_re[1:], -cp_im[1:]], axis=-1)
    a2 = a2.transpose(1, 3, 0, 2).reshape(S5_GROUPS, 2 * S5_STATE, c * S5_GROUP)
    rv_re, rv_im = pw_re[:c][::-1], pw_im[:c][::-1]
    x_re = rv_re[:, :, :, None] * bb_re[None] - rv_im[:, :, :, None] * bb_im[None]
    x_im = rv_re[:, :, :, None] * bb_im[None] + rv_im[:, :, :, None] * bb_re[None]
    a3 = jnp.concatenate([x_re, x_im], axis=2)
    a3 = a3.transpose(1, 0, 3, 2).reshape(S5_GROUPS, c * S5_GROUP, 2 * S5_STATE)
    are = jnp.concatenate([pw_re[c], pw_re[c]], axis=-1)[:, None, :]
    aim = jnp.concatenate([-pw_im[c], pw_im[c]], axis=-1)[:, None, :]
    return a1, a2, a3, are, aim


S5_LEVELS = 8


def s5_scan_powers(lam_re, lam_im, log_dt):
    dt = jnp.exp(log_dt)[:, None]
    e = (S5_CHUNK * 2.0 ** jnp.arange(S5_LEVELS, dtype=F32))[:, None, None]
    mag = jnp.exp(e * (lam_re * dt)[None])
    ang = e * (lam_im * dt)[None]
    pr, pi = mag * jnp.cos(ang), mag * jnp.sin(ang)
    pre = jnp.concatenate([pr, pr], axis=-1).transpose(1, 0, 2)
    pim = jnp.concatenate([-pi, pi], axis=-1).transpose(1, 0, 2)
    return pre, pim


def _s5_scan(x, pre, pim, reverse):
    n = x.shape[0]
    rowi = lax.broadcasted_iota(jnp.int32, (n, 1), 0)
    t = x
    for k in range(S5_LEVELS):
        shift = 2 ** k
        if shift >= n:
            break
        p_re, p_im = pre[k:k + 1, :], pim[k:k + 1, :]
        if reverse:
            far = jnp.where(rowi < n - shift, pltpu.roll(t, n - shift, 0), 0.0)
            t = t + (p_re * far + pltpu.roll(p_im * far, S5_STATE, 1))
        else:
            far = jnp.where(rowi >= shift, pltpu.roll(t, shift, 0), 0.0)
            t = t + (p_re * far + p_im * pltpu.roll(far, S5_STATE, 1))
    return t


def s5_fwd(ug, a1, a2, a3, pre, pim, nseq):
    g, nc, cw = ug.shape
    per = nc // nseq
    sw = 2 * S5_STATE
    assert per <= 2 ** S5_LEVELS

    def body(u_ref, a1_ref, a2_ref, a3_ref, pre_ref, pim_ref, y_ref, s_ref):
        u = u_ref[...]
        x = _dot3(u, a3_ref[...], 1, 0)
        first = lax.broadcasted_iota(jnp.int32, (per, 1), 0) == 0
        for b in range(nseq):
            t = _s5_scan(x[b * per:(b + 1) * per], pre_ref[...], pim_ref[...], False)
            s_ref[pl.ds(b * per, per), :] = jnp.where(first, 0.0, pltpu.roll(t, 1, 0))
        y_ref[...] = _dot3(u, a1_ref[...], 1, 0) + _dot3(s_ref[...], a2_ref[...], 1, 0)

    def spec(shape):
        return pl.BlockSpec((None,) + shape, lambda i: (i, 0, 0))
    lv = (S5_LEVELS, sw)
    return pl.pallas_call(
        body, name="s5_fwd", grid=(g,),
        in_specs=[spec((nc, cw)), spec((cw, cw)), spec((sw, cw)), spec((cw, sw)), spec(lv), spec(lv)],
        out_specs=[spec((nc, cw)), spec((nc, sw))],
        out_shape=[jax.ShapeDtypeStruct((g, nc, cw), F32), jax.ShapeDtypeStruct((g, nc, sw), F32)],
        compiler_params=_params(("parallel",)))(ug, a1, a2, a3, pre, pim)


def s5_bwd(dyg, ug, sst, a1, a2, a3, pre, pim, nseq):
    g, nc, cw = ug.shape
    per = nc // nseq
    sw = 2 * S5_STATE

    def body(dy_ref, u_ref, s_ref, a1_ref, a2_ref, a3_ref, pre_ref, pim_ref,
             du_ref, da1_ref, da2_ref, da3_ref, dare_ref, daim_ref, dx_ref):
        dy = dy_ref[...]
        u = u_ref[...]
        gd = _dot3(dy, a2_ref[...], 1, 1)
        s_all = s_ref[...]
        da2_ref[...] = _dot3(s_all, dy, 0, 0)
        last = lax.broadcasted_iota(jnp.int32, (per, 1), 0) == per - 1
        for b in range(nseq):
            gs = _s5_scan(gd[b * per:(b + 1) * per], pre_ref[...], pim_ref[...], True)
            dx_ref[pl.ds(b * per, per), :] = jnp.where(last, 0.0, pltpu.roll(gs, per - 1, 0))
        dx = dx_ref[...]
        dare_ref[...] = _colsum(dx * s_all)
        daim_ref[...] = _colsum(dx * pltpu.roll(s_all, S5_STATE, 1))
        du_ref[...] = _dot3(dy, a1_ref[...], 1, 1) + _dot3(dx, a3_ref[...], 1, 1)
        da1_ref[...] = _dot3(u, dy, 0, 0)
        da3_ref[...] = _dot3(u, dx, 0, 0)

    def spec(shape):
        return pl.BlockSpec((None,) + shape, lambda i: (i, 0, 0))
    sds = jax.ShapeDtypeStruct
    return pl.pallas_call(
        body, name="s5_bwd", grid=(g,),
        in_specs=[spec((nc, cw)), spec((nc, cw)), spec((nc, sw)), spec((cw, cw)), spec((sw, cw)), spec((cw, sw)),
                  spec((S5_LEVELS, sw)), spec((S5_LEVELS, sw))],
        out_specs=[spec((nc, cw)), spec((cw, cw)), spec((sw, cw)), spec((cw, sw)), spec((1, sw)), spec((1, sw))],
        out_shape=[sds((g, nc, cw), F32), sds((g, cw, cw), F32), sds((g, sw, cw), F32), sds((g, cw, sw), F32),
                   sds((g, 1, sw), F32), sds((g, 1, sw), F32)],
        scratch_shapes=[pltpu.VMEM((nc, sw), F32)],
        compiler_params=_params(("parallel",)))(dyg, ug, sst, a1, a2, a3, pre, pim)


def _to_groups(u):
    t = u.shape[0]
    nc = t // S5_CHUNK
    return u.reshape(nc, S5_CHUNK, S5_GROUPS, S5_GROUP).transpose(2, 0, 1, 3).reshape(S5_GROUPS, nc, -1)


def _from_groups(yg):
    g, nc, _ = yg.shape
    return yg.reshape(g, nc, S5_CHUNK, S5_GROUP).transpose(1, 2, 0, 3).reshape(nc * S5_CHUNK, g * S5_GROUP)


def _sb_masks():
    row = lax.broadcasted_iota(jnp.int32, (SB_BLOCK, SB_BLOCK), 0)
    col = lax.broadcasted_iota(jnp.int32, (SB_BLOCK, SB_BLOCK), 1)
    lane = lax.broadcasted_iota(jnp.int32, (1, LANES), 1)
    return row, col, [lane < 64, lane >= 64]


def _sb_weights(z, vis, later_of, after):
    sp = jnp.maximum(z, 0.0) + jnp.log1p(jnp.exp(-jnp.abs(z)))
    lk = jnp.where(vis, -sp, 0.0)
    rs = jnp.sum(lk, axis=1, keepdims=True)
    later = _dot_exact_rhs(lk, after, 1, 0) + later_of(rs)
    lb = z - sp
    w = jnp.where(vis, jnp.exp(lb + later), 0.0)
    return w, rs, lb


def attn_fwd(proj, nseq, lp):
    nqb = lp // SB_BLOCK
    t = proj.shape[0]

    def body(q_ref, k_ref, v_ref, o_ref, tot_ref, first_ref):
        row, col, hmask = _sb_masks()
        after = (row > col).astype(BF16)
        tri = col < row
        lane8 = lax.broadcasted_iota(jnp.int32, (8, LANES), 1)

        def qloop(qi, firsts):
            q0 = pl.multiple_of(qi * SB_BLOCK, SB_BLOCK)
            q = q_ref[pl.ds(q0, SB_BLOCK), :] * 0.125
            qm = [jnp.where(hm, q, 0.0).astype(BF16) for hm in hmask]

            def alive(carry):
                return jnp.logical_and(carry[0] <= qi, carry[1] > 0)

            def kstep(carry):
                s, _, acc, lat0, lat1 = carry
                lats = [lat0, lat1]
                kj = qi - s
                k0 = pl.multiple_of(kj * SB_BLOCK, SB_BLOCK)
                k = k_ref[pl.ds(k0, SB_BLOCK), :].astype(BF16)
                v = v_ref[pl.ds(k0, SB_BLOCK), :]
                vis = jnp.logical_or(kj < qi, tri)
                for hd in range(2):
                    z = _dot(qm[hd], k, 1, 1)
                    w, rs, _ = _sb_weights(z, vis, functools.partial(lambda rs, lat: lat, lat=lats[hd]), after)
                    acc = acc + _dot(w.astype(BF16), jnp.where(hmask[hd], v, 0.0).astype(BF16), 1, 0)
                    lats[hd] = lats[hd] + rs
                go = (jnp.max(jnp.maximum(lats[0], lats[1])) > SB_DEAD).astype(jnp.int32)
                return s + 1, go, acc, lats[0], lats[1]

            z1 = jnp.zeros((SB_BLOCK, 1), F32)
            s, _, acc, lat0, lat1 = lax.while_loop(
                alive, kstep, (jnp.int32(0), jnp.int32(1), jnp.zeros((SB_BLOCK, LANES), F32), z1, z1))
            o_ref[pl.ds(q0, SB_BLOCK), :] = acc
            tot_ref[pl.ds(q0, SB_BLOCK), :] = jnp.where(hmask[0], lat0, lat1)
            return jnp.where(lane8 == qi, (qi - s + 1).astype(F32), firsts)

        first_ref[...] = lax.fori_loop(0, nqb, qloop, jnp.zeros((8, LANES), F32))

    nhp = SB_WIDTH // LANES
    qoff, koff, voff = S5_WIDTH // LANES, (S5_WIDTH + SB_WIDTH) // LANES, (S5_WIDTH + 2 * SB_WIDTH) // LANES
    return pl.pallas_call(
        body, name="attn_fwd", grid=(nseq, nhp),
        in_specs=[pl.BlockSpec((lp, LANES), lambda b, h: (b, qoff + h)),
                  pl.BlockSpec((lp, LANES), lambda b, h: (b, koff + h)),
                  pl.BlockSpec((lp, LANES), lambda b, h: (b, voff + h))],
        out_specs=[pl.BlockSpec((lp, LANES), lambda b, h: (b, h))] * 2
        + [pl.BlockSpec((None, None, 8, LANES), lambda b, h: (b, h, 0, 0))],
        out_shape=[jax.ShapeDtypeStruct((t, SB_WIDTH), F32)] * 2
        + [jax.ShapeDtypeStruct((nseq, nhp, 8, LANES), F32)],
        compiler_params=_params(("parallel", "parallel")))(proj, proj, proj)


def attn_bwd(proj, tot, first, do, nseq, lp):
    nqb = lp // SB_BLOCK
    t = proj.shape[0]

    def body(q_ref, k_ref, v_ref, tot_ref, first_ref, do_ref, dq_ref, dk_ref, dv_ref):
        row, col, hmask = _sb_masks()
        after = (row > col).astype(BF16)
        before = (row < col).astype(BF16)
        tri = col < row
        lane8 = lax.broadcasted_iota(jnp.int32, (8, LANES), 1)
        firsts = first_ref[...]
        dk_ref[...] = jnp.zeros(dk_ref.shape, F32)
        dv_ref[...] = jnp.zeros(dv_ref.shape, F32)

        def qloop(qi, _):
            first = jnp.max(jnp.where(lane8 == qi, firsts, 0.0)).astype(jnp.int32)
            first = jnp.clip(first, 0, qi)
            q0 = pl.multiple_of(qi * SB_BLOCK, SB_BLOCK)
            q = q_ref[pl.ds(q0, SB_BLOCK), :] * 0.125
            do_ = do_ref[pl.ds(q0, SB_BLOCK), :]
            tot_ = tot_ref[pl.ds(q0, SB_BLOCK), :]
            qm = [jnp.where(hm, q, 0.0).astype(BF16) for hm in hmask]
            dom = [jnp.where(hm, do_, 0.0).astype(BF16) for hm in hmask]
            tot = [jnp.max(jnp.where(hm, tot_, -jnp.inf), axis=1, keepdims=True) for hm in hmask]

            def kloop(kj, carry):
                dq = carry[0]
                pre = [carry[1], carry[2]]
                gpre = [carry[3], carry[4]]
                k0 = pl.multiple_of(kj * SB_BLOCK, SB_BLOCK)
                kf = k_ref[pl.ds(k0, SB_BLOCK), :]
                k = kf.astype(BF16)
                v = v_ref[pl.ds(k0, SB_BLOCK), :].astype(BF16)
                vis = jnp.logical_or(kj < qi, tri)
                dk_acc = jnp.zeros((SB_BLOCK, LANES), F32)
                dv_acc = jnp.zeros((SB_BLOCK, LANES), F32)
                for hd in range(2):
                    z = _dot(qm[hd], k, 1, 1)
                    later_of = functools.partial(lambda rs, t_, p_: t_ - p_ - rs, t_=tot[hd], p_=pre[hd])
                    w, rs, lb = _sb_weights(z, vis, later_of, after)
                    dw = _dot(dom[hd], v, 1, 1)
                    g = dw * w
                    dlk = _dot_exact_rhs(g, before, 1, 0) + gpre[hd]
                    sig = jnp.exp(lb)
                    dz = jnp.where(vis, g * (1.0 - sig) - dlk * sig, 0.0).astype(BF16)
                    dq = dq + _dot(dz, jnp.where(hmask[hd], kf, 0.0).astype(BF16), 1, 0)
                    dk_acc = dk_acc + _dot(dz, qm[hd], 0, 0)
                    dv_acc = dv_acc + _dot(w.astype(BF16), dom[hd], 0, 0)
                    pre[hd] = pre[hd] + rs
                    gpre[hd] = gpre[hd] + jnp.sum(g, axis=1, keepdims=True)
                dk_ref[pl.ds(k0, SB_BLOCK), :] += dk_acc
                dv_ref[pl.ds(k0, SB_BLOCK), :] += dv_acc
                return dq, pre[0], pre[1], gpre[0], gpre[1]

            z1 = jnp.zeros((SB_BLOCK, 1), F32)
            res = lax.fori_loop(first, qi + 1, kloop, (jnp.zeros((SB_BLOCK, LANES), F32), z1, z1, z1, z1))
            dq_ref[pl.ds(q0, SB_BLOCK), :] = res[0] * 0.125
            return 0

        lax.fori_loop(0, nqb, qloop, 0)

    nhp = SB_WIDTH // LANES
    qoff, koff, voff = S5_WIDTH // LANES, (S5_WIDTH + SB_WIDTH) // LANES, (S5_WIDTH + 2 * SB_WIDTH) // LANES
    blk = (lp, LANES)
    return pl.pallas_call(
        body, name="attn_bwd", grid=(nseq, nhp),
        in_specs=[pl.BlockSpec(blk, lambda b, h: (b, qoff + h)),
                  pl.BlockSpec(blk, lambda b, h: (b, koff + h)),
                  pl.BlockSpec(blk, lambda b, h: (b, voff + h)),
                  pl.BlockSpec(blk, lambda b, h: (b, h)),
                  pl.BlockSpec((None, None, 8, LANES), lambda b, h: (b, h, 0, 0)),
                  pl.BlockSpec(blk, lambda b, h: (b, h))],
        out_specs=[pl.BlockSpec(blk, lambda b, h: (b, h))] * 3,
        out_shape=[jax.ShapeDtypeStruct((t, SB_WIDTH), F32)] * 3,
        compiler_params=_params(("parallel", "parallel")))(proj, proj, proj, tot, first, do)


def _hg_gates(fc, lb):
    sg = jax.nn.sigmoid(fc)
    f = lb + (1.0 - lb) * sg
    return sg, f


def _hg_decay(f, incl):
    bcum = _dot_exact_lhs(incl, jnp.log(f))
    return bcum, bcum[HG_CHUNK - 1:HG_CHUNK, :]


def _dot_exact_lhs(m, x):
    hi, mid, lo = _split3(x)
    return _dot(m, hi, 1, 0) + (_dot(m, mid, 1, 0) + _dot(m, lo, 1, 0))


def hgrn_fwd(proj, lb, ng, nseq, lp):
    nch = lp // HG_CHUNK
    t = proj.shape[0]
    c = HG_CHUNK

    def body(q_ref, f_ref, v_ref, g_ref, lb_ref, ng_ref, o_ref, og_ref, st_ref):
        row = lax.broadcasted_iota(jnp.int32, (c, c), 0)
        col = lax.broadcasted_iota(jnp.int32, (c, c), 1)
        causal = col <= row
        incl = causal.astype(BF16)
        lbv, ngv = lb_ref[...], ng_ref[...]

        def step(ci, st):
            r0 = pl.multiple_of(ci * c, c)
            rows = pl.ds(r0, c)
            st_ref[ci] = st
            _, f = _hg_gates(f_ref[rows, :], lbv)
            bcum, blast = _hg_decay(f, incl)
            kk = 1.0 - f
            vc = v_ref[rows, :].astype(BF16)
            qd = (q_ref[rows, :] * jnp.exp(bcum)).astype(BF16)
            kd = (kk * jnp.exp(-bcum)).astype(BF16)
            a = jnp.where(causal, _dot(qd, kd, 1, 1), 0.0)
            o = _dot(a.astype(BF16), vc, 1, 0) + _dot(qd, st.astype(BF16), 1, 1)
            kend = (kk * jnp.exp(blast - bcum)).astype(BF16)
            st = st * jnp.exp(blast) + _dot(vc, kend, 0, 0)
            o_ref[rows, :] = o
            r = lax.rsqrt(jnp.mean(o * o, axis=1, keepdims=True) + RMS_EPS)
            gc = g_ref[rows, :]
            og_ref[rows, :] = o * r * ngv * (gc * jax.nn.sigmoid(gc))
            return st

        lax.fori_loop(0, nch, step, jnp.zeros((HG_DK, HG_DK), F32))

    blk = (lp, LANES)
    h = HG_HEADS
    return pl.pallas_call(
        body, name="hgrn_fwd", grid=(nseq, h),
        in_specs=[pl.BlockSpec(blk, lambda b, hh: (b, hh)),
                  pl.BlockSpec(blk, lambda b, hh: (b, h + hh)),
                  pl.BlockSpec(blk, lambda b, hh: (b, 2 * h + hh)),
                  pl.BlockSpec(blk, lambda b, hh: (b, 3 * h + hh)),
                  pl.BlockSpec((1, LANES), lambda b, hh: (0, hh)),
                  pl.BlockSpec((1, LANES), lambda b, hh: (0, hh))],
        out_specs=[pl.BlockSpec(blk, lambda b, hh: (b, hh)),
                   pl.BlockSpec(blk, lambda b, hh: (b, hh)),
                   pl.BlockSpec((None, None, nch, HG_DK, HG_DK), lambda b, hh: (b, hh, 0, 0, 0))],
        out_shape=[jax.ShapeDtypeStruct((t, D_MODEL), F32), jax.ShapeDtypeStruct((t, D_MODEL), F32),
                   jax.ShapeDtypeStruct((nseq, h, nch, HG_DK, HG_DK), F32)],
        compiler_params=_params(("parallel", "parallel")))(proj, proj, proj, proj, lb, ng)


def hgrn_bwd(proj, o, dog, states, lb, ng, nseq, lp):
    nch = lp // HG_CHUNK
    t = proj.shape[0]
    c = HG_CHUNK

    def body(q_ref, f_ref, v_ref, g_ref, o_ref, dog_ref, st_ref, lb_ref, ng_ref,
             dq_ref, df_ref, dv_ref, dg_ref, dlb_ref, dng_ref):
        row = lax.broadcasted_iota(jnp.int32, (c, c), 0)
        col = lax.broadcasted_iota(jnp.int32, (c, c), 1)
        causal = col <= row
        incl = causal.astype(BF16)
        from_here = (col >= row).astype(BF16)
        last = lax.broadcasted_iota(jnp.int32, (c, 1), 0) == c - 1
        lbv, ngv = lb_ref[...], ng_ref[...]

        def step(s, carry):
            dst, dlb, dng = carry
            ci = nch - 1 - s
            r0 = pl.multiple_of(ci * c, c)
            rows = pl.ds(r0, c)
            oc, gc, dogc = o_ref[rows, :], g_ref[rows, :], dog_ref[rows, :]
            r = lax.rsqrt(jnp.mean(oc * oc, axis=1, keepdims=True) + RMS_EPS)
            sgg = jax.nn.sigmoid(gc)
            silu = gc * sgg
            dg_ref[rows, :] = dogc * (oc * r * ngv) * (sgg * (1.0 + gc * (1.0 - sgg)))
            don = dogc * silu
            dng = dng + _colsum(don * oc * r)
            tt = don * ngv
            do = r * (tt - oc * (r * r) * jnp.mean(tt * oc, axis=1, keepdims=True))
            st = st_ref[ci]
            fc = f_ref[rows, :]
            sg, f = _hg_gates(fc, lbv)
            bcum, blast = _hg_decay(f, incl)
            kk = 1.0 - f
            qc, vf = q_ref[rows, :], v_ref[rows, :]
            pdec = jnp.exp(bcum)
            ndec = jnp.exp(-bcum)
            edec = jnp.exp(blast - bcum)
            eb = jnp.exp(blast)
            qd_f, kd_f, kend_f = qc * pdec, kk * ndec, kk * edec
            qd, kd, kend = qd_f.astype(BF16), kd_f.astype(BF16), kend_f.astype(BF16)
            vc, dob, stb, dstb = vf.astype(BF16), do.astype(BF16), st.astype(BF16), dst.astype(BF16)
            a = jnp.where(causal, _dot(qd, kd, 1, 1), 0.0).astype(BF16)
            da = jnp.where(causal, _dot(dob, vc, 1, 1), 0.0).astype(BF16)
            dv_ref[rows, :] = _dot(a, dob, 0, 0) + _dot(kend, dstb, 1, 1)
            dqd = _dot(da, kd, 1, 0) + _dot(dob, stb, 1, 0)
            dkd = _dot(da, qd, 0, 0)
            dkend = _dot(vc, dstb, 1, 0)
            dq_ref[rows, :] = dqd * pdec
            dblast = _colsum(dkend * kend_f) + eb * _colsum(dst * st)
            dbcum = dqd * qd_f - dkd * kd_f - dkend * kend_f
            dbcum = dbcum + jnp.where(last, dblast, 0.0)
            dlf = _dot_exact_lhs(from_here, dbcum)
            df = dlf / f - (dkd * ndec + dkend * edec)
            df_ref[rows, :] = df * (1.0 - lbv) * sg * (1.0 - sg)
            dlb = dlb + _colsum(df * (1.0 - sg))
            dst = dst * eb + _dot(dob, qd, 0, 0)
            return dst, dlb, dng

        z = jnp.zeros((1, LANES), F32)
        _, dlb, dng = lax.fori_loop(0, nch, step, (jnp.zeros((HG_DK, HG_DK), F32), z, z))
        dlb_ref[...] = dlb
        dng_ref[...] = dng

    blk = (lp, LANES)
    h = HG_HEADS
    vec = pl.BlockSpec((1, LANES), lambda b, hh: (0, hh))
    svec = pl.BlockSpec((None, 1, LANES), lambda b, hh: (b, 0, hh))
    return pl.pallas_call(
        body, name="hgrn_bwd", grid=(nseq, h),
        in_specs=[pl.BlockSpec(blk, lambda b, hh: (b, hh)),
                  pl.BlockSpec(blk, lambda b, hh: (b, h + hh)),
                  pl.BlockSpec(blk, lambda b, hh: (b, 2 * h + hh)),
                  pl.BlockSpec(blk, lambda b, hh: (b, 3 * h + hh)),
                  pl.BlockSpec(blk, lambda b, hh: (b, hh)),
                  pl.BlockSpec(blk, lambda b, hh: (b, hh)),
                  pl.BlockSpec((None, None, nch, HG_DK, HG_DK), lambda b, hh: (b, hh, 0, 0, 0)),
                  vec, vec],
        out_specs=[pl.BlockSpec(blk, lambda b, hh: (b, hh))] * 4 + [svec, svec],
        out_shape=[jax.ShapeDtypeStruct((t, D_MODEL), F32)] * 4
        + [jax.ShapeDtypeStruct((nseq, 1, D_MODEL), F32)] * 2,
        compiler_params=_params(("parallel", "parallel")))(proj, proj, proj, proj, o, dog, states, lb, ng)


def _lower_bound(gamma):
    p = jax.nn.softmax(gamma, axis=0)
    return (jnp.cumsum(p, axis=0) - p[0])[1][None, :]


def local_step(x, tgt, w):
    nseq, seq, _ = x.shape
    lp = -(-(N_META + seq) // SB_BLOCK) * SB_BLOCK
    t = nseq * lp
    pad = lp - N_META - seq
    h0 = jnp.concatenate([jnp.broadcast_to(w['meta'][None], (nseq, N_META, D_MODEL)), x,
                          jnp.zeros((nseq, pad, D_MODEL), F32)], axis=1).reshape(t, D_MODEL)
    tgt_p = jnp.pad(tgt, ((0, 0), (N_META, pad), (0, 0))).reshape(t, D_MODEL)
    row = lambda v: v.reshape(1, -1)
    g = {}

    proj = matmul("in_ab", h0, w['w_in_ab'], 'nn')
    s5_names = ['s5_lam_re', 's5_lam_im', 's5_log_dt', 's5_b_re', 's5_b_im', 's5_c_re', 's5_c_im', 's5_d']
    mats, mats_vjp = jax.vjp(s5_matrices, *[w[n][0] for n in s5_names])
    ug = _to_groups(proj[:, :S5_WIDTH])
    s5_pows = s5_scan_powers(*[w[n][0] for n in s5_names[:3]])
    yg, sst = s5_fwd(ug, *mats[:3], *s5_pows, nseq)
    ys5 = _from_groups(yg)
    y = gelu_fwd("gelu", ys5)
    gp = matmul("glu", y, w['s5_w_glu'], 'nn', bias=w['s5_b_glu'])
    a_out = glu_gate("glu_gate", y, gp)
    b_out, sb_tot, sb_first = attn_fwd(proj, nseq, lp)
    cat = jnp.concatenate([a_out, b_out], axis=1)
    mix0 = matmul("out_ab", cat, w['w_out_ab'], 'nn')
    h1, xh1, rs1 = ln_fwd("ln_mix0", h0, mix0, row(w['ln_mix_g'][0]), row(w['ln_mix_b'][0]))
    hid0 = matmul("up0", h1, w['mlp_w_up'][0], 'nn', bias=row(w['mlp_b_up'][0]), act='relu2')
    dn0 = matmul("down0", hid0, w['mlp_w_down'][0], 'nn', bias=row(w['mlp_b_down'][0]))
    h2, xh2, rs2 = ln_fwd("ln_mlp0", h1, dn0, row(w['ln_mlp_g'][0]), row(w['ln_mlp_b'][0]))

    projc = matmul("in_c", h2, w['w_in_c'], 'nn')
    lb, lb_vjp = jax.vjp(_lower_bound, w['hgrn_gamma'])
    ng = w['hgrn_norm_g']
    o, og, states = hgrn_fwd(projc, lb, ng, nseq, lp)
    mix1 = matmul("out_c", og, w['w_out_c'], 'nn')
    h3, xh3, rs3 = ln_fwd("ln_mix1", h2, mix1, row(w['ln_mix_g'][1]), row(w['ln_mix_b'][1]))
    hid1 = matmul("up1", h3, w['mlp_w_up'][1], 'nn', bias=row(w['mlp_b_up'][1]), act='relu2')
    dn1 = matmul("down1", hid1, w['mlp_w_down'][1], 'nn', bias=row(w['mlp_b_down'][1]))
    h4, xh4, rs4 = ln_fwd("ln_mlp1", h3, dn1, row(w['ln_mlp_g'][1]), row(w['ln_mlp_b'][1]))
    dy, loss = loss_head("loss", h4, tgt_p, lp, seq)

    def mlp_bwd(l, dh_out, xh_out, rs_out, hid, h_in):
        dpre, dg_, db_, dbd = ln_bwd(f"ln_mlp{l}_b", dh_out, xh_out, rs_out, row(w['ln_mlp_g'][l]))
        dpu = matmul(f"down{l}_dx", dpre, w['mlp_w_down'][l], 'nt', relu2_of=hid)
        dwd = matmul(f"down{l}_dw", hid, dpre, 'tn')
        dh_in = matmul(f"up{l}_dx", dpu, w['mlp_w_up'][l], 'nt', add=dpre, add_scale=ALPHA)
        dwu, dbu = matmul(f"up{l}_dw", h_in, dpu, 'tn', out_slots=N_CHIPS, colsum=True)
        return dh_in, dict(ln_mlp_g=dg_, ln_mlp_b=db_, mlp_b_down=dbd, mlp_b_up=dbu, mlp_w_up=dwu, mlp_w_down=dwd)

    dh3, gm1 = mlp_bwd(1, dy, xh4, rs4, hid1, h3)
    dpre, dg1, db1, _ = ln_bwd("ln_mix1_b", dh3, xh3, rs3, row(w['ln_mix_g'][1]))
    g['w_out_c'] = matmul("out_c_dw", og, dpre, 'tn')
    dog = matmul("out_c_dx", dpre, w['w_out_c'], 'nt')
    dq, df, di, dgg, dlb, dng = hgrn_bwd(projc, o, dog, states, lb, ng, nseq, lp)
    dprojc = jnp.concatenate([dq, df, di, dgg], axis=1)
    dh2 = matmul("in_c_dx", dprojc, w['w_in_c'], 'nt', add=dpre, add_scale=ALPHA)
    g['w_in_c'] = matmul("in_c_dw", h2, dprojc, 'tn', out_slots=N_CHIPS)
    g['hgrn_gamma'] = lb_vjp(jnp.sum(dlb, axis=0))[0]
    g['hgrn_norm_g'] = jnp.sum(dng, axis=0)

    dh1, gm0 = mlp_bwd(0, dh2, xh2, rs2, hid0, h1)
    dpre, dg0, db0, _ = ln_bwd("ln_mix0_b", dh1, xh1, rs1, row(w['ln_mix_g'][0]))
    g['w_out_ab'] = matmul("out_ab_dw", cat, dpre, 'tn')
    dcat = matmul("out_ab_dx", dpre, w['w_out_ab'], 'nt')
    dgp, da_s, dbglu = glu_gate_bwd("glu_gate_b", dcat[:, :S5_WIDTH], y, gp)
    g['s5_w_glu'] = matmul("glu_dw", y, dgp, 'tn')
    g['s5_b_glu'] = dbglu
    dy5 = matmul("glu_dx", dgp, w['s5_w_glu'], 'nt', add=da_s)
    dys5 = gelu_bwd("gelu_b", dy5, ys5)
    dug, da1, da2, da3, dare, daim = s5_bwd(_to_groups(dys5), ug, sst, *mats[:3], *s5_pows, nseq)
    for n, v in zip(s5_names, mats_vjp((da1, da2, da3, dare, daim))):
        g[n] = v[None]
    dqa, dka, dva = attn_bwd(proj, sb_tot, sb_first, dcat[:, S5_WIDTH:], nseq, lp)
    dproj = jnp.concatenate([_from_groups(dug), dqa, dka, dva], axis=1)
    dh0 = matmul("in_ab_dx", dproj, w['w_in_ab'], 'nt', add=dpre, add_scale=ALPHA)
    g['w_in_ab'] = matmul("in_ab_dw", h0, dproj, 'tn', out_slots=N_CHIPS)

    dh0 = dh0.reshape(nseq, lp, D_MODEL)
    grad_x = dh0[:, N_META:N_META + seq]
    g['meta'] = jnp.sum(dh0[:, :N_META], axis=0)
    g['ln_mix_g'] = jnp.concatenate([dg0, dg1], axis=0)
    g['ln_mix_b'] = jnp.concatenate([db0, db1], axis=0)
    for n in ('mlp_w_up', 'mlp_w_down'):
        g[n] = [gm0[n], gm1[n]]
    for n in ('ln_mlp_g', 'ln_mlp_b', 'mlp_b_down', 'mlp_b_up'):
        g[n] = jnp.concatenate([gm0[n], gm1[n]], axis=0)
    return loss, grad_x, g


def _exchange(name, ins, out_shapes, n_local, n_remote, build, aliased=0):
    ni, no = len(ins), len(out_shapes)

    def body(*refs):
        in_refs, out_refs = refs[:ni], refs[ni:ni + no]
        lsem, ssem, rsem = refs[ni + no:]
        me = (lax.axis_index("x"), lax.axis_index("y"), lax.axis_index("c"))
        nl = nr = 0
        for loc, rem in build(in_refs, out_refs, me):
            copies = [pltpu.make_async_copy(s, d, lsem.at[nl + i]) for i, (s, d) in enumerate(loc)]
            copies += [pltpu.make_async_remote_copy(src_ref=s, dst_ref=d, send_sem=ssem.at[nr + k],
                                                    recv_sem=rsem.at[nr + k], device_id=p, device_id_type=MESH)
                       for k, (s, d, p) in enumerate(rem)]
            nl, nr = nl + len(loc), nr + len(rem)
            for cp in copies:
                cp.start()
            for cp in copies:
                cp.wait()

    anyspec = pl.BlockSpec(memory_space=pl.ANY)
    return pl.pallas_call(
        body, name=name, in_specs=[anyspec] * ni, out_specs=[anyspec] * no, out_shape=out_shapes,
        input_output_aliases={i: i for i in range(aliased)},
        scratch_shapes=[pltpu.SemaphoreType.DMA((max(n_local, 1),)), pltpu.SemaphoreType.DMA((n_remote,)),
                        pltpu.SemaphoreType.DMA((n_remote,))])(*ins)


def _chip_peers(me):
    x, y, c = me
    return [(x ^ (m >> 1), y ^ (m & 1), c) for m in (1, 2, 3)]


def _half(ref, c, axis):
    h = ref.shape[axis] // 2
    idx = [slice(None)] * axis + [pl.ds(c * h, h)]
    return ref.at[tuple(idx)]


def gather_over_chips(name, bufs):
    def build(in_refs, out_refs, me):
        x, y, c = me
        j = 2 * x + y
        sib = (x, y, 1 - c)
        ici = [(_half(o.at[j], c, 0), _half(o.at[j], c, 0), p) for o in out_refs for p in _chip_peers(me)]
        d2d = [(_half(o.at[2 * p[0] + p[1]], c, 0), _half(o.at[2 * p[0] + p[1]], c, 0), sib)
               for o in out_refs for p in _chip_peers(me)]
        return [([], ici), ([], d2d)]
    shapes = [jax.ShapeDtypeStruct(b.shape, b.dtype) for b in bufs]
    return _exchange(name, bufs, shapes, 0, 6 * len(bufs), build, aliased=len(bufs))


def fold_cores(name, fulls):
    def build(in_refs, out_refs, me):
        x, y, c = me
        return [([], [(_half(s, 1 - c, 1), o, (x, y, 1 - c)) for s, o in zip(in_refs, out_refs)])]
    shapes = [jax.ShapeDtypeStruct((f.shape[0], f.shape[1] // 2, f.shape[2]), f.dtype) for f in fulls]
    return _exchange(name, fulls, shapes, 0, len(fulls), build)


def scatter_over_chips(name, parts):
    def build(in_refs, out_refs, me):
        j = 2 * me[0] + me[1]
        rem = [(s.at[2 * p[0] + p[1]], o.at[j], p) for s, o in zip(in_refs, out_refs) for p in _chip_peers(me)]
        return [([], rem)]
    shapes = [jax.ShapeDtypeStruct(f.shape, f.dtype) for f in parts]
    return _exchange(name, parts, shapes, 0, 3 * len(parts), build)


def join_cores(name, bufs):
    def build(in_refs, out_refs, me):
        x, y, c = me
        return [([], [(o.at[c], o.at[c], (x, y, 1 - c)) for o in out_refs])]
    shapes = [jax.ShapeDtypeStruct(b.shape, b.dtype) for b in bufs]
    return _exchange(name, bufs, shapes, 0, len(bufs), build, aliased=len(bufs))


def gather_over_devices(name, buf):
    def build(in_refs, out_refs, me):
        x, y, c = me
        i = 4 * x + 2 * y + c
        out = out_refs[0]
        rem = [(out.at[i], out.at[i], (x ^ (m >> 2), y ^ ((m >> 1) & 1), c ^ (m & 1))) for m in range(1, N_DEV)]
        return [([], rem)]
    return _exchange(name, [buf], [jax.ShapeDtypeStruct(buf.shape, buf.dtype)], 0, N_DEV - 1, build, aliased=1)[0]


def _slot_call(name, body, scalars, ins, in_maps, out_shape, out_map, blk, grid):
    gs = pltpu.PrefetchScalarGridSpec(
        num_scalar_prefetch=1, grid=grid,
        in_specs=[pl.BlockSpec((None,) + blk, m) for m in in_maps],
        out_specs=pl.BlockSpec((None,) + blk, out_map))
    return pl.pallas_call(body, name=name, grid_spec=gs, out_shape=out_shape,
                          compiler_params=_params(("arbitrary",) * len(grid)))(scalars, *ins)


def cast_into_slot(name, w2d, chip, dtype):
    r, wd = w2d.shape
    tm = _pick(r, (512, 256, 128, 64, 32, 16))

    def body(s_ref, w_ref, o_ref):
        o_ref[...] = w_ref[...].astype(o_ref.dtype)

    gs = pltpu.PrefetchScalarGridSpec(
        num_scalar_prefetch=1, grid=(r // tm,),
        in_specs=[pl.BlockSpec((tm, wd), lambda i, s: (i, 0))],
        out_specs=pl.BlockSpec((None, tm, wd), lambda i, s: (s[0], i, 0)))
    return pl.pallas_call(body, name=name, grid_spec=gs,
                          out_shape=jax.ShapeDtypeStruct((N_CHIPS, r, wd), dtype),
                          compiler_params=_params(("arbitrary",)))(chip.reshape(1), w2d)


def sum_cores(name, full, theirs, core):
    s, r, wd = full.shape
    h = r // 2
    tm = _pick(h, (512, 256, 128, 64, 32, 16))
    nt = h // tm

    def body(s_ref, a_ref, b_ref, o_ref):
        o_ref[...] = (a_ref[...] + b_ref[...]).astype(o_ref.dtype)

    return _slot_call(name, body, core.reshape(1), [full, theirs],
                      [lambda k, i, s_: (k, s_[0] * nt + i, 0), lambda k, i, s_: (k, i, 0)],
                      jax.ShapeDtypeStruct((s, h, wd), BF16), lambda k, i, s_: (k, i, 0), (tm, wd), (s, nt))


def sum_chips(name, own, recv, chip, core):
    s, r, wd = own.shape
    tm = _pick(r, (512, 256, 128, 64, 32, 16))

    def body(s_ref, a_ref, b1_ref, b2_ref, b3_ref, o_ref):
        acc = a_ref[...].astype(F32)
        for ref in (b1_ref, b2_ref, b3_ref):
            acc = acc + ref[...].astype(F32)
        o_ref[...] = acc

    maps = [lambda i, s_: (s_[0], i, 0)]
    maps += [functools.partial(lambda i, s_, m: (s_[0] ^ m, i, 0), m=m) for m in (1, 2, 3)]
    return _slot_call(name, body, jnp.stack([chip, core]), [own, recv, recv, recv], maps,
                      jax.ShapeDtypeStruct((2, r, wd), F32), lambda i, s_: (s_[1], i, 0), (tm, wd), (r // tm,))


def sum_slots(name, arr):
    s, r, wd = arr.shape
    tm = _pick(r, (512, 256, 128, 64, 32, 16, 8))

    def body(*refs):
        acc = refs[0][...].astype(F32)
        for ref in refs[1:s]:
            acc = acc + ref[...].astype(F32)
        refs[s][...] = acc

    specs = [pl.BlockSpec((None, tm, wd), functools.partial(lambda i, k: (k, i, 0), k=k)) for k in range(s)]
    return pl.pallas_call(body, name=name, grid=(r // tm,), in_specs=specs,
                          out_specs=pl.BlockSpec((tm, wd), lambda i: (i, 0)),
                          out_shape=jax.ShapeDtypeStruct((r, wd), F32),
                          compiler_params=_params(("parallel",)))(*([arr] * s))


def _pack(arrs):
    flat = jnp.concatenate([a.reshape(-1) for a in arrs])
    n = flat.shape[0]
    padded = -(-n // (512 * LANES)) * (512 * LANES)
    return jnp.pad(flat, (0, padded - n)).reshape(-1, LANES)


def _unpack(packed, shapes):
    flat = packed.reshape(-1)
    out, pos = [], 0
    for s in shapes:
        n = math.prod(s)
        out.append(flat[pos:pos + n].reshape(s))
        pos += n
    return out


def _as2d(a):
    return a.reshape(-1, a.shape[-1])


def kernel(x, meta, w_in_ab, s5_lam_re, s5_lam_im, s5_log_dt, s5_b_re, s5_b_im, s5_c_re, s5_c_im, s5_d, s5_w_glu, s5_b_glu, w_out_ab, w_in_c, hgrn_gamma, hgrn_norm_g, w_out_c, ln_mix_g, ln_mix_b, mlp_w_up, mlp_b_up, mlp_w_down, mlp_b_down, ln_mlp_g, ln_mlp_b, loss_target, m_meta, m_w_in_ab, m_s5_lam_re, m_s5_lam_im, m_s5_log_dt, m_s5_b_re, m_s5_b_im, m_s5_c_re, m_s5_c_im, m_s5_d, m_s5_w_glu, m_s5_b_glu, m_w_out_ab, m_w_in_c, m_hgrn_gamma, m_hgrn_norm_g, m_w_out_c, m_ln_mix_g, m_ln_mix_b, m_mlp_w_up, m_mlp_b_up, m_mlp_w_down, m_mlp_b_down, m_ln_mlp_g, m_ln_mlp_b, v_meta, v_w_in_ab, v_s5_lam_re, v_s5_lam_im, v_s5_log_dt, v_s5_b_re, v_s5_b_im, v_s5_c_re, v_s5_c_im, v_s5_d, v_s5_w_glu, v_s5_b_glu, v_w_out_ab, v_w_in_c, v_hgrn_gamma, v_hgrn_norm_g, v_w_out_c, v_ln_mix_g, v_ln_mix_b, v_mlp_w_up, v_mlp_b_up, v_mlp_w_down, v_mlp_b_down, v_ln_mlp_g, v_ln_mlp_b):
    given = dict(locals())
    wts = {n: given[n] for n in WEIGHTS}
    ms = {n: given['m_' + n] for n in WEIGHTS}
    vs = {n: given['v_' + n] for n in WEIGHTS}
    chip = 2 * lax.axis_index("x") + lax.axis_index("y")

    core = lax.axis_index("c")
    parts = [(n, l) for n in BIG for l in (range(DEPTH) if n.startswith('mlp_') else [0])]
    tags = [f"{n}{l}" for n, l in parts]
    shards = [cast_into_slot("cast_" + t, wts[n][l], chip, BF16) for t, (n, l) in zip(tags, parts)]
    small_sh = jnp.concatenate([meta, hgrn_norm_g, jnp.zeros((15, meta.shape[1]), F32)], axis=0)
    full = gather_over_chips("gather_weights", shards + [cast_into_slot("cast_small", small_sh, chip, F32)])
    fw = dict(zip(parts, full[:len(parts)]))
    sm = full[len(parts)]
    w = {n: wts[n] for n in SMALL}
    w['meta'] = sm[:, :N_META].transpose(1, 0, 2).reshape(N_META, D_MODEL)
    w['hgrn_norm_g'] = sm[:, N_META:N_META + 1].transpose(1, 0, 2).reshape(1, D_MODEL)
    w['w_in_ab'] = fw['w_in_ab', 0]
    w['s5_w_glu'] = fw['s5_w_glu', 0].reshape(S5_WIDTH, S5_WIDTH)
    w['w_out_ab'] = fw['w_out_ab', 0].reshape(D_MODEL, D_MODEL)
    w['w_in_c'] = fw['w_in_c', 0]
    w['w_out_c'] = fw['w_out_c', 0].reshape(D_MODEL, D_MODEL)
    w['mlp_w_up'] = [fw['mlp_w_up', l] for l in range(DEPTH)]
    w['mlp_w_down'] = [fw['mlp_w_down', l].reshape(D_FF, D_MODEL) for l in range(DEPTH)]

    loss, grad_x, g = local_step(x, loss_target, w)

    def slot_major(n, l):
        v = g[n][l] if n.startswith('mlp_') else g[n]
        return v if v.ndim == 3 else v.reshape(N_CHIPS, -1, v.shape[-1])
    gfull = [slot_major(n, l) for n, l in parts]
    theirs = fold_cores("fold_grads", gfull)
    chip_sums = [sum_cores("sum_cores_" + t, a, b, core) for t, a, b in zip(tags, gfull, theirs)]
    recv = scatter_over_chips("scatter_grads", chip_sums)
    reduced = join_cores("join_grads", [sum_chips("sum_chips_" + t, a, b, chip, core)
                                        for t, a, b in zip(tags, chip_sums, recv)])
    reduced = dict(zip(parts, [_as2d(r) for r in reduced]))

    small_shapes = [(LANES,)] + [g[n].shape for n in SMALL]
    packed = _pack([loss.reshape(-1)] + [g[n] for n in SMALL])
    slots = lax.dynamic_update_slice(jnp.zeros((N_DEV,) + packed.shape, F32), packed[None],
                                     (2 * chip + core, 0, 0))
    red = sum_slots("sum_small", gather_over_devices("gather_small", slots))
    red = _unpack(red, small_shapes)
    loss_out = red[0][0]
    gs = dict(zip(SMALL, red[1:]))
    gs['meta'] = lax.dynamic_slice_in_dim(gs['meta'], chip * meta.shape[1], meta.shape[1], axis=1)
    gs['hgrn_norm_g'] = lax.dynamic_slice_in_dim(gs['hgrn_norm_g'].reshape(1, -1), chip * meta.shape[1],
                                                 meta.shape[1], axis=1)

    grads, deltas, new_m, new_v = {}, {}, {}, {}
    for n in BIG:
        r = jnp.concatenate([reduced[n, l] for l in range(wts[n].shape[0])], axis=0)
        res = adamw("adamw_" + n, _as2d(wts[n]), [r], _as2d(ms[n]), _as2d(vs[n]))
        grads[n], deltas[n], new_m[n], new_v[n] = [r.reshape(wts[n].shape) for r in res]
    shapes = [wts[n].shape for n in SMALL]
    res = adamw("adamw_small", _pack([wts[n] for n in SMALL]), [_pack([gs[n] for n in SMALL])],
                _pack([ms[n] for n in SMALL]), _pack([vs[n] for n in SMALL]))
    for d, r in zip((grads, deltas, new_m, new_v), res):
        d.update(zip(SMALL, _unpack(r, shapes)))
    return (loss_out, grad_x, *[grads[n] for n in WEIGHTS], *[deltas[n] for n in WEIGHTS],
            *[new_m[n] for n in WEIGHTS], *[new_v[n] for n in WEIGHTS])
```

```python
import functools
import math

import jax
import jax.numpy as jnp
from jax import lax
from jax.experimental import pallas as pl
from jax.experimental.pallas import tpu as pltpu

F32 = jnp.float32
BF16 = jnp.bfloat16

D_MODEL = 1024
N_META = 16
DEPTH = 2
S5_WIDTH = 512
S5_GROUP = 16
S5_GROUPS = 32
S5_STATE = 64
S5_CHUNK = 16
SB_WIDTH = 512
SB_BLOCK = 128
SB_DEAD = -110.0
HG_HEADS = 8
HG_DK = 128
HG_CHUNK = 64
HG_UNROLL = 2
D_FF = 4096
ALPHA = (2.0 * DEPTH) ** 0.25
LN_EPS = 1e-5
RMS_EPS = 1e-6
ADAM_LR = 0.001
ADAM_B1 = 0.9
ADAM_B2 = 0.999
ADAM_EPS = 1e-08
ADAM_WD = 0.01
ADAM_STEP = 10
N_CHIPS = 4
N_DEV = 8
LANES = 128
MESH = pl.DeviceIdType.MESH
VMEM_LIMIT = 56 * 1024 * 1024

WEIGHTS = ['meta', 'w_in_ab', 's5_lam_re', 's5_lam_im', 's5_log_dt', 's5_b_re', 's5_b_im', 's5_c_re', 's5_c_im',
           's5_d', 's5_w_glu', 's5_b_glu', 'w_out_ab', 'w_in_c', 'hgrn_gamma', 'hgrn_norm_g', 'w_out_c',
           'ln_mix_g', 'ln_mix_b', 'mlp_w_up', 'mlp_b_up', 'mlp_w_down', 'mlp_b_down', 'ln_mlp_g', 'ln_mlp_b']
BIG = ['w_in_ab', 's5_w_glu', 'w_out_ab', 'w_in_c', 'w_out_c', 'mlp_w_up', 'mlp_w_down']
SHARDED_SMALL = ['meta', 'hgrn_norm_g']
SMALL = [n for n in WEIGHTS if n not in BIG]


def _params(sem=None):
    return pltpu.CompilerParams(dimension_semantics=sem, vmem_limit_bytes=VMEM_LIMIT)


def _pick(n, cands):
    for c in cands:
        if n % c == 0:
            return c
    return n


def _dot(a, b, ca, cb):
    return lax.dot_general(a, b, (((ca,), (cb,)), ((), ())), preferred_element_type=F32)


def _split3(x):
    hi = x.astype(BF16)
    r = x - hi.astype(F32)
    mid = r.astype(BF16)
    lo = (r - mid.astype(F32)).astype(BF16)
    return hi, mid, lo


def _dot_exact_rhs(x, m, cx, cm):
    hi = x.astype(BF16)
    lo = (x - hi.astype(F32)).astype(BF16)
    return _dot(hi, m, cx, cm) + _dot(lo, m, cx, cm)


def _dot3(a, b, ca, cb):
    ah = a.astype(BF16)
    al = (a - ah.astype(F32)).astype(BF16)
    bh = b.astype(BF16)
    bl = (b - bh.astype(F32)).astype(BF16)
    return _dot(ah, bh, ca, cb) + (_dot(ah, bl, ca, cb) + _dot(al, bh, ca, cb))


def rowwise(name, fn, row_ins, bc_ins, out_widths, sum_widths=(), out_dtypes=None, tm=None):
    t = row_ins[0].shape[0]
    if tm is None:
        wmax = max([a.shape[1] for a in row_ins] + list(out_widths))
        tm = _pick(t, (272, 256, 128, 64, 16, 8)) if wmax > 1024 else _pick(t, (544, 512, 256, 128, 64, 16, 8))
    nr, nb, no, ns = len(row_ins), len(bc_ins), len(out_widths), len(sum_widths)
    out_dtypes = out_dtypes or [F32] * no

    def body(*refs):
        i = pl.program_id(0)
        rows = [r[...] for r in refs[:nr]]
        bcs = [r[...] for r in refs[nr:nr + nb]]
        outs, sums = fn(i, tm, rows, bcs)
        for r, v in zip(refs[nr + nb:nr + nb + no], outs):
            r[...] = v.astype(r.dtype)
        if ns:
            srefs = refs[nr + nb + no:]

            @pl.when(i == 0)
            def _():
                for r in srefs:
                    r[...] = jnp.zeros(r.shape, r.dtype)

            for r, v in zip(srefs, sums):
                r[...] += v

    in_specs = [pl.BlockSpec((tm, a.shape[1]), lambda i: (i, 0)) for a in row_ins]
    in_specs += [pl.BlockSpec(a.shape, lambda i: (0, 0)) for a in bc_ins]
    out_specs = [pl.BlockSpec((tm, w), lambda i: (i, 0)) for w in out_widths]
    out_specs += [pl.BlockSpec((1, w), lambda i: (0, 0)) for w in sum_widths]
    out_shape = [jax.ShapeDtypeStruct((t, w), dt) for w, dt in zip(out_widths, out_dtypes)]
    out_shape += [jax.ShapeDtypeStruct((1, w), F32) for w in sum_widths]
    res = pl.pallas_call(body, name=name, grid=(t // tm,), in_specs=in_specs, out_specs=out_specs,
                         out_shape=out_shape, compiler_params=_params(("arbitrary",)))(*row_ins, *bc_ins)
    return res


def _colsum(v):
    return jnp.sum(v, axis=0, keepdims=True)


def matmul(name, a, b, mode, *, bias=None, act=None, add=None, add_scale=1.0, relu2_of=None, colsum=False,
           out_dtype=F32, out_slots=0, tm=None, tn=None, tk=None):
    slotted = b.ndim == 3
    if mode == 'nn':
        m, k = a.shape
        n = b.shape[-1] * (b.shape[0] if slotted else 1)
    elif mode == 'nt':
        m, k = a.shape
        n = b.shape[-2]
    else:
        k, m = a.shape
        n = b.shape[-1]
    ns = b.shape[-1] if slotted else None
    if mode == 'tn':
        tm = tm or _pick(m, (512, 256, 128))
        tn = tn or _pick(n if not out_slots else n // out_slots, (1024, 512, 256, 128))
        tk = tk or _pick(k, (1088, 1024, 768, 512, 384, 256, 128))
    else:
        tm = tm or _pick(m, (1088, 1024, 768, 512, 384, 256, 128))
        tn = tn or _pick(n if not (slotted and mode == 'nn') else ns, (512, 256, 128))
        tk = tk or _pick(k if not (slotted and mode == 'nt') else ns, (1024, 512, 256, 128))
    gm, gn, gk = m // tm, n // tn, k // tk

    if mode == 'nn':
        a_spec = pl.BlockSpec((tm, tk), lambda i, j, kk: (i, kk))
        if slotted:
            per = ns // tn
            b_spec = pl.BlockSpec((None, tk, tn), lambda i, j, kk: (j // per, kk, j % per))
        else:
            b_spec = pl.BlockSpec((tk, tn), lambda i, j, kk: (kk, j))
        ca, cb = 1, 0
    elif mode == 'nt':
        a_spec = pl.BlockSpec((tm, tk), lambda i, j, kk: (i, kk))
        if slotted:
            per = ns // tk
            b_spec = pl.BlockSpec((None, tn, tk), lambda i, j, kk: (kk // per, j, kk % per))
        else:
            b_spec = pl.BlockSpec((tn, tk), lambda i, j, kk: (j, kk))
        ca, cb = 1, 1
    else:
        a_spec = pl.BlockSpec((tk, tm), lambda i, j, kk: (kk, i))
        b_spec = pl.BlockSpec((tk, tn), lambda i, j, kk: (kk, j))
        ca, cb = 0, 0
    in_specs = [a_spec, b_spec]
    args = [a, b]
    if bias is not None:
        in_specs.append(pl.BlockSpec((1, tn), lambda i, j, kk: (0, j)))
        args.append(bias)
    if add is not None:
        in_specs.append(pl.BlockSpec((tm, tn), lambda i, j, kk: (i, j)))
        args.append(add)
    if relu2_of is not None:
        in_specs.append(pl.BlockSpec((tm, tn), lambda i, j, kk: (i, j)))
        args.append(relu2_of)
    if out_slots:
        per_o = (n // out_slots) // tn
        out_spec = pl.BlockSpec((None, tm, tn), lambda i, j, kk: (j // per_o, i, j % per_o))
        out_shape = jax.ShapeDtypeStruct((out_slots, m, n // out_slots), out_dtype)
    else:
        out_spec = pl.BlockSpec((tm, tn), lambda i, j, kk: (i, j))
        out_shape = jax.ShapeDtypeStruct((m, n), out_dtype)
    grid = (gm, gn, gk)
    if colsum:
        assert mode == 'tn'
        out_spec = [out_spec, pl.BlockSpec((1, tn), lambda i, j, kk: (0, j))]
        out_shape = [out_shape, jax.ShapeDtypeStruct((1, n), F32)]

        def swapped(spec):
            return pl.BlockSpec(spec.block_shape, functools.partial(lambda j, i, kk, f: f(i, j, kk), f=spec.index_map))
        in_specs = [swapped(sp) for sp in in_specs]
        out_spec = [swapped(sp) for sp in out_spec]
        grid = (gn, gm, gk)

    def body(*refs):
        a_ref, b_ref = refs[0], refs[1]
        pos = 2
        bias_ref = add_ref = hid_ref = cs_ref = None
        if bias is not None:
            bias_ref = refs[pos]
            pos += 1
        if add is not None:
            add_ref = refs[pos]
            pos += 1
        if relu2_of is not None:
            hid_ref = refs[pos]
            pos += 1
        o_ref = refs[pos]
        pos += 1
        if colsum:
            cs_ref = refs[pos]
            pos += 1
        acc_ref = refs[pos] if gk > 1 else None
        bt = b_ref[...]
        p = _dot(a_ref[...].astype(BF16), bt.astype(BF16), ca, cb)
        kk = pl.program_id(2)

        if colsum:
            @pl.when(jnp.logical_and(pl.program_id(1) == 0, kk == 0))
            def _():
                cs_ref[...] = _colsum(bt.astype(F32))

            @pl.when(jnp.logical_and(pl.program_id(1) == 0, kk > 0))
            def _():
                cs_ref[...] += _colsum(bt.astype(F32))

        def finish(r):
            if bias_ref is not None:
                r = r + bias_ref[...]
            if act == 'relu2':
                r = jnp.square(jnp.maximum(r, 0.0))
            if hid_ref is not None:
                r = r * (2.0 * jnp.sqrt(hid_ref[...]))
            if add_ref is not None:
                r = r + add_scale * add_ref[...]
            o_ref[...] = r.astype(o_ref.dtype)

        if gk == 1:
            finish(p)
        else:
            @pl.when(kk == 0)
            def _():
                acc_ref[...] = p

            @pl.when(kk > 0)
            def _():
                acc_ref[...] += p

            @pl.when(kk == gk - 1)
            def _():
                finish(acc_ref[...])

    scratch = [pltpu.VMEM((tm, tn), F32)] if gk > 1 else []
    sem = ("arbitrary",) * 3 if colsum else ("parallel", "parallel", "arbitrary")
    return pl.pallas_call(body, name=name, grid=grid, in_specs=in_specs, out_specs=out_spec,
                          out_shape=out_shape, scratch_shapes=scratch, compiler_params=_params(sem))(*args)


def ln_fwd(name, h, mix, g, b):
    def fn(i, tm, rows, bcs):
        pre = ALPHA * rows[0] + rows[1]
        mu = jnp.mean(pre, axis=1, keepdims=True)
        xc = pre - mu
        var = jnp.mean(xc * xc, axis=1, keepdims=True)
        rstd = lax.rsqrt(var + LN_EPS)
        xhat = xc * rstd
        return (xhat * bcs[0] + bcs[1], xhat, rstd), ()
    return rowwise(name, fn, [h, mix], [g, b], [D_MODEL, D_MODEL, 1])


def ln_bwd(name, dy, xhat, rstd, g):
    def fn(i, tm, rows, bcs):
        dy_, xh, rs = rows
        dyg = dy_ * bcs[0]
        m1 = jnp.mean(dyg, axis=1, keepdims=True)
        m2 = jnp.mean(dyg * xh, axis=1, keepdims=True)
        dpre = rs * (dyg - m1 - xh * m2)
        return (dpre,), (_colsum(dy_ * xh), _colsum(dy_), _colsum(dpre))
    return rowwise(name, fn, [dy, xhat, rstd], [g], [D_MODEL], [D_MODEL] * 3)


_GELU_C = math.sqrt(2.0 / math.pi)


def gelu_fwd(name, x):
    def fn(i, tm, rows, bcs):
        v = rows[0]
        u = _GELU_C * (v + 0.044715 * (v * v * v))
        return (0.5 * v * (1.0 + jnp.tanh(u)),), ()
    return rowwise(name, fn, [x], [], [x.shape[1]])[0]


def gelu_bwd(name, dy, x):
    def fn(i, tm, rows, bcs):
        v = rows[1]
        th = jnp.tanh(_GELU_C * (v + 0.044715 * (v * v * v)))
        dg = 0.5 * (1.0 + th) + 0.5 * v * (1.0 - th * th) * (_GELU_C * (1.0 + 3.0 * 0.044715 * v * v))
        return (rows[0] * dg,), ()
    return rowwise(name, fn, [dy, x], [], [x.shape[1]])[0]


def glu_gate(name, y, gp):
    def fn(i, tm, rows, bcs):
        return (rows[0] * jax.nn.sigmoid(rows[1]),), ()
    return rowwise(name, fn, [y, gp], [], [y.shape[1]])[0]


def glu_gate_bwd(name, da, y, gp):
    def fn(i, tm, rows, bcs):
        s = jax.nn.sigmoid(rows[2])
        dgp = rows[0] * rows[1] * s * (1.0 - s)
        return (dgp, rows[0] * s), (_colsum(dgp),)
    w = y.shape[1]
    return rowwise(name, fn, [da, y, gp], [], [w, w], [w])


def loss_head(name, h, tgt, lp, seq):
    def fn(i, tm, rows, bcs):
        pos = (i * tm + lax.broadcasted_iota(jnp.int32, (tm, 1), 0)) % lp
        valid = jnp.logical_and(pos >= N_META, pos < N_META + seq)
        err = jnp.where(valid, rows[0] - rows[1], 0.0)
        part = 0.5 * jnp.sum(jnp.mean(err * err, axis=1, keepdims=True), axis=0, keepdims=True)
        return (err * (1.0 / D_MODEL),), (jnp.broadcast_to(part, (1, LANES)),)
    return rowwise(name, fn, [h, tgt], [], [D_MODEL], [LANES])


def adamw(name, w, gs, m, v):
    ng = len(gs)

    def fn(i, tm, rows, bcs):
        w_ = rows[0]
        g = rows[1] if ng == 1 else rows[1] + rows[2]
        m_, v_ = rows[1 + ng], rows[2 + ng]
        m_ = ADAM_B1 * m_ + (1.0 - ADAM_B1) * g
        v_ = ADAM_B2 * v_ + (1.0 - ADAM_B2) * jnp.square(g)
        m_hat = m_ / (1.0 - ADAM_B1 ** ADAM_STEP)
        v_hat = v_ / (1.0 - ADAM_B2 ** ADAM_STEP)
        delta = -ADAM_LR * (m_hat / (jnp.sqrt(v_hat) + ADAM_EPS) + ADAM_WD * w_)
        return (g, delta, m_, v_), ()
    wd = w.shape[1]
    return rowwise(name, fn, [w] + list(gs) + [m, v], [], [wd] * 4)


def s5_matrices(lam_re, lam_im, log_dt, b_re, b_im, c_re, c_im, d):
    c = S5_CHUNK
    dt = jnp.exp(log_dt)[:, None]
    tau = jnp.arange(c + 1, dtype=F32)[:, None, None]
    mag = jnp.exp(tau * (lam_re * dt)[None])
    ang = tau * (lam_im * dt)[None]
    pw_re, pw_im = mag * jnp.cos(ang), mag * jnp.sin(ang)
    nr, ni = pw_re[1] - 1.0, pw_im[1]
    den = lam_re * lam_re + lam_im * lam_im
    cf_re = (nr * lam_re + ni * lam_im) / den
    cf_im = (ni * lam_re - nr * lam_im) / den
    bb_re = cf_re[:, :, None] * b_re - cf_im[:, :, None] * b_im
    bb_im = cf_re[:, :, None] * b_im + cf_im[:, :, None] * b_re
    cp_re = c_re[None] * pw_re[:, :, None, :] - c_im[None] * pw_im[:, :, None, :]
    cp_im = c_re[None] * pw_im[:, :, None, :] + c_im[None] * pw_re[:, :, None, :]
    hp = lax.Precision.HIGHEST
    kern = (jnp.einsum('tghp,gpk->tghk', cp_re[:c], bb_re, precision=hp)
            - jnp.einsum('tghp,gpk->tghk', cp_im[:c], bb_im, precision=hp))
    ii = jnp.arange(c)
    sel = (ii[None, :, None] - ii[None, None, :] == ii[:, None, None]).astype(F32)
    a1 = jnp.einsum('tghk,tij->gjkih', kern, sel, precision=hp)
    a1 = a1 + jnp.einsum('ij,kh,gh->gjkih', jnp.eye(c, dtype=F32), jnp.eye(S5_GROUP, dtype=F32), d)
    a1 = a1.reshape(S5_GROUPS, c * S5_GROUP, c * S5_GROUP)
    a2 = jnp.concatenate([cp_re[1:], -cp_im[1:]], axis=-1)
    a2 = a2.transpose(1, 3, 0, 2).reshape(S5_GROUPS, 2 * S5_STATE, c * S5_GROUP)
    rv_re, rv_im = pw_re[:c][::-1], pw_im[:c][::-1]
    x_re = rv_re[:, :, :, None] * bb_re[None] - rv_im[:, :, :, None] * bb_im[None]
    x_im = rv_re[:, :, :, None] * bb_im[None] + rv_im[:, :, :, None] * bb_re[None]
    a3 = jnp.concatenate([x_re, x_im], axis=2)
    a3 = a3.transpose(1, 0, 3, 2).reshape(S5_GROUPS, c * S5_GROUP, 2 * S5_STATE)
    are = jnp.concatenate([pw_re[c], pw_re[c]], axis=-1)[:, None, :]
    aim = jnp.concatenate([-pw_im[c], pw_im[c]], axis=-1)[:, None, :]
    return a1, a2, a3, are, aim


S5_LEVELS = 8


def s5_scan_powers(lam_re, lam_im, log_dt):
    dt = jnp.exp(log_dt)[:, None]
    e = (S5_CHUNK * 2.0 ** jnp.arange(S5_LEVELS, dtype=F32))[:, None, None]
    mag = jnp.exp(e * (lam_re * dt)[None])
    ang = e * (lam_im * dt)[None]
    pr, pi = mag * jnp.cos(ang), mag * jnp.sin(ang)
    pre = jnp.concatenate([pr, pr], axis=-1).transpose(1, 0, 2)
    pim = jnp.concatenate([-pi, pi], axis=-1).transpose(1, 0, 2)
    return pre, pim


def _s5_scan(x, pre, pim, reverse):
    n = x.shape[0]
    rowi = lax.broadcasted_iota(jnp.int32, (n, 1), 0)
    t = x
    for k in range(S5_LEVELS):
        shift = 2 ** k
        if shift >= n:
            break
        p_re, p_im = pre[k:k + 1, :], pim[k:k + 1, :]
        if reverse:
            far = jnp.where(rowi < n - shift, pltpu.roll(t, n - shift, 0), 0.0)
            t = t + (p_re * far + pltpu.roll(p_im * far, S5_STATE, 1))
        else:
            far = jnp.where(rowi >= shift, pltpu.roll(t, shift, 0), 0.0)
            t = t + (p_re * far + p_im * pltpu.roll(far, S5_STATE, 1))
    return t


def s5_fwd(ug, a1, a2, a3, pre, pim, nseq):
    g, nc, cw = ug.shape
    per = nc // nseq
    sw = 2 * S5_STATE
    assert per <= 2 ** S5_LEVELS

    def body(u_ref, a1_ref, a2_ref, a3_ref, pre_ref, pim_ref, y_ref, s_ref):
        u = u_ref[...]
        x = _dot3(u, a3_ref[...], 1, 0)
        first = lax.broadcasted_iota(jnp.int32, (per, 1), 0) == 0
        for b in range(nseq):
            t = _s5_scan(x[b * per:(b + 1) * per], pre_ref[...], pim_ref[...], False)
            s_ref[pl.ds(b * per, per), :] = jnp.where(first, 0.0, pltpu.roll(t, 1, 0))
        y_ref[...] = _dot3(u, a1_ref[...], 1, 0) + _dot3(s_ref[...], a2_ref[...], 1, 0)

    def spec(shape):
        return pl.BlockSpec((None,) + shape, lambda i: (i, 0, 0))
    lv = (S5_LEVELS, sw)
    return pl.pallas_call(
        body, name="s5_fwd", grid=(g,),
        in_specs=[spec((nc, cw)), spec((cw, cw)), spec((sw, cw)), spec((cw, sw)), spec(lv), spec(lv)],
        out_specs=[spec((nc, cw)), spec((nc, sw))],
        out_shape=[jax.ShapeDtypeStruct((g, nc, cw), F32), jax.ShapeDtypeStruct((g, nc, sw), F32)],
        compiler_params=_params(("parallel",)))(ug, a1, a2, a3, pre, pim)


def s5_bwd(dyg, ug, sst, a1, a2, a3, pre, pim, nseq):
    g, nc, cw = ug.shape
    per = nc // nseq
    sw = 2 * S5_STATE

    def body(dy_ref, u_ref, s_ref, a1_ref, a2_ref, a3_ref, pre_ref, pim_ref,
             du_ref, da1_ref, da2_ref, da3_ref, dare_ref, daim_ref, dx_ref):
        dy = dy_ref[...]
        u = u_ref[...]
        gd = _dot3(dy, a2_ref[...], 1, 1)
        s_all = s_ref[...]
        da2_ref[...] = _dot3(s_all, dy, 0, 0)
        last = lax.broadcasted_iota(jnp.int32, (per, 1), 0) == per - 1
        for b in range(nseq):
            gs = _s5_scan(gd[b * per:(b + 1) * per], pre_ref[...], pim_ref[...], True)
            dx_ref[pl.ds(b * per, per), :] = jnp.where(last, 0.0, pltpu.roll(gs, per - 1, 0))
        dx = dx_ref[...]
        dare_ref[...] = _colsum(dx * s_all)
        daim_ref[...] = _colsum(dx * pltpu.roll(s_all, S5_STATE, 1))
        du_ref[...] = _dot3(dy, a1_ref[...], 1, 1) + _dot3(dx, a3_ref[...], 1, 1)
        da1_ref[...] = _dot3(u, dy, 0, 0)
        da3_ref[...] = _dot3(u, dx, 0, 0)

    def spec(shape):
        return pl.BlockSpec((None,) + shape, lambda i: (i, 0, 0))
    sds = jax.ShapeDtypeStruct
    return pl.pallas_call(
        body, name="s5_bwd", grid=(g,),
        in_specs=[spec((nc, cw)), spec((nc, cw)), spec((nc, sw)), spec((cw, cw)), spec((sw, cw)), spec((cw, sw)),
                  spec((S5_LEVELS, sw)), spec((S5_LEVELS, sw))],
        out_specs=[spec((nc, cw)), spec((cw, cw)), spec((sw, cw)), spec((cw, sw)), spec((1, sw)), spec((1, sw))],
        out_shape=[sds((g, nc, cw), F32), sds((g, cw, cw), F32), sds((g, sw, cw), F32), sds((g, cw, sw), F32),
                   sds((g, 1, sw), F32), sds((g, 1, sw), F32)],
        scratch_shapes=[pltpu.VMEM((nc, sw), F32)],
        compiler_params=_params(("parallel",)))(dyg, ug, sst, a1, a2, a3, pre, pim)


def _to_groups(u):
    t = u.shape[0]
    nc = t // S5_CHUNK
    return u.reshape(nc, S5_CHUNK, S5_GROUPS, S5_GROUP).transpose(2, 0, 1, 3).reshape(S5_GROUPS, nc, -1)


def _from_groups(yg):
    g, nc, _ = yg.shape
    return yg.reshape(g, nc, S5_CHUNK, S5_GROUP).transpose(1, 2, 0, 3).reshape(nc * S5_CHUNK, g * S5_GROUP)


def _sb_masks():
    row = lax.broadcasted_iota(jnp.int32, (SB_BLOCK, SB_BLOCK), 0)
    col = lax.broadcasted_iota(jnp.int32, (SB_BLOCK, SB_BLOCK), 1)
    lane = lax.broadcasted_iota(jnp.int32, (1, LANES), 1)
    return row, col, [lane < 64, lane >= 64]


def _sb_weights(z, vis, later_of, after):
    sp = jnp.maximum(z, 0.0) + jnp.log1p(jnp.exp(-jnp.abs(z)))
    lk = jnp.where(vis, -sp, 0.0)
    rs = jnp.sum(lk, axis=1, keepdims=True)
    later = _dot_exact_rhs(lk, after, 1, 0) + later_of(rs)
    lb = z - sp
    w = jnp.where(vis, jnp.exp(lb + later), 0.0)
    return w, rs, lb


def _hidden_exchange(ex, n_in, n_out, grid):
    ni, no = (len(ex.ins), len(ex.out_shapes)) if ex else (0, 0)

    def split(refs):
        a, b = n_in + ni, n_in + ni + n_out
        return refs[:n_in], refs[a:b], (refs[n_in:a], refs[b:b + no], refs[b + no:])

    def at_step(which, fn, parts):
        if ex is not None:
            ids = [pl.program_id(d) == (0 if which == 'first' else g - 1) for d, g in enumerate(grid)]
            cond = ids[0]
            for c in ids[1:]:
                cond = jnp.logical_and(cond, c)
            pl.when(cond)(lambda: fn(parts[0], parts[1], *parts[2]))

    anyspec = pl.BlockSpec(memory_space=pl.ANY)
    kw = dict(in_specs=[anyspec] * ni, out_specs=[anyspec] * no, out_shape=list(ex.out_shapes) if ex else [],
              aliases={n_in + i: n_out + i for i in range(ni if ex and ex.in_place else 0)},
              scratch=ex.sems() if ex else [], ins=list(ex.ins) if ex else [])
    start = functools.partial(at_step, 'first', ex.start if ex else None)
    finish = functools.partial(at_step, 'last', ex.finish if ex else None)
    return split, start, finish, kw


def attn_fwd(proj, nseq, lp, ex=None):
    nqb = lp // SB_BLOCK
    t = proj.shape[0]
    nhp = SB_WIDTH // LANES
    split, ex_start, ex_finish, exkw = _hidden_exchange(ex, 3, 3, (nseq, nhp))

    def body(*refs):
        (q_ref, k_ref, v_ref), (o_ref, tot_ref, first_ref), ex_parts = split(refs)
        ex_start(ex_parts)
        row, col, hmask = _sb_masks()
        after = (row > col).astype(BF16)
        tri = col < row
        lane8 = lax.broadcasted_iota(jnp.int32, (8, LANES), 1)

        def qloop(qi, firsts):
            q0 = pl.multiple_of(qi * SB_BLOCK, SB_BLOCK)
            q = q_ref[pl.ds(q0, SB_BLOCK), :] * 0.125
            qm = [jnp.where(hm, q, 0.0).astype(BF16) for hm in hmask]

            def alive(carry):
                return jnp.logical_and(carry[0] <= qi, carry[1] > 0)

            def kstep(carry):
                s, _, acc, lat0, lat1 = carry
                lats = [lat0, lat1]
                kj = qi - s
                k0 = pl.multiple_of(kj * SB_BLOCK, SB_BLOCK)
                k = k_ref[pl.ds(k0, SB_BLOCK), :].astype(BF16)
                v = v_ref[pl.ds(k0, SB_BLOCK), :]
                vis = jnp.logical_or(kj < qi, tri)
                for hd in range(2):
                    z = _dot(qm[hd], k, 1, 1)
                    w, rs, _ = _sb_weights(z, vis, functools.partial(lambda rs, lat: lat, lat=lats[hd]), after)
                    acc = acc + _dot(w.astype(BF16), jnp.where(hmask[hd], v, 0.0).astype(BF16), 1, 0)
                    lats[hd] = lats[hd] + rs
                go = (jnp.max(jnp.maximum(lats[0], lats[1])) > SB_DEAD).astype(jnp.int32)
                return s + 1, go, acc, lats[0], lats[1]

            z1 = jnp.zeros((SB_BLOCK, 1), F32)
            s, _, acc, lat0, lat1 = lax.while_loop(
                alive, kstep, (jnp.int32(0), jnp.int32(1), jnp.zeros((SB_BLOCK, LANES), F32), z1, z1))
            o_ref[pl.ds(q0, SB_BLOCK), :] = acc
            tot_ref[pl.ds(q0, SB_BLOCK), :] = jnp.where(hmask[0], lat0, lat1)
            return jnp.where(lane8 == qi, (qi - s + 1).astype(F32), firsts)

        first_ref[...] = lax.fori_loop(0, nqb, qloop, jnp.zeros((8, LANES), F32))
        ex_finish(ex_parts)

    qoff, koff, voff = S5_WIDTH // LANES, (S5_WIDTH + SB_WIDTH) // LANES, (S5_WIDTH + 2 * SB_WIDTH) // LANES
    return pl.pallas_call(
        body, name="attn_fwd", grid=(nseq, nhp),
        in_specs=[pl.BlockSpec((lp, LANES), lambda b, h: (b, qoff + h)),
                  pl.BlockSpec((lp, LANES), lambda b, h: (b, koff + h)),
                  pl.BlockSpec((lp, LANES), lambda b, h: (b, voff + h))] + exkw['in_specs'],
        out_specs=[pl.BlockSpec((lp, LANES), lambda b, h: (b, h))] * 2
        + [pl.BlockSpec((None, None, 8, LANES), lambda b, h: (b, h, 0, 0))] + exkw['out_specs'],
        out_shape=[jax.ShapeDtypeStruct((t, SB_WIDTH), F32)] * 2
        + [jax.ShapeDtypeStruct((nseq, nhp, 8, LANES), F32)] + exkw['out_shape'],
        input_output_aliases=exkw['aliases'], scratch_shapes=exkw['scratch'],
        compiler_params=_params(("arbitrary", "arbitrary")))(proj, proj, proj, *exkw['ins'])


def attn_bwd(proj, tot, first, do, nseq, lp, ex=None):
    nqb = lp // SB_BLOCK
    t = proj.shape[0]
    nhp = SB_WIDTH // LANES
    split, ex_start, ex_finish, exkw = _hidden_exchange(ex, 6, 3, (nseq, nhp))

    def body(*refs):
        (q_ref, k_ref, v_ref, tot_ref, first_ref, do_ref), (dq_ref, dk_ref, dv_ref), ex_parts = split(refs)
        ex_start(ex_parts)
        row, col, hmask = _sb_masks()
        after = (row > col).astype(BF16)
        before = (row < col).astype(BF16)
        tri = col < row
        lane8 = lax.broadcasted_iota(jnp.int32, (8, LANES), 1)
        firsts = first_ref[...]
        dk_ref[...] = jnp.zeros(dk_ref.shape, F32)
        dv_ref[...] = jnp.zeros(dv_ref.shape, F32)

        def qloop(qi, _):
            first = jnp.max(jnp.where(lane8 == qi, firsts, 0.0)).astype(jnp.int32)
            first = jnp.clip(first, 0, qi)
            q0 = pl.multiple_of(qi * SB_BLOCK, SB_BLOCK)
            q = q_ref[pl.ds(q0, SB_BLOCK), :] * 0.125
            do_ = do_ref[pl.ds(q0, SB_BLOCK), :]
            tot_ = tot_ref[pl.ds(q0, SB_BLOCK), :]
            qm = [jnp.where(hm, q, 0.0).astype(BF16) for hm in hmask]
            dom = [jnp.where(hm, do_, 0.0).astype(BF16) for hm in hmask]
            tot = [jnp.max(jnp.where(hm, tot_, -jnp.inf), axis=1, keepdims=True) for hm in hmask]

            def kloop(kj, carry):
                dq = carry[0]
                pre = [carry[1], carry[2]]
                gpre = [carry[3], carry[4]]
                k0 = pl.multiple_of(kj * SB_BLOCK, SB_BLOCK)
                kf = k_ref[pl.ds(k0, SB_BLOCK), :]
                k = kf.astype(BF16)
                v = v_ref[pl.ds(k0, SB_BLOCK), :].astype(BF16)
                vis = jnp.logical_or(kj < qi, tri)
                dk_acc = jnp.zeros((SB_BLOCK, LANES), F32)
                dv_acc = jnp.zeros((SB_BLOCK, LANES), F32)
                for hd in range(2):
                    z = _dot(qm[hd], k, 1, 1)
                    later_of = functools.partial(lambda rs, t_, p_: t_ - p_ - rs, t_=tot[hd], p_=pre[hd])
                    w, rs, lb = _sb_weights(z, vis, later_of, after)
                    dw = _dot(dom[hd], v, 1, 1)
                    g = dw * w
                    dlk = _dot_exact_rhs(g, before, 1, 0) + gpre[hd]
                    sig = jnp.exp(lb)
                    dz = jnp.where(vis, g * (1.0 - sig) - dlk * sig, 0.0).astype(BF16)
                    dq = dq + _dot(dz, jnp.where(hmask[hd], kf, 0.0).astype(BF16), 1, 0)
                    dk_acc = dk_acc + _dot(dz, qm[hd], 0, 0)
                    dv_acc = dv_acc + _dot(w.astype(BF16), dom[hd], 0, 0)
                    pre[hd] = pre[hd] + rs
                    gpre[hd] = gpre[hd] + jnp.sum(g, axis=1, keepdims=True)
                dk_ref[pl.ds(k0, SB_BLOCK), :] += dk_acc
                dv_ref[pl.ds(k0, SB_BLOCK), :] += dv_acc
                return dq, pre[0], pre[1], gpre[0], gpre[1]

            z1 = jnp.zeros((SB_BLOCK, 1), F32)
            res = lax.fori_loop(first, qi + 1, kloop, (jnp.zeros((SB_BLOCK, LANES), F32), z1, z1, z1, z1))
            dq_ref[pl.ds(q0, SB_BLOCK), :] = res[0] * 0.125
            return 0

        lax.fori_loop(0, nqb, qloop, 0)
        ex_finish(ex_parts)

    qoff, koff, voff = S5_WIDTH // LANES, (S5_WIDTH + SB_WIDTH) // LANES, (S5_WIDTH + 2 * SB_WIDTH) // LANES
    blk = (lp, LANES)
    return pl.pallas_call(
        body, name="attn_bwd", grid=(nseq, nhp),
        in_specs=[pl.BlockSpec(blk, lambda b, h: (b, qoff + h)),
                  pl.BlockSpec(blk, lambda b, h: (b, koff + h)),
                  pl.BlockSpec(blk, lambda b, h: (b, voff + h)),
                  pl.BlockSpec(blk, lambda b, h: (b, h)),
                  pl.BlockSpec((None, None, 8, LANES), lambda b, h: (b, h, 0, 0)),
                  pl.BlockSpec(blk, lambda b, h: (b, h))] + exkw['in_specs'],
        out_specs=[pl.BlockSpec(blk, lambda b, h: (b, h))] * 3 + exkw['out_specs'],
        out_shape=[jax.ShapeDtypeStruct((t, SB_WIDTH), F32)] * 3 + exkw['out_shape'],
        input_output_aliases=exkw['aliases'], scratch_shapes=exkw['scratch'],
        compiler_params=_params(("arbitrary", "arbitrary")))(proj, proj, proj, tot, first, do, *exkw['ins'])


def _hg_loop(nch, step, init):
    assert nch % HG_UNROLL == 0

    def trip(i, carry):
        for u in range(HG_UNROLL):
            carry = step(i * HG_UNROLL + u, carry)
        return carry
    return lax.fori_loop(0, nch // HG_UNROLL, trip, init)


def _hg_gates(fc, lb):
    sg = jax.nn.sigmoid(fc)
    f = lb + (1.0 - lb) * sg
    return sg, f


def _hg_decay(f, incl):
    bcum = _dot_exact_lhs(incl, jnp.log(f))
    return bcum, bcum[HG_CHUNK - 1:HG_CHUNK, :]


def _dot_exact_lhs(m, x):
    hi, mid, lo = _split3(x)
    return _dot(m, hi, 1, 0) + (_dot(m, mid, 1, 0) + _dot(m, lo, 1, 0))


def hgrn_fwd(proj, lb, ng, nseq, lp):
    nch = lp // HG_CHUNK
    t = proj.shape[0]
    c = HG_CHUNK

    def body(q_ref, f_ref, v_ref, g_ref, lb_ref, ng_ref, o_ref, og_ref, st_ref):
        row = lax.broadcasted_iota(jnp.int32, (c, c), 0)
        col = lax.broadcasted_iota(jnp.int32, (c, c), 1)
        causal = col <= row
        incl = causal.astype(BF16)
        lbv, ngv = lb_ref[...], ng_ref[...]

        def step(ci, st):
            r0 = pl.multiple_of(ci * c, c)
            rows = pl.ds(r0, c)
            st_ref[ci] = st
            _, f = _hg_gates(f_ref[rows, :], lbv)
            bcum, blast = _hg_decay(f, incl)
            kk = 1.0 - f
            vc = v_ref[rows, :].astype(BF16)
            qd = (q_ref[rows, :] * jnp.exp(bcum)).astype(BF16)
            kd = (kk * jnp.exp(-bcum)).astype(BF16)
            a = jnp.where(causal, _dot(qd, kd, 1, 1), 0.0)
            o = _dot(a.astype(BF16), vc, 1, 0) + _dot(qd, st.astype(BF16), 1, 1)
            kend = (kk * jnp.exp(blast - bcum)).astype(BF16)
            st = st * jnp.exp(blast) + _dot(vc, kend, 0, 0)
            o_ref[rows, :] = o
            r = lax.rsqrt(jnp.mean(o * o, axis=1, keepdims=True) + RMS_EPS)
            gc = g_ref[rows, :]
            og_ref[rows, :] = o * r * ngv * (gc * jax.nn.sigmoid(gc))
            return st

        _hg_loop(nch, step, jnp.zeros((HG_DK, HG_DK), F32))

    blk = (lp, LANES)
    h = HG_HEADS
    return pl.pallas_call(
        body, name="hgrn_fwd", grid=(nseq, h),
        in_specs=[pl.BlockSpec(blk, lambda b, hh: (b, hh)),
                  pl.BlockSpec(blk, lambda b, hh: (b, h + hh)),
                  pl.BlockSpec(blk, lambda b, hh: (b, 2 * h + hh)),
                  pl.BlockSpec(blk, lambda b, hh: (b, 3 * h + hh)),
                  pl.BlockSpec((1, LANES), lambda b, hh: (0, hh)),
                  pl.BlockSpec((1, LANES), lambda b, hh: (0, hh))],
        out_specs=[pl.BlockSpec(blk, lambda b, hh: (b, hh)),
                   pl.BlockSpec(blk, lambda b, hh: (b, hh)),
                   pl.BlockSpec((None, None, nch, HG_DK, HG_DK), lambda b, hh: (b, hh, 0, 0, 0))],
        out_shape=[jax.ShapeDtypeStruct((t, D_MODEL), F32), jax.ShapeDtypeStruct((t, D_MODEL), F32),
                   jax.ShapeDtypeStruct((nseq, h, nch, HG_DK, HG_DK), F32)],
        compiler_params=_params(("parallel", "parallel")))(proj, proj, proj, proj, lb, ng)


def hgrn_bwd(proj, o, dog, states, lb, ng, nseq, lp):
    nch = lp // HG_CHUNK
    t = proj.shape[0]
    c = HG_CHUNK

    def body(q_ref, f_ref, v_ref, g_ref, o_ref, dog_ref, st_ref, lb_ref, ng_ref,
             dq_ref, df_ref, dv_ref, dg_ref, dlb_ref, dng_ref):
        row = lax.broadcasted_iota(jnp.int32, (c, c), 0)
        col = lax.broadcasted_iota(jnp.int32, (c, c), 1)
        causal = col <= row
        incl = causal.astype(BF16)
        from_here = (col >= row).astype(BF16)
        last = lax.broadcasted_iota(jnp.int32, (c, 1), 0) == c - 1
        lbv, ngv = lb_ref[...], ng_ref[...]

        def step(s, carry):
            dst, dlb, dng = carry
            ci = nch - 1 - s
            r0 = pl.multiple_of(ci * c, c)
            rows = pl.ds(r0, c)
            oc, gc, dogc = o_ref[rows, :], g_ref[rows, :], dog_ref[rows, :]
            r = lax.rsqrt(jnp.mean(oc * oc, axis=1, keepdims=True) + RMS_EPS)
            sgg = jax.nn.sigmoid(gc)
            silu = gc * sgg
            dg_ref[rows, :] = dogc * (oc * r * ngv) * (sgg * (1.0 + gc * (1.0 - sgg)))
            don = dogc * silu
            dng = dng + _colsum(don * oc * r)
            tt = don * ngv
            do = r * (tt - oc * (r * r) * jnp.mean(tt * oc, axis=1, keepdims=True))
            st = st_ref[ci]
            fc = f_ref[rows, :]
            sg, f = _hg_gates(fc, lbv)
            bcum, blast = _hg_decay(f, incl)
            kk = 1.0 - f
            qc, vf = q_ref[rows, :], v_ref[rows, :]
            pdec = jnp.exp(bcum)
            ndec = jnp.exp(-bcum)
            edec = jnp.exp(blast - bcum)
            eb = jnp.exp(blast)
            qd_f, kd_f, kend_f = qc * pdec, kk * ndec, kk * edec
            qd, kd, kend = qd_f.astype(BF16), kd_f.astype(BF16), kend_f.astype(BF16)
            vc, dob, stb, dstb = vf.astype(BF16), do.astype(BF16), st.astype(BF16), dst.astype(BF16)
            a = jnp.where(causal, _dot(qd, kd, 1, 1), 0.0).astype(BF16)
            da = jnp.where(causal, _dot(dob, vc, 1, 1), 0.0).astype(BF16)
            dv_ref[rows, :] = _dot(a, dob, 0, 0) + _dot(kend, dstb, 1, 1)
            dqd = _dot(da, kd, 1, 0) + _dot(dob, stb, 1, 0)
            dkd = _dot(da, qd, 0, 0)
            dkend = _dot(vc, dstb, 1, 0)
            dq_ref[rows, :] = dqd * pdec
            dblast = _colsum(dkend * kend_f) + eb * _colsum(dst * st)
            dbcum = dqd * qd_f - dkd * kd_f - dkend * kend_f
            dbcum = dbcum + jnp.where(last, dblast, 0.0)
            dlf = _dot_exact_lhs(from_here, dbcum)
            df = dlf / f - (dkd * ndec + dkend * edec)
            df_ref[rows, :] = df * (1.0 - lbv) * sg * (1.0 - sg)
            dlb = dlb + _colsum(df * (1.0 - sg))
            dst = dst * eb + _dot(dob, qd, 0, 0)
            return dst, dlb, dng

        z = jnp.zeros((1, LANES), F32)
        _, dlb, dng = _hg_loop(nch, step, (jnp.zeros((HG_DK, HG_DK), F32), z, z))
        dlb_ref[...] = dlb
        dng_ref[...] = dng

    blk = (lp, LANES)
    h = HG_HEADS
    vec = pl.BlockSpec((1, LANES), lambda b, hh: (0, hh))
    svec = pl.BlockSpec((None, 1, LANES), lambda b, hh: (b, 0, hh))
    return pl.pallas_call(
        body, name="hgrn_bwd", grid=(nseq, h),
        in_specs=[pl.BlockSpec(blk, lambda b, hh: (b, hh)),
                  pl.BlockSpec(blk, lambda b, hh: (b, h + hh)),
                  pl.BlockSpec(blk, lambda b, hh: (b, 2 * h + hh)),
                  pl.BlockSpec(blk, lambda b, hh: (b, 3 * h + hh)),
                  pl.BlockSpec(blk, lambda b, hh: (b, hh)),
                  pl.BlockSpec(blk, lambda b, hh: (b, hh)),
                  pl.BlockSpec((None, None, nch, HG_DK, HG_DK), lambda b, hh: (b, hh, 0, 0, 0)),
                  vec, vec],
        out_specs=[pl.BlockSpec(blk, lambda b, hh: (b, hh))] * 4 + [svec, svec],
        out_shape=[jax.ShapeDtypeStruct((t, D_MODEL), F32)] * 4
        + [jax.ShapeDtypeStruct((nseq, 1, D_MODEL), F32)] * 2,
        compiler_params=_params(("parallel", "parallel")))(proj, proj, proj, proj, o, dog, states, lb, ng)


def _lower_bound(gamma):
    p = jax.nn.softmax(gamma, axis=0)
    return (jnp.cumsum(p, axis=0) - p[0])[1][None, :]


def local_step(x, tgt, w, hooks=None):
    nseq, seq, _ = x.shape
    lp = -(-(N_META + seq) // SB_BLOCK) * SB_BLOCK
    t = nseq * lp
    pad = lp - N_META - seq
    h0 = jnp.concatenate([jnp.broadcast_to(w['meta'][None], (nseq, N_META, D_MODEL)), x,
                          jnp.zeros((nseq, pad, D_MODEL), F32)], axis=1).reshape(t, D_MODEL)
    tgt_p = jnp.pad(tgt, ((0, 0), (N_META, pad), (0, 0))).reshape(t, D_MODEL)
    row = lambda v: v.reshape(1, -1)
    g = {}

    proj = matmul("in_ab", h0, w['w_in_ab'], 'nn')
    att = attn_fwd(proj, nseq, lp, ex=hooks.gather_late if hooks else None)
    b_out, sb_tot, sb_first = att[:3]
    if hooks:
        w = {**w, **hooks.late_weights(att[3:])}
    s5_names = ['s5_lam_re', 's5_lam_im', 's5_log_dt', 's5_b_re', 's5_b_im', 's5_c_re', 's5_c_im', 's5_d']
    mats, mats_vjp = jax.vjp(s5_matrices, *[w[n][0] for n in s5_names])
    ug = _to_groups(proj[:, :S5_WIDTH])
    s5_pows = s5_scan_powers(*[w[n][0] for n in s5_names[:3]])
    yg, sst = s5_fwd(ug, *mats[:3], *s5_pows, nseq)
    ys5 = _from_groups(yg)
    y = gelu_fwd("gelu", ys5)
    gp = matmul("glu", y, w['s5_w_glu'], 'nn', bias=w['s5_b_glu'])
    a_out = glu_gate("glu_gate", y, gp)
    cat = jnp.concatenate([a_out, b_out], axis=1)
    mix0 = matmul("out_ab", cat, w['w_out_ab'], 'nn')
    h1, xh1, rs1 = ln_fwd("ln_mix0", h0, mix0, row(w['ln_mix_g'][0]), row(w['ln_mix_b'][0]))
    hid0 = matmul("up0", h1, w['mlp_w_up'][0], 'nn', bias=row(w['mlp_b_up'][0]), act='relu2')
    dn0 = matmul("down0", hid0, w['mlp_w_down'][0], 'nn', bias=row(w['mlp_b_down'][0]))
    h2, xh2, rs2 = ln_fwd("ln_mlp0", h1, dn0, row(w['ln_mlp_g'][0]), row(w['ln_mlp_b'][0]))

    projc = matmul("in_c", h2, w['w_in_c'], 'nn')
    lb, lb_vjp = jax.vjp(_lower_bound, w['hgrn_gamma'])
    ng = w['hgrn_norm_g']
    o, og, states = hgrn_fwd(projc, lb, ng, nseq, lp)
    mix1 = matmul("out_c", og, w['w_out_c'], 'nn')
    h3, xh3, rs3 = ln_fwd("ln_mix1", h2, mix1, row(w['ln_mix_g'][1]), row(w['ln_mix_b'][1]))
    hid1 = matmul("up1", h3, w['mlp_w_up'][1], 'nn', bias=row(w['mlp_b_up'][1]), act='relu2')
    dn1 = matmul("down1", hid1, w['mlp_w_down'][1], 'nn', bias=row(w['mlp_b_down'][1]))
    h4, xh4, rs4 = ln_fwd("ln_mlp1", h3, dn1, row(w['ln_mlp_g'][1]), row(w['ln_mlp_b'][1]))
    dy, loss = loss_head("loss", h4, tgt_p, lp, seq)

    def mlp_bwd(l, dh_out, xh_out, rs_out, hid, h_in):
        dpre, dg_, db_, dbd = ln_bwd(f"ln_mlp{l}_b", dh_out, xh_out, rs_out, row(w['ln_mlp_g'][l]))
        dpu = matmul(f"down{l}_dx", dpre, w['mlp_w_down'][l], 'nt', relu2_of=hid)
        dwd = matmul(f"down{l}_dw", hid, dpre, 'tn')
        dh_in = matmul(f"up{l}_dx", dpu, w['mlp_w_up'][l], 'nt', add=dpre, add_scale=ALPHA)
        dwu, dbu = matmul(f"up{l}_dw", h_in, dpu, 'tn', out_slots=N_CHIPS, colsum=True)
        return dh_in, dict(ln_mlp_g=dg_, ln_mlp_b=db_, mlp_b_down=dbd, mlp_b_up=dbu, mlp_w_up=dwu, mlp_w_down=dwd)

    dh3, gm1 = mlp_bwd(1, dy, xh4, rs4, hid1, h3)
    dpre, dg1, db1, _ = ln_bwd("ln_mix1_b", dh3, xh3, rs3, row(w['ln_mix_g'][1]))
    g['w_out_c'] = matmul("out_c_dw", og, dpre, 'tn')
    dog = matmul("out_c_dx", dpre, w['w_out_c'], 'nt')
    dq, df, di, dgg, dlb, dng = hgrn_bwd(projc, o, dog, states, lb, ng, nseq, lp)
    dprojc = jnp.concatenate([dq, df, di, dgg], axis=1)
    dh2 = matmul("in_c_dx", dprojc, w['w_in_c'], 'nt', add=dpre, add_scale=ALPHA)
    g['w_in_c'] = matmul("in_c_dw", h2, dprojc, 'tn', out_slots=N_CHIPS)
    g['hgrn_gamma'] = lb_vjp(jnp.sum(dlb, axis=0))[0]
    g['hgrn_norm_g'] = jnp.sum(dng, axis=0)

    dh1, gm0 = mlp_bwd(0, dh2, xh2, rs2, hid0, h1)
    dpre, dg0, db0, _ = ln_bwd("ln_mix0_b", dh1, xh1, rs1, row(w['ln_mix_g'][0]))
    g['w_out_ab'] = matmul("out_ab_dw", cat, dpre, 'tn')
    dcat = matmul("out_ab_dx", dpre, w['w_out_ab'], 'nt')
    dgp, da_s, dbglu = glu_gate_bwd("glu_gate_b", dcat[:, :S5_WIDTH], y, gp)
    g['s5_w_glu'] = matmul("glu_dw", y, dgp, 'tn')
    g['s5_b_glu'] = dbglu
    dy5 = matmul("glu_dx", dgp, w['s5_w_glu'], 'nt', add=da_s)
    dys5 = gelu_bwd("gelu_b", dy5, ys5)
    dug, da1, da2, da3, dare, daim = s5_bwd(_to_groups(dys5), ug, sst, *mats[:3], *s5_pows, nseq)
    for n, v in zip(s5_names, mats_vjp((da1, da2, da3, dare, daim))):
        g[n] = v[None]
    for n in ('mlp_w_up', 'mlp_w_down'):
        g[n] = [gm0[n], gm1[n]]
    res = attn_bwd(proj, sb_tot, sb_first, dcat[:, S5_WIDTH:], nseq, lp, ex=hooks.scatter_early(g) if hooks else None)
    dqa, dka, dva = res[:3]
    if hooks:
        hooks.scattered(res[3:])
    dproj = jnp.concatenate([_from_groups(dug), dqa, dka, dva], axis=1)
    dh0 = matmul("in_ab_dx", dproj, w['w_in_ab'], 'nt', add=dpre, add_scale=ALPHA)
    g['w_in_ab'] = matmul("in_ab_dw", h0, dproj, 'tn', out_slots=N_CHIPS)

    dh0 = dh0.reshape(nseq, lp, D_MODEL)
    grad_x = dh0[:, N_META:N_META + seq]
    g['meta'] = jnp.sum(dh0[:, :N_META], axis=0)
    g['ln_mix_g'] = jnp.concatenate([dg0, dg1], axis=0)
    g['ln_mix_b'] = jnp.concatenate([db0, db1], axis=0)
    for n in ('ln_mlp_g', 'ln_mlp_b', 'mlp_b_down', 'mlp_b_up'):
        g[n] = jnp.concatenate([gm0[n], gm1[n]], axis=0)
    return loss, grad_x, g


class Exchange:
    def __init__(self, ins, out_shapes, n_remote, build, in_place):
        self.ins, self.out_shapes, self.n_remote, self.build, self.in_place = ins, out_shapes, n_remote, build, in_place

    def sems(self):
        return [pltpu.SemaphoreType.DMA((self.n_remote,)), pltpu.SemaphoreType.DMA((self.n_remote,))]

    def copies(self, in_refs, out_refs, ssem, rsem):
        me = (lax.axis_index("x"), lax.axis_index("y"), lax.axis_index("c"))
        out, k = [], 0
        for phase in self.build(in_refs, out_refs, me):
            out.append([pltpu.make_async_remote_copy(src_ref=s, dst_ref=d, send_sem=ssem.at[k + i],
                                                     recv_sem=rsem.at[k + i], device_id=p, device_id_type=MESH)
                        for i, (s, d, p) in enumerate(phase)])
            k += len(phase)
        return out

    def start(self, in_refs, out_refs, ssem, rsem):
        for cp in self.copies(in_refs, out_refs, ssem, rsem)[0]:
            cp.start()

    def finish(self, in_refs, out_refs, ssem, rsem):
        phases = self.copies(in_refs, out_refs, ssem, rsem)
        for cp in phases[0]:
            cp.wait()
        for phase in phases[1:]:
            for cp in phase:
                cp.start()
            for cp in phase:
                cp.wait()


def _exchange(name, ex):
    ni, no = len(ex.ins), len(ex.out_shapes)

    def body(*refs):
        in_refs, out_refs, sems = refs[:ni], refs[ni:ni + no], refs[ni + no:]
        ex.start(in_refs, out_refs, *sems)
        ex.finish(in_refs, out_refs, *sems)

    anyspec = pl.BlockSpec(memory_space=pl.ANY)
    return pl.pallas_call(
        body, name=name, in_specs=[anyspec] * ni, out_specs=[anyspec] * no, out_shape=ex.out_shapes,
        input_output_aliases={i: i for i in range(ni if ex.in_place else 0)}, scratch_shapes=ex.sems())(*ex.ins)


def _chip_peers(me):
    x, y, c = me
    return [(x ^ (m >> 1), y ^ (m & 1), c) for m in (1, 2, 3)]


def _half(ref, c, axis):
    h = ref.shape[axis] // 2
    idx = [slice(None)] * axis + [pl.ds(c * h, h)]
    return ref.at[tuple(idx)]


def _like(arrs):
    return [jax.ShapeDtypeStruct(a.shape, a.dtype) for a in arrs]


def gather_over_chips(bufs):
    def build(in_refs, out_refs, me):
        x, y, c = me
        j = 2 * x + y
        sib = (x, y, 1 - c)
        ici = [(_half(o.at[j], c, 0), _half(o.at[j], c, 0), p) for o in out_refs for p in _chip_peers(me)]
        d2d = [(_half(o.at[2 * p[0] + p[1]], c, 0), _half(o.at[2 * p[0] + p[1]], c, 0), sib)
               for o in out_refs for p in _chip_peers(me)]
        return [ici, d2d]
    return Exchange(bufs, _like(bufs), 6 * len(bufs), build, True)


def fold_cores(fulls):
    def build(in_refs, out_refs, me):
        x, y, c = me
        return [[(_half(s, 1 - c, 1), o, (x, y, 1 - c)) for s, o in zip(in_refs, out_refs)]]
    shapes = [jax.ShapeDtypeStruct((f.shape[0], f.shape[1] // 2, f.shape[2]), f.dtype) for f in fulls]
    return Exchange(fulls, shapes, len(fulls), build, False)


def scatter_over_chips(parts):
    def build(in_refs, out_refs, me):
        j = 2 * me[0] + me[1]
        return [[(s.at[2 * p[0] + p[1]], o.at[j], p) for s, o in zip(in_refs, out_refs) for p in _chip_peers(me)]]
    return Exchange(parts, _like(parts), 3 * len(parts), build, False)


def join_cores(bufs):
    def build(in_refs, out_refs, me):
        x, y, c = me
        return [[(o.at[c], o.at[c], (x, y, 1 - c)) for o in out_refs]]
    return Exchange(bufs, _like(bufs), len(bufs), build, True)


def gather_over_devices(buf):
    def build(in_refs, out_refs, me):
        x, y, c = me
        i = 4 * x + 2 * y + c
        out = out_refs[0]
        return [[(out.at[i], out.at[i], (x ^ (m >> 2), y ^ ((m >> 1) & 1), c ^ (m & 1))) for m in range(1, N_DEV)]]
    return Exchange([buf], _like([buf]), N_DEV - 1, build, True)


def _slot_call(name, body, scalars, ins, in_maps, out_shape, out_map, blk, grid):
    gs = pltpu.PrefetchScalarGridSpec(
        num_scalar_prefetch=1, grid=grid,
        in_specs=[pl.BlockSpec((None,) + blk, m) for m in in_maps],
        out_specs=pl.BlockSpec((None,) + blk, out_map))
    return pl.pallas_call(body, name=name, grid_spec=gs, out_shape=out_shape,
                          compiler_params=_params(("arbitrary",) * len(grid)))(scalars, *ins)


def cast_into_slot(name, w2d, chip, dtype):
    r, wd = w2d.shape
    tm = _pick(r, (512, 256, 128, 64, 32, 16))

    def body(s_ref, w_ref, o_ref):
        o_ref[...] = w_ref[...].astype(o_ref.dtype)

    gs = pltpu.PrefetchScalarGridSpec(
        num_scalar_prefetch=1, grid=(r // tm,),
        in_specs=[pl.BlockSpec((tm, wd), lambda i, s: (i, 0))],
        out_specs=pl.BlockSpec((None, tm, wd), lambda i, s: (s[0], i, 0)))
    return pl.pallas_call(body, name=name, grid_spec=gs,
                          out_shape=jax.ShapeDtypeStruct((N_CHIPS, r, wd), dtype),
                          compiler_params=_params(("arbitrary",)))(chip.reshape(1), w2d)


def sum_cores(name, full, theirs, core):
    s, r, wd = full.shape
    h = r // 2
    tm = _pick(h, (512, 256, 128, 64, 32, 16))
    nt = h // tm

    def body(s_ref, a_ref, b_ref, o_ref):
        o_ref[...] = (a_ref[...] + b_ref[...]).astype(o_ref.dtype)

    return _slot_call(name, body, core.reshape(1), [full, theirs],
                      [lambda k, i, s_: (k, s_[0] * nt + i, 0), lambda k, i, s_: (k, i, 0)],
                      jax.ShapeDtypeStruct((s, h, wd), BF16), lambda k, i, s_: (k, i, 0), (tm, wd), (s, nt))


def sum_chips(name, own, recv, chip, core):
    s, r, wd = own.shape
    tm = _pick(r, (512, 256, 128, 64, 32, 16))

    def body(s_ref, a_ref, b1_ref, b2_ref, b3_ref, o_ref):
        acc = a_ref[...].astype(F32)
        for ref in (b1_ref, b2_ref, b3_ref):
            acc = acc + ref[...].astype(F32)
        o_ref[...] = acc

    maps = [lambda i, s_: (s_[0], i, 0)]
    maps += [functools.partial(lambda i, s_, m: (s_[0] ^ m, i, 0), m=m) for m in (1, 2, 3)]
    return _slot_call(name, body, jnp.stack([chip, core]), [own, recv, recv, recv], maps,
                      jax.ShapeDtypeStruct((2, r, wd), F32), lambda i, s_: (s_[1], i, 0), (tm, wd), (r // tm,))


def sum_slots(name, arr):
    s, r, wd = arr.shape
    tm = _pick(r, (512, 256, 128, 64, 32, 16, 8))

    def body(*refs):
        acc = refs[0][...].astype(F32)
        for ref in refs[1:s]:
            acc = acc + ref[...].astype(F32)
        refs[s][...] = acc

    specs = [pl.BlockSpec((None, tm, wd), functools.partial(lambda i, k: (k, i, 0), k=k)) for k in range(s)]
    return pl.pallas_call(body, name=name, grid=(r // tm,), in_specs=specs,
                          out_specs=pl.BlockSpec((tm, wd), lambda i: (i, 0)),
                          out_shape=jax.ShapeDtypeStruct((r, wd), F32),
                          compiler_params=_params(("parallel",)))(*([arr] * s))


def _pack(arrs):
    flat = jnp.concatenate([a.reshape(-1) for a in arrs])
    n = flat.shape[0]
    padded = -(-n // (512 * LANES)) * (512 * LANES)
    return jnp.pad(flat, (0, padded - n)).reshape(-1, LANES)


def _unpack(packed, shapes):
    flat = packed.reshape(-1)
    out, pos = [], 0
    for s in shapes:
        n = math.prod(s)
        out.append(flat[pos:pos + n].reshape(s))
        pos += n
    return out


def _as2d(a):
    return a.reshape(-1, a.shape[-1])


def kernel(x, meta, w_in_ab, s5_lam_re, s5_lam_im, s5_log_dt, s5_b_re, s5_b_im, s5_c_re, s5_c_im, s5_d, s5_w_glu, s5_b_glu, w_out_ab, w_in_c, hgrn_gamma, hgrn_norm_g, w_out_c, ln_mix_g, ln_mix_b, mlp_w_up, mlp_b_up, mlp_w_down, mlp_b_down, ln_mlp_g, ln_mlp_b, loss_target, m_meta, m_w_in_ab, m_s5_lam_re, m_s5_lam_im, m_s5_log_dt, m_s5_b_re, m_s5_b_im, m_s5_c_re, m_s5_c_im, m_s5_d, m_s5_w_glu, m_s5_b_glu, m_w_out_ab, m_w_in_c, m_hgrn_gamma, m_hgrn_norm_g, m_w_out_c, m_ln_mix_g, m_ln_mix_b, m_mlp_w_up, m_mlp_b_up, m_mlp_w_down, m_mlp_b_down, m_ln_mlp_g, m_ln_mlp_b, v_meta, v_w_in_ab, v_s5_lam_re, v_s5_lam_im, v_s5_log_dt, v_s5_b_re, v_s5_b_im, v_s5_c_re, v_s5_c_im, v_s5_d, v_s5_w_glu, v_s5_b_glu, v_w_out_ab, v_w_in_c, v_hgrn_gamma, v_hgrn_norm_g, v_w_out_c, v_ln_mix_g, v_ln_mix_b, v_mlp_w_up, v_mlp_b_up, v_mlp_w_down, v_mlp_b_down, v_ln_mlp_g, v_ln_mlp_b):
    given = dict(locals())
    wts = {n: given[n] for n in WEIGHTS}
    ms = {n: given['m_' + n] for n in WEIGHTS}
    vs = {n: given['v_' + n] for n in WEIGHTS}
    chip = 2 * lax.axis_index("x") + lax.axis_index("y")

    core = lax.axis_index("c")
    parts = [(n, l) for n in BIG for l in (range(DEPTH) if n.startswith('mlp_') else [0])]
    tags = [f"{n}{l}" for n, l in parts]
    shards = {p: cast_into_slot("cast_" + t, wts[p[0]][p[1]], chip, BF16) for t, p in zip(tags, parts)}
    small_sh = jnp.concatenate([meta, hgrn_norm_g, jnp.zeros((15, meta.shape[1]), F32)], axis=0)
    first, late = parts[:1], parts[1:]
    w_in_ab_all, sm = _exchange("gather_first", gather_over_chips(
        [shards[p] for p in first] + [cast_into_slot("cast_small", small_sh, chip, F32)]))
    w = {n: wts[n] for n in SMALL}
    w['meta'] = sm[:, :N_META].transpose(1, 0, 2).reshape(N_META, D_MODEL)
    w['hgrn_norm_g'] = sm[:, N_META:N_META + 1].transpose(1, 0, 2).reshape(1, D_MODEL)
    w['w_in_ab'] = w_in_ab_all

    def slot_major(v):
        return v if v.ndim == 3 else v.reshape(N_CHIPS, -1, v.shape[-1])

    def chip_sums_of(ps, g, tag):
        gfull = [slot_major(g[n][l] if n.startswith('mlp_') else g[n]) for n, l in ps]
        theirs = _exchange("fold_grads" + tag, fold_cores(gfull))
        return [sum_cores(f"sum_cores_{n}{l}", a, b, core) for (n, l), a, b in zip(ps, gfull, theirs)]

    class Hooks:
        gather_late = gather_over_chips([shards[p] for p in late])

        @staticmethod
        def late_weights(filled):
            fw = dict(zip(late, filled))
            return {'s5_w_glu': fw['s5_w_glu', 0].reshape(S5_WIDTH, S5_WIDTH),
                    'w_out_ab': fw['w_out_ab', 0].reshape(D_MODEL, D_MODEL),
                    'w_in_c': fw['w_in_c', 0],
                    'w_out_c': fw['w_out_c', 0].reshape(D_MODEL, D_MODEL),
                    'mlp_w_up': [fw['mlp_w_up', l] for l in range(DEPTH)],
                    'mlp_w_down': [fw['mlp_w_down', l].reshape(D_FF, D_MODEL) for l in range(DEPTH)]}

        @staticmethod
        def scatter_early(g):
            Hooks.sums = chip_sums_of(late, g, "_early")
            return scatter_over_chips(Hooks.sums)

        @staticmethod
        def scattered(recv):
            Hooks.recv = list(recv)

    loss, grad_x, g = local_step(x, loss_target, w, Hooks)

    sums = chip_sums_of(first, g, "_last")
    recv = _exchange("scatter_last", scatter_over_chips(sums))
    reduced = _exchange("join_grads", join_cores(
        [sum_chips(f"sum_chips_{n}{l}", a, b, chip, core)
         for (n, l), a, b in zip(first + late, sums + Hooks.sums, list(recv) + Hooks.recv)]))
    reduced = dict(zip(first + late, [_as2d(r) for r in reduced]))

    small_shapes = [(LANES,)] + [g[n].shape for n in SMALL]
    packed = _pack([loss.reshape(-1)] + [g[n] for n in SMALL])
    slots = lax.dynamic_update_slice(jnp.zeros((N_DEV,) + packed.shape, F32), packed[None],
                                     (2 * chip + core, 0, 0))
    red = sum_slots("sum_small", _exchange("gather_small", gather_over_devices(slots))[0])
    red = _unpack(red, small_shapes)
    loss_out = red[0][0]
    gs = dict(zip(SMALL, red[1:]))
    gs['meta'] = lax.dynamic_slice_in_dim(gs['meta'], chip * meta.shape[1], meta.shape[1], axis=1)
    gs['hgrn_norm_g'] = lax.dynamic_slice_in_dim(gs['hgrn_norm_g'].reshape(1, -1), chip * meta.shape[1],
                                                 meta.shape[1], axis=1)

    grads, deltas, new_m, new_v = {}, {}, {}, {}
    for n in BIG:
        r = jnp.concatenate([reduced[n, l] for l in range(wts[n].shape[0])], axis=0)
        res = adamw("adamw_" + n, _as2d(wts[n]), [r], _as2d(ms[n]), _as2d(vs[n]))
        grads[n], deltas[n], new_m[n], new_v[n] = [r.reshape(wts[n].shape) for r in res]
    shapes = [wts[n].shape for n in SMALL]
    res = adamw("adamw_small", _pack([wts[n] for n in SMALL]), [_pack([gs[n] for n in SMALL])],
                _pack([ms[n] for n in SMALL]), _pack([vs[n] for n in SMALL]))
    for d, r in zip((grads, deltas, new_m, new_v), res):
        d.update(zip(SMALL, _unpack(r, shapes)))
    return (loss_out, grad_x, *[grads[n] for n in WEIGHTS], *[deltas[n] for n in WEIGHTS],
            *[new_m[n] for n in WEIGHTS], *[new_v[n] for n in WEIGHTS])
```

```python
import functools
import math

import jax
import jax.numpy as jnp
from jax import lax
from jax.experimental import pallas as pl
from jax.experimental.pallas import tpu as pltpu

F32 = jnp.float32
BF16 = jnp.bfloat16

D_MODEL = 1024
N_META = 16
DEPTH = 2
S5_WIDTH = 512
S5_GROUP = 16
S5_GROUPS = 32
S5_STATE = 64
S5_CHUNK = 16
SB_WIDTH = 512
SB_BLOCK = 128
SB_DEAD = -110.0
HG_HEADS = 8
HG_DK = 128
HG_CHUNK = 64
HG_UNROLL = 2
D_FF = 4096
ALPHA = (2.0 * DEPTH) ** 0.25
LN_EPS = 1e-5
RMS_EPS = 1e-6
ADAM_LR = 0.001
ADAM_B1 = 0.9
ADAM_B2 = 0.999
ADAM_EPS = 1e-08
ADAM_WD = 0.01
ADAM_STEP = 10
N_CHIPS = 4
N_DEV = 8
LANES = 128
MESH = pl.DeviceIdType.MESH
VMEM_LIMIT = 56 * 1024 * 1024

WEIGHTS = ['meta', 'w_in_ab', 's5_lam_re', 's5_lam_im', 's5_log_dt', 's5_b_re', 's5_b_im', 's5_c_re', 's5_c_im',
           's5_d', 's5_w_glu', 's5_b_glu', 'w_out_ab', 'w_in_c', 'hgrn_gamma', 'hgrn_norm_g', 'w_out_c',
           'ln_mix_g', 'ln_mix_b', 'mlp_w_up', 'mlp_b_up', 'mlp_w_down', 'mlp_b_down', 'ln_mlp_g', 'ln_mlp_b']
BIG = ['w_in_ab', 's5_w_glu', 'w_out_ab', 'w_in_c', 'w_out_c', 'mlp_w_up', 'mlp_w_down']
SHARDED_SMALL = ['meta', 'hgrn_norm_g']
SMALL = [n for n in WEIGHTS if n not in BIG]


def _params(sem=None):
    return pltpu.CompilerParams(dimension_semantics=sem, vmem_limit_bytes=VMEM_LIMIT)


def _pick(n, cands):
    for c in cands:
        if n % c == 0:
            return c
    return n


def _dot(a, b, ca, cb):
    return lax.dot_general(a, b, (((ca,), (cb,)), ((), ())), preferred_element_type=F32)


def _split3(x):
    hi = x.astype(BF16)
    r = x - hi.astype(F32)
    mid = r.astype(BF16)
    lo = (r - mid.astype(F32)).astype(BF16)
    return hi, mid, lo


def _dot_exact_rhs(x, m, cx, cm):
    hi = x.astype(BF16)
    lo = (x - hi.astype(F32)).astype(BF16)
    return _dot(hi, m, cx, cm) + _dot(lo, m, cx, cm)


def _dot3(a, b, ca, cb):
    ah = a.astype(BF16)
    al = (a - ah.astype(F32)).astype(BF16)
    bh = b.astype(BF16)
    bl = (b - bh.astype(F32)).astype(BF16)
    return _dot(ah, bh, ca, cb) + (_dot(ah, bl, ca, cb) + _dot(al, bh, ca, cb))


def rowwise(name, fn, row_ins, bc_ins, out_widths, sum_widths=(), out_dtypes=None, tm=None):
    t = row_ins[0].shape[0]
    if tm is None:
        wmax = max([a.shape[1] for a in row_ins] + list(out_widths))
        tm = _pick(t, (272, 256, 128, 64, 16, 8)) if wmax > 1024 else _pick(t, (544, 512, 256, 128, 64, 16, 8))
    nr, nb, no, ns = len(row_ins), len(bc_ins), len(out_widths), len(sum_widths)
    out_dtypes = out_dtypes or [F32] * no

    def body(*refs):
        i = pl.program_id(0)
        rows = [r[...] for r in refs[:nr]]
        bcs = [r[...] for r in refs[nr:nr + nb]]
        outs, sums = fn(i, tm, rows, bcs)
        for r, v in zip(refs[nr + nb:nr + nb + no], outs):
            r[...] = v.astype(r.dtype)
        if ns:
            srefs = refs[nr + nb + no:]

            @pl.when(i == 0)
            def _():
                for r in srefs:
                    r[...] = jnp.zeros(r.shape, r.dtype)

            for r, v in zip(srefs, sums):
                r[...] += v

    in_specs = [pl.BlockSpec((tm, a.shape[1]), lambda i: (i, 0)) for a in row_ins]
    in_specs += [pl.BlockSpec(a.shape, lambda i: (0, 0)) for a in bc_ins]
    out_specs = [pl.BlockSpec((tm, w), lambda i: (i, 0)) for w in out_widths]
    out_specs += [pl.BlockSpec((1, w), lambda i: (0, 0)) for w in sum_widths]
    out_shape = [jax.ShapeDtypeStruct((t, w), dt) for w, dt in zip(out_widths, out_dtypes)]
    out_shape += [jax.ShapeDtypeStruct((1, w), F32) for w in sum_widths]
    res = pl.pallas_call(body, name=name, grid=(t // tm,), in_specs=in_specs, out_specs=out_specs,
                         out_shape=out_shape, compiler_params=_params(("arbitrary",)))(*row_ins, *bc_ins)
    return res


def _colsum(v):
    return jnp.sum(v, axis=0, keepdims=True)


def matmul(name, a, b, mode, *, bias=None, act=None, add=None, add_scale=1.0, relu2_of=None, colsum=False,
           out_dtype=F32, out_slots=0, tm=None, tn=None, tk=None):
    slotted = b.ndim == 3
    if mode == 'nn':
        m, k = a.shape
        n = b.shape[-1] * (b.shape[0] if slotted else 1)
    elif mode == 'nt':
        m, k = a.shape
        n = b.shape[-2]
    else:
        k, m = a.shape
        n = b.shape[-1]
    ns = b.shape[-1] if slotted else None
    if mode == 'tn':
        tm = tm or _pick(m, (512, 256, 128))
        tn = tn or _pick(n if not out_slots else n // out_slots, (1024, 512, 256, 128))
        tk = tk or _pick(k, (1088, 1024, 768, 512, 384, 256, 128))
    else:
        tm = tm or _pick(m, (1088, 1024, 768, 512, 384, 256, 128))
        tn = tn or _pick(n if not (slotted and mode == 'nn') else ns, (1024, 512, 256, 128))
        tk = tk or _pick(k if not (slotted and mode == 'nt') else ns, (1024, 512, 256, 128))
    gm, gn, gk = m // tm, n // tn, k // tk

    if mode == 'nn':
        a_spec = pl.BlockSpec((tm, tk), lambda i, j, kk: (i, kk))
        if slotted:
            per = ns // tn
            b_spec = pl.BlockSpec((None, tk, tn), lambda i, j, kk: (j // per, kk, j % per))
        else:
            b_spec = pl.BlockSpec((tk, tn), lambda i, j, kk: (kk, j))
        ca, cb = 1, 0
    elif mode == 'nt':
        a_spec = pl.BlockSpec((tm, tk), lambda i, j, kk: (i, kk))
        if slotted:
            per = ns // tk
            b_spec = pl.BlockSpec((None, tn, tk), lambda i, j, kk: (kk // per, j, kk % per))
        else:
            b_spec = pl.BlockSpec((tn, tk), lambda i, j, kk: (j, kk))
        ca, cb = 1, 1
    else:
        a_spec = pl.BlockSpec((tk, tm), lambda i, j, kk: (kk, i))
        b_spec = pl.BlockSpec((tk, tn), lambda i, j, kk: (kk, j))
        ca, cb = 0, 0
    in_specs = [a_spec, b_spec]
    args = [a, b]
    if bias is not None:
        in_specs.append(pl.BlockSpec((1, tn), lambda i, j, kk: (0, j)))
        args.append(bias)
    if add is not None:
        in_specs.append(pl.BlockSpec((tm, tn), lambda i, j, kk: (i, j)))
        args.append(add)
    if relu2_of is not None:
        in_specs.append(pl.BlockSpec((tm, tn), lambda i, j, kk: (i, j)))
        args.append(relu2_of)
    if out_slots:
        per_o = (n // out_slots) // tn
        out_spec = pl.BlockSpec((None, tm, tn), lambda i, j, kk: (j // per_o, i, j % per_o))
        out_shape = jax.ShapeDtypeStruct((out_slots, m, n // out_slots), out_dtype)
    else:
        out_spec = pl.BlockSpec((tm, tn), lambda i, j, kk: (i, j))
        out_shape = jax.ShapeDtypeStruct((m, n), out_dtype)
    grid = (gm, gn, gk)
    if colsum:
        assert mode == 'tn'
        out_spec = [out_spec, pl.BlockSpec((1, tn), lambda i, j, kk: (0, j))]
        out_shape = [out_shape, jax.ShapeDtypeStruct((1, n), F32)]

        def swapped(spec):
            return pl.BlockSpec(spec.block_shape, functools.partial(lambda j, i, kk, f: f(i, j, kk), f=spec.index_map))
        in_specs = [swapped(sp) for sp in in_specs]
        out_spec = [swapped(sp) for sp in out_spec]
        grid = (gn, gm, gk)

    def body(*refs):
        a_ref, b_ref = refs[0], refs[1]
        pos = 2
        bias_ref = add_ref = hid_ref = cs_ref = None
        if bias is not None:
            bias_ref = refs[pos]
            pos += 1
        if add is not None:
            add_ref = refs[pos]
            pos += 1
        if relu2_of is not None:
            hid_ref = refs[pos]
            pos += 1
        o_ref = refs[pos]
        pos += 1
        if colsum:
            cs_ref = refs[pos]
            pos += 1
        acc_ref = refs[pos] if gk > 1 else None
        bt = b_ref[...]
        p = _dot(a_ref[...].astype(BF16), bt.astype(BF16), ca, cb)
        kk = pl.program_id(2)

        if colsum:
            @pl.when(jnp.logical_and(pl.program_id(1) == 0, kk == 0))
            def _():
                cs_ref[...] = _colsum(bt.astype(F32))

            @pl.when(jnp.logical_and(pl.program_id(1) == 0, kk > 0))
            def _():
                cs_ref[...] += _colsum(bt.astype(F32))

        def finish(r):
            if bias_ref is not None:
                r = r + bias_ref[...]
            if act == 'relu2':
                r = jnp.square(jnp.maximum(r, 0.0))
            if hid_ref is not None:
                r = r * (2.0 * jnp.sqrt(hid_ref[...].astype(F32)))
            if add_ref is not None:
                r = r + add_scale * add_ref[...]
            o_ref[...] = r.astype(o_ref.dtype)

        if gk == 1:
            finish(p)
        else:
            @pl.when(kk == 0)
            def _():
                acc_ref[...] = p

            @pl.when(kk > 0)
            def _():
                acc_ref[...] += p

            @pl.when(kk == gk - 1)
            def _():
                finish(acc_ref[...])

    scratch = [pltpu.VMEM((tm, tn), F32)] if gk > 1 else []
    sem = ("arbitrary",) * 3 if colsum else ("parallel", "parallel", "arbitrary")
    return pl.pallas_call(body, name=name, grid=grid, in_specs=in_specs, out_specs=out_spec,
                          out_shape=out_shape, scratch_shapes=scratch, compiler_params=_params(sem))(*args)


def ln_fwd(name, h, mix, g, b):
    def fn(i, tm, rows, bcs):
        pre = ALPHA * rows[0] + rows[1]
        mu = jnp.mean(pre, axis=1, keepdims=True)
        xc = pre - mu
        var = jnp.mean(xc * xc, axis=1, keepdims=True)
        rstd = lax.rsqrt(var + LN_EPS)
        xhat = xc * rstd
        return (xhat * bcs[0] + bcs[1], xhat, rstd), ()
    return rowwise(name, fn, [h, mix], [g, b], [D_MODEL, D_MODEL, 1])


def ln_bwd(name, dy, xhat, rstd, g):
    def fn(i, tm, rows, bcs):
        dy_, xh, rs = rows
        dyg = dy_ * bcs[0]
        m1 = jnp.mean(dyg, axis=1, keepdims=True)
        m2 = jnp.mean(dyg * xh, axis=1, keepdims=True)
        dpre = rs * (dyg - m1 - xh * m2)
        return (dpre,), (_colsum(dy_ * xh), _colsum(dy_), _colsum(dpre))
    return rowwise(name, fn, [dy, xhat, rstd], [g], [D_MODEL], [D_MODEL] * 3)


_GELU_C = math.sqrt(2.0 / math.pi)


def gelu_fwd(name, x):
    def fn(i, tm, rows, bcs):
        v = rows[0]
        u = _GELU_C * (v + 0.044715 * (v * v * v))
        return (0.5 * v * (1.0 + jnp.tanh(u)),), ()
    return rowwise(name, fn, [x], [], [x.shape[1]])[0]


def gelu_bwd(name, dy, x):
    def fn(i, tm, rows, bcs):
        v = rows[1]
        th = jnp.tanh(_GELU_C * (v + 0.044715 * (v * v * v)))
        dg = 0.5 * (1.0 + th) + 0.5 * v * (1.0 - th * th) * (_GELU_C * (1.0 + 3.0 * 0.044715 * v * v))
        return (rows[0] * dg,), ()
    return rowwise(name, fn, [dy, x], [], [x.shape[1]])[0]


def glu_gate(name, y, gp):
    def fn(i, tm, rows, bcs):
        return (rows[0] * jax.nn.sigmoid(rows[1]),), ()
    return rowwise(name, fn, [y, gp], [], [y.shape[1]], out_dtypes=[BF16])[0]


def glu_gate_bwd(name, da, y, gp):
    def fn(i, tm, rows, bcs):
        s = jax.nn.sigmoid(rows[2])
        dgp = rows[0] * rows[1] * s * (1.0 - s)
        return (dgp, rows[0] * s), (_colsum(dgp),)
    w = y.shape[1]
    return rowwise(name, fn, [da, y, gp], [], [w, w], [w])


def loss_head(name, h, tgt, lp, seq):
    def fn(i, tm, rows, bcs):
        pos = (i * tm + lax.broadcasted_iota(jnp.int32, (tm, 1), 0)) % lp
        valid = jnp.logical_and(pos >= N_META, pos < N_META + seq)
        err = jnp.where(valid, rows[0] - rows[1], 0.0)
        part = 0.5 * jnp.sum(jnp.mean(err * err, axis=1, keepdims=True), axis=0, keepdims=True)
        return (err * (1.0 / D_MODEL),), (jnp.broadcast_to(part, (1, LANES)),)
    return rowwise(name, fn, [h, tgt], [], [D_MODEL], [LANES])


def adamw(name, w, gs, m, v):
    ng = len(gs)

    def fn(i, tm, rows, bcs):
        w_ = rows[0]
        g = rows[1] if ng == 1 else rows[1] + rows[2]
        m_, v_ = rows[1 + ng], rows[2 + ng]
        m_ = ADAM_B1 * m_ + (1.0 - ADAM_B1) * g
        v_ = ADAM_B2 * v_ + (1.0 - ADAM_B2) * jnp.square(g)
        m_hat = m_ / (1.0 - ADAM_B1 ** ADAM_STEP)
        v_hat = v_ / (1.0 - ADAM_B2 ** ADAM_STEP)
        delta = -ADAM_LR * (m_hat / (jnp.sqrt(v_hat) + ADAM_EPS) + ADAM_WD * w_)
        return (g, delta, m_, v_), ()
    wd = w.shape[1]
    return rowwise(name, fn, [w] + list(gs) + [m, v], [], [wd] * 4)


def s5_matrices(lam_re, lam_im, log_dt, b_re, b_im, c_re, c_im, d):
    c = S5_CHUNK
    dt = jnp.exp(log_dt)[:, None]
    tau = jnp.arange(c + 1, dtype=F32)[:, None, None]
    mag = jnp.exp(tau * (lam_re * dt)[None])
    ang = tau * (lam_im * dt)[None]
    pw_re, pw_im = mag * jnp.cos(ang), mag * jnp.sin(ang)
    nr, ni = pw_re[1] - 1.0, pw_im[1]
    den = lam_re * lam_re + lam_im * lam_im
    cf_re = (nr * lam_re + ni * lam_im) / den
    cf_im = (ni * lam_re - nr * lam_im) / den
    bb_re = cf_re[:, :, None] * b_re - cf_im[:, :, None] * b_im
    bb_im = cf_re[:, :, None] * b_im + cf_im[:, :, None] * b_re
    cp_re = c_re[None] * pw_re[:, :, None, :] - c_im[None] * pw_im[:, :, None, :]
    cp_im = c_re[None] * pw_im[:, :, None, :] + c_im[None] * pw_re[:, :, None, :]
    hp = lax.Precision.HIGHEST
    kern = (jnp.einsum('tghp,gpk->tghk', cp_re[:c], bb_re, precision=hp)
            - jnp.einsum('tghp,gpk->tghk', cp_im[:c], bb_im, precision=hp))
    ii = jnp.arange(c)
    sel = (ii[None, :, None] - ii[None, None, :] == ii[:, None, None]).astype(F32)
    a1 = jnp.einsum('tghk,tij->gjkih', kern, sel, precision=hp)
    a1 = a1 + jnp.einsum('ij,kh,gh->gjkih', jnp.eye(c, dtype=F32), jnp.eye(S5_GROUP, dtype=F32), d)
    a1 = a1.reshape(S5_GROUPS, c * S5_GROUP, c * S5_GROUP)
    a2 = jnp.concatenate([cp_re[1:], -cp_im[1:]], axis=-1)
    a2 = a2.transpose(1, 3, 0, 2).reshape(S5_GROUPS, 2 * S5_STATE, c * S5_GROUP)
    rv_re, rv_im = pw_re[:c][::-1], pw_im[:c][::-1]
    x_re = rv_re[:, :, :, None] * bb_re[None] - rv_im[:, :, :, None] * bb_im[None]
    x_im = rv_re[:, :, :, None] * bb_im[None] + rv_im[:, :, :, None] * bb_re[None]
    a3 = jnp.concatenate([x_re, x_im], axis=2)
    a3 = a3.transpose(1, 0, 3, 2).reshape(S5_GROUPS, c * S5_GROUP, 2 * S5_STATE)
    are = jnp.concatenate([pw_re[c], pw_re[c]], axis=-1)[:, None, :]
    aim = jnp.concatenate([-pw_im[c], pw_im[c]], axis=-1)[:, None, :]
    return a1, a2, a3, are, aim


S5_LEVELS = 8


def s5_scan_powers(lam_re, lam_im, log_dt):
    dt = jnp.exp(log_dt)[:, None]
    e = (S5_CHUNK * 2.0 ** jnp.arange(S5_LEVELS, dtype=F32))[:, None, None]
    mag = jnp.exp(e * (lam_re * dt)[None])
    ang = e * (lam_im * dt)[None]
    pr, pi = mag * jnp.cos(ang), mag * jnp.sin(ang)
    pre = jnp.concatenate([pr, pr], axis=-1).transpose(1, 0, 2)
    pim = jnp.concatenate([-pi, pi], axis=-1).transpose(1, 0, 2)
    return pre, pim


def _s5_scan(x, pre, pim, reverse):
    n = x.shape[0]
    rowi = lax.broadcasted_iota(jnp.int32, (n, 1), 0)
    t = x
    for k in range(S5_LEVELS):
        shift = 2 ** k
        if shift >= n:
            break
        p_re, p_im = pre[k:k + 1, :], pim[k:k + 1, :]
        if reverse:
            far = jnp.where(rowi < n - shift, pltpu.roll(t, n - shift, 0), 0.0)
            t = t + (p_re * far + pltpu.roll(p_im * far, S5_STATE, 1))
        else:
            far = jnp.where(rowi >= shift, pltpu.roll(t, shift, 0), 0.0)
            t = t + (p_re * far + p_im * pltpu.roll(far, S5_STATE, 1))
    return t


def s5_fwd(ug, a1, a2, a3, pre, pim, nseq):
    g, nc, cw = ug.shape
    per = nc // nseq
    sw = 2 * S5_STATE
    assert per <= 2 ** S5_LEVELS

    def body(u_ref, a1_ref, a2_ref, a3_ref, pre_ref, pim_ref, y_ref, s_ref):
        u = u_ref[...]
        x = _dot3(u, a3_ref[...], 1, 0)
        first = lax.broadcasted_iota(jnp.int32, (per, 1), 0) == 0
        for b in range(nseq):
            t = _s5_scan(x[b * per:(b + 1) * per], pre_ref[...], pim_ref[...], False)
            s_ref[pl.ds(b * per, per), :] = jnp.where(first, 0.0, pltpu.roll(t, 1, 0))
        y_ref[...] = _dot3(u, a1_ref[...], 1, 0) + _dot3(s_ref[...], a2_ref[...], 1, 0)

    def spec(shape):
        return pl.BlockSpec((None,) + shape, lambda i: (i, 0, 0))
    lv = (S5_LEVELS, sw)
    return pl.pallas_call(
        body, name="s5_fwd", grid=(g,),
        in_specs=[spec((nc, cw)), spec((cw, cw)), spec((sw, cw)), spec((cw, sw)), spec(lv), spec(lv)],
        out_specs=[spec((nc, cw)), spec((nc, sw))],
        out_shape=[jax.ShapeDtypeStruct((g, nc, cw), F32), jax.ShapeDtypeStruct((g, nc, sw), F32)],
        compiler_params=_params(("parallel",)))(ug, a1, a2, a3, pre, pim)


def s5_bwd(dyg, ug, sst, a1, a2, a3, pre, pim, nseq):
    g, nc, cw = ug.shape
    per = nc // nseq
    sw = 2 * S5_STATE

    def body(dy_ref, u_ref, s_ref, a1_ref, a2_ref, a3_ref, pre_ref, pim_ref,
             du_ref, da1_ref, da2_ref, da3_ref, dare_ref, daim_ref, dx_ref):
        dy = dy_ref[...]
        u = u_ref[...]
        gd = _dot3(dy, a2_ref[...], 1, 1)
        s_all = s_ref[...]
        da2_ref[...] = _dot3(s_all, dy, 0, 0)
        last = lax.broadcasted_iota(jnp.int32, (per, 1), 0) == per - 1
        for b in range(nseq):
            gs = _s5_scan(gd[b * per:(b + 1) * per], pre_ref[...], pim_ref[...], True)
            dx_ref[pl.ds(b * per, per), :] = jnp.where(last, 0.0, pltpu.roll(gs, per - 1, 0))
        dx = dx_ref[...]
        dare_ref[...] = _colsum(dx * s_all)
        daim_ref[...] = _colsum(dx * pltpu.roll(s_all, S5_STATE, 1))
        du_ref[...] = _dot3(dy, a1_ref[...], 1, 1) + _dot3(dx, a3_ref[...], 1, 1)
        da1_ref[...] = _dot3(u, dy, 0, 0)
        da3_ref[...] = _dot3(u, dx, 0, 0)

    def spec(shape):
        return pl.BlockSpec((None,) + shape, lambda i: (i, 0, 0))
    sds = jax.ShapeDtypeStruct
    return pl.pallas_call(
        body, name="s5_bwd", grid=(g,),
        in_specs=[spec((nc, cw)), spec((nc, cw)), spec((nc, sw)), spec((cw, cw)), spec((sw, cw)), spec((cw, sw)),
                  spec((S5_LEVELS, sw)), spec((S5_LEVELS, sw))],
        out_specs=[spec((nc, cw)), spec((cw, cw)), spec((sw, cw)), spec((cw, sw)), spec((1, sw)), spec((1, sw))],
        out_shape=[sds((g, nc, cw), F32), sds((g, cw, cw), F32), sds((g, sw, cw), F32), sds((g, cw, sw), F32),
                   sds((g, 1, sw), F32), sds((g, 1, sw), F32)],
        scratch_shapes=[pltpu.VMEM((nc, sw), F32)],
        compiler_params=_params(("parallel",)))(dyg, ug, sst, a1, a2, a3, pre, pim)


def _to_groups(u):
    t = u.shape[0]
    nc = t // S5_CHUNK
    return u.reshape(nc, S5_CHUNK, S5_GROUPS, S5_GROUP).transpose(2, 0, 1, 3).reshape(S5_GROUPS, nc, -1)


def _from_groups(yg):
    g, nc, _ = yg.shape
    return yg.reshape(g, nc, S5_CHUNK, S5_GROUP).transpose(1, 2, 0, 3).reshape(nc * S5_CHUNK, g * S5_GROUP)


def _sb_masks():
    row = lax.broadcasted_iota(jnp.int32, (SB_BLOCK, SB_BLOCK), 0)
    col = lax.broadcasted_iota(jnp.int32, (SB_BLOCK, SB_BLOCK), 1)
    lane = lax.broadcasted_iota(jnp.int32, (1, LANES), 1)
    return row, col, [lane < 64, lane >= 64]


def _sb_weights(z, vis, later_of, after):
    sp = jnp.maximum(z, 0.0) + jnp.log1p(jnp.exp(-jnp.abs(z)))
    lk = jnp.where(vis, -sp, 0.0)
    rs = jnp.sum(lk, axis=1, keepdims=True)
    later = _dot_exact_rhs(lk, after, 1, 0) + later_of(rs)
    lb = z - sp
    w = jnp.where(vis, jnp.exp(lb + later), 0.0)
    return w, rs, lb


def _hidden_exchange(ex, n_in, n_out, grid):
    ni, no = (len(ex.ins), len(ex.out_shapes)) if ex else (0, 0)

    def split(refs):
        a, b = n_in + ni, n_in + ni + n_out
        return refs[:n_in], refs[a:b], (refs[n_in:a], refs[b:b + no], refs[b + no:])

    def at_step(which, fn, parts):
        if ex is not None:
            ids = [pl.program_id(d) == (0 if which == 'first' else g - 1) for d, g in enumerate(grid)]
            cond = ids[0]
            for c in ids[1:]:
                cond = jnp.logical_and(cond, c)
            pl.when(cond)(lambda: fn(parts[0], parts[1], *parts[2]))

    anyspec = pl.BlockSpec(memory_space=pl.ANY)
    kw = dict(in_specs=[anyspec] * ni, out_specs=[anyspec] * no, out_shape=list(ex.out_shapes) if ex else [],
              aliases={n_in + i: n_out + i for i in range(ni if ex and ex.in_place else 0)},
              scratch=ex.sems() if ex else [], ins=list(ex.ins) if ex else [])
    start = functools.partial(at_step, 'first', ex.start if ex else None)
    finish = functools.partial(at_step, 'last', ex.finish if ex else None)
    return split, start, finish, kw


def _lane_tile(x, tl):
    return x[:, tl * LANES:(tl + 1) * LANES]


SB_TILES = 2
_SB_CHAINS = [(tl, hd) for tl in range(SB_TILES) for hd in range(2)]


def _sb_specs(lp, nseq):
    wd = SB_TILES * LANES
    off = [(S5_WIDTH + i * SB_WIDTH) // wd for i in range(3)]
    blk = (lp, wd)
    qkv = [pl.BlockSpec(blk, functools.partial(lambda b, h, o: (b, o + h), o=o)) for o in off]
    return (blk, qkv, pl.BlockSpec(blk, lambda b, h: (b, h)),
            pl.BlockSpec((None, None, 8, LANES), lambda b, h: (b, h, 0, 0)))


def attn_fwd(proj, nseq, lp, ex=None):
    nqb = lp // SB_BLOCK
    t = proj.shape[0]
    nstep = SB_WIDTH // (SB_TILES * LANES)
    split, ex_start, ex_finish, exkw = _hidden_exchange(ex, 3, 3, (nseq, nstep))

    def body(*refs):
        (q_ref, k_ref, v_ref), (o_ref, tot_ref, first_ref), ex_parts = split(refs)
        ex_start(ex_parts)
        row, col, hmask = _sb_masks()
        after = (row > col).astype(BF16)
        tri = col < row
        lane8 = lax.broadcasted_iota(jnp.int32, (8, LANES), 1)

        def qloop(qi, firsts):
            q0 = pl.multiple_of(qi * SB_BLOCK, SB_BLOCK)
            q = q_ref[pl.ds(q0, SB_BLOCK), :] * 0.125
            qm = [jnp.where(hmask[hd], _lane_tile(q, tl), 0.0).astype(BF16) for tl, hd in _SB_CHAINS]

            def alive(carry):
                return jnp.logical_and(carry[0] <= qi, carry[1] > 0)

            def kstep(carry):
                s = carry[0]
                accs, lats = list(carry[2:2 + SB_TILES]), list(carry[2 + SB_TILES:])
                kj = qi - s
                k0 = pl.multiple_of(kj * SB_BLOCK, SB_BLOCK)
                k = k_ref[pl.ds(k0, SB_BLOCK), :].astype(BF16)
                v = v_ref[pl.ds(k0, SB_BLOCK), :]
                vis = jnp.logical_or(kj < qi, tri)
                for c, (tl, hd) in enumerate(_SB_CHAINS):
                    z = _dot(qm[c], _lane_tile(k, tl), 1, 1)
                    w, rs, _ = _sb_weights(z, vis, functools.partial(lambda rs, lat: lat, lat=lats[c]), after)
                    vm = jnp.where(hmask[hd], _lane_tile(v, tl), 0.0).astype(BF16)
                    accs[tl] = accs[tl] + _dot(w.astype(BF16), vm, 1, 0)
                    lats[c] = lats[c] + rs
                top = functools.reduce(jnp.maximum, lats)
                go = (jnp.max(top) > SB_DEAD).astype(jnp.int32)
                return (s + 1, go, *accs, *lats)

            z1 = jnp.zeros((SB_BLOCK, 1), F32)
            res = lax.while_loop(alive, kstep, (jnp.int32(0), jnp.int32(1),
                                                *[jnp.zeros((SB_BLOCK, LANES), F32)] * SB_TILES,
                                                *[z1] * len(_SB_CHAINS)))
            accs, lats = res[2:2 + SB_TILES], res[2 + SB_TILES:]
            for tl in range(SB_TILES):
                cols = slice(tl * LANES, (tl + 1) * LANES)
                o_ref[pl.ds(q0, SB_BLOCK), cols] = accs[tl].astype(o_ref.dtype)
                tot_ref[pl.ds(q0, SB_BLOCK), cols] = jnp.where(hmask[0], lats[2 * tl], lats[2 * tl + 1])
            return jnp.where(lane8 == qi, (qi - res[0] + 1).astype(F32), firsts)

        first_ref[...] = lax.fori_loop(0, nqb, qloop, jnp.zeros((8, LANES), F32))
        ex_finish(ex_parts)

    blk, qkv, out, vec = _sb_specs(lp, nseq)
    return pl.pallas_call(
        body, name="attn_fwd", grid=(nseq, nstep),
        in_specs=qkv + exkw['in_specs'], out_specs=[out, out, vec] + exkw['out_specs'],
        out_shape=[jax.ShapeDtypeStruct((t, SB_WIDTH), BF16), jax.ShapeDtypeStruct((t, SB_WIDTH), F32)]
        + [jax.ShapeDtypeStruct((nseq, nstep, 8, LANES), F32)] + exkw['out_shape'],
        input_output_aliases=exkw['aliases'], scratch_shapes=exkw['scratch'],
        compiler_params=_params(("arbitrary", "arbitrary")))(proj, proj, proj, *exkw['ins'])


def attn_bwd(proj, tot, first, do, nseq, lp, ex=None):
    nqb = lp // SB_BLOCK
    t = proj.shape[0]
    nstep = SB_WIDTH // (SB_TILES * LANES)
    split, ex_start, ex_finish, exkw = _hidden_exchange(ex, 6, 3, (nseq, nstep))

    def body(*refs):
        (q_ref, k_ref, v_ref, tot_ref, first_ref, do_ref), (dq_ref, dk_ref, dv_ref), ex_parts = split(refs)
        ex_start(ex_parts)
        row, col, hmask = _sb_masks()
        after = (row > col).astype(BF16)
        before = (row < col).astype(BF16)
        tri = col < row
        lane8 = lax.broadcasted_iota(jnp.int32, (8, LANES), 1)
        firsts = first_ref[...]
        dk_ref[...] = jnp.zeros(dk_ref.shape, F32)
        dv_ref[...] = jnp.zeros(dv_ref.shape, F32)
        nc = len(_SB_CHAINS)

        def qloop(qi, _):
            first = jnp.max(jnp.where(lane8 == qi, firsts, 0.0)).astype(jnp.int32)
            first = jnp.clip(first, 0, qi)
            q0 = pl.multiple_of(qi * SB_BLOCK, SB_BLOCK)
            q = q_ref[pl.ds(q0, SB_BLOCK), :] * 0.125
            do_ = do_ref[pl.ds(q0, SB_BLOCK), :]
            tot_ = tot_ref[pl.ds(q0, SB_BLOCK), :]
            qm = [jnp.where(hmask[hd], _lane_tile(q, tl), 0.0).astype(BF16) for tl, hd in _SB_CHAINS]
            dom = [jnp.where(hmask[hd], _lane_tile(do_, tl), 0.0).astype(BF16) for tl, hd in _SB_CHAINS]
            tot = [jnp.max(jnp.where(hmask[hd], _lane_tile(tot_, tl), -jnp.inf), axis=1, keepdims=True)
                   for tl, hd in _SB_CHAINS]

            def kloop(kj, carry):
                dqs = list(carry[:SB_TILES])
                pre = list(carry[SB_TILES:SB_TILES + nc])
                gpre = list(carry[SB_TILES + nc:])
                k0 = pl.multiple_of(kj * SB_BLOCK, SB_BLOCK)
                kf = k_ref[pl.ds(k0, SB_BLOCK), :]
                k = kf.astype(BF16)
                v = v_ref[pl.ds(k0, SB_BLOCK), :].astype(BF16)
                vis = jnp.logical_or(kj < qi, tri)
                dk_acc = [jnp.zeros((SB_BLOCK, LANES), F32)] * SB_TILES
                dv_acc = [jnp.zeros((SB_BLOCK, LANES), F32)] * SB_TILES
                for c, (tl, hd) in enumerate(_SB_CHAINS):
                    z = _dot(qm[c], _lane_tile(k, tl), 1, 1)
                    later_of = functools.partial(lambda rs, t_, p_: t_ - p_ - rs, t_=tot[c], p_=pre[c])
                    w, rs, lb = _sb_weights(z, vis, later_of, after)
                    dw = _dot(dom[c], _lane_tile(v, tl), 1, 1)
                    g = dw * w
                    dlk = _dot_exact_rhs(g, before, 1, 0) + gpre[c]
                    sig = jnp.exp(lb)
                    dz = jnp.where(vis, g * (1.0 - sig) - dlk * sig, 0.0).astype(BF16)
                    km = jnp.where(hmask[hd], _lane_tile(kf, tl), 0.0).astype(BF16)
                    dqs[tl] = dqs[tl] + _dot(dz, km, 1, 0)
                    dk_acc[tl] = dk_acc[tl] + _dot(dz, qm[c], 0, 0)
                    dv_acc[tl] = dv_acc[tl] + _dot(w.astype(BF16), dom[c], 0, 0)
                    pre[c] = pre[c] + rs
                    gpre[c] = gpre[c] + jnp.sum(g, axis=1, keepdims=True)
                for tl in range(SB_TILES):
                    cols = slice(tl * LANES, (tl + 1) * LANES)
                    dk_ref[pl.ds(k0, SB_BLOCK), cols] += dk_acc[tl]
                    dv_ref[pl.ds(k0, SB_BLOCK), cols] += dv_acc[tl]
                return (*dqs, *pre, *gpre)

            z1 = jnp.zeros((SB_BLOCK, 1), F32)
            res = lax.fori_loop(first, qi + 1, kloop,
                                (*[jnp.zeros((SB_BLOCK, LANES), F32)] * SB_TILES, *[z1] * (2 * nc)))
            for tl in range(SB_TILES):
                dq_ref[pl.ds(q0, SB_BLOCK), tl * LANES:(tl + 1) * LANES] = res[tl] * 0.125
            return 0

        lax.fori_loop(0, nqb, qloop, 0)
        ex_finish(ex_parts)

    blk, qkv, out, vec = _sb_specs(lp, nseq)
    return pl.pallas_call(
        body, name="attn_bwd", grid=(nseq, nstep),
        in_specs=qkv + [out, vec, out] + exkw['in_specs'], out_specs=[out] * 3 + exkw['out_specs'],
        out_shape=[jax.ShapeDtypeStruct((t, SB_WIDTH), F32)] * 3 + exkw['out_shape'],
        input_output_aliases=exkw['aliases'], scratch_shapes=exkw['scratch'],
        compiler_params=_params(("arbitrary", "arbitrary")))(proj, proj, proj, tot, first, do, *exkw['ins'])


def _hg_loop(nch, step, init):
    assert nch % HG_UNROLL == 0

    def trip(i, carry):
        for u in range(HG_UNROLL):
            carry = step(i * HG_UNROLL + u, carry)
        return carry
    return lax.fori_loop(0, nch // HG_UNROLL, trip, init)


def _hg_gates(fc, lb):
    sg = jax.nn.sigmoid(fc)
    f = lb + (1.0 - lb) * sg
    return sg, f


def _hg_decay(f, incl):
    bcum = _dot_exact_lhs(incl, jnp.log(f))
    return bcum, bcum[HG_CHUNK - 1:HG_CHUNK, :]


def _dot_exact_lhs(m, x):
    hi, mid, lo = _split3(x)
    return _dot(m, hi, 1, 0) + (_dot(m, mid, 1, 0) + _dot(m, lo, 1, 0))


def hgrn_fwd(proj, lb, ng, nseq, lp):
    nch = lp // HG_CHUNK
    t = proj.shape[0]
    c = HG_CHUNK

    def body(q_ref, f_ref, v_ref, g_ref, lb_ref, ng_ref, o_ref, og_ref, st_ref):
        row = lax.broadcasted_iota(jnp.int32, (c, c), 0)
        col = lax.broadcasted_iota(jnp.int32, (c, c), 1)
        causal = col <= row
        incl = causal.astype(BF16)
        lbv, ngv = lb_ref[...], ng_ref[...]

        def step(ci, st):
            r0 = pl.multiple_of(ci * c, c)
            rows = pl.ds(r0, c)
            st_ref[ci] = st
            _, f = _hg_gates(f_ref[rows, :], lbv)
            bcum, blast = _hg_decay(f, incl)
            kk = 1.0 - f
            vc = v_ref[rows, :].astype(BF16)
            qd = (q_ref[rows, :] * jnp.exp(bcum)).astype(BF16)
            kd = (kk * jnp.exp(-bcum)).astype(BF16)
            a = jnp.where(causal, _dot(qd, kd, 1, 1), 0.0)
            o = _dot(a.astype(BF16), vc, 1, 0) + _dot(qd, st.astype(BF16), 1, 1)
            kend = (kk * jnp.exp(blast - bcum)).astype(BF16)
            st = st * jnp.exp(blast) + _dot(vc, kend, 0, 0)
            o_ref[rows, :] = o
            r = lax.rsqrt(jnp.mean(o * o, axis=1, keepdims=True) + RMS_EPS)
            gc = g_ref[rows, :]
            og_ref[rows, :] = (o * r * ngv * (gc * jax.nn.sigmoid(gc))).astype(og_ref.dtype)
            return st

        _hg_loop(nch, step, jnp.zeros((HG_DK, HG_DK), F32))

    blk = (lp, LANES)
    h = HG_HEADS
    return pl.pallas_call(
        body, name="hgrn_fwd", grid=(nseq, h),
        in_specs=[pl.BlockSpec(blk, lambda b, hh: (b, hh)),
                  pl.BlockSpec(blk, lambda b, hh: (b, h + hh)),
                  pl.BlockSpec(blk, lambda b, hh: (b, 2 * h + hh)),
                  pl.BlockSpec(blk, lambda b, hh: (b, 3 * h + hh)),
                  pl.BlockSpec((1, LANES), lambda b, hh: (0, hh)),
                  pl.BlockSpec((1, LANES), lambda b, hh: (0, hh))],
        out_specs=[pl.BlockSpec(blk, lambda b, hh: (b, hh)),
                   pl.BlockSpec(blk, lambda b, hh: (b, hh)),
                   pl.BlockSpec((None, None, nch, HG_DK, HG_DK), lambda b, hh: (b, hh, 0, 0, 0))],
        out_shape=[jax.ShapeDtypeStruct((t, D_MODEL), F32), jax.ShapeDtypeStruct((t, D_MODEL), BF16),
                   jax.ShapeDtypeStruct((nseq, h, nch, HG_DK, HG_DK), F32)],
        compiler_params=_params(("parallel", "parallel")))(proj, proj, proj, proj, lb, ng)


def hgrn_bwd(proj, o, dog, states, lb, ng, nseq, lp):
    nch = lp // HG_CHUNK
    t = proj.shape[0]
    c = HG_CHUNK

    def body(q_ref, f_ref, v_ref, g_ref, o_ref, dog_ref, st_ref, lb_ref, ng_ref,
             dq_ref, df_ref, dv_ref, dg_ref, dlb_ref, dng_ref):
        row = lax.broadcasted_iota(jnp.int32, (c, c), 0)
        col = lax.broadcasted_iota(jnp.int32, (c, c), 1)
        causal = col <= row
        incl = causal.astype(BF16)
        from_here = (col >= row).astype(BF16)
        last = lax.broadcasted_iota(jnp.int32, (c, 1), 0) == c - 1
        lbv, ngv = lb_ref[...], ng_ref[...]

        def step(s, carry):
            dst, dlb, dng = carry
            ci = nch - 1 - s
            r0 = pl.multiple_of(ci * c, c)
            rows = pl.ds(r0, c)
            oc, gc, dogc = o_ref[rows, :], g_ref[rows, :], dog_ref[rows, :]
            r = lax.rsqrt(jnp.mean(oc * oc, axis=1, keepdims=True) + RMS_EPS)
            sgg = jax.nn.sigmoid(gc)
            silu = gc * sgg
            dg_ref[rows, :] = dogc * (oc * r * ngv) * (sgg * (1.0 + gc * (1.0 - sgg)))
            don = dogc * silu
            dng = dng + _colsum(don * oc * r)
            tt = don * ngv
            do = r * (tt - oc * (r * r) * jnp.mean(tt * oc, axis=1, keepdims=True))
            st = st_ref[ci]
            fc = f_ref[rows, :]
            sg, f = _hg_gates(fc, lbv)
            bcum, blast = _hg_decay(f, incl)
            kk = 1.0 - f
            qc, vf = q_ref[rows, :], v_ref[rows, :]
            pdec = jnp.exp(bcum)
            ndec = jnp.exp(-bcum)
            edec = jnp.exp(blast - bcum)
            eb = jnp.exp(blast)
            qd_f, kd_f, kend_f = qc * pdec, kk * ndec, kk * edec
            qd, kd, kend = qd_f.astype(BF16), kd_f.astype(BF16), kend_f.astype(BF16)
            vc, dob, stb, dstb = vf.astype(BF16), do.astype(BF16), st.astype(BF16), dst.astype(BF16)
            a = jnp.where(causal, _dot(qd, kd, 1, 1), 0.0).astype(BF16)
            da = jnp.where(causal, _dot(dob, vc, 1, 1), 0.0).astype(BF16)
            dv_ref[rows, :] = _dot(a, dob, 0, 0) + _dot(kend, dstb, 1, 1)
            dqd = _dot(da, kd, 1, 0) + _dot(dob, stb, 1, 0)
            dkd = _dot(da, qd, 0, 0)
            dkend = _dot(vc, dstb, 1, 0)
            dq_ref[rows, :] = dqd * pdec
            dblast = _colsum(dkend * kend_f) + eb * _colsum(dst * st)
            dbcum = dqd * qd_f - dkd * kd_f - dkend * kend_f
            dbcum = dbcum + jnp.where(last, dblast, 0.0)
            dlf = _dot_exact_lhs(from_here, dbcum)
            df = dlf / f - (dkd * ndec + dkend * edec)
            df_ref[rows, :] = df * (1.0 - lbv) * sg * (1.0 - sg)
            dlb = dlb + _colsum(df * (1.0 - sg))
            dst = dst * eb + _dot(dob, qd, 0, 0)
            return dst, dlb, dng

        z = jnp.zeros((1, LANES), F32)
        _, dlb, dng = _hg_loop(nch, step, (jnp.zeros((HG_DK, HG_DK), F32), z, z))
        dlb_ref[...] = dlb
        dng_ref[...] = dng

    blk = (lp, LANES)
    h = HG_HEADS
    vec = pl.BlockSpec((1, LANES), lambda b, hh: (0, hh))
    svec = pl.BlockSpec((None, 1, LANES), lambda b, hh: (b, 0, hh))
    return pl.pallas_call(
        body, name="hgrn_bwd", grid=(nseq, h),
        in_specs=[pl.BlockSpec(blk, lambda b, hh: (b, hh)),
                  pl.BlockSpec(blk, lambda b, hh: (b, h + hh)),
                  pl.BlockSpec(blk, lambda b, hh: (b, 2 * h + hh)),
                  pl.BlockSpec(blk, lambda b, hh: (b, 3 * h + hh)),
                  pl.BlockSpec(blk, lambda b, hh: (b, hh)),
                  pl.BlockSpec(blk, lambda b, hh: (b, hh)),
                  pl.BlockSpec((None, None, nch, HG_DK, HG_DK), lambda b, hh: (b, hh, 0, 0, 0)),
                  vec, vec],
        out_specs=[pl.BlockSpec(blk, lambda b, hh: (b, hh))] * 4 + [svec, svec],
        out_shape=[jax.ShapeDtypeStruct((t, D_MODEL), F32)] * 4
        + [jax.ShapeDtypeStruct((nseq, 1, D_MODEL), F32)] * 2,
        compiler_params=_params(("parallel", "parallel")))(proj, proj, proj, proj, o, dog, states, lb, ng)


def _lower_bound(gamma):
    p = jax.nn.softmax(gamma, axis=0)
    return (jnp.cumsum(p, axis=0) - p[0])[1][None, :]


def local_step(x, tgt, w, hooks=None):
    nseq, seq, _ = x.shape
    lp = -(-(N_META + seq) // SB_BLOCK) * SB_BLOCK
    t = nseq * lp
    pad = lp - N_META - seq
    h0 = jnp.concatenate([jnp.broadcast_to(w['meta'][None], (nseq, N_META, D_MODEL)), x,
                          jnp.zeros((nseq, pad, D_MODEL), F32)], axis=1).reshape(t, D_MODEL)
    tgt_p = jnp.pad(tgt, ((0, 0), (N_META, pad), (0, 0))).reshape(t, D_MODEL)
    row = lambda v: v.reshape(1, -1)
    g = {}

    proj = matmul("in_ab", h0, w['w_in_ab'], 'nn')
    att = attn_fwd(proj, nseq, lp, ex=hooks.gather_late if hooks else None)
    b_out, sb_tot, sb_first = att[:3]
    if hooks:
        w = {**w, **hooks.late_weights(att[3:])}
    s5_names = ['s5_lam_re', 's5_lam_im', 's5_log_dt', 's5_b_re', 's5_b_im', 's5_c_re', 's5_c_im', 's5_d']
    mats, mats_vjp = jax.vjp(s5_matrices, *[w[n][0] for n in s5_names])
    ug = _to_groups(proj[:, :S5_WIDTH])
    s5_pows = s5_scan_powers(*[w[n][0] for n in s5_names[:3]])
    yg, sst = s5_fwd(ug, *mats[:3], *s5_pows, nseq)
    ys5 = _from_groups(yg)
    y = gelu_fwd("gelu", ys5)
    gp = matmul("glu", y, w['s5_w_glu'], 'nn', bias=w['s5_b_glu'])
    a_out = glu_gate("glu_gate", y, gp)
    cat = jnp.concatenate([a_out, b_out], axis=1)
    mix0 = matmul("out_ab", cat, w['w_out_ab'], 'nn')
    h1, xh1, rs1 = ln_fwd("ln_mix0", h0, mix0, row(w['ln_mix_g'][0]), row(w['ln_mix_b'][0]))
    hid0 = matmul("up0", h1, w['mlp_w_up'][0], 'nn', bias=row(w['mlp_b_up'][0]), act='relu2', out_dtype=BF16)
    dn0 = matmul("down0", hid0, w['mlp_w_down'][0], 'nn', bias=row(w['mlp_b_down'][0]))
    h2, xh2, rs2 = ln_fwd("ln_mlp0", h1, dn0, row(w['ln_mlp_g'][0]), row(w['ln_mlp_b'][0]))

    projc = matmul("in_c", h2, w['w_in_c'], 'nn')
    lb, lb_vjp = jax.vjp(_lower_bound, w['hgrn_gamma'])
    ng = w['hgrn_norm_g']
    o, og, states = hgrn_fwd(projc, lb, ng, nseq, lp)
    mix1 = matmul("out_c", og, w['w_out_c'], 'nn')
    h3, xh3, rs3 = ln_fwd("ln_mix1", h2, mix1, row(w['ln_mix_g'][1]), row(w['ln_mix_b'][1]))
    hid1 = matmul("up1", h3, w['mlp_w_up'][1], 'nn', bias=row(w['mlp_b_up'][1]), act='relu2', out_dtype=BF16)
    dn1 = matmul("down1", hid1, w['mlp_w_down'][1], 'nn', bias=row(w['mlp_b_down'][1]))
    h4, xh4, rs4 = ln_fwd("ln_mlp1", h3, dn1, row(w['ln_mlp_g'][1]), row(w['ln_mlp_b'][1]))
    dy, loss = loss_head("loss", h4, tgt_p, lp, seq)

    def mlp_bwd(l, dh_out, xh_out, rs_out, hid, h_in):
        dpre, dg_, db_, dbd = ln_bwd(f"ln_mlp{l}_b", dh_out, xh_out, rs_out, row(w['ln_mlp_g'][l]))
        dpu = matmul(f"down{l}_dx", dpre, w['mlp_w_down'][l], 'nt', relu2_of=hid, out_dtype=BF16)
        dwd = matmul(f"down{l}_dw", hid, dpre, 'tn')
        dh_in = matmul(f"up{l}_dx", dpu, w['mlp_w_up'][l], 'nt', add=dpre, add_scale=ALPHA)
        dwu, dbu = matmul(f"up{l}_dw", h_in, dpu, 'tn', out_slots=N_CHIPS, colsum=True)
        return dh_in, dict(ln_mlp_g=dg_, ln_mlp_b=db_, mlp_b_down=dbd, mlp_b_up=dbu, mlp_w_up=dwu, mlp_w_down=dwd)

    dh3, gm1 = mlp_bwd(1, dy, xh4, rs4, hid1, h3)
    dpre, dg1, db1, _ = ln_bwd("ln_mix1_b", dh3, xh3, rs3, row(w['ln_mix_g'][1]))
    g['w_out_c'] = matmul("out_c_dw", og, dpre, 'tn')
    dog = matmul("out_c_dx", dpre, w['w_out_c'], 'nt')
    dq, df, di, dgg, dlb, dng = hgrn_bwd(projc, o, dog, states, lb, ng, nseq, lp)
    dprojc = jnp.concatenate([dq, df, di, dgg], axis=1)
    dh2 = matmul("in_c_dx", dprojc, w['w_in_c'], 'nt', add=dpre, add_scale=ALPHA)
    g['w_in_c'] = matmul("in_c_dw", h2, dprojc, 'tn', out_slots=N_CHIPS)
    g['hgrn_gamma'] = lb_vjp(jnp.sum(dlb, axis=0))[0]
    g['hgrn_norm_g'] = jnp.sum(dng, axis=0)

    dh1, gm0 = mlp_bwd(0, dh2, xh2, rs2, hid0, h1)
    dpre, dg0, db0, _ = ln_bwd("ln_mix0_b", dh1, xh1, rs1, row(w['ln_mix_g'][0]))
    g['w_out_ab'] = matmul("out_ab_dw", cat, dpre, 'tn')
    dcat = matmul("out_ab_dx", dpre, w['w_out_ab'], 'nt')
    dgp, da_s, dbglu = glu_gate_bwd("glu_gate_b", dcat[:, :S5_WIDTH], y, gp)
    g['s5_w_glu'] = matmul("glu_dw", y, dgp, 'tn')
    g['s5_b_glu'] = dbglu
    dy5 = matmul("glu_dx", dgp, w['s5_w_glu'], 'nt', add=da_s)
    dys5 = gelu_bwd("gelu_b", dy5, ys5)
    dug, da1, da2, da3, dare, daim = s5_bwd(_to_groups(dys5), ug, sst, *mats[:3], *s5_pows, nseq)
    for n, v in zip(s5_names, mats_vjp((da1, da2, da3, dare, daim))):
        g[n] = v[None]
    for n in ('mlp_w_up', 'mlp_w_down'):
        g[n] = [gm0[n], gm1[n]]
    res = attn_bwd(proj, sb_tot, sb_first, dcat[:, S5_WIDTH:], nseq, lp, ex=hooks.scatter_early(g) if hooks else None)
    dqa, dka, dva = res[:3]
    if hooks:
        hooks.scattered(res[3:])
    dproj = jnp.concatenate([_from_groups(dug), dqa, dka, dva], axis=1)
    dh0 = matmul("in_ab_dx", dproj, w['w_in_ab'], 'nt', add=dpre, add_scale=ALPHA)
    g['w_in_ab'] = matmul("in_ab_dw", h0, dproj, 'tn', out_slots=N_CHIPS)

    dh0 = dh0.reshape(nseq, lp, D_MODEL)
    grad_x = dh0[:, N_META:N_META + seq]
    g['meta'] = jnp.sum(dh0[:, :N_META], axis=0)
    g['ln_mix_g'] = jnp.concatenate([dg0, dg1], axis=0)
    g['ln_mix_b'] = jnp.concatenate([db0, db1], axis=0)
    for n in ('ln_mlp_g', 'ln_mlp_b', 'mlp_b_down', 'mlp_b_up'):
        g[n] = jnp.concatenate([gm0[n], gm1[n]], axis=0)
    return loss, grad_x, g


class Exchange:
    def __init__(self, ins, out_shapes, n_remote, build, in_place):
        self.ins, self.out_shapes, self.n_remote, self.build, self.in_place = ins, out_shapes, n_remote, build, in_place

    def sems(self):
        return [pltpu.SemaphoreType.DMA((self.n_remote,)), pltpu.SemaphoreType.DMA((self.n_remote,))]

    def phases(self, in_refs, out_refs, ssem, rsem):
        me = (lax.axis_index("x"), lax.axis_index("y"), lax.axis_index("c"))
        out, k = [], 0
        for phase in self.build(in_refs, out_refs, me):
            out.append(functools.partial(
                lambda phase, k: [pltpu.make_async_remote_copy(src_ref=s, dst_ref=d, send_sem=ssem.at[k + i],
                                                               recv_sem=rsem.at[k + i], device_id=p,
                                                               device_id_type=MESH)
                                  for i, (s, d, p) in enumerate(phase)], phase, k))
            k += len(phase)
        return out

    def start(self, in_refs, out_refs, ssem, rsem):
        for cp in self.phases(in_refs, out_refs, ssem, rsem)[0]():
            cp.start()

    def finish(self, in_refs, out_refs, ssem, rsem):
        phases = self.phases(in_refs, out_refs, ssem, rsem)
        for cp in phases[0]():
            cp.wait()
        for make in phases[1:]:
            copies = make()
            for cp in copies:
                cp.start()
            for cp in copies:
                cp.wait()


def _exchange(name, ex):
    ni, no = len(ex.ins), len(ex.out_shapes)

    def body(*refs):
        in_refs, out_refs, sems = refs[:ni], refs[ni:ni + no], refs[ni + no:]
        ex.start(in_refs, out_refs, *sems)
        ex.finish(in_refs, out_refs, *sems)

    anyspec = pl.BlockSpec(memory_space=pl.ANY)
    return pl.pallas_call(
        body, name=name, in_specs=[anyspec] * ni, out_specs=[anyspec] * no, out_shape=ex.out_shapes,
        input_output_aliases={i: i for i in range(ni if ex.in_place else 0)}, scratch_shapes=ex.sems())(*ex.ins)


def _chip_peers(me):
    x, y, c = me
    return [(x ^ (m >> 1), y ^ (m & 1), c) for m in (1, 2, 3)]


def _half(ref, c, axis):
    h = ref.shape[axis] // 2
    idx = [slice(None)] * axis + [pl.ds(c * h, h)]
    return ref.at[tuple(idx)]


def _like(arrs):
    return [jax.ShapeDtypeStruct(a.shape, a.dtype) for a in arrs]


def gather_over_chips(bufs):
    def build(in_refs, out_refs, me):
        x, y, c = me
        j = 2 * x + y
        sib = (x, y, 1 - c)
        ici = [(_half(o.at[j], c, 0), _half(o.at[j], c, 0), p) for o in out_refs for p in _chip_peers(me)]
        d2d = [(_half(o.at[2 * p[0] + p[1]], c, 0), _half(o.at[2 * p[0] + p[1]], c, 0), sib)
               for o in out_refs for p in _chip_peers(me)]
        return [ici, d2d]
    return Exchange(bufs, _like(bufs), 6 * len(bufs), build, True)


def fold_cores(fulls):
    def build(in_refs, out_refs, me):
        x, y, c = me
        return [[(_half(s, 1 - c, 1), o, (x, y, 1 - c)) for s, o in zip(in_refs, out_refs)]]
    shapes = [jax.ShapeDtypeStruct((f.shape[0], f.shape[1] // 2, f.shape[2]), f.dtype) for f in fulls]
    return Exchange(fulls, shapes, len(fulls), build, False)


def scatter_over_chips(parts):
    def build(in_refs, out_refs, me):
        j = 2 * me[0] + me[1]
        return [[(s.at[2 * p[0] + p[1]], o.at[j], p) for s, o in zip(in_refs, out_refs) for p in _chip_peers(me)]]
    return Exchange(parts, _like(parts), 3 * len(parts), build, False)


def join_cores(bufs):
    def build(in_refs, out_refs, me):
        x, y, c = me
        return [[(o.at[c], o.at[c], (x, y, 1 - c)) for o in out_refs]]
    return Exchange(bufs, _like(bufs), len(bufs), build, True)


def gather_over_devices(buf):
    def build(in_refs, out_refs, me):
        x, y, c = me
        i = 4 * x + 2 * y + c
        out = out_refs[0]
        return [[(out.at[i], out.at[i], (x ^ (m >> 2), y ^ ((m >> 1) & 1), c ^ (m & 1))) for m in range(1, N_DEV)]]
    return Exchange([buf], _like([buf]), N_DEV - 1, build, True)


def _slot_call(name, body, scalars, ins, in_maps, out_shape, out_map, blk, grid):
    gs = pltpu.PrefetchScalarGridSpec(
        num_scalar_prefetch=1, grid=grid,
        in_specs=[pl.BlockSpec((None,) + blk, m) for m in in_maps],
        out_specs=pl.BlockSpec((None,) + blk, out_map))
    return pl.pallas_call(body, name=name, grid_spec=gs, out_shape=out_shape,
                          compiler_params=_params(("arbitrary",) * len(grid)))(scalars, *ins)


def cast_into_slot(name, w2d, chip, dtype):
    r, wd = w2d.shape
    tm = _pick(r, (512, 256, 128, 64, 32, 16))

    def body(s_ref, w_ref, o_ref):
        o_ref[...] = w_ref[...].astype(o_ref.dtype)

    gs = pltpu.PrefetchScalarGridSpec(
        num_scalar_prefetch=1, grid=(r // tm,),
        in_specs=[pl.BlockSpec((tm, wd), lambda i, s: (i, 0))],
        out_specs=pl.BlockSpec((None, tm, wd), lambda i, s: (s[0], i, 0)))
    return pl.pallas_call(body, name=name, grid_spec=gs,
                          out_shape=jax.ShapeDtypeStruct((N_CHIPS, r, wd), dtype),
                          compiler_params=_params(("arbitrary",)))(chip.reshape(1), w2d)


def sum_cores(name, full, theirs, core):
    s, r, wd = full.shape
    h = r // 2
    tm = _pick(h, (512, 256, 128, 64, 32, 16))
    nt = h // tm

    def body(s_ref, a_ref, b_ref, o_ref):
        o_ref[...] = (a_ref[...] + b_ref[...]).astype(o_ref.dtype)

    return _slot_call(name, body, core.reshape(1), [full, theirs],
                      [lambda k, i, s_: (k, s_[0] * nt + i, 0), lambda k, i, s_: (k, i, 0)],
                      jax.ShapeDtypeStruct((s, h, wd), BF16), lambda k, i, s_: (k, i, 0), (tm, wd), (s, nt))


def sum_chips(name, own, recv, chip, core):
    s, r, wd = own.shape
    tm = _pick(r, (512, 256, 128, 64, 32, 16))

    def body(s_ref, a_ref, b1_ref, b2_ref, b3_ref, o_ref):
        acc = a_ref[...].astype(F32)
        for ref in (b1_ref, b2_ref, b3_ref):
            acc = acc + ref[...].astype(F32)
        o_ref[...] = acc

    maps = [lambda i, s_: (s_[0], i, 0)]
    maps += [functools.partial(lambda i, s_, m: (s_[0] ^ m, i, 0), m=m) for m in (1, 2, 3)]
    return _slot_call(name, body, jnp.stack([chip, core]), [own, recv, recv, recv], maps,
                      jax.ShapeDtypeStruct((2, r, wd), F32), lambda i, s_: (s_[1], i, 0), (tm, wd), (r // tm,))


def sum_slots(name, arr):
    s, r, wd = arr.shape
    tm = _pick(r, (512, 256, 128, 64, 32, 16, 8))

    def body(*refs):
        acc = refs[0][...].astype(F32)
        for ref in refs[1:s]:
            acc = acc + ref[...].astype(F32)
        refs[s][...] = acc

    specs = [pl.BlockSpec((None, tm, wd), functools.partial(lambda i, k: (k, i, 0), k=k)) for k in range(s)]
    return pl.pallas_call(body, name=name, grid=(r // tm,), in_specs=specs,
                          out_specs=pl.BlockSpec((tm, wd), lambda i: (i, 0)),
                          out_shape=jax.ShapeDtypeStruct((r, wd), F32),
                          compiler_params=_params(("parallel",)))(*([arr] * s))


def _pack(arrs):
    flat = jnp.concatenate([a.reshape(-1) for a in arrs])
    n = flat.shape[0]
    padded = -(-n // (512 * LANES)) * (512 * LANES)
    return jnp.pad(flat, (0, padded - n)).reshape(-1, LANES)


def _unpack(packed, shapes):
    flat = packed.reshape(-1)
    out, pos = [], 0
    for s in shapes:
        n = math.prod(s)
        out.append(flat[pos:pos + n].reshape(s))
        pos += n
    return out


def _as2d(a):
    return a.reshape(-1, a.shape[-1])


def kernel(x, meta, w_in_ab, s5_lam_re, s5_lam_im, s5_log_dt, s5_b_re, s5_b_im, s5_c_re, s5_c_im, s5_d, s5_w_glu, s5_b_glu, w_out_ab, w_in_c, hgrn_gamma, hgrn_norm_g, w_out_c, ln_mix_g, ln_mix_b, mlp_w_up, mlp_b_up, mlp_w_down, mlp_b_down, ln_mlp_g, ln_mlp_b, loss_target, m_meta, m_w_in_ab, m_s5_lam_re, m_s5_lam_im, m_s5_log_dt, m_s5_b_re, m_s5_b_im, m_s5_c_re, m_s5_c_im, m_s5_d, m_s5_w_glu, m_s5_b_glu, m_w_out_ab, m_w_in_c, m_hgrn_gamma, m_hgrn_norm_g, m_w_out_c, m_ln_mix_g, m_ln_mix_b, m_mlp_w_up, m_mlp_b_up, m_mlp_w_down, m_mlp_b_down, m_ln_mlp_g, m_ln_mlp_b, v_meta, v_w_in_ab, v_s5_lam_re, v_s5_lam_im, v_s5_log_dt, v_s5_b_re, v_s5_b_im, v_s5_c_re, v_s5_c_im, v_s5_d, v_s5_w_glu, v_s5_b_glu, v_w_out_ab, v_w_in_c, v_hgrn_gamma, v_hgrn_norm_g, v_w_out_c, v_ln_mix_g, v_ln_mix_b, v_mlp_w_up, v_mlp_b_up, v_mlp_w_down, v_mlp_b_down, v_ln_mlp_g, v_ln_mlp_b):
    given = dict(locals())
    wts = {n: given[n] for n in WEIGHTS}
    ms = {n: given['m_' + n] for n in WEIGHTS}
    vs = {n: given['v_' + n] for n in WEIGHTS}
    chip = 2 * lax.axis_index("x") + lax.axis_index("y")

    core = lax.axis_index("c")
    parts = [(n, l) for n in BIG for l in (range(DEPTH) if n.startswith('mlp_') else [0])]
    tags = [f"{n}{l}" for n, l in parts]
    shards = {p: cast_into_slot("cast_" + t, wts[p[0]][p[1]], chip, BF16) for t, p in zip(tags, parts)}
    small_sh = jnp.concatenate([meta, hgrn_norm_g, jnp.zeros((15, meta.shape[1]), F32)], axis=0)
    first, late = parts[:1], parts[1:]
    w_in_ab_all, sm = _exchange("gather_first", gather_over_chips(
        [shards[p] for p in first] + [cast_into_slot("cast_small", small_sh, chip, F32)]))
    w = {n: wts[n] for n in SMALL}
    w['meta'] = sm[:, :N_META].transpose(1, 0, 2).reshape(N_META, D_MODEL)
    w['hgrn_norm_g'] = sm[:, N_META:N_META + 1].transpose(1, 0, 2).reshape(1, D_MODEL)
    w['w_in_ab'] = w_in_ab_all

    def slot_major(v):
        return v if v.ndim == 3 else v.reshape(N_CHIPS, -1, v.shape[-1])

    def chip_sums_of(ps, g, tag):
        gfull = [slot_major(g[n][l] if n.startswith('mlp_') else g[n]) for n, l in ps]
        theirs = _exchange("fold_grads" + tag, fold_cores(gfull))
        return [sum_cores(f"sum_cores_{n}{l}", a, b, core) for (n, l), a, b in zip(ps, gfull, theirs)]

    class Hooks:
        gather_late = gather_over_chips([shards[p] for p in late])

        @staticmethod
        def late_weights(filled):
            fw = dict(zip(late, filled))
            return {'s5_w_glu': fw['s5_w_glu', 0].reshape(S5_WIDTH, S5_WIDTH),
                    'w_out_ab': fw['w_out_ab', 0].reshape(D_MODEL, D_MODEL),
                    'w_in_c': fw['w_in_c', 0],
                    'w_out_c': fw['w_out_c', 0].reshape(D_MODEL, D_MODEL),
                    'mlp_w_up': [fw['mlp_w_up', l] for l in range(DEPTH)],
                    'mlp_w_down': [fw['mlp_w_down', l].reshape(D_FF, D_MODEL) for l in range(DEPTH)]}

        @staticmethod
        def scatter_early(g):
            Hooks.sums = chip_sums_of(late, g, "_early")
            return scatter_over_chips(Hooks.sums)

        @staticmethod
        def scattered(recv):
            Hooks.recv = list(recv)

    loss, grad_x, g = local_step(x, loss_target, w, Hooks)

    sums = chip_sums_of(first, g, "_last")
    recv = _exchange("scatter_last", scatter_over_chips(sums))
    reduced = _exchange("join_grads", join_cores(
        [sum_chips(f"sum_chips_{n}{l}", a, b, chip, core)
         for (n, l), a, b in zip(first + late, sums + Hooks.sums, list(recv) + Hooks.recv)]))
    reduced = dict(zip(first + late, [_as2d(r) for r in reduced]))

    small_shapes = [(LANES,)] + [g[n].shape for n in SMALL]
    packed = _pack([loss.reshape(-1)] + [g[n] for n in SMALL])
    slots = lax.dynamic_update_slice(jnp.zeros((N_DEV,) + packed.shape, F32), packed[None],
                                     (2 * chip + core, 0, 0))
    red = sum_slots("sum_small", _exchange("gather_small", gather_over_devices(slots))[0])
    red = _unpack(red, small_shapes)
    loss_out = red[0][0]
    gs = dict(zip(SMALL, red[1:]))
    gs['meta'] = lax.dynamic_slice_in_dim(gs['meta'], chip * meta.shape[1], meta.shape[1], axis=1)
    gs['hgrn_norm_g'] = lax.dynamic_slice_in_dim(gs['hgrn_norm_g'].reshape(1, -1), chip * meta.shape[1],
                                                 meta.shape[1], axis=1)

    grads, deltas, new_m, new_v = {}, {}, {}, {}
    for n in BIG:
        r = jnp.concatenate([reduced[n, l] for l in range(wts[n].shape[0])], axis=0)
        res = adamw("adamw_" + n, _as2d(wts[n]), [r], _as2d(ms[n]), _as2d(vs[n]))
        grads[n], deltas[n], new_m[n], new_v[n] = [r.reshape(wts[n].shape) for r in res]
    shapes = [wts[n].shape for n in SMALL]
    res = adamw("adamw_small", _pack([wts[n] for n in SMALL]), [_pack([gs[n] for n in SMALL])],
                _pack([ms[n] for n in SMALL]), _pack([vs[n] for n in SMALL]))
    for d, r in zip((grads, deltas, new_m, new_v), res):
        d.update(zip(SMALL, _unpack(r, shapes)))
    return (loss_out, grad_x, *[grads[n] for n in WEIGHTS], *[deltas[n] for n in WEIGHTS],
            *[new_m[n] for n in WEIGHTS], *[new_v[n] for n in WEIGHTS])
```

```python
import functools
import math

import jax
import jax.numpy as jnp
from jax import lax
from jax.experimental import pallas as pl
from jax.experimental.pallas import tpu as pltpu

F32 = jnp.float32
BF16 = jnp.bfloat16

D_MODEL = 1024
N_META = 16
DEPTH = 2
S5_WIDTH = 512
S5_GROUP = 16
S5_GROUPS = 32
S5_STATE = 64
S5_CHUNK = 16
SB_WIDTH = 512
SB_BLOCK = 128
SB_DEAD = -110.0
HG_HEADS = 8
HG_DK = 128
HG_CHUNK = 64
HG_UNROLL = 2
D_FF = 4096
ALPHA = (2.0 * DEPTH) ** 0.25
LN_EPS = 1e-5
RMS_EPS = 1e-6
ADAM_LR = 0.001
ADAM_B1 = 0.9
ADAM_B2 = 0.999
ADAM_EPS = 1e-08
ADAM_WD = 0.01
ADAM_STEP = 10
N_CHIPS = 4
N_DEV = 8
LANES = 128
MESH = pl.DeviceIdType.MESH
VMEM_LIMIT = 56 * 1024 * 1024
WHOLE_K_BYTES = 42 * 1024 * 1024

WEIGHTS = ['meta', 'w_in_ab', 's5_lam_re', 's5_lam_im', 's5_log_dt', 's5_b_re', 's5_b_im', 's5_c_re', 's5_c_im',
           's5_d', 's5_w_glu', 's5_b_glu', 'w_out_ab', 'w_in_c', 'hgrn_gamma', 'hgrn_norm_g', 'w_out_c',
           'ln_mix_g', 'ln_mix_b', 'mlp_w_up', 'mlp_b_up', 'mlp_w_down', 'mlp_b_down', 'ln_mlp_g', 'ln_mlp_b']
BIG = ['w_in_ab', 's5_w_glu', 'w_out_ab', 'w_in_c', 'w_out_c', 'mlp_w_up', 'mlp_w_down']
SHARDED_SMALL = ['meta', 'hgrn_norm_g']
SMALL = [n for n in WEIGHTS if n not in BIG]


def _params(sem=None):
    return pltpu.CompilerParams(dimension_semantics=sem, vmem_limit_bytes=VMEM_LIMIT)


def _pick(n, cands):
    for c in cands:
        if n % c == 0:
            return c
    return n


def _dot(a, b, ca, cb):
    return lax.dot_general(a, b, (((ca,), (cb,)), ((), ())), preferred_element_type=F32)


def _split3(x):
    hi = x.astype(BF16)
    r = x - hi.astype(F32)
    mid = r.astype(BF16)
    lo = (r - mid.astype(F32)).astype(BF16)
    return hi, mid, lo


def _dot_exact_rhs(x, m, cx, cm):
    hi = x.astype(BF16)
    lo = (x - hi.astype(F32)).astype(BF16)
    return _dot(hi, m, cx, cm) + _dot(lo, m, cx, cm)


def _dot3(a, b, ca, cb):
    ah = a.astype(BF16)
    al = (a - ah.astype(F32)).astype(BF16)
    bh = b.astype(BF16)
    bl = (b - bh.astype(F32)).astype(BF16)
    return _dot(ah, bh, ca, cb) + (_dot(ah, bl, ca, cb) + _dot(al, bh, ca, cb))


def rowwise(name, fn, row_ins, bc_ins, out_widths, sum_widths=(), out_dtypes=None, tm=None):
    t = row_ins[0].shape[0]
    if tm is None:
        wmax = max([a.shape[1] for a in row_ins] + list(out_widths))
        tm = _pick(t, (272, 256, 128, 64, 16, 8)) if wmax > 1024 else _pick(t, (544, 512, 256, 128, 64, 16, 8))
    nr, nb, no, ns = len(row_ins), len(bc_ins), len(out_widths), len(sum_widths)
    out_dtypes = out_dtypes or [F32] * no

    def body(*refs):
        i = pl.program_id(0)
        rows = [r[...] for r in refs[:nr]]
        bcs = [r[...] for r in refs[nr:nr + nb]]
        outs, sums = fn(i, tm, rows, bcs)
        for r, v in zip(refs[nr + nb:nr + nb + no], outs):
            r[...] = v.astype(r.dtype)
        if ns:
            srefs = refs[nr + nb + no:]

            @pl.when(i == 0)
            def _():
                for r in srefs:
                    r[...] = jnp.zeros(r.shape, r.dtype)

            for r, v in zip(srefs, sums):
                r[...] += v

    in_specs = [pl.BlockSpec((tm, a.shape[1]), lambda i: (i, 0)) for a in row_ins]
    in_specs += [pl.BlockSpec(a.shape, lambda i: (0, 0)) for a in bc_ins]
    out_specs = [pl.BlockSpec((tm, w), lambda i: (i, 0)) for w in out_widths]
    out_specs += [pl.BlockSpec((1, w), lambda i: (0, 0)) for w in sum_widths]
    out_shape = [jax.ShapeDtypeStruct((t, w), dt) for w, dt in zip(out_widths, out_dtypes)]
    out_shape += [jax.ShapeDtypeStruct((1, w), F32) for w in sum_widths]
    res = pl.pallas_call(body, name=name, grid=(t // tm,), in_specs=in_specs, out_specs=out_specs,
                         out_shape=out_shape, compiler_params=_params(("arbitrary",)))(*row_ins, *bc_ins)
    return res


def _colsum(v):
    return jnp.sum(v, axis=0, keepdims=True)


def matmul(name, a, b, mode, *, bias=None, act=None, add=None, add_scale=1.0, relu2_of=None, colsum=False,
           out_dtype=F32, out_slots=0, tm=None, tn=None, tk=None):
    slotted = b.ndim == 3
    if mode == 'nn':
        m, k = a.shape
        n = b.shape[-1] * (b.shape[0] if slotted else 1)
    elif mode == 'nt':
        m, k = a.shape
        n = b.shape[-2]
    else:
        k, m = a.shape
        n = b.shape[-1]
    ns = b.shape[-1] if slotted else None
    if mode == 'tn':
        tm = tm or _pick(m, (512, 256, 128))
        nw = n if not out_slots else n // out_slots
        if tn is None and tk is None:
            for cand in (512, 256):
                if nw % cand == 0 and (2 * k * (tm * a.dtype.itemsize + cand * b.dtype.itemsize)
                                       + 2 * tm * cand * 4 <= WHOLE_K_BYTES):
                    tn, tk = cand, k
                    break
        tn = tn or _pick(nw, (1024, 512, 256, 128))
        tk = tk or _pick(k, (1088, 1024, 768, 512, 384, 256, 128))
    else:
        tm = tm or _pick(m, (1088, 1024, 768, 512, 384, 256, 128))
        tn = tn or _pick(n if not (slotted and mode == 'nn') else ns, (1024, 512, 256, 128))
        extra = sum(x is not None for x in (add, relu2_of)) * 4
        if tk is None and not slotted and (2 * (tm * k * a.dtype.itemsize + k * tn * b.dtype.itemsize)
                                           + 2 * tm * tn * (4 + extra) <= WHOLE_K_BYTES):
            tk = k
        tk = tk or _pick(k if not (slotted and mode == 'nt') else ns, (1024, 512, 256, 128))
    gm, gn, gk = m // tm, n // tn, k // tk

    if mode == 'nn':
        a_spec = pl.BlockSpec((tm, tk), lambda i, j, kk: (i, kk))
        if slotted:
            per = ns // tn
            b_spec = pl.BlockSpec((None, tk, tn), lambda i, j, kk: (j // per, kk, j % per))
        else:
            b_spec = pl.BlockSpec((tk, tn), lambda i, j, kk: (kk, j))
        ca, cb = 1, 0
    elif mode == 'nt':
        a_spec = pl.BlockSpec((tm, tk), lambda i, j, kk: (i, kk))
        if slotted:
            per = ns // tk
            b_spec = pl.BlockSpec((None, tn, tk), lambda i, j, kk: (kk // per, j, kk % per))
        else:
            b_spec = pl.BlockSpec((tn, tk), lambda i, j, kk: (j, kk))
        ca, cb = 1, 1
    else:
        a_spec = pl.BlockSpec((tk, tm), lambda i, j, kk: (kk, i))
        b_spec = pl.BlockSpec((tk, tn), lambda i, j, kk: (kk, j))
        ca, cb = 0, 0
    in_specs = [a_spec, b_spec]
    args = [a, b]
    if bias is not None:
        in_specs.append(pl.BlockSpec((1, tn), lambda i, j, kk: (0, j)))
        args.append(bias)
    if add is not None:
        in_specs.append(pl.BlockSpec((tm, tn), lambda i, j, kk: (i, j)))
        args.append(add)
    if relu2_of is not None:
        in_specs.append(pl.BlockSpec((tm, tn), lambda i, j, kk: (i, j)))
        args.append(relu2_of)
    if out_slots:
        per_o = (n // out_slots) // tn
        out_spec = pl.BlockSpec((None, tm, tn), lambda i, j, kk: (j // per_o, i, j % per_o))
        out_shape = jax.ShapeDtypeStruct((out_slots, m, n // out_slots), out_dtype)
    else:
        out_spec = pl.BlockSpec((tm, tn), lambda i, j, kk: (i, j))
        out_shape = jax.ShapeDtypeStruct((m, n), out_dtype)
    grid = (gm, gn, gk)
    if colsum:
        assert mode == 'tn'
        out_spec = [out_spec, pl.BlockSpec((1, tn), lambda i, j, kk: (0, j))]
        out_shape = [out_shape, jax.ShapeDtypeStruct((1, n), F32)]

        def swapped(spec):
            return pl.BlockSpec(spec.block_shape, functools.partial(lambda j, i, kk, f: f(i, j, kk), f=spec.index_map))
        in_specs = [swapped(sp) for sp in in_specs]
        out_spec = [swapped(sp) for sp in out_spec]
        grid = (gn, gm, gk)

    def body(*refs):
        a_ref, b_ref = refs[0], refs[1]
        pos = 2
        bias_ref = add_ref = hid_ref = cs_ref = None
        if bias is not None:
            bias_ref = refs[pos]
            pos += 1
        if add is not None:
            add_ref = refs[pos]
            pos += 1
        if relu2_of is not None:
            hid_ref = refs[pos]
            pos += 1
        o_ref = refs[pos]
        pos += 1
        if colsum:
            cs_ref = refs[pos]
            pos += 1
        acc_ref = refs[pos] if gk > 1 else None
        bt = b_ref[...]
        p = _dot(a_ref[...].astype(BF16), bt.astype(BF16), ca, cb)
        kk = pl.program_id(2)

        if colsum:
            @pl.when(jnp.logical_and(pl.program_id(1) == 0, kk == 0))
            def _():
                cs_ref[...] = _colsum(bt.astype(F32))

            @pl.when(jnp.logical_and(pl.program_id(1) == 0, kk > 0))
            def _():
                cs_ref[...] += _colsum(bt.astype(F32))

        def finish(r):
            if bias_ref is not None:
                r = r + bias_ref[...]
            if act == 'relu2':
                r = jnp.square(jnp.maximum(r, 0.0))
            if hid_ref is not None:
                r = r * (2.0 * jnp.sqrt(hid_ref[...].astype(F32)))
            if add_ref is not None:
                r = r + add_scale * add_ref[...]
            o_ref[...] = r.astype(o_ref.dtype)

        if gk == 1:
            finish(p)
        else:
            @pl.when(kk == 0)
            def _():
                acc_ref[...] = p

            @pl.when(kk > 0)
            def _():
                acc_ref[...] += p

            @pl.when(kk == gk - 1)
            def _():
                finish(acc_ref[...])

    scratch = [pltpu.VMEM((tm, tn), F32)] if gk > 1 else []
    sem = ("arbitrary",) * 3 if colsum else ("parallel", "parallel", "arbitrary")
    return pl.pallas_call(body, name=name, grid=grid, in_specs=in_specs, out_specs=out_spec,
                          out_shape=out_shape, scratch_shapes=scratch, compiler_params=_params(sem))(*args)


def ln_fwd(name, h, mix, g, b):
    def fn(i, tm, rows, bcs):
        pre = ALPHA * rows[0] + rows[1]
        mu = jnp.mean(pre, axis=1, keepdims=True)
        xc = pre - mu
        var = jnp.mean(xc * xc, axis=1, keepdims=True)
        rstd = lax.rsqrt(var + LN_EPS)
        xhat = xc * rstd
        return (xhat * bcs[0] + bcs[1], xhat, rstd), ()
    return rowwise(name, fn, [h, mix], [g, b], [D_MODEL, D_MODEL, 1])


def ln_bwd(name, dy, xhat, rstd, g):
    def fn(i, tm, rows, bcs):
        dy_, xh, rs = rows
        dyg = dy_ * bcs[0]
        m1 = jnp.mean(dyg, axis=1, keepdims=True)
        m2 = jnp.mean(dyg * xh, axis=1, keepdims=True)
        dpre = rs * (dyg - m1 - xh * m2)
        return (dpre,), (_colsum(dy_ * xh), _colsum(dy_), _colsum(dpre))
    return rowwise(name, fn, [dy, xhat, rstd], [g], [D_MODEL], [D_MODEL] * 3)


_GELU_C = math.sqrt(2.0 / math.pi)


def gelu_fwd(name, x):
    def fn(i, tm, rows, bcs):
        v = rows[0]
        u = _GELU_C * (v + 0.044715 * (v * v * v))
        return (0.5 * v * (1.0 + jnp.tanh(u)),), ()
    return rowwise(name, fn, [x], [], [x.shape[1]])[0]


def gelu_bwd(name, dy, x):
    def fn(i, tm, rows, bcs):
        v = rows[1]
        th = jnp.tanh(_GELU_C * (v + 0.044715 * (v * v * v)))
        dg = 0.5 * (1.0 + th) + 0.5 * v * (1.0 - th * th) * (_GELU_C * (1.0 + 3.0 * 0.044715 * v * v))
        return (rows[0] * dg,), ()
    return rowwise(name, fn, [dy, x], [], [x.shape[1]])[0]


def glu_gate(name, y, gp):
    def fn(i, tm, rows, bcs):
        return (rows[0] * jax.nn.sigmoid(rows[1]),), ()
    return rowwise(name, fn, [y, gp], [], [y.shape[1]], out_dtypes=[BF16])[0]


def glu_gate_bwd(name, da, y, gp):
    def fn(i, tm, rows, bcs):
        s = jax.nn.sigmoid(rows[2])
        dgp = rows[0] * rows[1] * s * (1.0 - s)
        return (dgp, rows[0] * s), (_colsum(dgp),)
    w = y.shape[1]
    return rowwise(name, fn, [da, y, gp], [], [w, w], [w])


def loss_head(name, h, tgt, lp, seq):
    def fn(i, tm, rows, bcs):
        pos = (i * tm + lax.broadcasted_iota(jnp.int32, (tm, 1), 0)) % lp
        valid = jnp.logical_and(pos >= N_META, pos < N_META + seq)
        err = jnp.where(valid, rows[0] - rows[1], 0.0)
        part = 0.5 * jnp.sum(jnp.mean(err * err, axis=1, keepdims=True), axis=0, keepdims=True)
        return (err * (1.0 / D_MODEL),), (jnp.broadcast_to(part, (1, LANES)),)
    return rowwise(name, fn, [h, tgt], [], [D_MODEL], [LANES])


def adamw(name, w, gs, m, v):
    ng = len(gs)

    def fn(i, tm, rows, bcs):
        w_ = rows[0]
        g = rows[1] if ng == 1 else rows[1] + rows[2]
        m_, v_ = rows[1 + ng], rows[2 + ng]
        m_ = ADAM_B1 * m_ + (1.0 - ADAM_B1) * g
        v_ = ADAM_B2 * v_ + (1.0 - ADAM_B2) * jnp.square(g)
        m_hat = m_ / (1.0 - ADAM_B1 ** ADAM_STEP)
        v_hat = v_ / (1.0 - ADAM_B2 ** ADAM_STEP)
        delta = -ADAM_LR * (m_hat / (jnp.sqrt(v_hat) + ADAM_EPS) + ADAM_WD * w_)
        return (g, delta, m_, v_), ()
    wd = w.shape[1]
    return rowwise(name, fn, [w] + list(gs) + [m, v], [], [wd] * 4)


def s5_matrices(lam_re, lam_im, log_dt, b_re, b_im, c_re, c_im, d):
    c = S5_CHUNK
    dt = jnp.exp(log_dt)[:, None]
    tau = jnp.arange(c + 1, dtype=F32)[:, None, None]
    mag = jnp.exp(tau * (lam_re * dt)[None])
    ang = tau * (lam_im * dt)[None]
    pw_re, pw_im = mag * jnp.cos(ang), mag * jnp.sin(ang)
    nr, ni = pw_re[1] - 1.0, pw_im[1]
    den = lam_re * lam_re + lam_im * lam_im
    cf_re = (nr * lam_re + ni * lam_im) / den
    cf_im = (ni * lam_re - nr * lam_im) / den
    bb_re = cf_re[:, :, None] * b_re - cf_im[:, :, None] * b_im
    bb_im = cf_re[:, :, None] * b_im + cf_im[:, :, None] * b_re
    cp_re = c_re[None] * pw_re[:, :, None, :] - c_im[None] * pw_im[:, :, None, :]
    cp_im = c_re[None] * pw_im[:, :, None, :] + c_im[None] * pw_re[:, :, None, :]
    hp = lax.Precision.HIGHEST
    kern = (jnp.einsum('tghp,gpk->tghk', cp_re[:c], bb_re, precision=hp)
            - jnp.einsum('tghp,gpk->tghk', cp_im[:c], bb_im, precision=hp))
    ii = jnp.arange(c)
    sel = (ii[None, :, None] - ii[None, None, :] == ii[:, None, None]).astype(F32)
    a1 = jnp.einsum('tghk,tij->gjkih', kern, sel, precision=hp)
    a1 = a1 + jnp.einsum('ij,kh,gh->gjkih', jnp.eye(c, dtype=F32), jnp.eye(S5_GROUP, dtype=F32), d)
    a1 = a1.reshape(S5_GROUPS, c * S5_GROUP, c * S5_GROUP)
    a2 = jnp.concatenate([cp_re[1:], -cp_im[1:]], axis=-1)
    a2 = a2.transpose(1, 3, 0, 2).reshape(S5_GROUPS, 2 * S5_STATE, c * S5_GROUP)
    rv_re, rv_im = pw_re[:c][::-1], pw_im[:c][::-1]
    x_re = rv_re[:, :, :, None] * bb_re[None] - rv_im[:, :, :, None] * bb_im[None]
    x_im = rv_re[:, :, :, None] * bb_im[None] + rv_im[:, :, :, None] * bb_re[None]
    a3 = jnp.concatenate([x_re, x_im], axis=2)
    a3 = a3.transpose(1, 0, 3, 2).reshape(S5_GROUPS, c * S5_GROUP, 2 * S5_STATE)
    are = jnp.concatenate([pw_re[c], pw_re[c]], axis=-1)[:, None, :]
    aim = jnp.concatenate([-pw_im[c], pw_im[c]], axis=-1)[:, None, :]
    return a1, a2, a3, are, aim


S5_LEVELS = 8


def s5_scan_powers(lam_re, lam_im, log_dt):
    dt = jnp.exp(log_dt)[:, None]
    e = (S5_CHUNK * 2.0 ** jnp.arange(S5_LEVELS, dtype=F32))[:, None, None]
    mag = jnp.exp(e * (lam_re * dt)[None])
    ang = e * (lam_im * dt)[None]
    pr, pi = mag * jnp.cos(ang), mag * jnp.sin(ang)
    pre = jnp.concatenate([pr, pr], axis=-1).transpose(1, 0, 2)
    pim = jnp.concatenate([-pi, pi], axis=-1).transpose(1, 0, 2)
    return pre, pim


def _s5_scan(x, pre, pim, reverse):
    n = x.shape[0]
    rowi = lax.broadcasted_iota(jnp.int32, (n, 1), 0)
    t = x
    for k in range(S5_LEVELS):
        shift = 2 ** k
        if shift >= n:
            break
        p_re, p_im = pre[k:k + 1, :], pim[k:k + 1, :]
        if reverse:
            far = jnp.where(rowi < n - shift, pltpu.roll(t, n - shift, 0), 0.0)
            t = t + (p_re * far + pltpu.roll(p_im * far, S5_STATE, 1))
        else:
            far = jnp.where(rowi >= shift, pltpu.roll(t, shift, 0), 0.0)
            t = t + (p_re * far + p_im * pltpu.roll(far, S5_STATE, 1))
    return t


def s5_fwd(ug, a1, a2, a3, pre, pim, nseq):
    g, nc, cw = ug.shape
    per = nc // nseq
    sw = 2 * S5_STATE
    assert per <= 2 ** S5_LEVELS

    def body(u_ref, a1_ref, a2_ref, a3_ref, pre_ref, pim_ref, y_ref, s_ref):
        u = u_ref[...]
        x = _dot3(u, a3_ref[...], 1, 0)
        first = lax.broadcasted_iota(jnp.int32, (per, 1), 0) == 0
        for b in range(nseq):
            t = _s5_scan(x[b * per:(b + 1) * per], pre_ref[...], pim_ref[...], False)
            s_ref[pl.ds(b * per, per), :] = jnp.where(first, 0.0, pltpu.roll(t, 1, 0))
        y_ref[...] = _dot3(u, a1_ref[...], 1, 0) + _dot3(s_ref[...], a2_ref[...], 1, 0)

    def spec(shape):
        return pl.BlockSpec((None,) + shape, lambda i: (i, 0, 0))
    lv = (S5_LEVELS, sw)
    return pl.pallas_call(
        body, name="s5_fwd", grid=(g,),
        in_specs=[spec((nc, cw)), spec((cw, cw)), spec((sw, cw)), spec((cw, sw)), spec(lv), spec(lv)],
        out_specs=[spec((nc, cw)), spec((nc, sw))],
        out_shape=[jax.ShapeDtypeStruct((g, nc, cw), F32), jax.ShapeDtypeStruct((g, nc, sw), F32)],
        compiler_params=_params(("parallel",)))(ug, a1, a2, a3, pre, pim)


def s5_bwd(dyg, ug, sst, a1, a2, a3, pre, pim, nseq):
    g, nc, cw = ug.shape
    per = nc // nseq
    sw = 2 * S5_STATE

    def body(dy_ref, u_ref, s_ref, a1_ref, a2_ref, a3_ref, pre_ref, pim_ref,
             du_ref, da1_ref, da2_ref, da3_ref, dare_ref, daim_ref, dx_ref):
        dy = dy_ref[...]
        u = u_ref[...]
        gd = _dot3(dy, a2_ref[...], 1, 1)
        s_all = s_ref[...]
        da2_ref[...] = _dot3(s_all, dy, 0, 0)
        last = lax.broadcasted_iota(jnp.int32, (per, 1), 0) == per - 1
        for b in range(nseq):
            gs = _s5_scan(gd[b * per:(b + 1) * per], pre_ref[...], pim_ref[...], True)
            dx_ref[pl.ds(b * per, per), :] = jnp.where(last, 0.0, pltpu.roll(gs, per - 1, 0))
        dx = dx_ref[...]
        dare_ref[...] = _colsum(dx * s_all)
        daim_ref[...] = _colsum(dx * pltpu.roll(s_all, S5_STATE, 1))
        du_ref[...] = _dot3(dy, a1_ref[...], 1, 1) + _dot3(dx, a3_ref[...], 1, 1)
        da1_ref[...] = _dot3(u, dy, 0, 0)
        da3_ref[...] = _dot3(u, dx, 0, 0)

    def spec(shape):
        return pl.BlockSpec((None,) + shape, lambda i: (i, 0, 0))
    sds = jax.ShapeDtypeStruct
    return pl.pallas_call(
        body, name="s5_bwd", grid=(g,),
        in_specs=[spec((nc, cw)), spec((nc, cw)), spec((nc, sw)), spec((cw, cw)), spec((sw, cw)), spec((cw, sw)),
                  spec((S5_LEVELS, sw)), spec((S5_LEVELS, sw))],
        out_specs=[spec((nc, cw)), spec((cw, cw)), spec((sw, cw)), spec((cw, sw)), spec((1, sw)), spec((1, sw))],
        out_shape=[sds((g, nc, cw), F32), sds((g, cw, cw), F32), sds((g, sw, cw), F32), sds((g, cw, sw), F32),
                   sds((g, 1, sw), F32), sds((g, 1, sw), F32)],
        scratch_shapes=[pltpu.VMEM((nc, sw), F32)],
        compiler_params=_params(("parallel",)))(dyg, ug, sst, a1, a2, a3, pre, pim)


def _to_groups(u):
    t = u.shape[0]
    nc = t // S5_CHUNK
    return u.reshape(nc, S5_CHUNK, S5_GROUPS, S5_GROUP).transpose(2, 0, 1, 3).reshape(S5_GROUPS, nc, -1)


def _from_groups(yg):
    g, nc, _ = yg.shape
    return yg.reshape(g, nc, S5_CHUNK, S5_GROUP).transpose(1, 2, 0, 3).reshape(nc * S5_CHUNK, g * S5_GROUP)


def _sb_masks():
    row = lax.broadcasted_iota(jnp.int32, (SB_BLOCK, SB_BLOCK), 0)
    col = lax.broadcasted_iota(jnp.int32, (SB_BLOCK, SB_BLOCK), 1)
    lane = lax.broadcasted_iota(jnp.int32, (1, LANES), 1)
    return row, col, [lane < 64, lane >= 64]


def _sb_weights(z, vis, later_of, after):
    sp = jnp.maximum(z, 0.0) + jnp.log1p(jnp.exp(-jnp.abs(z)))
    lk = jnp.where(vis, -sp, 0.0)
    rs = jnp.sum(lk, axis=1, keepdims=True)
    later = _dot_exact_rhs(lk, after, 1, 0) + later_of(rs)
    lb = z - sp
    w = jnp.where(vis, jnp.exp(lb + later), 0.0)
    return w, rs, lb


def _hidden_exchange(ex, n_in, n_out, grid):
    ni, no = (len(ex.ins), len(ex.out_shapes)) if ex else (0, 0)

    def split(refs):
        a, b = n_in + ni, n_in + ni + n_out
        return refs[:n_in], refs[a:b], (refs[n_in:a], refs[b:b + no], refs[b + no:])

    def at_step(which, fn, parts):
        if ex is not None:
            ids = [pl.program_id(d) == (0 if which == 'first' else g - 1) for d, g in enumerate(grid)]
            cond = ids[0]
            for c in ids[1:]:
                cond = jnp.logical_and(cond, c)
            pl.when(cond)(lambda: fn(parts[0], parts[1], *parts[2]))

    anyspec = pl.BlockSpec(memory_space=pl.ANY)
    kw = dict(in_specs=[anyspec] * ni, out_specs=[anyspec] * no, out_shape=list(ex.out_shapes) if ex else [],
              aliases={n_in + i: n_out + i for i in range(ni if ex and ex.in_place else 0)},
              scratch=ex.sems() if ex else [], ins=list(ex.ins) if ex else [])
    start = functools.partial(at_step, 'first', ex.start if ex else None)
    finish = functools.partial(at_step, 'last', ex.finish if ex else None)
    return split, start, finish, kw


def _lane_tile(x, tl):
    return x[:, tl * LANES:(tl + 1) * LANES]


SB_TILES = 2
_SB_CHAINS = [(tl, hd) for tl in range(SB_TILES) for hd in range(2)]


def _sb_specs(lp, nseq):
    wd = SB_TILES * LANES
    off = [(S5_WIDTH + i * SB_WIDTH) // wd for i in range(3)]
    blk = (lp, wd)
    qkv = [pl.BlockSpec(blk, functools.partial(lambda b, h, o: (b, o + h), o=o)) for o in off]
    return (blk, qkv, pl.BlockSpec(blk, lambda b, h: (b, h)),
            pl.BlockSpec((None, None, 8, LANES), lambda b, h: (b, h, 0, 0)))


def attn_fwd(proj, nseq, lp, ex=None):
    nqb = lp // SB_BLOCK
    t = proj.shape[0]
    nstep = SB_WIDTH // (SB_TILES * LANES)
    split, ex_start, ex_finish, exkw = _hidden_exchange(ex, 3, 3, (nseq, nstep))

    def body(*refs):
        (q_ref, k_ref, v_ref), (o_ref, tot_ref, first_ref), ex_parts = split(refs)
        ex_start(ex_parts)
        row, col, hmask = _sb_masks()
        after = (row > col).astype(BF16)
        tri = col < row
        lane8 = lax.broadcasted_iota(jnp.int32, (8, LANES), 1)

        def qloop(qi, firsts):
            q0 = pl.multiple_of(qi * SB_BLOCK, SB_BLOCK)
            q = q_ref[pl.ds(q0, SB_BLOCK), :] * 0.125
            qm = [jnp.where(hmask[hd], _lane_tile(q, tl), 0.0).astype(BF16) for tl, hd in _SB_CHAINS]

            def alive(carry):
                return jnp.logical_and(carry[0] <= qi, carry[1] > 0)

            def kstep(carry):
                s = carry[0]
                accs, lats = list(carry[2:2 + SB_TILES]), list(carry[2 + SB_TILES:])
                kj = qi - s
                k0 = pl.multiple_of(kj * SB_BLOCK, SB_BLOCK)
                k = k_ref[pl.ds(k0, SB_BLOCK), :].astype(BF16)
                v = v_ref[pl.ds(k0, SB_BLOCK), :]
                vis = jnp.logical_or(kj < qi, tri)
                for c, (tl, hd) in enumerate(_SB_CHAINS):
                    z = _dot(qm[c], _lane_tile(k, tl), 1, 1)
                    w, rs, _ = _sb_weights(z, vis, functools.partial(lambda rs, lat: lat, lat=lats[c]), after)
                    vm = jnp.where(hmask[hd], _lane_tile(v, tl), 0.0).astype(BF16)
                    accs[tl] = accs[tl] + _dot(w.astype(BF16), vm, 1, 0)
                    lats[c] = lats[c] + rs
                top = functools.reduce(jnp.maximum, lats)
                go = (jnp.max(top) > SB_DEAD).astype(jnp.int32)
                return (s + 1, go, *accs, *lats)

            z1 = jnp.zeros((SB_BLOCK, 1), F32)
            res = lax.while_loop(alive, kstep, (jnp.int32(0), jnp.int32(1),
                                                *[jnp.zeros((SB_BLOCK, LANES), F32)] * SB_TILES,
                                                *[z1] * len(_SB_CHAINS)))
            accs, lats = res[2:2 + SB_TILES], res[2 + SB_TILES:]
            for tl in range(SB_TILES):
                cols = slice(tl * LANES, (tl + 1) * LANES)
                o_ref[pl.ds(q0, SB_BLOCK), cols] = accs[tl].astype(o_ref.dtype)
                tot_ref[pl.ds(q0, SB_BLOCK), cols] = jnp.where(hmask[0], lats[2 * tl], lats[2 * tl + 1])
            return jnp.where(lane8 == qi, (qi - res[0] + 1).astype(F32), firsts)

        first_ref[...] = lax.fori_loop(0, nqb, qloop, jnp.zeros((8, LANES), F32))
        ex_finish(ex_parts)

    blk, qkv, out, vec = _sb_specs(lp, nseq)
    return pl.pallas_call(
        body, name="attn_fwd", grid=(nseq, nstep),
        in_specs=qkv + exkw['in_specs'], out_specs=[out, out, vec] + exkw['out_specs'],
        out_shape=[jax.ShapeDtypeStruct((t, SB_WIDTH), BF16), jax.ShapeDtypeStruct((t, SB_WIDTH), F32)]
        + [jax.ShapeDtypeStruct((nseq, nstep, 8, LANES), F32)] + exkw['out_shape'],
        input_output_aliases=exkw['aliases'], scratch_shapes=exkw['scratch'],
        compiler_params=_params(("arbitrary", "arbitrary")))(proj, proj, proj, *exkw['ins'])


def attn_bwd(proj, tot, first, do, nseq, lp, ex=None):
    nqb = lp // SB_BLOCK
    t = proj.shape[0]
    nstep = SB_WIDTH // (SB_TILES * LANES)
    split, ex_start, ex_finish, exkw = _hidden_exchange(ex, 6, 3, (nseq, nstep))

    def body(*refs):
        (q_ref, k_ref, v_ref, tot_ref, first_ref, do_ref), (dq_ref, dk_ref, dv_ref), ex_parts = split(refs)
        ex_start(ex_parts)
        row, col, hmask = _sb_masks()
        after = (row > col).astype(BF16)
        before = (row < col).astype(BF16)
        tri = col < row
        lane8 = lax.broadcasted_iota(jnp.int32, (8, LANES), 1)
        firsts = first_ref[...]
        dk_ref[...] = jnp.zeros(dk_ref.shape, F32)
        dv_ref[...] = jnp.zeros(dv_ref.shape, F32)
        nc = len(_SB_CHAINS)

        def qloop(qi, _):
            first = jnp.max(jnp.where(lane8 == qi, firsts, 0.0)).astype(jnp.int32)
            first = jnp.clip(first, 0, qi)
            q0 = pl.multiple_of(qi * SB_BLOCK, SB_BLOCK)
            q = q_ref[pl.ds(q0, SB_BLOCK), :] * 0.125
            do_ = do_ref[pl.ds(q0, SB_BLOCK), :]
            tot_ = tot_ref[pl.ds(q0, SB_BLOCK), :]
            qm = [jnp.where(hmask[hd], _lane_tile(q, tl), 0.0).astype(BF16) for tl, hd in _SB_CHAINS]
            dom = [jnp.where(hmask[hd], _lane_tile(do_, tl), 0.0).astype(BF16) for tl, hd in _SB_CHAINS]
            tot = [jnp.max(jnp.where(hmask[hd], _lane_tile(tot_, tl), -jnp.inf), axis=1, keepdims=True)
                   for tl, hd in _SB_CHAINS]

            def kloop(kj, carry):
                dqs = list(carry[:SB_TILES])
                pre = list(carry[SB_TILES:SB_TILES + nc])
                gpre = list(carry[SB_TILES + nc:])
                k0 = pl.multiple_of(kj * SB_BLOCK, SB_BLOCK)
                kf = k_ref[pl.ds(k0, SB_BLOCK), :]
                k = kf.astype(BF16)
                v = v_ref[pl.ds(k0, SB_BLOCK), :].astype(BF16)
                vis = jnp.logical_or(kj < qi, tri)
                dk_acc = [jnp.zeros((SB_BLOCK, LANES), F32)] * SB_TILES
                dv_acc = [jnp.zeros((SB_BLOCK, LANES), F32)] * SB_TILES
                for c, (tl, hd) in enumerate(_SB_CHAINS):
                    z = _dot(qm[c], _lane_tile(k, tl), 1, 1)
                    later_of = functools.partial(lambda rs, t_, p_: t_ - p_ - rs, t_=tot[c], p_=pre[c])
                    w, rs, lb = _sb_weights(z, vis, later_of, after)
                    dw = _dot(dom[c], _lane_tile(v, tl), 1, 1)
                    g = dw * w
                    dlk = _dot_exact_rhs(g, before, 1, 0) + gpre[c]
                    sig = jnp.exp(lb)
                    dz = jnp.where(vis, g * (1.0 - sig) - dlk * sig, 0.0).astype(BF16)
                    km = jnp.where(hmask[hd], _lane_tile(kf, tl), 0.0).astype(BF16)
                    dqs[tl] = dqs[tl] + _dot(dz, km, 1, 0)
                    dk_acc[tl] = dk_acc[tl] + _dot(dz, qm[c], 0, 0)
                    dv_acc[tl] = dv_acc[tl] + _dot(w.astype(BF16), dom[c], 0, 0)
                    pre[c] = pre[c] + rs
                    gpre[c] = gpre[c] + jnp.sum(g, axis=1, keepdims=True)
                for tl in range(SB_TILES):
                    cols = slice(tl * LANES, (tl + 1) * LANES)
                    dk_ref[pl.ds(k0, SB_BLOCK), cols] += dk_acc[tl]
                    dv_ref[pl.ds(k0, SB_BLOCK), cols] += dv_acc[tl]
                return (*dqs, *pre, *gpre)

            z1 = jnp.zeros((SB_BLOCK, 1), F32)
            res = lax.fori_loop(first, qi + 1, kloop,
                                (*[jnp.zeros((SB_BLOCK, LANES), F32)] * SB_TILES, *[z1] * (2 * nc)))
            for tl in range(SB_TILES):
                dq_ref[pl.ds(q0, SB_BLOCK), tl * LANES:(tl + 1) * LANES] = res[tl] * 0.125
            return 0

        lax.fori_loop(0, nqb, qloop, 0)
        ex_finish(ex_parts)

    blk, qkv, out, vec = _sb_specs(lp, nseq)
    return pl.pallas_call(
        body, name="attn_bwd", grid=(nseq, nstep),
        in_specs=qkv + [out, vec, out] + exkw['in_specs'], out_specs=[out] * 3 + exkw['out_specs'],
        out_shape=[jax.ShapeDtypeStruct((t, SB_WIDTH), F32)] * 3 + exkw['out_shape'],
        input_output_aliases=exkw['aliases'], scratch_shapes=exkw['scratch'],
        compiler_params=_params(("arbitrary", "arbitrary")))(proj, proj, proj, tot, first, do, *exkw['ins'])


def _hg_loop(nch, step, init):
    assert nch % HG_UNROLL == 0

    def trip(i, carry):
        for u in range(HG_UNROLL):
            carry = step(i * HG_UNROLL + u, carry)
        return carry
    return lax.fori_loop(0, nch // HG_UNROLL, trip, init)


def _hg_gates(fc, lb):
    sg = jax.nn.sigmoid(fc)
    f = lb + (1.0 - lb) * sg
    return sg, f


def _hg_decay(f, incl):
    bcum = _dot_exact_lhs(incl, jnp.log(f))
    return bcum, bcum[HG_CHUNK - 1:HG_CHUNK, :]


def _dot_exact_lhs(m, x):
    hi, mid, lo = _split3(x)
    return _dot(m, hi, 1, 0) + (_dot(m, mid, 1, 0) + _dot(m, lo, 1, 0))


def hgrn_fwd(proj, lb, ng, nseq, lp):
    nch = lp // HG_CHUNK
    t = proj.shape[0]
    c = HG_CHUNK

    def body(q_ref, f_ref, v_ref, g_ref, lb_ref, ng_ref, o_ref, og_ref, st_ref):
        row = lax.broadcasted_iota(jnp.int32, (c, c), 0)
        col = lax.broadcasted_iota(jnp.int32, (c, c), 1)
        causal = col <= row
        incl = causal.astype(BF16)
        lbv, ngv = lb_ref[...], ng_ref[...]

        def step(ci, st):
            r0 = pl.multiple_of(ci * c, c)
            rows = pl.ds(r0, c)
            st_ref[ci] = st
            _, f = _hg_gates(f_ref[rows, :], lbv)
            bcum, blast = _hg_decay(f, incl)
            kk = 1.0 - f
            vc = v_ref[rows, :].astype(BF16)
            qd = (q_ref[rows, :] * jnp.exp(bcum)).astype(BF16)
            kd = (kk * jnp.exp(-bcum)).astype(BF16)
            a = jnp.where(causal, _dot(qd, kd, 1, 1), 0.0)
            o = _dot(a.astype(BF16), vc, 1, 0) + _dot(qd, st.astype(BF16), 1, 1)
            kend = (kk * jnp.exp(blast - bcum)).astype(BF16)
            st = st * jnp.exp(blast) + _dot(vc, kend, 0, 0)
            o_ref[rows, :] = o
            r = lax.rsqrt(jnp.mean(o * o, axis=1, keepdims=True) + RMS_EPS)
            gc = g_ref[rows, :]
            og_ref[rows, :] = (o * r * ngv * (gc * jax.nn.sigmoid(gc))).astype(og_ref.dtype)
            return st

        _hg_loop(nch, step, jnp.zeros((HG_DK, HG_DK), F32))

    blk = (lp, LANES)
    h = HG_HEADS
    return pl.pallas_call(
        body, name="hgrn_fwd", grid=(nseq, h),
        in_specs=[pl.BlockSpec(blk, lambda b, hh: (b, hh)),
                  pl.BlockSpec(blk, lambda b, hh: (b, h + hh)),
                  pl.BlockSpec(blk, lambda b, hh: (b, 2 * h + hh)),
                  pl.BlockSpec(blk, lambda b, hh: (b, 3 * h + hh)),
                  pl.BlockSpec((1, LANES), lambda b, hh: (0, hh)),
                  pl.BlockSpec((1, LANES), lambda b, hh: (0, hh))],
        out_specs=[pl.BlockSpec(blk, lambda b, hh: (b, hh)),
                   pl.BlockSpec(blk, lambda b, hh: (b, hh)),
                   pl.BlockSpec((None, None, nch, HG_DK, HG_DK), lambda b, hh: (b, hh, 0, 0, 0))],
        out_shape=[jax.ShapeDtypeStruct((t, D_MODEL), F32), jax.ShapeDtypeStruct((t, D_MODEL), BF16),
                   jax.ShapeDtypeStruct((nseq, h, nch, HG_DK, HG_DK), F32)],
        compiler_params=_params(("parallel", "parallel")))(proj, proj, proj, proj, lb, ng)


def hgrn_bwd(proj, o, dog, states, lb, ng, nseq, lp):
    nch = lp // HG_CHUNK
    t = proj.shape[0]
    c = HG_CHUNK

    def body(q_ref, f_ref, v_ref, g_ref, o_ref, dog_ref, st_ref, lb_ref, ng_ref,
             dq_ref, df_ref, dv_ref, dg_ref, dlb_ref, dng_ref):
        row = lax.broadcasted_iota(jnp.int32, (c, c), 0)
        col = lax.broadcasted_iota(jnp.int32, (c, c), 1)
        causal = col <= row
        incl = causal.astype(BF16)
        from_here = (col >= row).astype(BF16)
        last = lax.broadcasted_iota(jnp.int32, (c, 1), 0) == c - 1
        lbv, ngv = lb_ref[...], ng_ref[...]

        def step(s, carry):
            dst, dlb, dng = carry
            ci = nch - 1 - s
            r0 = pl.multiple_of(ci * c, c)
            rows = pl.ds(r0, c)
            oc, gc, dogc = o_ref[rows, :], g_ref[rows, :], dog_ref[rows, :]
            r = lax.rsqrt(jnp.mean(oc * oc, axis=1, keepdims=True) + RMS_EPS)
            sgg = jax.nn.sigmoid(gc)
            silu = gc * sgg
            dg_ref[rows, :] = dogc * (oc * r * ngv) * (sgg * (1.0 + gc * (1.0 - sgg)))
            don = dogc * silu
            dng = dng + _colsum(don * oc * r)
            tt = don * ngv
            do = r * (tt - oc * (r * r) * jnp.mean(tt * oc, axis=1, keepdims=True))
            st = st_ref[ci]
            fc = f_ref[rows, :]
            sg, f = _hg_gates(fc, lbv)
            bcum, blast = _hg_decay(f, incl)
            kk = 1.0 - f
            qc, vf = q_ref[rows, :], v_ref[rows, :]
            pdec = jnp.exp(bcum)
            ndec = jnp.exp(-bcum)
            edec = jnp.exp(blast - bcum)
            eb = jnp.exp(blast)
            qd_f, kd_f, kend_f = qc * pdec, kk * ndec, kk * edec
            qd, kd, kend = qd_f.astype(BF16), kd_f.astype(BF16), kend_f.astype(BF16)
            vc, dob, stb, dstb = vf.astype(BF16), do.astype(BF16), st.astype(BF16), dst.astype(BF16)
            a = jnp.where(causal, _dot(qd, kd, 1, 1), 0.0).astype(BF16)
            da = jnp.where(causal, _dot(dob, vc, 1, 1), 0.0).astype(BF16)
            dv_ref[rows, :] = _dot(a, dob, 0, 0) + _dot(kend, dstb, 1, 1)
            dqd = _dot(da, kd, 1, 0) + _dot(dob, stb, 1, 0)
            dkd = _dot(da, qd, 0, 0)
            dkend = _dot(vc, dstb, 1, 0)
            dq_ref[rows, :] = dqd * pdec
            dblast = _colsum(dkend * kend_f) + eb * _colsum(dst * st)
            dbcum = dqd * qd_f - dkd * kd_f - dkend * kend_f
            dbcum = dbcum + jnp.where(last, dblast, 0.0)
            dlf = _dot_exact_lhs(from_here, dbcum)
            df = dlf / f - (dkd * ndec + dkend * edec)
            df_ref[rows, :] = df * (1.0 - lbv) * sg * (1.0 - sg)
            dlb = dlb + _colsum(df * (1.0 - sg))
            dst = dst * eb + _dot(dob, qd, 0, 0)
            return dst, dlb, dng

        z = jnp.zeros((1, LANES), F32)
        _, dlb, dng = _hg_loop(nch, step, (jnp.zeros((HG_DK, HG_DK), F32), z, z))
        dlb_ref[...] = dlb
        dng_ref[...] = dng

    blk = (lp, LANES)
    h = HG_HEADS
    vec = pl.BlockSpec((1, LANES), lambda b, hh: (0, hh))
    svec = pl.BlockSpec((None, 1, LANES), lambda b, hh: (b, 0, hh))
    return pl.pallas_call(
        body, name="hgrn_bwd", grid=(nseq, h),
        in_specs=[pl.BlockSpec(blk, lambda b, hh: (b, hh)),
                  pl.BlockSpec(blk, lambda b, hh: (b, h + hh)),
                  pl.BlockSpec(blk, lambda b, hh: (b, 2 * h + hh)),
                  pl.BlockSpec(blk, lambda b, hh: (b, 3 * h + hh)),
                  pl.BlockSpec(blk, lambda b, hh: (b, hh)),
                  pl.BlockSpec(blk, lambda b, hh: (b, hh)),
                  pl.BlockSpec((None, None, nch, HG_DK, HG_DK), lambda b, hh: (b, hh, 0, 0, 0)),
                  vec, vec],
        out_specs=[pl.BlockSpec(blk, lambda b, hh: (b, hh))] * 4 + [svec, svec],
        out_shape=[jax.ShapeDtypeStruct((t, D_MODEL), F32)] * 4
        + [jax.ShapeDtypeStruct((nseq, 1, D_MODEL), F32)] * 2,
        compiler_params=_params(("parallel", "parallel")))(proj, proj, proj, proj, o, dog, states, lb, ng)


def _lower_bound(gamma):
    p = jax.nn.softmax(gamma, axis=0)
    return (jnp.cumsum(p, axis=0) - p[0])[1][None, :]


def local_step(x, tgt, w, hooks=None):
    nseq, seq, _ = x.shape
    lp = -(-(N_META + seq) // SB_BLOCK) * SB_BLOCK
    t = nseq * lp
    pad = lp - N_META - seq
    h0 = jnp.concatenate([jnp.broadcast_to(w['meta'][None], (nseq, N_META, D_MODEL)), x,
                          jnp.zeros((nseq, pad, D_MODEL), F32)], axis=1).reshape(t, D_MODEL)
    tgt_p = jnp.pad(tgt, ((0, 0), (N_META, pad), (0, 0))).reshape(t, D_MODEL)
    row = lambda v: v.reshape(1, -1)
    g = {}

    proj = matmul("in_ab", h0, w['w_in_ab'], 'nn')
    att = attn_fwd(proj, nseq, lp, ex=hooks.gather_late if hooks else None)
    b_out, sb_tot, sb_first = att[:3]
    if hooks:
        w = {**w, **hooks.late_weights(att[3:])}
    s5_names = ['s5_lam_re', 's5_lam_im', 's5_log_dt', 's5_b_re', 's5_b_im', 's5_c_re', 's5_c_im', 's5_d']
    mats, mats_vjp = jax.vjp(s5_matrices, *[w[n][0] for n in s5_names])
    ug = _to_groups(proj[:, :S5_WIDTH])
    s5_pows = s5_scan_powers(*[w[n][0] for n in s5_names[:3]])
    yg, sst = s5_fwd(ug, *mats[:3], *s5_pows, nseq)
    ys5 = _from_groups(yg)
    y = gelu_fwd("gelu", ys5)
    gp = matmul("glu", y, w['s5_w_glu'], 'nn', bias=w['s5_b_glu'])
    a_out = glu_gate("glu_gate", y, gp)
    cat = jnp.concatenate([a_out, b_out], axis=1)
    mix0 = matmul("out_ab", cat, w['w_out_ab'], 'nn')
    h1, xh1, rs1 = ln_fwd("ln_mix0", h0, mix0, row(w['ln_mix_g'][0]), row(w['ln_mix_b'][0]))
    hid0 = matmul("up0", h1, w['mlp_w_up'][0], 'nn', bias=row(w['mlp_b_up'][0]), act='relu2', out_dtype=BF16)
    dn0 = matmul("down0", hid0, w['mlp_w_down'][0], 'nn', bias=row(w['mlp_b_down'][0]))
    h2, xh2, rs2 = ln_fwd("ln_mlp0", h1, dn0, row(w['ln_mlp_g'][0]), row(w['ln_mlp_b'][0]))

    projc = matmul("in_c", h2, w['w_in_c'], 'nn')
    lb, lb_vjp = jax.vjp(_lower_bound, w['hgrn_gamma'])
    ng = w['hgrn_norm_g']
    o, og, states = hgrn_fwd(projc, lb, ng, nseq, lp)
    mix1 = matmul("out_c", og, w['w_out_c'], 'nn')
    h3, xh3, rs3 = ln_fwd("ln_mix1", h2, mix1, row(w['ln_mix_g'][1]), row(w['ln_mix_b'][1]))
    hid1 = matmul("up1", h3, w['mlp_w_up'][1], 'nn', bias=row(w['mlp_b_up'][1]), act='relu2', out_dtype=BF16)
    dn1 = matmul("down1", hid1, w['mlp_w_down'][1], 'nn', bias=row(w['mlp_b_down'][1]))
    h4, xh4, rs4 = ln_fwd("ln_mlp1", h3, dn1, row(w['ln_mlp_g'][1]), row(w['ln_mlp_b'][1]))
    dy, loss = loss_head("loss", h4, tgt_p, lp, seq)

    def mlp_bwd(l, dh_out, xh_out, rs_out, hid, h_in):
        dpre, dg_, db_, dbd = ln_bwd(f"ln_mlp{l}_b", dh_out, xh_out, rs_out, row(w['ln_mlp_g'][l]))
        dpu = matmul(f"down{l}_dx", dpre, w['mlp_w_down'][l], 'nt', relu2_of=hid, out_dtype=BF16)
        dwd = matmul(f"down{l}_dw", hid, dpre, 'tn')
        dh_in = matmul(f"up{l}_dx", dpu, w['mlp_w_up'][l], 'nt', add=dpre, add_scale=ALPHA)
        dwu, dbu = matmul(f"up{l}_dw", h_in, dpu, 'tn', out_slots=N_CHIPS, colsum=True)
        return dh_in, dict(ln_mlp_g=dg_, ln_mlp_b=db_, mlp_b_down=dbd, mlp_b_up=dbu, mlp_w_up=dwu, mlp_w_down=dwd)

    dh3, gm1 = mlp_bwd(1, dy, xh4, rs4, hid1, h3)
    dpre, dg1, db1, _ = ln_bwd("ln_mix1_b", dh3, xh3, rs3, row(w['ln_mix_g'][1]))
    g['w_out_c'] = matmul("out_c_dw", og, dpre, 'tn')
    dog = matmul("out_c_dx", dpre, w['w_out_c'], 'nt')
    dq, df, di, dgg, dlb, dng = hgrn_bwd(projc, o, dog, states, lb, ng, nseq, lp)
    dprojc = jnp.concatenate([dq, df, di, dgg], axis=1)
    dh2 = matmul("in_c_dx", dprojc, w['w_in_c'], 'nt', add=dpre, add_scale=ALPHA)
    g['w_in_c'] = matmul("in_c_dw", h2, dprojc, 'tn', out_slots=N_CHIPS)
    g['hgrn_gamma'] = lb_vjp(jnp.sum(dlb, axis=0))[0]
    g['hgrn_norm_g'] = jnp.sum(dng, axis=0)

    dh1, gm0 = mlp_bwd(0, dh2, xh2, rs2, hid0, h1)
    dpre, dg0, db0, _ = ln_bwd("ln_mix0_b", dh1, xh1, rs1, row(w['ln_mix_g'][0]))
    g['w_out_ab'] = matmul("out_ab_dw", cat, dpre, 'tn')
    dcat = matmul("out_ab_dx", dpre, w['w_out_ab'], 'nt')
    dgp, da_s, dbglu = glu_gate_bwd("glu_gate_b", dcat[:, :S5_WIDTH], y, gp)
    g['s5_w_glu'] = matmul("glu_dw", y, dgp, 'tn')
    g['s5_b_glu'] = dbglu
    dy5 = matmul("glu_dx", dgp, w['s5_w_glu'], 'nt', add=da_s)
    dys5 = gelu_bwd("gelu_b", dy5, ys5)
    dug, da1, da2, da3, dare, daim = s5_bwd(_to_groups(dys5), ug, sst, *mats[:3], *s5_pows, nseq)
    for n, v in zip(s5_names, mats_vjp((da1, da2, da3, dare, daim))):
        g[n] = v[None]
    for n in ('mlp_w_up', 'mlp_w_down'):
        g[n] = [gm0[n], gm1[n]]
    res = attn_bwd(proj, sb_tot, sb_first, dcat[:, S5_WIDTH:], nseq, lp, ex=hooks.scatter_early(g) if hooks else None)
    dqa, dka, dva = res[:3]
    if hooks:
        hooks.scattered(res[3:])
    dproj = jnp.concatenate([_from_groups(dug), dqa, dka, dva], axis=1)
    dh0 = matmul("in_ab_dx", dproj, w['w_in_ab'], 'nt', add=dpre, add_scale=ALPHA)
    g['w_in_ab'] = matmul("in_ab_dw", h0, dproj, 'tn', out_slots=N_CHIPS)

    dh0 = dh0.reshape(nseq, lp, D_MODEL)
    grad_x = dh0[:, N_META:N_META + seq]
    g['meta'] = jnp.sum(dh0[:, :N_META], axis=0)
    g['ln_mix_g'] = jnp.concatenate([dg0, dg1], axis=0)
    g['ln_mix_b'] = jnp.concatenate([db0, db1], axis=0)
    for n in ('ln_mlp_g', 'ln_mlp_b', 'mlp_b_down', 'mlp_b_up'):
        g[n] = jnp.concatenate([gm0[n], gm1[n]], axis=0)
    return loss, grad_x, g


class Exchange:
    def __init__(self, ins, out_shapes, n_remote, build, in_place):
        self.ins, self.out_shapes, self.n_remote, self.build, self.in_place = ins, out_shapes, n_remote, build, in_place

    def sems(self):
        return [pltpu.SemaphoreType.DMA((self.n_remote,)), pltpu.SemaphoreType.DMA((self.n_remote,))]

    def phases(self, in_refs, out_refs, ssem, rsem):
        me = (lax.axis_index("x"), lax.axis_index("y"), lax.axis_index("c"))
        out, k = [], 0
        for phase in self.build(in_refs, out_refs, me):
            out.append(functools.partial(
                lambda phase, k: [pltpu.make_async_remote_copy(src_ref=s, dst_ref=d, send_sem=ssem.at[k + i],
                                                               recv_sem=rsem.at[k + i], device_id=p,
                                                               device_id_type=MESH)
                                  for i, (s, d, p) in enumerate(phase)], phase, k))
            k += len(phase)
        return out

    def start(self, in_refs, out_refs, ssem, rsem):
        for cp in self.phases(in_refs, out_refs, ssem, rsem)[0]():
            cp.start()

    def finish(self, in_refs, out_refs, ssem, rsem):
        phases = self.phases(in_refs, out_refs, ssem, rsem)
        for cp in phases[0]():
            cp.wait()
        for make in phases[1:]:
            copies = make()
            for cp in copies:
                cp.start()
            for cp in copies:
                cp.wait()


def _exchange(name, ex):
    ni, no = len(ex.ins), len(ex.out_shapes)

    def body(*refs):
        in_refs, out_refs, sems = refs[:ni], refs[ni:ni + no], refs[ni + no:]
        ex.start(in_refs, out_refs, *sems)
        ex.finish(in_refs, out_refs, *sems)

    anyspec = pl.BlockSpec(memory_space=pl.ANY)
    return pl.pallas_call(
        body, name=name, in_specs=[anyspec] * ni, out_specs=[anyspec] * no, out_shape=ex.out_shapes,
        input_output_aliases={i: i for i in range(ni if ex.in_place else 0)}, scratch_shapes=ex.sems())(*ex.ins)


def _chip_peers(me):
    x, y, c = me
    return [(x ^ (m >> 1), y ^ (m & 1), c) for m in (1, 2, 3)]


def _half(ref, c, axis):
    h = ref.shape[axis] // 2
    idx = [slice(None)] * axis + [pl.ds(c * h, h)]
    return ref.at[tuple(idx)]


def _like(arrs):
    return [jax.ShapeDtypeStruct(a.shape, a.dtype) for a in arrs]


def gather_over_chips(bufs):
    def build(in_refs, out_refs, me):
        x, y, c = me
        j = 2 * x + y
        sib = (x, y, 1 - c)
        ici = [(_half(o.at[j], c, 0), _half(o.at[j], c, 0), p) for o in out_refs for p in _chip_peers(me)]
        d2d = [(_half(o.at[2 * p[0] + p[1]], c, 0), _half(o.at[2 * p[0] + p[1]], c, 0), sib)
               for o in out_refs for p in _chip_peers(me)]
        return [ici, d2d]
    return Exchange(bufs, _like(bufs), 6 * len(bufs), build, True)


def fold_cores(fulls):
    def build(in_refs, out_refs, me):
        x, y, c = me
        return [[(_half(s, 1 - c, 1), o, (x, y, 1 - c)) for s, o in zip(in_refs, out_refs)]]
    shapes = [jax.ShapeDtypeStruct((f.shape[0], f.shape[1] // 2, f.shape[2]), f.dtype) for f in fulls]
    return Exchange(fulls, shapes, len(fulls), build, False)


def scatter_over_chips(parts):
    def build(in_refs, out_refs, me):
        j = 2 * me[0] + me[1]
        return [[(s.at[2 * p[0] + p[1]], o.at[j], p) for s, o in zip(in_refs, out_refs) for p in _chip_peers(me)]]
    return Exchange(parts, _like(parts), 3 * len(parts), build, False)


def join_cores(bufs):
    def build(in_refs, out_refs, me):
        x, y, c = me
        return [[(o.at[c], o.at[c], (x, y, 1 - c)) for o in out_refs]]
    return Exchange(bufs, _like(bufs), len(bufs), build, True)


def gather_over_devices(buf):
    def build(in_refs, out_refs, me):
        x, y, c = me
        i = 4 * x + 2 * y + c
        out = out_refs[0]
        return [[(out.at[i], out.at[i], (x ^ (m >> 2), y ^ ((m >> 1) & 1), c ^ (m & 1))) for m in range(1, N_DEV)]]
    return Exchange([buf], _like([buf]), N_DEV - 1, build, True)


def _slot_call(name, body, scalars, ins, in_maps, out_shape, out_map, blk, grid):
    gs = pltpu.PrefetchScalarGridSpec(
        num_scalar_prefetch=1, grid=grid,
        in_specs=[pl.BlockSpec((None,) + blk, m) for m in in_maps],
        out_specs=pl.BlockSpec((None,) + blk, out_map))
    return pl.pallas_call(body, name=name, grid_spec=gs, out_shape=out_shape,
                          compiler_params=_params(("arbitrary",) * len(grid)))(scalars, *ins)


def cast_into_slot(name, w2d, chip, dtype):
    r, wd = w2d.shape
    tm = _pick(r, (512, 256, 128, 64, 32, 16))

    def body(s_ref, w_ref, o_ref):
        o_ref[...] = w_ref[...].astype(o_ref.dtype)

    gs = pltpu.PrefetchScalarGridSpec(
        num_scalar_prefetch=1, grid=(r // tm,),
        in_specs=[pl.BlockSpec((tm, wd), lambda i, s: (i, 0))],
        out_specs=pl.BlockSpec((None, tm, wd), lambda i, s: (s[0], i, 0)))
    return pl.pallas_call(body, name=name, grid_spec=gs,
                          out_shape=jax.ShapeDtypeStruct((N_CHIPS, r, wd), dtype),
                          compiler_params=_params(("arbitrary",)))(chip.reshape(1), w2d)


def sum_cores(name, full, theirs, core):
    s, r, wd = full.shape
    h = r // 2
    tm = _pick(h, (512, 256, 128, 64, 32, 16))
    nt = h // tm

    def body(s_ref, a_ref, b_ref, o_ref):
        o_ref[...] = (a_ref[...] + b_ref[...]).astype(o_ref.dtype)

    return _slot_call(name, body, core.reshape(1), [full, theirs],
                      [lambda k, i, s_: (k, s_[0] * nt + i, 0), lambda k, i, s_: (k, i, 0)],
                      jax.ShapeDtypeStruct((s, h, wd), BF16), lambda k, i, s_: (k, i, 0), (tm, wd), (s, nt))


def sum_chips(name, own, recv, chip, core):
    s, r, wd = own.shape
    tm = _pick(r, (512, 256, 128, 64, 32, 16))

    def body(s_ref, a_ref, b1_ref, b2_ref, b3_ref, o_ref):
        acc = a_ref[...].astype(F32)
        for ref in (b1_ref, b2_ref, b3_ref):
            acc = acc + ref[...].astype(F32)
        o_ref[...] = acc

    maps = [lambda i, s_: (s_[0], i, 0)]
    maps += [functools.partial(lambda i, s_, m: (s_[0] ^ m, i, 0), m=m) for m in (1, 2, 3)]
    return _slot_call(name, body, jnp.stack([chip, core]), [own, recv, recv, recv], maps,
                      jax.ShapeDtypeStruct((2, r, wd), F32), lambda i, s_: (s_[1], i, 0), (tm, wd), (r // tm,))


def sum_slots(name, arr):
    s, r, wd = arr.shape
    tm = _pick(r, (512, 256, 128, 64, 32, 16, 8))

    def body(*refs):
        acc = refs[0][...].astype(F32)
        for ref in refs[1:s]:
            acc = acc + ref[...].astype(F32)
        refs[s][...] = acc

    specs = [pl.BlockSpec((None, tm, wd), functools.partial(lambda i, k: (k, i, 0), k=k)) for k in range(s)]
    return pl.pallas_call(body, name=name, grid=(r // tm,), in_specs=specs,
                          out_specs=pl.BlockSpec((tm, wd), lambda i: (i, 0)),
                          out_shape=jax.ShapeDtypeStruct((r, wd), F32),
                          compiler_params=_params(("parallel",)))(*([arr] * s))


def _pack(arrs):
    flat = jnp.concatenate([a.reshape(-1) for a in arrs])
    n = flat.shape[0]
    padded = -(-n // (512 * LANES)) * (512 * LANES)
    return jnp.pad(flat, (0, padded - n)).reshape(-1, LANES)


def _unpack(packed, shapes):
    flat = packed.reshape(-1)
    out, pos = [], 0
    for s in shapes:
        n = math.prod(s)
        out.append(flat[pos:pos + n].reshape(s))
        pos += n
    return out


def _as2d(a):
    return a.reshape(-1, a.shape[-1])


def kernel(x, meta, w_in_ab, s5_lam_re, s5_lam_im, s5_log_dt, s5_b_re, s5_b_im, s5_c_re, s5_c_im, s5_d, s5_w_glu, s5_b_glu, w_out_ab, w_in_c, hgrn_gamma, hgrn_norm_g, w_out_c, ln_mix_g, ln_mix_b, mlp_w_up, mlp_b_up, mlp_w_down, mlp_b_down, ln_mlp_g, ln_mlp_b, loss_target, m_meta, m_w_in_ab, m_s5_lam_re, m_s5_lam_im, m_s5_log_dt, m_s5_b_re, m_s5_b_im, m_s5_c_re, m_s5_c_im, m_s5_d, m_s5_w_glu, m_s5_b_glu, m_w_out_ab, m_w_in_c, m_hgrn_gamma, m_hgrn_norm_g, m_w_out_c, m_ln_mix_g, m_ln_mix_b, m_mlp_w_up, m_mlp_b_up, m_mlp_w_down, m_mlp_b_down, m_ln_mlp_g, m_ln_mlp_b, v_meta, v_w_in_ab, v_s5_lam_re, v_s5_lam_im, v_s5_log_dt, v_s5_b_re, v_s5_b_im, v_s5_c_re, v_s5_c_im, v_s5_d, v_s5_w_glu, v_s5_b_glu, v_w_out_ab, v_w_in_c, v_hgrn_gamma, v_hgrn_norm_g, v_w_out_c, v_ln_mix_g, v_ln_mix_b, v_mlp_w_up, v_mlp_b_up, v_mlp_w_down, v_mlp_b_down, v_ln_mlp_g, v_ln_mlp_b):
    given = dict(locals())
    wts = {n: given[n] for n in WEIGHTS}
    ms = {n: given['m_' + n] for n in WEIGHTS}
    vs = {n: given['v_' + n] for n in WEIGHTS}
    chip = 2 * lax.axis_index("x") + lax.axis_index("y")

    core = lax.axis_index("c")
    parts = [(n, l) for n in BIG for l in (range(DEPTH) if n.startswith('mlp_') else [0])]
    tags = [f"{n}{l}" for n, l in parts]
    shards = {p: cast_into_slot("cast_" + t, wts[p[0]][p[1]], chip, BF16) for t, p in zip(tags, parts)}
    small_sh = jnp.concatenate([meta, hgrn_norm_g, jnp.zeros((15, meta.shape[1]), F32)], axis=0)
    first, late = parts[:1], parts[1:]
    w_in_ab_all, sm = _exchange("gather_first", gather_over_chips(
        [shards[p] for p in first] + [cast_into_slot("cast_small", small_sh, chip, F32)]))
    w = {n: wts[n] for n in SMALL}
    w['meta'] = sm[:, :N_META].transpose(1, 0, 2).reshape(N_META, D_MODEL)
    w['hgrn_norm_g'] = sm[:, N_META:N_META + 1].transpose(1, 0, 2).reshape(1, D_MODEL)
    w['w_in_ab'] = w_in_ab_all

    def slot_major(v):
        return v if v.ndim == 3 else v.reshape(N_CHIPS, -1, v.shape[-1])

    def chip_sums_of(ps, g, tag):
        gfull = [slot_major(g[n][l] if n.startswith('mlp_') else g[n]) for n, l in ps]
        theirs = _exchange("fold_grads" + tag, fold_cores(gfull))
        return [sum_cores(f"sum_cores_{n}{l}", a, b, core) for (n, l), a, b in zip(ps, gfull, theirs)]

    class Hooks:
        gather_late = gather_over_chips([shards[p] for p in late])

        @staticmethod
        def late_weights(filled):
            fw = dict(zip(late, filled))
            return {'s5_w_glu': fw['s5_w_glu', 0].reshape(S5_WIDTH, S5_WIDTH),
                    'w_out_ab': fw['w_out_ab', 0].reshape(D_MODEL, D_MODEL),
                    'w_in_c': fw['w_in_c', 0],
                    'w_out_c': fw['w_out_c', 0].reshape(D_MODEL, D_MODEL),
                    'mlp_w_up': [fw['mlp_w_up', l] for l in range(DEPTH)],
                    'mlp_w_down': [fw['mlp_w_down', l].reshape(D_FF, D_MODEL) for l in range(DEPTH)]}

        @staticmethod
        def scatter_early(g):
            Hooks.sums = chip_sums_of(late, g, "_early")
            return scatter_over_chips(Hooks.sums)

        @staticmethod
        def scattered(recv):
            Hooks.recv = list(recv)

    loss, grad_x, g = local_step(x, loss_target, w, Hooks)

    sums = chip_sums_of(first, g, "_last")
    recv = _exchange("scatter_last", scatter_over_chips(sums))
    reduced = _exchange("join_grads", join_cores(
        [sum_chips(f"sum_chips_{n}{l}", a, b, chip, core)
         for (n, l), a, b in zip(first + late, sums + Hooks.sums, list(recv) + Hooks.recv)]))
    reduced = dict(zip(first + late, [_as2d(r) for r in reduced]))

    small_shapes = [(LANES,)] + [g[n].shape for n in SMALL]
    packed = _pack([loss.reshape(-1)] + [g[n] for n in SMALL])
    slots = lax.dynamic_update_slice(jnp.zeros((N_DEV,) + packed.shape, F32), packed[None],
                                     (2 * chip + core, 0, 0))
    red = sum_slots("sum_small", _exchange("gather_small", gather_over_devices(slots))[0])
    red = _unpack(red, small_shapes)
    loss_out = red[0][0]
    gs = dict(zip(SMALL, red[1:]))
    gs['meta'] = lax.dynamic_slice_in_dim(gs['meta'], chip * meta.shape[1], meta.shape[1], axis=1)
    gs['hgrn_norm_g'] = lax.dynamic_slice_in_dim(gs['hgrn_norm_g'].reshape(1, -1), chip * meta.shape[1],
                                                 meta.shape[1], axis=1)

    grads, deltas, new_m, new_v = {}, {}, {}, {}
    for n in BIG:
        r = jnp.concatenate([reduced[n, l] for l in range(wts[n].shape[0])], axis=0)
        res = adamw("adamw_" + n, _as2d(wts[n]), [r], _as2d(ms[n]), _as2d(vs[n]))
        grads[n], deltas[n], new_m[n], new_v[n] = [r.reshape(wts[n].shape) for r in res]
    shapes = [wts[n].shape for n in SMALL]
    res = adamw("adamw_small", _pack([wts[n] for n in SMALL]), [_pack([gs[n] for n in SMALL])],
                _pack([ms[n] for n in SMALL]), _pack([vs[n] for n in SMALL]))
    for d, r in zip((grads, deltas, new_m, new_v), res):
        d.update(zip(SMALL, _unpack(r, shapes)))
    return (loss_out, grad_x, *[grads[n] for n in WEIGHTS], *[deltas[n] for n in WEIGHTS],
            *[new_m[n] for n in WEIGHTS], *[new_v[n] for n in WEIGHTS])
```

```python
import functools
import math

import jax
import jax.numpy as jnp
from jax import lax
from jax.experimental import pallas as pl
from jax.experimental.pallas import tpu as pltpu

F32 = jnp.float32
BF16 = jnp.bfloat16

D_MODEL = 1024
N_META = 16
DEPTH = 2
S5_WIDTH = 512
S5_GROUP = 16
S5_GROUPS = 32
S5_STATE = 64
S5_CHUNK = 16
SB_WIDTH = 512
SB_BLOCK = 128
SB_DEAD = -110.0
HG_HEADS = 8
HG_DK = 128
HG_CHUNK = 64
HG_UNROLL = 2
D_FF = 4096
ALPHA = (2.0 * DEPTH) ** 0.25
LN_EPS = 1e-5
RMS_EPS = 1e-6
ADAM_LR = 0.001
ADAM_B1 = 0.9
ADAM_B2 = 0.999
ADAM_EPS = 1e-08
ADAM_WD = 0.01
ADAM_STEP = 10
N_CHIPS = 4
N_DEV = 8
LANES = 128
MESH = pl.DeviceIdType.MESH
VMEM_LIMIT = 56 * 1024 * 1024
WHOLE_K_BYTES = 42 * 1024 * 1024

WEIGHTS = ['meta', 'w_in_ab', 's5_lam_re', 's5_lam_im', 's5_log_dt', 's5_b_re', 's5_b_im', 's5_c_re', 's5_c_im',
           's5_d', 's5_w_glu', 's5_b_glu', 'w_out_ab', 'w_in_c', 'hgrn_gamma', 'hgrn_norm_g', 'w_out_c',
           'ln_mix_g', 'ln_mix_b', 'mlp_w_up', 'mlp_b_up', 'mlp_w_down', 'mlp_b_down', 'ln_mlp_g', 'ln_mlp_b']
BIG = ['w_in_ab', 's5_w_glu', 'w_out_ab', 'w_in_c', 'w_out_c', 'mlp_w_up', 'mlp_w_down']
SHARDED_SMALL = ['meta', 'hgrn_norm_g']
SMALL = [n for n in WEIGHTS if n not in BIG]


def _params(sem=None):
    return pltpu.CompilerParams(dimension_semantics=sem, vmem_limit_bytes=VMEM_LIMIT)


def _pick(n, cands):
    for c in cands:
        if n % c == 0:
            return c
    return n


def _dot(a, b, ca, cb):
    return lax.dot_general(a, b, (((ca,), (cb,)), ((), ())), preferred_element_type=F32)


def _split3(x):
    hi = x.astype(BF16)
    r = x - hi.astype(F32)
    mid = r.astype(BF16)
    lo = (r - mid.astype(F32)).astype(BF16)
    return hi, mid, lo


def _dot_exact_rhs(x, m, cx, cm):
    hi = x.astype(BF16)
    lo = (x - hi.astype(F32)).astype(BF16)
    return _dot(hi, m, cx, cm) + _dot(lo, m, cx, cm)


def _dot3(a, b, ca, cb):
    ah = a.astype(BF16)
    al = (a - ah.astype(F32)).astype(BF16)
    bh = b.astype(BF16)
    bl = (b - bh.astype(F32)).astype(BF16)
    return _dot(ah, bh, ca, cb) + (_dot(ah, bl, ca, cb) + _dot(al, bh, ca, cb))


def rowwise(name, fn, row_ins, bc_ins, out_widths, sum_widths=(), out_dtypes=None, tm=None):
    t = row_ins[0].shape[0]
    if tm is None:
        wmax = max([a.shape[1] for a in row_ins] + list(out_widths))
        tm = _pick(t, (272, 256, 128, 64, 16, 8)) if wmax > 1024 else _pick(t, (544, 512, 256, 128, 64, 16, 8))
    nr, nb, no, ns = len(row_ins), len(bc_ins), len(out_widths), len(sum_widths)
    out_dtypes = out_dtypes or [F32] * no

    def body(*refs):
        i = pl.program_id(0)
        rows = [r[...] for r in refs[:nr]]
        bcs = [r[...] for r in refs[nr:nr + nb]]
        outs, sums = fn(i, tm, rows, bcs)
        for r, v in zip(refs[nr + nb:nr + nb + no], outs):
            r[...] = v.astype(r.dtype)
        if ns:
            srefs = refs[nr + nb + no:]

            @pl.when(i == 0)
            def _():
                for r in srefs:
                    r[...] = jnp.zeros(r.shape, r.dtype)

            for r, v in zip(srefs, sums):
                r[...] += v

    in_specs = [pl.BlockSpec((tm, a.shape[1]), lambda i: (i, 0)) for a in row_ins]
    in_specs += [pl.BlockSpec(a.shape, lambda i: (0, 0)) for a in bc_ins]
    out_specs = [pl.BlockSpec((tm, w), lambda i: (i, 0)) for w in out_widths]
    out_specs += [pl.BlockSpec((1, w), lambda i: (0, 0)) for w in sum_widths]
    out_shape = [jax.ShapeDtypeStruct((t, w), dt) for w, dt in zip(out_widths, out_dtypes)]
    out_shape += [jax.ShapeDtypeStruct((1, w), F32) for w in sum_widths]
    res = pl.pallas_call(body, name=name, grid=(t // tm,), in_specs=in_specs, out_specs=out_specs,
                         out_shape=out_shape, compiler_params=_params(("arbitrary",)))(*row_ins, *bc_ins)
    return res


def _colsum(v):
    return jnp.sum(v, axis=0, keepdims=True)


def matmul(name, a, b, mode, *, bias=None, act=None, add=None, add_scale=1.0, relu2_of=None, colsum=False,
           out_dtype=F32, out_slots=0, tm=None, tn=None, tk=None):
    slotted = b.ndim == 3
    if mode == 'nn':
        m, k = a.shape
        n = b.shape[-1] * (b.shape[0] if slotted else 1)
    elif mode == 'nt':
        m, k = a.shape
        n = b.shape[-2]
    else:
        k, m = a.shape
        n = b.shape[-1]
    ns = b.shape[-1] if slotted else None
    if mode == 'tn':
        tm = tm or _pick(m, (512, 256, 128))
        nw = n if not out_slots else n // out_slots
        if tn is None and tk is None:
            for cand in (512, 256):
                if nw % cand == 0 and (2 * k * (tm * a.dtype.itemsize + cand * b.dtype.itemsize)
                                       + 2 * tm * cand * 4 <= WHOLE_K_BYTES):
                    tn, tk = cand, k
                    break
        tn = tn or _pick(nw, (1024, 512, 256, 128))
        tk = tk or _pick(k, (1088, 1024, 768, 512, 384, 256, 128))
    else:
        tm = tm or _pick(m, (1088, 1024, 768, 512, 384, 256, 128))
        tn = tn or _pick(n if not (slotted and mode == 'nn') else ns, (1024, 512, 256, 128))
        extra = sum(x is not None for x in (add, relu2_of)) * 4
        if tk is None and not slotted and (2 * (tm * k * a.dtype.itemsize + k * tn * b.dtype.itemsize)
                                           + 2 * tm * tn * (4 + extra) <= WHOLE_K_BYTES):
            tk = k
        tk = tk or _pick(k if not (slotted and mode == 'nt') else ns, (1024, 512, 256, 128))
    gm, gn, gk = m // tm, n // tn, k // tk

    if mode == 'nn':
        a_spec = pl.BlockSpec((tm, tk), lambda i, j, kk: (i, kk))
        if slotted:
            per = ns // tn
            b_spec = pl.BlockSpec((None, tk, tn), lambda i, j, kk: (j // per, kk, j % per))
        else:
            b_spec = pl.BlockSpec((tk, tn), lambda i, j, kk: (kk, j))
        ca, cb = 1, 0
    elif mode == 'nt':
        a_spec = pl.BlockSpec((tm, tk), lambda i, j, kk: (i, kk))
        if slotted:
            per = ns // tk
            b_spec = pl.BlockSpec((None, tn, tk), lambda i, j, kk: (kk // per, j, kk % per))
        else:
            b_spec = pl.BlockSpec((tn, tk), lambda i, j, kk: (j, kk))
        ca, cb = 1, 1
    else:
        a_spec = pl.BlockSpec((tk, tm), lambda i, j, kk: (kk, i))
        b_spec = pl.BlockSpec((tk, tn), lambda i, j, kk: (kk, j))
        ca, cb = 0, 0
    in_specs = [a_spec, b_spec]
    args = [a, b]
    if bias is not None:
        in_specs.append(pl.BlockSpec((1, tn), lambda i, j, kk: (0, j)))
        args.append(bias)
    if add is not None:
        in_specs.append(pl.BlockSpec((tm, tn), lambda i, j, kk: (i, j)))
        args.append(add)
    if relu2_of is not None:
        in_specs.append(pl.BlockSpec((tm, tn), lambda i, j, kk: (i, j)))
        args.append(relu2_of)
    if out_slots:
        per_o = (n // out_slots) // tn
        out_spec = pl.BlockSpec((None, tm, tn), lambda i, j, kk: (j // per_o, i, j % per_o))
        out_shape = jax.ShapeDtypeStruct((out_slots, m, n // out_slots), out_dtype)
    else:
        out_spec = pl.BlockSpec((tm, tn), lambda i, j, kk: (i, j))
        out_shape = jax.ShapeDtypeStruct((m, n), out_dtype)
    grid = (gm, gn, gk)
    if colsum:
        assert mode == 'tn'
        out_spec = [out_spec, pl.BlockSpec((1, tn), lambda i, j, kk: (0, j))]
        out_shape = [out_shape, jax.ShapeDtypeStruct((1, n), F32)]

        def swapped(spec):
            return pl.BlockSpec(spec.block_shape, functools.partial(lambda j, i, kk, f: f(i, j, kk), f=spec.index_map))
        in_specs = [swapped(sp) for sp in in_specs]
        out_spec = [swapped(sp) for sp in out_spec]
        grid = (gn, gm, gk)

    def body(*refs):
        a_ref, b_ref = refs[0], refs[1]
        pos = 2
        bias_ref = add_ref = hid_ref = cs_ref = None
        if bias is not None:
            bias_ref = refs[pos]
            pos += 1
        if add is not None:
            add_ref = refs[pos]
            pos += 1
        if relu2_of is not None:
            hid_ref = refs[pos]
            pos += 1
        o_ref = refs[pos]
        pos += 1
        if colsum:
            cs_ref = refs[pos]
            pos += 1
        acc_ref = refs[pos] if gk > 1 else None
        bt = b_ref[...]
        p = _dot(a_ref[...].astype(BF16), bt.astype(BF16), ca, cb)
        kk = pl.program_id(2)

        if colsum:
            @pl.when(jnp.logical_and(pl.program_id(1) == 0, kk == 0))
            def _():
                cs_ref[...] = _colsum(bt.astype(F32))

            @pl.when(jnp.logical_and(pl.program_id(1) == 0, kk > 0))
            def _():
                cs_ref[...] += _colsum(bt.astype(F32))

        def finish(r):
            if bias_ref is not None:
                r = r + bias_ref[...]
            if act == 'relu2':
                r = jnp.square(jnp.maximum(r, 0.0))
            if hid_ref is not None:
                r = r * (2.0 * jnp.sqrt(hid_ref[...].astype(F32)))
            if add_ref is not None:
                r = r + add_scale * add_ref[...]
            o_ref[...] = r.astype(o_ref.dtype)

        if gk == 1:
            finish(p)
        else:
            @pl.when(kk == 0)
            def _():
                acc_ref[...] = p

            @pl.when(kk > 0)
            def _():
                acc_ref[...] += p

            @pl.when(kk == gk - 1)
            def _():
                finish(acc_ref[...])

    scratch = [pltpu.VMEM((tm, tn), F32)] if gk > 1 else []
    sem = ("arbitrary",) * 3 if colsum else ("parallel", "parallel", "arbitrary")
    return pl.pallas_call(body, name=name, grid=grid, in_specs=in_specs, out_specs=out_spec,
                          out_shape=out_shape, scratch_shapes=scratch, compiler_params=_params(sem))(*args)


def ln_fwd(name, h, mix, g, b):
    def fn(i, tm, rows, bcs):
        pre = ALPHA * rows[0] + rows[1]
        mu = jnp.mean(pre, axis=1, keepdims=True)
        xc = pre - mu
        var = jnp.mean(xc * xc, axis=1, keepdims=True)
        rstd = lax.rsqrt(var + LN_EPS)
        xhat = xc * rstd
        return (xhat * bcs[0] + bcs[1], xhat, rstd), ()
    return rowwise(name, fn, [h, mix], [g, b], [D_MODEL, D_MODEL, 1])


def ln_bwd(name, dy, xhat, rstd, g):
    def fn(i, tm, rows, bcs):
        dy_, xh, rs = rows
        dyg = dy_ * bcs[0]
        m1 = jnp.mean(dyg, axis=1, keepdims=True)
        m2 = jnp.mean(dyg * xh, axis=1, keepdims=True)
        dpre = rs * (dyg - m1 - xh * m2)
        return (dpre,), (_colsum(dy_ * xh), _colsum(dy_), _colsum(dpre))
    return rowwise(name, fn, [dy, xhat, rstd], [g], [D_MODEL], [D_MODEL] * 3)


_GELU_C = math.sqrt(2.0 / math.pi)


def gelu_fwd(name, x):
    def fn(i, tm, rows, bcs):
        v = rows[0]
        u = _GELU_C * (v + 0.044715 * (v * v * v))
        return (0.5 * v * (1.0 + jnp.tanh(u)),), ()
    return rowwise(name, fn, [x], [], [x.shape[1]])[0]


def gelu_bwd(name, dy, x):
    def fn(i, tm, rows, bcs):
        v = rows[1]
        th = jnp.tanh(_GELU_C * (v + 0.044715 * (v * v * v)))
        dg = 0.5 * (1.0 + th) + 0.5 * v * (1.0 - th * th) * (_GELU_C * (1.0 + 3.0 * 0.044715 * v * v))
        return (rows[0] * dg,), ()
    return rowwise(name, fn, [dy, x], [], [x.shape[1]])[0]


def glu_gate(name, y, gp):
    def fn(i, tm, rows, bcs):
        return (rows[0] * jax.nn.sigmoid(rows[1]),), ()
    return rowwise(name, fn, [y, gp], [], [y.shape[1]], out_dtypes=[BF16])[0]


def glu_gate_bwd(name, da, y, gp):
    def fn(i, tm, rows, bcs):
        s = jax.nn.sigmoid(rows[2])
        dgp = rows[0] * rows[1] * s * (1.0 - s)
        return (dgp, rows[0] * s), (_colsum(dgp),)
    w = y.shape[1]
    return rowwise(name, fn, [da, y, gp], [], [w, w], [w])


def loss_head(name, h, tgt, lp, seq):
    def fn(i, tm, rows, bcs):
        pos = (i * tm + lax.broadcasted_iota(jnp.int32, (tm, 1), 0)) % lp
        valid = jnp.logical_and(pos >= N_META, pos < N_META + seq)
        err = jnp.where(valid, rows[0] - rows[1], 0.0)
        part = 0.5 * jnp.sum(jnp.mean(err * err, axis=1, keepdims=True), axis=0, keepdims=True)
        return (err * (1.0 / D_MODEL),), (jnp.broadcast_to(part, (1, LANES)),)
    return rowwise(name, fn, [h, tgt], [], [D_MODEL], [LANES])


def adamw(name, w, gs, m, v):
    ng = len(gs)

    def fn(i, tm, rows, bcs):
        w_ = rows[0]
        g = rows[1] if ng == 1 else rows[1] + rows[2]
        m_, v_ = rows[1 + ng], rows[2 + ng]
        m_ = ADAM_B1 * m_ + (1.0 - ADAM_B1) * g
        v_ = ADAM_B2 * v_ + (1.0 - ADAM_B2) * jnp.square(g)
        m_hat = m_ / (1.0 - ADAM_B1 ** ADAM_STEP)
        v_hat = v_ / (1.0 - ADAM_B2 ** ADAM_STEP)
        delta = -ADAM_LR * (m_hat / (jnp.sqrt(v_hat) + ADAM_EPS) + ADAM_WD * w_)
        return (g, delta, m_, v_), ()
    wd = w.shape[1]
    return rowwise(name, fn, [w] + list(gs) + [m, v], [], [wd] * 4)


def s5_matrices(lam_re, lam_im, log_dt, b_re, b_im, c_re, c_im, d):
    c = S5_CHUNK
    dt = jnp.exp(log_dt)[:, None]
    tau = jnp.arange(c + 1, dtype=F32)[:, None, None]
    mag = jnp.exp(tau * (lam_re * dt)[None])
    ang = tau * (lam_im * dt)[None]
    pw_re, pw_im = mag * jnp.cos(ang), mag * jnp.sin(ang)
    nr, ni = pw_re[1] - 1.0, pw_im[1]
    den = lam_re * lam_re + lam_im * lam_im
    cf_re = (nr * lam_re + ni * lam_im) / den
    cf_im = (ni * lam_re - nr * lam_im) / den
    bb_re = cf_re[:, :, None] * b_re - cf_im[:, :, None] * b_im
    bb_im = cf_re[:, :, None] * b_im + cf_im[:, :, None] * b_re
    cp_re = c_re[None] * pw_re[:, :, None, :] - c_im[None] * pw_im[:, :, None, :]
    cp_im = c_re[None] * pw_im[:, :, None, :] + c_im[None] * pw_re[:, :, None, :]
    hp = lax.Precision.HIGHEST
    kern = (jnp.einsum('tghp,gpk->tghk', cp_re[:c], bb_re, precision=hp)
            - jnp.einsum('tghp,gpk->tghk', cp_im[:c], bb_im, precision=hp))
    ii = jnp.arange(c)
    sel = (ii[None, :, None] - ii[None, None, :] == ii[:, None, None]).astype(F32)
    a1 = jnp.einsum('tghk,tij->gjkih', kern, sel, precision=hp)
    a1 = a1 + jnp.einsum('ij,kh,gh->gjkih', jnp.eye(c, dtype=F32), jnp.eye(S5_GROUP, dtype=F32), d)
    a1 = a1.reshape(S5_GROUPS, c * S5_GROUP, c * S5_GROUP)
    a2 = jnp.concatenate([cp_re[1:], -cp_im[1:]], axis=-1)
    a2 = a2.transpose(1, 3, 0, 2).reshape(S5_GROUPS, 2 * S5_STATE, c * S5_GROUP)
    rv_re, rv_im = pw_re[:c][::-1], pw_im[:c][::-1]
    x_re = rv_re[:, :, :, None] * bb_re[None] - rv_im[:, :, :, None] * bb_im[None]
    x_im = rv_re[:, :, :, None] * bb_im[None] + rv_im[:, :, :, None] * bb_re[None]
    a3 = jnp.concatenate([x_re, x_im], axis=2)
    a3 = a3.transpose(1, 0, 3, 2).reshape(S5_GROUPS, c * S5_GROUP, 2 * S5_STATE)
    are = jnp.concatenate([pw_re[c], pw_re[c]], axis=-1)[:, None, :]
    aim = jnp.concatenate([-pw_im[c], pw_im[c]], axis=-1)[:, None, :]
    return a1, a2, a3, are, aim


S5_LEVELS = 8


def s5_scan_powers(lam_re, lam_im, log_dt):
    dt = jnp.exp(log_dt)[:, None]
    e = (S5_CHUNK * 2.0 ** jnp.arange(S5_LEVELS, dtype=F32))[:, None, None]
    mag = jnp.exp(e * (lam_re * dt)[None])
    ang = e * (lam_im * dt)[None]
    pr, pi = mag * jnp.cos(ang), mag * jnp.sin(ang)
    pre = jnp.concatenate([pr, pr], axis=-1).transpose(1, 0, 2)
    pim = jnp.concatenate([-pi, pi], axis=-1).transpose(1, 0, 2)
    return pre, pim


def _s5_scan(x, pre, pim, reverse):
    n = x.shape[0]
    rowi = lax.broadcasted_iota(jnp.int32, (n, 1), 0)
    t = x
    for k in range(S5_LEVELS):
        shift = 2 ** k
        if shift >= n:
            break
        p_re, p_im = pre[k:k + 1, :], pim[k:k + 1, :]
        if reverse:
            far = jnp.where(rowi < n - shift, pltpu.roll(t, n - shift, 0), 0.0)
            t = t + (p_re * far + pltpu.roll(p_im * far, S5_STATE, 1))
        else:
            far = jnp.where(rowi >= shift, pltpu.roll(t, shift, 0), 0.0)
            t = t + (p_re * far + p_im * pltpu.roll(far, S5_STATE, 1))
    return t


def s5_fwd(ug, a1, a2, a3, pre, pim, nseq):
    g, nc, cw = ug.shape
    per = nc // nseq
    sw = 2 * S5_STATE
    assert per <= 2 ** S5_LEVELS

    def body(u_ref, a1_ref, a2_ref, a3_ref, pre_ref, pim_ref, y_ref, s_ref):
        u = u_ref[...]
        x = _dot3(u, a3_ref[...], 1, 0)
        first = lax.broadcasted_iota(jnp.int32, (per, 1), 0) == 0
        for b in range(nseq):
            t = _s5_scan(x[b * per:(b + 1) * per], pre_ref[...], pim_ref[...], False)
            s_ref[pl.ds(b * per, per), :] = jnp.where(first, 0.0, pltpu.roll(t, 1, 0))
        y_ref[...] = _dot3(u, a1_ref[...], 1, 0) + _dot3(s_ref[...], a2_ref[...], 1, 0)

    def spec(shape):
        return pl.BlockSpec((None,) + shape, lambda i: (i, 0, 0))
    lv = (S5_LEVELS, sw)
    return pl.pallas_call(
        body, name="s5_fwd", grid=(g,),
        in_specs=[spec((nc, cw)), spec((cw, cw)), spec((sw, cw)), spec((cw, sw)), spec(lv), spec(lv)],
        out_specs=[spec((nc, cw)), spec((nc, sw))],
        out_shape=[jax.ShapeDtypeStruct((g, nc, cw), F32), jax.ShapeDtypeStruct((g, nc, sw), F32)],
        compiler_params=_params(("parallel",)))(ug, a1, a2, a3, pre, pim)


def s5_bwd(dyg, ug, sst, a1, a2, a3, pre, pim, nseq, ex=None):
    g, nc, cw = ug.shape
    per = nc // nseq
    sw = 2 * S5_STATE
    split, ex_start, ex_finish, exkw = _hidden_exchange(ex, 8, 6, (g,))

    def body(*refs):
        ((dy_ref, u_ref, s_ref, a1_ref, a2_ref, a3_ref, pre_ref, pim_ref),
         (du_ref, da1_ref, da2_ref, da3_ref, dare_ref, daim_ref), ex_parts, (dx_ref,)) = split(refs)
        ex_start(ex_parts)
        dy = dy_ref[...]
        u = u_ref[...]
        gd = _dot3(dy, a2_ref[...], 1, 1)
        s_all = s_ref[...]
        da2_ref[...] = _dot3(s_all, dy, 0, 0)
        last = lax.broadcasted_iota(jnp.int32, (per, 1), 0) == per - 1
        for b in range(nseq):
            gs = _s5_scan(gd[b * per:(b + 1) * per], pre_ref[...], pim_ref[...], True)
            dx_ref[pl.ds(b * per, per), :] = jnp.where(last, 0.0, pltpu.roll(gs, per - 1, 0))
        dx = dx_ref[...]
        dare_ref[...] = _colsum(dx * s_all)
        daim_ref[...] = _colsum(dx * pltpu.roll(s_all, S5_STATE, 1))
        du_ref[...] = _dot3(dy, a1_ref[...], 1, 1) + _dot3(dx, a3_ref[...], 1, 1)
        da1_ref[...] = _dot3(u, dy, 0, 0)
        da3_ref[...] = _dot3(u, dx, 0, 0)
        ex_finish(ex_parts)

    def spec(shape):
        return pl.BlockSpec((None,) + shape, lambda i: (i, 0, 0))
    sds = jax.ShapeDtypeStruct
    return pl.pallas_call(
        body, name="s5_bwd", grid=(g,),
        in_specs=[spec((nc, cw)), spec((nc, cw)), spec((nc, sw)), spec((cw, cw)), spec((sw, cw)), spec((cw, sw)),
                  spec((S5_LEVELS, sw)), spec((S5_LEVELS, sw))] + exkw['in_specs'],
        out_specs=[spec((nc, cw)), spec((cw, cw)), spec((sw, cw)), spec((cw, sw)), spec((1, sw)), spec((1, sw))]
        + exkw['out_specs'],
        out_shape=[sds((g, nc, cw), F32), sds((g, cw, cw), F32), sds((g, sw, cw), F32), sds((g, cw, sw), F32),
                   sds((g, 1, sw), F32), sds((g, 1, sw), F32)] + exkw['out_shape'],
        input_output_aliases=exkw['aliases'], scratch_shapes=[pltpu.VMEM((nc, sw), F32)] + exkw['scratch'],
        compiler_params=_params(("arbitrary",)))(dyg, ug, sst, a1, a2, a3, pre, pim, *exkw['ins'])


def _to_groups(u):
    t = u.shape[0]
    nc = t // S5_CHUNK
    return u.reshape(nc, S5_CHUNK, S5_GROUPS, S5_GROUP).transpose(2, 0, 1, 3).reshape(S5_GROUPS, nc, -1)


def _from_groups(yg):
    g, nc, _ = yg.shape
    return yg.reshape(g, nc, S5_CHUNK, S5_GROUP).transpose(1, 2, 0, 3).reshape(nc * S5_CHUNK, g * S5_GROUP)


def _sb_masks():
    row = lax.broadcasted_iota(jnp.int32, (SB_BLOCK, SB_BLOCK), 0)
    col = lax.broadcasted_iota(jnp.int32, (SB_BLOCK, SB_BLOCK), 1)
    lane = lax.broadcasted_iota(jnp.int32, (1, LANES), 1)
    return row, col, [lane < 64, lane >= 64]


def _sb_weights(z, vis, later_of, after):
    sp = jnp.maximum(z, 0.0) + jnp.log1p(jnp.exp(-jnp.abs(z)))
    lk = jnp.where(vis, -sp, 0.0)
    rs = jnp.sum(lk, axis=1, keepdims=True)
    later = _dot_exact_rhs(lk, after, 1, 0) + later_of(rs)
    lb = z - sp
    w = jnp.where(vis, jnp.exp(lb + later), 0.0)
    return w, rs, lb


def _hidden_exchange(ex, n_in, n_out, grid):
    ni, no = (len(ex.ins), len(ex.out_shapes)) if ex else (0, 0)

    def split(refs):
        a, b = n_in + ni, n_in + ni + n_out
        end = len(refs) - (2 if ex else 0)
        return refs[:n_in], refs[a:b], (refs[n_in:a], refs[b:b + no], refs[end:]), refs[b + no:end]

    def at_step(which, fn, parts):
        if ex is not None:
            ids = [pl.program_id(d) == (0 if which == 'first' else g - 1) for d, g in enumerate(grid)]
            cond = ids[0]
            for c in ids[1:]:
                cond = jnp.logical_and(cond, c)
            pl.when(cond)(lambda: fn(parts[0], parts[1], *parts[2]))

    anyspec = pl.BlockSpec(memory_space=pl.ANY)
    kw = dict(in_specs=[anyspec] * ni, out_specs=[anyspec] * no, out_shape=list(ex.out_shapes) if ex else [],
              aliases={n_in + i: n_out + j for i, j in (ex.aliases().items() if ex else ())},
              scratch=ex.sems() if ex else [], ins=list(ex.ins) if ex else [])
    start = functools.partial(at_step, 'first', ex.start if ex else None)
    finish = functools.partial(at_step, 'last', ex.finish if ex else None)
    return split, start, finish, kw


def _lane_tile(x, tl):
    return x[:, tl * LANES:(tl + 1) * LANES]


SB_TILES = 2
_SB_CHAINS = [(tl, hd) for tl in range(SB_TILES) for hd in range(2)]


def _sb_specs(lp, nseq):
    wd = SB_TILES * LANES
    off = [(S5_WIDTH + i * SB_WIDTH) // wd for i in range(3)]
    blk = (lp, wd)
    qkv = [pl.BlockSpec(blk, functools.partial(lambda b, h, o: (b, o + h), o=o)) for o in off]
    return (blk, qkv, pl.BlockSpec(blk, lambda b, h: (b, h)),
            pl.BlockSpec((None, None, 8, LANES), lambda b, h: (b, h, 0, 0)))


def attn_fwd(proj, nseq, lp, ex=None):
    nqb = lp // SB_BLOCK
    t = proj.shape[0]
    nstep = SB_WIDTH // (SB_TILES * LANES)
    split, ex_start, ex_finish, exkw = _hidden_exchange(ex, 3, 3, (nseq, nstep))

    def body(*refs):
        (q_ref, k_ref, v_ref), (o_ref, tot_ref, first_ref), ex_parts, _ = split(refs)
        ex_start(ex_parts)
        row, col, hmask = _sb_masks()
        after = (row > col).astype(BF16)
        tri = col < row
        lane8 = lax.broadcasted_iota(jnp.int32, (8, LANES), 1)

        def qloop(qi, firsts):
            q0 = pl.multiple_of(qi * SB_BLOCK, SB_BLOCK)
            q = q_ref[pl.ds(q0, SB_BLOCK), :] * 0.125
            qm = [jnp.where(hmask[hd], _lane_tile(q, tl), 0.0).astype(BF16) for tl, hd in _SB_CHAINS]

            def alive(carry):
                return jnp.logical_and(carry[0] <= qi, carry[1] > 0)

            def kstep(carry):
                s = carry[0]
                accs, lats = list(carry[2:2 + SB_TILES]), list(carry[2 + SB_TILES:])
                kj = qi - s
                k0 = pl.multiple_of(kj * SB_BLOCK, SB_BLOCK)
                k = k_ref[pl.ds(k0, SB_BLOCK), :].astype(BF16)
                v = v_ref[pl.ds(k0, SB_BLOCK), :]
                vis = jnp.logical_or(kj < qi, tri)
                for c, (tl, hd) in enumerate(_SB_CHAINS):
                    z = _dot(qm[c], _lane_tile(k, tl), 1, 1)
                    w, rs, _ = _sb_weights(z, vis, functools.partial(lambda rs, lat: lat, lat=lats[c]), after)
                    vm = jnp.where(hmask[hd], _lane_tile(v, tl), 0.0).astype(BF16)
                    accs[tl] = accs[tl] + _dot(w.astype(BF16), vm, 1, 0)
                    lats[c] = lats[c] + rs
                top = functools.reduce(jnp.maximum, lats)
                go = (jnp.max(top) > SB_DEAD).astype(jnp.int32)
                return (s + 1, go, *accs, *lats)

            z1 = jnp.zeros((SB_BLOCK, 1), F32)
            res = lax.while_loop(alive, kstep, (jnp.int32(0), jnp.int32(1),
                                                *[jnp.zeros((SB_BLOCK, LANES), F32)] * SB_TILES,
                                                *[z1] * len(_SB_CHAINS)))
            accs, lats = res[2:2 + SB_TILES], res[2 + SB_TILES:]
            for tl in range(SB_TILES):
                cols = slice(tl * LANES, (tl + 1) * LANES)
                o_ref[pl.ds(q0, SB_BLOCK), cols] = accs[tl].astype(o_ref.dtype)
                tot_ref[pl.ds(q0, SB_BLOCK), cols] = jnp.where(hmask[0], lats[2 * tl], lats[2 * tl + 1])
            return jnp.where(lane8 == qi, (qi - res[0] + 1).astype(F32), firsts)

        first_ref[...] = lax.fori_loop(0, nqb, qloop, jnp.zeros((8, LANES), F32))
        ex_finish(ex_parts)

    blk, qkv, out, vec = _sb_specs(lp, nseq)
    return pl.pallas_call(
        body, name="attn_fwd", grid=(nseq, nstep),
        in_specs=qkv + exkw['in_specs'], out_specs=[out, out, vec] + exkw['out_specs'],
        out_shape=[jax.ShapeDtypeStruct((t, SB_WIDTH), BF16), jax.ShapeDtypeStruct((t, SB_WIDTH), F32)]
        + [jax.ShapeDtypeStruct((nseq, nstep, 8, LANES), F32)] + exkw['out_shape'],
        input_output_aliases=exkw['aliases'], scratch_shapes=exkw['scratch'],
        compiler_params=_params(("arbitrary", "arbitrary")))(proj, proj, proj, *exkw['ins'])


def attn_bwd(proj, tot, first, do, nseq, lp, ex=None):
    nqb = lp // SB_BLOCK
    t = proj.shape[0]
    nstep = SB_WIDTH // (SB_TILES * LANES)
    split, ex_start, ex_finish, exkw = _hidden_exchange(ex, 6, 3, (nseq, nstep))

    def body(*refs):
        (q_ref, k_ref, v_ref, tot_ref, first_ref, do_ref), (dq_ref, dk_ref, dv_ref), ex_parts, _ = split(refs)
        ex_start(ex_parts)
        row, col, hmask = _sb_masks()
        after = (row > col).astype(BF16)
        before = (row < col).astype(BF16)
        tri = col < row
        lane8 = lax.broadcasted_iota(jnp.int32, (8, LANES), 1)
        firsts = first_ref[...]
        dk_ref[...] = jnp.zeros(dk_ref.shape, F32)
        dv_ref[...] = jnp.zeros(dv_ref.shape, F32)
        nc = len(_SB_CHAINS)

        def qloop(qi, _):
            first = jnp.max(jnp.where(lane8 == qi, firsts, 0.0)).astype(jnp.int32)
            first = jnp.clip(first, 0, qi)
            q0 = pl.multiple_of(qi * SB_BLOCK, SB_BLOCK)
            q = q_ref[pl.ds(q0, SB_BLOCK), :] * 0.125
            do_ = do_ref[pl.ds(q0, SB_BLOCK), :]
            tot_ = tot_ref[pl.ds(q0, SB_BLOCK), :]
            qm = [jnp.where(hmask[hd], _lane_tile(q, tl), 0.0).astype(BF16) for tl, hd in _SB_CHAINS]
            dom = [jnp.where(hmask[hd], _lane_tile(do_, tl), 0.0).astype(BF16) for tl, hd in _SB_CHAINS]
            tot = [jnp.max(jnp.where(hmask[hd], _lane_tile(tot_, tl), -jnp.inf), axis=1, keepdims=True)
                   for tl, hd in _SB_CHAINS]

            def kloop(kj, carry):
                dqs = list(carry[:SB_TILES])
                pre = list(carry[SB_TILES:SB_TILES + nc])
                gpre = list(carry[SB_TILES + nc:])
                k0 = pl.multiple_of(kj * SB_BLOCK, SB_BLOCK)
                kf = k_ref[pl.ds(k0, SB_BLOCK), :]
                k = kf.astype(BF16)
                v = v_ref[pl.ds(k0, SB_BLOCK), :].astype(BF16)
                vis = jnp.logical_or(kj < qi, tri)
                dk_acc = [jnp.zeros((SB_BLOCK, LANES), F32)] * SB_TILES
                dv_acc = [jnp.zeros((SB_BLOCK, LANES), F32)] * SB_TILES
                for c, (tl, hd) in enumerate(_SB_CHAINS):
                    z = _dot(qm[c], _lane_tile(k, tl), 1, 1)
                    later_of = functools.partial(lambda rs, t_, p_: t_ - p_ - rs, t_=tot[c], p_=pre[c])
                    w, rs, lb = _sb_weights(z, vis, later_of, after)
                    dw = _dot(dom[c], _lane_tile(v, tl), 1, 1)
                    g = dw * w
                    dlk = _dot_exact_rhs(g, before, 1, 0) + gpre[c]
                    sig = jnp.exp(lb)
                    dz = jnp.where(vis, g * (1.0 - sig) - dlk * sig, 0.0).astype(BF16)
                    km = jnp.where(hmask[hd], _lane_tile(kf, tl), 0.0).astype(BF16)
                    dqs[tl] = dqs[tl] + _dot(dz, km, 1, 0)
                    dk_acc[tl] = dk_acc[tl] + _dot(dz, qm[c], 0, 0)
                    dv_acc[tl] = dv_acc[tl] + _dot(w.astype(BF16), dom[c], 0, 0)
                    pre[c] = pre[c] + rs
                    gpre[c] = gpre[c] + jnp.sum(g, axis=1, keepdims=True)
                for tl in range(SB_TILES):
                    cols = slice(tl * LANES, (tl + 1) * LANES)
                    dk_ref[pl.ds(k0, SB_BLOCK), cols] += dk_acc[tl]
                    dv_ref[pl.ds(k0, SB_BLOCK), cols] += dv_acc[tl]
                return (*dqs, *pre, *gpre)

            z1 = jnp.zeros((SB_BLOCK, 1), F32)
            res = lax.fori_loop(first, qi + 1, kloop,
                                (*[jnp.zeros((SB_BLOCK, LANES), F32)] * SB_TILES, *[z1] * (2 * nc)))
            for tl in range(SB_TILES):
                dq_ref[pl.ds(q0, SB_BLOCK), tl * LANES:(tl + 1) * LANES] = res[tl] * 0.125
            return 0

        lax.fori_loop(0, nqb, qloop, 0)
        ex_finish(ex_parts)

    blk, qkv, out, vec = _sb_specs(lp, nseq)
    return pl.pallas_call(
        body, name="attn_bwd", grid=(nseq, nstep),
        in_specs=qkv + [out, vec, out] + exkw['in_specs'], out_specs=[out] * 3 + exkw['out_specs'],
        out_shape=[jax.ShapeDtypeStruct((t, SB_WIDTH), F32)] * 3 + exkw['out_shape'],
        input_output_aliases=exkw['aliases'], scratch_shapes=exkw['scratch'],
        compiler_params=_params(("arbitrary", "arbitrary")))(proj, proj, proj, tot, first, do, *exkw['ins'])


def _hg_loop(nch, step, init):
    assert nch % HG_UNROLL == 0

    def trip(i, carry):
        for u in range(HG_UNROLL):
            carry = step(i * HG_UNROLL + u, carry)
        return carry
    return lax.fori_loop(0, nch // HG_UNROLL, trip, init)


def _hg_gates(fc, lb):
    sg = jax.nn.sigmoid(fc)
    f = lb + (1.0 - lb) * sg
    return sg, f


def _hg_decay(f, incl):
    bcum = _dot_exact_lhs(incl, jnp.log(f))
    return bcum, bcum[HG_CHUNK - 1:HG_CHUNK, :]


def _dot_exact_lhs(m, x):
    hi, mid, lo = _split3(x)
    return _dot(m, hi, 1, 0) + (_dot(m, mid, 1, 0) + _dot(m, lo, 1, 0))


def hgrn_fwd(proj, lb, ng, nseq, lp):
    nch = lp // HG_CHUNK
    t = proj.shape[0]
    c = HG_CHUNK

    def body(q_ref, f_ref, v_ref, g_ref, lb_ref, ng_ref, o_ref, og_ref, st_ref):
        row = lax.broadcasted_iota(jnp.int32, (c, c), 0)
        col = lax.broadcasted_iota(jnp.int32, (c, c), 1)
        causal = col <= row
        incl = causal.astype(BF16)
        lbv, ngv = lb_ref[...], ng_ref[...]

        def chunk(b, ci, st):
            rows = pl.ds(pl.multiple_of(b * lp + ci * c, c), c)
            st_ref[b, ci] = st
            _, f = _hg_gates(f_ref[rows, :], lbv)
            bcum, blast = _hg_decay(f, incl)
            kk = 1.0 - f
            vc = v_ref[rows, :].astype(BF16)
            qd = (q_ref[rows, :] * jnp.exp(bcum)).astype(BF16)
            kd = (kk * jnp.exp(-bcum)).astype(BF16)
            a = jnp.where(causal, _dot(qd, kd, 1, 1), 0.0)
            o = _dot(a.astype(BF16), vc, 1, 0) + _dot(qd, st.astype(BF16), 1, 1)
            kend = (kk * jnp.exp(blast - bcum)).astype(BF16)
            st = st * jnp.exp(blast) + _dot(vc, kend, 0, 0)
            o_ref[rows, :] = o
            r = lax.rsqrt(jnp.mean(o * o, axis=1, keepdims=True) + RMS_EPS)
            gc = g_ref[rows, :]
            og_ref[rows, :] = (o * r * ngv * (gc * jax.nn.sigmoid(gc))).astype(og_ref.dtype)
            return st

        def step(ci, sts):
            return tuple(chunk(b, ci, sts[b]) for b in range(nseq))

        _hg_loop(nch, step, tuple(jnp.zeros((HG_DK, HG_DK), F32) for _ in range(nseq)))

    blk = (nseq * lp, LANES)
    h = HG_HEADS
    return pl.pallas_call(
        body, name="hgrn_fwd", grid=(h,),
        in_specs=[pl.BlockSpec(blk, lambda hh: (0, hh)),
                  pl.BlockSpec(blk, lambda hh: (0, h + hh)),
                  pl.BlockSpec(blk, lambda hh: (0, 2 * h + hh)),
                  pl.BlockSpec(blk, lambda hh: (0, 3 * h + hh)),
                  pl.BlockSpec((1, LANES), lambda hh: (0, hh)),
                  pl.BlockSpec((1, LANES), lambda hh: (0, hh))],
        out_specs=[pl.BlockSpec(blk, lambda hh: (0, hh)),
                   pl.BlockSpec(blk, lambda hh: (0, hh)),
                   pl.BlockSpec((nseq, None, nch, HG_DK, HG_DK), lambda hh: (0, hh, 0, 0, 0))],
        out_shape=[jax.ShapeDtypeStruct((t, D_MODEL), F32), jax.ShapeDtypeStruct((t, D_MODEL), BF16),
                   jax.ShapeDtypeStruct((nseq, h, nch, HG_DK, HG_DK), F32)],
        compiler_params=_params(("parallel",)))(proj, proj, proj, proj, lb, ng)


def hgrn_bwd(proj, o, dog, states, lb, ng, nseq, lp):
    nch = lp // HG_CHUNK
    t = proj.shape[0]
    c = HG_CHUNK

    def body(q_ref, f_ref, v_ref, g_ref, o_ref, dog_ref, st_ref, lb_ref, ng_ref,
             dq_ref, df_ref, dv_ref, dg_ref, dlb_ref, dng_ref):
        row = lax.broadcasted_iota(jnp.int32, (c, c), 0)
        col = lax.broadcasted_iota(jnp.int32, (c, c), 1)
        causal = col <= row
        incl = causal.astype(BF16)
        from_here = (col >= row).astype(BF16)
        last = lax.broadcasted_iota(jnp.int32, (c, 1), 0) == c - 1
        lbv, ngv = lb_ref[...], ng_ref[...]

        def step(s, carry):
            dst, dlb, dng = carry
            ci = nch - 1 - s
            r0 = pl.multiple_of(ci * c, c)
            rows = pl.ds(r0, c)
            oc, gc, dogc = o_ref[rows, :], g_ref[rows, :], dog_ref[rows, :]
            r = lax.rsqrt(jnp.mean(oc * oc, axis=1, keepdims=True) + RMS_EPS)
            sgg = jax.nn.sigmoid(gc)
            silu = gc * sgg
            dg_ref[rows, :] = dogc * (oc * r * ngv) * (sgg * (1.0 + gc * (1.0 - sgg)))
            don = dogc * silu
            dng = dng + _colsum(don * oc * r)
            tt = don * ngv
            do = r * (tt - oc * (r * r) * jnp.mean(tt * oc, axis=1, keepdims=True))
            st = st_ref[ci]
            fc = f_ref[rows, :]
            sg, f = _hg_gates(fc, lbv)
            bcum, blast = _hg_decay(f, incl)
            kk = 1.0 - f
            qc, vf = q_ref[rows, :], v_ref[rows, :]
            pdec = jnp.exp(bcum)
            ndec = jnp.exp(-bcum)
            edec = jnp.exp(blast - bcum)
            eb = jnp.exp(blast)
            qd_f, kd_f, kend_f = qc * pdec, kk * ndec, kk * edec
            qd, kd, kend = qd_f.astype(BF16), kd_f.astype(BF16), kend_f.astype(BF16)
            vc, dob, stb, dstb = vf.astype(BF16), do.astype(BF16), st.astype(BF16), dst.astype(BF16)
            a = jnp.where(causal, _dot(qd, kd, 1, 1), 0.0).astype(BF16)
            da = jnp.where(causal, _dot(dob, vc, 1, 1), 0.0).astype(BF16)
            dv_ref[rows, :] = _dot(a, dob, 0, 0) + _dot(kend, dstb, 1, 1)
            dqd = _dot(da, kd, 1, 0) + _dot(dob, stb, 1, 0)
            dkd = _dot(da, qd, 0, 0)
            dkend = _dot(vc, dstb, 1, 0)
            dq_ref[rows, :] = dqd * pdec
            dblast = _colsum(dkend * kend_f) + eb * _colsum(dst * st)
            dbcum = dqd * qd_f - dkd * kd_f - dkend * kend_f
            dbcum = dbcum + jnp.where(last, dblast, 0.0)
            dlf = _dot_exact_lhs(from_here, dbcum)
            df = dlf / f - (dkd * ndec + dkend * edec)
            df_ref[rows, :] = df * (1.0 - lbv) * sg * (1.0 - sg)
            dlb = dlb + _colsum(df * (1.0 - sg))
            dst = dst * eb + _dot(dob, qd, 0, 0)
            return dst, dlb, dng

        z = jnp.zeros((1, LANES), F32)
        _, dlb, dng = _hg_loop(nch, step, (jnp.zeros((HG_DK, HG_DK), F32), z, z))
        dlb_ref[...] = dlb
        dng_ref[...] = dng

    blk = (lp, LANES)
    h = HG_HEADS
    vec = pl.BlockSpec((1, LANES), lambda b, hh: (0, hh))
    svec = pl.BlockSpec((None, 1, LANES), lambda b, hh: (b, 0, hh))
    return pl.pallas_call(
        body, name="hgrn_bwd", grid=(nseq, h),
        in_specs=[pl.BlockSpec(blk, lambda b, hh: (b, hh)),
                  pl.BlockSpec(blk, lambda b, hh: (b, h + hh)),
                  pl.BlockSpec(blk, lambda b, hh: (b, 2 * h + hh)),
                  pl.BlockSpec(blk, lambda b, hh: (b, 3 * h + hh)),
                  pl.BlockSpec(blk, lambda b, hh: (b, hh)),
                  pl.BlockSpec(blk, lambda b, hh: (b, hh)),
                  pl.BlockSpec((None, None, nch, HG_DK, HG_DK), lambda b, hh: (b, hh, 0, 0, 0)),
                  vec, vec],
        out_specs=[pl.BlockSpec(blk, lambda b, hh: (b, hh))] * 4 + [svec, svec],
        out_shape=[jax.ShapeDtypeStruct((t, D_MODEL), F32)] * 4
        + [jax.ShapeDtypeStruct((nseq, 1, D_MODEL), F32)] * 2,
        compiler_params=_params(("parallel", "parallel")))(proj, proj, proj, proj, o, dog, states, lb, ng)


def _lower_bound(gamma):
    p = jax.nn.softmax(gamma, axis=0)
    return (jnp.cumsum(p, axis=0) - p[0])[1][None, :]


def local_step(x, tgt, w, hooks=None):
    nseq, seq, _ = x.shape
    lp = -(-(N_META + seq) // SB_BLOCK) * SB_BLOCK
    t = nseq * lp
    pad = lp - N_META - seq
    h0 = jnp.concatenate([jnp.broadcast_to(w['meta'][None], (nseq, N_META, D_MODEL)), x,
                          jnp.zeros((nseq, pad, D_MODEL), F32)], axis=1).reshape(t, D_MODEL)
    tgt_p = jnp.pad(tgt, ((0, 0), (N_META, pad), (0, 0))).reshape(t, D_MODEL)
    row = lambda v: v.reshape(1, -1)
    g = {}

    proj = matmul("in_ab", h0, w['w_in_ab'], 'nn')
    att = attn_fwd(proj, nseq, lp, ex=hooks.gather_late if hooks else None)
    b_out, sb_tot, sb_first = att[:3]
    if hooks:
        w = {**w, **hooks.late_weights(att[3:])}
    s5_names = ['s5_lam_re', 's5_lam_im', 's5_log_dt', 's5_b_re', 's5_b_im', 's5_c_re', 's5_c_im', 's5_d']
    mats, mats_vjp = jax.vjp(s5_matrices, *[w[n][0] for n in s5_names])
    ug = _to_groups(proj[:, :S5_WIDTH])
    s5_pows = s5_scan_powers(*[w[n][0] for n in s5_names[:3]])
    yg, sst = s5_fwd(ug, *mats[:3], *s5_pows, nseq)
    ys5 = _from_groups(yg)
    y = gelu_fwd("gelu", ys5)
    gp = matmul("glu", y, w['s5_w_glu'], 'nn', bias=w['s5_b_glu'])
    a_out = glu_gate("glu_gate", y, gp)
    cat = jnp.concatenate([a_out, b_out], axis=1)
    mix0 = matmul("out_ab", cat, w['w_out_ab'], 'nn')
    h1, xh1, rs1 = ln_fwd("ln_mix0", h0, mix0, row(w['ln_mix_g'][0]), row(w['ln_mix_b'][0]))
    hid0 = matmul("up0", h1, w['mlp_w_up'][0], 'nn', bias=row(w['mlp_b_up'][0]), act='relu2', out_dtype=BF16)
    dn0 = matmul("down0", hid0, w['mlp_w_down'][0], 'nn', bias=row(w['mlp_b_down'][0]))
    h2, xh2, rs2 = ln_fwd("ln_mlp0", h1, dn0, row(w['ln_mlp_g'][0]), row(w['ln_mlp_b'][0]))

    projc = matmul("in_c", h2, w['w_in_c'], 'nn')
    lb, lb_vjp = jax.vjp(_lower_bound, w['hgrn_gamma'])
    ng = w['hgrn_norm_g']
    o, og, states = hgrn_fwd(projc, lb, ng, nseq, lp)
    mix1 = matmul("out_c", og, w['w_out_c'], 'nn')
    h3, xh3, rs3 = ln_fwd("ln_mix1", h2, mix1, row(w['ln_mix_g'][1]), row(w['ln_mix_b'][1]))
    hid1 = matmul("up1", h3, w['mlp_w_up'][1], 'nn', bias=row(w['mlp_b_up'][1]), act='relu2', out_dtype=BF16)
    dn1 = matmul("down1", hid1, w['mlp_w_down'][1], 'nn', bias=row(w['mlp_b_down'][1]))
    h4, xh4, rs4 = ln_fwd("ln_mlp1", h3, dn1, row(w['ln_mlp_g'][1]), row(w['ln_mlp_b'][1]))
    dy, loss = loss_head("loss", h4, tgt_p, lp, seq)

    def mlp_bwd(l, dh_out, xh_out, rs_out, hid, h_in):
        dpre, dg_, db_, dbd = ln_bwd(f"ln_mlp{l}_b", dh_out, xh_out, rs_out, row(w['ln_mlp_g'][l]))
        dpu = matmul(f"down{l}_dx", dpre, w['mlp_w_down'][l], 'nt', relu2_of=hid, out_dtype=BF16)
        dwd = matmul(f"down{l}_dw", hid, dpre, 'tn')
        dh_in = matmul(f"up{l}_dx", dpu, w['mlp_w_up'][l], 'nt', add=dpre, add_scale=ALPHA)
        dwu, dbu = matmul(f"up{l}_dw", h_in, dpu, 'tn', out_slots=N_CHIPS, colsum=True)
        return dh_in, dict(ln_mlp_g=dg_, ln_mlp_b=db_, mlp_b_down=dbd, mlp_b_up=dbu, mlp_w_up=dwu, mlp_w_down=dwd)

    dh3, gm1 = mlp_bwd(1, dy, xh4, rs4, hid1, h3)
    dpre, dg1, db1, _ = ln_bwd("ln_mix1_b", dh3, xh3, rs3, row(w['ln_mix_g'][1]))
    g['w_out_c'] = matmul("out_c_dw", og, dpre, 'tn')
    dog = matmul("out_c_dx", dpre, w['w_out_c'], 'nt')
    dq, df, di, dgg, dlb, dng = hgrn_bwd(projc, o, dog, states, lb, ng, nseq, lp)
    dprojc = jnp.concatenate([dq, df, di, dgg], axis=1)
    dh2 = matmul("in_c_dx", dprojc, w['w_in_c'], 'nt', add=dpre, add_scale=ALPHA)
    g['w_in_c'] = matmul("in_c_dw", h2, dprojc, 'tn', out_slots=N_CHIPS)
    g['hgrn_gamma'] = lb_vjp(jnp.sum(dlb, axis=0))[0]
    g['hgrn_norm_g'] = jnp.sum(dng, axis=0)

    dh1, gm0 = mlp_bwd(0, dh2, xh2, rs2, hid0, h1)
    dpre, dg0, db0, _ = ln_bwd("ln_mix0_b", dh1, xh1, rs1, row(w['ln_mix_g'][0]))
    g['w_out_ab'] = matmul("out_ab_dw", cat, dpre, 'tn')
    dcat = matmul("out_ab_dx", dpre, w['w_out_ab'], 'nt')
    dgp, da_s, dbglu = glu_gate_bwd("glu_gate_b", dcat[:, :S5_WIDTH], y, gp)
    g['s5_w_glu'] = matmul("glu_dw", y, dgp, 'tn')
    g['s5_b_glu'] = dbglu
    dy5 = matmul("glu_dx", dgp, w['s5_w_glu'], 'nt', add=da_s)
    dys5 = gelu_bwd("gelu_b", dy5, ys5)
    for n in ('mlp_w_up', 'mlp_w_down'):
        g[n] = [gm0[n], gm1[n]]
    g['ln_mix_g'] = jnp.concatenate([dg0, dg1], axis=0)
    g['ln_mix_b'] = jnp.concatenate([db0, db1], axis=0)
    for n in ('ln_mlp_g', 'ln_mlp_b', 'mlp_b_down', 'mlp_b_up'):
        g[n] = jnp.concatenate([gm0[n], gm1[n]], axis=0)
    res = s5_bwd(_to_groups(dys5), ug, sst, *mats[:3], *s5_pows, nseq, ex=hooks.fold_early(g) if hooks else None)
    dug, da1, da2, da3, dare, daim = res[:6]
    for n, v in zip(s5_names, mats_vjp((da1, da2, da3, dare, daim))):
        g[n] = v[None]
    ex = hooks.scatter_early(g, loss, res[6:]) if hooks else None
    res = attn_bwd(proj, sb_tot, sb_first, dcat[:, S5_WIDTH:], nseq, lp, ex=ex)
    dqa, dka, dva = res[:3]
    if hooks:
        hooks.scattered(res[3:])
    dproj = jnp.concatenate([_from_groups(dug), dqa, dka, dva], axis=1)
    dh0 = matmul("in_ab_dx", dproj, w['w_in_ab'], 'nt', add=dpre, add_scale=ALPHA)
    g['w_in_ab'] = matmul("in_ab_dw", h0, dproj, 'tn', out_slots=N_CHIPS)

    dh0 = dh0.reshape(nseq, lp, D_MODEL)
    grad_x = dh0[:, N_META:N_META + seq]
    g['meta'] = jnp.sum(dh0[:, :N_META], axis=0)
    return loss, grad_x, g


class Exchange:
    def __init__(self, ins, out_shapes, n_remote, build, in_place):
        self.ins, self.out_shapes, self.n_remote, self.build, self.in_place = ins, out_shapes, n_remote, build, in_place

    def sems(self):
        return [pltpu.SemaphoreType.DMA((self.n_remote,)), pltpu.SemaphoreType.DMA((self.n_remote,))]

    def aliases(self):
        return self.in_place if isinstance(self.in_place, dict) else \
            {i: i for i in range(len(self.ins) if self.in_place else 0)}

    def beside(self, other):
        na, nao = len(self.ins), len(self.out_shapes)

        def build(in_refs, out_refs, me):
            pa = self.build(in_refs[:na], out_refs[:nao], me)
            pb = other.build(in_refs[na:], out_refs[nao:], me)
            n = max(len(pa), len(pb))
            return [(pa[i] if i < len(pa) else []) + (pb[i] if i < len(pb) else []) for i in range(n)]
        al = dict(self.aliases())
        al.update({na + i: nao + j for i, j in other.aliases().items()})
        return Exchange(list(self.ins) + list(other.ins), list(self.out_shapes) + list(other.out_shapes),
                        self.n_remote + other.n_remote, build, al)

    def phases(self, in_refs, out_refs, ssem, rsem):
        me = (lax.axis_index("x"), lax.axis_index("y"), lax.axis_index("c"))
        out, k = [], 0
        for phase in self.build(in_refs, out_refs, me):
            out.append(functools.partial(
                lambda phase, k: [pltpu.make_async_remote_copy(src_ref=s, dst_ref=d, send_sem=ssem.at[k + i],
                                                               recv_sem=rsem.at[k + i], device_id=p,
                                                               device_id_type=MESH)
                                  for i, (s, d, p) in enumerate(phase)], phase, k))
            k += len(phase)
        return out

    def start(self, in_refs, out_refs, ssem, rsem):
        for cp in self.phases(in_refs, out_refs, ssem, rsem)[0]():
            cp.start()

    def finish(self, in_refs, out_refs, ssem, rsem):
        phases = self.phases(in_refs, out_refs, ssem, rsem)
        for cp in phases[0]():
            cp.wait()
        for make in phases[1:]:
            copies = make()
            for cp in copies:
                cp.start()
            for cp in copies:
                cp.wait()


def _exchange(name, ex):
    ni, no = len(ex.ins), len(ex.out_shapes)

    def body(*refs):
        in_refs, out_refs, sems = refs[:ni], refs[ni:ni + no], refs[ni + no:]
        ex.start(in_refs, out_refs, *sems)
        ex.finish(in_refs, out_refs, *sems)

    anyspec = pl.BlockSpec(memory_space=pl.ANY)
    return pl.pallas_call(
        body, name=name, in_specs=[anyspec] * ni, out_specs=[anyspec] * no, out_shape=ex.out_shapes,
        input_output_aliases=ex.aliases(), scratch_shapes=ex.sems())(*ex.ins)


def _chip_peers(me):
    x, y, c = me
    return [(x ^ (m >> 1), y ^ (m & 1), c) for m in (1, 2, 3)]


def _half(ref, c, axis):
    h = ref.shape[axis] // 2
    idx = [slice(None)] * axis + [pl.ds(c * h, h)]
    return ref.at[tuple(idx)]


def _like(arrs):
    return [jax.ShapeDtypeStruct(a.shape, a.dtype) for a in arrs]


def gather_over_chips(bufs):
    def build(in_refs, out_refs, me):
        x, y, c = me
        j = 2 * x + y
        sib = (x, y, 1 - c)
        ici = [(_half(o.at[j], c, 0), _half(o.at[j], c, 0), p) for o in out_refs for p in _chip_peers(me)]
        d2d = [(_half(o.at[2 * p[0] + p[1]], c, 0), _half(o.at[2 * p[0] + p[1]], c, 0), sib)
               for o in out_refs for p in _chip_peers(me)]
        return [ici, d2d]
    return Exchange(bufs, _like(bufs), 6 * len(bufs), build, True)


def fold_cores(fulls):
    def build(in_refs, out_refs, me):
        x, y, c = me
        return [[(_half(s, 1 - c, 1), o, (x, y, 1 - c)) for s, o in zip(in_refs, out_refs)]]
    shapes = [jax.ShapeDtypeStruct((f.shape[0], f.shape[1] // 2, f.shape[2]), f.dtype) for f in fulls]
    return Exchange(fulls, shapes, len(fulls), build, False)


def scatter_over_chips(parts):
    def build(in_refs, out_refs, me):
        j = 2 * me[0] + me[1]
        return [[(s.at[2 * p[0] + p[1]], o.at[j], p) for s, o in zip(in_refs, out_refs) for p in _chip_peers(me)]]
    return Exchange(parts, _like(parts), 3 * len(parts), build, False)


def join_cores(bufs):
    def build(in_refs, out_refs, me):
        x, y, c = me
        return [[(o.at[c], o.at[c], (x, y, 1 - c)) for o in out_refs]]
    return Exchange(bufs, _like(bufs), len(bufs), build, True)


def gather_over_devices(buf):
    def build(in_refs, out_refs, me):
        x, y, c = me
        i = 4 * x + 2 * y + c
        out = out_refs[0]
        return [[(out.at[i], out.at[i], (x ^ (m >> 2), y ^ ((m >> 1) & 1), c ^ (m & 1))) for m in range(1, N_DEV)]]
    return Exchange([buf], _like([buf]), N_DEV - 1, build, True)


def _slot_call(name, body, scalars, ins, in_maps, out_shape, out_map, blk, grid):
    gs = pltpu.PrefetchScalarGridSpec(
        num_scalar_prefetch=1, grid=grid,
        in_specs=[pl.BlockSpec((None,) + blk, m) for m in in_maps],
        out_specs=pl.BlockSpec((None,) + blk, out_map))
    return pl.pallas_call(body, name=name, grid_spec=gs, out_shape=out_shape,
                          compiler_params=_params(("arbitrary",) * len(grid)))(scalars, *ins)


def cast_into_slot(name, w2d, chip, dtype):
    r, wd = w2d.shape
    tm = _pick(r, (512, 256, 128, 64, 32, 16))

    def body(s_ref, w_ref, o_ref):
        o_ref[...] = w_ref[...].astype(o_ref.dtype)

    gs = pltpu.PrefetchScalarGridSpec(
        num_scalar_prefetch=1, grid=(r // tm,),
        in_specs=[pl.BlockSpec((tm, wd), lambda i, s: (i, 0))],
        out_specs=pl.BlockSpec((None, tm, wd), lambda i, s: (s[0], i, 0)))
    return pl.pallas_call(body, name=name, grid_spec=gs,
                          out_shape=jax.ShapeDtypeStruct((N_CHIPS, r, wd), dtype),
                          compiler_params=_params(("arbitrary",)))(chip.reshape(1), w2d)


def sum_cores(name, full, theirs, core):
    s, r, wd = full.shape
    h = r // 2
    tm = _pick(h, (512, 256, 128, 64, 32, 16))
    nt = h // tm

    def body(s_ref, a_ref, b_ref, o_ref):
        o_ref[...] = (a_ref[...] + b_ref[...]).astype(o_ref.dtype)

    return _slot_call(name, body, core.reshape(1), [full, theirs],
                      [lambda k, i, s_: (k, s_[0] * nt + i, 0), lambda k, i, s_: (k, i, 0)],
                      jax.ShapeDtypeStruct((s, h, wd), BF16), lambda k, i, s_: (k, i, 0), (tm, wd), (s, nt))


def sum_chips(name, own, recv, chip, core):
    s, r, wd = own.shape
    tm = _pick(r, (512, 256, 128, 64, 32, 16))

    def body(s_ref, a_ref, b1_ref, b2_ref, b3_ref, o_ref):
        acc = a_ref[...].astype(F32)
        for ref in (b1_ref, b2_ref, b3_ref):
            acc = acc + ref[...].astype(F32)
        o_ref[...] = acc

    maps = [lambda i, s_: (s_[0], i, 0)]
    maps += [functools.partial(lambda i, s_, m: (s_[0] ^ m, i, 0), m=m) for m in (1, 2, 3)]
    return _slot_call(name, body, jnp.stack([chip, core]), [own, recv, recv, recv], maps,
                      jax.ShapeDtypeStruct((2, r, wd), F32), lambda i, s_: (s_[1], i, 0), (tm, wd), (r // tm,))


def sum_slots(name, arr):
    s, r, wd = arr.shape
    tm = _pick(r, (512, 256, 128, 64, 32, 16, 8))

    def body(*refs):
        acc = refs[0][...].astype(F32)
        for ref in refs[1:s]:
            acc = acc + ref[...].astype(F32)
        refs[s][...] = acc

    specs = [pl.BlockSpec((None, tm, wd), functools.partial(lambda i, k: (k, i, 0), k=k)) for k in range(s)]
    return pl.pallas_call(body, name=name, grid=(r // tm,), in_specs=specs,
                          out_specs=pl.BlockSpec((tm, wd), lambda i: (i, 0)),
                          out_shape=jax.ShapeDtypeStruct((r, wd), F32),
                          compiler_params=_params(("parallel",)))(*([arr] * s))


def _pack(arrs):
    flat = jnp.concatenate([a.reshape(-1) for a in arrs])
    n = flat.shape[0]
    padded = -(-n // (512 * LANES)) * (512 * LANES)
    return jnp.pad(flat, (0, padded - n)).reshape(-1, LANES)


def _unpack(packed, shapes):
    flat = packed.reshape(-1)
    out, pos = [], 0
    for s in shapes:
        n = math.prod(s)
        out.append(flat[pos:pos + n].reshape(s))
        pos += n
    return out


def _as2d(a):
    return a.reshape(-1, a.shape[-1])


def kernel(x, meta, w_in_ab, s5_lam_re, s5_lam_im, s5_log_dt, s5_b_re, s5_b_im, s5_c_re, s5_c_im, s5_d, s5_w_glu, s5_b_glu, w_out_ab, w_in_c, hgrn_gamma, hgrn_norm_g, w_out_c, ln_mix_g, ln_mix_b, mlp_w_up, mlp_b_up, mlp_w_down, mlp_b_down, ln_mlp_g, ln_mlp_b, loss_target, m_meta, m_w_in_ab, m_s5_lam_re, m_s5_lam_im, m_s5_log_dt, m_s5_b_re, m_s5_b_im, m_s5_c_re, m_s5_c_im, m_s5_d, m_s5_w_glu, m_s5_b_glu, m_w_out_ab, m_w_in_c, m_hgrn_gamma, m_hgrn_norm_g, m_w_out_c, m_ln_mix_g, m_ln_mix_b, m_mlp_w_up, m_mlp_b_up, m_mlp_w_down, m_mlp_b_down, m_ln_mlp_g, m_ln_mlp_b, v_meta, v_w_in_ab, v_s5_lam_re, v_s5_lam_im, v_s5_log_dt, v_s5_b_re, v_s5_b_im, v_s5_c_re, v_s5_c_im, v_s5_d, v_s5_w_glu, v_s5_b_glu, v_w_out_ab, v_w_in_c, v_hgrn_gamma, v_hgrn_norm_g, v_w_out_c, v_ln_mix_g, v_ln_mix_b, v_mlp_w_up, v_mlp_b_up, v_mlp_w_down, v_mlp_b_down, v_ln_mlp_g, v_ln_mlp_b):
    given = dict(locals())
    wts = {n: given[n] for n in WEIGHTS}
    ms = {n: given['m_' + n] for n in WEIGHTS}
    vs = {n: given['v_' + n] for n in WEIGHTS}
    chip = 2 * lax.axis_index("x") + lax.axis_index("y")

    core = lax.axis_index("c")
    parts = [(n, l) for n in BIG for l in (range(DEPTH) if n.startswith('mlp_') else [0])]
    tags = [f"{n}{l}" for n, l in parts]
    shards = {p: cast_into_slot("cast_" + t, wts[p[0]][p[1]], chip, BF16) for t, p in zip(tags, parts)}
    small_sh = jnp.concatenate([meta, hgrn_norm_g, jnp.zeros((15, meta.shape[1]), F32)], axis=0)
    first, late = parts[:1], parts[1:]
    w_in_ab_all, sm = _exchange("gather_first", gather_over_chips(
        [shards[p] for p in first] + [cast_into_slot("cast_small", small_sh, chip, F32)]))
    w = {n: wts[n] for n in SMALL}
    w['meta'] = sm[:, :N_META].transpose(1, 0, 2).reshape(N_META, D_MODEL)
    w['hgrn_norm_g'] = sm[:, N_META:N_META + 1].transpose(1, 0, 2).reshape(1, D_MODEL)
    w['w_in_ab'] = w_in_ab_all

    def slot_major(v):
        return v if v.ndim == 3 else v.reshape(N_CHIPS, -1, v.shape[-1])

    def whole_grads(ps, g):
        return [slot_major(g[n][l] if n.startswith('mlp_') else g[n]) for n, l in ps]

    def chip_sums_of(ps, gfull, theirs):
        return [sum_cores(f"sum_cores_{n}{l}", a, b, core) for (n, l), a, b in zip(ps, gfull, theirs)]

    def all_devices(arrs):
        packed = _pack(arrs)
        return gather_over_devices(lax.dynamic_update_slice(
            jnp.zeros((N_DEV,) + packed.shape, F32), packed[None], (2 * chip + core, 0, 0)))

    early_small = [n for n in SMALL if n != 'meta']

    class Hooks:
        gather_late = gather_over_chips([shards[p] for p in late])

        @staticmethod
        def late_weights(filled):
            fw = dict(zip(late, filled))
            return {'s5_w_glu': fw['s5_w_glu', 0].reshape(S5_WIDTH, S5_WIDTH),
                    'w_out_ab': fw['w_out_ab', 0].reshape(D_MODEL, D_MODEL),
                    'w_in_c': fw['w_in_c', 0],
                    'w_out_c': fw['w_out_c', 0].reshape(D_MODEL, D_MODEL),
                    'mlp_w_up': [fw['mlp_w_up', l] for l in range(DEPTH)],
                    'mlp_w_down': [fw['mlp_w_down', l].reshape(D_FF, D_MODEL) for l in range(DEPTH)]}

        @staticmethod
        def fold_early(g):
            Hooks.whole = whole_grads(late, g)
            return fold_cores(Hooks.whole)

        @staticmethod
        def scatter_early(g, loss, theirs):
            Hooks.sums = chip_sums_of(late, Hooks.whole, theirs)
            Hooks.small_shapes = [(LANES,)] + [g[n].shape for n in early_small]
            return scatter_over_chips(Hooks.sums).beside(
                all_devices([loss.reshape(-1)] + [g[n] for n in early_small]))

        @staticmethod
        def scattered(outs):
            Hooks.recv, Hooks.small = list(outs[:-1]), outs[-1]

    loss, grad_x, g = local_step(x, loss_target, w, Hooks)

    whole = whole_grads(first, g)
    sums = chip_sums_of(first, whole, _exchange("fold_last", fold_cores(whole)))
    *recv, meta_slots = _exchange("scatter_last", scatter_over_chips(sums).beside(all_devices([g['meta']])))
    reduced = _exchange("join_grads", join_cores(
        [sum_chips(f"sum_chips_{n}{l}", a, b, chip, core)
         for (n, l), a, b in zip(first + late, sums + Hooks.sums, list(recv) + Hooks.recv)]))
    reduced = dict(zip(first + late, [_as2d(r) for r in reduced]))
    red = _unpack(sum_slots("sum_small", Hooks.small), Hooks.small_shapes)
    loss_out = red[0][0]
    gs = dict(zip(early_small, red[1:]))
    gs['meta'] = _unpack(sum_slots("sum_meta", meta_slots), [g['meta'].shape])[0]
    gs['meta'] = lax.dynamic_slice_in_dim(gs['meta'], chip * meta.shape[1], meta.shape[1], axis=1)
    gs['hgrn_norm_g'] = lax.dynamic_slice_in_dim(gs['hgrn_norm_g'].reshape(1, -1), chip * meta.shape[1],
                                                 meta.shape[1], axis=1)

    grads, deltas, new_m, new_v = {}, {}, {}, {}
    for n in BIG:
        r = jnp.concatenate([reduced[n, l] for l in range(wts[n].shape[0])], axis=0)
        res = adamw("adamw_" + n, _as2d(wts[n]), [r], _as2d(ms[n]), _as2d(vs[n]))
        grads[n], deltas[n], new_m[n], new_v[n] = [r.reshape(wts[n].shape) for r in res]
    shapes = [wts[n].shape for n in SMALL]
    res = adamw("adamw_small", _pack([wts[n] for n in SMALL]), [_pack([gs[n] for n in SMALL])],
                _pack([ms[n] for n in SMALL]), _pack([vs[n] for n in SMALL]))
    for d, r in zip((grads, deltas, new_m, new_v), res):
        d.update(zip(SMALL, _unpack(r, shapes)))
    return (loss_out, grad_x, *[grads[n] for n in WEIGHTS], *[deltas[n] for n in WEIGHTS],
            *[new_m[n] for n in WEIGHTS], *[new_v[n] for n in WEIGHTS])
```

```python
import functools
import math

import jax
import jax.numpy as jnp
from jax import lax
from jax.experimental import pallas as pl
from jax.experimental.pallas import tpu as pltpu

F32 = jnp.float32
BF16 = jnp.bfloat16

D_MODEL = 1024
N_META = 16
DEPTH = 2
S5_WIDTH = 512
S5_GROUP = 16
S5_GROUPS = 32
S5_STATE = 64
S5_CHUNK = 16
SB_WIDTH = 512
SB_BLOCK = 128
SB_DEAD = -110.0
HG_HEADS = 8
HG_DK = 128
HG_CHUNK = 64
HG_UNROLL = 2
D_FF = 4096
ALPHA = (2.0 * DEPTH) ** 0.25
LN_EPS = 1e-5
RMS_EPS = 1e-6
ADAM_LR = 0.001
ADAM_B1 = 0.9
ADAM_B2 = 0.999
ADAM_EPS = 1e-08
ADAM_WD = 0.01
ADAM_STEP = 10
N_CHIPS = 4
N_DEV = 8
LANES = 128
MESH = pl.DeviceIdType.MESH
VMEM_LIMIT = 56 * 1024 * 1024
WHOLE_K_BYTES = 42 * 1024 * 1024

WEIGHTS = ['meta', 'w_in_ab', 's5_lam_re', 's5_lam_im', 's5_log_dt', 's5_b_re', 's5_b_im', 's5_c_re', 's5_c_im',
           's5_d', 's5_w_glu', 's5_b_glu', 'w_out_ab', 'w_in_c', 'hgrn_gamma', 'hgrn_norm_g', 'w_out_c',
           'ln_mix_g', 'ln_mix_b', 'mlp_w_up', 'mlp_b_up', 'mlp_w_down', 'mlp_b_down', 'ln_mlp_g', 'ln_mlp_b']
BIG = ['w_in_ab', 's5_w_glu', 'w_out_ab', 'w_in_c', 'w_out_c', 'mlp_w_up', 'mlp_w_down']
SHARDED_SMALL = ['meta', 'hgrn_norm_g']
SMALL = [n for n in WEIGHTS if n not in BIG]


def _params(sem=None):
    return pltpu.CompilerParams(dimension_semantics=sem, vmem_limit_bytes=VMEM_LIMIT)


def _pick(n, cands):
    for c in cands:
        if n % c == 0:
            return c
    return n


def _dot(a, b, ca, cb):
    return lax.dot_general(a, b, (((ca,), (cb,)), ((), ())), preferred_element_type=F32)


def _split3(x):
    hi = x.astype(BF16)
    r = x - hi.astype(F32)
    mid = r.astype(BF16)
    lo = (r - mid.astype(F32)).astype(BF16)
    return hi, mid, lo


def _dot_exact_rhs(x, m, cx, cm):
    hi = x.astype(BF16)
    lo = (x - hi.astype(F32)).astype(BF16)
    return _dot(hi, m, cx, cm) + _dot(lo, m, cx, cm)


def _dot3(a, b, ca, cb):
    ah = a.astype(BF16)
    al = (a - ah.astype(F32)).astype(BF16)
    bh = b.astype(BF16)
    bl = (b - bh.astype(F32)).astype(BF16)
    return _dot(ah, bh, ca, cb) + (_dot(ah, bl, ca, cb) + _dot(al, bh, ca, cb))


def rowwise(name, fn, row_ins, bc_ins, out_widths, sum_widths=(), out_dtypes=None, tm=None):
    t = row_ins[0].shape[0]
    if tm is None:
        wmax = max([a.shape[1] for a in row_ins] + list(out_widths))
        tm = _pick(t, (272, 256, 128, 64, 16, 8)) if wmax > 1024 else _pick(t, (544, 512, 256, 128, 64, 16, 8))
    nr, nb, no, ns = len(row_ins), len(bc_ins), len(out_widths), len(sum_widths)
    out_dtypes = out_dtypes or [F32] * no

    def body(*refs):
        i = pl.program_id(0)
        rows = [r[...] for r in refs[:nr]]
        bcs = [r[...] for r in refs[nr:nr + nb]]
        outs, sums = fn(i, tm, rows, bcs)
        for r, v in zip(refs[nr + nb:nr + nb + no], outs):
            r[...] = v.astype(r.dtype)
        if ns:
            srefs = refs[nr + nb + no:]

            @pl.when(i == 0)
            def _():
                for r in srefs:
                    r[...] = jnp.zeros(r.shape, r.dtype)

            for r, v in zip(srefs, sums):
                r[...] += v

    in_specs = [pl.BlockSpec((tm, a.shape[1]), lambda i: (i, 0)) for a in row_ins]
    in_specs += [pl.BlockSpec(a.shape, lambda i: (0, 0)) for a in bc_ins]
    out_specs = [pl.BlockSpec((tm, w), lambda i: (i, 0)) for w in out_widths]
    out_specs += [pl.BlockSpec((1, w), lambda i: (0, 0)) for w in sum_widths]
    out_shape = [jax.ShapeDtypeStruct((t, w), dt) for w, dt in zip(out_widths, out_dtypes)]
    out_shape += [jax.ShapeDtypeStruct((1, w), F32) for w in sum_widths]
    res = pl.pallas_call(body, name=name, grid=(t // tm,), in_specs=in_specs, out_specs=out_specs,
                         out_shape=out_shape, compiler_params=_params(("arbitrary",)))(*row_ins, *bc_ins)
    return res


def _colsum(v):
    return jnp.sum(v, axis=0, keepdims=True)


def matmul(name, a, b, mode, *, bias=None, act=None, add=None, add_scale=1.0, relu2_of=None, colsum=False,
           out_dtype=F32, out_slots=0, tm=None, tn=None, tk=None):
    slotted = b.ndim == 3
    if mode == 'nn':
        m, k = a.shape
        n = b.shape[-1] * (b.shape[0] if slotted else 1)
    elif mode == 'nt':
        m, k = a.shape
        n = b.shape[-2]
    else:
        k, m = a.shape
        n = b.shape[-1]
    ns = b.shape[-1] if slotted else None
    if mode == 'tn':
        tm = tm or _pick(m, (512, 256, 128))
        nw = n if not out_slots else n // out_slots
        if tn is None and tk is None:
            for cand in (512, 256):
                if nw % cand == 0 and (2 * k * (tm * a.dtype.itemsize + cand * b.dtype.itemsize)
                                       + 2 * tm * cand * 4 <= WHOLE_K_BYTES):
                    tn, tk = cand, k
                    break
        tn = tn or _pick(nw, (1024, 512, 256, 128))
        tk = tk or _pick(k, (1088, 1024, 768, 512, 384, 256, 128))
    else:
        tm = tm or _pick(m, (1088, 1024, 768, 512, 384, 256, 128))
        tn = tn or _pick(n if not (slotted and mode == 'nn') else ns, (1024, 512, 256, 128))
        extra = sum(x is not None for x in (add, relu2_of)) * 4
        if tk is None and not slotted and (2 * (tm * k * a.dtype.itemsize + k * tn * b.dtype.itemsize)
                                           + 2 * tm * tn * (4 + extra) <= WHOLE_K_BYTES):
            tk = k
        tk = tk or _pick(k if not (slotted and mode == 'nt') else ns, (1024, 512, 256, 128))
    gm, gn, gk = m // tm, n // tn, k // tk

    if mode == 'nn':
        a_spec = pl.BlockSpec((tm, tk), lambda i, j, kk: (i, kk))
        if slotted:
            per = ns // tn
            b_spec = pl.BlockSpec((None, tk, tn), lambda i, j, kk: (j // per, kk, j % per))
        else:
            b_spec = pl.BlockSpec((tk, tn), lambda i, j, kk: (kk, j))
        ca, cb = 1, 0
    elif mode == 'nt':
        a_spec = pl.BlockSpec((tm, tk), lambda i, j, kk: (i, kk))
        if slotted:
            per = ns // tk
            b_spec = pl.BlockSpec((None, tn, tk), lambda i, j, kk: (kk // per, j, kk % per))
        else:
            b_spec = pl.BlockSpec((tn, tk), lambda i, j, kk: (j, kk))
        ca, cb = 1, 1
    else:
        a_spec = pl.BlockSpec((tk, tm), lambda i, j, kk: (kk, i))
        b_spec = pl.BlockSpec((tk, tn), lambda i, j, kk: (kk, j))
        ca, cb = 0, 0
    in_specs = [a_spec, b_spec]
    args = [a, b]
    if bias is not None:
        in_specs.append(pl.BlockSpec((1, tn), lambda i, j, kk: (0, j)))
        args.append(bias)
    if add is not None:
        in_specs.append(pl.BlockSpec((tm, tn), lambda i, j, kk: (i, j)))
        args.append(add)
    if relu2_of is not None:
        in_specs.append(pl.BlockSpec((tm, tn), lambda i, j, kk: (i, j)))
        args.append(relu2_of)
    if out_slots:
        per_o = (n // out_slots) // tn
        out_spec = pl.BlockSpec((None, tm, tn), lambda i, j, kk: (j // per_o, i, j % per_o))
        out_shape = jax.ShapeDtypeStruct((out_slots, m, n // out_slots), out_dtype)
    else:
        out_spec = pl.BlockSpec((tm, tn), lambda i, j, kk: (i, j))
        out_shape = jax.ShapeDtypeStruct((m, n), out_dtype)
    grid = (gm, gn, gk)
    if colsum:
        assert mode == 'tn'
        out_spec = [out_spec, pl.BlockSpec((1, tn), lambda i, j, kk: (0, j))]
        out_shape = [out_shape, jax.ShapeDtypeStruct((1, n), F32)]

        def swapped(spec):
            return pl.BlockSpec(spec.block_shape, functools.partial(lambda j, i, kk, f: f(i, j, kk), f=spec.index_map))
        in_specs = [swapped(sp) for sp in in_specs]
        out_spec = [swapped(sp) for sp in out_spec]
        grid = (gn, gm, gk)

    def body(*refs):
        a_ref, b_ref = refs[0], refs[1]
        pos = 2
        bias_ref = add_ref = hid_ref = cs_ref = None
        if bias is not None:
            bias_ref = refs[pos]
            pos += 1
        if add is not None:
            add_ref = refs[pos]
            pos += 1
        if relu2_of is not None:
            hid_ref = refs[pos]
            pos += 1
        o_ref = refs[pos]
        pos += 1
        if colsum:
            cs_ref = refs[pos]
            pos += 1
        acc_ref = refs[pos] if gk > 1 else None
        bt = b_ref[...]
        p = _dot(a_ref[...].astype(BF16), bt.astype(BF16), ca, cb)
        kk = pl.program_id(2)

        if colsum:
            @pl.when(jnp.logical_and(pl.program_id(1) == 0, kk == 0))
            def _():
                cs_ref[...] = _colsum(bt.astype(F32))

            @pl.when(jnp.logical_and(pl.program_id(1) == 0, kk > 0))
            def _():
                cs_ref[...] += _colsum(bt.astype(F32))

        def finish(r):
            if bias_ref is not None:
                r = r + bias_ref[...]
            if act == 'relu2':
                r = jnp.square(jnp.maximum(r, 0.0))
            if hid_ref is not None:
                r = r * (2.0 * jnp.sqrt(hid_ref[...].astype(F32)))
            if add_ref is not None:
                r = r + add_scale * add_ref[...]
            o_ref[...] = r.astype(o_ref.dtype)

        if gk == 1:
            finish(p)
        else:
            @pl.when(kk == 0)
            def _():
                acc_ref[...] = p

            @pl.when(kk > 0)
            def _():
                acc_ref[...] += p

            @pl.when(kk == gk - 1)
            def _():
                finish(acc_ref[...])

    scratch = [pltpu.VMEM((tm, tn), F32)] if gk > 1 else []
    sem = ("arbitrary",) * 3 if colsum else ("parallel", "parallel", "arbitrary")
    return pl.pallas_call(body, name=name, grid=grid, in_specs=in_specs, out_specs=out_spec,
                          out_shape=out_shape, scratch_shapes=scratch, compiler_params=_params(sem))(*args)


def ln_fwd(name, h, mix, g, b):
    def fn(i, tm, rows, bcs):
        pre = ALPHA * rows[0] + rows[1]
        mu = jnp.mean(pre, axis=1, keepdims=True)
        xc = pre - mu
        var = jnp.mean(xc * xc, axis=1, keepdims=True)
        rstd = lax.rsqrt(var + LN_EPS)
        xhat = xc * rstd
        return (xhat * bcs[0] + bcs[1], xhat, rstd), ()
    return rowwise(name, fn, [h, mix], [g, b], [D_MODEL, D_MODEL, 1])


def ln_bwd(name, dy, xhat, rstd, g):
    def fn(i, tm, rows, bcs):
        dy_, xh, rs = rows
        dyg = dy_ * bcs[0]
        m1 = jnp.mean(dyg, axis=1, keepdims=True)
        m2 = jnp.mean(dyg * xh, axis=1, keepdims=True)
        dpre = rs * (dyg - m1 - xh * m2)
        return (dpre,), (_colsum(dy_ * xh), _colsum(dy_), _colsum(dpre))
    return rowwise(name, fn, [dy, xhat, rstd], [g], [D_MODEL], [D_MODEL] * 3)


_GELU_C = math.sqrt(2.0 / math.pi)


def gelu_fwd(name, x):
    def fn(i, tm, rows, bcs):
        v = rows[0]
        u = _GELU_C * (v + 0.044715 * (v * v * v))
        return (0.5 * v * (1.0 + jnp.tanh(u)),), ()
    return rowwise(name, fn, [x], [], [x.shape[1]])[0]


def gelu_bwd(name, dy, x):
    def fn(i, tm, rows, bcs):
        v = rows[1]
        th = jnp.tanh(_GELU_C * (v + 0.044715 * (v * v * v)))
        dg = 0.5 * (1.0 + th) + 0.5 * v * (1.0 - th * th) * (_GELU_C * (1.0 + 3.0 * 0.044715 * v * v))
        return (rows[0] * dg,), ()
    return rowwise(name, fn, [dy, x], [], [x.shape[1]])[0]


def glu_gate(name, y, gp):
    def fn(i, tm, rows, bcs):
        return (rows[0] * jax.nn.sigmoid(rows[1]),), ()
    return rowwise(name, fn, [y, gp], [], [y.shape[1]], out_dtypes=[BF16])[0]


def glu_gate_bwd(name, da, y, gp):
    def fn(i, tm, rows, bcs):
        s = jax.nn.sigmoid(rows[2])
        dgp = rows[0] * rows[1] * s * (1.0 - s)
        return (dgp, rows[0] * s), (_colsum(dgp),)
    w = y.shape[1]
    return rowwise(name, fn, [da, y, gp], [], [w, w], [w])


def loss_head(name, h, tgt, lp, seq):
    def fn(i, tm, rows, bcs):
        pos = (i * tm + lax.broadcasted_iota(jnp.int32, (tm, 1), 0)) % lp
        valid = jnp.logical_and(pos >= N_META, pos < N_META + seq)
        err = jnp.where(valid, rows[0] - rows[1], 0.0)
        part = 0.5 * jnp.sum(jnp.mean(err * err, axis=1, keepdims=True), axis=0, keepdims=True)
        return (err * (1.0 / D_MODEL),), (jnp.broadcast_to(part, (1, LANES)),)
    return rowwise(name, fn, [h, tgt], [], [D_MODEL], [LANES])


def adamw(name, w, gs, m, v):
    ng = len(gs)

    def fn(i, tm, rows, bcs):
        w_ = rows[0]
        g = rows[1] if ng == 1 else rows[1] + rows[2]
        m_, v_ = rows[1 + ng], rows[2 + ng]
        m_ = ADAM_B1 * m_ + (1.0 - ADAM_B1) * g
        v_ = ADAM_B2 * v_ + (1.0 - ADAM_B2) * jnp.square(g)
        m_hat = m_ / (1.0 - ADAM_B1 ** ADAM_STEP)
        v_hat = v_ / (1.0 - ADAM_B2 ** ADAM_STEP)
        delta = -ADAM_LR * (m_hat / (jnp.sqrt(v_hat) + ADAM_EPS) + ADAM_WD * w_)
        return (g, delta, m_, v_), ()
    wd = w.shape[1]
    return rowwise(name, fn, [w] + list(gs) + [m, v], [], [wd] * 4)


def s5_matrices(lam_re, lam_im, log_dt, b_re, b_im, c_re, c_im, d):
    c, hh, gg, pp = S5_CHUNK, S5_GROUP, S5_GROUPS, S5_STATE
    dt = jnp.exp(log_dt)[:, None]
    tau = jnp.arange(c + 1, dtype=F32)[None, :, None]
    mag = jnp.exp(tau * (lam_re * dt)[:, None, :])
    ang = tau * (lam_im * dt)[:, None, :]
    pw_re, pw_im = mag * jnp.cos(ang), mag * jnp.sin(ang)
    nr, ni = pw_re[:, 1] - 1.0, pw_im[:, 1]
    den = lam_re * lam_re + lam_im * lam_im
    cf_re = (nr * lam_re + ni * lam_im) / den
    cf_im = (ni * lam_re - nr * lam_im) / den
    bb_re = cf_re[:, :, None] * b_re - cf_im[:, :, None] * b_im
    bb_im = cf_re[:, :, None] * b_im + cf_im[:, :, None] * b_re
    ct_re, ct_im = c_re.transpose(0, 2, 1)[:, :, None, :], c_im.transpose(0, 2, 1)[:, :, None, :]
    pt_re, pt_im = pw_re.transpose(0, 2, 1)[:, :, :, None], pw_im.transpose(0, 2, 1)[:, :, :, None]
    cp_re = ct_re * pt_re - ct_im * pt_im
    cp_im = ct_re * pt_im + ct_im * pt_re
    hp = lax.Precision.HIGHEST
    kmat = (jnp.einsum('gpk,gpn->gkn', bb_re, cp_re[:, :, :c].reshape(gg, pp, c * hh), precision=hp)
            - jnp.einsum('gpk,gpn->gkn', bb_im, cp_im[:, :, :c].reshape(gg, pp, c * hh), precision=hp))
    rows = [jnp.pad(kmat[:, :, :(c - j) * hh], ((0, 0), (0, 0), (j * hh, 0))) for j in range(c)]
    a1 = jnp.stack(rows, axis=1).reshape(gg, c * hh, c * hh)
    a1 = a1 + jnp.eye(c * hh, dtype=F32)[None] * jnp.tile(d, (1, c))[:, None, :]
    a2 = jnp.concatenate([cp_re[:, :, 1:].reshape(gg, pp, c * hh), -cp_im[:, :, 1:].reshape(gg, pp, c * hh)], axis=1)
    rv_re, rv_im = pw_re[:, :c][:, ::-1, None, :], pw_im[:, :c][:, ::-1, None, :]
    bt_re, bt_im = bb_re.transpose(0, 2, 1)[:, None], bb_im.transpose(0, 2, 1)[:, None]
    a3 = jnp.concatenate([rv_re * bt_re - rv_im * bt_im, rv_re * bt_im + rv_im * bt_re], axis=-1)
    a3 = a3.reshape(gg, c * hh, 2 * pp)
    are = jnp.concatenate([pw_re[:, c], pw_re[:, c]], axis=-1)[:, None, :]
    aim = jnp.concatenate([-pw_im[:, c], pw_im[:, c]], axis=-1)[:, None, :]
    return a1, a2, a3, are, aim


S5_LEVELS = 8


def s5_scan_powers(lam_re, lam_im, log_dt):
    dt = jnp.exp(log_dt)[:, None]
    e = (S5_CHUNK * 2.0 ** jnp.arange(S5_LEVELS, dtype=F32))[:, None, None]
    mag = jnp.exp(e * (lam_re * dt)[None])
    ang = e * (lam_im * dt)[None]
    pr, pi = mag * jnp.cos(ang), mag * jnp.sin(ang)
    pre = jnp.concatenate([pr, pr], axis=-1).transpose(1, 0, 2)
    pim = jnp.concatenate([-pi, pi], axis=-1).transpose(1, 0, 2)
    return pre, pim


def _s5_scan(x, pre, pim, reverse):
    n = x.shape[0]
    rowi = lax.broadcasted_iota(jnp.int32, (n, 1), 0)
    t = x
    for k in range(S5_LEVELS):
        shift = 2 ** k
        if shift >= n:
            break
        p_re, p_im = pre[k:k + 1, :], pim[k:k + 1, :]
        if reverse:
            far = jnp.where(rowi < n - shift, pltpu.roll(t, n - shift, 0), 0.0)
            t = t + (p_re * far + pltpu.roll(p_im * far, S5_STATE, 1))
        else:
            far = jnp.where(rowi >= shift, pltpu.roll(t, shift, 0), 0.0)
            t = t + (p_re * far + p_im * pltpu.roll(far, S5_STATE, 1))
    return t


def s5_fwd(ug, a1, a2, a3, pre, pim, nseq):
    g, nc, cw = ug.shape
    per = nc // nseq
    sw = 2 * S5_STATE
    assert per <= 2 ** S5_LEVELS

    def body(u_ref, a1_ref, a2_ref, a3_ref, pre_ref, pim_ref, y_ref, s_ref):
        u = u_ref[...]
        x = _dot3(u, a3_ref[...], 1, 0)
        first = lax.broadcasted_iota(jnp.int32, (per, 1), 0) == 0
        for b in range(nseq):
            t = _s5_scan(x[b * per:(b + 1) * per], pre_ref[...], pim_ref[...], False)
            s_ref[pl.ds(b * per, per), :] = jnp.where(first, 0.0, pltpu.roll(t, 1, 0))
        y_ref[...] = _dot3(u, a1_ref[...], 1, 0) + _dot3(s_ref[...], a2_ref[...], 1, 0)

    def spec(shape):
        return pl.BlockSpec((None,) + shape, lambda i: (i, 0, 0))
    lv = (S5_LEVELS, sw)
    return pl.pallas_call(
        body, name="s5_fwd", grid=(g,),
        in_specs=[spec((nc, cw)), spec((cw, cw)), spec((sw, cw)), spec((cw, sw)), spec(lv), spec(lv)],
        out_specs=[spec((nc, cw)), spec((nc, sw))],
        out_shape=[jax.ShapeDtypeStruct((g, nc, cw), F32), jax.ShapeDtypeStruct((g, nc, sw), F32)],
        compiler_params=_params(("parallel",)))(ug, a1, a2, a3, pre, pim)


def s5_bwd(dyg, ug, sst, a1, a2, a3, pre, pim, nseq, ex=None):
    g, nc, cw = ug.shape
    per = nc // nseq
    sw = 2 * S5_STATE
    split, ex_start, ex_finish, exkw = _hidden_exchange(ex, 8, 6, (g,))

    def body(*refs):
        ((dy_ref, u_ref, s_ref, a1_ref, a2_ref, a3_ref, pre_ref, pim_ref),
         (du_ref, da1_ref, da2_ref, da3_ref, dare_ref, daim_ref), ex_parts, (dx_ref,)) = split(refs)
        ex_start(ex_parts)
        dy = dy_ref[...]
        u = u_ref[...]
        gd = _dot3(dy, a2_ref[...], 1, 1)
        s_all = s_ref[...]
        da2_ref[...] = _dot3(s_all, dy, 0, 0)
        last = lax.broadcasted_iota(jnp.int32, (per, 1), 0) == per - 1
        for b in range(nseq):
            gs = _s5_scan(gd[b * per:(b + 1) * per], pre_ref[...], pim_ref[...], True)
            dx_ref[pl.ds(b * per, per), :] = jnp.where(last, 0.0, pltpu.roll(gs, per - 1, 0))
        dx = dx_ref[...]
        dare_ref[...] = _colsum(dx * s_all)
        daim_ref[...] = _colsum(dx * pltpu.roll(s_all, S5_STATE, 1))
        du_ref[...] = _dot3(dy, a1_ref[...], 1, 1) + _dot3(dx, a3_ref[...], 1, 1)
        da1_ref[...] = _dot3(u, dy, 0, 0)
        da3_ref[...] = _dot3(u, dx, 0, 0)
        ex_finish(ex_parts)

    def spec(shape):
        return pl.BlockSpec((None,) + shape, lambda i: (i, 0, 0))
    sds = jax.ShapeDtypeStruct
    return pl.pallas_call(
        body, name="s5_bwd", grid=(g,),
        in_specs=[spec((nc, cw)), spec((nc, cw)), spec((nc, sw)), spec((cw, cw)), spec((sw, cw)), spec((cw, sw)),
                  spec((S5_LEVELS, sw)), spec((S5_LEVELS, sw))] + exkw['in_specs'],
        out_specs=[spec((nc, cw)), spec((cw, cw)), spec((sw, cw)), spec((cw, sw)), spec((1, sw)), spec((1, sw))]
        + exkw['out_specs'],
        out_shape=[sds((g, nc, cw), F32), sds((g, cw, cw), F32), sds((g, sw, cw), F32), sds((g, cw, sw), F32),
                   sds((g, 1, sw), F32), sds((g, 1, sw), F32)] + exkw['out_shape'],
        input_output_aliases=exkw['aliases'], scratch_shapes=[pltpu.VMEM((nc, sw), F32)] + exkw['scratch'],
        compiler_params=_params(("arbitrary",)))(dyg, ug, sst, a1, a2, a3, pre, pim, *exkw['ins'])


def _to_groups(u):
    t = u.shape[0]
    nc = t // S5_CHUNK
    return u.reshape(nc, S5_CHUNK, S5_GROUPS, S5_GROUP).transpose(2, 0, 1, 3).reshape(S5_GROUPS, nc, -1)


def _from_groups(yg):
    g, nc, _ = yg.shape
    return yg.reshape(g, nc, S5_CHUNK, S5_GROUP).transpose(1, 2, 0, 3).reshape(nc * S5_CHUNK, g * S5_GROUP)


def _sb_masks():
    row = lax.broadcasted_iota(jnp.int32, (SB_BLOCK, SB_BLOCK), 0)
    col = lax.broadcasted_iota(jnp.int32, (SB_BLOCK, SB_BLOCK), 1)
    lane = lax.broadcasted_iota(jnp.int32, (1, LANES), 1)
    return row, col, [lane < 64, lane >= 64]


def _sb_weights(z, vis, later_of, after):
    sp = jnp.maximum(z, 0.0) + jnp.log1p(jnp.exp(-jnp.abs(z)))
    lk = jnp.where(vis, -sp, 0.0)
    rs = jnp.sum(lk, axis=1, keepdims=True)
    later = _dot_exact_rhs(lk, after, 1, 0) + later_of(rs)
    lb = z - sp
    w = jnp.where(vis, jnp.exp(lb + later), 0.0)
    return w, rs, lb


def _hidden_exchange(ex, n_in, n_out, grid):
    ni, no = (len(ex.ins), len(ex.out_shapes)) if ex else (0, 0)

    def split(refs):
        a, b = n_in + ni, n_in + ni + n_out
        end = len(refs) - (2 if ex else 0)
        return refs[:n_in], refs[a:b], (refs[n_in:a], refs[b:b + no], refs[end:]), refs[b + no:end]

    def at_step(which, fn, parts):
        if ex is not None:
            ids = [pl.program_id(d) == (0 if which == 'first' else g - 1) for d, g in enumerate(grid)]
            cond = ids[0]
            for c in ids[1:]:
                cond = jnp.logical_and(cond, c)
            pl.when(cond)(lambda: fn(parts[0], parts[1], *parts[2]))

    anyspec = pl.BlockSpec(memory_space=pl.ANY)
    kw = dict(in_specs=[anyspec] * ni, out_specs=[anyspec] * no, out_shape=list(ex.out_shapes) if ex else [],
              aliases={n_in + i: n_out + j for i, j in (ex.aliases().items() if ex else ())},
              scratch=ex.sems() if ex else [], ins=list(ex.ins) if ex else [])
    start = functools.partial(at_step, 'first', ex.start if ex else None)
    finish = functools.partial(at_step, 'last', ex.finish if ex else None)
    return split, start, finish, kw


def _lane_tile(x, tl):
    return x[:, tl * LANES:(tl + 1) * LANES]


SB_TILES = 2
_SB_CHAINS = [(tl, hd) for tl in range(SB_TILES) for hd in range(2)]


def _sb_specs(lp, nseq):
    wd = SB_TILES * LANES
    off = [(S5_WIDTH + i * SB_WIDTH) // wd for i in range(3)]
    blk = (lp, wd)
    qkv = [pl.BlockSpec(blk, functools.partial(lambda b, h, o: (b, o + h), o=o)) for o in off]
    return (blk, qkv, pl.BlockSpec(blk, lambda b, h: (b, h)),
            pl.BlockSpec((None, None, 8, LANES), lambda b, h: (b, h, 0, 0)))


def attn_fwd(proj, nseq, lp, ex=None):
    nqb = lp // SB_BLOCK
    t = proj.shape[0]
    nstep = SB_WIDTH // (SB_TILES * LANES)
    split, ex_start, ex_finish, exkw = _hidden_exchange(ex, 3, 3, (nseq, nstep))

    def body(*refs):
        (q_ref, k_ref, v_ref), (o_ref, tot_ref, first_ref), ex_parts, _ = split(refs)
        ex_start(ex_parts)
        row, col, hmask = _sb_masks()
        after = (row > col).astype(BF16)
        tri = col < row
        lane8 = lax.broadcasted_iota(jnp.int32, (8, LANES), 1)

        def qloop(qi, firsts):
            q0 = pl.multiple_of(qi * SB_BLOCK, SB_BLOCK)
            q = q_ref[pl.ds(q0, SB_BLOCK), :] * 0.125
            qm = [jnp.where(hmask[hd], _lane_tile(q, tl), 0.0).astype(BF16) for tl, hd in _SB_CHAINS]

            def alive(carry):
                return jnp.logical_and(carry[0] <= qi, carry[1] > 0)

            def kstep(carry):
                s = carry[0]
                accs, lats = list(carry[2:2 + SB_TILES]), list(carry[2 + SB_TILES:])
                kj = qi - s
                k0 = pl.multiple_of(kj * SB_BLOCK, SB_BLOCK)
                k = k_ref[pl.ds(k0, SB_BLOCK), :].astype(BF16)
                v = v_ref[pl.ds(k0, SB_BLOCK), :]
                vis = jnp.logical_or(kj < qi, tri)
                for c, (tl, hd) in enumerate(_SB_CHAINS):
                    z = _dot(qm[c], _lane_tile(k, tl), 1, 1)
                    w, rs, _ = _sb_weights(z, vis, functools.partial(lambda rs, lat: lat, lat=lats[c]), after)
                    vm = jnp.where(hmask[hd], _lane_tile(v, tl), 0.0).astype(BF16)
                    accs[tl] = accs[tl] + _dot(w.astype(BF16), vm, 1, 0)
                    lats[c] = lats[c] + rs
                top = functools.reduce(jnp.maximum, lats)
                go = (jnp.max(top) > SB_DEAD).astype(jnp.int32)
                return (s + 1, go, *accs, *lats)

            z1 = jnp.zeros((SB_BLOCK, 1), F32)
            res = lax.while_loop(alive, kstep, (jnp.int32(0), jnp.int32(1),
                                                *[jnp.zeros((SB_BLOCK, LANES), F32)] * SB_TILES,
                                                *[z1] * len(_SB_CHAINS)))
            accs, lats = res[2:2 + SB_TILES], res[2 + SB_TILES:]
            for tl in range(SB_TILES):
                cols = slice(tl * LANES, (tl + 1) * LANES)
                o_ref[pl.ds(q0, SB_BLOCK), cols] = accs[tl].astype(o_ref.dtype)
                tot_ref[pl.ds(q0, SB_BLOCK), cols] = jnp.where(hmask[0], lats[2 * tl], lats[2 * tl + 1])
            return jnp.where(lane8 == qi, (qi - res[0] + 1).astype(F32), firsts)

        first_ref[...] = lax.fori_loop(0, nqb, qloop, jnp.zeros((8, LANES), F32))
        ex_finish(ex_parts)

    blk, qkv, out, vec = _sb_specs(lp, nseq)
    return pl.pallas_call(
        body, name="attn_fwd", grid=(nseq, nstep),
        in_specs=qkv + exkw['in_specs'], out_specs=[out, out, vec] + exkw['out_specs'],
        out_shape=[jax.ShapeDtypeStruct((t, SB_WIDTH), BF16), jax.ShapeDtypeStruct((t, SB_WIDTH), F32)]
        + [jax.ShapeDtypeStruct((nseq, nstep, 8, LANES), F32)] + exkw['out_shape'],
        input_output_aliases=exkw['aliases'], scratch_shapes=exkw['scratch'],
        compiler_params=_params(("arbitrary", "arbitrary")))(proj, proj, proj, *exkw['ins'])


def attn_bwd(proj, tot, first, do, nseq, lp, ex=None):
    nqb = lp // SB_BLOCK
    t = proj.shape[0]
    nstep = SB_WIDTH // (SB_TILES * LANES)
    split, ex_start, ex_finish, exkw = _hidden_exchange(ex, 6, 3, (nseq, nstep))

    def body(*refs):
        (q_ref, k_ref, v_ref, tot_ref, first_ref, do_ref), (dq_ref, dk_ref, dv_ref), ex_parts, _ = split(refs)
        ex_start(ex_parts)
        row, col, hmask = _sb_masks()
        after = (row > col).astype(BF16)
        before = (row < col).astype(BF16)
        tri = col < row
        lane8 = lax.broadcasted_iota(jnp.int32, (8, LANES), 1)
        firsts = first_ref[...]
        dk_ref[...] = jnp.zeros(dk_ref.shape, F32)
        dv_ref[...] = jnp.zeros(dv_ref.shape, F32)
        nc = len(_SB_CHAINS)

        def qloop(qi, _):
            first = jnp.max(jnp.where(lane8 == qi, firsts, 0.0)).astype(jnp.int32)
            first = jnp.clip(first, 0, qi)
            q0 = pl.multiple_of(qi * SB_BLOCK, SB_BLOCK)
            q = q_ref[pl.ds(q0, SB_BLOCK), :] * 0.125
            do_ = do_ref[pl.ds(q0, SB_BLOCK), :]
            tot_ = tot_ref[pl.ds(q0, SB_BLOCK), :]
            qm = [jnp.where(hmask[hd], _lane_tile(q, tl), 0.0).astype(BF16) for tl, hd in _SB_CHAINS]
            dom = [jnp.where(hmask[hd], _lane_tile(do_, tl), 0.0).astype(BF16) for tl, hd in _SB_CHAINS]
            tot = [jnp.max(jnp.where(hmask[hd], _lane_tile(tot_, tl), -jnp.inf), axis=1, keepdims=True)
                   for tl, hd in _SB_CHAINS]

            def kloop(kj, carry):
                dqs = list(carry[:SB_TILES])
                pre = list(carry[SB_TILES:SB_TILES + nc])
                gpre = list(carry[SB_TILES + nc:])
                k0 = pl.multiple_of(kj * SB_BLOCK, SB_BLOCK)
                kf = k_ref[pl.ds(k0, SB_BLOCK), :]
                k = kf.astype(BF16)
                v = v_ref[pl.ds(k0, SB_BLOCK), :].astype(BF16)
                vis = jnp.logical_or(kj < qi, tri)
                dk_acc = [jnp.zeros((SB_BLOCK, LANES), F32)] * SB_TILES
                dv_acc = [jnp.zeros((SB_BLOCK, LANES), F32)] * SB_TILES
                for c, (tl, hd) in enumerate(_SB_CHAINS):
                    z = _dot(qm[c], _lane_tile(k, tl), 1, 1)
                    later_of = functools.partial(lambda rs, t_, p_: t_ - p_ - rs, t_=tot[c], p_=pre[c])
                    w, rs, lb = _sb_weights(z, vis, later_of, after)
                    dw = _dot(dom[c], _lane_tile(v, tl), 1, 1)
                    g = dw * w
                    dlk = _dot_exact_rhs(g, before, 1, 0) + gpre[c]
                    sig = jnp.exp(lb)
                    dz = jnp.where(vis, g * (1.0 - sig) - dlk * sig, 0.0).astype(BF16)
                    km = jnp.where(hmask[hd], _lane_tile(kf, tl), 0.0).astype(BF16)
                    dqs[tl] = dqs[tl] + _dot(dz, km, 1, 0)
                    dk_acc[tl] = dk_acc[tl] + _dot(dz, qm[c], 0, 0)
                    dv_acc[tl] = dv_acc[tl] + _dot(w.astype(BF16), dom[c], 0, 0)
                    pre[c] = pre[c] + rs
                    gpre[c] = gpre[c] + jnp.sum(g, axis=1, keepdims=True)
                for tl in range(SB_TILES):
                    cols = slice(tl * LANES, (tl + 1) * LANES)
                    dk_ref[pl.ds(k0, SB_BLOCK), cols] += dk_acc[tl]
                    dv_ref[pl.ds(k0, SB_BLOCK), cols] += dv_acc[tl]
                return (*dqs, *pre, *gpre)

            z1 = jnp.zeros((SB_BLOCK, 1), F32)
            res = lax.fori_loop(first, qi + 1, kloop,
                                (*[jnp.zeros((SB_BLOCK, LANES), F32)] * SB_TILES, *[z1] * (2 * nc)))
            for tl in range(SB_TILES):
                dq_ref[pl.ds(q0, SB_BLOCK), tl * LANES:(tl + 1) * LANES] = res[tl] * 0.125
            return 0

        lax.fori_loop(0, nqb, qloop, 0)
        ex_finish(ex_parts)

    blk, qkv, out, vec = _sb_specs(lp, nseq)
    return pl.pallas_call(
        body, name="attn_bwd", grid=(nseq, nstep),
        in_specs=qkv + [out, vec, out] + exkw['in_specs'], out_specs=[out] * 3 + exkw['out_specs'],
        out_shape=[jax.ShapeDtypeStruct((t, SB_WIDTH), F32)] * 3 + exkw['out_shape'],
        input_output_aliases=exkw['aliases'], scratch_shapes=exkw['scratch'],
        compiler_params=_params(("arbitrary", "arbitrary")))(proj, proj, proj, tot, first, do, *exkw['ins'])


def _hg_loop(nch, step, init):
    assert nch % HG_UNROLL == 0

    def trip(i, carry):
        for u in range(HG_UNROLL):
            carry = step(i * HG_UNROLL + u, carry)
        return carry
    return lax.fori_loop(0, nch // HG_UNROLL, trip, init)


def _hg_gates(fc, lb):
    sg = jax.nn.sigmoid(fc)
    f = lb + (1.0 - lb) * sg
    return sg, f


def _hg_decay(f, incl):
    bcum = _dot_exact_lhs(incl, jnp.log(f))
    return bcum, bcum[HG_CHUNK - 1:HG_CHUNK, :]


def _dot_exact_lhs(m, x):
    hi, mid, lo = _split3(x)
    return _dot(m, hi, 1, 0) + (_dot(m, mid, 1, 0) + _dot(m, lo, 1, 0))


def hgrn_fwd(proj, lb, ng, nseq, lp):
    nch = lp // HG_CHUNK
    t = proj.shape[0]
    c = HG_CHUNK

    def body(q_ref, f_ref, v_ref, g_ref, lb_ref, ng_ref, o_ref, og_ref, st_ref):
        row = lax.broadcasted_iota(jnp.int32, (c, c), 0)
        col = lax.broadcasted_iota(jnp.int32, (c, c), 1)
        causal = col <= row
        incl = causal.astype(BF16)
        lbv, ngv = lb_ref[...], ng_ref[...]

        def chunk(b, ci, st):
            rows = pl.ds(pl.multiple_of(b * lp + ci * c, c), c)
            st_ref[b, ci] = st
            _, f = _hg_gates(f_ref[rows, :], lbv)
            bcum, blast = _hg_decay(f, incl)
            kk = 1.0 - f
            vc = v_ref[rows, :].astype(BF16)
            qd = (q_ref[rows, :] * jnp.exp(bcum)).astype(BF16)
            kd = (kk * jnp.exp(-bcum)).astype(BF16)
            a = jnp.where(causal, _dot(qd, kd, 1, 1), 0.0)
            o = _dot(a.astype(BF16), vc, 1, 0) + _dot(qd, st.astype(BF16), 1, 1)
            kend = (kk * jnp.exp(blast - bcum)).astype(BF16)
            st = st * jnp.exp(blast) + _dot(vc, kend, 0, 0)
            o_ref[rows, :] = o
            r = lax.rsqrt(jnp.mean(o * o, axis=1, keepdims=True) + RMS_EPS)
            gc = g_ref[rows, :]
            og_ref[rows, :] = (o * r * ngv * (gc * jax.nn.sigmoid(gc))).astype(og_ref.dtype)
            return st

        def step(ci, sts):
            return tuple(chunk(b, ci, sts[b]) for b in range(nseq))

        _hg_loop(nch, step, tuple(jnp.zeros((HG_DK, HG_DK), F32) for _ in range(nseq)))

    blk = (nseq * lp, LANES)
    h = HG_HEADS
    return pl.pallas_call(
        body, name="hgrn_fwd", grid=(h,),
        in_specs=[pl.BlockSpec(blk, lambda hh: (0, hh)),
                  pl.BlockSpec(blk, lambda hh: (0, h + hh)),
                  pl.BlockSpec(blk, lambda hh: (0, 2 * h + hh)),
                  pl.BlockSpec(blk, lambda hh: (0, 3 * h + hh)),
                  pl.BlockSpec((1, LANES), lambda hh: (0, hh)),
                  pl.BlockSpec((1, LANES), lambda hh: (0, hh))],
        out_specs=[pl.BlockSpec(blk, lambda hh: (0, hh)),
                   pl.BlockSpec(blk, lambda hh: (0, hh)),
                   pl.BlockSpec((nseq, None, nch, HG_DK, HG_DK), lambda hh: (0, hh, 0, 0, 0))],
        out_shape=[jax.ShapeDtypeStruct((t, D_MODEL), F32), jax.ShapeDtypeStruct((t, D_MODEL), BF16),
                   jax.ShapeDtypeStruct((nseq, h, nch, HG_DK, HG_DK), F32)],
        compiler_params=_params(("parallel",)))(proj, proj, proj, proj, lb, ng)


def hgrn_bwd(proj, o, dog, states, lb, ng, nseq, lp):
    nch = lp // HG_CHUNK
    t = proj.shape[0]
    c = HG_CHUNK

    def body(q_ref, f_ref, v_ref, g_ref, o_ref, dog_ref, st_ref, lb_ref, ng_ref,
             dq_ref, df_ref, dv_ref, dg_ref, dlb_ref, dng_ref):
        row = lax.broadcasted_iota(jnp.int32, (c, c), 0)
        col = lax.broadcasted_iota(jnp.int32, (c, c), 1)
        causal = col <= row
        incl = causal.astype(BF16)
        from_here = (col >= row).astype(BF16)
        last = lax.broadcasted_iota(jnp.int32, (c, 1), 0) == c - 1
        lbv, ngv = lb_ref[...], ng_ref[...]

        def step(s, carry):
            dst, dlb, dng = carry
            ci = nch - 1 - s
            r0 = pl.multiple_of(ci * c, c)
            rows = pl.ds(r0, c)
            oc, gc, dogc = o_ref[rows, :], g_ref[rows, :], dog_ref[rows, :]
            r = lax.rsqrt(jnp.mean(oc * oc, axis=1, keepdims=True) + RMS_EPS)
            sgg = jax.nn.sigmoid(gc)
            silu = gc * sgg
            dg_ref[rows, :] = dogc * (oc * r * ngv) * (sgg * (1.0 + gc * (1.0 - sgg)))
            don = dogc * silu
            dng = dng + _colsum(don * oc * r)
            tt = don * ngv
            do = r * (tt - oc * (r * r) * jnp.mean(tt * oc, axis=1, keepdims=True))
            st = st_ref[ci]
            fc = f_ref[rows, :]
            sg, f = _hg_gates(fc, lbv)
            bcum, blast = _hg_decay(f, incl)
            kk = 1.0 - f
            qc, vf = q_ref[rows, :], v_ref[rows, :]
            pdec = jnp.exp(bcum)
            ndec = jnp.exp(-bcum)
            edec = jnp.exp(blast - bcum)
            eb = jnp.exp(blast)
            qd_f, kd_f, kend_f = qc * pdec, kk * ndec, kk * edec
            qd, kd, kend = qd_f.astype(BF16), kd_f.astype(BF16), kend_f.astype(BF16)
            vc, dob, stb, dstb = vf.astype(BF16), do.astype(BF16), st.astype(BF16), dst.astype(BF16)
            a = jnp.where(causal, _dot(qd, kd, 1, 1), 0.0).astype(BF16)
            da = jnp.where(causal, _dot(dob, vc, 1, 1), 0.0).astype(BF16)
            dv_ref[rows, :] = _dot(a, dob, 0, 0) + _dot(kend, dstb, 1, 1)
            dqd = _dot(da, kd, 1, 0) + _dot(dob, stb, 1, 0)
            dkd = _dot(da, qd, 0, 0)
            dkend = _dot(vc, dstb, 1, 0)
            dq_ref[rows, :] = dqd * pdec
            dblast = _colsum(dkend * kend_f) + eb * _colsum(dst * st)
            dbcum = dqd * qd_f - dkd * kd_f - dkend * kend_f
            dbcum = dbcum + jnp.where(last, dblast, 0.0)
            dlf = _dot_exact_lhs(from_here, dbcum)
            df = dlf / f - (dkd * ndec + dkend * edec)
            df_ref[rows, :] = df * (1.0 - lbv) * sg * (1.0 - sg)
            dlb = dlb + _colsum(df * (1.0 - sg))
            dst = dst * eb + _dot(dob, qd, 0, 0)
            return dst, dlb, dng

        z = jnp.zeros((1, LANES), F32)
        _, dlb, dng = _hg_loop(nch, step, (jnp.zeros((HG_DK, HG_DK), F32), z, z))
        dlb_ref[...] = dlb
        dng_ref[...] = dng

    blk = (lp, LANES)
    h = HG_HEADS
    vec = pl.BlockSpec((1, LANES), lambda b, hh: (0, hh))
    svec = pl.BlockSpec((None, 1, LANES), lambda b, hh: (b, 0, hh))
    return pl.pallas_call(
        body, name="hgrn_bwd", grid=(nseq, h),
        in_specs=[pl.BlockSpec(blk, lambda b, hh: (b, hh)),
                  pl.BlockSpec(blk, lambda b, hh: (b, h + hh)),
                  pl.BlockSpec(blk, lambda b, hh: (b, 2 * h + hh)),
                  pl.BlockSpec(blk, lambda b, hh: (b, 3 * h + hh)),
                  pl.BlockSpec(blk, lambda b, hh: (b, hh)),
                  pl.BlockSpec(blk, lambda b, hh: (b, hh)),
                  pl.BlockSpec((None, None, nch, HG_DK, HG_DK), lambda b, hh: (b, hh, 0, 0, 0)),
                  vec, vec],
        out_specs=[pl.BlockSpec(blk, lambda b, hh: (b, hh))] * 4 + [svec, svec],
        out_shape=[jax.ShapeDtypeStruct((t, D_MODEL), F32)] * 4
        + [jax.ShapeDtypeStruct((nseq, 1, D_MODEL), F32)] * 2,
        compiler_params=_params(("parallel", "parallel")))(proj, proj, proj, proj, o, dog, states, lb, ng)


def _lower_bound(gamma):
    p = jax.nn.softmax(gamma, axis=0)
    return (jnp.cumsum(p, axis=0) - p[0])[1][None, :]


def local_step(x, tgt, w, hooks=None):
    nseq, seq, _ = x.shape
    lp = -(-(N_META + seq) // SB_BLOCK) * SB_BLOCK
    t = nseq * lp
    pad = lp - N_META - seq
    h0 = jnp.concatenate([jnp.broadcast_to(w['meta'][None], (nseq, N_META, D_MODEL)), x,
                          jnp.zeros((nseq, pad, D_MODEL), F32)], axis=1).reshape(t, D_MODEL)
    tgt_p = jnp.pad(tgt, ((0, 0), (N_META, pad), (0, 0))).reshape(t, D_MODEL)
    row = lambda v: v.reshape(1, -1)
    g = {}

    proj = matmul("in_ab", h0, w['w_in_ab'], 'nn')
    att = attn_fwd(proj, nseq, lp, ex=hooks.gather_late if hooks else None)
    b_out, sb_tot, sb_first = att[:3]
    if hooks:
        w = {**w, **hooks.late_weights(att[3:])}
    s5_names = ['s5_lam_re', 's5_lam_im', 's5_log_dt', 's5_b_re', 's5_b_im', 's5_c_re', 's5_c_im', 's5_d']
    mats, mats_vjp = jax.vjp(s5_matrices, *[w[n][0] for n in s5_names])
    ug = _to_groups(proj[:, :S5_WIDTH])
    s5_pows = s5_scan_powers(*[w[n][0] for n in s5_names[:3]])
    yg, sst = s5_fwd(ug, *mats[:3], *s5_pows, nseq)
    ys5 = _from_groups(yg)
    y = gelu_fwd("gelu", ys5)
    gp = matmul("glu", y, w['s5_w_glu'], 'nn', bias=w['s5_b_glu'])
    a_out = glu_gate("glu_gate", y, gp)
    cat = jnp.concatenate([a_out, b_out], axis=1)
    mix0 = matmul("out_ab", cat, w['w_out_ab'], 'nn')
    h1, xh1, rs1 = ln_fwd("ln_mix0", h0, mix0, row(w['ln_mix_g'][0]), row(w['ln_mix_b'][0]))
    hid0 = matmul("up0", h1, w['mlp_w_up'][0], 'nn', bias=row(w['mlp_b_up'][0]), act='relu2', out_dtype=BF16)
    dn0 = matmul("down0", hid0, w['mlp_w_down'][0], 'nn', bias=row(w['mlp_b_down'][0]))
    h2, xh2, rs2 = ln_fwd("ln_mlp0", h1, dn0, row(w['ln_mlp_g'][0]), row(w['ln_mlp_b'][0]))

    projc = matmul("in_c", h2, w['w_in_c'], 'nn')
    lb, lb_vjp = jax.vjp(_lower_bound, w['hgrn_gamma'])
    ng = w['hgrn_norm_g']
    o, og, states = hgrn_fwd(projc, lb, ng, nseq, lp)
    mix1 = matmul("out_c", og, w['w_out_c'], 'nn')
    h3, xh3, rs3 = ln_fwd("ln_mix1", h2, mix1, row(w['ln_mix_g'][1]), row(w['ln_mix_b'][1]))
    hid1 = matmul("up1", h3, w['mlp_w_up'][1], 'nn', bias=row(w['mlp_b_up'][1]), act='relu2', out_dtype=BF16)
    dn1 = matmul("down1", hid1, w['mlp_w_down'][1], 'nn', bias=row(w['mlp_b_down'][1]))
    h4, xh4, rs4 = ln_fwd("ln_mlp1", h3, dn1, row(w['ln_mlp_g'][1]), row(w['ln_mlp_b'][1]))
    dy, loss = loss_head("loss", h4, tgt_p, lp, seq)

    def mlp_bwd(l, dh_out, xh_out, rs_out, hid, h_in):
        dpre, dg_, db_, dbd = ln_bwd(f"ln_mlp{l}_b", dh_out, xh_out, rs_out, row(w['ln_mlp_g'][l]))
        dpu = matmul(f"down{l}_dx", dpre, w['mlp_w_down'][l], 'nt', relu2_of=hid, out_dtype=BF16)
        dwd = matmul(f"down{l}_dw", hid, dpre, 'tn')
        dh_in = matmul(f"up{l}_dx", dpu, w['mlp_w_up'][l], 'nt', add=dpre, add_scale=ALPHA)
        dwu, dbu = matmul(f"up{l}_dw", h_in, dpu, 'tn', out_slots=N_CHIPS, colsum=True)
        return dh_in, dict(ln_mlp_g=dg_, ln_mlp_b=db_, mlp_b_down=dbd, mlp_b_up=dbu, mlp_w_up=dwu, mlp_w_down=dwd)

    dh3, gm1 = mlp_bwd(1, dy, xh4, rs4, hid1, h3)
    dpre, dg1, db1, _ = ln_bwd("ln_mix1_b", dh3, xh3, rs3, row(w['ln_mix_g'][1]))
    g['w_out_c'] = matmul("out_c_dw", og, dpre, 'tn')
    dog = matmul("out_c_dx", dpre, w['w_out_c'], 'nt')
    dq, df, di, dgg, dlb, dng = hgrn_bwd(projc, o, dog, states, lb, ng, nseq, lp)
    dprojc = jnp.concatenate([dq, df, di, dgg], axis=1)
    dh2 = matmul("in_c_dx", dprojc, w['w_in_c'], 'nt', add=dpre, add_scale=ALPHA)
    g['w_in_c'] = matmul("in_c_dw", h2, dprojc, 'tn', out_slots=N_CHIPS)
    g['hgrn_gamma'] = lb_vjp(jnp.sum(dlb, axis=0))[0]
    g['hgrn_norm_g'] = jnp.sum(dng, axis=0)

    dh1, gm0 = mlp_bwd(0, dh2, xh2, rs2, hid0, h1)
    dpre, dg0, db0, _ = ln_bwd("ln_mix0_b", dh1, xh1, rs1, row(w['ln_mix_g'][0]))
    g['w_out_ab'] = matmul("out_ab_dw", cat, dpre, 'tn')
    dcat = matmul("out_ab_dx", dpre, w['w_out_ab'], 'nt')
    dgp, da_s, dbglu = glu_gate_bwd("glu_gate_b", dcat[:, :S5_WIDTH], y, gp)
    g['s5_w_glu'] = matmul("glu_dw", y, dgp, 'tn')
    g['s5_b_glu'] = dbglu
    dy5 = matmul("glu_dx", dgp, w['s5_w_glu'], 'nt', add=da_s)
    dys5 = gelu_bwd("gelu_b", dy5, ys5)
    for n in ('mlp_w_up', 'mlp_w_down'):
        g[n] = [gm0[n], gm1[n]]
    g['ln_mix_g'] = jnp.concatenate([dg0, dg1], axis=0)
    g['ln_mix_b'] = jnp.concatenate([db0, db1], axis=0)
    for n in ('ln_mlp_g', 'ln_mlp_b', 'mlp_b_down', 'mlp_b_up'):
        g[n] = jnp.concatenate([gm0[n], gm1[n]], axis=0)
    res = s5_bwd(_to_groups(dys5), ug, sst, *mats[:3], *s5_pows, nseq, ex=hooks.fold_early(g) if hooks else None)
    dug, da1, da2, da3, dare, daim = res[:6]
    for n, v in zip(s5_names, mats_vjp((da1, da2, da3, dare, daim))):
        g[n] = v[None]
    ex = hooks.scatter_early(g, loss, res[6:]) if hooks else None
    res = attn_bwd(proj, sb_tot, sb_first, dcat[:, S5_WIDTH:], nseq, lp, ex=ex)
    dqa, dka, dva = res[:3]
    if hooks:
        hooks.scattered(res[3:])
    dproj = jnp.concatenate([_from_groups(dug), dqa, dka, dva], axis=1)
    dh0 = matmul("in_ab_dx", dproj, w['w_in_ab'], 'nt', add=dpre, add_scale=ALPHA)
    g['w_in_ab'] = matmul("in_ab_dw", h0, dproj, 'tn', out_slots=N_CHIPS)

    dh0 = dh0.reshape(nseq, lp, D_MODEL)
    grad_x = dh0[:, N_META:N_META + seq]
    g['meta'] = jnp.sum(dh0[:, :N_META], axis=0)
    return loss, grad_x, g


class Exchange:
    def __init__(self, ins, out_shapes, n_remote, build, in_place):
        self.ins, self.out_shapes, self.n_remote, self.build, self.in_place = ins, out_shapes, n_remote, build, in_place

    def sems(self):
        return [pltpu.SemaphoreType.DMA((self.n_remote,)), pltpu.SemaphoreType.DMA((self.n_remote,))]

    def aliases(self):
        return self.in_place if isinstance(self.in_place, dict) else \
            {i: i for i in range(len(self.ins) if self.in_place else 0)}

    def beside(self, other):
        na, nao = len(self.ins), len(self.out_shapes)

        def build(in_refs, out_refs, me):
            pa = self.build(in_refs[:na], out_refs[:nao], me)
            pb = other.build(in_refs[na:], out_refs[nao:], me)
            n = max(len(pa), len(pb))
            return [(pa[i] if i < len(pa) else []) + (pb[i] if i < len(pb) else []) for i in range(n)]
        al = dict(self.aliases())
        al.update({na + i: nao + j for i, j in other.aliases().items()})
        return Exchange(list(self.ins) + list(other.ins), list(self.out_shapes) + list(other.out_shapes),
                        self.n_remote + other.n_remote, build, al)

    def phases(self, in_refs, out_refs, ssem, rsem):
        me = (lax.axis_index("x"), lax.axis_index("y"), lax.axis_index("c"))
        out, k = [], 0
        for phase in self.build(in_refs, out_refs, me):
            out.append(functools.partial(
                lambda phase, k: [pltpu.make_async_remote_copy(src_ref=s, dst_ref=d, send_sem=ssem.at[k + i],
                                                               recv_sem=rsem.at[k + i], device_id=p,
                                                               device_id_type=MESH)
                                  for i, (s, d, p) in enumerate(phase)], phase, k))
            k += len(phase)
        return out

    def start(self, in_refs, out_refs, ssem, rsem):
        for cp in self.phases(in_refs, out_refs, ssem, rsem)[0]():
            cp.start()

    def finish(self, in_refs, out_refs, ssem, rsem):
        phases = self.phases(in_refs, out_refs, ssem, rsem)
        for cp in phases[0]():
            cp.wait()
        for make in phases[1:]:
            copies = make()
            for cp in copies:
                cp.start()
            for cp in copies:
                cp.wait()


def _exchange(name, ex):
    ni, no = len(ex.ins), len(ex.out_shapes)

    def body(*refs):
        in_refs, out_refs, sems = refs[:ni], refs[ni:ni + no], refs[ni + no:]
        ex.start(in_refs, out_refs, *sems)
        ex.finish(in_refs, out_refs, *sems)

    anyspec = pl.BlockSpec(memory_space=pl.ANY)
    return pl.pallas_call(
        body, name=name, in_specs=[anyspec] * ni, out_specs=[anyspec] * no, out_shape=ex.out_shapes,
        input_output_aliases=ex.aliases(), scratch_shapes=ex.sems())(*ex.ins)


def _chip_peers(me):
    x, y, c = me
    return [(x ^ (m >> 1), y ^ (m & 1), c) for m in (1, 2, 3)]


def _half(ref, c, axis):
    h = ref.shape[axis] // 2
    idx = [slice(None)] * axis + [pl.ds(c * h, h)]
    return ref.at[tuple(idx)]


def _like(arrs):
    return [jax.ShapeDtypeStruct(a.shape, a.dtype) for a in arrs]


def gather_over_chips(bufs):
    def build(in_refs, out_refs, me):
        x, y, c = me
        j = 2 * x + y
        sib = (x, y, 1 - c)
        ici = [(_half(o.at[j], c, 0), _half(o.at[j], c, 0), p) for o in out_refs for p in _chip_peers(me)]
        d2d = [(_half(o.at[2 * p[0] + p[1]], c, 0), _half(o.at[2 * p[0] + p[1]], c, 0), sib)
               for o in out_refs for p in _chip_peers(me)]
        return [ici, d2d]
    return Exchange(bufs, _like(bufs), 6 * len(bufs), build, True)


def fold_cores(fulls):
    def build(in_refs, out_refs, me):
        x, y, c = me
        return [[(_half(s, 1 - c, 1), o, (x, y, 1 - c)) for s, o in zip(in_refs, out_refs)]]
    shapes = [jax.ShapeDtypeStruct((f.shape[0], f.shape[1] // 2, f.shape[2]), f.dtype) for f in fulls]
    return Exchange(fulls, shapes, len(fulls), build, False)


def scatter_over_chips(parts):
    def build(in_refs, out_refs, me):
        j = 2 * me[0] + me[1]
        return [[(s.at[2 * p[0] + p[1]], o.at[j], p) for s, o in zip(in_refs, out_refs) for p in _chip_peers(me)]]
    return Exchange(parts, _like(parts), 3 * len(parts), build, False)


def join_cores(bufs):
    def build(in_refs, out_refs, me):
        x, y, c = me
        return [[(o.at[c], o.at[c], (x, y, 1 - c)) for o in out_refs]]
    return Exchange(bufs, _like(bufs), len(bufs), build, True)


def gather_over_devices(buf):
    def build(in_refs, out_refs, me):
        x, y, c = me
        i = 4 * x + 2 * y + c
        out = out_refs[0]
        return [[(out.at[i], out.at[i], (x ^ (m >> 2), y ^ ((m >> 1) & 1), c ^ (m & 1))) for m in range(1, N_DEV)]]
    return Exchange([buf], _like([buf]), N_DEV - 1, build, True)


def _slot_call(name, body, scalars, ins, in_maps, out_shape, out_map, blk, grid):
    gs = pltpu.PrefetchScalarGridSpec(
        num_scalar_prefetch=1, grid=grid,
        in_specs=[pl.BlockSpec((None,) + blk, m) for m in in_maps],
        out_specs=pl.BlockSpec((None,) + blk, out_map))
    return pl.pallas_call(body, name=name, grid_spec=gs, out_shape=out_shape,
                          compiler_params=_params(("arbitrary",) * len(grid)))(scalars, *ins)


def cast_into_slot(name, w2d, chip, dtype):
    r, wd = w2d.shape
    tm = _pick(r, (512, 256, 128, 64, 32, 16))

    def body(s_ref, w_ref, o_ref):
        o_ref[...] = w_ref[...].astype(o_ref.dtype)

    gs = pltpu.PrefetchScalarGridSpec(
        num_scalar_prefetch=1, grid=(r // tm,),
        in_specs=[pl.BlockSpec((tm, wd), lambda i, s: (i, 0))],
        out_specs=pl.BlockSpec((None, tm, wd), lambda i, s: (s[0], i, 0)))
    return pl.pallas_call(body, name=name, grid_spec=gs,
                          out_shape=jax.ShapeDtypeStruct((N_CHIPS, r, wd), dtype),
                          compiler_params=_params(("arbitrary",)))(chip.reshape(1), w2d)


def sum_cores(name, full, theirs, core):
    s, r, wd = full.shape
    h = r // 2
    tm = _pick(h, (512, 256, 128, 64, 32, 16))
    nt = h // tm

    def body(s_ref, a_ref, b_ref, o_ref):
        o_ref[...] = (a_ref[...] + b_ref[...]).astype(o_ref.dtype)

    return _slot_call(name, body, core.reshape(1), [full, theirs],
                      [lambda k, i, s_: (k, s_[0] * nt + i, 0), lambda k, i, s_: (k, i, 0)],
                      jax.ShapeDtypeStruct((s, h, wd), BF16), lambda k, i, s_: (k, i, 0), (tm, wd), (s, nt))


def sum_chips(name, own, recv, chip, core):
    s, r, wd = own.shape
    tm = _pick(r, (512, 256, 128, 64, 32, 16))

    def body(s_ref, a_ref, b1_ref, b2_ref, b3_ref, o_ref):
        acc = a_ref[...].astype(F32)
        for ref in (b1_ref, b2_ref, b3_ref):
            acc = acc + ref[...].astype(F32)
        o_ref[...] = acc

    maps = [lambda i, s_: (s_[0], i, 0)]
    maps += [functools.partial(lambda i, s_, m: (s_[0] ^ m, i, 0), m=m) for m in (1, 2, 3)]
    return _slot_call(name, body, jnp.stack([chip, core]), [own, recv, recv, recv], maps,
                      jax.ShapeDtypeStruct((2, r, wd), F32), lambda i, s_: (s_[1], i, 0), (tm, wd), (r // tm,))


def sum_slots(name, arr):
    s, r, wd = arr.shape
    tm = _pick(r, (512, 256, 128, 64, 32, 16, 8))

    def body(*refs):
        acc = refs[0][...].astype(F32)
        for ref in refs[1:s]:
            acc = acc + ref[...].astype(F32)
        refs[s][...] = acc

    specs = [pl.BlockSpec((None, tm, wd), functools.partial(lambda i, k: (k, i, 0), k=k)) for k in range(s)]
    return pl.pallas_call(body, name=name, grid=(r // tm,), in_specs=specs,
                          out_specs=pl.BlockSpec((tm, wd), lambda i: (i, 0)),
                          out_shape=jax.ShapeDtypeStruct((r, wd), F32),
                          compiler_params=_params(("parallel",)))(*([arr] * s))


def _pack(arrs):
    flat = jnp.concatenate([a.reshape(-1) for a in arrs])
    n = flat.shape[0]
    padded = -(-n // (512 * LANES)) * (512 * LANES)
    return jnp.pad(flat, (0, padded - n)).reshape(-1, LANES)


def _unpack(packed, shapes):
    flat = packed.reshape(-1)
    out, pos = [], 0
    for s in shapes:
        n = math.prod(s)
        out.append(flat[pos:pos + n].reshape(s))
        pos += n
    return out


def _as2d(a):
    return a.reshape(-1, a.shape[-1])


def kernel(x, meta, w_in_ab, s5_lam_re, s5_lam_im, s5_log_dt, s5_b_re, s5_b_im, s5_c_re, s5_c_im, s5_d, s5_w_glu, s5_b_glu, w_out_ab, w_in_c, hgrn_gamma, hgrn_norm_g, w_out_c, ln_mix_g, ln_mix_b, mlp_w_up, mlp_b_up, mlp_w_down, mlp_b_down, ln_mlp_g, ln_mlp_b, loss_target, m_meta, m_w_in_ab, m_s5_lam_re, m_s5_lam_im, m_s5_log_dt, m_s5_b_re, m_s5_b_im, m_s5_c_re, m_s5_c_im, m_s5_d, m_s5_w_glu, m_s5_b_glu, m_w_out_ab, m_w_in_c, m_hgrn_gamma, m_hgrn_norm_g, m_w_out_c, m_ln_mix_g, m_ln_mix_b, m_mlp_w_up, m_mlp_b_up, m_mlp_w_down, m_mlp_b_down, m_ln_mlp_g, m_ln_mlp_b, v_meta, v_w_in_ab, v_s5_lam_re, v_s5_lam_im, v_s5_log_dt, v_s5_b_re, v_s5_b_im, v_s5_c_re, v_s5_c_im, v_s5_d, v_s5_w_glu, v_s5_b_glu, v_w_out_ab, v_w_in_c, v_hgrn_gamma, v_hgrn_norm_g, v_w_out_c, v_ln_mix_g, v_ln_mix_b, v_mlp_w_up, v_mlp_b_up, v_mlp_w_down, v_mlp_b_down, v_ln_mlp_g, v_ln_mlp_b):
    given = dict(locals())
    wts = {n: given[n] for n in WEIGHTS}
    ms = {n: given['m_' + n] for n in WEIGHTS}
    vs = {n: given['v_' + n] for n in WEIGHTS}
    chip = 2 * lax.axis_index("x") + lax.axis_index("y")

    core = lax.axis_index("c")
    parts = [(n, l) for n in BIG for l in (range(DEPTH) if n.startswith('mlp_') else [0])]
    tags = [f"{n}{l}" for n, l in parts]
    shards = {p: cast_into_slot("cast_" + t, wts[p[0]][p[1]], chip, BF16) for t, p in zip(tags, parts)}
    small_sh = jnp.concatenate([meta, hgrn_norm_g, jnp.zeros((15, meta.shape[1]), F32)], axis=0)
    first, late = parts[:1], parts[1:]
    w_in_ab_all, sm = _exchange("gather_first", gather_over_chips(
        [shards[p] for p in first] + [cast_into_slot("cast_small", small_sh, chip, F32)]))
    w = {n: wts[n] for n in SMALL}
    w['meta'] = sm[:, :N_META].transpose(1, 0, 2).reshape(N_META, D_MODEL)
    w['hgrn_norm_g'] = sm[:, N_META:N_META + 1].transpose(1, 0, 2).reshape(1, D_MODEL)
    w['w_in_ab'] = w_in_ab_all

    def slot_major(v):
        return v if v.ndim == 3 else v.reshape(N_CHIPS, -1, v.shape[-1])

    def whole_grads(ps, g):
        return [slot_major(g[n][l] if n.startswith('mlp_') else g[n]) for n, l in ps]

    def chip_sums_of(ps, gfull, theirs):
        return [sum_cores(f"sum_cores_{n}{l}", a, b, core) for (n, l), a, b in zip(ps, gfull, theirs)]

    def all_devices(arrs):
        packed = _pack(arrs)
        return gather_over_devices(lax.dynamic_update_slice(
            jnp.zeros((N_DEV,) + packed.shape, F32), packed[None], (2 * chip + core, 0, 0)))

    early_small = [n for n in SMALL if n != 'meta']

    class Hooks:
        gather_late = gather_over_chips([shards[p] for p in late])

        @staticmethod
        def late_weights(filled):
            fw = dict(zip(late, filled))
            return {'s5_w_glu': fw['s5_w_glu', 0].reshape(S5_WIDTH, S5_WIDTH),
                    'w_out_ab': fw['w_out_ab', 0].reshape(D_MODEL, D_MODEL),
                    'w_in_c': fw['w_in_c', 0],
                    'w_out_c': fw['w_out_c', 0].reshape(D_MODEL, D_MODEL),
                    'mlp_w_up': [fw['mlp_w_up', l] for l in range(DEPTH)],
                    'mlp_w_down': [fw['mlp_w_down', l].reshape(D_FF, D_MODEL) for l in range(DEPTH)]}

        @staticmethod
        def fold_early(g):
            Hooks.whole = whole_grads(late, g)
            return fold_cores(Hooks.whole)

        @staticmethod
        def scatter_early(g, loss, theirs):
            Hooks.sums = chip_sums_of(late, Hooks.whole, theirs)
            Hooks.small_shapes = [(LANES,)] + [g[n].shape for n in early_small]
            return scatter_over_chips(Hooks.sums).beside(
                all_devices([loss.reshape(-1)] + [g[n] for n in early_small]))

        @staticmethod
        def scattered(outs):
            Hooks.recv, Hooks.small = list(outs[:-1]), outs[-1]

    loss, grad_x, g = local_step(x, loss_target, w, Hooks)

    whole = whole_grads(first, g)
    sums = chip_sums_of(first, whole, _exchange("fold_last", fold_cores(whole)))
    *recv, meta_slots = _exchange("scatter_last", scatter_over_chips(sums).beside(all_devices([g['meta']])))
    reduced = _exchange("join_grads", join_cores(
        [sum_chips(f"sum_chips_{n}{l}", a, b, chip, core)
         for (n, l), a, b in zip(first + late, sums + Hooks.sums, list(recv) + Hooks.recv)]))
    reduced = dict(zip(first + late, [_as2d(r) for r in reduced]))
    red = _unpack(sum_slots("sum_small", Hooks.small), Hooks.small_shapes)
    loss_out = red[0][0]
    gs = dict(zip(early_small, red[1:]))
    gs['meta'] = _unpack(sum_slots("sum_meta", meta_slots), [g['meta'].shape])[0]
    gs['meta'] = lax.dynamic_slice_in_dim(gs['meta'], chip * meta.shape[1], meta.shape[1], axis=1)
    gs['hgrn_norm_g'] = lax.dynamic_slice_in_dim(gs['hgrn_norm_g'].reshape(1, -1), chip * meta.shape[1],
                                                 meta.shape[1], axis=1)

    grads, deltas, new_m, new_v = {}, {}, {}, {}
    for n in BIG:
        r = jnp.concatenate([reduced[n, l] for l in range(wts[n].shape[0])], axis=0)
        res = adamw("adamw_" + n, _as2d(wts[n]), [r], _as2d(ms[n]), _as2d(vs[n]))
        grads[n], deltas[n], new_m[n], new_v[n] = [r.reshape(wts[n].shape) for r in res]
    shapes = [wts[n].shape for n in SMALL]
    res = adamw("adamw_small", _pack([wts[n] for n in SMALL]), [_pack([gs[n] for n in SMALL])],
                _pack([ms[n] for n in SMALL]), _pack([vs[n] for n in SMALL]))
    for d, r in zip((grads, deltas, new_m, new_v), res):
        d.update(zip(SMALL, _unpack(r, shapes)))
    return (loss_out, grad_x, *[grads[n] for n in WEIGHTS], *[deltas[n] for n in WEIGHTS],
            *[new_m[n] for n in WEIGHTS], *[new_v[n] for n in WEIGHTS])
```

```python
import functools
import math

import jax
import jax.numpy as jnp
from jax import lax
from jax.experimental import pallas as pl
from jax.experimental.pallas import tpu as pltpu

F32 = jnp.float32
BF16 = jnp.bfloat16

D_MODEL = 1024
N_META = 16
DEPTH = 2
S5_WIDTH = 512
S5_GROUP = 16
S5_GROUPS = 32
S5_STATE = 64
S5_CHUNK = 16
SB_WIDTH = 512
SB_BLOCK = 128
SB_DEAD = -110.0
HG_HEADS = 8
HG_DK = 128
HG_CHUNK = 64
HG_UNROLL = 2
D_FF = 4096
ALPHA = (2.0 * DEPTH) ** 0.25
LN_EPS = 1e-5
RMS_EPS = 1e-6
ADAM_LR = 0.001
ADAM_B1 = 0.9
ADAM_B2 = 0.999
ADAM_EPS = 1e-08
ADAM_WD = 0.01
ADAM_STEP = 10
N_CHIPS = 4
N_DEV = 8
LANES = 128
MESH = pl.DeviceIdType.MESH
VMEM_LIMIT = 56 * 1024 * 1024
WHOLE_K_BYTES = 42 * 1024 * 1024

WEIGHTS = ['meta', 'w_in_ab', 's5_lam_re', 's5_lam_im', 's5_log_dt', 's5_b_re', 's5_b_im', 's5_c_re', 's5_c_im',
           's5_d', 's5_w_glu', 's5_b_glu', 'w_out_ab', 'w_in_c', 'hgrn_gamma', 'hgrn_norm_g', 'w_out_c',
           'ln_mix_g', 'ln_mix_b', 'mlp_w_up', 'mlp_b_up', 'mlp_w_down', 'mlp_b_down', 'ln_mlp_g', 'ln_mlp_b']
BIG = ['w_in_ab', 's5_w_glu', 'w_out_ab', 'w_in_c', 'w_out_c', 'mlp_w_up', 'mlp_w_down']
SMALL = [n for n in WEIGHTS if n not in BIG]


def _params(sem=None):
    return pltpu.CompilerParams(dimension_semantics=sem, vmem_limit_bytes=VMEM_LIMIT)


def _pick(n, cands):
    for c in cands:
        if n % c == 0:
            return c
    return n


def _dot(a, b, ca, cb):
    return lax.dot_general(a, b, (((ca,), (cb,)), ((), ())), preferred_element_type=F32)


def _split3(x):
    hi = x.astype(BF16)
    r = x - hi.astype(F32)
    mid = r.astype(BF16)
    lo = (r - mid.astype(F32)).astype(BF16)
    return hi, mid, lo


def _dot_exact_rhs(x, m, cx, cm):
    hi = x.astype(BF16)
    lo = (x - hi.astype(F32)).astype(BF16)
    return _dot(hi, m, cx, cm) + _dot(lo, m, cx, cm)


def _dot3(a, b, ca, cb):
    ah = a.astype(BF16)
    al = (a - ah.astype(F32)).astype(BF16)
    bh = b.astype(BF16)
    bl = (b - bh.astype(F32)).astype(BF16)
    return _dot(ah, bh, ca, cb) + (_dot(ah, bl, ca, cb) + _dot(al, bh, ca, cb))


def rowwise(name, fn, row_ins, bc_ins, out_widths, sum_widths=(), out_dtypes=None, tm=None):
    t = row_ins[0].shape[0]
    if tm is None:
        wmax = max([a.shape[1] for a in row_ins] + list(out_widths))
        tm = _pick(t, (272, 256, 128, 64, 16, 8)) if wmax > 1024 else _pick(t, (544, 512, 256, 128, 64, 16, 8))
    nr, nb, no, ns = len(row_ins), len(bc_ins), len(out_widths), len(sum_widths)
    out_dtypes = out_dtypes or [F32] * no

    def body(*refs):
        i = pl.program_id(0)
        rows = [r[...] for r in refs[:nr]]
        bcs = [r[...] for r in refs[nr:nr + nb]]
        outs, sums = fn(i, tm, rows, bcs)
        for r, v in zip(refs[nr + nb:nr + nb + no], outs):
            r[...] = v.astype(r.dtype)
        if ns:
            srefs = refs[nr + nb + no:]

            @pl.when(i == 0)
            def _():
                for r in srefs:
                    r[...] = jnp.zeros(r.shape, r.dtype)

            for r, v in zip(srefs, sums):
                r[...] += v

    in_specs = [pl.BlockSpec((tm, a.shape[1]), lambda i: (i, 0)) for a in row_ins]
    in_specs += [pl.BlockSpec(a.shape, lambda i: (0, 0)) for a in bc_ins]
    out_specs = [pl.BlockSpec((tm, w), lambda i: (i, 0)) for w in out_widths]
    out_specs += [pl.BlockSpec((1, w), lambda i: (0, 0)) for w in sum_widths]
    out_shape = [jax.ShapeDtypeStruct((t, w), dt) for w, dt in zip(out_widths, out_dtypes)]
    out_shape += [jax.ShapeDtypeStruct((1, w), F32) for w in sum_widths]
    res = pl.pallas_call(body, name=name, grid=(t // tm,), in_specs=in_specs, out_specs=out_specs,
                         out_shape=out_shape, compiler_params=_params(("arbitrary",)))(*row_ins, *bc_ins)
    return res


def _colsum(v):
    return jnp.sum(v, axis=0, keepdims=True)


def matmul(name, a, b, mode, *, bias=None, act=None, add=None, add_scale=1.0, relu2_of=None, colsum=False,
           out_dtype=F32, out_slots=0, tm=None, tn=None, tk=None):
    slotted = b.ndim == 3
    if mode == 'nn':
        m, k = a.shape
        n = b.shape[-1] * (b.shape[0] if slotted else 1)
    elif mode == 'nt':
        m, k = a.shape
        n = b.shape[-2]
    else:
        k, m = a.shape
        n = b.shape[-1]
    ns = b.shape[-1] if slotted else None
    if mode == 'tn':
        tm = tm or _pick(m, (512, 256, 128))
        nw = n if not out_slots else n // out_slots
        if tn is None and tk is None:
            for cand in (512, 256):
                if nw % cand == 0 and (2 * k * (tm * a.dtype.itemsize + cand * b.dtype.itemsize)
                                       + 2 * tm * cand * 4 <= WHOLE_K_BYTES):
                    tn, tk = cand, k
                    break
        tn = tn or _pick(nw, (1024, 512, 256, 128))
        tk = tk or _pick(k, (1088, 1024, 768, 512, 384, 256, 128))
    else:
        tm = tm or _pick(m, (1088, 1024, 768, 512, 384, 256, 128))
        tn = tn or _pick(n if not (slotted and mode == 'nn') else ns, (1024, 512, 256, 128))
        extra = sum(x is not None for x in (add, relu2_of)) * 4
        if tk is None and not slotted and (2 * (tm * k * a.dtype.itemsize + k * tn * b.dtype.itemsize)
                                           + 2 * tm * tn * (4 + extra) <= WHOLE_K_BYTES):
            tk = k
        tk = tk or _pick(k if not (slotted and mode == 'nt') else ns, (1024, 512, 256, 128))
    gm, gn, gk = m // tm, n // tn, k // tk

    if mode == 'nn':
        a_spec = pl.BlockSpec((tm, tk), lambda i, j, kk: (i, kk))
        if slotted:
            per = ns // tn
            b_spec = pl.BlockSpec((None, tk, tn), lambda i, j, kk: (j // per, kk, j % per))
        else:
            b_spec = pl.BlockSpec((tk, tn), lambda i, j, kk: (kk, j))
        ca, cb = 1, 0
    elif mode == 'nt':
        a_spec = pl.BlockSpec((tm, tk), lambda i, j, kk: (i, kk))
        if slotted:
            per = ns // tk
            b_spec = pl.BlockSpec((None, tn, tk), lambda i, j, kk: (kk // per, j, kk % per))
        else:
            b_spec = pl.BlockSpec((tn, tk), lambda i, j, kk: (j, kk))
        ca, cb = 1, 1
    else:
        a_spec = pl.BlockSpec((tk, tm), lambda i, j, kk: (kk, i))
        b_spec = pl.BlockSpec((tk, tn), lambda i, j, kk: (kk, j))
        ca, cb = 0, 0
    in_specs = [a_spec, b_spec]
    args = [a, b]
    if bias is not None:
        in_specs.append(pl.BlockSpec((1, tn), lambda i, j, kk: (0, j)))
        args.append(bias)
    if add is not None:
        in_specs.append(pl.BlockSpec((tm, tn), lambda i, j, kk: (i, j)))
        args.append(add)
    if relu2_of is not None:
        in_specs.append(pl.BlockSpec((tm, tn), lambda i, j, kk: (i, j)))
        args.append(relu2_of)
    if out_slots:
        per_o = (n // out_slots) // tn
        out_spec = pl.BlockSpec((None, tm, tn), lambda i, j, kk: (j // per_o, i, j % per_o))
        out_shape = jax.ShapeDtypeStruct((out_slots, m, n // out_slots), out_dtype)
    else:
        out_spec = pl.BlockSpec((tm, tn), lambda i, j, kk: (i, j))
        out_shape = jax.ShapeDtypeStruct((m, n), out_dtype)
    grid = (gm, gn, gk)
    if colsum:
        assert mode == 'tn'
        out_spec = [out_spec, pl.BlockSpec((1, tn), lambda i, j, kk: (0, j))]
        out_shape = [out_shape, jax.ShapeDtypeStruct((1, n), F32)]

        def swapped(spec):
            return pl.BlockSpec(spec.block_shape, functools.partial(lambda j, i, kk, f: f(i, j, kk), f=spec.index_map))
        in_specs = [swapped(sp) for sp in in_specs]
        out_spec = [swapped(sp) for sp in out_spec]
        grid = (gn, gm, gk)

    def body(*refs):
        a_ref, b_ref = refs[0], refs[1]
        pos = 2
        bias_ref = add_ref = hid_ref = cs_ref = None
        if bias is not None:
            bias_ref = refs[pos]
            pos += 1
        if add is not None:
            add_ref = refs[pos]
            pos += 1
        if relu2_of is not None:
            hid_ref = refs[pos]
            pos += 1
        o_ref = refs[pos]
        pos += 1
        if colsum:
            cs_ref = refs[pos]
            pos += 1
        acc_ref = refs[pos] if gk > 1 else None
        bt = b_ref[...]
        p = _dot(a_ref[...].astype(BF16), bt.astype(BF16), ca, cb)
        kk = pl.program_id(2)

        if colsum:
            @pl.when(jnp.logical_and(pl.program_id(1) == 0, kk == 0))
            def _():
                cs_ref[...] = _colsum(bt.astype(F32))

            @pl.when(jnp.logical_and(pl.program_id(1) == 0, kk > 0))
            def _():
                cs_ref[...] += _colsum(bt.astype(F32))

        def finish(r):
            if bias_ref is not None:
                r = r + bias_ref[...]
            if act == 'relu2':
                r = jnp.square(jnp.maximum(r, 0.0))
            if hid_ref is not None:
                r = r * (2.0 * jnp.sqrt(hid_ref[...].astype(F32)))
            if add_ref is not None:
                r = r + add_scale * add_ref[...]
            o_ref[...] = r.astype(o_ref.dtype)

        if gk == 1:
            finish(p)
        else:
            @pl.when(kk == 0)
            def _():
                acc_ref[...] = p

            @pl.when(kk > 0)
            def _():
                acc_ref[...] += p

            @pl.when(kk == gk - 1)
            def _():
                finish(acc_ref[...])

    scratch = [pltpu.VMEM((tm, tn), F32)] if gk > 1 else []
    sem = ("arbitrary",) * 3 if colsum else ("parallel", "parallel", "arbitrary")
    return pl.pallas_call(body, name=name, grid=grid, in_specs=in_specs, out_specs=out_spec,
                          out_shape=out_shape, scratch_shapes=scratch, compiler_params=_params(sem))(*args)


def ln_fwd(name, h, mix, g, b):
    def fn(i, tm, rows, bcs):
        pre = ALPHA * rows[0] + rows[1]
        mu = jnp.mean(pre, axis=1, keepdims=True)
        xc = pre - mu
        var = jnp.mean(xc * xc, axis=1, keepdims=True)
        rstd = lax.rsqrt(var + LN_EPS)
        xhat = xc * rstd
        return (xhat * bcs[0] + bcs[1], xhat, rstd), ()
    return rowwise(name, fn, [h, mix], [g, b], [D_MODEL, D_MODEL, 1])


def ln_bwd(name, dy, xhat, rstd, g):
    def fn(i, tm, rows, bcs):
        dy_, xh, rs = rows
        dyg = dy_ * bcs[0]
        m1 = jnp.mean(dyg, axis=1, keepdims=True)
        m2 = jnp.mean(dyg * xh, axis=1, keepdims=True)
        dpre = rs * (dyg - m1 - xh * m2)
        return (dpre,), (_colsum(dy_ * xh), _colsum(dy_), _colsum(dpre))
    return rowwise(name, fn, [dy, xhat, rstd], [g], [D_MODEL], [D_MODEL] * 3)


_GELU_C = math.sqrt(2.0 / math.pi)


def gelu_fwd(name, x):
    def fn(i, tm, rows, bcs):
        v = rows[0]
        u = _GELU_C * (v + 0.044715 * (v * v * v))
        return (0.5 * v * (1.0 + jnp.tanh(u)),), ()
    return rowwise(name, fn, [x], [], [x.shape[1]])[0]


def gelu_bwd(name, dy, x):
    def fn(i, tm, rows, bcs):
        v = rows[1]
        th = jnp.tanh(_GELU_C * (v + 0.044715 * (v * v * v)))
        dg = 0.5 * (1.0 + th) + 0.5 * v * (1.0 - th * th) * (_GELU_C * (1.0 + 3.0 * 0.044715 * v * v))
        return (rows[0] * dg,), ()
    return rowwise(name, fn, [dy, x], [], [x.shape[1]])[0]


def glu_gate(name, y, gp):
    def fn(i, tm, rows, bcs):
        return (rows[0] * jax.nn.sigmoid(rows[1]),), ()
    return rowwise(name, fn, [y, gp], [], [y.shape[1]], out_dtypes=[BF16])[0]


def glu_gate_bwd(name, da, y, gp):
    def fn(i, tm, rows, bcs):
        s = jax.nn.sigmoid(rows[2])
        dgp = rows[0] * rows[1] * s * (1.0 - s)
        return (dgp, rows[0] * s), (_colsum(dgp),)
    w = y.shape[1]
    return rowwise(name, fn, [da, y, gp], [], [w, w], [w])


def loss_head(name, h, tgt, lp, seq):
    def fn(i, tm, rows, bcs):
        pos = (i * tm + lax.broadcasted_iota(jnp.int32, (tm, 1), 0)) % lp
        valid = jnp.logical_and(pos >= N_META, pos < N_META + seq)
        err = jnp.where(valid, rows[0] - rows[1], 0.0)
        part = 0.5 * jnp.sum(jnp.mean(err * err, axis=1, keepdims=True), axis=0, keepdims=True)
        return (err * (1.0 / D_MODEL),), (jnp.broadcast_to(part, (1, LANES)),)
    return rowwise(name, fn, [h, tgt], [], [D_MODEL], [LANES])


def adamw(name, w, gs, m, v):
    ng = len(gs)

    def fn(i, tm, rows, bcs):
        w_ = rows[0]
        g = rows[1] if ng == 1 else rows[1] + rows[2]
        m_, v_ = rows[1 + ng], rows[2 + ng]
        m_ = ADAM_B1 * m_ + (1.0 - ADAM_B1) * g
        v_ = ADAM_B2 * v_ + (1.0 - ADAM_B2) * jnp.square(g)
        m_hat = m_ / (1.0 - ADAM_B1 ** ADAM_STEP)
        v_hat = v_ / (1.0 - ADAM_B2 ** ADAM_STEP)
        delta = -ADAM_LR * (m_hat / (jnp.sqrt(v_hat) + ADAM_EPS) + ADAM_WD * w_)
        return (g, delta, m_, v_), ()
    wd = w.shape[1]
    return rowwise(name, fn, [w] + list(gs) + [m, v], [], [wd] * 4)


def s5_matrices(lam_re, lam_im, log_dt, b_re, b_im, c_re, c_im, d):
    c, hh, gg, pp = S5_CHUNK, S5_GROUP, S5_GROUPS, S5_STATE
    dt = jnp.exp(log_dt)[:, None]
    tau = jnp.arange(c + 1, dtype=F32)[None, :, None]
    mag = jnp.exp(tau * (lam_re * dt)[:, None, :])
    ang = tau * (lam_im * dt)[:, None, :]
    pw_re, pw_im = mag * jnp.cos(ang), mag * jnp.sin(ang)
    nr, ni = pw_re[:, 1] - 1.0, pw_im[:, 1]
    den = lam_re * lam_re + lam_im * lam_im
    cf_re = (nr * lam_re + ni * lam_im) / den
    cf_im = (ni * lam_re - nr * lam_im) / den
    bb_re = cf_re[:, :, None] * b_re - cf_im[:, :, None] * b_im
    bb_im = cf_re[:, :, None] * b_im + cf_im[:, :, None] * b_re
    ct_re, ct_im = c_re.transpose(0, 2, 1)[:, :, None, :], c_im.transpose(0, 2, 1)[:, :, None, :]
    pt_re, pt_im = pw_re.transpose(0, 2, 1)[:, :, :, None], pw_im.transpose(0, 2, 1)[:, :, :, None]
    cp_re = ct_re * pt_re - ct_im * pt_im
    cp_im = ct_re * pt_im + ct_im * pt_re
    hp = lax.Precision.HIGHEST
    kmat = (jnp.einsum('gpk,gpn->gkn', bb_re, cp_re[:, :, :c].reshape(gg, pp, c * hh), precision=hp)
            - jnp.einsum('gpk,gpn->gkn', bb_im, cp_im[:, :, :c].reshape(gg, pp, c * hh), precision=hp))
    rows = [jnp.pad(kmat[:, :, :(c - j) * hh], ((0, 0), (0, 0), (j * hh, 0))) for j in range(c)]
    a1 = jnp.stack(rows, axis=1).reshape(gg, c * hh, c * hh)
    a1 = a1 + jnp.eye(c * hh, dtype=F32)[None] * jnp.tile(d, (1, c))[:, None, :]
    a2 = jnp.concatenate([cp_re[:, :, 1:].reshape(gg, pp, c * hh), -cp_im[:, :, 1:].reshape(gg, pp, c * hh)], axis=1)
    rv_re, rv_im = pw_re[:, :c][:, ::-1, None, :], pw_im[:, :c][:, ::-1, None, :]
    bt_re, bt_im = bb_re.transpose(0, 2, 1)[:, None], bb_im.transpose(0, 2, 1)[:, None]
    a3 = jnp.concatenate([rv_re * bt_re - rv_im * bt_im, rv_re * bt_im + rv_im * bt_re], axis=-1)
    a3 = a3.reshape(gg, c * hh, 2 * pp)
    are = jnp.concatenate([pw_re[:, c], pw_re[:, c]], axis=-1)[:, None, :]
    aim = jnp.concatenate([-pw_im[:, c], pw_im[:, c]], axis=-1)[:, None, :]
    return a1, a2, a3, are, aim


S5_LEVELS = 8


def s5_scan_powers(lam_re, lam_im, log_dt):
    dt = jnp.exp(log_dt)[:, None]
    e = (S5_CHUNK * 2.0 ** jnp.arange(S5_LEVELS, dtype=F32))[:, None, None]
    mag = jnp.exp(e * (lam_re * dt)[None])
    ang = e * (lam_im * dt)[None]
    pr, pi = mag * jnp.cos(ang), mag * jnp.sin(ang)
    pre = jnp.concatenate([pr, pr], axis=-1).transpose(1, 0, 2)
    pim = jnp.concatenate([-pi, pi], axis=-1).transpose(1, 0, 2)
    return pre, pim


def _s5_scan(x, pre, pim, reverse):
    n = x.shape[0]
    rowi = lax.broadcasted_iota(jnp.int32, (n, 1), 0)
    t = x
    for k in range(S5_LEVELS):
        shift = 2 ** k
        if shift >= n:
            break
        p_re, p_im = pre[k:k + 1, :], pim[k:k + 1, :]
        if reverse:
            far = jnp.where(rowi < n - shift, pltpu.roll(t, n - shift, 0), 0.0)
            t = t + (p_re * far + pltpu.roll(p_im * far, S5_STATE, 1))
        else:
            far = jnp.where(rowi >= shift, pltpu.roll(t, shift, 0), 0.0)
            t = t + (p_re * far + p_im * pltpu.roll(far, S5_STATE, 1))
    return t


def s5_fwd(ug, a1, a2, a3, pre, pim, nseq):
    g, nc, cw = ug.shape
    per = nc // nseq
    sw = 2 * S5_STATE
    assert per <= 2 ** S5_LEVELS

    def body(u_ref, a1_ref, a2_ref, a3_ref, pre_ref, pim_ref, y_ref, s_ref):
        u = u_ref[...]
        x = _dot3(u, a3_ref[...], 1, 0)
        first = lax.broadcasted_iota(jnp.int32, (per, 1), 0) == 0
        for b in range(nseq):
            t = _s5_scan(x[b * per:(b + 1) * per], pre_ref[...], pim_ref[...], False)
            s_ref[pl.ds(b * per, per), :] = jnp.where(first, 0.0, pltpu.roll(t, 1, 0))
        y_ref[...] = _dot3(u, a1_ref[...], 1, 0) + _dot3(s_ref[...], a2_ref[...], 1, 0)

    def spec(shape):
        return pl.BlockSpec((None,) + shape, lambda i: (i, 0, 0))
    lv = (S5_LEVELS, sw)
    return pl.pallas_call(
        body, name="s5_fwd", grid=(g,),
        in_specs=[spec((nc, cw)), spec((cw, cw)), spec((sw, cw)), spec((cw, sw)), spec(lv), spec(lv)],
        out_specs=[spec((nc, cw)), spec((nc, sw))],
        out_shape=[jax.ShapeDtypeStruct((g, nc, cw), F32), jax.ShapeDtypeStruct((g, nc, sw), F32)],
        compiler_params=_params(("parallel",)))(ug, a1, a2, a3, pre, pim)


def s5_bwd(dyg, ug, sst, a1, a2, a3, pre, pim, nseq, ex=None):
    g, nc, cw = ug.shape
    per = nc // nseq
    sw = 2 * S5_STATE
    split, ex_start, ex_finish, exkw = _hidden_exchange(ex, 8, 6, (g,))

    def body(*refs):
        ((dy_ref, u_ref, s_ref, a1_ref, a2_ref, a3_ref, pre_ref, pim_ref),
         (du_ref, da1_ref, da2_ref, da3_ref, dare_ref, daim_ref), ex_parts, (dx_ref,)) = split(refs)
        ex_start(ex_parts)
        dy = dy_ref[...]
        u = u_ref[...]
        gd = _dot3(dy, a2_ref[...], 1, 1)
        s_all = s_ref[...]
        da2_ref[...] = _dot3(s_all, dy, 0, 0)
        last = lax.broadcasted_iota(jnp.int32, (per, 1), 0) == per - 1
        for b in range(nseq):
            gs = _s5_scan(gd[b * per:(b + 1) * per], pre_ref[...], pim_ref[...], True)
            dx_ref[pl.ds(b * per, per), :] = jnp.where(last, 0.0, pltpu.roll(gs, per - 1, 0))
        dx = dx_ref[...]
        dare_ref[...] = _colsum(dx * s_all)
        daim_ref[...] = _colsum(dx * pltpu.roll(s_all, S5_STATE, 1))
        du_ref[...] = _dot3(dy, a1_ref[...], 1, 1) + _dot3(dx, a3_ref[...], 1, 1)
        da1_ref[...] = _dot3(u, dy, 0, 0)
        da3_ref[...] = _dot3(u, dx, 0, 0)
        ex_finish(ex_parts)

    def spec(shape):
        return pl.BlockSpec((None,) + shape, lambda i: (i, 0, 0))
    sds = jax.ShapeDtypeStruct
    return pl.pallas_call(
        body, name="s5_bwd", grid=(g,),
        in_specs=[spec((nc, cw)), spec((nc, cw)), spec((nc, sw)), spec((cw, cw)), spec((sw, cw)), spec((cw, sw)),
                  spec((S5_LEVELS, sw)), spec((S5_LEVELS, sw))] + exkw['in_specs'],
        out_specs=[spec((nc, cw)), spec((cw, cw)), spec((sw, cw)), spec((cw, sw)), spec((1, sw)), spec((1, sw))]
        + exkw['out_specs'],
        out_shape=[sds((g, nc, cw), F32), sds((g, cw, cw), F32), sds((g, sw, cw), F32), sds((g, cw, sw), F32),
                   sds((g, 1, sw), F32), sds((g, 1, sw), F32)] + exkw['out_shape'],
        input_output_aliases=exkw['aliases'], scratch_shapes=[pltpu.VMEM((nc, sw), F32)] + exkw['scratch'],
        compiler_params=_params(("arbitrary",)))(dyg, ug, sst, a1, a2, a3, pre, pim, *exkw['ins'])


def _to_groups(u):
    t = u.shape[0]
    nc = t // S5_CHUNK
    return u.reshape(nc, S5_CHUNK, S5_GROUPS, S5_GROUP).transpose(2, 0, 1, 3).reshape(S5_GROUPS, nc, -1)


def _from_groups(yg):
    g, nc, _ = yg.shape
    return yg.reshape(g, nc, S5_CHUNK, S5_GROUP).transpose(1, 2, 0, 3).reshape(nc * S5_CHUNK, g * S5_GROUP)


def _sb_masks():
    row = lax.broadcasted_iota(jnp.int32, (SB_BLOCK, SB_BLOCK), 0)
    col = lax.broadcasted_iota(jnp.int32, (SB_BLOCK, SB_BLOCK), 1)
    lane = lax.broadcasted_iota(jnp.int32, (1, LANES), 1)
    return row, col, [lane < 64, lane >= 64]


def _sb_weights(z, vis, later_of, after):
    sp = jnp.maximum(z, 0.0) + jnp.log1p(jnp.exp(-jnp.abs(z)))
    lk = jnp.where(vis, -sp, 0.0)
    rs = jnp.sum(lk, axis=1, keepdims=True)
    later = _dot_exact_rhs(lk, after, 1, 0) + later_of(rs)
    lb = z - sp
    w = jnp.where(vis, jnp.exp(lb + later), 0.0)
    return w, rs, lb


def _hidden_exchange(ex, n_in, n_out, grid):
    ni, no = (len(ex.ins), len(ex.out_shapes)) if ex else (0, 0)

    def split(refs):
        a, b = n_in + ni, n_in + ni + n_out
        end = len(refs) - (2 if ex else 0)
        return refs[:n_in], refs[a:b], (refs[n_in:a], refs[b:b + no], refs[end:]), refs[b + no:end]

    def at_step(which, fn, parts):
        if ex is not None:
            ids = [pl.program_id(d) == (0 if which == 'first' else g - 1) for d, g in enumerate(grid)]
            cond = ids[0]
            for c in ids[1:]:
                cond = jnp.logical_and(cond, c)
            pl.when(cond)(lambda: fn(parts[0], parts[1], *parts[2]))

    anyspec = pl.BlockSpec(memory_space=pl.ANY)
    kw = dict(in_specs=[anyspec] * ni, out_specs=[anyspec] * no, out_shape=list(ex.out_shapes) if ex else [],
              aliases={n_in + i: n_out + j for i, j in (ex.aliases().items() if ex else ())},
              scratch=ex.sems() if ex else [], ins=list(ex.ins) if ex else [])
    start = functools.partial(at_step, 'first', ex.start if ex else None)
    finish = functools.partial(at_step, 'last', ex.finish if ex else None)
    return split, start, finish, kw


def _lane_tile(x, tl):
    return x[:, tl * LANES:(tl + 1) * LANES]


SB_TILES = 2


def _two_heads(x):
    return jnp.concatenate([x, x], axis=0)


def _two_heads_masks():
    row = lax.broadcasted_iota(jnp.int32, (2 * SB_BLOCK, LANES), 0)
    col = lax.broadcasted_iota(jnp.int32, (2 * SB_BLOCK, LANES), 1)
    return (col // 64) == (row // SB_BLOCK), col < (row % SB_BLOCK)


def _own_lanes(r, lanes0):
    return jnp.where(lanes0, r[:SB_BLOCK], r[SB_BLOCK:])


def _sb_specs(lp, nseq):
    wd = SB_TILES * LANES
    off = [(S5_WIDTH + i * SB_WIDTH) // wd for i in range(3)]
    blk = (lp, wd)
    qkv = [pl.BlockSpec(blk, functools.partial(lambda b, h, o: (b, o + h), o=o)) for o in off]
    return (blk, qkv, pl.BlockSpec(blk, lambda b, h: (b, h)),
            pl.BlockSpec((None, None, 8, LANES), lambda b, h: (b, h, 0, 0)))


def attn_fwd(proj, nseq, lp, ex=None):
    nqb = lp // SB_BLOCK
    t = proj.shape[0]
    nstep = SB_WIDTH // (SB_TILES * LANES)
    split, ex_start, ex_finish, exkw = _hidden_exchange(ex, 3, 3, (nseq, nstep))

    def body(*refs):
        (q_ref, k_ref, v_ref), (o_ref, tot_ref, first_ref), ex_parts, _ = split(refs)
        ex_start(ex_parts)
        row, col, hmask = _sb_masks()
        after = (row > col).astype(BF16)
        tri = col < row
        lane8 = lax.broadcasted_iota(jnp.int32, (8, LANES), 1)

        hm2, tri2 = _two_heads_masks()

        def qloop(qi, firsts):
            q0 = pl.multiple_of(qi * SB_BLOCK, SB_BLOCK)
            q = q_ref[pl.ds(q0, SB_BLOCK), :] * 0.125
            qm = [jnp.where(hm2, _two_heads(_lane_tile(q, tl)), 0.0).astype(BF16) for tl in range(SB_TILES)]

            def alive(carry):
                return jnp.logical_and(carry[0] <= qi, carry[1] > 0)

            def kstep(carry):
                s = carry[0]
                accs, lats = list(carry[2:2 + SB_TILES]), list(carry[2 + SB_TILES:])
                kj = qi - s
                k0 = pl.multiple_of(kj * SB_BLOCK, SB_BLOCK)
                k = k_ref[pl.ds(k0, SB_BLOCK), :].astype(BF16)
                v = v_ref[pl.ds(k0, SB_BLOCK), :].astype(BF16)
                vis = jnp.logical_or(kj < qi, tri2)
                for tl in range(SB_TILES):
                    z = _dot(qm[tl], _lane_tile(k, tl), 1, 1)
                    w, rs, _ = _sb_weights(z, vis, functools.partial(lambda rs, lat: lat, lat=lats[tl]), after)
                    accs[tl] = accs[tl] + _own_lanes(_dot(w.astype(BF16), _lane_tile(v, tl), 1, 0), hmask[0])
                    lats[tl] = lats[tl] + rs
                top = functools.reduce(jnp.maximum, lats)
                go = (jnp.max(top) > SB_DEAD).astype(jnp.int32)
                return (s + 1, go, *accs, *lats)

            res = lax.while_loop(alive, kstep, (jnp.int32(0), jnp.int32(1),
                                                *[jnp.zeros((SB_BLOCK, LANES), F32)] * SB_TILES,
                                                *[jnp.zeros((2 * SB_BLOCK, 1), F32)] * SB_TILES))
            accs, lats = res[2:2 + SB_TILES], res[2 + SB_TILES:]
            for tl in range(SB_TILES):
                cols = slice(tl * LANES, (tl + 1) * LANES)
                o_ref[pl.ds(q0, SB_BLOCK), cols] = accs[tl].astype(o_ref.dtype)
                tot_ref[pl.ds(q0, SB_BLOCK), cols] = _own_lanes(jnp.broadcast_to(lats[tl], (2 * SB_BLOCK, LANES)),
                                                                hmask[0])
            return jnp.where(lane8 == qi, (qi - res[0] + 1).astype(F32), firsts)

        first_ref[...] = lax.fori_loop(0, nqb, qloop, jnp.zeros((8, LANES), F32))
        ex_finish(ex_parts)

    blk, qkv, out, vec = _sb_specs(lp, nseq)
    return pl.pallas_call(
        body, name="attn_fwd", grid=(nseq, nstep),
        in_specs=qkv + exkw['in_specs'], out_specs=[out, out, vec] + exkw['out_specs'],
        out_shape=[jax.ShapeDtypeStruct((t, SB_WIDTH), BF16), jax.ShapeDtypeStruct((t, SB_WIDTH), F32)]
        + [jax.ShapeDtypeStruct((nseq, nstep, 8, LANES), F32)] + exkw['out_shape'],
        input_output_aliases=exkw['aliases'], scratch_shapes=exkw['scratch'],
        compiler_params=_params(("arbitrary", "arbitrary")))(proj, proj, proj, *exkw['ins'])


def attn_bwd(proj, tot, first, do, nseq, lp, ex=None):
    nqb = lp // SB_BLOCK
    t = proj.shape[0]
    nstep = SB_WIDTH // (SB_TILES * LANES)
    split, ex_start, ex_finish, exkw = _hidden_exchange(ex, 6, 3, (nseq, nstep))

    def body(*refs):
        (q_ref, k_ref, v_ref, tot_ref, first_ref, do_ref), (dq_ref, dk_ref, dv_ref), ex_parts, _ = split(refs)
        ex_start(ex_parts)
        row, col, hmask = _sb_masks()
        after = (row > col).astype(BF16)
        before = (row < col).astype(BF16)
        tri = col < row
        lane8 = lax.broadcasted_iota(jnp.int32, (8, LANES), 1)
        firsts = first_ref[...]
        dk_ref[...] = jnp.zeros(dk_ref.shape, F32)
        dv_ref[...] = jnp.zeros(dv_ref.shape, F32)
        hm2, tri2 = _two_heads_masks()

        def qloop(qi, _):
            first = jnp.max(jnp.where(lane8 == qi, firsts, 0.0)).astype(jnp.int32)
            first = jnp.clip(first, 0, qi)
            q0 = pl.multiple_of(qi * SB_BLOCK, SB_BLOCK)
            q = q_ref[pl.ds(q0, SB_BLOCK), :] * 0.125
            do_ = do_ref[pl.ds(q0, SB_BLOCK), :]
            tot_ = tot_ref[pl.ds(q0, SB_BLOCK), :]
            qm = [jnp.where(hm2, _two_heads(_lane_tile(q, tl)), 0.0).astype(BF16) for tl in range(SB_TILES)]
            dom = [jnp.where(hm2, _two_heads(_lane_tile(do_, tl)), 0.0).astype(BF16) for tl in range(SB_TILES)]
            tot = [jnp.max(jnp.where(hm2, _two_heads(_lane_tile(tot_, tl)), -jnp.inf), axis=1, keepdims=True)
                   for tl in range(SB_TILES)]

            def kloop(kj, carry):
                dqs = list(carry[:SB_TILES])
                pre = list(carry[SB_TILES:2 * SB_TILES])
                gpre = list(carry[2 * SB_TILES:])
                k0 = pl.multiple_of(kj * SB_BLOCK, SB_BLOCK)
                k = k_ref[pl.ds(k0, SB_BLOCK), :].astype(BF16)
                v = v_ref[pl.ds(k0, SB_BLOCK), :].astype(BF16)
                vis = jnp.logical_or(kj < qi, tri2)
                for tl in range(SB_TILES):
                    kt, vt = _lane_tile(k, tl), _lane_tile(v, tl)
                    z = _dot(qm[tl], kt, 1, 1)
                    later_of = functools.partial(lambda rs, t_, p_: t_ - p_ - rs, t_=tot[tl], p_=pre[tl])
                    w, rs, lb = _sb_weights(z, vis, later_of, after)
                    dw = _dot(dom[tl], vt, 1, 1)
                    g = dw * w
                    dlk = _dot_exact_rhs(g, before, 1, 0) + gpre[tl]
                    sig = jnp.exp(lb)
                    dz = jnp.where(vis, g * (1.0 - sig) - dlk * sig, 0.0).astype(BF16)
                    dqs[tl] = dqs[tl] + _own_lanes(_dot(dz, kt, 1, 0), hmask[0])
                    cols = slice(tl * LANES, (tl + 1) * LANES)
                    dk_ref[pl.ds(k0, SB_BLOCK), cols] += _dot(dz, qm[tl], 0, 0)
                    dv_ref[pl.ds(k0, SB_BLOCK), cols] += _dot(w.astype(BF16), dom[tl], 0, 0)
                    pre[tl] = pre[tl] + rs
                    gpre[tl] = gpre[tl] + jnp.sum(g, axis=1, keepdims=True)
                return (*dqs, *pre, *gpre)

            z1 = jnp.zeros((2 * SB_BLOCK, 1), F32)
            res = lax.fori_loop(first, qi + 1, kloop,
                                (*[jnp.zeros((SB_BLOCK, LANES), F32)] * SB_TILES, *[z1] * (2 * SB_TILES)))
            for tl in range(SB_TILES):
                dq_ref[pl.ds(q0, SB_BLOCK), tl * LANES:(tl + 1) * LANES] = res[tl] * 0.125
            return 0

        lax.fori_loop(0, nqb, qloop, 0)
        ex_finish(ex_parts)

    blk, qkv, out, vec = _sb_specs(lp, nseq)
    return pl.pallas_call(
        body, name="attn_bwd", grid=(nseq, nstep),
        in_specs=qkv + [out, vec, out] + exkw['in_specs'], out_specs=[out] * 3 + exkw['out_specs'],
        out_shape=[jax.ShapeDtypeStruct((t, SB_WIDTH), F32)] * 3 + exkw['out_shape'],
        input_output_aliases=exkw['aliases'], scratch_shapes=exkw['scratch'],
        compiler_params=_params(("arbitrary", "arbitrary")))(proj, proj, proj, tot, first, do, *exkw['ins'])


def _hg_loop(nch, step, init):
    assert nch % HG_UNROLL == 0

    def trip(i, carry):
        for u in range(HG_UNROLL):
            carry = step(i * HG_UNROLL + u, carry)
        return carry
    return lax.fori_loop(0, nch // HG_UNROLL, trip, init)


def _hg_gates(fc, lb):
    sg = jax.nn.sigmoid(fc)
    f = lb + (1.0 - lb) * sg
    return sg, f


def _hg_decay(f, incl):
    bcum = _dot_exact_lhs(incl, jnp.log(f))
    return bcum, bcum[HG_CHUNK - 1:HG_CHUNK, :]


def _dot_exact_lhs(m, x):
    hi, mid, lo = _split3(x)
    return _dot(m, hi, 1, 0) + (_dot(m, mid, 1, 0) + _dot(m, lo, 1, 0))


def hgrn_fwd(proj, lb, ng, nseq, lp):
    nch = lp // HG_CHUNK
    t = proj.shape[0]
    c = HG_CHUNK

    def body(q_ref, f_ref, v_ref, g_ref, lb_ref, ng_ref, o_ref, og_ref, st_ref):
        row = lax.broadcasted_iota(jnp.int32, (c, c), 0)
        col = lax.broadcasted_iota(jnp.int32, (c, c), 1)
        causal = col <= row
        incl = causal.astype(BF16)
        lbv, ngv = lb_ref[...], ng_ref[...]

        def chunk(b, ci, st):
            rows = pl.ds(pl.multiple_of(b * lp + ci * c, c), c)
            st_ref[b, ci] = st
            _, f = _hg_gates(f_ref[rows, :], lbv)
            bcum, blast = _hg_decay(f, incl)
            kk = 1.0 - f
            vc = v_ref[rows, :].astype(BF16)
            qd = (q_ref[rows, :] * jnp.exp(bcum)).astype(BF16)
            kd = (kk * jnp.exp(-bcum)).astype(BF16)
            a = jnp.where(causal, _dot(qd, kd, 1, 1), 0.0)
            o = _dot(a.astype(BF16), vc, 1, 0) + _dot(qd, st.astype(BF16), 1, 1)
            kend = (kk * jnp.exp(blast - bcum)).astype(BF16)
            st = st * jnp.exp(blast) + _dot(vc, kend, 0, 0)
            o_ref[rows, :] = o
            r = lax.rsqrt(jnp.mean(o * o, axis=1, keepdims=True) + RMS_EPS)
            gc = g_ref[rows, :]
            og_ref[rows, :] = (o * r * ngv * (gc * jax.nn.sigmoid(gc))).astype(og_ref.dtype)
            return st

        def step(ci, sts):
            return tuple(chunk(b, ci, sts[b]) for b in range(nseq))

        _hg_loop(nch, step, tuple(jnp.zeros((HG_DK, HG_DK), F32) for _ in range(nseq)))

    blk = (nseq * lp, LANES)
    h = HG_HEADS
    return pl.pallas_call(
        body, name="hgrn_fwd", grid=(h,),
        in_specs=[pl.BlockSpec(blk, lambda hh: (0, hh)),
                  pl.BlockSpec(blk, lambda hh: (0, h + hh)),
                  pl.BlockSpec(blk, lambda hh: (0, 2 * h + hh)),
                  pl.BlockSpec(blk, lambda hh: (0, 3 * h + hh)),
                  pl.BlockSpec((1, LANES), lambda hh: (0, hh)),
                  pl.BlockSpec((1, LANES), lambda hh: (0, hh))],
        out_specs=[pl.BlockSpec(blk, lambda hh: (0, hh)),
                   pl.BlockSpec(blk, lambda hh: (0, hh)),
                   pl.BlockSpec((nseq, None, nch, HG_DK, HG_DK), lambda hh: (0, hh, 0, 0, 0))],
        out_shape=[jax.ShapeDtypeStruct((t, D_MODEL), F32), jax.ShapeDtypeStruct((t, D_MODEL), BF16),
                   jax.ShapeDtypeStruct((nseq, h, nch, HG_DK, HG_DK), F32)],
        compiler_params=_params(("parallel",)))(proj, proj, proj, proj, lb, ng)


def hgrn_bwd(proj, o, dog, states, lb, ng, nseq, lp):
    nch = lp // HG_CHUNK
    t = proj.shape[0]
    c = HG_CHUNK

    def body(q_ref, f_ref, v_ref, g_ref, o_ref, dog_ref, st_ref, lb_ref, ng_ref,
             dq_ref, df_ref, dv_ref, dg_ref, dlb_ref, dng_ref):
        row = lax.broadcasted_iota(jnp.int32, (c, c), 0)
        col = lax.broadcasted_iota(jnp.int32, (c, c), 1)
        causal = col <= row
        incl = causal.astype(BF16)
        from_here = (col >= row).astype(BF16)
        last = lax.broadcasted_iota(jnp.int32, (c, 1), 0) == c - 1
        lbv, ngv = lb_ref[...], ng_ref[...]

        def step(s, carry):
            dst, dlb, dng = carry
            ci = nch - 1 - s
            r0 = pl.multiple_of(ci * c, c)
            rows = pl.ds(r0, c)
            oc, gc, dogc = o_ref[rows, :], g_ref[rows, :], dog_ref[rows, :]
            r = lax.rsqrt(jnp.mean(oc * oc, axis=1, keepdims=True) + RMS_EPS)
            sgg = jax.nn.sigmoid(gc)
            silu = gc * sgg
            dg_ref[rows, :] = dogc * (oc * r * ngv) * (sgg * (1.0 + gc * (1.0 - sgg)))
            don = dogc * silu
            dng = dng + _colsum(don * oc * r)
            tt = don * ngv
            do = r * (tt - oc * (r * r) * jnp.mean(tt * oc, axis=1, keepdims=True))
            st = st_ref[ci]
            fc = f_ref[rows, :]
            sg, f = _hg_gates(fc, lbv)
            bcum, blast = _hg_decay(f, incl)
            kk = 1.0 - f
            qc, vf = q_ref[rows, :], v_ref[rows, :]
            pdec = jnp.exp(bcum)
            ndec = jnp.exp(-bcum)
            edec = jnp.exp(blast - bcum)
            eb = jnp.exp(blast)
            qd_f, kd_f, kend_f = qc * pdec, kk * ndec, kk * edec
            qd, kd, kend = qd_f.astype(BF16), kd_f.astype(BF16), kend_f.astype(BF16)
            vc, dob, stb, dstb = vf.astype(BF16), do.astype(BF16), st.astype(BF16), dst.astype(BF16)
            a = jnp.where(causal, _dot(qd, kd, 1, 1), 0.0).astype(BF16)
            da = jnp.where(causal, _dot(dob, vc, 1, 1), 0.0).astype(BF16)
            dv_ref[rows, :] = _dot(a, dob, 0, 0) + _dot(kend, dstb, 1, 1)
            dqd = _dot(da, kd, 1, 0) + _dot(dob, stb, 1, 0)
            dkd = _dot(da, qd, 0, 0)
            dkend = _dot(vc, dstb, 1, 0)
            dq_ref[rows, :] = dqd * pdec
            dblast = _colsum(dkend * kend_f) + eb * _colsum(dst * st)
            dbcum = dqd * qd_f - dkd * kd_f - dkend * kend_f
            dbcum = dbcum + jnp.where(last, dblast, 0.0)
            dlf = _dot_exact_lhs(from_here, dbcum)
            df = dlf / f - (dkd * ndec + dkend * edec)
            df_ref[rows, :] = df * (1.0 - lbv) * sg * (1.0 - sg)
            dlb = dlb + _colsum(df * (1.0 - sg))
            dst = dst * eb + _dot(dob, qd, 0, 0)
            return dst, dlb, dng

        z = jnp.zeros((1, LANES), F32)
        _, dlb, dng = _hg_loop(nch, step, (jnp.zeros((HG_DK, HG_DK), F32), z, z))
        dlb_ref[...] = dlb
        dng_ref[...] = dng

    blk = (lp, LANES)
    h = HG_HEADS
    vec = pl.BlockSpec((1, LANES), lambda b, hh: (0, hh))
    svec = pl.BlockSpec((None, 1, LANES), lambda b, hh: (b, 0, hh))
    return pl.pallas_call(
        body, name="hgrn_bwd", grid=(nseq, h),
        in_specs=[pl.BlockSpec(blk, lambda b, hh: (b, hh)),
                  pl.BlockSpec(blk, lambda b, hh: (b, h + hh)),
                  pl.BlockSpec(blk, lambda b, hh: (b, 2 * h + hh)),
                  pl.BlockSpec(blk, lambda b, hh: (b, 3 * h + hh)),
                  pl.BlockSpec(blk, lambda b, hh: (b, hh)),
                  pl.BlockSpec(blk, lambda b, hh: (b, hh)),
                  pl.BlockSpec((None, None, nch, HG_DK, HG_DK), lambda b, hh: (b, hh, 0, 0, 0)),
                  vec, vec],
        out_specs=[pl.BlockSpec(blk, lambda b, hh: (b, hh))] * 4 + [svec, svec],
        out_shape=[jax.ShapeDtypeStruct((t, D_MODEL), F32)] * 4
        + [jax.ShapeDtypeStruct((nseq, 1, D_MODEL), F32)] * 2,
        compiler_params=_params(("parallel", "parallel")))(proj, proj, proj, proj, o, dog, states, lb, ng)


def _lower_bound(gamma):
    p = jax.nn.softmax(gamma, axis=0)
    return (jnp.cumsum(p, axis=0) - p[0])[1][None, :]


def local_step(x, tgt, w, hooks=None):
    nseq, seq, _ = x.shape
    lp = -(-(N_META + seq) // SB_BLOCK) * SB_BLOCK
    t = nseq * lp
    pad = lp - N_META - seq
    h0 = jnp.concatenate([jnp.broadcast_to(w['meta'][None], (nseq, N_META, D_MODEL)), x,
                          jnp.zeros((nseq, pad, D_MODEL), F32)], axis=1).reshape(t, D_MODEL)
    tgt_p = jnp.pad(tgt, ((0, 0), (N_META, pad), (0, 0))).reshape(t, D_MODEL)
    row = lambda v: v.reshape(1, -1)
    g = {}

    proj = matmul("in_ab", h0, w['w_in_ab'], 'nn')
    att = attn_fwd(proj, nseq, lp, ex=hooks.gather_late if hooks else None)
    b_out, sb_tot, sb_first = att[:3]
    if hooks:
        w = {**w, **hooks.late_weights(att[3:])}
    s5_names = ['s5_lam_re', 's5_lam_im', 's5_log_dt', 's5_b_re', 's5_b_im', 's5_c_re', 's5_c_im', 's5_d']
    mats, mats_vjp = jax.vjp(s5_matrices, *[w[n][0] for n in s5_names])
    ug = _to_groups(proj[:, :S5_WIDTH])
    s5_pows = s5_scan_powers(*[w[n][0] for n in s5_names[:3]])
    yg, sst = s5_fwd(ug, *mats[:3], *s5_pows, nseq)
    ys5 = _from_groups(yg)
    y = gelu_fwd("gelu", ys5)
    gp = matmul("glu", y, w['s5_w_glu'], 'nn', bias=w['s5_b_glu'])
    a_out = glu_gate("glu_gate", y, gp)
    cat = jnp.concatenate([a_out, b_out], axis=1)
    mix0 = matmul("out_ab", cat, w['w_out_ab'], 'nn')
    h1, xh1, rs1 = ln_fwd("ln_mix0", h0, mix0, row(w['ln_mix_g'][0]), row(w['ln_mix_b'][0]))
    hid0 = matmul("up0", h1, w['mlp_w_up'][0], 'nn', bias=row(w['mlp_b_up'][0]), act='relu2', out_dtype=BF16)
    dn0 = matmul("down0", hid0, w['mlp_w_down'][0], 'nn', bias=row(w['mlp_b_down'][0]))
    h2, xh2, rs2 = ln_fwd("ln_mlp0", h1, dn0, row(w['ln_mlp_g'][0]), row(w['ln_mlp_b'][0]))

    projc = matmul("in_c", h2, w['w_in_c'], 'nn')
    lb, lb_vjp = jax.vjp(_lower_bound, w['hgrn_gamma'])
    ng = w['hgrn_norm_g']
    o, og, states = hgrn_fwd(projc, lb, ng, nseq, lp)
    mix1 = matmul("out_c", og, w['w_out_c'], 'nn')
    h3, xh3, rs3 = ln_fwd("ln_mix1", h2, mix1, row(w['ln_mix_g'][1]), row(w['ln_mix_b'][1]))
    hid1 = matmul("up1", h3, w['mlp_w_up'][1], 'nn', bias=row(w['mlp_b_up'][1]), act='relu2', out_dtype=BF16)
    dn1 = matmul("down1", hid1, w['mlp_w_down'][1], 'nn', bias=row(w['mlp_b_down'][1]))
    h4, xh4, rs4 = ln_fwd("ln_mlp1", h3, dn1, row(w['ln_mlp_g'][1]), row(w['ln_mlp_b'][1]))
    dy, loss = loss_head("loss", h4, tgt_p, lp, seq)

    def mlp_bwd(l, dh_out, xh_out, rs_out, hid, h_in):
        dpre, dg_, db_, dbd = ln_bwd(f"ln_mlp{l}_b", dh_out, xh_out, rs_out, row(w['ln_mlp_g'][l]))
        dpu = matmul(f"down{l}_dx", dpre, w['mlp_w_down'][l], 'nt', relu2_of=hid, out_dtype=BF16)
        dwd = matmul(f"down{l}_dw", hid, dpre, 'tn')
        dh_in = matmul(f"up{l}_dx", dpu, w['mlp_w_up'][l], 'nt', add=dpre, add_scale=ALPHA)
        dwu, dbu = matmul(f"up{l}_dw", h_in, dpu, 'tn', out_slots=N_CHIPS, colsum=True)
        return dh_in, dict(ln_mlp_g=dg_, ln_mlp_b=db_, mlp_b_down=dbd, mlp_b_up=dbu, mlp_w_up=dwu, mlp_w_down=dwd)

    dh3, gm1 = mlp_bwd(1, dy, xh4, rs4, hid1, h3)
    dpre, dg1, db1, _ = ln_bwd("ln_mix1_b", dh3, xh3, rs3, row(w['ln_mix_g'][1]))
    g['w_out_c'] = matmul("out_c_dw", og, dpre, 'tn')
    dog = matmul("out_c_dx", dpre, w['w_out_c'], 'nt')
    dq, df, di, dgg, dlb, dng = hgrn_bwd(projc, o, dog, states, lb, ng, nseq, lp)
    dprojc = jnp.concatenate([dq, df, di, dgg], axis=1)
    dh2 = matmul("in_c_dx", dprojc, w['w_in_c'], 'nt', add=dpre, add_scale=ALPHA)
    g['w_in_c'] = matmul("in_c_dw", h2, dprojc, 'tn', out_slots=N_CHIPS)
    g['hgrn_gamma'] = lb_vjp(jnp.sum(dlb, axis=0))[0]
    g['hgrn_norm_g'] = jnp.sum(dng, axis=0)

    dh1, gm0 = mlp_bwd(0, dh2, xh2, rs2, hid0, h1)
    dpre, dg0, db0, _ = ln_bwd("ln_mix0_b", dh1, xh1, rs1, row(w['ln_mix_g'][0]))
    g['w_out_ab'] = matmul("out_ab_dw", cat, dpre, 'tn')
    dcat = matmul("out_ab_dx", dpre, w['w_out_ab'], 'nt')
    dgp, da_s, dbglu = glu_gate_bwd("glu_gate_b", dcat[:, :S5_WIDTH], y, gp)
    g['s5_w_glu'] = matmul("glu_dw", y, dgp, 'tn')
    g['s5_b_glu'] = dbglu
    dy5 = matmul("glu_dx", dgp, w['s5_w_glu'], 'nt', add=da_s)
    dys5 = gelu_bwd("gelu_b", dy5, ys5)
    for n in ('mlp_w_up', 'mlp_w_down'):
        g[n] = [gm0[n], gm1[n]]
    g['ln_mix_g'] = jnp.concatenate([dg0, dg1], axis=0)
    g['ln_mix_b'] = jnp.concatenate([db0, db1], axis=0)
    for n in ('ln_mlp_g', 'ln_mlp_b', 'mlp_b_down', 'mlp_b_up'):
        g[n] = jnp.concatenate([gm0[n], gm1[n]], axis=0)
    res = s5_bwd(_to_groups(dys5), ug, sst, *mats[:3], *s5_pows, nseq, ex=hooks.fold_early(g) if hooks else None)
    dug, da1, da2, da3, dare, daim = res[:6]
    for n, v in zip(s5_names, mats_vjp((da1, da2, da3, dare, daim))):
        g[n] = v[None]
    ex = hooks.scatter_early(g, loss, res[6:]) if hooks else None
    res = attn_bwd(proj, sb_tot, sb_first, dcat[:, S5_WIDTH:], nseq, lp, ex=ex)
    dqa, dka, dva = res[:3]
    if hooks:
        hooks.scattered(res[3:])
    dproj = jnp.concatenate([_from_groups(dug), dqa, dka, dva], axis=1)
    dh0 = matmul("in_ab_dx", dproj, w['w_in_ab'], 'nt', add=dpre, add_scale=ALPHA)
    g['w_in_ab'] = matmul("in_ab_dw", h0, dproj, 'tn', out_slots=N_CHIPS)

    dh0 = dh0.reshape(nseq, lp, D_MODEL)
    grad_x = dh0[:, N_META:N_META + seq]
    g['meta'] = jnp.sum(dh0[:, :N_META], axis=0)
    return loss, grad_x, g


class Exchange:
    def __init__(self, ins, out_shapes, n_remote, build, in_place):
        self.ins, self.out_shapes, self.n_remote, self.build, self.in_place = ins, out_shapes, n_remote, build, in_place

    def sems(self):
        return [pltpu.SemaphoreType.DMA((self.n_remote,)), pltpu.SemaphoreType.DMA((self.n_remote,))]

    def aliases(self):
        return self.in_place if isinstance(self.in_place, dict) else \
            {i: i for i in range(len(self.ins) if self.in_place else 0)}

    def beside(self, other):
        na, nao = len(self.ins), len(self.out_shapes)

        def build(in_refs, out_refs, me):
            pa = self.build(in_refs[:na], out_refs[:nao], me)
            pb = other.build(in_refs[na:], out_refs[nao:], me)
            n = max(len(pa), len(pb))
            return [(pa[i] if i < len(pa) else []) + (pb[i] if i < len(pb) else []) for i in range(n)]
        al = dict(self.aliases())
        al.update({na + i: nao + j for i, j in other.aliases().items()})
        return Exchange(list(self.ins) + list(other.ins), list(self.out_shapes) + list(other.out_shapes),
                        self.n_remote + other.n_remote, build, al)

    def phases(self, in_refs, out_refs, ssem, rsem):
        me = (lax.axis_index("x"), lax.axis_index("y"), lax.axis_index("c"))
        out, k = [], 0
        for phase in self.build(in_refs, out_refs, me):
            out.append(functools.partial(
                lambda phase, k: [pltpu.make_async_remote_copy(src_ref=s, dst_ref=d, send_sem=ssem.at[k + i],
                                                               recv_sem=rsem.at[k + i], device_id=p,
                                                               device_id_type=MESH)
                                  for i, (s, d, p) in enumerate(phase)], phase, k))
            k += len(phase)
        return out

    def start(self, in_refs, out_refs, ssem, rsem):
        for cp in self.phases(in_refs, out_refs, ssem, rsem)[0]():
            cp.start()

    def finish(self, in_refs, out_refs, ssem, rsem):
        phases = self.phases(in_refs, out_refs, ssem, rsem)
        for cp in phases[0]():
            cp.wait()
        for make in phases[1:]:
            copies = make()
            for cp in copies:
                cp.start()
            for cp in copies:
                cp.wait()


def _exchange(name, ex):
    ni, no = len(ex.ins), len(ex.out_shapes)

    def body(*refs):
        in_refs, out_refs, sems = refs[:ni], refs[ni:ni + no], refs[ni + no:]
        ex.start(in_refs, out_refs, *sems)
        ex.finish(in_refs, out_refs, *sems)

    anyspec = pl.BlockSpec(memory_space=pl.ANY)
    return pl.pallas_call(
        body, name=name, in_specs=[anyspec] * ni, out_specs=[anyspec] * no, out_shape=ex.out_shapes,
        input_output_aliases=ex.aliases(), scratch_shapes=ex.sems())(*ex.ins)


def _chip_peers(me):
    x, y, c = me
    return [(x ^ (m >> 1), y ^ (m & 1), c) for m in (1, 2, 3)]


def _half(ref, c, axis):
    h = ref.shape[axis] // 2
    idx = [slice(None)] * axis + [pl.ds(c * h, h)]
    return ref.at[tuple(idx)]


def _like(arrs):
    return [jax.ShapeDtypeStruct(a.shape, a.dtype) for a in arrs]


def gather_over_chips(bufs):
    def build(in_refs, out_refs, me):
        x, y, c = me
        j = 2 * x + y
        sib = (x, y, 1 - c)
        ici = [(_half(o.at[j], c, 0), _half(o.at[j], c, 0), p) for o in out_refs for p in _chip_peers(me)]
        d2d = [(_half(o.at[2 * p[0] + p[1]], c, 0), _half(o.at[2 * p[0] + p[1]], c, 0), sib)
               for o in out_refs for p in _chip_peers(me)]
        return [ici, d2d]
    return Exchange(bufs, _like(bufs), 6 * len(bufs), build, True)


def fold_cores(fulls):
    def build(in_refs, out_refs, me):
        x, y, c = me
        return [[(_half(s, 1 - c, 1), o, (x, y, 1 - c)) for s, o in zip(in_refs, out_refs)]]
    shapes = [jax.ShapeDtypeStruct((f.shape[0], f.shape[1] // 2, f.shape[2]), f.dtype) for f in fulls]
    return Exchange(fulls, shapes, len(fulls), build, False)


def scatter_over_chips(parts):
    def build(in_refs, out_refs, me):
        j = 2 * me[0] + me[1]
        return [[(s.at[2 * p[0] + p[1]], o.at[j], p) for s, o in zip(in_refs, out_refs) for p in _chip_peers(me)]]
    return Exchange(parts, _like(parts), 3 * len(parts), build, False)


def join_cores(bufs):
    def build(in_refs, out_refs, me):
        x, y, c = me
        return [[(o.at[c], o.at[c], (x, y, 1 - c)) for o in out_refs]]
    return Exchange(bufs, _like(bufs), len(bufs), build, True)


def gather_over_devices(buf):
    def build(in_refs, out_refs, me):
        x, y, c = me
        i = 4 * x + 2 * y + c
        out = out_refs[0]
        return [[(out.at[i], out.at[i], (x ^ (m >> 2), y ^ ((m >> 1) & 1), c ^ (m & 1))) for m in range(1, N_DEV)]]
    return Exchange([buf], _like([buf]), N_DEV - 1, build, True)


def _slot_call(name, body, scalars, ins, in_maps, out_shape, out_map, blk, grid):
    gs = pltpu.PrefetchScalarGridSpec(
        num_scalar_prefetch=1, grid=grid,
        in_specs=[pl.BlockSpec((None,) + blk, m) for m in in_maps],
        out_specs=pl.BlockSpec((None,) + blk, out_map))
    return pl.pallas_call(body, name=name, grid_spec=gs, out_shape=out_shape,
                          compiler_params=_params(("arbitrary",) * len(grid)))(scalars, *ins)


def cast_into_slot(name, w2d, chip, dtype):
    r, wd = w2d.shape
    tm = _pick(r, (512, 256, 128, 64, 32, 16))

    def body(s_ref, w_ref, o_ref):
        o_ref[...] = w_ref[...].astype(o_ref.dtype)

    gs = pltpu.PrefetchScalarGridSpec(
        num_scalar_prefetch=1, grid=(r // tm,),
        in_specs=[pl.BlockSpec((tm, wd), lambda i, s: (i, 0))],
        out_specs=pl.BlockSpec((None, tm, wd), lambda i, s: (s[0], i, 0)))
    return pl.pallas_call(body, name=name, grid_spec=gs,
                          out_shape=jax.ShapeDtypeStruct((N_CHIPS, r, wd), dtype),
                          compiler_params=_params(("arbitrary",)))(chip.reshape(1), w2d)


def sum_cores(name, full, theirs, core):
    s, r, wd = full.shape
    h = r // 2
    tm = _pick(h, (512, 256, 128, 64, 32, 16))
    nt = h // tm

    def body(s_ref, a_ref, b_ref, o_ref):
        o_ref[...] = (a_ref[...] + b_ref[...]).astype(o_ref.dtype)

    return _slot_call(name, body, core.reshape(1), [full, theirs],
                      [lambda k, i, s_: (k, s_[0] * nt + i, 0), lambda k, i, s_: (k, i, 0)],
                      jax.ShapeDtypeStruct((s, h, wd), BF16), lambda k, i, s_: (k, i, 0), (tm, wd), (s, nt))


def sum_chips(name, own, recv, chip, core):
    s, r, wd = own.shape
    tm = _pick(r, (512, 256, 128, 64, 32, 16))

    def body(s_ref, a_ref, b1_ref, b2_ref, b3_ref, o_ref):
        acc = a_ref[...].astype(F32)
        for ref in (b1_ref, b2_ref, b3_ref):
            acc = acc + ref[...].astype(F32)
        o_ref[...] = acc

    maps = [lambda i, s_: (s_[0], i, 0)]
    maps += [functools.partial(lambda i, s_, m: (s_[0] ^ m, i, 0), m=m) for m in (1, 2, 3)]
    return _slot_call(name, body, jnp.stack([chip, core]), [own, recv, recv, recv], maps,
                      jax.ShapeDtypeStruct((2, r, wd), F32), lambda i, s_: (s_[1], i, 0), (tm, wd), (r // tm,))


def sum_slots(name, arr):
    s, r, wd = arr.shape
    tm = _pick(r, (512, 256, 128, 64, 32, 16, 8))

    def body(*refs):
        acc = refs[0][...].astype(F32)
        for ref in refs[1:s]:
            acc = acc + ref[...].astype(F32)
        refs[s][...] = acc

    specs = [pl.BlockSpec((None, tm, wd), functools.partial(lambda i, k: (k, i, 0), k=k)) for k in range(s)]
    return pl.pallas_call(body, name=name, grid=(r // tm,), in_specs=specs,
                          out_specs=pl.BlockSpec((tm, wd), lambda i: (i, 0)),
                          out_shape=jax.ShapeDtypeStruct((r, wd), F32),
                          compiler_params=_params(("parallel",)))(*([arr] * s))


def _pack(arrs):
    flat = jnp.concatenate([a.reshape(-1) for a in arrs])
    n = flat.shape[0]
    padded = -(-n // (512 * LANES)) * (512 * LANES)
    return jnp.pad(flat, (0, padded - n)).reshape(-1, LANES)


def _unpack(packed, shapes):
    flat = packed.reshape(-1)
    out, pos = [], 0
    for s in shapes:
        n = math.prod(s)
        out.append(flat[pos:pos + n].reshape(s))
        pos += n
    return out


def _as2d(a):
    return a.reshape(-1, a.shape[-1])


def kernel(x, meta, w_in_ab, s5_lam_re, s5_lam_im, s5_log_dt, s5_b_re, s5_b_im, s5_c_re, s5_c_im, s5_d, s5_w_glu, s5_b_glu, w_out_ab, w_in_c, hgrn_gamma, hgrn_norm_g, w_out_c, ln_mix_g, ln_mix_b, mlp_w_up, mlp_b_up, mlp_w_down, mlp_b_down, ln_mlp_g, ln_mlp_b, loss_target, m_meta, m_w_in_ab, m_s5_lam_re, m_s5_lam_im, m_s5_log_dt, m_s5_b_re, m_s5_b_im, m_s5_c_re, m_s5_c_im, m_s5_d, m_s5_w_glu, m_s5_b_glu, m_w_out_ab, m_w_in_c, m_hgrn_gamma, m_hgrn_norm_g, m_w_out_c, m_ln_mix_g, m_ln_mix_b, m_mlp_w_up, m_mlp_b_up, m_mlp_w_down, m_mlp_b_down, m_ln_mlp_g, m_ln_mlp_b, v_meta, v_w_in_ab, v_s5_lam_re, v_s5_lam_im, v_s5_log_dt, v_s5_b_re, v_s5_b_im, v_s5_c_re, v_s5_c_im, v_s5_d, v_s5_w_glu, v_s5_b_glu, v_w_out_ab, v_w_in_c, v_hgrn_gamma, v_hgrn_norm_g, v_w_out_c, v_ln_mix_g, v_ln_mix_b, v_mlp_w_up, v_mlp_b_up, v_mlp_w_down, v_mlp_b_down, v_ln_mlp_g, v_ln_mlp_b):
    given = dict(locals())
    wts = {n: given[n] for n in WEIGHTS}
    ms = {n: given['m_' + n] for n in WEIGHTS}
    vs = {n: given['v_' + n] for n in WEIGHTS}
    chip = 2 * lax.axis_index("x") + lax.axis_index("y")

    core = lax.axis_index("c")
    parts = [(n, l) for n in BIG for l in (range(DEPTH) if n.startswith('mlp_') else [0])]
    tags = [f"{n}{l}" for n, l in parts]
    shards = {p: cast_into_slot("cast_" + t, wts[p[0]][p[1]], chip, BF16) for t, p in zip(tags, parts)}
    small_sh = jnp.concatenate([meta, hgrn_norm_g, jnp.zeros((15, meta.shape[1]), F32)], axis=0)
    first, late = parts[:1], parts[1:]
    w_in_ab_all, sm = _exchange("gather_first", gather_over_chips(
        [shards[p] for p in first] + [cast_into_slot("cast_small", small_sh, chip, F32)]))
    w = {n: wts[n] for n in SMALL}
    w['meta'] = sm[:, :N_META].transpose(1, 0, 2).reshape(N_META, D_MODEL)
    w['hgrn_norm_g'] = sm[:, N_META:N_META + 1].transpose(1, 0, 2).reshape(1, D_MODEL)
    w['w_in_ab'] = w_in_ab_all

    def slot_major(v):
        return v if v.ndim == 3 else v.reshape(N_CHIPS, -1, v.shape[-1])

    def whole_grads(ps, g):
        return [slot_major(g[n][l] if n.startswith('mlp_') else g[n]) for n, l in ps]

    def chip_sums_of(ps, gfull, theirs):
        return [sum_cores(f"sum_cores_{n}{l}", a, b, core) for (n, l), a, b in zip(ps, gfull, theirs)]

    def all_devices(arrs):
        packed = _pack(arrs)
        return gather_over_devices(lax.dynamic_update_slice(
            jnp.zeros((N_DEV,) + packed.shape, F32), packed[None], (2 * chip + core, 0, 0)))

    early_small = [n for n in SMALL if n != 'meta']

    class Hooks:
        gather_late = gather_over_chips([shards[p] for p in late])

        @staticmethod
        def late_weights(filled):
            fw = dict(zip(late, filled))
            return {'s5_w_glu': fw['s5_w_glu', 0].reshape(S5_WIDTH, S5_WIDTH),
                    'w_out_ab': fw['w_out_ab', 0].reshape(D_MODEL, D_MODEL),
                    'w_in_c': fw['w_in_c', 0],
                    'w_out_c': fw['w_out_c', 0].reshape(D_MODEL, D_MODEL),
                    'mlp_w_up': [fw['mlp_w_up', l] for l in range(DEPTH)],
                    'mlp_w_down': [fw['mlp_w_down', l].reshape(D_FF, D_MODEL) for l in range(DEPTH)]}

        @staticmethod
        def fold_early(g):
            Hooks.whole = whole_grads(late, g)
            return fold_cores(Hooks.whole)

        @staticmethod
        def scatter_early(g, loss, theirs):
            Hooks.sums = chip_sums_of(late, Hooks.whole, theirs)
            Hooks.small_shapes = [(LANES,)] + [g[n].shape for n in early_small]
            return scatter_over_chips(Hooks.sums).beside(
                all_devices([loss.reshape(-1)] + [g[n] for n in early_small]))

        @staticmethod
        def scattered(outs):
            Hooks.recv, Hooks.small = list(outs[:-1]), outs[-1]

    loss, grad_x, g = local_step(x, loss_target, w, Hooks)

    whole = whole_grads(first, g)
    sums = chip_sums_of(first, whole, _exchange("fold_last", fold_cores(whole)))
    *recv, meta_slots = _exchange("scatter_last", scatter_over_chips(sums).beside(all_devices([g['meta']])))
    reduced = _exchange("join_grads", join_cores(
        [sum_chips(f"sum_chips_{n}{l}", a, b, chip, core)
         for (n, l), a, b in zip(first + late, sums + Hooks.sums, list(recv) + Hooks.recv)]))
    reduced = dict(zip(first + late, [_as2d(r) for r in reduced]))
    red = _unpack(sum_slots("sum_small", Hooks.small), Hooks.small_shapes)
    loss_out = red[0][0]
    gs = dict(zip(early_small, red[1:]))
    gs['meta'] = _unpack(sum_slots("sum_meta", meta_slots), [g['meta'].shape])[0]
    gs['meta'] = lax.dynamic_slice_in_dim(gs['meta'], chip * meta.shape[1], meta.shape[1], axis=1)
    gs['hgrn_norm_g'] = lax.dynamic_slice_in_dim(gs['hgrn_norm_g'].reshape(1, -1), chip * meta.shape[1],
                                                 meta.shape[1], axis=1)

    grads, deltas, new_m, new_v = {}, {}, {}, {}
    for n in BIG:
        r = jnp.concatenate([reduced[n, l] for l in range(wts[n].shape[0])], axis=0)
        res = adamw("adamw_" + n, _as2d(wts[n]), [r], _as2d(ms[n]), _as2d(vs[n]))
        grads[n], deltas[n], new_m[n], new_v[n] = [r.reshape(wts[n].shape) for r in res]
    shapes = [wts[n].shape for n in SMALL]
    res = adamw("adamw_small", _pack([wts[n] for n in SMALL]), [_pack([gs[n] for n in SMALL])],
                _pack([ms[n] for n in SMALL]), _pack([vs[n] for n in SMALL]))
    for d, r in zip((grads, deltas, new_m, new_v), res):
        d.update(zip(SMALL, _unpack(r, shapes)))
    return (loss_out, grad_x, *[grads[n] for n in WEIGHTS], *[deltas[n] for n in WEIGHTS],
            *[new_m[n] for n in WEIGHTS], *[new_v[n] for n in WEIGHTS])
```

```python
import functools
import math

import jax
import jax.numpy as jnp
from jax import lax
from jax.experimental import pallas as pl
from jax.experimental.pallas import tpu as pltpu

F32 = jnp.float32
BF16 = jnp.bfloat16

D_MODEL = 1024
N_META = 16
DEPTH = 2
S5_WIDTH = 512
S5_GROUP = 16
S5_GROUPS = 32
S5_STATE = 64
S5_CHUNK = 16
SB_WIDTH = 512
SB_BLOCK = 128
SB_DEAD = -110.0
HG_HEADS = 8
HG_DK = 128
HG_CHUNK = 64
HG_UNROLL = 2
D_FF = 4096
ALPHA = (2.0 * DEPTH) ** 0.25
LN_EPS = 1e-5
RMS_EPS = 1e-6
ADAM_LR = 0.001
ADAM_B1 = 0.9
ADAM_B2 = 0.999
ADAM_EPS = 1e-08
ADAM_WD = 0.01
ADAM_STEP = 10
N_CHIPS = 4
N_DEV = 8
LANES = 128
MESH = pl.DeviceIdType.MESH
VMEM_LIMIT = 56 * 1024 * 1024
WHOLE_K_BYTES = 42 * 1024 * 1024

WEIGHTS = ['meta', 'w_in_ab', 's5_lam_re', 's5_lam_im', 's5_log_dt', 's5_b_re', 's5_b_im', 's5_c_re', 's5_c_im',
           's5_d', 's5_w_glu', 's5_b_glu', 'w_out_ab', 'w_in_c', 'hgrn_gamma', 'hgrn_norm_g', 'w_out_c',
           'ln_mix_g', 'ln_mix_b', 'mlp_w_up', 'mlp_b_up', 'mlp_w_down', 'mlp_b_down', 'ln_mlp_g', 'ln_mlp_b']
BIG = ['w_in_ab', 's5_w_glu', 'w_out_ab', 'w_in_c', 'w_out_c', 'mlp_w_up', 'mlp_w_down']
SMALL = [n for n in WEIGHTS if n not in BIG]


def _params(sem=None):
    return pltpu.CompilerParams(dimension_semantics=sem, vmem_limit_bytes=VMEM_LIMIT)


def _pick(n, cands):
    for c in cands:
        if n % c == 0:
            return c
    return n


def _dot(a, b, ca, cb):
    return lax.dot_general(a, b, (((ca,), (cb,)), ((), ())), preferred_element_type=F32)


def _split3(x):
    hi = x.astype(BF16)
    r = x - hi.astype(F32)
    mid = r.astype(BF16)
    lo = (r - mid.astype(F32)).astype(BF16)
    return hi, mid, lo


def _dot_exact_rhs(x, m, cx, cm):
    hi = x.astype(BF16)
    lo = (x - hi.astype(F32)).astype(BF16)
    return _dot(hi, m, cx, cm) + _dot(lo, m, cx, cm)


def _dot3(a, b, ca, cb):
    ah = a.astype(BF16)
    al = (a - ah.astype(F32)).astype(BF16)
    bh = b.astype(BF16)
    bl = (b - bh.astype(F32)).astype(BF16)
    return _dot(ah, bh, ca, cb) + (_dot(ah, bl, ca, cb) + _dot(al, bh, ca, cb))


def rowwise(name, fn, row_ins, bc_ins, out_widths, sum_widths=(), out_dtypes=None, tm=None):
    t = row_ins[0].shape[0]
    if tm is None:
        wmax = max([a.shape[1] for a in row_ins] + list(out_widths))
        tm = _pick(t, (272, 256, 128, 64, 16, 8)) if wmax > 1024 else _pick(t, (544, 512, 256, 128, 64, 16, 8))
    nr, nb, no, ns = len(row_ins), len(bc_ins), len(out_widths), len(sum_widths)
    out_dtypes = out_dtypes or [F32] * no

    def body(*refs):
        i = pl.program_id(0)
        rows = [r[...] for r in refs[:nr]]
        bcs = [r[...] for r in refs[nr:nr + nb]]
        outs, sums = fn(i, tm, rows, bcs)
        for r, v in zip(refs[nr + nb:nr + nb + no], outs):
            r[...] = v.astype(r.dtype)
        if ns:
            srefs = refs[nr + nb + no:]

            @pl.when(i == 0)
            def _():
                for r in srefs:
                    r[...] = jnp.zeros(r.shape, r.dtype)

            for r, v in zip(srefs, sums):
                r[...] += v

    in_specs = [pl.BlockSpec((tm, a.shape[1]), lambda i: (i, 0)) for a in row_ins]
    in_specs += [pl.BlockSpec(a.shape, lambda i: (0, 0)) for a in bc_ins]
    out_specs = [pl.BlockSpec((tm, w), lambda i: (i, 0)) for w in out_widths]
    out_specs += [pl.BlockSpec((1, w), lambda i: (0, 0)) for w in sum_widths]
    out_shape = [jax.ShapeDtypeStruct((t, w), dt) for w, dt in zip(out_widths, out_dtypes)]
    out_shape += [jax.ShapeDtypeStruct((1, w), F32) for w in sum_widths]
    res = pl.pallas_call(body, name=name, grid=(t // tm,), in_specs=in_specs, out_specs=out_specs,
                         out_shape=out_shape, compiler_params=_params(("arbitrary",)))(*row_ins, *bc_ins)
    return res


def _colsum(v):
    return jnp.sum(v, axis=0, keepdims=True)


def matmul(name, a, b, mode, *, bias=None, act=None, add=None, add_scale=1.0, relu2_of=None, colsum=False,
           out_dtype=F32, out_slots=0, tm=None, tn=None, tk=None):
    slotted = b.ndim == 3
    if mode == 'nn':
        m, k = a.shape
        n = b.shape[-1] * (b.shape[0] if slotted else 1)
    elif mode == 'nt':
        m, k = a.shape
        n = b.shape[-2]
    else:
        k, m = a.shape
        n = b.shape[-1]
    ns = b.shape[-1] if slotted else None
    if mode == 'tn':
        tm = tm or _pick(m, (512, 256, 128))
        nw = n if not out_slots else n // out_slots
        if tn is None and tk is None:
            for cand in (512, 256):
                if nw % cand == 0 and (2 * k * (tm * a.dtype.itemsize + cand * b.dtype.itemsize)
                                       + 2 * tm * cand * 4 <= WHOLE_K_BYTES):
                    tn, tk = cand, k
                    break
        tn = tn or _pick(nw, (1024, 512, 256, 128))
        tk = tk or _pick(k, (1088, 1024, 768, 512, 384, 256, 128))
    else:
        tm = tm or _pick(m, (1088, 1024, 768, 512, 384, 256, 128))
        tn = tn or _pick(n if not (slotted and mode == 'nn') else ns, (1024, 512, 256, 128))
        extra = sum(x is not None for x in (add, relu2_of)) * 4
        if tk is None and not slotted and (2 * (tm * k * a.dtype.itemsize + k * tn * b.dtype.itemsize)
                                           + 2 * tm * tn * (4 + extra) <= WHOLE_K_BYTES):
            tk = k
        tk = tk or _pick(k if not (slotted and mode == 'nt') else ns, (1024, 512, 256, 128))
    gm, gn, gk = m // tm, n // tn, k // tk

    if mode == 'nn':
        a_spec = pl.BlockSpec((tm, tk), lambda i, j, kk: (i, kk))
        if slotted:
            per = ns // tn
            b_spec = pl.BlockSpec((None, tk, tn), lambda i, j, kk: (j // per, kk, j % per))
        else:
            b_spec = pl.BlockSpec((tk, tn), lambda i, j, kk: (kk, j))
        ca, cb = 1, 0
    elif mode == 'nt':
        a_spec = pl.BlockSpec((tm, tk), lambda i, j, kk: (i, kk))
        if slotted:
            per = ns // tk
            b_spec = pl.BlockSpec((None, tn, tk), lambda i, j, kk: (kk // per, j, kk % per))
        else:
            b_spec = pl.BlockSpec((tn, tk), lambda i, j, kk: (j, kk))
        ca, cb = 1, 1
    else:
        a_spec = pl.BlockSpec((tk, tm), lambda i, j, kk: (kk, i))
        b_spec = pl.BlockSpec((tk, tn), lambda i, j, kk: (kk, j))
        ca, cb = 0, 0
    in_specs = [a_spec, b_spec]
    args = [a, b]
    if bias is not None:
        in_specs.append(pl.BlockSpec((1, tn), lambda i, j, kk: (0, j)))
        args.append(bias)
    if add is not None:
        in_specs.append(pl.BlockSpec((tm, tn), lambda i, j, kk: (i, j)))
        args.append(add)
    if relu2_of is not None:
        in_specs.append(pl.BlockSpec((tm, tn), lambda i, j, kk: (i, j)))
        args.append(relu2_of)
    if out_slots:
        per_o = (n // out_slots) // tn
        out_spec = pl.BlockSpec((None, tm, tn), lambda i, j, kk: (j // per_o, i, j % per_o))
        out_shape = jax.ShapeDtypeStruct((out_slots, m, n // out_slots), out_dtype)
    else:
        out_spec = pl.BlockSpec((tm, tn), lambda i, j, kk: (i, j))
        out_shape = jax.ShapeDtypeStruct((m, n), out_dtype)
    grid = (gm, gn, gk)
    if colsum:
        assert mode == 'tn'
        out_spec = [out_spec, pl.BlockSpec((1, tn), lambda i, j, kk: (0, j))]
        out_shape = [out_shape, jax.ShapeDtypeStruct((1, n), F32)]

        def swapped(spec):
            return pl.BlockSpec(spec.block_shape, functools.partial(lambda j, i, kk, f: f(i, j, kk), f=spec.index_map))
        in_specs = [swapped(sp) for sp in in_specs]
        out_spec = [swapped(sp) for sp in out_spec]
        grid = (gn, gm, gk)

    def body(*refs):
        a_ref, b_ref = refs[0], refs[1]
        pos = 2
        bias_ref = add_ref = hid_ref = cs_ref = None
        if bias is not None:
            bias_ref = refs[pos]
            pos += 1
        if add is not None:
            add_ref = refs[pos]
            pos += 1
        if relu2_of is not None:
            hid_ref = refs[pos]
            pos += 1
        o_ref = refs[pos]
        pos += 1
        if colsum:
            cs_ref = refs[pos]
            pos += 1
        acc_ref = refs[pos] if gk > 1 else None
        bt = b_ref[...]
        p = _dot(a_ref[...].astype(BF16), bt.astype(BF16), ca, cb)
        kk = pl.program_id(2)

        if colsum:
            @pl.when(jnp.logical_and(pl.program_id(1) == 0, kk == 0))
            def _():
                cs_ref[...] = _colsum(bt.astype(F32))

            @pl.when(jnp.logical_and(pl.program_id(1) == 0, kk > 0))
            def _():
                cs_ref[...] += _colsum(bt.astype(F32))

        def finish(r):
            if bias_ref is not None:
                r = r + bias_ref[...]
            if act == 'relu2':
                r = jnp.square(jnp.maximum(r, 0.0))
            if hid_ref is not None:
                r = r * (2.0 * jnp.sqrt(hid_ref[...].astype(F32)))
            if add_ref is not None:
                r = r + add_scale * add_ref[...]
            o_ref[...] = r.astype(o_ref.dtype)

        if gk == 1:
            finish(p)
        else:
            @pl.when(kk == 0)
            def _():
                acc_ref[...] = p

            @pl.when(kk > 0)
            def _():
                acc_ref[...] += p

            @pl.when(kk == gk - 1)
            def _():
                finish(acc_ref[...])

    scratch = [pltpu.VMEM((tm, tn), F32)] if gk > 1 else []
    sem = ("arbitrary",) * 3 if colsum else ("parallel", "parallel", "arbitrary")
    return pl.pallas_call(body, name=name, grid=grid, in_specs=in_specs, out_specs=out_spec,
                          out_shape=out_shape, scratch_shapes=scratch, compiler_params=_params(sem))(*args)


def ln_fwd(name, h, mix, g, b):
    def fn(i, tm, rows, bcs):
        pre = ALPHA * rows[0] + rows[1]
        mu = jnp.mean(pre, axis=1, keepdims=True)
        xc = pre - mu
        var = jnp.mean(xc * xc, axis=1, keepdims=True)
        rstd = lax.rsqrt(var + LN_EPS)
        xhat = xc * rstd
        return (xhat * bcs[0] + bcs[1], xhat, rstd), ()
    return rowwise(name, fn, [h, mix], [g, b], [D_MODEL, D_MODEL, 1])


def ln_bwd(name, dy, xhat, rstd, g):
    def fn(i, tm, rows, bcs):
        dy_, xh, rs = rows
        dyg = dy_ * bcs[0]
        m1 = jnp.mean(dyg, axis=1, keepdims=True)
        m2 = jnp.mean(dyg * xh, axis=1, keepdims=True)
        dpre = rs * (dyg - m1 - xh * m2)
        return (dpre,), (_colsum(dy_ * xh), _colsum(dy_), _colsum(dpre))
    return rowwise(name, fn, [dy, xhat, rstd], [g], [D_MODEL], [D_MODEL] * 3)


_GELU_C = math.sqrt(2.0 / math.pi)


def gelu_fwd(name, x):
    def fn(i, tm, rows, bcs):
        v = rows[0]
        u = _GELU_C * (v + 0.044715 * (v * v * v))
        return (0.5 * v * (1.0 + jnp.tanh(u)),), ()
    return rowwise(name, fn, [x], [], [x.shape[1]])[0]


def gelu_bwd(name, dy, x):
    def fn(i, tm, rows, bcs):
        v = rows[1]
        th = jnp.tanh(_GELU_C * (v + 0.044715 * (v * v * v)))
        dg = 0.5 * (1.0 + th) + 0.5 * v * (1.0 - th * th) * (_GELU_C * (1.0 + 3.0 * 0.044715 * v * v))
        return (rows[0] * dg,), ()
    return rowwise(name, fn, [dy, x], [], [x.shape[1]])[0]


def glu_gate(name, y, gp):
    def fn(i, tm, rows, bcs):
        return (rows[0] * jax.nn.sigmoid(rows[1]),), ()
    return rowwise(name, fn, [y, gp], [], [y.shape[1]], out_dtypes=[BF16])[0]


def glu_gate_bwd(name, da, y, gp):
    def fn(i, tm, rows, bcs):
        s = jax.nn.sigmoid(rows[2])
        dgp = rows[0] * rows[1] * s * (1.0 - s)
        return (dgp, rows[0] * s), (_colsum(dgp),)
    w = y.shape[1]
    return rowwise(name, fn, [da, y, gp], [], [w, w], [w])


def loss_head(name, h, tgt, lp, seq):
    def fn(i, tm, rows, bcs):
        pos = (i * tm + lax.broadcasted_iota(jnp.int32, (tm, 1), 0)) % lp
        valid = jnp.logical_and(pos >= N_META, pos < N_META + seq)
        err = jnp.where(valid, rows[0] - rows[1], 0.0)
        part = 0.5 * jnp.sum(jnp.mean(err * err, axis=1, keepdims=True), axis=0, keepdims=True)
        return (err * (1.0 / D_MODEL),), (jnp.broadcast_to(part, (1, LANES)),)
    return rowwise(name, fn, [h, tgt], [], [D_MODEL], [LANES])


def adamw(name, w, gs, m, v):
    ng = len(gs)

    def fn(i, tm, rows, bcs):
        w_ = rows[0]
        g = rows[1] if ng == 1 else rows[1] + rows[2]
        m_, v_ = rows[1 + ng], rows[2 + ng]
        m_ = ADAM_B1 * m_ + (1.0 - ADAM_B1) * g
        v_ = ADAM_B2 * v_ + (1.0 - ADAM_B2) * jnp.square(g)
        m_hat = m_ / (1.0 - ADAM_B1 ** ADAM_STEP)
        v_hat = v_ / (1.0 - ADAM_B2 ** ADAM_STEP)
        delta = -ADAM_LR * (m_hat / (jnp.sqrt(v_hat) + ADAM_EPS) + ADAM_WD * w_)
        return (g, delta, m_, v_), ()
    wd = w.shape[1]
    return rowwise(name, fn, [w] + list(gs) + [m, v], [], [wd] * 4)


def s5_matrices(lam_re, lam_im, log_dt, b_re, b_im, c_re, c_im, d):
    c, hh, gg, pp = S5_CHUNK, S5_GROUP, S5_GROUPS, S5_STATE
    dt = jnp.exp(log_dt)[:, None]
    tau = jnp.arange(c + 1, dtype=F32)[None, :, None]
    mag = jnp.exp(tau * (lam_re * dt)[:, None, :])
    ang = tau * (lam_im * dt)[:, None, :]
    pw_re, pw_im = mag * jnp.cos(ang), mag * jnp.sin(ang)
    nr, ni = pw_re[:, 1] - 1.0, pw_im[:, 1]
    den = lam_re * lam_re + lam_im * lam_im
    cf_re = (nr * lam_re + ni * lam_im) / den
    cf_im = (ni * lam_re - nr * lam_im) / den
    bb_re = cf_re[:, :, None] * b_re - cf_im[:, :, None] * b_im
    bb_im = cf_re[:, :, None] * b_im + cf_im[:, :, None] * b_re
    ct_re, ct_im = c_re.transpose(0, 2, 1)[:, :, None, :], c_im.transpose(0, 2, 1)[:, :, None, :]
    pt_re, pt_im = pw_re.transpose(0, 2, 1)[:, :, :, None], pw_im.transpose(0, 2, 1)[:, :, :, None]
    cp_re = ct_re * pt_re - ct_im * pt_im
    cp_im = ct_re * pt_im + ct_im * pt_re
    hp = lax.Precision.HIGHEST
    kmat = (jnp.einsum('gpk,gpn->gkn', bb_re, cp_re[:, :, :c].reshape(gg, pp, c * hh), precision=hp)
            - jnp.einsum('gpk,gpn->gkn', bb_im, cp_im[:, :, :c].reshape(gg, pp, c * hh), precision=hp))
    rows = [jnp.pad(kmat[:, :, :(c - j) * hh], ((0, 0), (0, 0), (j * hh, 0))) for j in range(c)]
    a1 = jnp.stack(rows, axis=1).reshape(gg, c * hh, c * hh)
    a1 = a1 + jnp.eye(c * hh, dtype=F32)[None] * jnp.tile(d, (1, c))[:, None, :]
    a2 = jnp.concatenate([cp_re[:, :, 1:].reshape(gg, pp, c * hh), -cp_im[:, :, 1:].reshape(gg, pp, c * hh)], axis=1)
    rv_re, rv_im = pw_re[:, :c][:, ::-1, None, :], pw_im[:, :c][:, ::-1, None, :]
    bt_re, bt_im = bb_re.transpose(0, 2, 1)[:, None], bb_im.transpose(0, 2, 1)[:, None]
    a3 = jnp.concatenate([rv_re * bt_re - rv_im * bt_im, rv_re * bt_im + rv_im * bt_re], axis=-1)
    a3 = a3.reshape(gg, c * hh, 2 * pp)
    are = jnp.concatenate([pw_re[:, c], pw_re[:, c]], axis=-1)[:, None, :]
    aim = jnp.concatenate([-pw_im[:, c], pw_im[:, c]], axis=-1)[:, None, :]
    return a1, a2, a3, are, aim


S5_LEVELS = 8


def s5_scan_powers(lam_re, lam_im, log_dt):
    dt = jnp.exp(log_dt)[:, None]
    e = (S5_CHUNK * 2.0 ** jnp.arange(S5_LEVELS, dtype=F32))[:, None, None]
    mag = jnp.exp(e * (lam_re * dt)[None])
    ang = e * (lam_im * dt)[None]
    pr, pi = mag * jnp.cos(ang), mag * jnp.sin(ang)
    pre = jnp.concatenate([pr, pr], axis=-1).transpose(1, 0, 2)
    pim = jnp.concatenate([-pi, pi], axis=-1).transpose(1, 0, 2)
    return pre, pim


def _s5_scan(x, pre, pim, reverse):
    n = x.shape[0]
    rowi = lax.broadcasted_iota(jnp.int32, (n, 1), 0)
    t = x
    for k in range(S5_LEVELS):
        shift = 2 ** k
        if shift >= n:
            break
        p_re, p_im = pre[k:k + 1, :], pim[k:k + 1, :]
        if reverse:
            far = jnp.where(rowi < n - shift, pltpu.roll(t, n - shift, 0), 0.0)
            t = t + (p_re * far + pltpu.roll(p_im * far, S5_STATE, 1))
        else:
            far = jnp.where(rowi >= shift, pltpu.roll(t, shift, 0), 0.0)
            t = t + (p_re * far + p_im * pltpu.roll(far, S5_STATE, 1))
    return t


def s5_fwd(ug, a1, a2, a3, pre, pim, nseq):
    g, nc, cw = ug.shape
    per = nc // nseq
    sw = 2 * S5_STATE
    assert per <= 2 ** S5_LEVELS

    def body(u_ref, a1_ref, a2_ref, a3_ref, pre_ref, pim_ref, y_ref, s_ref):
        u = u_ref[...]
        x = _dot3(u, a3_ref[...], 1, 0)
        first = lax.broadcasted_iota(jnp.int32, (per, 1), 0) == 0
        for b in range(nseq):
            t = _s5_scan(x[b * per:(b + 1) * per], pre_ref[...], pim_ref[...], False)
            s_ref[pl.ds(b * per, per), :] = jnp.where(first, 0.0, pltpu.roll(t, 1, 0))
        y_ref[...] = _dot3(u, a1_ref[...], 1, 0) + _dot3(s_ref[...], a2_ref[...], 1, 0)

    def spec(shape):
        return pl.BlockSpec((None,) + shape, lambda i: (i, 0, 0))
    lv = (S5_LEVELS, sw)
    return pl.pallas_call(
        body, name="s5_fwd", grid=(g,),
        in_specs=[spec((nc, cw)), spec((cw, cw)), spec((sw, cw)), spec((cw, sw)), spec(lv), spec(lv)],
        out_specs=[spec((nc, cw)), spec((nc, sw))],
        out_shape=[jax.ShapeDtypeStruct((g, nc, cw), F32), jax.ShapeDtypeStruct((g, nc, sw), F32)],
        compiler_params=_params(("parallel",)))(ug, a1, a2, a3, pre, pim)


def s5_bwd(dyg, ug, sst, a1, a2, a3, pre, pim, nseq, ex=None):
    g, nc, cw = ug.shape
    per = nc // nseq
    sw = 2 * S5_STATE
    split, ex_start, ex_finish, exkw = _hidden_exchange(ex, 8, 6, (g,))

    def body(*refs):
        ((dy_ref, u_ref, s_ref, a1_ref, a2_ref, a3_ref, pre_ref, pim_ref),
         (du_ref, da1_ref, da2_ref, da3_ref, dare_ref, daim_ref), ex_parts, (dx_ref,)) = split(refs)
        ex_start(ex_parts)
        dy = dy_ref[...]
        u = u_ref[...]
        gd = _dot3(dy, a2_ref[...], 1, 1)
        s_all = s_ref[...]
        da2_ref[...] = _dot3(s_all, dy, 0, 0)
        last = lax.broadcasted_iota(jnp.int32, (per, 1), 0) == per - 1
        for b in range(nseq):
            gs = _s5_scan(gd[b * per:(b + 1) * per], pre_ref[...], pim_ref[...], True)
            dx_ref[pl.ds(b * per, per), :] = jnp.where(last, 0.0, pltpu.roll(gs, per - 1, 0))
        dx = dx_ref[...]
        dare_ref[...] = _colsum(dx * s_all)
        daim_ref[...] = _colsum(dx * pltpu.roll(s_all, S5_STATE, 1))
        du_ref[...] = _dot3(dy, a1_ref[...], 1, 1) + _dot3(dx, a3_ref[...], 1, 1)
        da1_ref[...] = _dot3(u, dy, 0, 0)
        da3_ref[...] = _dot3(u, dx, 0, 0)
        ex_finish(ex_parts)

    def spec(shape):
        return pl.BlockSpec((None,) + shape, lambda i: (i, 0, 0))
    sds = jax.ShapeDtypeStruct
    return pl.pallas_call(
        body, name="s5_bwd", grid=(g,),
        in_specs=[spec((nc, cw)), spec((nc, cw)), spec((nc, sw)), spec((cw, cw)), spec((sw, cw)), spec((cw, sw)),
                  spec((S5_LEVELS, sw)), spec((S5_LEVELS, sw))] + exkw['in_specs'],
        out_specs=[spec((nc, cw)), spec((cw, cw)), spec((sw, cw)), spec((cw, sw)), spec((1, sw)), spec((1, sw))]
        + exkw['out_specs'],
        out_shape=[sds((g, nc, cw), F32), sds((g, cw, cw), F32), sds((g, sw, cw), F32), sds((g, cw, sw), F32),
                   sds((g, 1, sw), F32), sds((g, 1, sw), F32)] + exkw['out_shape'],
        input_output_aliases=exkw['aliases'], scratch_shapes=[pltpu.VMEM((nc, sw), F32)] + exkw['scratch'],
        compiler_params=_params(("arbitrary",)))(dyg, ug, sst, a1, a2, a3, pre, pim, *exkw['ins'])


def _to_groups(u):
    t = u.shape[0]
    nc = t // S5_CHUNK
    return u.reshape(nc, S5_CHUNK, S5_GROUPS, S5_GROUP).transpose(2, 0, 1, 3).reshape(S5_GROUPS, nc, -1)


def _from_groups(yg):
    g, nc, _ = yg.shape
    return yg.reshape(g, nc, S5_CHUNK, S5_GROUP).transpose(1, 2, 0, 3).reshape(nc * S5_CHUNK, g * S5_GROUP)


def _sb_masks():
    row = lax.broadcasted_iota(jnp.int32, (SB_BLOCK, SB_BLOCK), 0)
    col = lax.broadcasted_iota(jnp.int32, (SB_BLOCK, SB_BLOCK), 1)
    lane = lax.broadcasted_iota(jnp.int32, (1, LANES), 1)
    return row, col, [lane < 64, lane >= 64]


def _sb_weights(z, vis, later_of, after):
    sp = jnp.maximum(z, 0.0) + jnp.log1p(jnp.exp(-jnp.abs(z)))
    lk = jnp.where(vis, -sp, 0.0)
    rs = jnp.sum(lk, axis=1, keepdims=True)
    later = _dot_exact_rhs(lk, after, 1, 0) + later_of(rs)
    lb = z - sp
    w = jnp.where(vis, jnp.exp(lb + later), 0.0)
    return w, rs, lb


def _hidden_exchange(ex, n_in, n_out, grid):
    ni, no = (len(ex.ins), len(ex.out_shapes)) if ex else (0, 0)

    def split(refs):
        a, b = n_in + ni, n_in + ni + n_out
        end = len(refs) - (2 if ex else 0)
        return refs[:n_in], refs[a:b], (refs[n_in:a], refs[b:b + no], refs[end:]), refs[b + no:end]

    def at_step(which, fn, parts):
        if ex is not None:
            ids = [pl.program_id(d) == (0 if which == 'first' else g - 1) for d, g in enumerate(grid)]
            cond = ids[0]
            for c in ids[1:]:
                cond = jnp.logical_and(cond, c)
            pl.when(cond)(lambda: fn(parts[0], parts[1], *parts[2]))

    anyspec = pl.BlockSpec(memory_space=pl.ANY)
    kw = dict(in_specs=[anyspec] * ni, out_specs=[anyspec] * no, out_shape=list(ex.out_shapes) if ex else [],
              aliases={n_in + i: n_out + j for i, j in (ex.aliases().items() if ex else ())},
              scratch=ex.sems() if ex else [], ins=list(ex.ins) if ex else [])
    start = functools.partial(at_step, 'first', ex.start if ex else None)
    finish = functools.partial(at_step, 'last', ex.finish if ex else None)
    return split, start, finish, kw


def _lane_tile(x, tl):
    return x[:, tl * LANES:(tl + 1) * LANES]


SB_TILES = 2


def _two_heads(x):
    return jnp.concatenate([x, x], axis=0)


def _two_heads_masks():
    row = lax.broadcasted_iota(jnp.int32, (2 * SB_BLOCK, LANES), 0)
    col = lax.broadcasted_iota(jnp.int32, (2 * SB_BLOCK, LANES), 1)
    return (col // 64) == (row // SB_BLOCK), col < (row % SB_BLOCK)


def _own_lanes(r, lanes0):
    return jnp.where(lanes0, r[:SB_BLOCK], r[SB_BLOCK:])


def _sb_specs(lp, nseq):
    wd = SB_TILES * LANES
    off = [(S5_WIDTH + i * SB_WIDTH) // wd for i in range(3)]
    blk = (lp, wd)
    qkv = [pl.BlockSpec(blk, functools.partial(lambda b, h, o: (b, o + h), o=o)) for o in off]
    return (blk, qkv, pl.BlockSpec(blk, lambda b, h: (b, h)),
            pl.BlockSpec((None, None, 8, LANES), lambda b, h: (b, h, 0, 0)))


def attn_fwd(proj, nseq, lp, ex=None):
    nqb = lp // SB_BLOCK
    t = proj.shape[0]
    nstep = SB_WIDTH // (SB_TILES * LANES)
    split, ex_start, ex_finish, exkw = _hidden_exchange(ex, 3, 3, (nseq, nstep))

    def body(*refs):
        (q_ref, k_ref, v_ref), (o_ref, tot_ref, first_ref), ex_parts, _ = split(refs)
        ex_start(ex_parts)
        row, col, hmask = _sb_masks()
        after = (row > col).astype(BF16)
        tri = col < row
        lane8 = lax.broadcasted_iota(jnp.int32, (8, LANES), 1)

        hm2, tri2 = _two_heads_masks()

        def qloop(qi, firsts):
            q0 = pl.multiple_of(qi * SB_BLOCK, SB_BLOCK)
            q = q_ref[pl.ds(q0, SB_BLOCK), :] * 0.125
            qm = [jnp.where(hm2, _two_heads(_lane_tile(q, tl)), 0.0).astype(BF16) for tl in range(SB_TILES)]

            def alive(carry):
                return jnp.logical_and(carry[0] <= qi, carry[1] > 0)

            def kstep(carry):
                s = carry[0]
                accs, lats = list(carry[2:2 + SB_TILES]), list(carry[2 + SB_TILES:])
                kj = qi - s
                k0 = pl.multiple_of(kj * SB_BLOCK, SB_BLOCK)
                k = k_ref[pl.ds(k0, SB_BLOCK), :].astype(BF16)
                v = v_ref[pl.ds(k0, SB_BLOCK), :].astype(BF16)
                vis = jnp.logical_or(kj < qi, tri2)
                for tl in range(SB_TILES):
                    z = _dot(qm[tl], _lane_tile(k, tl), 1, 1)
                    w, rs, _ = _sb_weights(z, vis, functools.partial(lambda rs, lat: lat, lat=lats[tl]), after)
                    accs[tl] = accs[tl] + _own_lanes(_dot(w.astype(BF16), _lane_tile(v, tl), 1, 0), hmask[0])
                    lats[tl] = lats[tl] + rs
                top = functools.reduce(jnp.maximum, lats)
                go = (jnp.max(top) > SB_DEAD).astype(jnp.int32)
                return (s + 1, go, *accs, *lats)

            res = lax.while_loop(alive, kstep, (jnp.int32(0), jnp.int32(1),
                                                *[jnp.zeros((SB_BLOCK, LANES), F32)] * SB_TILES,
                                                *[jnp.zeros((2 * SB_BLOCK, 1), F32)] * SB_TILES))
            accs, lats = res[2:2 + SB_TILES], res[2 + SB_TILES:]
            for tl in range(SB_TILES):
                cols = slice(tl * LANES, (tl + 1) * LANES)
                o_ref[pl.ds(q0, SB_BLOCK), cols] = accs[tl].astype(o_ref.dtype)
                tot_ref[pl.ds(q0, SB_BLOCK), cols] = _own_lanes(jnp.broadcast_to(lats[tl], (2 * SB_BLOCK, LANES)),
                                                                hmask[0])
            return jnp.where(lane8 == qi, (qi - res[0] + 1).astype(F32), firsts)

        first_ref[...] = lax.fori_loop(0, nqb, qloop, jnp.zeros((8, LANES), F32))
        ex_finish(ex_parts)

    blk, qkv, out, vec = _sb_specs(lp, nseq)
    return pl.pallas_call(
        body, name="attn_fwd", grid=(nseq, nstep),
        in_specs=qkv + exkw['in_specs'], out_specs=[out, out, vec] + exkw['out_specs'],
        out_shape=[jax.ShapeDtypeStruct((t, SB_WIDTH), BF16), jax.ShapeDtypeStruct((t, SB_WIDTH), F32)]
        + [jax.ShapeDtypeStruct((nseq, nstep, 8, LANES), F32)] + exkw['out_shape'],
        input_output_aliases=exkw['aliases'], scratch_shapes=exkw['scratch'],
        compiler_params=_params(("arbitrary", "arbitrary")))(proj, proj, proj, *exkw['ins'])


def attn_bwd(proj, tot, first, do, nseq, lp, ex=None):
    nqb = lp // SB_BLOCK
    t = proj.shape[0]
    nstep = SB_WIDTH // (SB_TILES * LANES)
    split, ex_start, ex_finish, exkw = _hidden_exchange(ex, 6, 3, (nseq, nstep))

    def body(*refs):
        (q_ref, k_ref, v_ref, tot_ref, first_ref, do_ref), (dq_ref, dk_ref, dv_ref), ex_parts, _ = split(refs)
        ex_start(ex_parts)
        row, col, hmask = _sb_masks()
        after = (row > col).astype(BF16)
        before = (row < col).astype(BF16)
        tri = col < row
        lane8 = lax.broadcasted_iota(jnp.int32, (8, LANES), 1)
        firsts = first_ref[...]
        dk_ref[...] = jnp.zeros(dk_ref.shape, F32)
        dv_ref[...] = jnp.zeros(dv_ref.shape, F32)
        hm2, tri2 = _two_heads_masks()

        def qloop(qi, _):
            first = jnp.max(jnp.where(lane8 == qi, firsts, 0.0)).astype(jnp.int32)
            first = jnp.clip(first, 0, qi)
            q0 = pl.multiple_of(qi * SB_BLOCK, SB_BLOCK)
            q = q_ref[pl.ds(q0, SB_BLOCK), :] * 0.125
            do_ = do_ref[pl.ds(q0, SB_BLOCK), :]
            tot_ = tot_ref[pl.ds(q0, SB_BLOCK), :]
            qm = [jnp.where(hm2, _two_heads(_lane_tile(q, tl)), 0.0).astype(BF16) for tl in range(SB_TILES)]
            dom = [jnp.where(hm2, _two_heads(_lane_tile(do_, tl)), 0.0).astype(BF16) for tl in range(SB_TILES)]
            tot = [jnp.max(jnp.where(hm2, _two_heads(_lane_tile(tot_, tl)), -jnp.inf), axis=1, keepdims=True)
                   for tl in range(SB_TILES)]

            def kloop(kj, carry):
                dqs = list(carry[:SB_TILES])
                pre = list(carry[SB_TILES:2 * SB_TILES])
                gpre = list(carry[2 * SB_TILES:])
                k0 = pl.multiple_of(kj * SB_BLOCK, SB_BLOCK)
                k = k_ref[pl.ds(k0, SB_BLOCK), :].astype(BF16)
                v = v_ref[pl.ds(k0, SB_BLOCK), :].astype(BF16)
                vis = jnp.logical_or(kj < qi, tri2)
                for tl in range(SB_TILES):
                    kt, vt = _lane_tile(k, tl), _lane_tile(v, tl)
                    z = _dot(qm[tl], kt, 1, 1)
                    later_of = functools.partial(lambda rs, t_, p_: t_ - p_ - rs, t_=tot[tl], p_=pre[tl])
                    w, rs, lb = _sb_weights(z, vis, later_of, after)
                    dw = _dot(dom[tl], vt, 1, 1)
                    g = dw * w
                    dlk = _dot_exact_rhs(g, before, 1, 0) + gpre[tl]
                    sig = jnp.exp(lb)
                    dz = jnp.where(vis, g * (1.0 - sig) - dlk * sig, 0.0).astype(BF16)
                    dqs[tl] = dqs[tl] + _own_lanes(_dot(dz, kt, 1, 0), hmask[0])
                    cols = slice(tl * LANES, (tl + 1) * LANES)
                    dk_ref[pl.ds(k0, SB_BLOCK), cols] += _dot(dz, qm[tl], 0, 0)
                    dv_ref[pl.ds(k0, SB_BLOCK), cols] += _dot(w.astype(BF16), dom[tl], 0, 0)
                    pre[tl] = pre[tl] + rs
                    gpre[tl] = gpre[tl] + jnp.sum(g, axis=1, keepdims=True)
                return (*dqs, *pre, *gpre)

            z1 = jnp.zeros((2 * SB_BLOCK, 1), F32)
            res = lax.fori_loop(first, qi + 1, kloop,
                                (*[jnp.zeros((SB_BLOCK, LANES), F32)] * SB_TILES, *[z1] * (2 * SB_TILES)))
            for tl in range(SB_TILES):
                dq_ref[pl.ds(q0, SB_BLOCK), tl * LANES:(tl + 1) * LANES] = res[tl] * 0.125
            return 0

        lax.fori_loop(0, nqb, qloop, 0)
        ex_finish(ex_parts)

    blk, qkv, out, vec = _sb_specs(lp, nseq)
    return pl.pallas_call(
        body, name="attn_bwd", grid=(nseq, nstep),
        in_specs=qkv + [out, vec, out] + exkw['in_specs'], out_specs=[out] * 3 + exkw['out_specs'],
        out_shape=[jax.ShapeDtypeStruct((t, SB_WIDTH), F32)] * 3 + exkw['out_shape'],
        input_output_aliases=exkw['aliases'], scratch_shapes=exkw['scratch'],
        compiler_params=_params(("arbitrary", "arbitrary")))(proj, proj, proj, tot, first, do, *exkw['ins'])


def _hg_loop(nch, step, init):
    assert nch % HG_UNROLL == 0

    def trip(i, carry):
        for u in range(HG_UNROLL):
            carry = step(i * HG_UNROLL + u, carry)
        return carry
    return lax.fori_loop(0, nch // HG_UNROLL, trip, init)


def _hg_gates(fc, lb):
    sg = jax.nn.sigmoid(fc)
    f = lb + (1.0 - lb) * sg
    return sg, f


def _hg_decay(f, incl):
    bcum = _dot_exact_lhs(incl, jnp.log(f))
    return bcum, bcum[HG_CHUNK - 1:HG_CHUNK, :]


def _dot_exact_lhs(m, x):
    hi, mid, lo = _split3(x)
    return _dot(m, hi, 1, 0) + (_dot(m, mid, 1, 0) + _dot(m, lo, 1, 0))


def hgrn_fwd(proj, lb, ng, nseq, lp, ex=None):
    nch = lp // HG_CHUNK
    t = proj.shape[0]
    c = HG_CHUNK

    split, ex_start, ex_finish, exkw = _hidden_exchange(ex, 6, 3, (HG_HEADS,))

    def body(*refs):
        (q_ref, f_ref, v_ref, g_ref, lb_ref, ng_ref), (o_ref, og_ref, st_ref), ex_parts, _ = split(refs)
        ex_start(ex_parts)
        row = lax.broadcasted_iota(jnp.int32, (c, c), 0)
        col = lax.broadcasted_iota(jnp.int32, (c, c), 1)
        causal = col <= row
        incl = causal.astype(BF16)
        lbv, ngv = lb_ref[...], ng_ref[...]

        def chunk(b, ci, st):
            rows = pl.ds(pl.multiple_of(b * lp + ci * c, c), c)
            st_ref[b, ci] = st
            _, f = _hg_gates(f_ref[rows, :], lbv)
            bcum, blast = _hg_decay(f, incl)
            kk = 1.0 - f
            vc = v_ref[rows, :].astype(BF16)
            qd = (q_ref[rows, :] * jnp.exp(bcum)).astype(BF16)
            kd = (kk * jnp.exp(-bcum)).astype(BF16)
            a = jnp.where(causal, _dot(qd, kd, 1, 1), 0.0)
            o = _dot(a.astype(BF16), vc, 1, 0) + _dot(qd, st.astype(BF16), 1, 1)
            kend = (kk * jnp.exp(blast - bcum)).astype(BF16)
            st = st * jnp.exp(blast) + _dot(vc, kend, 0, 0)
            o_ref[rows, :] = o
            r = lax.rsqrt(jnp.mean(o * o, axis=1, keepdims=True) + RMS_EPS)
            gc = g_ref[rows, :]
            og_ref[rows, :] = (o * r * ngv * (gc * jax.nn.sigmoid(gc))).astype(og_ref.dtype)
            return st

        def step(ci, sts):
            return tuple(chunk(b, ci, sts[b]) for b in range(nseq))

        _hg_loop(nch, step, tuple(jnp.zeros((HG_DK, HG_DK), F32) for _ in range(nseq)))
        ex_finish(ex_parts)

    blk = (nseq * lp, LANES)
    h = HG_HEADS
    return pl.pallas_call(
        body, name="hgrn_fwd", grid=(h,),
        in_specs=[pl.BlockSpec(blk, lambda hh: (0, hh)),
                  pl.BlockSpec(blk, lambda hh: (0, h + hh)),
                  pl.BlockSpec(blk, lambda hh: (0, 2 * h + hh)),
                  pl.BlockSpec(blk, lambda hh: (0, 3 * h + hh)),
                  pl.BlockSpec((1, LANES), lambda hh: (0, hh)),
                  pl.BlockSpec((1, LANES), lambda hh: (0, hh))] + exkw['in_specs'],
        out_specs=[pl.BlockSpec(blk, lambda hh: (0, hh)),
                   pl.BlockSpec(blk, lambda hh: (0, hh)),
                   pl.BlockSpec((nseq, None, nch, HG_DK, HG_DK), lambda hh: (0, hh, 0, 0, 0))] + exkw['out_specs'],
        out_shape=[jax.ShapeDtypeStruct((t, D_MODEL), F32), jax.ShapeDtypeStruct((t, D_MODEL), BF16),
                   jax.ShapeDtypeStruct((nseq, h, nch, HG_DK, HG_DK), F32)] + exkw['out_shape'],
        input_output_aliases=exkw['aliases'], scratch_shapes=exkw['scratch'],
        compiler_params=_params(("arbitrary",)))(proj, proj, proj, proj, lb, ng, *exkw['ins'])


def hgrn_bwd(proj, o, dog, states, lb, ng, nseq, lp):
    nch = lp // HG_CHUNK
    t = proj.shape[0]
    c = HG_CHUNK

    def body(q_ref, f_ref, v_ref, g_ref, o_ref, dog_ref, st_ref, lb_ref, ng_ref,
             dq_ref, df_ref, dv_ref, dg_ref, dlb_ref, dng_ref):
        row = lax.broadcasted_iota(jnp.int32, (c, c), 0)
        col = lax.broadcasted_iota(jnp.int32, (c, c), 1)
        causal = col <= row
        incl = causal.astype(BF16)
        from_here = (col >= row).astype(BF16)
        last = lax.broadcasted_iota(jnp.int32, (c, 1), 0) == c - 1
        lbv, ngv = lb_ref[...], ng_ref[...]

        def step(s, carry):
            dst, dlb, dng = carry
            ci = nch - 1 - s
            r0 = pl.multiple_of(ci * c, c)
            rows = pl.ds(r0, c)
            oc, gc, dogc = o_ref[rows, :], g_ref[rows, :], dog_ref[rows, :]
            r = lax.rsqrt(jnp.mean(oc * oc, axis=1, keepdims=True) + RMS_EPS)
            sgg = jax.nn.sigmoid(gc)
            silu = gc * sgg
            dg_ref[rows, :] = dogc * (oc * r * ngv) * (sgg * (1.0 + gc * (1.0 - sgg)))
            don = dogc * silu
            dng = dng + _colsum(don * oc * r)
            tt = don * ngv
            do = r * (tt - oc * (r * r) * jnp.mean(tt * oc, axis=1, keepdims=True))
            st = st_ref[ci]
            fc = f_ref[rows, :]
            sg, f = _hg_gates(fc, lbv)
            bcum, blast = _hg_decay(f, incl)
            kk = 1.0 - f
            qc, vf = q_ref[rows, :], v_ref[rows, :]
            pdec = jnp.exp(bcum)
            ndec = jnp.exp(-bcum)
            edec = jnp.exp(blast - bcum)
            eb = jnp.exp(blast)
            qd_f, kd_f, kend_f = qc * pdec, kk * ndec, kk * edec
            qd, kd, kend = qd_f.astype(BF16), kd_f.astype(BF16), kend_f.astype(BF16)
            vc, dob, stb, dstb = vf.astype(BF16), do.astype(BF16), st.astype(BF16), dst.astype(BF16)
            a = jnp.where(causal, _dot(qd, kd, 1, 1), 0.0).astype(BF16)
            da = jnp.where(causal, _dot(dob, vc, 1, 1), 0.0).astype(BF16)
            dv_ref[rows, :] = _dot(a, dob, 0, 0) + _dot(kend, dstb, 1, 1)
            dqd = _dot(da, kd, 1, 0) + _dot(dob, stb, 1, 0)
            dkd = _dot(da, qd, 0, 0)
            dkend = _dot(vc, dstb, 1, 0)
            dq_ref[rows, :] = dqd * pdec
            dblast = _colsum(dkend * kend_f) + eb * _colsum(dst * st)
            dbcum = dqd * qd_f - dkd * kd_f - dkend * kend_f
            dbcum = dbcum + jnp.where(last, dblast, 0.0)
            dlf = _dot_exact_lhs(from_here, dbcum)
            df = dlf / f - (dkd * ndec + dkend * edec)
            df_ref[rows, :] = df * (1.0 - lbv) * sg * (1.0 - sg)
            dlb = dlb + _colsum(df * (1.0 - sg))
            dst = dst * eb + _dot(dob, qd, 0, 0)
            return dst, dlb, dng

        z = jnp.zeros((1, LANES), F32)
        _, dlb, dng = _hg_loop(nch, step, (jnp.zeros((HG_DK, HG_DK), F32), z, z))
        dlb_ref[...] = dlb
        dng_ref[...] = dng

    blk = (lp, LANES)
    h = HG_HEADS
    vec = pl.BlockSpec((1, LANES), lambda b, hh: (0, hh))
    svec = pl.BlockSpec((None, 1, LANES), lambda b, hh: (b, 0, hh))
    return pl.pallas_call(
        body, name="hgrn_bwd", grid=(nseq, h),
        in_specs=[pl.BlockSpec(blk, lambda b, hh: (b, hh)),
                  pl.BlockSpec(blk, lambda b, hh: (b, h + hh)),
                  pl.BlockSpec(blk, lambda b, hh: (b, 2 * h + hh)),
                  pl.BlockSpec(blk, lambda b, hh: (b, 3 * h + hh)),
                  pl.BlockSpec(blk, lambda b, hh: (b, hh)),
                  pl.BlockSpec(blk, lambda b, hh: (b, hh)),
                  pl.BlockSpec((None, None, nch, HG_DK, HG_DK), lambda b, hh: (b, hh, 0, 0, 0)),
                  vec, vec],
        out_specs=[pl.BlockSpec(blk, lambda b, hh: (b, hh))] * 4 + [svec, svec],
        out_shape=[jax.ShapeDtypeStruct((t, D_MODEL), F32)] * 4
        + [jax.ShapeDtypeStruct((nseq, 1, D_MODEL), F32)] * 2,
        compiler_params=_params(("parallel", "parallel")))(proj, proj, proj, proj, o, dog, states, lb, ng)


def _lower_bound(gamma):
    p = jax.nn.softmax(gamma, axis=0)
    return (jnp.cumsum(p, axis=0) - p[0])[1][None, :]


def local_step(x, tgt, w, hooks=None):
    nseq, seq, _ = x.shape
    lp = -(-(N_META + seq) // SB_BLOCK) * SB_BLOCK
    t = nseq * lp
    pad = lp - N_META - seq
    h0 = jnp.concatenate([jnp.broadcast_to(w['meta'][None], (nseq, N_META, D_MODEL)), x,
                          jnp.zeros((nseq, pad, D_MODEL), F32)], axis=1).reshape(t, D_MODEL)
    tgt_p = jnp.pad(tgt, ((0, 0), (N_META, pad), (0, 0))).reshape(t, D_MODEL)
    row = lambda v: v.reshape(1, -1)
    g = {}

    proj = matmul("in_ab", h0, w['w_in_ab'], 'nn')
    att = attn_fwd(proj, nseq, lp, ex=hooks.gather_late if hooks else None)
    b_out, sb_tot, sb_first = att[:3]
    if hooks:
        w = {**w, **hooks.late_weights(att[3:])}
    s5_names = ['s5_lam_re', 's5_lam_im', 's5_log_dt', 's5_b_re', 's5_b_im', 's5_c_re', 's5_c_im', 's5_d']
    mats, mats_vjp = jax.vjp(s5_matrices, *[w[n][0] for n in s5_names])
    ug = _to_groups(proj[:, :S5_WIDTH])
    s5_pows = s5_scan_powers(*[w[n][0] for n in s5_names[:3]])
    yg, sst = s5_fwd(ug, *mats[:3], *s5_pows, nseq)
    ys5 = _from_groups(yg)
    y = gelu_fwd("gelu", ys5)
    gp = matmul("glu", y, w['s5_w_glu'], 'nn', bias=w['s5_b_glu'])
    a_out = glu_gate("glu_gate", y, gp)
    cat = jnp.concatenate([a_out, b_out], axis=1)
    mix0 = matmul("out_ab", cat, w['w_out_ab'], 'nn')
    h1, xh1, rs1 = ln_fwd("ln_mix0", h0, mix0, row(w['ln_mix_g'][0]), row(w['ln_mix_b'][0]))
    hid0 = matmul("up0", h1, w['mlp_w_up'][0], 'nn', bias=row(w['mlp_b_up'][0]), act='relu2', out_dtype=BF16)
    dn0 = matmul("down0", hid0, w['mlp_w_down'][0], 'nn', bias=row(w['mlp_b_down'][0]))
    h2, xh2, rs2 = ln_fwd("ln_mlp0", h1, dn0, row(w['ln_mlp_g'][0]), row(w['ln_mlp_b'][0]))

    projc = matmul("in_c", h2, w['w_in_c'], 'nn')
    lb, lb_vjp = jax.vjp(_lower_bound, w['hgrn_gamma'])
    ng = w['hgrn_norm_g']
    res = hgrn_fwd(projc, lb, ng, nseq, lp, ex=hooks.gather_last if hooks else None)
    o, og, states = res[:3]
    if hooks:
        w = {**w, **hooks.last_weights(res[3:])}
    mix1 = matmul("out_c", og, w['w_out_c'], 'nn')
    h3, xh3, rs3 = ln_fwd("ln_mix1", h2, mix1, row(w['ln_mix_g'][1]), row(w['ln_mix_b'][1]))
    hid1 = matmul("up1", h3, w['mlp_w_up'][1], 'nn', bias=row(w['mlp_b_up'][1]), act='relu2', out_dtype=BF16)
    dn1 = matmul("down1", hid1, w['mlp_w_down'][1], 'nn', bias=row(w['mlp_b_down'][1]))
    h4, xh4, rs4 = ln_fwd("ln_mlp1", h3, dn1, row(w['ln_mlp_g'][1]), row(w['ln_mlp_b'][1]))
    dy, loss = loss_head("loss", h4, tgt_p, lp, seq)

    def mlp_bwd(l, dh_out, xh_out, rs_out, hid, h_in):
        dpre, dg_, db_, dbd = ln_bwd(f"ln_mlp{l}_b", dh_out, xh_out, rs_out, row(w['ln_mlp_g'][l]))
        dpu = matmul(f"down{l}_dx", dpre, w['mlp_w_down'][l], 'nt', relu2_of=hid, out_dtype=BF16)
        dwd = matmul(f"down{l}_dw", hid, dpre, 'tn')
        dh_in = matmul(f"up{l}_dx", dpu, w['mlp_w_up'][l], 'nt', add=dpre, add_scale=ALPHA)
        dwu, dbu = matmul(f"up{l}_dw", h_in, dpu, 'tn', out_slots=N_CHIPS, colsum=True)
        return dh_in, dict(ln_mlp_g=dg_, ln_mlp_b=db_, mlp_b_down=dbd, mlp_b_up=dbu, mlp_w_up=dwu, mlp_w_down=dwd)

    dh3, gm1 = mlp_bwd(1, dy, xh4, rs4, hid1, h3)
    dpre, dg1, db1, _ = ln_bwd("ln_mix1_b", dh3, xh3, rs3, row(w['ln_mix_g'][1]))
    g['w_out_c'] = matmul("out_c_dw", og, dpre, 'tn')
    dog = matmul("out_c_dx", dpre, w['w_out_c'], 'nt')
    dq, df, di, dgg, dlb, dng = hgrn_bwd(projc, o, dog, states, lb, ng, nseq, lp)
    dprojc = jnp.concatenate([dq, df, di, dgg], axis=1)
    dh2 = matmul("in_c_dx", dprojc, w['w_in_c'], 'nt', add=dpre, add_scale=ALPHA)
    g['w_in_c'] = matmul("in_c_dw", h2, dprojc, 'tn', out_slots=N_CHIPS)
    g['hgrn_gamma'] = lb_vjp(jnp.sum(dlb, axis=0))[0]
    g['hgrn_norm_g'] = jnp.sum(dng, axis=0)

    dh1, gm0 = mlp_bwd(0, dh2, xh2, rs2, hid0, h1)
    dpre, dg0, db0, _ = ln_bwd("ln_mix0_b", dh1, xh1, rs1, row(w['ln_mix_g'][0]))
    g['w_out_ab'] = matmul("out_ab_dw", cat, dpre, 'tn')
    dcat = matmul("out_ab_dx", dpre, w['w_out_ab'], 'nt')
    dgp, da_s, dbglu = glu_gate_bwd("glu_gate_b", dcat[:, :S5_WIDTH], y, gp)
    g['s5_w_glu'] = matmul("glu_dw", y, dgp, 'tn')
    g['s5_b_glu'] = dbglu
    dy5 = matmul("glu_dx", dgp, w['s5_w_glu'], 'nt', add=da_s)
    dys5 = gelu_bwd("gelu_b", dy5, ys5)
    for n in ('mlp_w_up', 'mlp_w_down'):
        g[n] = [gm0[n], gm1[n]]
    g['ln_mix_g'] = jnp.concatenate([dg0, dg1], axis=0)
    g['ln_mix_b'] = jnp.concatenate([db0, db1], axis=0)
    for n in ('ln_mlp_g', 'ln_mlp_b', 'mlp_b_down', 'mlp_b_up'):
        g[n] = jnp.concatenate([gm0[n], gm1[n]], axis=0)
    res = s5_bwd(_to_groups(dys5), ug, sst, *mats[:3], *s5_pows, nseq, ex=hooks.fold_early(g) if hooks else None)
    dug, da1, da2, da3, dare, daim = res[:6]
    for n, v in zip(s5_names, mats_vjp((da1, da2, da3, dare, daim))):
        g[n] = v[None]
    ex = hooks.scatter_early(g, loss, res[6:]) if hooks else None
    res = attn_bwd(proj, sb_tot, sb_first, dcat[:, S5_WIDTH:], nseq, lp, ex=ex)
    dqa, dka, dva = res[:3]
    if hooks:
        hooks.scattered(res[3:])
    dproj = jnp.concatenate([_from_groups(dug), dqa, dka, dva], axis=1)
    dh0 = matmul("in_ab_dx", dproj, w['w_in_ab'], 'nt', add=dpre, add_scale=ALPHA)
    g['w_in_ab'] = matmul("in_ab_dw", h0, dproj, 'tn', out_slots=N_CHIPS)

    dh0 = dh0.reshape(nseq, lp, D_MODEL)
    grad_x = dh0[:, N_META:N_META + seq]
    g['meta'] = jnp.sum(dh0[:, :N_META], axis=0)
    return loss, grad_x, g


class Exchange:
    def __init__(self, ins, out_shapes, n_remote, build, in_place):
        self.ins, self.out_shapes, self.n_remote, self.build, self.in_place = ins, out_shapes, n_remote, build, in_place

    def sems(self):
        return [pltpu.SemaphoreType.DMA((self.n_remote,)), pltpu.SemaphoreType.DMA((self.n_remote,))]

    def aliases(self):
        return self.in_place if isinstance(self.in_place, dict) else \
            {i: i for i in range(len(self.ins) if self.in_place else 0)}

    def beside(self, other):
        na, nao = len(self.ins), len(self.out_shapes)

        def build(in_refs, out_refs, me):
            pa = self.build(in_refs[:na], out_refs[:nao], me)
            pb = other.build(in_refs[na:], out_refs[nao:], me)
            n = max(len(pa), len(pb))
            return [(pa[i] if i < len(pa) else []) + (pb[i] if i < len(pb) else []) for i in range(n)]
        al = dict(self.aliases())
        al.update({na + i: nao + j for i, j in other.aliases().items()})
        return Exchange(list(self.ins) + list(other.ins), list(self.out_shapes) + list(other.out_shapes),
                        self.n_remote + other.n_remote, build, al)

    def phases(self, in_refs, out_refs, ssem, rsem):
        me = (lax.axis_index("x"), lax.axis_index("y"), lax.axis_index("c"))
        out, k = [], 0
        for phase in self.build(in_refs, out_refs, me):
            out.append(functools.partial(
                lambda phase, k: [pltpu.make_async_remote_copy(src_ref=s, dst_ref=d, send_sem=ssem.at[k + i],
                                                               recv_sem=rsem.at[k + i], device_id=p,
                                                               device_id_type=MESH)
                                  for i, (s, d, p) in enumerate(phase)], phase, k))
            k += len(phase)
        return out

    def start(self, in_refs, out_refs, ssem, rsem):
        for cp in self.phases(in_refs, out_refs, ssem, rsem)[0]():
            cp.start()

    def finish(self, in_refs, out_refs, ssem, rsem):
        phases = self.phases(in_refs, out_refs, ssem, rsem)
        for cp in phases[0]():
            cp.wait()
        for make in phases[1:]:
            copies = make()
            for cp in copies:
                cp.start()
            for cp in copies:
                cp.wait()


def _exchange(name, ex):
    ni, no = len(ex.ins), len(ex.out_shapes)

    def body(*refs):
        in_refs, out_refs, sems = refs[:ni], refs[ni:ni + no], refs[ni + no:]
        ex.start(in_refs, out_refs, *sems)
        ex.finish(in_refs, out_refs, *sems)

    anyspec = pl.BlockSpec(memory_space=pl.ANY)
    return pl.pallas_call(
        body, name=name, in_specs=[anyspec] * ni, out_specs=[anyspec] * no, out_shape=ex.out_shapes,
        input_output_aliases=ex.aliases(), scratch_shapes=ex.sems())(*ex.ins)


def _chip_peers(me):
    x, y, c = me
    return [(x ^ (m >> 1), y ^ (m & 1), c) for m in (1, 2, 3)]


def _half(ref, c, axis):
    h = ref.shape[axis] // 2
    idx = [slice(None)] * axis + [pl.ds(c * h, h)]
    return ref.at[tuple(idx)]


def _like(arrs):
    return [jax.ShapeDtypeStruct(a.shape, a.dtype) for a in arrs]


def gather_over_chips(bufs):
    def build(in_refs, out_refs, me):
        x, y, c = me
        j = 2 * x + y
        sib = (x, y, 1 - c)
        ici = [(_half(o.at[j], c, 0), _half(o.at[j], c, 0), p) for o in out_refs for p in _chip_peers(me)]
        d2d = [(_half(o.at[2 * p[0] + p[1]], c, 0), _half(o.at[2 * p[0] + p[1]], c, 0), sib)
               for o in out_refs for p in _chip_peers(me)]
        return [ici, d2d]
    return Exchange(bufs, _like(bufs), 6 * len(bufs), build, True)


def fold_cores(fulls):
    def build(in_refs, out_refs, me):
        x, y, c = me
        return [[(_half(s, 1 - c, 1), o, (x, y, 1 - c)) for s, o in zip(in_refs, out_refs)]]
    shapes = [jax.ShapeDtypeStruct((f.shape[0], f.shape[1] // 2, f.shape[2]), f.dtype) for f in fulls]
    return Exchange(fulls, shapes, len(fulls), build, False)


def scatter_over_chips(parts):
    def build(in_refs, out_refs, me):
        j = 2 * me[0] + me[1]
        return [[(s.at[2 * p[0] + p[1]], o.at[j], p) for s, o in zip(in_refs, out_refs) for p in _chip_peers(me)]]
    return Exchange(parts, _like(parts), 3 * len(parts), build, False)


def join_cores(bufs):
    def build(in_refs, out_refs, me):
        x, y, c = me
        return [[(o.at[c], o.at[c], (x, y, 1 - c)) for o in out_refs]]
    return Exchange(bufs, _like(bufs), len(bufs), build, True)


def gather_over_devices(buf):
    def build(in_refs, out_refs, me):
        x, y, c = me
        i = 4 * x + 2 * y + c
        out = out_refs[0]
        return [[(out.at[i], out.at[i], (x ^ (m >> 2), y ^ ((m >> 1) & 1), c ^ (m & 1))) for m in range(1, N_DEV)]]
    return Exchange([buf], _like([buf]), N_DEV - 1, build, True)


def _slot_call(name, body, scalars, ins, in_maps, out_shape, out_map, blk, grid):
    gs = pltpu.PrefetchScalarGridSpec(
        num_scalar_prefetch=1, grid=grid,
        in_specs=[pl.BlockSpec((None,) + blk, m) for m in in_maps],
        out_specs=pl.BlockSpec((None,) + blk, out_map))
    return pl.pallas_call(body, name=name, grid_spec=gs, out_shape=out_shape,
                          compiler_params=_params(("arbitrary",) * len(grid)))(scalars, *ins)


def cast_into_slot(name, w2d, chip, dtype):
    r, wd = w2d.shape
    tm = _pick(r, (512, 256, 128, 64, 32, 16))

    def body(s_ref, w_ref, o_ref):
        o_ref[...] = w_ref[...].astype(o_ref.dtype)

    gs = pltpu.PrefetchScalarGridSpec(
        num_scalar_prefetch=1, grid=(r // tm,),
        in_specs=[pl.BlockSpec((tm, wd), lambda i, s: (i, 0))],
        out_specs=pl.BlockSpec((None, tm, wd), lambda i, s: (s[0], i, 0)))
    return pl.pallas_call(body, name=name, grid_spec=gs,
                          out_shape=jax.ShapeDtypeStruct((N_CHIPS, r, wd), dtype),
                          compiler_params=_params(("arbitrary",)))(chip.reshape(1), w2d)


def sum_cores(name, full, theirs, core):
    s, r, wd = full.shape
    h = r // 2
    tm = _pick(h, (512, 256, 128, 64, 32, 16))
    nt = h // tm

    def body(s_ref, a_ref, b_ref, o_ref):
        o_ref[...] = (a_ref[...] + b_ref[...]).astype(o_ref.dtype)

    return _slot_call(name, body, core.reshape(1), [full, theirs],
                      [lambda k, i, s_: (k, s_[0] * nt + i, 0), lambda k, i, s_: (k, i, 0)],
                      jax.ShapeDtypeStruct((s, h, wd), BF16), lambda k, i, s_: (k, i, 0), (tm, wd), (s, nt))


def sum_chips(name, own, recv, chip, core):
    s, r, wd = own.shape
    tm = _pick(r, (512, 256, 128, 64, 32, 16))

    def body(s_ref, a_ref, b1_ref, b2_ref, b3_ref, o_ref):
        acc = a_ref[...].astype(F32)
        for ref in (b1_ref, b2_ref, b3_ref):
            acc = acc + ref[...].astype(F32)
        o_ref[...] = acc

    maps = [lambda i, s_: (s_[0], i, 0)]
    maps += [functools.partial(lambda i, s_, m: (s_[0] ^ m, i, 0), m=m) for m in (1, 2, 3)]
    return _slot_call(name, body, jnp.stack([chip, core]), [own, recv, recv, recv], maps,
                      jax.ShapeDtypeStruct((2, r, wd), F32), lambda i, s_: (s_[1], i, 0), (tm, wd), (r // tm,))


def sum_slots(name, arr):
    s, r, wd = arr.shape
    tm = _pick(r, (512, 256, 128, 64, 32, 16, 8))

    def body(*refs):
        acc = refs[0][...].astype(F32)
        for ref in refs[1:s]:
            acc = acc + ref[...].astype(F32)
        refs[s][...] = acc

    specs = [pl.BlockSpec((None, tm, wd), functools.partial(lambda i, k: (k, i, 0), k=k)) for k in range(s)]
    return pl.pallas_call(body, name=name, grid=(r // tm,), in_specs=specs,
                          out_specs=pl.BlockSpec((tm, wd), lambda i: (i, 0)),
                          out_shape=jax.ShapeDtypeStruct((r, wd), F32),
                          compiler_params=_params(("parallel",)))(*([arr] * s))


def _pack(arrs):
    flat = jnp.concatenate([a.reshape(-1) for a in arrs])
    n = flat.shape[0]
    padded = -(-n // (512 * LANES)) * (512 * LANES)
    return jnp.pad(flat, (0, padded - n)).reshape(-1, LANES)


def _unpack(packed, shapes):
    flat = packed.reshape(-1)
    out, pos = [], 0
    for s in shapes:
        n = math.prod(s)
        out.append(flat[pos:pos + n].reshape(s))
        pos += n
    return out


def _as2d(a):
    return a.reshape(-1, a.shape[-1])


def kernel(x, meta, w_in_ab, s5_lam_re, s5_lam_im, s5_log_dt, s5_b_re, s5_b_im, s5_c_re, s5_c_im, s5_d, s5_w_glu, s5_b_glu, w_out_ab, w_in_c, hgrn_gamma, hgrn_norm_g, w_out_c, ln_mix_g, ln_mix_b, mlp_w_up, mlp_b_up, mlp_w_down, mlp_b_down, ln_mlp_g, ln_mlp_b, loss_target, m_meta, m_w_in_ab, m_s5_lam_re, m_s5_lam_im, m_s5_log_dt, m_s5_b_re, m_s5_b_im, m_s5_c_re, m_s5_c_im, m_s5_d, m_s5_w_glu, m_s5_b_glu, m_w_out_ab, m_w_in_c, m_hgrn_gamma, m_hgrn_norm_g, m_w_out_c, m_ln_mix_g, m_ln_mix_b, m_mlp_w_up, m_mlp_b_up, m_mlp_w_down, m_mlp_b_down, m_ln_mlp_g, m_ln_mlp_b, v_meta, v_w_in_ab, v_s5_lam_re, v_s5_lam_im, v_s5_log_dt, v_s5_b_re, v_s5_b_im, v_s5_c_re, v_s5_c_im, v_s5_d, v_s5_w_glu, v_s5_b_glu, v_w_out_ab, v_w_in_c, v_hgrn_gamma, v_hgrn_norm_g, v_w_out_c, v_ln_mix_g, v_ln_mix_b, v_mlp_w_up, v_mlp_b_up, v_mlp_w_down, v_mlp_b_down, v_ln_mlp_g, v_ln_mlp_b):
    given = dict(locals())
    wts = {n: given[n] for n in WEIGHTS}
    ms = {n: given['m_' + n] for n in WEIGHTS}
    vs = {n: given['v_' + n] for n in WEIGHTS}
    chip = 2 * lax.axis_index("x") + lax.axis_index("y")

    core = lax.axis_index("c")
    parts = [(n, l) for n in BIG for l in (range(DEPTH) if n.startswith('mlp_') else [0])]
    tags = [f"{n}{l}" for n, l in parts]
    shards = {p: cast_into_slot("cast_" + t, wts[p[0]][p[1]], chip, BF16) for t, p in zip(tags, parts)}
    small_sh = jnp.concatenate([meta, hgrn_norm_g, jnp.zeros((15, meta.shape[1]), F32)], axis=0)
    first, late = parts[:1], parts[1:]
    w_in_ab_all, sm = _exchange("gather_first", gather_over_chips(
        [shards[p] for p in first] + [cast_into_slot("cast_small", small_sh, chip, F32)]))
    w = {n: wts[n] for n in SMALL}
    w['meta'] = sm[:, :N_META].transpose(1, 0, 2).reshape(N_META, D_MODEL)
    w['hgrn_norm_g'] = sm[:, N_META:N_META + 1].transpose(1, 0, 2).reshape(1, D_MODEL)
    w['w_in_ab'] = w_in_ab_all

    def slot_major(v):
        return v if v.ndim == 3 else v.reshape(N_CHIPS, -1, v.shape[-1])

    def whole_grads(ps, g):
        return [slot_major(g[n][l] if n.startswith('mlp_') else g[n]) for n, l in ps]

    def chip_sums_of(ps, gfull, theirs):
        return [sum_cores(f"sum_cores_{n}{l}", a, b, core) for (n, l), a, b in zip(ps, gfull, theirs)]

    def all_devices(arrs):
        packed = _pack(arrs)
        return gather_over_devices(lax.dynamic_update_slice(
            jnp.zeros((N_DEV,) + packed.shape, F32), packed[None], (2 * chip + core, 0, 0)))

    early_small = [n for n in SMALL if n != 'meta']
    last = [('w_out_c', 0), ('mlp_w_up', 1), ('mlp_w_down', 1)]
    mid = [p for p in late if p not in last]

    class Hooks:
        gather_late = gather_over_chips([shards[p] for p in mid])
        gather_last = gather_over_chips([shards[p] for p in last])

        @staticmethod
        def late_weights(filled):
            fw = Hooks.fw = dict(zip(mid, filled))
            return {'s5_w_glu': fw['s5_w_glu', 0].reshape(S5_WIDTH, S5_WIDTH),
                    'w_out_ab': fw['w_out_ab', 0].reshape(D_MODEL, D_MODEL),
                    'w_in_c': fw['w_in_c', 0],
                    'mlp_w_up': [fw['mlp_w_up', 0]],
                    'mlp_w_down': [fw['mlp_w_down', 0].reshape(D_FF, D_MODEL)]}

        @staticmethod
        def last_weights(filled):
            fw = dict(zip(last, filled))
            return {'w_out_c': fw['w_out_c', 0].reshape(D_MODEL, D_MODEL),
                    'mlp_w_up': [Hooks.fw['mlp_w_up', 0], fw['mlp_w_up', 1]],
                    'mlp_w_down': [Hooks.fw['mlp_w_down', 0].reshape(D_FF, D_MODEL),
                                   fw['mlp_w_down', 1].reshape(D_FF, D_MODEL)]}

        @staticmethod
        def fold_early(g):
            Hooks.whole = whole_grads(late, g)
            return fold_cores(Hooks.whole)

        @staticmethod
        def scatter_early(g, loss, theirs):
            Hooks.sums = chip_sums_of(late, Hooks.whole, theirs)
            Hooks.small_shapes = [(LANES,)] + [g[n].shape for n in early_small]
            return scatter_over_chips(Hooks.sums).beside(
                all_devices([loss.reshape(-1)] + [g[n] for n in early_small]))

        @staticmethod
        def scattered(outs):
            Hooks.recv, Hooks.small = list(outs[:-1]), outs[-1]

    loss, grad_x, g = local_step(x, loss_target, w, Hooks)

    whole = whole_grads(first, g)
    sums = chip_sums_of(first, whole, _exchange("fold_last", fold_cores(whole)))
    *recv, meta_slots = _exchange("scatter_last", scatter_over_chips(sums).beside(all_devices([g['meta']])))
    reduced = _exchange("join_grads", join_cores(
        [sum_chips(f"sum_chips_{n}{l}", a, b, chip, core)
         for (n, l), a, b in zip(first + late, sums + Hooks.sums, list(recv) + Hooks.recv)]))
    reduced = dict(zip(first + late, [_as2d(r) for r in reduced]))
    red = _unpack(sum_slots("sum_small", Hooks.small), Hooks.small_shapes)
    loss_out = red[0][0]
    gs = dict(zip(early_small, red[1:]))
    gs['meta'] = _unpack(sum_slots("sum_meta", meta_slots), [g['meta'].shape])[0]
    gs['meta'] = lax.dynamic_slice_in_dim(gs['meta'], chip * meta.shape[1], meta.shape[1], axis=1)
    gs['hgrn_norm_g'] = lax.dynamic_slice_in_dim(gs['hgrn_norm_g'].reshape(1, -1), chip * meta.shape[1],
                                                 meta.shape[1], axis=1)

    grads, deltas, new_m, new_v = {}, {}, {}, {}
    for n in BIG:
        r = jnp.concatenate([reduced[n, l] for l in range(wts[n].shape[0])], axis=0)
        res = adamw("adamw_" + n, _as2d(wts[n]), [r], _as2d(ms[n]), _as2d(vs[n]))
        grads[n], deltas[n], new_m[n], new_v[n] = [r.reshape(wts[n].shape) for r in res]
    shapes = [wts[n].shape for n in SMALL]
    res = adamw("adamw_small", _pack([wts[n] for n in SMALL]), [_pack([gs[n] for n in SMALL])],
                _pack([ms[n] for n in SMALL]), _pack([vs[n] for n in SMALL]))
    for d, r in zip((grads, deltas, new_m, new_v), res):
        d.update(zip(SMALL, _unpack(r, shapes)))
    return (loss_out, grad_x, *[grads[n] for n in WEIGHTS], *[deltas[n] for n in WEIGHTS],
            *[new_m[n] for n in WEIGHTS], *[new_v[n] for n in WEIGHTS])
```

```python
import functools
import math

import jax
import jax.numpy as jnp
from jax import lax
from jax.experimental import pallas as pl
from jax.experimental.pallas import tpu as pltpu

F32 = jnp.float32
BF16 = jnp.bfloat16

D_MODEL = 1024
N_META = 16
DEPTH = 2
S5_WIDTH = 512
S5_GROUP = 16
S5_GROUPS = 32
S5_STATE = 64
S5_CHUNK = 16
SB_WIDTH = 512
SB_BLOCK = 128
SB_DEAD = -110.0
HG_HEADS = 8
HG_DK = 128
HG_CHUNK = 64
HG_UNROLL = 2
D_FF = 4096
ALPHA = (2.0 * DEPTH) ** 0.25
LN_EPS = 1e-5
RMS_EPS = 1e-6
ADAM_LR = 0.001
ADAM_B1 = 0.9
ADAM_B2 = 0.999
ADAM_EPS = 1e-08
ADAM_WD = 0.01
ADAM_STEP = 10
N_CHIPS = 4
N_DEV = 8
LANES = 128
MESH = pl.DeviceIdType.MESH
VMEM_LIMIT = 56 * 1024 * 1024
WHOLE_K_BYTES = 42 * 1024 * 1024

WEIGHTS = ['meta', 'w_in_ab', 's5_lam_re', 's5_lam_im', 's5_log_dt', 's5_b_re', 's5_b_im', 's5_c_re', 's5_c_im',
           's5_d', 's5_w_glu', 's5_b_glu', 'w_out_ab', 'w_in_c', 'hgrn_gamma', 'hgrn_norm_g', 'w_out_c',
           'ln_mix_g', 'ln_mix_b', 'mlp_w_up', 'mlp_b_up', 'mlp_w_down', 'mlp_b_down', 'ln_mlp_g', 'ln_mlp_b']
BIG = ['w_in_ab', 's5_w_glu', 'w_out_ab', 'w_in_c', 'w_out_c', 'mlp_w_up', 'mlp_w_down']
SMALL = [n for n in WEIGHTS if n not in BIG]


def _params(sem=None):
    return pltpu.CompilerParams(dimension_semantics=sem, vmem_limit_bytes=VMEM_LIMIT)


def _pick(n, cands):
    for c in cands:
        if n % c == 0:
            return c
    return n


def _dot(a, b, ca, cb):
    return lax.dot_general(a, b, (((ca,), (cb,)), ((), ())), preferred_element_type=F32)


def _split3(x):
    hi = x.astype(BF16)
    r = x - hi.astype(F32)
    mid = r.astype(BF16)
    lo = (r - mid.astype(F32)).astype(BF16)
    return hi, mid, lo


def _dot_exact_rhs(x, m, cx, cm):
    hi = x.astype(BF16)
    lo = (x - hi.astype(F32)).astype(BF16)
    return _dot(hi, m, cx, cm) + _dot(lo, m, cx, cm)


def _dot3(a, b, ca, cb):
    ah = a.astype(BF16)
    al = (a - ah.astype(F32)).astype(BF16)
    bh = b.astype(BF16)
    bl = (b - bh.astype(F32)).astype(BF16)
    return _dot(ah, bh, ca, cb) + (_dot(ah, bl, ca, cb) + _dot(al, bh, ca, cb))


def rowwise(name, fn, row_ins, bc_ins, out_widths, sum_widths=(), out_dtypes=None, tm=None):
    t = row_ins[0].shape[0]
    if tm is None:
        wmax = max([a.shape[1] for a in row_ins] + list(out_widths))
        tm = _pick(t, (272, 256, 128, 64, 16, 8)) if wmax > 1024 else _pick(t, (544, 512, 256, 128, 64, 16, 8))
    nr, nb, no, ns = len(row_ins), len(bc_ins), len(out_widths), len(sum_widths)
    out_dtypes = out_dtypes or [F32] * no

    def body(*refs):
        i = pl.program_id(0)
        rows = [r[...] for r in refs[:nr]]
        bcs = [r[...] for r in refs[nr:nr + nb]]
        outs, sums = fn(i, tm, rows, bcs)
        for r, v in zip(refs[nr + nb:nr + nb + no], outs):
            r[...] = v.astype(r.dtype)
        if ns:
            srefs = refs[nr + nb + no:]

            @pl.when(i == 0)
            def _():
                for r in srefs:
                    r[...] = jnp.zeros(r.shape, r.dtype)

            for r, v in zip(srefs, sums):
                r[...] += v

    in_specs = [pl.BlockSpec((tm, a.shape[1]), lambda i: (i, 0)) for a in row_ins]
    in_specs += [pl.BlockSpec(a.shape, lambda i: (0, 0)) for a in bc_ins]
    out_specs = [pl.BlockSpec((tm, w), lambda i: (i, 0)) for w in out_widths]
    out_specs += [pl.BlockSpec((1, w), lambda i: (0, 0)) for w in sum_widths]
    out_shape = [jax.ShapeDtypeStruct((t, w), dt) for w, dt in zip(out_widths, out_dtypes)]
    out_shape += [jax.ShapeDtypeStruct((1, w), F32) for w in sum_widths]
    res = pl.pallas_call(body, name=name, grid=(t // tm,), in_specs=in_specs, out_specs=out_specs,
                         out_shape=out_shape, compiler_params=_params(("arbitrary",)))(*row_ins, *bc_ins)
    return res


def _colsum(v):
    return jnp.sum(v, axis=0, keepdims=True)


def matmul(name, a, b, mode, *, bias=None, act=None, add=None, add_scale=1.0, relu2_of=None, colsum=False,
           out_dtype=F32, out_slots=0, tm=None, tn=None, tk=None):
    slotted = b.ndim == 3
    if mode == 'nn':
        m, k = a.shape
        n = b.shape[-1] * (b.shape[0] if slotted else 1)
    elif mode == 'nt':
        m, k = a.shape
        n = b.shape[-2]
    else:
        k, m = a.shape
        n = b.shape[-1]
    ns = b.shape[-1] if slotted else None
    if mode == 'tn':
        tm = tm or _pick(m, (512, 256, 128))
        nw = n if not out_slots else n // out_slots
        if tn is None and tk is None:
            for cand in (512, 256):
                if nw % cand == 0 and (2 * k * (tm * a.dtype.itemsize + cand * b.dtype.itemsize)
                                       + 2 * tm * cand * 4 <= WHOLE_K_BYTES):
                    tn, tk = cand, k
                    break
        tn = tn or _pick(nw, (1024, 512, 256, 128))
        tk = tk or _pick(k, (1088, 1024, 768, 512, 384, 256, 128))
    else:
        tm = tm or _pick(m, (1088, 1024, 768, 512, 384, 256, 128))
        tn = tn or _pick(n if not (slotted and mode == 'nn') else ns, (1024, 512, 256, 128))
        extra = sum(x is not None for x in (add, relu2_of)) * 4
        if tk is None and not slotted and (2 * (tm * k * a.dtype.itemsize + k * tn * b.dtype.itemsize)
                                           + 2 * tm * tn * (4 + extra) <= WHOLE_K_BYTES):
            tk = k
        tk = tk or _pick(k if not (slotted and mode == 'nt') else ns, (1024, 512, 256, 128))
    gm, gn, gk = m // tm, n // tn, k // tk

    if mode == 'nn':
        a_spec = pl.BlockSpec((tm, tk), lambda i, j, kk: (i, kk))
        if slotted:
            per = ns // tn
            b_spec = pl.BlockSpec((None, tk, tn), lambda i, j, kk: (j // per, kk, j % per))
        else:
            b_spec = pl.BlockSpec((tk, tn), lambda i, j, kk: (kk, j))
        ca, cb = 1, 0
    elif mode == 'nt':
        a_spec = pl.BlockSpec((tm, tk), lambda i, j, kk: (i, kk))
        if slotted:
            per = ns // tk
            b_spec = pl.BlockSpec((None, tn, tk), lambda i, j, kk: (kk // per, j, kk % per))
        else:
            b_spec = pl.BlockSpec((tn, tk), lambda i, j, kk: (j, kk))
        ca, cb = 1, 1
    else:
        a_spec = pl.BlockSpec((tk, tm), lambda i, j, kk: (kk, i))
        b_spec = pl.BlockSpec((tk, tn), lambda i, j, kk: (kk, j))
        ca, cb = 0, 0
    in_specs = [a_spec, b_spec]
    args = [a, b]
    if bias is not None:
        in_specs.append(pl.BlockSpec((1, tn), lambda i, j, kk: (0, j)))
        args.append(bias)
    if add is not None:
        in_specs.append(pl.BlockSpec((tm, tn), lambda i, j, kk: (i, j)))
        args.append(add)
    if relu2_of is not None:
        in_specs.append(pl.BlockSpec((tm, tn), lambda i, j, kk: (i, j)))
        args.append(relu2_of)
    if out_slots:
        per_o = (n // out_slots) // tn
        out_spec = pl.BlockSpec((None, tm, tn), lambda i, j, kk: (j // per_o, i, j % per_o))
        out_shape = jax.ShapeDtypeStruct((out_slots, m, n // out_slots), out_dtype)
    else:
        out_spec = pl.BlockSpec((tm, tn), lambda i, j, kk: (i, j))
        out_shape = jax.ShapeDtypeStruct((m, n), out_dtype)
    grid = (gm, gn, gk)
    if colsum:
        assert mode == 'tn'
        out_spec = [out_spec, pl.BlockSpec((1, tn), lambda i, j, kk: (0, j))]
        out_shape = [out_shape, jax.ShapeDtypeStruct((1, n), F32)]

        def swapped(spec):
            return pl.BlockSpec(spec.block_shape, functools.partial(lambda j, i, kk, f: f(i, j, kk), f=spec.index_map))
        in_specs = [swapped(sp) for sp in in_specs]
        out_spec = [swapped(sp) for sp in out_spec]
        grid = (gn, gm, gk)

    def body(*refs):
        a_ref, b_ref = refs[0], refs[1]
        pos = 2
        bias_ref = add_ref = hid_ref = cs_ref = None
        if bias is not None:
            bias_ref = refs[pos]
            pos += 1
        if add is not None:
            add_ref = refs[pos]
            pos += 1
        if relu2_of is not None:
            hid_ref = refs[pos]
            pos += 1
        o_ref = refs[pos]
        pos += 1
        if colsum:
            cs_ref = refs[pos]
            pos += 1
        acc_ref = refs[pos] if gk > 1 else None
        bt = b_ref[...]
        p = _dot(a_ref[...].astype(BF16), bt.astype(BF16), ca, cb)
        kk = pl.program_id(2)

        if colsum:
            @pl.when(jnp.logical_and(pl.program_id(1) == 0, kk == 0))
            def _():
                cs_ref[...] = _colsum(bt.astype(F32))

            @pl.when(jnp.logical_and(pl.program_id(1) == 0, kk > 0))
            def _():
                cs_ref[...] += _colsum(bt.astype(F32))

        def finish(r):
            if bias_ref is not None:
                r = r + bias_ref[...]
            if act == 'relu2':
                r = jnp.square(jnp.maximum(r, 0.0))
            if hid_ref is not None:
                r = r * (2.0 * jnp.sqrt(hid_ref[...].astype(F32)))
            if add_ref is not None:
                r = r + add_scale * add_ref[...]
            o_ref[...] = r.astype(o_ref.dtype)

        if gk == 1:
            finish(p)
        else:
            @pl.when(kk == 0)
            def _():
                acc_ref[...] = p

            @pl.when(kk > 0)
            def _():
                acc_ref[...] += p

            @pl.when(kk == gk - 1)
            def _():
                finish(acc_ref[...])

    scratch = [pltpu.VMEM((tm, tn), F32)] if gk > 1 else []
    sem = ("arbitrary",) * 3 if colsum else ("parallel", "parallel", "arbitrary")
    return pl.pallas_call(body, name=name, grid=grid, in_specs=in_specs, out_specs=out_spec,
                          out_shape=out_shape, scratch_shapes=scratch, compiler_params=_params(sem))(*args)


def ln_fwd(name, h, mix, g, b):
    def fn(i, tm, rows, bcs):
        pre = ALPHA * rows[0] + rows[1]
        mu = jnp.mean(pre, axis=1, keepdims=True)
        xc = pre - mu
        var = jnp.mean(xc * xc, axis=1, keepdims=True)
        rstd = lax.rsqrt(var + LN_EPS)
        xhat = xc * rstd
        return (xhat * bcs[0] + bcs[1], xhat, rstd), ()
    return rowwise(name, fn, [h, mix], [g, b], [D_MODEL, D_MODEL, 1])


def ln_bwd(name, dy, xhat, rstd, g):
    def fn(i, tm, rows, bcs):
        dy_, xh, rs = rows
        dyg = dy_ * bcs[0]
        m1 = jnp.mean(dyg, axis=1, keepdims=True)
        m2 = jnp.mean(dyg * xh, axis=1, keepdims=True)
        dpre = rs * (dyg - m1 - xh * m2)
        return (dpre,), (_colsum(dy_ * xh), _colsum(dy_), _colsum(dpre))
    return rowwise(name, fn, [dy, xhat, rstd], [g], [D_MODEL], [D_MODEL] * 3)


_GELU_C = math.sqrt(2.0 / math.pi)


def gelu_fwd(name, x):
    def fn(i, tm, rows, bcs):
        v = rows[0]
        u = _GELU_C * (v + 0.044715 * (v * v * v))
        return (0.5 * v * (1.0 + jnp.tanh(u)),), ()
    return rowwise(name, fn, [x], [], [x.shape[1]])[0]


def gelu_bwd(name, dy, x):
    def fn(i, tm, rows, bcs):
        v = rows[1]
        th = jnp.tanh(_GELU_C * (v + 0.044715 * (v * v * v)))
        dg = 0.5 * (1.0 + th) + 0.5 * v * (1.0 - th * th) * (_GELU_C * (1.0 + 3.0 * 0.044715 * v * v))
        return (rows[0] * dg,), ()
    return rowwise(name, fn, [dy, x], [], [x.shape[1]])[0]


def glu_gate(name, y, gp):
    def fn(i, tm, rows, bcs):
        return (rows[0] * jax.nn.sigmoid(rows[1]),), ()
    return rowwise(name, fn, [y, gp], [], [y.shape[1]], out_dtypes=[BF16])[0]


def glu_gate_bwd(name, da, y, gp):
    def fn(i, tm, rows, bcs):
        s = jax.nn.sigmoid(rows[2])
        dgp = rows[0] * rows[1] * s * (1.0 - s)
        return (dgp, rows[0] * s), (_colsum(dgp),)
    w = y.shape[1]
    return rowwise(name, fn, [da, y, gp], [], [w, w], [w])


def loss_head(name, h, tgt, lp, seq):
    def fn(i, tm, rows, bcs):
        pos = (i * tm + lax.broadcasted_iota(jnp.int32, (tm, 1), 0)) % lp
        valid = jnp.logical_and(pos >= N_META, pos < N_META + seq)
        err = jnp.where(valid, rows[0] - rows[1], 0.0)
        part = 0.5 * jnp.sum(jnp.mean(err * err, axis=1, keepdims=True), axis=0, keepdims=True)
        return (err * (1.0 / D_MODEL),), (jnp.broadcast_to(part, (1, LANES)),)
    return rowwise(name, fn, [h, tgt], [], [D_MODEL], [LANES])


def adamw(name, w, gs, m, v):
    ng = len(gs)

    def fn(i, tm, rows, bcs):
        w_ = rows[0]
        g = rows[1] if ng == 1 else rows[1] + rows[2]
        m_, v_ = rows[1 + ng], rows[2 + ng]
        m_ = ADAM_B1 * m_ + (1.0 - ADAM_B1) * g
        v_ = ADAM_B2 * v_ + (1.0 - ADAM_B2) * jnp.square(g)
        m_hat = m_ / (1.0 - ADAM_B1 ** ADAM_STEP)
        v_hat = v_ / (1.0 - ADAM_B2 ** ADAM_STEP)
        delta = -ADAM_LR * (m_hat / (jnp.sqrt(v_hat) + ADAM_EPS) + ADAM_WD * w_)
        return (g, delta, m_, v_), ()
    wd = w.shape[1]
    return rowwise(name, fn, [w] + list(gs) + [m, v], [], [wd] * 4)


def s5_matrices(lam_re, lam_im, log_dt, b_re, b_im, c_re, c_im, d):
    c, hh, gg, pp = S5_CHUNK, S5_GROUP, S5_GROUPS, S5_STATE
    dt = jnp.exp(log_dt)[:, None]
    tau = jnp.arange(c + 1, dtype=F32)[None, :, None]
    mag = jnp.exp(tau * (lam_re * dt)[:, None, :])
    ang = tau * (lam_im * dt)[:, None, :]
    pw_re, pw_im = mag * jnp.cos(ang), mag * jnp.sin(ang)
    nr, ni = pw_re[:, 1] - 1.0, pw_im[:, 1]
    den = lam_re * lam_re + lam_im * lam_im
    cf_re = (nr * lam_re + ni * lam_im) / den
    cf_im = (ni * lam_re - nr * lam_im) / den
    bb_re = cf_re[:, :, None] * b_re - cf_im[:, :, None] * b_im
    bb_im = cf_re[:, :, None] * b_im + cf_im[:, :, None] * b_re
    ct_re, ct_im = c_re.transpose(0, 2, 1)[:, :, None, :], c_im.transpose(0, 2, 1)[:, :, None, :]
    pt_re, pt_im = pw_re.transpose(0, 2, 1)[:, :, :, None], pw_im.transpose(0, 2, 1)[:, :, :, None]
    cp_re = ct_re * pt_re - ct_im * pt_im
    cp_im = ct_re * pt_im + ct_im * pt_re
    hp = lax.Precision.HIGHEST
    kmat = (jnp.einsum('gpk,gpn->gkn', bb_re, cp_re[:, :, :c].reshape(gg, pp, c * hh), precision=hp)
            - jnp.einsum('gpk,gpn->gkn', bb_im, cp_im[:, :, :c].reshape(gg, pp, c * hh), precision=hp))
    rows = [jnp.pad(kmat[:, :, :(c - j) * hh], ((0, 0), (0, 0), (j * hh, 0))) for j in range(c)]
    a1 = jnp.stack(rows, axis=1).reshape(gg, c * hh, c * hh)
    a1 = a1 + jnp.eye(c * hh, dtype=F32)[None] * jnp.tile(d, (1, c))[:, None, :]
    a2 = jnp.concatenate([cp_re[:, :, 1:].reshape(gg, pp, c * hh), -cp_im[:, :, 1:].reshape(gg, pp, c * hh)], axis=1)
    rv_re, rv_im = pw_re[:, :c][:, ::-1, None, :], pw_im[:, :c][:, ::-1, None, :]
    bt_re, bt_im = bb_re.transpose(0, 2, 1)[:, None], bb_im.transpose(0, 2, 1)[:, None]
    a3 = jnp.concatenate([rv_re * bt_re - rv_im * bt_im, rv_re * bt_im + rv_im * bt_re], axis=-1)
    a3 = a3.reshape(gg, c * hh, 2 * pp)
    are = jnp.concatenate([pw_re[:, c], pw_re[:, c]], axis=-1)[:, None, :]
    aim = jnp.concatenate([-pw_im[:, c], pw_im[:, c]], axis=-1)[:, None, :]
    return a1, a2, a3, are, aim


S5_LEVELS = 8


def s5_scan_powers(lam_re, lam_im, log_dt):
    dt = jnp.exp(log_dt)[:, None]
    e = (S5_CHUNK * 2.0 ** jnp.arange(S5_LEVELS, dtype=F32))[:, None, None]
    mag = jnp.exp(e * (lam_re * dt)[None])
    ang = e * (lam_im * dt)[None]
    pr, pi = mag * jnp.cos(ang), mag * jnp.sin(ang)
    pre = jnp.concatenate([pr, pr], axis=-1).transpose(1, 0, 2)
    pim = jnp.concatenate([-pi, pi], axis=-1).transpose(1, 0, 2)
    return pre, pim


def _s5_scan(x, pre, pim, reverse):
    n = x.shape[0]
    rowi = lax.broadcasted_iota(jnp.int32, (n, 1), 0)
    t = x
    for k in range(S5_LEVELS):
        shift = 2 ** k
        if shift >= n:
            break
        p_re, p_im = pre[k:k + 1, :], pim[k:k + 1, :]
        if reverse:
            far = jnp.where(rowi < n - shift, pltpu.roll(t, n - shift, 0), 0.0)
            t = t + (p_re * far + pltpu.roll(p_im * far, S5_STATE, 1))
        else:
            far = jnp.where(rowi >= shift, pltpu.roll(t, shift, 0), 0.0)
            t = t + (p_re * far + p_im * pltpu.roll(far, S5_STATE, 1))
    return t


def s5_fwd(ug, a1, a2, a3, pre, pim, nseq):
    g, nc, cw = ug.shape
    per = nc // nseq
    sw = 2 * S5_STATE
    assert per <= 2 ** S5_LEVELS

    def body(u_ref, a1_ref, a2_ref, a3_ref, pre_ref, pim_ref, y_ref, s_ref):
        u = u_ref[...]
        x = _dot3(u, a3_ref[...], 1, 0)
        first = lax.broadcasted_iota(jnp.int32, (per, 1), 0) == 0
        for b in range(nseq):
            t = _s5_scan(x[b * per:(b + 1) * per], pre_ref[...], pim_ref[...], False)
            s_ref[pl.ds(b * per, per), :] = jnp.where(first, 0.0, pltpu.roll(t, 1, 0))
        y_ref[...] = _dot3(u, a1_ref[...], 1, 0) + _dot3(s_ref[...], a2_ref[...], 1, 0)

    def spec(shape):
        return pl.BlockSpec((None,) + shape, lambda i: (i, 0, 0))
    lv = (S5_LEVELS, sw)
    return pl.pallas_call(
        body, name="s5_fwd", grid=(g,),
        in_specs=[spec((nc, cw)), spec((cw, cw)), spec((sw, cw)), spec((cw, sw)), spec(lv), spec(lv)],
        out_specs=[spec((nc, cw)), spec((nc, sw))],
        out_shape=[jax.ShapeDtypeStruct((g, nc, cw), F32), jax.ShapeDtypeStruct((g, nc, sw), F32)],
        compiler_params=_params(("parallel",)))(ug, a1, a2, a3, pre, pim)


def s5_bwd(dyg, ug, sst, a1, a2, a3, pre, pim, nseq, ex=None):
    g, nc, cw = ug.shape
    per = nc // nseq
    sw = 2 * S5_STATE
    split, ex_start, ex_finish, exkw = _hidden_exchange(ex, 8, 6, (g,))

    def body(*refs):
        ((dy_ref, u_ref, s_ref, a1_ref, a2_ref, a3_ref, pre_ref, pim_ref),
         (du_ref, da1_ref, da2_ref, da3_ref, dare_ref, daim_ref), ex_parts, (dx_ref,)) = split(refs)
        ex_start(ex_parts)
        dy = dy_ref[...]
        u = u_ref[...]
        gd = _dot3(dy, a2_ref[...], 1, 1)
        s_all = s_ref[...]
        da2_ref[...] = _dot3(s_all, dy, 0, 0)
        last = lax.broadcasted_iota(jnp.int32, (per, 1), 0) == per - 1
        for b in range(nseq):
            gs = _s5_scan(gd[b * per:(b + 1) * per], pre_ref[...], pim_ref[...], True)
            dx_ref[pl.ds(b * per, per), :] = jnp.where(last, 0.0, pltpu.roll(gs, per - 1, 0))
        dx = dx_ref[...]
        dare_ref[...] = _colsum(dx * s_all)
        daim_ref[...] = _colsum(dx * pltpu.roll(s_all, S5_STATE, 1))
        du_ref[...] = _dot3(dy, a1_ref[...], 1, 1) + _dot3(dx, a3_ref[...], 1, 1)
        da1_ref[...] = _dot3(u, dy, 0, 0)
        da3_ref[...] = _dot3(u, dx, 0, 0)
        ex_finish(ex_parts)

    def spec(shape):
        return pl.BlockSpec((None,) + shape, lambda i: (i, 0, 0))
    sds = jax.ShapeDtypeStruct
    return pl.pallas_call(
        body, name="s5_bwd", grid=(g,),
        in_specs=[spec((nc, cw)), spec((nc, cw)), spec((nc, sw)), spec((cw, cw)), spec((sw, cw)), spec((cw, sw)),
                  spec((S5_LEVELS, sw)), spec((S5_LEVELS, sw))] + exkw['in_specs'],
        out_specs=[spec((nc, cw)), spec((cw, cw)), spec((sw, cw)), spec((cw, sw)), spec((1, sw)), spec((1, sw))]
        + exkw['out_specs'],
        out_shape=[sds((g, nc, cw), F32), sds((g, cw, cw), F32), sds((g, sw, cw), F32), sds((g, cw, sw), F32),
                   sds((g, 1, sw), F32), sds((g, 1, sw), F32)] + exkw['out_shape'],
        input_output_aliases=exkw['aliases'], scratch_shapes=[pltpu.VMEM((nc, sw), F32)] + exkw['scratch'],
        compiler_params=_params(("arbitrary",)))(dyg, ug, sst, a1, a2, a3, pre, pim, *exkw['ins'])


def _to_groups(u):
    t = u.shape[0]
    nc = t // S5_CHUNK
    return u.reshape(nc, S5_CHUNK, S5_GROUPS, S5_GROUP).transpose(2, 0, 1, 3).reshape(S5_GROUPS, nc, -1)


def _from_groups(yg):
    g, nc, _ = yg.shape
    return yg.reshape(g, nc, S5_CHUNK, S5_GROUP).transpose(1, 2, 0, 3).reshape(nc * S5_CHUNK, g * S5_GROUP)


def _sb_masks():
    row = lax.broadcasted_iota(jnp.int32, (SB_BLOCK, SB_BLOCK), 0)
    col = lax.broadcasted_iota(jnp.int32, (SB_BLOCK, SB_BLOCK), 1)
    lane = lax.broadcasted_iota(jnp.int32, (1, LANES), 1)
    return row, col, [lane < 64, lane >= 64]


def _sb_weights(z, vis, later_of, after):
    sp = jnp.maximum(z, 0.0) + jnp.log1p(jnp.exp(-jnp.abs(z)))
    lk = jnp.where(vis, -sp, 0.0)
    rs = jnp.sum(lk, axis=1, keepdims=True)
    later = _dot_exact_rhs(lk, after, 1, 0) + later_of(rs)
    lb = z - sp
    w = jnp.where(vis, jnp.exp(lb + later), 0.0)
    return w, rs, lb


def _hidden_exchange(ex, n_in, n_out, grid):
    ni, no = (len(ex.ins), len(ex.out_shapes)) if ex else (0, 0)

    def split(refs):
        a, b = n_in + ni, n_in + ni + n_out
        end = len(refs) - (2 if ex else 0)
        return refs[:n_in], refs[a:b], (refs[n_in:a], refs[b:b + no], refs[end:]), refs[b + no:end]

    def at_step(which, fn, parts):
        if ex is not None:
            ids = [pl.program_id(d) == (0 if which == 'first' else g - 1) for d, g in enumerate(grid)]
            cond = ids[0]
            for c in ids[1:]:
                cond = jnp.logical_and(cond, c)
            pl.when(cond)(lambda: fn(parts[0], parts[1], *parts[2]))

    anyspec = pl.BlockSpec(memory_space=pl.ANY)
    kw = dict(in_specs=[anyspec] * ni, out_specs=[anyspec] * no, out_shape=list(ex.out_shapes) if ex else [],
              aliases={n_in + i: n_out + j for i, j in (ex.aliases().items() if ex else ())},
              scratch=ex.sems() if ex else [], ins=list(ex.ins) if ex else [])
    start = functools.partial(at_step, 'first', ex.start if ex else None)
    finish = functools.partial(at_step, 'last', ex.finish if ex else None)
    return split, start, finish, kw


def _lane_tile(x, tl):
    return x[:, tl * LANES:(tl + 1) * LANES]


SB_TILES = 2


def _two_heads(x):
    return jnp.concatenate([x, x], axis=0)


def _two_heads_masks():
    row = lax.broadcasted_iota(jnp.int32, (2 * SB_BLOCK, LANES), 0)
    col = lax.broadcasted_iota(jnp.int32, (2 * SB_BLOCK, LANES), 1)
    return (col // 64) == (row // SB_BLOCK), col < (row % SB_BLOCK)


def _own_lanes(r, lanes0):
    return jnp.where(lanes0, r[:SB_BLOCK], r[SB_BLOCK:])


def _sb_specs(lp, nseq):
    wd = SB_TILES * LANES
    off = [(S5_WIDTH + i * SB_WIDTH) // wd for i in range(3)]
    blk = (lp, wd)
    qkv = [pl.BlockSpec(blk, functools.partial(lambda b, h, o: (b, o + h), o=o)) for o in off]
    return (blk, qkv, pl.BlockSpec(blk, lambda b, h: (b, h)),
            pl.BlockSpec((None, None, 8, LANES), lambda b, h: (b, h, 0, 0)))


def attn_fwd(proj, nseq, lp, ex=None):
    nqb = lp // SB_BLOCK
    t = proj.shape[0]
    nstep = SB_WIDTH // (SB_TILES * LANES)
    split, ex_start, ex_finish, exkw = _hidden_exchange(ex, 3, 3, (nseq, nstep))

    def body(*refs):
        (q_ref, k_ref, v_ref), (o_ref, tot_ref, first_ref), ex_parts, _ = split(refs)
        ex_start(ex_parts)
        row, col, hmask = _sb_masks()
        after = (row > col).astype(BF16)
        tri = col < row
        lane8 = lax.broadcasted_iota(jnp.int32, (8, LANES), 1)

        hm2, tri2 = _two_heads_masks()

        def qloop(qi, firsts):
            q0 = pl.multiple_of(qi * SB_BLOCK, SB_BLOCK)
            q = q_ref[pl.ds(q0, SB_BLOCK), :] * 0.125
            qm = [jnp.where(hm2, _two_heads(_lane_tile(q, tl)), 0.0).astype(BF16) for tl in range(SB_TILES)]

            def alive(carry):
                return jnp.logical_and(carry[0] <= qi, carry[1] > 0)

            def kstep(carry):
                s = carry[0]
                accs, lats = list(carry[2:2 + SB_TILES]), list(carry[2 + SB_TILES:])
                kj = qi - s
                k0 = pl.multiple_of(kj * SB_BLOCK, SB_BLOCK)
                k = k_ref[pl.ds(k0, SB_BLOCK), :].astype(BF16)
                v = v_ref[pl.ds(k0, SB_BLOCK), :].astype(BF16)
                vis = jnp.logical_or(kj < qi, tri2)
                for tl in range(SB_TILES):
                    z = _dot(qm[tl], _lane_tile(k, tl), 1, 1)
                    w, rs, _ = _sb_weights(z, vis, functools.partial(lambda rs, lat: lat, lat=lats[tl]), after)
                    accs[tl] = accs[tl] + _own_lanes(_dot(w.astype(BF16), _lane_tile(v, tl), 1, 0), hmask[0])
                    lats[tl] = lats[tl] + rs
                top = functools.reduce(jnp.maximum, lats)
                go = (jnp.max(top) > SB_DEAD).astype(jnp.int32)
                return (s + 1, go, *accs, *lats)

            res = lax.while_loop(alive, kstep, (jnp.int32(0), jnp.int32(1),
                                                *[jnp.zeros((SB_BLOCK, LANES), F32)] * SB_TILES,
                                                *[jnp.zeros((2 * SB_BLOCK, 1), F32)] * SB_TILES))
            accs, lats = res[2:2 + SB_TILES], res[2 + SB_TILES:]
            for tl in range(SB_TILES):
                cols = slice(tl * LANES, (tl + 1) * LANES)
                o_ref[pl.ds(q0, SB_BLOCK), cols] = accs[tl].astype(o_ref.dtype)
                tot_ref[pl.ds(q0, SB_BLOCK), cols] = _own_lanes(jnp.broadcast_to(lats[tl], (2 * SB_BLOCK, LANES)),
                                                                hmask[0])
            return jnp.where(lane8 == qi, (qi - res[0] + 1).astype(F32), firsts)

        first_ref[...] = lax.fori_loop(0, nqb, qloop, jnp.zeros((8, LANES), F32))
        ex_finish(ex_parts)

    blk, qkv, out, vec = _sb_specs(lp, nseq)
    return pl.pallas_call(
        body, name="attn_fwd", grid=(nseq, nstep),
        in_specs=qkv + exkw['in_specs'], out_specs=[out, out, vec] + exkw['out_specs'],
        out_shape=[jax.ShapeDtypeStruct((t, SB_WIDTH), BF16), jax.ShapeDtypeStruct((t, SB_WIDTH), F32)]
        + [jax.ShapeDtypeStruct((nseq, nstep, 8, LANES), F32)] + exkw['out_shape'],
        input_output_aliases=exkw['aliases'], scratch_shapes=exkw['scratch'],
        compiler_params=_params(("arbitrary", "arbitrary")))(proj, proj, proj, *exkw['ins'])


def attn_bwd(proj, tot, first, do, nseq, lp, ex=None):
    nqb = lp // SB_BLOCK
    t = proj.shape[0]
    nstep = SB_WIDTH // (SB_TILES * LANES)
    split, ex_start, ex_finish, exkw = _hidden_exchange(ex, 6, 3, (nseq, nstep))

    def body(*refs):
        (q_ref, k_ref, v_ref, tot_ref, first_ref, do_ref), (dq_ref, dk_ref, dv_ref), ex_parts, _ = split(refs)
        ex_start(ex_parts)
        row, col, hmask = _sb_masks()
        after = (row > col).astype(BF16)
        before = (row < col).astype(BF16)
        tri = col < row
        lane8 = lax.broadcasted_iota(jnp.int32, (8, LANES), 1)
        firsts = first_ref[...]
        dk_ref[...] = jnp.zeros(dk_ref.shape, F32)
        dv_ref[...] = jnp.zeros(dv_ref.shape, F32)
        hm2, tri2 = _two_heads_masks()

        def qloop(qi, _):
            first = jnp.max(jnp.where(lane8 == qi, firsts, 0.0)).astype(jnp.int32)
            first = jnp.clip(first, 0, qi)
            q0 = pl.multiple_of(qi * SB_BLOCK, SB_BLOCK)
            q = q_ref[pl.ds(q0, SB_BLOCK), :] * 0.125
            do_ = do_ref[pl.ds(q0, SB_BLOCK), :]
            tot_ = tot_ref[pl.ds(q0, SB_BLOCK), :]
            qm = [jnp.where(hm2, _two_heads(_lane_tile(q, tl)), 0.0).astype(BF16) for tl in range(SB_TILES)]
            dom = [jnp.where(hm2, _two_heads(_lane_tile(do_, tl)), 0.0).astype(BF16) for tl in range(SB_TILES)]
            tot = [jnp.max(jnp.where(hm2, _two_heads(_lane_tile(tot_, tl)), -jnp.inf), axis=1, keepdims=True)
                   for tl in range(SB_TILES)]

            def kloop(kj, carry):
                dqs = list(carry[:SB_TILES])
                pre = list(carry[SB_TILES:2 * SB_TILES])
                gpre = list(carry[2 * SB_TILES:])
                k0 = pl.multiple_of(kj * SB_BLOCK, SB_BLOCK)
                k = k_ref[pl.ds(k0, SB_BLOCK), :].astype(BF16)
                v = v_ref[pl.ds(k0, SB_BLOCK), :].astype(BF16)
                vis = jnp.logical_or(kj < qi, tri2)
                for tl in range(SB_TILES):
                    kt, vt = _lane_tile(k, tl), _lane_tile(v, tl)
                    z = _dot(qm[tl], kt, 1, 1)
                    later_of = functools.partial(lambda rs, t_, p_: t_ - p_ - rs, t_=tot[tl], p_=pre[tl])
                    w, rs, lb = _sb_weights(z, vis, later_of, after)
                    dw = _dot(dom[tl], vt, 1, 1)
                    g = dw * w
                    dlk = _dot_exact_rhs(g, before, 1, 0) + gpre[tl]
                    sig = jnp.exp(lb)
                    dz = jnp.where(vis, g * (1.0 - sig) - dlk * sig, 0.0).astype(BF16)
                    dqs[tl] = dqs[tl] + _own_lanes(_dot(dz, kt, 1, 0), hmask[0])
                    cols = slice(tl * LANES, (tl + 1) * LANES)
                    dk_ref[pl.ds(k0, SB_BLOCK), cols] += _dot(dz, qm[tl], 0, 0)
                    dv_ref[pl.ds(k0, SB_BLOCK), cols] += _dot(w.astype(BF16), dom[tl], 0, 0)
                    pre[tl] = pre[tl] + rs
                    gpre[tl] = gpre[tl] + jnp.sum(g, axis=1, keepdims=True)
                return (*dqs, *pre, *gpre)

            z1 = jnp.zeros((2 * SB_BLOCK, 1), F32)
            res = lax.fori_loop(first, qi + 1, kloop,
                                (*[jnp.zeros((SB_BLOCK, LANES), F32)] * SB_TILES, *[z1] * (2 * SB_TILES)))
            for tl in range(SB_TILES):
                dq_ref[pl.ds(q0, SB_BLOCK), tl * LANES:(tl + 1) * LANES] = res[tl] * 0.125
            return 0

        lax.fori_loop(0, nqb, qloop, 0)
        ex_finish(ex_parts)

    blk, qkv, out, vec = _sb_specs(lp, nseq)
    return pl.pallas_call(
        body, name="attn_bwd", grid=(nseq, nstep),
        in_specs=qkv + [out, vec, out] + exkw['in_specs'], out_specs=[out] * 3 + exkw['out_specs'],
        out_shape=[jax.ShapeDtypeStruct((t, SB_WIDTH), F32)] * 3 + exkw['out_shape'],
        input_output_aliases=exkw['aliases'], scratch_shapes=exkw['scratch'],
        compiler_params=_params(("arbitrary", "arbitrary")))(proj, proj, proj, tot, first, do, *exkw['ins'])


def _hg_loop(nch, step, init):
    assert nch % HG_UNROLL == 0

    def trip(i, carry):
        for u in range(HG_UNROLL):
            carry = step(i * HG_UNROLL + u, carry)
        return carry
    return lax.fori_loop(0, nch // HG_UNROLL, trip, init)


def _hg_gates(fc, lb):
    sg = jax.nn.sigmoid(fc)
    f = lb + (1.0 - lb) * sg
    return sg, f


def _hg_decay(f, incl):
    bcum = _dot_exact_lhs(incl, jnp.log(f))
    return bcum, bcum[HG_CHUNK - 1:HG_CHUNK, :]


def _dot_exact_lhs(m, x):
    hi, mid, lo = _split3(x)
    return _dot(m, hi, 1, 0) + (_dot(m, mid, 1, 0) + _dot(m, lo, 1, 0))


def hgrn_fwd(proj, lb, ng, nseq, lp, ex=None):
    nch = lp // HG_CHUNK
    t = proj.shape[0]
    c = HG_CHUNK

    split, ex_start, ex_finish, exkw = _hidden_exchange(ex, 6, 3, (HG_HEADS,))

    def body(*refs):
        (q_ref, f_ref, v_ref, g_ref, lb_ref, ng_ref), (o_ref, og_ref, st_ref), ex_parts, _ = split(refs)
        ex_start(ex_parts)
        lbv, ngv = lb_ref[...], ng_ref[...]

        rr = nseq * c
        seq_r = lax.broadcasted_iota(jnp.int32, (rr, 1), 0) // c
        bd_row = lax.broadcasted_iota(jnp.int32, (rr, rr), 0)
        bd_col = lax.broadcasted_iota(jnp.int32, (rr, rr), 1)
        causal = jnp.logical_and(bd_row // c == bd_col // c, bd_col <= bd_row)
        incl = causal.astype(BF16)

        def stacked(ref, ci):
            return jnp.concatenate([ref[pl.ds(pl.multiple_of(b * lp + ci * c, c), c), :] for b in range(nseq)], axis=0)

        def wide(x):
            return jnp.concatenate([jnp.where(seq_r == b, x, jnp.zeros_like(x)) for b in range(nseq)], axis=1)

        def step(ci, st):
            for b in range(nseq):
                st_ref[b, ci] = st[:, b * HG_DK:(b + 1) * HG_DK]
            _, f = _hg_gates(stacked(f_ref, ci), lbv)
            bcum = _dot_exact_lhs(incl, jnp.log(f))
            lasts = [bcum[b * c + c - 1:b * c + c, :] for b in range(nseq)]
            blast = jnp.concatenate([jnp.broadcast_to(x, (c, LANES)) for x in lasts], axis=0)
            kk = 1.0 - f
            vc = stacked(v_ref, ci).astype(BF16)
            qd = (stacked(q_ref, ci) * jnp.exp(bcum)).astype(BF16)
            kd = (kk * jnp.exp(-bcum)).astype(BF16)
            a = jnp.where(causal, _dot(qd, kd, 1, 1), 0.0)
            o = _dot(a.astype(BF16), vc, 1, 0) + _dot(wide(qd), st.astype(BF16), 1, 1)
            kend = (kk * jnp.exp(blast - bcum)).astype(BF16)
            st = st * jnp.exp(jnp.concatenate(lasts, axis=1)) + _dot(vc, wide(kend), 0, 0)
            r = lax.rsqrt(jnp.mean(o * o, axis=1, keepdims=True) + RMS_EPS)
            gc = stacked(g_ref, ci)
            og = (o * r * ngv * (gc * jax.nn.sigmoid(gc))).astype(og_ref.dtype)
            for b in range(nseq):
                rows = pl.ds(pl.multiple_of(b * lp + ci * c, c), c)
                o_ref[rows, :] = o[b * c:(b + 1) * c]
                og_ref[rows, :] = og[b * c:(b + 1) * c]
            return st

        _hg_loop(nch, step, jnp.zeros((HG_DK, nseq * HG_DK), F32))
        ex_finish(ex_parts)

    blk = (nseq * lp, LANES)
    h = HG_HEADS
    return pl.pallas_call(
        body, name="hgrn_fwd", grid=(h,),
        in_specs=[pl.BlockSpec(blk, lambda hh: (0, hh)),
                  pl.BlockSpec(blk, lambda hh: (0, h + hh)),
                  pl.BlockSpec(blk, lambda hh: (0, 2 * h + hh)),
                  pl.BlockSpec(blk, lambda hh: (0, 3 * h + hh)),
                  pl.BlockSpec((1, LANES), lambda hh: (0, hh)),
                  pl.BlockSpec((1, LANES), lambda hh: (0, hh))] + exkw['in_specs'],
        out_specs=[pl.BlockSpec(blk, lambda hh: (0, hh)),
                   pl.BlockSpec(blk, lambda hh: (0, hh)),
                   pl.BlockSpec((nseq, None, nch, HG_DK, HG_DK), lambda hh: (0, hh, 0, 0, 0))] + exkw['out_specs'],
        out_shape=[jax.ShapeDtypeStruct((t, D_MODEL), F32), jax.ShapeDtypeStruct((t, D_MODEL), BF16),
                   jax.ShapeDtypeStruct((nseq, h, nch, HG_DK, HG_DK), F32)] + exkw['out_shape'],
        input_output_aliases=exkw['aliases'], scratch_shapes=exkw['scratch'],
        compiler_params=_params(("arbitrary",)))(proj, proj, proj, proj, lb, ng, *exkw['ins'])


def hgrn_bwd(proj, o, dog, states, lb, ng, nseq, lp):
    nch = lp // HG_CHUNK
    t = proj.shape[0]
    c = HG_CHUNK

    def body(q_ref, f_ref, v_ref, g_ref, o_ref, dog_ref, st_ref, lb_ref, ng_ref,
             dq_ref, df_ref, dv_ref, dg_ref, dlb_ref, dng_ref):
        row = lax.broadcasted_iota(jnp.int32, (c, c), 0)
        col = lax.broadcasted_iota(jnp.int32, (c, c), 1)
        causal = col <= row
        incl = causal.astype(BF16)
        from_here = (col >= row).astype(BF16)
        last = lax.broadcasted_iota(jnp.int32, (c, 1), 0) == c - 1
        lbv, ngv = lb_ref[...], ng_ref[...]

        def step(s, carry):
            dst, dlb, dng = carry
            ci = nch - 1 - s
            r0 = pl.multiple_of(ci * c, c)
            rows = pl.ds(r0, c)
            oc, gc, dogc = o_ref[rows, :], g_ref[rows, :], dog_ref[rows, :]
            r = lax.rsqrt(jnp.mean(oc * oc, axis=1, keepdims=True) + RMS_EPS)
            sgg = jax.nn.sigmoid(gc)
            silu = gc * sgg
            dg_ref[rows, :] = dogc * (oc * r * ngv) * (sgg * (1.0 + gc * (1.0 - sgg)))
            don = dogc * silu
            dng = dng + _colsum(don * oc * r)
            tt = don * ngv
            do = r * (tt - oc * (r * r) * jnp.mean(tt * oc, axis=1, keepdims=True))
            st = st_ref[ci]
            fc = f_ref[rows, :]
            sg, f = _hg_gates(fc, lbv)
            bcum, blast = _hg_decay(f, incl)
            kk = 1.0 - f
            qc, vf = q_ref[rows, :], v_ref[rows, :]
            pdec = jnp.exp(bcum)
            ndec = jnp.exp(-bcum)
            edec = jnp.exp(blast - bcum)
            eb = jnp.exp(blast)
            qd_f, kd_f, kend_f = qc * pdec, kk * ndec, kk * edec
            qd, kd, kend = qd_f.astype(BF16), kd_f.astype(BF16), kend_f.astype(BF16)
            vc, dob, stb, dstb = vf.astype(BF16), do.astype(BF16), st.astype(BF16), dst.astype(BF16)
            a = jnp.where(causal, _dot(qd, kd, 1, 1), 0.0).astype(BF16)
            da = jnp.where(causal, _dot(dob, vc, 1, 1), 0.0).astype(BF16)
            dv_ref[rows, :] = _dot(a, dob, 0, 0) + _dot(kend, dstb, 1, 1)
            dqd = _dot(da, kd, 1, 0) + _dot(dob, stb, 1, 0)
            dkd = _dot(da, qd, 0, 0)
            dkend = _dot(vc, dstb, 1, 0)
            dq_ref[rows, :] = dqd * pdec
            dblast = _colsum(dkend * kend_f) + eb * _colsum(dst * st)
            dbcum = dqd * qd_f - dkd * kd_f - dkend * kend_f
            dbcum = dbcum + jnp.where(last, dblast, 0.0)
            dlf = _dot_exact_lhs(from_here, dbcum)
            df = dlf / f - (dkd * ndec + dkend * edec)
            df_ref[rows, :] = df * (1.0 - lbv) * sg * (1.0 - sg)
            dlb = dlb + _colsum(df * (1.0 - sg))
            dst = dst * eb + _dot(dob, qd, 0, 0)
            return dst, dlb, dng

        z = jnp.zeros((1, LANES), F32)
        _, dlb, dng = _hg_loop(nch, step, (jnp.zeros((HG_DK, HG_DK), F32), z, z))
        dlb_ref[...] = dlb
        dng_ref[...] = dng

    blk = (lp, LANES)
    h = HG_HEADS
    vec = pl.BlockSpec((1, LANES), lambda b, hh: (0, hh))
    svec = pl.BlockSpec((None, 1, LANES), lambda b, hh: (b, 0, hh))
    return pl.pallas_call(
        body, name="hgrn_bwd", grid=(nseq, h),
        in_specs=[pl.BlockSpec(blk, lambda b, hh: (b, hh)),
                  pl.BlockSpec(blk, lambda b, hh: (b, h + hh)),
                  pl.BlockSpec(blk, lambda b, hh: (b, 2 * h + hh)),
                  pl.BlockSpec(blk, lambda b, hh: (b, 3 * h + hh)),
                  pl.BlockSpec(blk, lambda b, hh: (b, hh)),
                  pl.BlockSpec(blk, lambda b, hh: (b, hh)),
                  pl.BlockSpec((None, None, nch, HG_DK, HG_DK), lambda b, hh: (b, hh, 0, 0, 0)),
                  vec, vec],
        out_specs=[pl.BlockSpec(blk, lambda b, hh: (b, hh))] * 4 + [svec, svec],
        out_shape=[jax.ShapeDtypeStruct((t, D_MODEL), F32)] * 4
        + [jax.ShapeDtypeStruct((nseq, 1, D_MODEL), F32)] * 2,
        compiler_params=_params(("parallel", "parallel")))(proj, proj, proj, proj, o, dog, states, lb, ng)


def _lower_bound(gamma):
    p = jax.nn.softmax(gamma, axis=0)
    return (jnp.cumsum(p, axis=0) - p[0])[1][None, :]


def local_step(x, tgt, w, hooks=None):
    nseq, seq, _ = x.shape
    lp = -(-(N_META + seq) // SB_BLOCK) * SB_BLOCK
    t = nseq * lp
    pad = lp - N_META - seq
    h0 = jnp.concatenate([jnp.broadcast_to(w['meta'][None], (nseq, N_META, D_MODEL)), x,
                          jnp.zeros((nseq, pad, D_MODEL), F32)], axis=1).reshape(t, D_MODEL)
    tgt_p = jnp.pad(tgt, ((0, 0), (N_META, pad), (0, 0))).reshape(t, D_MODEL)
    row = lambda v: v.reshape(1, -1)
    g = {}

    proj = matmul("in_ab", h0, w['w_in_ab'], 'nn')
    att = attn_fwd(proj, nseq, lp, ex=hooks.gather_late if hooks else None)
    b_out, sb_tot, sb_first = att[:3]
    if hooks:
        w = {**w, **hooks.late_weights(att[3:])}
    s5_names = ['s5_lam_re', 's5_lam_im', 's5_log_dt', 's5_b_re', 's5_b_im', 's5_c_re', 's5_c_im', 's5_d']
    mats, mats_vjp = jax.vjp(s5_matrices, *[w[n][0] for n in s5_names])
    ug = _to_groups(proj[:, :S5_WIDTH])
    s5_pows = s5_scan_powers(*[w[n][0] for n in s5_names[:3]])
    yg, sst = s5_fwd(ug, *mats[:3], *s5_pows, nseq)
    ys5 = _from_groups(yg)
    y = gelu_fwd("gelu", ys5)
    gp = matmul("glu", y, w['s5_w_glu'], 'nn', bias=w['s5_b_glu'])
    a_out = glu_gate("glu_gate", y, gp)
    cat = jnp.concatenate([a_out, b_out], axis=1)
    mix0 = matmul("out_ab", cat, w['w_out_ab'], 'nn')
    h1, xh1, rs1 = ln_fwd("ln_mix0", h0, mix0, row(w['ln_mix_g'][0]), row(w['ln_mix_b'][0]))
    hid0 = matmul("up0", h1, w['mlp_w_up'][0], 'nn', bias=row(w['mlp_b_up'][0]), act='relu2', out_dtype=BF16)
    dn0 = matmul("down0", hid0, w['mlp_w_down'][0], 'nn', bias=row(w['mlp_b_down'][0]))
    h2, xh2, rs2 = ln_fwd("ln_mlp0", h1, dn0, row(w['ln_mlp_g'][0]), row(w['ln_mlp_b'][0]))

    projc = matmul("in_c", h2, w['w_in_c'], 'nn')
    lb, lb_vjp = jax.vjp(_lower_bound, w['hgrn_gamma'])
    ng = w['hgrn_norm_g']
    res = hgrn_fwd(projc, lb, ng, nseq, lp, ex=hooks.gather_last if hooks else None)
    o, og, states = res[:3]
    if hooks:
        w = {**w, **hooks.last_weights(res[3:])}
    mix1 = matmul("out_c", og, w['w_out_c'], 'nn')
    h3, xh3, rs3 = ln_fwd("ln_mix1", h2, mix1, row(w['ln_mix_g'][1]), row(w['ln_mix_b'][1]))
    hid1 = matmul("up1", h3, w['mlp_w_up'][1], 'nn', bias=row(w['mlp_b_up'][1]), act='relu2', out_dtype=BF16)
    dn1 = matmul("down1", hid1, w['mlp_w_down'][1], 'nn', bias=row(w['mlp_b_down'][1]))
    h4, xh4, rs4 = ln_fwd("ln_mlp1", h3, dn1, row(w['ln_mlp_g'][1]), row(w['ln_mlp_b'][1]))
    dy, loss = loss_head("loss", h4, tgt_p, lp, seq)

    def mlp_bwd(l, dh_out, xh_out, rs_out, hid, h_in):
        dpre, dg_, db_, dbd = ln_bwd(f"ln_mlp{l}_b", dh_out, xh_out, rs_out, row(w['ln_mlp_g'][l]))
        dpu = matmul(f"down{l}_dx", dpre, w['mlp_w_down'][l], 'nt', relu2_of=hid, out_dtype=BF16)
        dwd = matmul(f"down{l}_dw", hid, dpre, 'tn')
        dh_in = matmul(f"up{l}_dx", dpu, w['mlp_w_up'][l], 'nt', add=dpre, add_scale=ALPHA)
        dwu, dbu = matmul(f"up{l}_dw", h_in, dpu, 'tn', out_slots=N_CHIPS, colsum=True)
        return dh_in, dict(ln_mlp_g=dg_, ln_mlp_b=db_, mlp_b_down=dbd, mlp_b_up=dbu, mlp_w_up=dwu, mlp_w_down=dwd)

    dh3, gm1 = mlp_bwd(1, dy, xh4, rs4, hid1, h3)
    dpre, dg1, db1, _ = ln_bwd("ln_mix1_b", dh3, xh3, rs3, row(w['ln_mix_g'][1]))
    g['w_out_c'] = matmul("out_c_dw", og, dpre, 'tn')
    dog = matmul("out_c_dx", dpre, w['w_out_c'], 'nt')
    dq, df, di, dgg, dlb, dng = hgrn_bwd(projc, o, dog, states, lb, ng, nseq, lp)
    dprojc = jnp.concatenate([dq, df, di, dgg], axis=1)
    dh2 = matmul("in_c_dx", dprojc, w['w_in_c'], 'nt', add=dpre, add_scale=ALPHA)
    g['w_in_c'] = matmul("in_c_dw", h2, dprojc, 'tn', out_slots=N_CHIPS)
    g['hgrn_gamma'] = lb_vjp(jnp.sum(dlb, axis=0))[0]
    g['hgrn_norm_g'] = jnp.sum(dng, axis=0)

    dh1, gm0 = mlp_bwd(0, dh2, xh2, rs2, hid0, h1)
    dpre, dg0, db0, _ = ln_bwd("ln_mix0_b", dh1, xh1, rs1, row(w['ln_mix_g'][0]))
    g['w_out_ab'] = matmul("out_ab_dw", cat, dpre, 'tn')
    dcat = matmul("out_ab_dx", dpre, w['w_out_ab'], 'nt')
    dgp, da_s, dbglu = glu_gate_bwd("glu_gate_b", dcat[:, :S5_WIDTH], y, gp)
    g['s5_w_glu'] = matmul("glu_dw", y, dgp, 'tn')
    g['s5_b_glu'] = dbglu
    dy5 = matmul("glu_dx", dgp, w['s5_w_glu'], 'nt', add=da_s)
    dys5 = gelu_bwd("gelu_b", dy5, ys5)
    for n in ('mlp_w_up', 'mlp_w_down'):
        g[n] = [gm0[n], gm1[n]]
    g['ln_mix_g'] = jnp.concatenate([dg0, dg1], axis=0)
    g['ln_mix_b'] = jnp.concatenate([db0, db1], axis=0)
    for n in ('ln_mlp_g', 'ln_mlp_b', 'mlp_b_down', 'mlp_b_up'):
        g[n] = jnp.concatenate([gm0[n], gm1[n]], axis=0)
    res = s5_bwd(_to_groups(dys5), ug, sst, *mats[:3], *s5_pows, nseq, ex=hooks.fold_early(g) if hooks else None)
    dug, da1, da2, da3, dare, daim = res[:6]
    for n, v in zip(s5_names, mats_vjp((da1, da2, da3, dare, daim))):
        g[n] = v[None]
    ex = hooks.scatter_early(g, loss, res[6:]) if hooks else None
    res = attn_bwd(proj, sb_tot, sb_first, dcat[:, S5_WIDTH:], nseq, lp, ex=ex)
    dqa, dka, dva = res[:3]
    if hooks:
        hooks.scattered(res[3:])
    dproj = jnp.concatenate([_from_groups(dug), dqa, dka, dva], axis=1)
    dh0 = matmul("in_ab_dx", dproj, w['w_in_ab'], 'nt', add=dpre, add_scale=ALPHA)
    g['w_in_ab'] = matmul("in_ab_dw", h0, dproj, 'tn', out_slots=N_CHIPS)

    dh0 = dh0.reshape(nseq, lp, D_MODEL)
    grad_x = dh0[:, N_META:N_META + seq]
    g['meta'] = jnp.sum(dh0[:, :N_META], axis=0)
    return loss, grad_x, g


class Exchange:
    def __init__(self, ins, out_shapes, n_remote, build, in_place):
        self.ins, self.out_shapes, self.n_remote, self.build, self.in_place = ins, out_shapes, n_remote, build, in_place

    def sems(self):
        return [pltpu.SemaphoreType.DMA((self.n_remote,)), pltpu.SemaphoreType.DMA((self.n_remote,))]

    def aliases(self):
        return self.in_place if isinstance(self.in_place, dict) else \
            {i: i for i in range(len(self.ins) if self.in_place else 0)}

    def beside(self, other):
        na, nao = len(self.ins), len(self.out_shapes)

        def build(in_refs, out_refs, me):
            pa = self.build(in_refs[:na], out_refs[:nao], me)
            pb = other.build(in_refs[na:], out_refs[nao:], me)
            n = max(len(pa), len(pb))
            return [(pa[i] if i < len(pa) else []) + (pb[i] if i < len(pb) else []) for i in range(n)]
        al = dict(self.aliases())
        al.update({na + i: nao + j for i, j in other.aliases().items()})
        return Exchange(list(self.ins) + list(other.ins), list(self.out_shapes) + list(other.out_shapes),
                        self.n_remote + other.n_remote, build, al)

    def phases(self, in_refs, out_refs, ssem, rsem):
        me = (lax.axis_index("x"), lax.axis_index("y"), lax.axis_index("c"))
        out, k = [], 0
        for phase in self.build(in_refs, out_refs, me):
            out.append(functools.partial(
                lambda phase, k: [pltpu.make_async_remote_copy(src_ref=s, dst_ref=d, send_sem=ssem.at[k + i],
                                                               recv_sem=rsem.at[k + i], device_id=p,
                                                               device_id_type=MESH)
                                  for i, (s, d, p) in enumerate(phase)], phase, k))
            k += len(phase)
        return out

    def start(self, in_refs, out_refs, ssem, rsem):
        for cp in self.phases(in_refs, out_refs, ssem, rsem)[0]():
            cp.start()

    def finish(self, in_refs, out_refs, ssem, rsem):
        phases = self.phases(in_refs, out_refs, ssem, rsem)
        for cp in phases[0]():
            cp.wait()
        for make in phases[1:]:
            copies = make()
            for cp in copies:
                cp.start()
            for cp in copies:
                cp.wait()


def _exchange(name, ex):
    ni, no = len(ex.ins), len(ex.out_shapes)

    def body(*refs):
        in_refs, out_refs, sems = refs[:ni], refs[ni:ni + no], refs[ni + no:]
        ex.start(in_refs, out_refs, *sems)
        ex.finish(in_refs, out_refs, *sems)

    anyspec = pl.BlockSpec(memory_space=pl.ANY)
    return pl.pallas_call(
        body, name=name, in_specs=[anyspec] * ni, out_specs=[anyspec] * no, out_shape=ex.out_shapes,
        input_output_aliases=ex.aliases(), scratch_shapes=ex.sems())(*ex.ins)


def _chip_peers(me):
    x, y, c = me
    return [(x ^ (m >> 1), y ^ (m & 1), c) for m in (1, 2, 3)]


def _half(ref, c, axis):
    h = ref.shape[axis] // 2
    idx = [slice(None)] * axis + [pl.ds(c * h, h)]
    return ref.at[tuple(idx)]


def _like(arrs):
    return [jax.ShapeDtypeStruct(a.shape, a.dtype) for a in arrs]


def gather_over_chips(bufs):
    def build(in_refs, out_refs, me):
        x, y, c = me
        j = 2 * x + y
        sib = (x, y, 1 - c)
        ici = [(_half(o.at[j], c, 0), _half(o.at[j], c, 0), p) for o in out_refs for p in _chip_peers(me)]
        d2d = [(_half(o.at[2 * p[0] + p[1]], c, 0), _half(o.at[2 * p[0] + p[1]], c, 0), sib)
               for o in out_refs for p in _chip_peers(me)]
        return [ici, d2d]
    return Exchange(bufs, _like(bufs), 6 * len(bufs), build, True)


def fold_cores(fulls):
    def build(in_refs, out_refs, me):
        x, y, c = me
        return [[(_half(s, 1 - c, 1), o, (x, y, 1 - c)) for s, o in zip(in_refs, out_refs)]]
    shapes = [jax.ShapeDtypeStruct((f.shape[0], f.shape[1] // 2, f.shape[2]), f.dtype) for f in fulls]
    return Exchange(fulls, shapes, len(fulls), build, False)


def scatter_over_chips(parts):
    def build(in_refs, out_refs, me):
        j = 2 * me[0] + me[1]
        return [[(s.at[2 * p[0] + p[1]], o.at[j], p) for s, o in zip(in_refs, out_refs) for p in _chip_peers(me)]]
    return Exchange(parts, _like(parts), 3 * len(parts), build, False)


def join_cores(bufs):
    def build(in_refs, out_refs, me):
        x, y, c = me
        return [[(o.at[c], o.at[c], (x, y, 1 - c)) for o in out_refs]]
    return Exchange(bufs, _like(bufs), len(bufs), build, True)


def gather_over_devices(buf):
    def build(in_refs, out_refs, me):
        x, y, c = me
        i = 4 * x + 2 * y + c
        out = out_refs[0]
        return [[(out.at[i], out.at[i], (x ^ (m >> 2), y ^ ((m >> 1) & 1), c ^ (m & 1))) for m in range(1, N_DEV)]]
    return Exchange([buf], _like([buf]), N_DEV - 1, build, True)


def _slot_call(name, body, scalars, ins, in_maps, out_shape, out_map, blk, grid):
    gs = pltpu.PrefetchScalarGridSpec(
        num_scalar_prefetch=1, grid=grid,
        in_specs=[pl.BlockSpec((None,) + blk, m) for m in in_maps],
        out_specs=pl.BlockSpec((None,) + blk, out_map))
    return pl.pallas_call(body, name=name, grid_spec=gs, out_shape=out_shape,
                          compiler_params=_params(("arbitrary",) * len(grid)))(scalars, *ins)


def cast_into_slot(name, w2d, chip, dtype):
    r, wd = w2d.shape
    tm = _pick(r, (512, 256, 128, 64, 32, 16))

    def body(s_ref, w_ref, o_ref):
        o_ref[...] = w_ref[...].astype(o_ref.dtype)

    gs = pltpu.PrefetchScalarGridSpec(
        num_scalar_prefetch=1, grid=(r // tm,),
        in_specs=[pl.BlockSpec((tm, wd), lambda i, s: (i, 0))],
        out_specs=pl.BlockSpec((None, tm, wd), lambda i, s: (s[0], i, 0)))
    return pl.pallas_call(body, name=name, grid_spec=gs,
                          out_shape=jax.ShapeDtypeStruct((N_CHIPS, r, wd), dtype),
                          compiler_params=_params(("arbitrary",)))(chip.reshape(1), w2d)


def sum_cores(name, full, theirs, core):
    s, r, wd = full.shape
    h = r // 2
    tm = _pick(h, (512, 256, 128, 64, 32, 16))
    nt = h // tm

    def body(s_ref, a_ref, b_ref, o_ref):
        o_ref[...] = (a_ref[...] + b_ref[...]).astype(o_ref.dtype)

    return _slot_call(name, body, core.reshape(1), [full, theirs],
                      [lambda k, i, s_: (k, s_[0] * nt + i, 0), lambda k, i, s_: (k, i, 0)],
                      jax.ShapeDtypeStruct((s, h, wd), BF16), lambda k, i, s_: (k, i, 0), (tm, wd), (s, nt))


def sum_chips(name, own, recv, chip, core):
    s, r, wd = own.shape
    tm = _pick(r, (512, 256, 128, 64, 32, 16))

    def body(s_ref, a_ref, b1_ref, b2_ref, b3_ref, o_ref):
        acc = a_ref[...].astype(F32)
        for ref in (b1_ref, b2_ref, b3_ref):
            acc = acc + ref[...].astype(F32)
        o_ref[...] = acc

    maps = [lambda i, s_: (s_[0], i, 0)]
    maps += [functools.partial(lambda i, s_, m: (s_[0] ^ m, i, 0), m=m) for m in (1, 2, 3)]
    return _slot_call(name, body, jnp.stack([chip, core]), [own, recv, recv, recv], maps,
                      jax.ShapeDtypeStruct((2, r, wd), F32), lambda i, s_: (s_[1], i, 0), (tm, wd), (r // tm,))


def sum_slots(name, arr):
    s, r, wd = arr.shape
    tm = _pick(r, (512, 256, 128, 64, 32, 16, 8))

    def body(*refs):
        acc = refs[0][...].astype(F32)
        for ref in refs[1:s]:
            acc = acc + ref[...].astype(F32)
        refs[s][...] = acc

    specs = [pl.BlockSpec((None, tm, wd), functools.partial(lambda i, k: (k, i, 0), k=k)) for k in range(s)]
    return pl.pallas_call(body, name=name, grid=(r // tm,), in_specs=specs,
                          out_specs=pl.BlockSpec((tm, wd), lambda i: (i, 0)),
                          out_shape=jax.ShapeDtypeStruct((r, wd), F32),
                          compiler_params=_params(("parallel",)))(*([arr] * s))


def _pack(arrs):
    flat = jnp.concatenate([a.reshape(-1) for a in arrs])
    n = flat.shape[0]
    padded = -(-n // (512 * LANES)) * (512 * LANES)
    return jnp.pad(flat, (0, padded - n)).reshape(-1, LANES)


def _unpack(packed, shapes):
    flat = packed.reshape(-1)
    out, pos = [], 0
    for s in shapes:
        n = math.prod(s)
        out.append(flat[pos:pos + n].reshape(s))
        pos += n
    return out


def _as2d(a):
    return a.reshape(-1, a.shape[-1])


def kernel(x, meta, w_in_ab, s5_lam_re, s5_lam_im, s5_log_dt, s5_b_re, s5_b_im, s5_c_re, s5_c_im, s5_d, s5_w_glu, s5_b_glu, w_out_ab, w_in_c, hgrn_gamma, hgrn_norm_g, w_out_c, ln_mix_g, ln_mix_b, mlp_w_up, mlp_b_up, mlp_w_down, mlp_b_down, ln_mlp_g, ln_mlp_b, loss_target, m_meta, m_w_in_ab, m_s5_lam_re, m_s5_lam_im, m_s5_log_dt, m_s5_b_re, m_s5_b_im, m_s5_c_re, m_s5_c_im, m_s5_d, m_s5_w_glu, m_s5_b_glu, m_w_out_ab, m_w_in_c, m_hgrn_gamma, m_hgrn_norm_g, m_w_out_c, m_ln_mix_g, m_ln_mix_b, m_mlp_w_up, m_mlp_b_up, m_mlp_w_down, m_mlp_b_down, m_ln_mlp_g, m_ln_mlp_b, v_meta, v_w_in_ab, v_s5_lam_re, v_s5_lam_im, v_s5_log_dt, v_s5_b_re, v_s5_b_im, v_s5_c_re, v_s5_c_im, v_s5_d, v_s5_w_glu, v_s5_b_glu, v_w_out_ab, v_w_in_c, v_hgrn_gamma, v_hgrn_norm_g, v_w_out_c, v_ln_mix_g, v_ln_mix_b, v_mlp_w_up, v_mlp_b_up, v_mlp_w_down, v_mlp_b_down, v_ln_mlp_g, v_ln_mlp_b):
    given = dict(locals())
    wts = {n: given[n] for n in WEIGHTS}
    ms = {n: given['m_' + n] for n in WEIGHTS}
    vs = {n: given['v_' + n] for n in WEIGHTS}
    chip = 2 * lax.axis_index("x") + lax.axis_index("y")

    core = lax.axis_index("c")
    parts = [(n, l) for n in BIG for l in (range(DEPTH) if n.startswith('mlp_') else [0])]
    tags = [f"{n}{l}" for n, l in parts]
    shards = {p: cast_into_slot("cast_" + t, wts[p[0]][p[1]], chip, BF16) for t, p in zip(tags, parts)}
    small_sh = jnp.concatenate([meta, hgrn_norm_g, jnp.zeros((15, meta.shape[1]), F32)], axis=0)
    first, late = parts[:1], parts[1:]
    w_in_ab_all, sm = _exchange("gather_first", gather_over_chips(
        [shards[p] for p in first] + [cast_into_slot("cast_small", small_sh, chip, F32)]))
    w = {n: wts[n] for n in SMALL}
    w['meta'] = sm[:, :N_META].transpose(1, 0, 2).reshape(N_META, D_MODEL)
    w['hgrn_norm_g'] = sm[:, N_META:N_META + 1].transpose(1, 0, 2).reshape(1, D_MODEL)
    w['w_in_ab'] = w_in_ab_all

    def slot_major(v):
        return v if v.ndim == 3 else v.reshape(N_CHIPS, -1, v.shape[-1])

    def whole_grads(ps, g):
        return [slot_major(g[n][l] if n.startswith('mlp_') else g[n]) for n, l in ps]

    def chip_sums_of(ps, gfull, theirs):
        return [sum_cores(f"sum_cores_{n}{l}", a, b, core) for (n, l), a, b in zip(ps, gfull, theirs)]

    def all_devices(arrs):
        packed = _pack(arrs)
        return gather_over_devices(lax.dynamic_update_slice(
            jnp.zeros((N_DEV,) + packed.shape, F32), packed[None], (2 * chip + core, 0, 0)))

    early_small = [n for n in SMALL if n != 'meta']
    last = [('w_out_c', 0), ('mlp_w_up', 1), ('mlp_w_down', 1)]
    mid = [p for p in late if p not in last]

    class Hooks:
        gather_late = gather_over_chips([shards[p] for p in mid])
        gather_last = gather_over_chips([shards[p] for p in last])

        @staticmethod
        def late_weights(filled):
            fw = Hooks.fw = dict(zip(mid, filled))
            return {'s5_w_glu': fw['s5_w_glu', 0].reshape(S5_WIDTH, S5_WIDTH),
                    'w_out_ab': fw['w_out_ab', 0].reshape(D_MODEL, D_MODEL),
                    'w_in_c': fw['w_in_c', 0],
                    'mlp_w_up': [fw['mlp_w_up', 0]],
                    'mlp_w_down': [fw['mlp_w_down', 0].reshape(D_FF, D_MODEL)]}

        @staticmethod
        def last_weights(filled):
            fw = dict(zip(last, filled))
            return {'w_out_c': fw['w_out_c', 0].reshape(D_MODEL, D_MODEL),
                    'mlp_w_up': [Hooks.fw['mlp_w_up', 0], fw['mlp_w_up', 1]],
                    'mlp_w_down': [Hooks.fw['mlp_w_down', 0].reshape(D_FF, D_MODEL),
                                   fw['mlp_w_down', 1].reshape(D_FF, D_MODEL)]}

        @staticmethod
        def fold_early(g):
            Hooks.whole = whole_grads(late, g)
            return fold_cores(Hooks.whole)

        @staticmethod
        def scatter_early(g, loss, theirs):
            Hooks.sums = chip_sums_of(late, Hooks.whole, theirs)
            Hooks.small_shapes = [(LANES,)] + [g[n].shape for n in early_small]
            return scatter_over_chips(Hooks.sums).beside(
                all_devices([loss.reshape(-1)] + [g[n] for n in early_small]))

        @staticmethod
        def scattered(outs):
            Hooks.recv, Hooks.small = list(outs[:-1]), outs[-1]

    loss, grad_x, g = local_step(x, loss_target, w, Hooks)

    whole = whole_grads(first, g)
    sums = chip_sums_of(first, whole, _exchange("fold_last", fold_cores(whole)))
    *recv, meta_slots = _exchange("scatter_last", scatter_over_chips(sums).beside(all_devices([g['meta']])))
    reduced = _exchange("join_grads", join_cores(
        [sum_chips(f"sum_chips_{n}{l}", a, b, chip, core)
         for (n, l), a, b in zip(first + late, sums + Hooks.sums, list(recv) + Hooks.recv)]))
    reduced = dict(zip(first + late, [_as2d(r) for r in reduced]))
    red = _unpack(sum_slots("sum_small", Hooks.small), Hooks.small_shapes)
    loss_out = red[0][0]
    gs = dict(zip(early_small, red[1:]))
    gs['meta'] = _unpack(sum_slots("sum_meta", meta_slots), [g['meta'].shape])[0]
    gs['meta'] = lax.dynamic_slice_in_dim(gs['meta'], chip * meta.shape[1], meta.shape[1], axis=1)
    gs['hgrn_norm_g'] = lax.dynamic_slice_in_dim(gs['hgrn_norm_g'].reshape(1, -1), chip * meta.shape[1],
                                                 meta.shape[1], axis=1)

    grads, deltas, new_m, new_v = {}, {}, {}, {}
    for n in BIG:
        r = jnp.concatenate([reduced[n, l] for l in range(wts[n].shape[0])], axis=0)
        res = adamw("adamw_" + n, _as2d(wts[n]), [r], _as2d(ms[n]), _as2d(vs[n]))
        grads[n], deltas[n], new_m[n], new_v[n] = [r.reshape(wts[n].shape) for r in res]
    shapes = [wts[n].shape for n in SMALL]
    res = adamw("adamw_small", _pack([wts[n] for n in SMALL]), [_pack([gs[n] for n in SMALL])],
                _pack([ms[n] for n in SMALL]), _pack([vs[n] for n in SMALL]))
    for d, r in zip((grads, deltas, new_m, new_v), res):
        d.update(zip(SMALL, _unpack(r, shapes)))
    return (loss_out, grad_x, *[grads[n] for n in WEIGHTS], *[deltas[n] for n in WEIGHTS],
            *[new_m[n] for n in WEIGHTS], *[new_v[n] for n in WEIGHTS])
```

```python
import functools
import math

import jax
import jax.numpy as jnp
from jax import lax
from jax.experimental import pallas as pl
from jax.experimental.pallas import tpu as pltpu

F32 = jnp.float32
BF16 = jnp.bfloat16

D_MODEL = 1024
N_META = 16
DEPTH = 2
S5_WIDTH = 512
S5_GROUP = 16
S5_GROUPS = 32
S5_STATE = 64
S5_CHUNK = 16
SB_WIDTH = 512
SB_BLOCK = 128
SB_DEAD = -110.0
HG_HEADS = 8
HG_DK = 128
HG_CHUNK = 64
HG_UNROLL = 2
D_FF = 4096
ALPHA = (2.0 * DEPTH) ** 0.25
LN_EPS = 1e-5
RMS_EPS = 1e-6
ADAM_LR = 0.001
ADAM_B1 = 0.9
ADAM_B2 = 0.999
ADAM_EPS = 1e-08
ADAM_WD = 0.01
ADAM_STEP = 10
N_CHIPS = 4
N_DEV = 8
LANES = 128
MESH = pl.DeviceIdType.MESH
VMEM_LIMIT = 56 * 1024 * 1024
WHOLE_K_BYTES = 42 * 1024 * 1024
HGRN_BWD_VMEM = 60 * 1024 * 1024

WEIGHTS = ['meta', 'w_in_ab', 's5_lam_re', 's5_lam_im', 's5_log_dt', 's5_b_re', 's5_b_im', 's5_c_re', 's5_c_im',
           's5_d', 's5_w_glu', 's5_b_glu', 'w_out_ab', 'w_in_c', 'hgrn_gamma', 'hgrn_norm_g', 'w_out_c',
           'ln_mix_g', 'ln_mix_b', 'mlp_w_up', 'mlp_b_up', 'mlp_w_down', 'mlp_b_down', 'ln_mlp_g', 'ln_mlp_b']
BIG = ['w_in_ab', 's5_w_glu', 'w_out_ab', 'w_in_c', 'w_out_c', 'mlp_w_up', 'mlp_w_down']
SMALL = [n for n in WEIGHTS if n not in BIG]


def _params(sem=None):
    return pltpu.CompilerParams(dimension_semantics=sem, vmem_limit_bytes=VMEM_LIMIT)


def _pick(n, cands):
    for c in cands:
        if n % c == 0:
            return c
    return n


def _dot(a, b, ca, cb):
    return lax.dot_general(a, b, (((ca,), (cb,)), ((), ())), preferred_element_type=F32)


def _split3(x):
    hi = x.astype(BF16)
    r = x - hi.astype(F32)
    mid = r.astype(BF16)
    lo = (r - mid.astype(F32)).astype(BF16)
    return hi, mid, lo


def _dot_exact_rhs(x, m, cx, cm):
    hi = x.astype(BF16)
    lo = (x - hi.astype(F32)).astype(BF16)
    return _dot(hi, m, cx, cm) + _dot(lo, m, cx, cm)


def _dot3(a, b, ca, cb):
    ah = a.astype(BF16)
    al = (a - ah.astype(F32)).astype(BF16)
    bh = b.astype(BF16)
    bl = (b - bh.astype(F32)).astype(BF16)
    return _dot(ah, bh, ca, cb) + (_dot(ah, bl, ca, cb) + _dot(al, bh, ca, cb))


def rowwise(name, fn, row_ins, bc_ins, out_widths, sum_widths=(), out_dtypes=None, tm=None):
    t = row_ins[0].shape[0]
    if tm is None:
        wmax = max([a.shape[1] for a in row_ins] + list(out_widths))
        tm = _pick(t, (272, 256, 128, 64, 16, 8)) if wmax > 1024 else _pick(t, (544, 512, 256, 128, 64, 16, 8))
    nr, nb, no, ns = len(row_ins), len(bc_ins), len(out_widths), len(sum_widths)
    out_dtypes = out_dtypes or [F32] * no

    def body(*refs):
        i = pl.program_id(0)
        rows = [r[...] for r in refs[:nr]]
        bcs = [r[...] for r in refs[nr:nr + nb]]
        outs, sums = fn(i, tm, rows, bcs)
        for r, v in zip(refs[nr + nb:nr + nb + no], outs):
            r[...] = v.astype(r.dtype)
        if ns:
            srefs = refs[nr + nb + no:]

            @pl.when(i == 0)
            def _():
                for r in srefs:
                    r[...] = jnp.zeros(r.shape, r.dtype)

            for r, v in zip(srefs, sums):
                r[...] += v

    in_specs = [pl.BlockSpec((tm, a.shape[1]), lambda i: (i, 0)) for a in row_ins]
    in_specs += [pl.BlockSpec(a.shape, lambda i: (0, 0)) for a in bc_ins]
    out_specs = [pl.BlockSpec((tm, w), lambda i: (i, 0)) for w in out_widths]
    out_specs += [pl.BlockSpec((1, w), lambda i: (0, 0)) for w in sum_widths]
    out_shape = [jax.ShapeDtypeStruct((t, w), dt) for w, dt in zip(out_widths, out_dtypes)]
    out_shape += [jax.ShapeDtypeStruct((1, w), F32) for w in sum_widths]
    res = pl.pallas_call(body, name=name, grid=(t // tm,), in_specs=in_specs, out_specs=out_specs,
                         out_shape=out_shape, compiler_params=_params(("arbitrary",)))(*row_ins, *bc_ins)
    return res


def _colsum(v):
    return jnp.sum(v, axis=0, keepdims=True)


def matmul(name, a, b, mode, *, bias=None, act=None, add=None, add_scale=1.0, relu2_of=None, colsum=False,
           out_dtype=F32, out_slots=0, tm=None, tn=None, tk=None):
    slotted = b.ndim == 3
    if mode == 'nn':
        m, k = a.shape
        n = b.shape[-1] * (b.shape[0] if slotted else 1)
    elif mode == 'nt':
        m, k = a.shape
        n = b.shape[-2]
    else:
        k, m = a.shape
        n = b.shape[-1]
    ns = b.shape[-1] if slotted else None
    if mode == 'tn':
        tm = tm or _pick(m, (512, 256, 128))
        nw = n if not out_slots else n // out_slots
        if tn is None and tk is None:
            for cand in (512, 256):
                if nw % cand == 0 and (2 * k * (tm * a.dtype.itemsize + cand * b.dtype.itemsize)
                                       + 2 * tm * cand * 4 <= WHOLE_K_BYTES):
                    tn, tk = cand, k
                    break
        tn = tn or _pick(nw, (1024, 512, 256, 128))
        tk = tk or _pick(k, (1088, 1024, 768, 512, 384, 256, 128))
    else:
        tm = tm or _pick(m, (1088, 1024, 768, 512, 384, 256, 128))
        tn = tn or _pick(n if not (slotted and mode == 'nn') else ns, (1024, 512, 256, 128))
        extra = sum(x is not None for x in (add, relu2_of)) * 4
        if tk is None and not slotted and (2 * (tm * k * a.dtype.itemsize + k * tn * b.dtype.itemsize)
                                           + 2 * tm * tn * (4 + extra) <= WHOLE_K_BYTES):
            tk = k
        tk = tk or _pick(k if not (slotted and mode == 'nt') else ns, (1024, 512, 256, 128))
    gm, gn, gk = m // tm, n // tn, k // tk

    if mode == 'nn':
        a_spec = pl.BlockSpec((tm, tk), lambda i, j, kk: (i, kk))
        if slotted:
            per = ns // tn
            b_spec = pl.BlockSpec((None, tk, tn), lambda i, j, kk: (j // per, kk, j % per))
        else:
            b_spec = pl.BlockSpec((tk, tn), lambda i, j, kk: (kk, j))
        ca, cb = 1, 0
    elif mode == 'nt':
        a_spec = pl.BlockSpec((tm, tk), lambda i, j, kk: (i, kk))
        if slotted:
            per = ns // tk
            b_spec = pl.BlockSpec((None, tn, tk), lambda i, j, kk: (kk // per, j, kk % per))
        else:
            b_spec = pl.BlockSpec((tn, tk), lambda i, j, kk: (j, kk))
        ca, cb = 1, 1
    else:
        a_spec = pl.BlockSpec((tk, tm), lambda i, j, kk: (kk, i))
        b_spec = pl.BlockSpec((tk, tn), lambda i, j, kk: (kk, j))
        ca, cb = 0, 0
    in_specs = [a_spec, b_spec]
    args = [a, b]
    if bias is not None:
        in_specs.append(pl.BlockSpec((1, tn), lambda i, j, kk: (0, j)))
        args.append(bias)
    if add is not None:
        in_specs.append(pl.BlockSpec((tm, tn), lambda i, j, kk: (i, j)))
        args.append(add)
    if relu2_of is not None:
        in_specs.append(pl.BlockSpec((tm, tn), lambda i, j, kk: (i, j)))
        args.append(relu2_of)
    if out_slots:
        per_o = (n // out_slots) // tn
        out_spec = pl.BlockSpec((None, tm, tn), lambda i, j, kk: (j // per_o, i, j % per_o))
        out_shape = jax.ShapeDtypeStruct((out_slots, m, n // out_slots), out_dtype)
    else:
        out_spec = pl.BlockSpec((tm, tn), lambda i, j, kk: (i, j))
        out_shape = jax.ShapeDtypeStruct((m, n), out_dtype)
    grid = (gm, gn, gk)
    if colsum:
        assert mode == 'tn'
        out_spec = [out_spec, pl.BlockSpec((1, tn), lambda i, j, kk: (0, j))]
        out_shape = [out_shape, jax.ShapeDtypeStruct((1, n), F32)]

        def swapped(spec):
            return pl.BlockSpec(spec.block_shape, functools.partial(lambda j, i, kk, f: f(i, j, kk), f=spec.index_map))
        in_specs = [swapped(sp) for sp in in_specs]
        out_spec = [swapped(sp) for sp in out_spec]
        grid = (gn, gm, gk)

    def body(*refs):
        a_ref, b_ref = refs[0], refs[1]
        pos = 2
        bias_ref = add_ref = hid_ref = cs_ref = None
        if bias is not None:
            bias_ref = refs[pos]
            pos += 1
        if add is not None:
            add_ref = refs[pos]
            pos += 1
        if relu2_of is not None:
            hid_ref = refs[pos]
            pos += 1
        o_ref = refs[pos]
        pos += 1
        if colsum:
            cs_ref = refs[pos]
            pos += 1
        acc_ref = refs[pos] if gk > 1 else None
        bt = b_ref[...]
        p = _dot(a_ref[...].astype(BF16), bt.astype(BF16), ca, cb)
        kk = pl.program_id(2)

        if colsum:
            @pl.when(jnp.logical_and(pl.program_id(1) == 0, kk == 0))
            def _():
                cs_ref[...] = _colsum(bt.astype(F32))

            @pl.when(jnp.logical_and(pl.program_id(1) == 0, kk > 0))
            def _():
                cs_ref[...] += _colsum(bt.astype(F32))

        def finish(r):
            if bias_ref is not None:
                r = r + bias_ref[...]
            if act == 'relu2':
                r = jnp.square(jnp.maximum(r, 0.0))
            if hid_ref is not None:
                r = r * (2.0 * jnp.sqrt(hid_ref[...].astype(F32)))
            if add_ref is not None:
                r = r + add_scale * add_ref[...]
            o_ref[...] = r.astype(o_ref.dtype)

        if gk == 1:
            finish(p)
        else:
            @pl.when(kk == 0)
            def _():
                acc_ref[...] = p

            @pl.when(kk > 0)
            def _():
                acc_ref[...] += p

            @pl.when(kk == gk - 1)
            def _():
                finish(acc_ref[...])

    scratch = [pltpu.VMEM((tm, tn), F32)] if gk > 1 else []
    sem = ("arbitrary",) * 3 if colsum else ("parallel", "parallel", "arbitrary")
    return pl.pallas_call(body, name=name, grid=grid, in_specs=in_specs, out_specs=out_spec,
                          out_shape=out_shape, scratch_shapes=scratch, compiler_params=_params(sem))(*args)


def ln_fwd(name, h, mix, g, b):
    def fn(i, tm, rows, bcs):
        pre = ALPHA * rows[0] + rows[1]
        mu = jnp.mean(pre, axis=1, keepdims=True)
        xc = pre - mu
        var = jnp.mean(xc * xc, axis=1, keepdims=True)
        rstd = lax.rsqrt(var + LN_EPS)
        xhat = xc * rstd
        return (xhat * bcs[0] + bcs[1], xhat, rstd), ()
    return rowwise(name, fn, [h, mix], [g, b], [D_MODEL, D_MODEL, 1])


def ln_bwd(name, dy, xhat, rstd, g):
    def fn(i, tm, rows, bcs):
        dy_, xh, rs = rows
        dyg = dy_ * bcs[0]
        m1 = jnp.mean(dyg, axis=1, keepdims=True)
        m2 = jnp.mean(dyg * xh, axis=1, keepdims=True)
        dpre = rs * (dyg - m1 - xh * m2)
        return (dpre,), (_colsum(dy_ * xh), _colsum(dy_), _colsum(dpre))
    return rowwise(name, fn, [dy, xhat, rstd], [g], [D_MODEL], [D_MODEL] * 3)


_GELU_C = math.sqrt(2.0 / math.pi)


def gelu_fwd(name, x):
    def fn(i, tm, rows, bcs):
        v = rows[0]
        u = _GELU_C * (v + 0.044715 * (v * v * v))
        return (0.5 * v * (1.0 + jnp.tanh(u)),), ()
    return rowwise(name, fn, [x], [], [x.shape[1]])[0]


def gelu_bwd(name, dy, x):
    def fn(i, tm, rows, bcs):
        v = rows[1]
        th = jnp.tanh(_GELU_C * (v + 0.044715 * (v * v * v)))
        dg = 0.5 * (1.0 + th) + 0.5 * v * (1.0 - th * th) * (_GELU_C * (1.0 + 3.0 * 0.044715 * v * v))
        return (rows[0] * dg,), ()
    return rowwise(name, fn, [dy, x], [], [x.shape[1]])[0]


def glu_gate(name, y, gp):
    def fn(i, tm, rows, bcs):
        return (rows[0] * jax.nn.sigmoid(rows[1]),), ()
    return rowwise(name, fn, [y, gp], [], [y.shape[1]], out_dtypes=[BF16])[0]


def glu_gate_bwd(name, da, y, gp):
    def fn(i, tm, rows, bcs):
        s = jax.nn.sigmoid(rows[2])
        dgp = rows[0] * rows[1] * s * (1.0 - s)
        return (dgp, rows[0] * s), (_colsum(dgp),)
    w = y.shape[1]
    return rowwise(name, fn, [da, y, gp], [], [w, w], [w])


def loss_head(name, h, tgt, lp, seq):
    def fn(i, tm, rows, bcs):
        pos = (i * tm + lax.broadcasted_iota(jnp.int32, (tm, 1), 0)) % lp
        valid = jnp.logical_and(pos >= N_META, pos < N_META + seq)
        err = jnp.where(valid, rows[0] - rows[1], 0.0)
        part = 0.5 * jnp.sum(jnp.mean(err * err, axis=1, keepdims=True), axis=0, keepdims=True)
        return (err * (1.0 / D_MODEL),), (jnp.broadcast_to(part, (1, LANES)),)
    return rowwise(name, fn, [h, tgt], [], [D_MODEL], [LANES])


def adamw(name, w, gs, m, v):
    ng = len(gs)

    def fn(i, tm, rows, bcs):
        w_ = rows[0]
        g = rows[1] if ng == 1 else rows[1] + rows[2]
        m_, v_ = rows[1 + ng], rows[2 + ng]
        m_ = ADAM_B1 * m_ + (1.0 - ADAM_B1) * g
        v_ = ADAM_B2 * v_ + (1.0 - ADAM_B2) * jnp.square(g)
        m_hat = m_ / (1.0 - ADAM_B1 ** ADAM_STEP)
        v_hat = v_ / (1.0 - ADAM_B2 ** ADAM_STEP)
        delta = -ADAM_LR * (m_hat / (jnp.sqrt(v_hat) + ADAM_EPS) + ADAM_WD * w_)
        return (g, delta, m_, v_), ()
    wd = w.shape[1]
    return rowwise(name, fn, [w] + list(gs) + [m, v], [], [wd] * 4)


def s5_matrices(lam_re, lam_im, log_dt, b_re, b_im, c_re, c_im, d):
    c, hh, gg, pp = S5_CHUNK, S5_GROUP, S5_GROUPS, S5_STATE
    dt = jnp.exp(log_dt)[:, None]
    tau = jnp.arange(c + 1, dtype=F32)[None, :, None]
    mag = jnp.exp(tau * (lam_re * dt)[:, None, :])
    ang = tau * (lam_im * dt)[:, None, :]
    pw_re, pw_im = mag * jnp.cos(ang), mag * jnp.sin(ang)
    nr, ni = pw_re[:, 1] - 1.0, pw_im[:, 1]
    den = lam_re * lam_re + lam_im * lam_im
    cf_re = (nr * lam_re + ni * lam_im) / den
    cf_im = (ni * lam_re - nr * lam_im) / den
    bb_re = cf_re[:, :, None] * b_re - cf_im[:, :, None] * b_im
    bb_im = cf_re[:, :, None] * b_im + cf_im[:, :, None] * b_re
    ct_re, ct_im = c_re.transpose(0, 2, 1)[:, :, None, :], c_im.transpose(0, 2, 1)[:, :, None, :]
    pt_re, pt_im = pw_re.transpose(0, 2, 1)[:, :, :, None], pw_im.transpose(0, 2, 1)[:, :, :, None]
    cp_re = ct_re * pt_re - ct_im * pt_im
    cp_im = ct_re * pt_im + ct_im * pt_re
    hp = lax.Precision.HIGHEST
    kmat = (jnp.einsum('gpk,gpn->gkn', bb_re, cp_re[:, :, :c].reshape(gg, pp, c * hh), precision=hp)
            - jnp.einsum('gpk,gpn->gkn', bb_im, cp_im[:, :, :c].reshape(gg, pp, c * hh), precision=hp))
    rows = [jnp.pad(kmat[:, :, :(c - j) * hh], ((0, 0), (0, 0), (j * hh, 0))) for j in range(c)]
    a1 = jnp.stack(rows, axis=1).reshape(gg, c * hh, c * hh)
    a1 = a1 + jnp.eye(c * hh, dtype=F32)[None] * jnp.tile(d, (1, c))[:, None, :]
    a2 = jnp.concatenate([cp_re[:, :, 1:].reshape(gg, pp, c * hh), -cp_im[:, :, 1:].reshape(gg, pp, c * hh)], axis=1)
    rv_re, rv_im = pw_re[:, :c][:, ::-1, None, :], pw_im[:, :c][:, ::-1, None, :]
    bt_re, bt_im = bb_re.transpose(0, 2, 1)[:, None], bb_im.transpose(0, 2, 1)[:, None]
    a3 = jnp.concatenate([rv_re * bt_re - rv_im * bt_im, rv_re * bt_im + rv_im * bt_re], axis=-1)
    a3 = a3.reshape(gg, c * hh, 2 * pp)
    are = jnp.concatenate([pw_re[:, c], pw_re[:, c]], axis=-1)[:, None, :]
    aim = jnp.concatenate([-pw_im[:, c], pw_im[:, c]], axis=-1)[:, None, :]
    return a1, a2, a3, are, aim


S5_LEVELS = 8


def s5_scan_powers(lam_re, lam_im, log_dt):
    dt = jnp.exp(log_dt)[:, None]
    e = (S5_CHUNK * 2.0 ** jnp.arange(S5_LEVELS, dtype=F32))[:, None, None]
    mag = jnp.exp(e * (lam_re * dt)[None])
    ang = e * (lam_im * dt)[None]
    pr, pi = mag * jnp.cos(ang), mag * jnp.sin(ang)
    pre = jnp.concatenate([pr, pr], axis=-1).transpose(1, 0, 2)
    pim = jnp.concatenate([-pi, pi], axis=-1).transpose(1, 0, 2)
    return pre, pim


def _s5_scan(x, pre, pim, reverse):
    n = x.shape[0]
    rowi = lax.broadcasted_iota(jnp.int32, (n, 1), 0)
    t = x
    for k in range(S5_LEVELS):
        shift = 2 ** k
        if shift >= n:
            break
        p_re, p_im = pre[k:k + 1, :], pim[k:k + 1, :]
        if reverse:
            far = jnp.where(rowi < n - shift, pltpu.roll(t, n - shift, 0), 0.0)
            t = t + (p_re * far + pltpu.roll(p_im * far, S5_STATE, 1))
        else:
            far = jnp.where(rowi >= shift, pltpu.roll(t, shift, 0), 0.0)
            t = t + (p_re * far + p_im * pltpu.roll(far, S5_STATE, 1))
    return t


def s5_fwd(ug, a1, a2, a3, pre, pim, nseq):
    g, nc, cw = ug.shape
    per = nc // nseq
    sw = 2 * S5_STATE
    assert per <= 2 ** S5_LEVELS

    def body(u_ref, a1_ref, a2_ref, a3_ref, pre_ref, pim_ref, y_ref, s_ref):
        u = u_ref[...]
        x = _dot3(u, a3_ref[...], 1, 0)
        first = lax.broadcasted_iota(jnp.int32, (per, 1), 0) == 0
        for b in range(nseq):
            t = _s5_scan(x[b * per:(b + 1) * per], pre_ref[...], pim_ref[...], False)
            s_ref[pl.ds(b * per, per), :] = jnp.where(first, 0.0, pltpu.roll(t, 1, 0))
        y_ref[...] = _dot3(u, a1_ref[...], 1, 0) + _dot3(s_ref[...], a2_ref[...], 1, 0)

    def spec(shape):
        return pl.BlockSpec((None,) + shape, lambda i: (i, 0, 0))
    lv = (S5_LEVELS, sw)
    return pl.pallas_call(
        body, name="s5_fwd", grid=(g,),
        in_specs=[spec((nc, cw)), spec((cw, cw)), spec((sw, cw)), spec((cw, sw)), spec(lv), spec(lv)],
        out_specs=[spec((nc, cw)), spec((nc, sw))],
        out_shape=[jax.ShapeDtypeStruct((g, nc, cw), F32), jax.ShapeDtypeStruct((g, nc, sw), F32)],
        compiler_params=_params(("parallel",)))(ug, a1, a2, a3, pre, pim)


def s5_bwd(dyg, ug, sst, a1, a2, a3, pre, pim, nseq, ex=None):
    g, nc, cw = ug.shape
    per = nc // nseq
    sw = 2 * S5_STATE
    split, ex_start, ex_finish, exkw = _hidden_exchange(ex, 8, 6, (g,))

    def body(*refs):
        ((dy_ref, u_ref, s_ref, a1_ref, a2_ref, a3_ref, pre_ref, pim_ref),
         (du_ref, da1_ref, da2_ref, da3_ref, dare_ref, daim_ref), ex_parts, (dx_ref,)) = split(refs)
        ex_start(ex_parts)
        dy = dy_ref[...]
        u = u_ref[...]
        gd = _dot3(dy, a2_ref[...], 1, 1)
        s_all = s_ref[...]
        da2_ref[...] = _dot3(s_all, dy, 0, 0)
        last = lax.broadcasted_iota(jnp.int32, (per, 1), 0) == per - 1
        for b in range(nseq):
            gs = _s5_scan(gd[b * per:(b + 1) * per], pre_ref[...], pim_ref[...], True)
            dx_ref[pl.ds(b * per, per), :] = jnp.where(last, 0.0, pltpu.roll(gs, per - 1, 0))
        dx = dx_ref[...]
        dare_ref[...] = _colsum(dx * s_all)
        daim_ref[...] = _colsum(dx * pltpu.roll(s_all, S5_STATE, 1))
        du_ref[...] = _dot3(dy, a1_ref[...], 1, 1) + _dot3(dx, a3_ref[...], 1, 1)
        da1_ref[...] = _dot3(u, dy, 0, 0)
        da3_ref[...] = _dot3(u, dx, 0, 0)
        ex_finish(ex_parts)

    def spec(shape):
        return pl.BlockSpec((None,) + shape, lambda i: (i, 0, 0))
    sds = jax.ShapeDtypeStruct
    return pl.pallas_call(
        body, name="s5_bwd", grid=(g,),
        in_specs=[spec((nc, cw)), spec((nc, cw)), spec((nc, sw)), spec((cw, cw)), spec((sw, cw)), spec((cw, sw)),
                  spec((S5_LEVELS, sw)), spec((S5_LEVELS, sw))] + exkw['in_specs'],
        out_specs=[spec((nc, cw)), spec((cw, cw)), spec((sw, cw)), spec((cw, sw)), spec((1, sw)), spec((1, sw))]
        + exkw['out_specs'],
        out_shape=[sds((g, nc, cw), F32), sds((g, cw, cw), F32), sds((g, sw, cw), F32), sds((g, cw, sw), F32),
                   sds((g, 1, sw), F32), sds((g, 1, sw), F32)] + exkw['out_shape'],
        input_output_aliases=exkw['aliases'], scratch_shapes=[pltpu.VMEM((nc, sw), F32)] + exkw['scratch'],
        compiler_params=_params(("arbitrary",)))(dyg, ug, sst, a1, a2, a3, pre, pim, *exkw['ins'])


def _to_groups(u):
    t = u.shape[0]
    nc = t // S5_CHUNK
    return u.reshape(nc, S5_CHUNK, S5_GROUPS, S5_GROUP).transpose(2, 0, 1, 3).reshape(S5_GROUPS, nc, -1)


def _from_groups(yg):
    g, nc, _ = yg.shape
    return yg.reshape(g, nc, S5_CHUNK, S5_GROUP).transpose(1, 2, 0, 3).reshape(nc * S5_CHUNK, g * S5_GROUP)


def _sb_masks():
    row = lax.broadcasted_iota(jnp.int32, (SB_BLOCK, SB_BLOCK), 0)
    col = lax.broadcasted_iota(jnp.int32, (SB_BLOCK, SB_BLOCK), 1)
    lane = lax.broadcasted_iota(jnp.int32, (1, LANES), 1)
    return row, col, [lane < 64, lane >= 64]


def _sb_weights(z, vis, later_of, after):
    sp = jnp.maximum(z, 0.0) + jnp.log1p(jnp.exp(-jnp.abs(z)))
    lk = jnp.where(vis, -sp, 0.0)
    rs = jnp.sum(lk, axis=1, keepdims=True)
    later = _dot_exact_rhs(lk, after, 1, 0) + later_of(rs)
    lb = z - sp
    w = jnp.where(vis, jnp.exp(lb + later), 0.0)
    return w, rs, lb


def _hidden_exchange(ex, n_in, n_out, grid):
    ni, no = (len(ex.ins), len(ex.out_shapes)) if ex else (0, 0)

    def split(refs):
        a, b = n_in + ni, n_in + ni + n_out
        end = len(refs) - (2 if ex else 0)
        return refs[:n_in], refs[a:b], (refs[n_in:a], refs[b:b + no], refs[end:]), refs[b + no:end]

    def at_step(which, fn, parts):
        if ex is not None:
            ids = [pl.program_id(d) == (0 if which == 'first' else g - 1) for d, g in enumerate(grid)]
            cond = ids[0]
            for c in ids[1:]:
                cond = jnp.logical_and(cond, c)
            pl.when(cond)(lambda: fn(parts[0], parts[1], *parts[2]))

    anyspec = pl.BlockSpec(memory_space=pl.ANY)
    kw = dict(in_specs=[anyspec] * ni, out_specs=[anyspec] * no, out_shape=list(ex.out_shapes) if ex else [],
              aliases={n_in + i: n_out + j for i, j in (ex.aliases().items() if ex else ())},
              scratch=ex.sems() if ex else [], ins=list(ex.ins) if ex else [])
    start = functools.partial(at_step, 'first', ex.start if ex else None)
    finish = functools.partial(at_step, 'last', ex.finish if ex else None)
    return split, start, finish, kw


def _lane_tile(x, tl):
    return x[:, tl * LANES:(tl + 1) * LANES]


SB_TILES = 2


def _two_heads(x):
    return jnp.concatenate([x, x], axis=0)


def _two_heads_masks():
    row = lax.broadcasted_iota(jnp.int32, (2 * SB_BLOCK, LANES), 0)
    col = lax.broadcasted_iota(jnp.int32, (2 * SB_BLOCK, LANES), 1)
    return (col // 64) == (row // SB_BLOCK), col < (row % SB_BLOCK)


def _own_lanes(r, lanes0):
    return jnp.where(lanes0, r[:SB_BLOCK], r[SB_BLOCK:])


def _sb_specs(lp, nseq):
    wd = SB_TILES * LANES
    off = [(S5_WIDTH + i * SB_WIDTH) // wd for i in range(3)]
    blk = (lp, wd)
    qkv = [pl.BlockSpec(blk, functools.partial(lambda b, h, o: (b, o + h), o=o)) for o in off]
    return (blk, qkv, pl.BlockSpec(blk, lambda b, h: (b, h)),
            pl.BlockSpec((None, None, 8, LANES), lambda b, h: (b, h, 0, 0)))


def attn_fwd(proj, nseq, lp, ex=None):
    nqb = lp // SB_BLOCK
    t = proj.shape[0]
    nstep = SB_WIDTH // (SB_TILES * LANES)
    split, ex_start, ex_finish, exkw = _hidden_exchange(ex, 3, 3, (nseq, nstep))

    def body(*refs):
        (q_ref, k_ref, v_ref), (o_ref, tot_ref, first_ref), ex_parts, _ = split(refs)
        ex_start(ex_parts)
        row, col, hmask = _sb_masks()
        after = (row > col).astype(BF16)
        tri = col < row
        lane8 = lax.broadcasted_iota(jnp.int32, (8, LANES), 1)

        hm2, tri2 = _two_heads_masks()

        def qloop(qi, firsts):
            q0 = pl.multiple_of(qi * SB_BLOCK, SB_BLOCK)
            q = q_ref[pl.ds(q0, SB_BLOCK), :] * 0.125
            qm = [jnp.where(hm2, _two_heads(_lane_tile(q, tl)), 0.0).astype(BF16) for tl in range(SB_TILES)]

            def alive(carry):
                return jnp.logical_and(carry[0] <= qi, carry[1] > 0)

            def kstep(carry):
                s = carry[0]
                accs, lats = list(carry[2:2 + SB_TILES]), list(carry[2 + SB_TILES:])
                kj = qi - s
                k0 = pl.multiple_of(kj * SB_BLOCK, SB_BLOCK)
                k = k_ref[pl.ds(k0, SB_BLOCK), :].astype(BF16)
                v = v_ref[pl.ds(k0, SB_BLOCK), :].astype(BF16)
                vis = jnp.logical_or(kj < qi, tri2)
                for tl in range(SB_TILES):
                    z = _dot(qm[tl], _lane_tile(k, tl), 1, 1)
                    w, rs, _ = _sb_weights(z, vis, functools.partial(lambda rs, lat: lat, lat=lats[tl]), after)
                    accs[tl] = accs[tl] + _own_lanes(_dot(w.astype(BF16), _lane_tile(v, tl), 1, 0), hmask[0])
                    lats[tl] = lats[tl] + rs
                top = functools.reduce(jnp.maximum, lats)
                go = (jnp.max(top) > SB_DEAD).astype(jnp.int32)
                return (s + 1, go, *accs, *lats)

            res = lax.while_loop(alive, kstep, (jnp.int32(0), jnp.int32(1),
                                                *[jnp.zeros((SB_BLOCK, LANES), F32)] * SB_TILES,
                                                *[jnp.zeros((2 * SB_BLOCK, 1), F32)] * SB_TILES))
            accs, lats = res[2:2 + SB_TILES], res[2 + SB_TILES:]
            for tl in range(SB_TILES):
                cols = slice(tl * LANES, (tl + 1) * LANES)
                o_ref[pl.ds(q0, SB_BLOCK), cols] = accs[tl].astype(o_ref.dtype)
                tot_ref[pl.ds(q0, SB_BLOCK), cols] = _own_lanes(jnp.broadcast_to(lats[tl], (2 * SB_BLOCK, LANES)),
                                                                hmask[0])
            return jnp.where(lane8 == qi, (qi - res[0] + 1).astype(F32), firsts)

        first_ref[...] = lax.fori_loop(0, nqb, qloop, jnp.zeros((8, LANES), F32))
        ex_finish(ex_parts)

    blk, qkv, out, vec = _sb_specs(lp, nseq)
    return pl.pallas_call(
        body, name="attn_fwd", grid=(nseq, nstep),
        in_specs=qkv + exkw['in_specs'], out_specs=[out, out, vec] + exkw['out_specs'],
        out_shape=[jax.ShapeDtypeStruct((t, SB_WIDTH), BF16), jax.ShapeDtypeStruct((t, SB_WIDTH), F32)]
        + [jax.ShapeDtypeStruct((nseq, nstep, 8, LANES), F32)] + exkw['out_shape'],
        input_output_aliases=exkw['aliases'], scratch_shapes=exkw['scratch'],
        compiler_params=_params(("arbitrary", "arbitrary")))(proj, proj, proj, *exkw['ins'])


def attn_bwd(proj, tot, first, do, nseq, lp, ex=None):
    nqb = lp // SB_BLOCK
    t = proj.shape[0]
    nstep = SB_WIDTH // (SB_TILES * LANES)
    split, ex_start, ex_finish, exkw = _hidden_exchange(ex, 6, 3, (nseq, nstep))

    def body(*refs):
        (q_ref, k_ref, v_ref, tot_ref, first_ref, do_ref), (dq_ref, dk_ref, dv_ref), ex_parts, _ = split(refs)
        ex_start(ex_parts)
        row, col, hmask = _sb_masks()
        after = (row > col).astype(BF16)
        before = (row < col).astype(BF16)
        tri = col < row
        lane8 = lax.broadcasted_iota(jnp.int32, (8, LANES), 1)
        firsts = first_ref[...]
        dk_ref[...] = jnp.zeros(dk_ref.shape, F32)
        dv_ref[...] = jnp.zeros(dv_ref.shape, F32)
        hm2, tri2 = _two_heads_masks()

        def qloop(qi, _):
            first = jnp.max(jnp.where(lane8 == qi, firsts, 0.0)).astype(jnp.int32)
            first = jnp.clip(first, 0, qi)
            q0 = pl.multiple_of(qi * SB_BLOCK, SB_BLOCK)
            q = q_ref[pl.ds(q0, SB_BLOCK), :] * 0.125
            do_ = do_ref[pl.ds(q0, SB_BLOCK), :]
            tot_ = tot_ref[pl.ds(q0, SB_BLOCK), :]
            qm = [jnp.where(hm2, _two_heads(_lane_tile(q, tl)), 0.0).astype(BF16) for tl in range(SB_TILES)]
            dom = [jnp.where(hm2, _two_heads(_lane_tile(do_, tl)), 0.0).astype(BF16) for tl in range(SB_TILES)]
            tot = [jnp.max(jnp.where(hm2, _two_heads(_lane_tile(tot_, tl)), -jnp.inf), axis=1, keepdims=True)
                   for tl in range(SB_TILES)]

            def kloop(kj, carry):
                dqs = list(carry[:SB_TILES])
                pre = list(carry[SB_TILES:2 * SB_TILES])
                gpre = list(carry[2 * SB_TILES:])
                k0 = pl.multiple_of(kj * SB_BLOCK, SB_BLOCK)
                k = k_ref[pl.ds(k0, SB_BLOCK), :].astype(BF16)
                v = v_ref[pl.ds(k0, SB_BLOCK), :].astype(BF16)
                vis = jnp.logical_or(kj < qi, tri2)
                for tl in range(SB_TILES):
                    kt, vt = _lane_tile(k, tl), _lane_tile(v, tl)
                    z = _dot(qm[tl], kt, 1, 1)
                    later_of = functools.partial(lambda rs, t_, p_: t_ - p_ - rs, t_=tot[tl], p_=pre[tl])
                    w, rs, lb = _sb_weights(z, vis, later_of, after)
                    dw = _dot(dom[tl], vt, 1, 1)
                    g = dw * w
                    dlk = _dot_exact_rhs(g, before, 1, 0) + gpre[tl]
                    sig = jnp.exp(lb)
                    dz = jnp.where(vis, g * (1.0 - sig) - dlk * sig, 0.0).astype(BF16)
                    dqs[tl] = dqs[tl] + _own_lanes(_dot(dz, kt, 1, 0), hmask[0])
                    cols = slice(tl * LANES, (tl + 1) * LANES)
                    dk_ref[pl.ds(k0, SB_BLOCK), cols] += _dot(dz, qm[tl], 0, 0)
                    dv_ref[pl.ds(k0, SB_BLOCK), cols] += _dot(w.astype(BF16), dom[tl], 0, 0)
                    pre[tl] = pre[tl] + rs
                    gpre[tl] = gpre[tl] + jnp.sum(g, axis=1, keepdims=True)
                return (*dqs, *pre, *gpre)

            z1 = jnp.zeros((2 * SB_BLOCK, 1), F32)
            res = lax.fori_loop(first, qi + 1, kloop,
                                (*[jnp.zeros((SB_BLOCK, LANES), F32)] * SB_TILES, *[z1] * (2 * SB_TILES)))
            for tl in range(SB_TILES):
                dq_ref[pl.ds(q0, SB_BLOCK), tl * LANES:(tl + 1) * LANES] = res[tl] * 0.125
            return 0

        lax.fori_loop(0, nqb, qloop, 0)
        ex_finish(ex_parts)

    blk, qkv, out, vec = _sb_specs(lp, nseq)
    return pl.pallas_call(
        body, name="attn_bwd", grid=(nseq, nstep),
        in_specs=qkv + [out, vec, out] + exkw['in_specs'], out_specs=[out] * 3 + exkw['out_specs'],
        out_shape=[jax.ShapeDtypeStruct((t, SB_WIDTH), F32)] * 3 + exkw['out_shape'],
        input_output_aliases=exkw['aliases'], scratch_shapes=exkw['scratch'],
        compiler_params=_params(("arbitrary", "arbitrary")))(proj, proj, proj, tot, first, do, *exkw['ins'])


def _hg_loop(nch, step, init):
    assert nch % HG_UNROLL == 0

    def trip(i, carry):
        for u in range(HG_UNROLL):
            carry = step(i * HG_UNROLL + u, carry)
        return carry
    return lax.fori_loop(0, nch // HG_UNROLL, trip, init)


def _hg_gates(fc, lb):
    sg = jax.nn.sigmoid(fc)
    f = lb + (1.0 - lb) * sg
    return sg, f


def _dot_exact_lhs(m, x):
    hi, mid, lo = _split3(x)
    return _dot(m, hi, 1, 0) + (_dot(m, mid, 1, 0) + _dot(m, lo, 1, 0))


def hgrn_fwd(proj, lb, ng, nseq, lp, ex=None):
    nch = lp // HG_CHUNK
    t = proj.shape[0]
    c = HG_CHUNK

    split, ex_start, ex_finish, exkw = _hidden_exchange(ex, 6, 3, (HG_HEADS,))

    def body(*refs):
        (q_ref, f_ref, v_ref, g_ref, lb_ref, ng_ref), (o_ref, og_ref, st_ref), ex_parts, _ = split(refs)
        ex_start(ex_parts)
        lbv, ngv = lb_ref[...], ng_ref[...]

        rr = nseq * c
        seq_r = lax.broadcasted_iota(jnp.int32, (rr, 1), 0) // c
        bd_row = lax.broadcasted_iota(jnp.int32, (rr, rr), 0)
        bd_col = lax.broadcasted_iota(jnp.int32, (rr, rr), 1)
        causal = jnp.logical_and(bd_row // c == bd_col // c, bd_col <= bd_row)
        incl = causal.astype(BF16)

        def stacked(ref, ci):
            return jnp.concatenate([ref[pl.ds(pl.multiple_of(b * lp + ci * c, c), c), :] for b in range(nseq)], axis=0)

        def wide(x):
            return jnp.concatenate([jnp.where(seq_r == b, x, jnp.zeros_like(x)) for b in range(nseq)], axis=1)

        def step(ci, st):
            for b in range(nseq):
                st_ref[b, ci] = st[:, b * HG_DK:(b + 1) * HG_DK]
            _, f = _hg_gates(stacked(f_ref, ci), lbv)
            bcum = _dot_exact_lhs(incl, jnp.log(f))
            lasts = [bcum[b * c + c - 1:b * c + c, :] for b in range(nseq)]
            blast = jnp.concatenate([jnp.broadcast_to(x, (c, LANES)) for x in lasts], axis=0)
            kk = 1.0 - f
            vc = stacked(v_ref, ci).astype(BF16)
            qd = (stacked(q_ref, ci) * jnp.exp(bcum)).astype(BF16)
            kd = (kk * jnp.exp(-bcum)).astype(BF16)
            a = jnp.where(causal, _dot(qd, kd, 1, 1), 0.0)
            o = _dot(a.astype(BF16), vc, 1, 0) + _dot(wide(qd), st.astype(BF16), 1, 1)
            kend = (kk * jnp.exp(blast - bcum)).astype(BF16)
            st = st * jnp.exp(jnp.concatenate(lasts, axis=1)) + _dot(vc, wide(kend), 0, 0)
            r = lax.rsqrt(jnp.mean(o * o, axis=1, keepdims=True) + RMS_EPS)
            gc = stacked(g_ref, ci)
            og = (o * r * ngv * (gc * jax.nn.sigmoid(gc))).astype(og_ref.dtype)
            for b in range(nseq):
                rows = pl.ds(pl.multiple_of(b * lp + ci * c, c), c)
                o_ref[rows, :] = o[b * c:(b + 1) * c]
                og_ref[rows, :] = og[b * c:(b + 1) * c]
            return st

        _hg_loop(nch, step, jnp.zeros((HG_DK, nseq * HG_DK), F32))
        ex_finish(ex_parts)

    blk = (nseq * lp, LANES)
    h = HG_HEADS
    return pl.pallas_call(
        body, name="hgrn_fwd", grid=(h,),
        in_specs=[pl.BlockSpec(blk, lambda hh: (0, hh)),
                  pl.BlockSpec(blk, lambda hh: (0, h + hh)),
                  pl.BlockSpec(blk, lambda hh: (0, 2 * h + hh)),
                  pl.BlockSpec(blk, lambda hh: (0, 3 * h + hh)),
                  pl.BlockSpec((1, LANES), lambda hh: (0, hh)),
                  pl.BlockSpec((1, LANES), lambda hh: (0, hh))] + exkw['in_specs'],
        out_specs=[pl.BlockSpec(blk, lambda hh: (0, hh)),
                   pl.BlockSpec(blk, lambda hh: (0, hh)),
                   pl.BlockSpec((nseq, None, nch, HG_DK, HG_DK), lambda hh: (0, hh, 0, 0, 0))] + exkw['out_specs'],
        out_shape=[jax.ShapeDtypeStruct((t, D_MODEL), F32), jax.ShapeDtypeStruct((t, D_MODEL), BF16),
                   jax.ShapeDtypeStruct((nseq, h, nch, HG_DK, HG_DK), F32)] + exkw['out_shape'],
        input_output_aliases=exkw['aliases'], scratch_shapes=exkw['scratch'],
        compiler_params=_params(("arbitrary",)))(proj, proj, proj, proj, lb, ng, *exkw['ins'])


def hgrn_bwd(proj, o, dog, states, lb, ng, nseq, lp):
    nch = lp // HG_CHUNK
    t = proj.shape[0]
    c = HG_CHUNK
    rr = nseq * c

    def body(q_ref, f_ref, v_ref, g_ref, o_ref, dog_ref, st_ref, lb_ref, ng_ref,
             dq_ref, df_ref, dv_ref, dg_ref, dlb_ref, dng_ref):
        seq_r = lax.broadcasted_iota(jnp.int32, (rr, 1), 0) // c
        last = lax.broadcasted_iota(jnp.int32, (rr, 1), 0) % c == c - 1
        bd_row = lax.broadcasted_iota(jnp.int32, (rr, rr), 0)
        bd_col = lax.broadcasted_iota(jnp.int32, (rr, rr), 1)
        same = bd_row // c == bd_col // c
        causal = jnp.logical_and(same, bd_col <= bd_row)
        incl = causal.astype(BF16)
        from_here = jnp.logical_and(same, bd_col >= bd_row).astype(BF16)
        lbv, ngv = lb_ref[...], ng_ref[...]

        def stacked(ref, ci):
            return jnp.concatenate([ref[pl.ds(pl.multiple_of(b * lp + ci * c, c), c), :] for b in range(nseq)], axis=0)

        def wide(x):
            return jnp.concatenate([jnp.where(seq_r == b, x, jnp.zeros_like(x)) for b in range(nseq)], axis=1)

        def own(x):
            return functools.reduce(jnp.add, [jnp.where(seq_r == b, x[:, b * LANES:(b + 1) * LANES], 0.0)
                                              for b in range(nseq)])

        def per_seq_rows(vals):
            return jnp.concatenate([jnp.broadcast_to(x, (c, LANES)) for x in vals], axis=0)

        def step(s, carry):
            dst, dlb, dng = carry
            ci = nch - 1 - s
            oc, gc, dogc = stacked(o_ref, ci), stacked(g_ref, ci), stacked(dog_ref, ci)
            r = lax.rsqrt(jnp.mean(oc * oc, axis=1, keepdims=True) + RMS_EPS)
            sgg = jax.nn.sigmoid(gc)
            dgc = dogc * (oc * r * ngv) * (sgg * (1.0 + gc * (1.0 - sgg)))
            don = dogc * (gc * sgg)
            dng = dng + _colsum(don * oc * r)
            tt = don * ngv
            do = r * (tt - oc * (r * r) * jnp.mean(tt * oc, axis=1, keepdims=True))
            st = jnp.concatenate([st_ref[b, ci] for b in range(nseq)], axis=1)
            sg, f = _hg_gates(stacked(f_ref, ci), lbv)
            bcum = _dot_exact_lhs(incl, jnp.log(f))
            lasts = [bcum[b * c + c - 1:b * c + c, :] for b in range(nseq)]
            blast = per_seq_rows(lasts)
            kk = 1.0 - f
            qc, vf = stacked(q_ref, ci), stacked(v_ref, ci)
            pdec = jnp.exp(bcum)
            ndec = jnp.exp(-bcum)
            edec = jnp.exp(blast - bcum)
            eb = jnp.exp(jnp.concatenate(lasts, axis=1))
            qd_f, kd_f, kend_f = qc * pdec, kk * ndec, kk * edec
            qd, kd, kend = qd_f.astype(BF16), kd_f.astype(BF16), kend_f.astype(BF16)
            vc, dob, stb, dstb = vf.astype(BF16), do.astype(BF16), st.astype(BF16), dst.astype(BF16)
            a = jnp.where(causal, _dot(qd, kd, 1, 1), 0.0).astype(BF16)
            da = jnp.where(causal, _dot(dob, vc, 1, 1), 0.0).astype(BF16)
            dv = _dot(a, dob, 0, 0) + _dot(wide(kend), dstb, 1, 1)
            dqd = _dot(da, kd, 1, 0) + own(_dot(dob, stb, 1, 0))
            dkd = _dot(da, qd, 0, 0)
            dkend = own(_dot(vc, dstb, 1, 0))
            sts = _colsum(dst * st)
            dkk = dkend * kend_f
            dblast = per_seq_rows([_colsum(jnp.where(seq_r == b, dkk, 0.0))
                                   + eb[:, b * LANES:(b + 1) * LANES] * sts[:, b * LANES:(b + 1) * LANES]
                                   for b in range(nseq)])
            dbcum = dqd * qd_f - dkd * kd_f - dkk
            dbcum = dbcum + jnp.where(last, dblast, 0.0)
            dlf = _dot_exact_lhs(from_here, dbcum)
            df = dlf / f - (dkd * ndec + dkend * edec)
            dfp = df * (1.0 - lbv) * sg * (1.0 - sg)
            dqc = dqd * pdec
            for b in range(nseq):
                rows = pl.ds(pl.multiple_of(b * lp + ci * c, c), c)
                blk_rows = slice(b * c, (b + 1) * c)
                dg_ref[rows, :], dv_ref[rows, :] = dgc[blk_rows], dv[blk_rows]
                dq_ref[rows, :], df_ref[rows, :] = dqc[blk_rows], dfp[blk_rows]
            dlb = dlb + _colsum(df * (1.0 - sg))
            dst = dst * eb + _dot(dob, wide(qd), 0, 0)
            return dst, dlb, dng

        z = jnp.zeros((1, LANES), F32)
        _, dlb, dng = _hg_loop(nch, step, (jnp.zeros((HG_DK, nseq * HG_DK), F32), z, z))
        dlb_ref[...] = dlb
        dng_ref[...] = dng

    blk = (nseq * lp, LANES)
    h = HG_HEADS
    vec = pl.BlockSpec((1, LANES), lambda hh: (0, hh))
    col = pl.BlockSpec(blk, lambda hh: (0, hh))
    return pl.pallas_call(
        body, name="hgrn_bwd", grid=(h,),
        in_specs=[col,
                  pl.BlockSpec(blk, lambda hh: (0, h + hh)),
                  pl.BlockSpec(blk, lambda hh: (0, 2 * h + hh)),
                  pl.BlockSpec(blk, lambda hh: (0, 3 * h + hh)),
                  col, col,
                  pl.BlockSpec((nseq, None, nch, HG_DK, HG_DK), lambda hh: (0, hh, 0, 0, 0)),
                  vec, vec],
        out_specs=[col] * 4 + [vec, vec],
        out_shape=[jax.ShapeDtypeStruct((t, D_MODEL), F32)] * 4 + [jax.ShapeDtypeStruct((1, D_MODEL), F32)] * 2,
        compiler_params=pltpu.CompilerParams(dimension_semantics=("parallel",), vmem_limit_bytes=HGRN_BWD_VMEM))(
            proj, proj, proj, proj, o, dog, states, lb, ng)


def _lower_bound(gamma):
    p = jax.nn.softmax(gamma, axis=0)
    return (jnp.cumsum(p, axis=0) - p[0])[1][None, :]


def local_step(x, tgt, w, hooks=None):
    nseq, seq, _ = x.shape
    lp = -(-(N_META + seq) // SB_BLOCK) * SB_BLOCK
    t = nseq * lp
    pad = lp - N_META - seq
    h0 = jnp.concatenate([jnp.broadcast_to(w['meta'][None], (nseq, N_META, D_MODEL)), x,
                          jnp.zeros((nseq, pad, D_MODEL), F32)], axis=1).reshape(t, D_MODEL)
    tgt_p = jnp.pad(tgt, ((0, 0), (N_META, pad), (0, 0))).reshape(t, D_MODEL)
    row = lambda v: v.reshape(1, -1)
    g = {}

    proj = matmul("in_ab", h0, w['w_in_ab'], 'nn')
    att = attn_fwd(proj, nseq, lp, ex=hooks.gather_late if hooks else None)
    b_out, sb_tot, sb_first = att[:3]
    if hooks:
        w = {**w, **hooks.late_weights(att[3:])}
    s5_names = ['s5_lam_re', 's5_lam_im', 's5_log_dt', 's5_b_re', 's5_b_im', 's5_c_re', 's5_c_im', 's5_d']
    mats, mats_vjp = jax.vjp(s5_matrices, *[w[n][0] for n in s5_names])
    ug = _to_groups(proj[:, :S5_WIDTH])
    s5_pows = s5_scan_powers(*[w[n][0] for n in s5_names[:3]])
    yg, sst = s5_fwd(ug, *mats[:3], *s5_pows, nseq)
    ys5 = _from_groups(yg)
    y = gelu_fwd("gelu", ys5)
    gp = matmul("glu", y, w['s5_w_glu'], 'nn', bias=w['s5_b_glu'])
    a_out = glu_gate("glu_gate", y, gp)
    cat = jnp.concatenate([a_out, b_out], axis=1)
    mix0 = matmul("out_ab", cat, w['w_out_ab'], 'nn')
    h1, xh1, rs1 = ln_fwd("ln_mix0", h0, mix0, row(w['ln_mix_g'][0]), row(w['ln_mix_b'][0]))
    hid0 = matmul("up0", h1, w['mlp_w_up'][0], 'nn', bias=row(w['mlp_b_up'][0]), act='relu2', out_dtype=BF16)
    dn0 = matmul("down0", hid0, w['mlp_w_down'][0], 'nn', bias=row(w['mlp_b_down'][0]))
    h2, xh2, rs2 = ln_fwd("ln_mlp0", h1, dn0, row(w['ln_mlp_g'][0]), row(w['ln_mlp_b'][0]))

    projc = matmul("in_c", h2, w['w_in_c'], 'nn')
    lb, lb_vjp = jax.vjp(_lower_bound, w['hgrn_gamma'])
    ng = w['hgrn_norm_g']
    res = hgrn_fwd(projc, lb, ng, nseq, lp, ex=hooks.gather_last if hooks else None)
    o, og, states = res[:3]
    if hooks:
        w = {**w, **hooks.last_weights(res[3:])}
    mix1 = matmul("out_c", og, w['w_out_c'], 'nn')
    h3, xh3, rs3 = ln_fwd("ln_mix1", h2, mix1, row(w['ln_mix_g'][1]), row(w['ln_mix_b'][1]))
    hid1 = matmul("up1", h3, w['mlp_w_up'][1], 'nn', bias=row(w['mlp_b_up'][1]), act='relu2', out_dtype=BF16)
    dn1 = matmul("down1", hid1, w['mlp_w_down'][1], 'nn', bias=row(w['mlp_b_down'][1]))
    h4, xh4, rs4 = ln_fwd("ln_mlp1", h3, dn1, row(w['ln_mlp_g'][1]), row(w['ln_mlp_b'][1]))
    dy, loss = loss_head("loss", h4, tgt_p, lp, seq)

    def mlp_bwd(l, dh_out, xh_out, rs_out, hid, h_in):
        dpre, dg_, db_, dbd = ln_bwd(f"ln_mlp{l}_b", dh_out, xh_out, rs_out, row(w['ln_mlp_g'][l]))
        dpu = matmul(f"down{l}_dx", dpre, w['mlp_w_down'][l], 'nt', relu2_of=hid, out_dtype=BF16)
        dwd = matmul(f"down{l}_dw", hid, dpre, 'tn')
        dh_in = matmul(f"up{l}_dx", dpu, w['mlp_w_up'][l], 'nt', add=dpre, add_scale=ALPHA)
        dwu, dbu = matmul(f"up{l}_dw", h_in, dpu, 'tn', out_slots=N_CHIPS, colsum=True)
        return dh_in, dict(ln_mlp_g=dg_, ln_mlp_b=db_, mlp_b_down=dbd, mlp_b_up=dbu, mlp_w_up=dwu, mlp_w_down=dwd)

    dh3, gm1 = mlp_bwd(1, dy, xh4, rs4, hid1, h3)
    dpre, dg1, db1, _ = ln_bwd("ln_mix1_b", dh3, xh3, rs3, row(w['ln_mix_g'][1]))
    g['w_out_c'] = matmul("out_c_dw", og, dpre, 'tn')
    dog = matmul("out_c_dx", dpre, w['w_out_c'], 'nt')
    dq, df, di, dgg, dlb, dng = hgrn_bwd(projc, o, dog, states, lb, ng, nseq, lp)
    dprojc = jnp.concatenate([dq, df, di, dgg], axis=1)
    dh2 = matmul("in_c_dx", dprojc, w['w_in_c'], 'nt', add=dpre, add_scale=ALPHA)
    g['w_in_c'] = matmul("in_c_dw", h2, dprojc, 'tn', out_slots=N_CHIPS)
    g['hgrn_gamma'] = lb_vjp(dlb)[0]
    g['hgrn_norm_g'] = dng

    dh1, gm0 = mlp_bwd(0, dh2, xh2, rs2, hid0, h1)
    dpre, dg0, db0, _ = ln_bwd("ln_mix0_b", dh1, xh1, rs1, row(w['ln_mix_g'][0]))
    g['w_out_ab'] = matmul("out_ab_dw", cat, dpre, 'tn')
    dcat = matmul("out_ab_dx", dpre, w['w_out_ab'], 'nt')
    dgp, da_s, dbglu = glu_gate_bwd("glu_gate_b", dcat[:, :S5_WIDTH], y, gp)
    g['s5_w_glu'] = matmul("glu_dw", y, dgp, 'tn')
    g['s5_b_glu'] = dbglu
    dy5 = matmul("glu_dx", dgp, w['s5_w_glu'], 'nt', add=da_s)
    dys5 = gelu_bwd("gelu_b", dy5, ys5)
    for n in ('mlp_w_up', 'mlp_w_down'):
        g[n] = [gm0[n], gm1[n]]
    g['ln_mix_g'] = jnp.concatenate([dg0, dg1], axis=0)
    g['ln_mix_b'] = jnp.concatenate([db0, db1], axis=0)
    for n in ('ln_mlp_g', 'ln_mlp_b', 'mlp_b_down', 'mlp_b_up'):
        g[n] = jnp.concatenate([gm0[n], gm1[n]], axis=0)
    res = s5_bwd(_to_groups(dys5), ug, sst, *mats[:3], *s5_pows, nseq, ex=hooks.fold_early(g) if hooks else None)
    dug, da1, da2, da3, dare, daim = res[:6]
    for n, v in zip(s5_names, mats_vjp((da1, da2, da3, dare, daim))):
        g[n] = v[None]
    ex = hooks.scatter_early(g, loss, res[6:]) if hooks else None
    res = attn_bwd(proj, sb_tot, sb_first, dcat[:, S5_WIDTH:], nseq, lp, ex=ex)
    dqa, dka, dva = res[:3]
    if hooks:
        hooks.scattered(res[3:])
    dproj = jnp.concatenate([_from_groups(dug), dqa, dka, dva], axis=1)
    dh0 = matmul("in_ab_dx", dproj, w['w_in_ab'], 'nt', add=dpre, add_scale=ALPHA)
    g['w_in_ab'] = matmul("in_ab_dw", h0, dproj, 'tn', out_slots=N_CHIPS)

    dh0 = dh0.reshape(nseq, lp, D_MODEL)
    grad_x = dh0[:, N_META:N_META + seq]
    g['meta'] = jnp.sum(dh0[:, :N_META], axis=0)
    return loss, grad_x, g


class Exchange:
    def __init__(self, ins, out_shapes, n_remote, build, in_place):
        self.ins, self.out_shapes, self.n_remote, self.build, self.in_place = ins, out_shapes, n_remote, build, in_place

    def sems(self):
        return [pltpu.SemaphoreType.DMA((self.n_remote,)), pltpu.SemaphoreType.DMA((self.n_remote,))]

    def aliases(self):
        return self.in_place if isinstance(self.in_place, dict) else \
            {i: i for i in range(len(self.ins) if self.in_place else 0)}

    def beside(self, other):
        na, nao = len(self.ins), len(self.out_shapes)

        def build(in_refs, out_refs, me):
            pa = self.build(in_refs[:na], out_refs[:nao], me)
            pb = other.build(in_refs[na:], out_refs[nao:], me)
            n = max(len(pa), len(pb))
            return [(pa[i] if i < len(pa) else []) + (pb[i] if i < len(pb) else []) for i in range(n)]
        al = dict(self.aliases())
        al.update({na + i: nao + j for i, j in other.aliases().items()})
        return Exchange(list(self.ins) + list(other.ins), list(self.out_shapes) + list(other.out_shapes),
                        self.n_remote + other.n_remote, build, al)

    def phases(self, in_refs, out_refs, ssem, rsem):
        me = (lax.axis_index("x"), lax.axis_index("y"), lax.axis_index("c"))
        out, k = [], 0
        for phase in self.build(in_refs, out_refs, me):
            out.append(functools.partial(
                lambda phase, k: [pltpu.make_async_remote_copy(src_ref=s, dst_ref=d, send_sem=ssem.at[k + i],
                                                               recv_sem=rsem.at[k + i], device_id=p,
                                                               device_id_type=MESH)
                                  for i, (s, d, p) in enumerate(phase)], phase, k))
            k += len(phase)
        return out

    def start(self, in_refs, out_refs, ssem, rsem):
        for cp in self.phases(in_refs, out_refs, ssem, rsem)[0]():
            cp.start()

    def finish(self, in_refs, out_refs, ssem, rsem):
        phases = self.phases(in_refs, out_refs, ssem, rsem)
        for cp in phases[0]():
            cp.wait()
        for make in phases[1:]:
            copies = make()
            for cp in copies:
                cp.start()
            for cp in copies:
                cp.wait()


def _exchange(name, ex):
    ni, no = len(ex.ins), len(ex.out_shapes)

    def body(*refs):
        in_refs, out_refs, sems = refs[:ni], refs[ni:ni + no], refs[ni + no:]
        ex.start(in_refs, out_refs, *sems)
        ex.finish(in_refs, out_refs, *sems)

    anyspec = pl.BlockSpec(memory_space=pl.ANY)
    return pl.pallas_call(
        body, name=name, in_specs=[anyspec] * ni, out_specs=[anyspec] * no, out_shape=ex.out_shapes,
        input_output_aliases=ex.aliases(), scratch_shapes=ex.sems())(*ex.ins)


def _chip_peers(me):
    x, y, c = me
    return [(x ^ (m >> 1), y ^ (m & 1), c) for m in (1, 2, 3)]


def _half(ref, c, axis):
    h = ref.shape[axis] // 2
    idx = [slice(None)] * axis + [pl.ds(c * h, h)]
    return ref.at[tuple(idx)]


def _like(arrs):
    return [jax.ShapeDtypeStruct(a.shape, a.dtype) for a in arrs]


def gather_over_chips(bufs):
    def build(in_refs, out_refs, me):
        x, y, c = me
        j = 2 * x + y
        sib = (x, y, 1 - c)
        ici = [(_half(o.at[j], c, 0), _half(o.at[j], c, 0), p) for o in out_refs for p in _chip_peers(me)]
        d2d = [(_half(o.at[2 * p[0] + p[1]], c, 0), _half(o.at[2 * p[0] + p[1]], c, 0), sib)
               for o in out_refs for p in _chip_peers(me)]
        return [ici, d2d]
    return Exchange(bufs, _like(bufs), 6 * len(bufs), build, True)


def fold_cores(fulls):
    def build(in_refs, out_refs, me):
        x, y, c = me
        return [[(_half(s, 1 - c, 1), o, (x, y, 1 - c)) for s, o in zip(in_refs, out_refs)]]
    shapes = [jax.ShapeDtypeStruct((f.shape[0], f.shape[1] // 2, f.shape[2]), f.dtype) for f in fulls]
    return Exchange(fulls, shapes, len(fulls), build, False)


def scatter_over_chips(parts):
    def build(in_refs, out_refs, me):
        j = 2 * me[0] + me[1]
        return [[(s.at[2 * p[0] + p[1]], o.at[j], p) for s, o in zip(in_refs, out_refs) for p in _chip_peers(me)]]
    return Exchange(parts, _like(parts), 3 * len(parts), build, False)


def join_cores(bufs):
    def build(in_refs, out_refs, me):
        x, y, c = me
        return [[(o.at[c], o.at[c], (x, y, 1 - c)) for o in out_refs]]
    return Exchange(bufs, _like(bufs), len(bufs), build, True)


def gather_over_devices(buf):
    def build(in_refs, out_refs, me):
        x, y, c = me
        i = 4 * x + 2 * y + c
        out = out_refs[0]
        return [[(out.at[i], out.at[i], (x ^ (m >> 2), y ^ ((m >> 1) & 1), c ^ (m & 1))) for m in range(1, N_DEV)]]
    return Exchange([buf], _like([buf]), N_DEV - 1, build, True)


def _slot_call(name, body, scalars, ins, in_maps, out_shape, out_map, blk, grid):
    gs = pltpu.PrefetchScalarGridSpec(
        num_scalar_prefetch=1, grid=grid,
        in_specs=[pl.BlockSpec((None,) + blk, m) for m in in_maps],
        out_specs=pl.BlockSpec((None,) + blk, out_map))
    return pl.pallas_call(body, name=name, grid_spec=gs, out_shape=out_shape,
                          compiler_params=_params(("arbitrary",) * len(grid)))(scalars, *ins)


def cast_into_slot(name, w2d, chip, dtype):
    r, wd = w2d.shape
    tm = _pick(r, (512, 256, 128, 64, 32, 16))

    def body(s_ref, w_ref, o_ref):
        o_ref[...] = w_ref[...].astype(o_ref.dtype)

    gs = pltpu.PrefetchScalarGridSpec(
        num_scalar_prefetch=1, grid=(r // tm,),
        in_specs=[pl.BlockSpec((tm, wd), lambda i, s: (i, 0))],
        out_specs=pl.BlockSpec((None, tm, wd), lambda i, s: (s[0], i, 0)))
    return pl.pallas_call(body, name=name, grid_spec=gs,
                          out_shape=jax.ShapeDtypeStruct((N_CHIPS, r, wd), dtype),
                          compiler_params=_params(("arbitrary",)))(chip.reshape(1), w2d)


def sum_cores(name, full, theirs, core):
    s, r, wd = full.shape
    h = r // 2
    tm = _pick(h, (512, 256, 128, 64, 32, 16))
    nt = h // tm

    def body(s_ref, a_ref, b_ref, o_ref):
        o_ref[...] = (a_ref[...] + b_ref[...]).astype(o_ref.dtype)

    return _slot_call(name, body, core.reshape(1), [full, theirs],
                      [lambda k, i, s_: (k, s_[0] * nt + i, 0), lambda k, i, s_: (k, i, 0)],
                      jax.ShapeDtypeStruct((s, h, wd), BF16), lambda k, i, s_: (k, i, 0), (tm, wd), (s, nt))


def sum_chips(name, own, recv, chip, core):
    s, r, wd = own.shape
    tm = _pick(r, (512, 256, 128, 64, 32, 16))

    def body(s_ref, a_ref, b1_ref, b2_ref, b3_ref, o_ref):
        acc = a_ref[...].astype(F32)
        for ref in (b1_ref, b2_ref, b3_ref):
            acc = acc + ref[...].astype(F32)
        o_ref[...] = acc

    maps = [lambda i, s_: (s_[0], i, 0)]
    maps += [functools.partial(lambda i, s_, m: (s_[0] ^ m, i, 0), m=m) for m in (1, 2, 3)]
    return _slot_call(name, body, jnp.stack([chip, core]), [own, recv, recv, recv], maps,
                      jax.ShapeDtypeStruct((2, r, wd), F32), lambda i, s_: (s_[1], i, 0), (tm, wd), (r // tm,))


def sum_slots(name, arr):
    s, r, wd = arr.shape
    tm = _pick(r, (512, 256, 128, 64, 32, 16, 8))

    def body(*refs):
        acc = refs[0][...].astype(F32)
        for ref in refs[1:s]:
            acc = acc + ref[...].astype(F32)
        refs[s][...] = acc

    specs = [pl.BlockSpec((None, tm, wd), functools.partial(lambda i, k: (k, i, 0), k=k)) for k in range(s)]
    return pl.pallas_call(body, name=name, grid=(r // tm,), in_specs=specs,
                          out_specs=pl.BlockSpec((tm, wd), lambda i: (i, 0)),
                          out_shape=jax.ShapeDtypeStruct((r, wd), F32),
                          compiler_params=_params(("parallel",)))(*([arr] * s))


def _pack(arrs):
    flat = jnp.concatenate([a.reshape(-1) for a in arrs])
    n = flat.shape[0]
    padded = -(-n // (512 * LANES)) * (512 * LANES)
    return jnp.pad(flat, (0, padded - n)).reshape(-1, LANES)


def _unpack(packed, shapes):
    flat = packed.reshape(-1)
    out, pos = [], 0
    for s in shapes:
        n = math.prod(s)
        out.append(flat[pos:pos + n].reshape(s))
        pos += n
    return out


def _as2d(a):
    return a.reshape(-1, a.shape[-1])


def kernel(x, meta, w_in_ab, s5_lam_re, s5_lam_im, s5_log_dt, s5_b_re, s5_b_im, s5_c_re, s5_c_im, s5_d, s5_w_glu, s5_b_glu, w_out_ab, w_in_c, hgrn_gamma, hgrn_norm_g, w_out_c, ln_mix_g, ln_mix_b, mlp_w_up, mlp_b_up, mlp_w_down, mlp_b_down, ln_mlp_g, ln_mlp_b, loss_target, m_meta, m_w_in_ab, m_s5_lam_re, m_s5_lam_im, m_s5_log_dt, m_s5_b_re, m_s5_b_im, m_s5_c_re, m_s5_c_im, m_s5_d, m_s5_w_glu, m_s5_b_glu, m_w_out_ab, m_w_in_c, m_hgrn_gamma, m_hgrn_norm_g, m_w_out_c, m_ln_mix_g, m_ln_mix_b, m_mlp_w_up, m_mlp_b_up, m_mlp_w_down, m_mlp_b_down, m_ln_mlp_g, m_ln_mlp_b, v_meta, v_w_in_ab, v_s5_lam_re, v_s5_lam_im, v_s5_log_dt, v_s5_b_re, v_s5_b_im, v_s5_c_re, v_s5_c_im, v_s5_d, v_s5_w_glu, v_s5_b_glu, v_w_out_ab, v_w_in_c, v_hgrn_gamma, v_hgrn_norm_g, v_w_out_c, v_ln_mix_g, v_ln_mix_b, v_mlp_w_up, v_mlp_b_up, v_mlp_w_down, v_mlp_b_down, v_ln_mlp_g, v_ln_mlp_b):
    given = dict(locals())
    wts = {n: given[n] for n in WEIGHTS}
    ms = {n: given['m_' + n] for n in WEIGHTS}
    vs = {n: given['v_' + n] for n in WEIGHTS}
    chip = 2 * lax.axis_index("x") + lax.axis_index("y")

    core = lax.axis_index("c")
    parts = [(n, l) for n in BIG for l in (range(DEPTH) if n.startswith('mlp_') else [0])]
    tags = [f"{n}{l}" for n, l in parts]
    shards = {p: cast_into_slot("cast_" + t, wts[p[0]][p[1]], chip, BF16) for t, p in zip(tags, parts)}
    small_sh = jnp.concatenate([meta, hgrn_norm_g, jnp.zeros((15, meta.shape[1]), F32)], axis=0)
    first, late = parts[:1], parts[1:]
    w_in_ab_all, sm = _exchange("gather_first", gather_over_chips(
        [shards[p] for p in first] + [cast_into_slot("cast_small", small_sh, chip, F32)]))
    w = {n: wts[n] for n in SMALL}
    w['meta'] = sm[:, :N_META].transpose(1, 0, 2).reshape(N_META, D_MODEL)
    w['hgrn_norm_g'] = sm[:, N_META:N_META + 1].transpose(1, 0, 2).reshape(1, D_MODEL)
    w['w_in_ab'] = w_in_ab_all

    def slot_major(v):
        return v if v.ndim == 3 else v.reshape(N_CHIPS, -1, v.shape[-1])

    def whole_grads(ps, g):
        return [slot_major(g[n][l] if n.startswith('mlp_') else g[n]) for n, l in ps]

    def chip_sums_of(ps, gfull, theirs):
        return [sum_cores(f"sum_cores_{n}{l}", a, b, core) for (n, l), a, b in zip(ps, gfull, theirs)]

    def all_devices(arrs):
        packed = _pack(arrs)
        return gather_over_devices(lax.dynamic_update_slice(
            jnp.zeros((N_DEV,) + packed.shape, F32), packed[None], (2 * chip + core, 0, 0)))

    early_small = [n for n in SMALL if n != 'meta']
    last = [('w_out_c', 0), ('mlp_w_up', 1), ('mlp_w_down', 1)]
    mid = [p for p in late if p not in last]

    class Hooks:
        gather_late = gather_over_chips([shards[p] for p in mid])
        gather_last = gather_over_chips([shards[p] for p in last])

        @staticmethod
        def late_weights(filled):
            fw = Hooks.fw = dict(zip(mid, filled))
            return {'s5_w_glu': fw['s5_w_glu', 0].reshape(S5_WIDTH, S5_WIDTH),
                    'w_out_ab': fw['w_out_ab', 0].reshape(D_MODEL, D_MODEL),
                    'w_in_c': fw['w_in_c', 0],
                    'mlp_w_up': [fw['mlp_w_up', 0]],
                    'mlp_w_down': [fw['mlp_w_down', 0].reshape(D_FF, D_MODEL)]}

        @staticmethod
        def last_weights(filled):
            fw = dict(zip(last, filled))
            return {'w_out_c': fw['w_out_c', 0].reshape(D_MODEL, D_MODEL),
                    'mlp_w_up': [Hooks.fw['mlp_w_up', 0], fw['mlp_w_up', 1]],
                    'mlp_w_down': [Hooks.fw['mlp_w_down', 0].reshape(D_FF, D_MODEL),
                                   fw['mlp_w_down', 1].reshape(D_FF, D_MODEL)]}

        @staticmethod
        def fold_early(g):
            Hooks.whole = whole_grads(late, g)
            return fold_cores(Hooks.whole)

        @staticmethod
        def scatter_early(g, loss, theirs):
            Hooks.sums = chip_sums_of(late, Hooks.whole, theirs)
            Hooks.small_shapes = [(LANES,)] + [g[n].shape for n in early_small]
            return scatter_over_chips(Hooks.sums).beside(
                all_devices([loss.reshape(-1)] + [g[n] for n in early_small]))

        @staticmethod
        def scattered(outs):
            Hooks.recv, Hooks.small = list(outs[:-1]), outs[-1]

    loss, grad_x, g = local_step(x, loss_target, w, Hooks)

    whole = whole_grads(first, g)
    sums = chip_sums_of(first, whole, _exchange("fold_last", fold_cores(whole)))
    *recv, meta_slots = _exchange("scatter_last", scatter_over_chips(sums).beside(all_devices([g['meta']])))
    reduced = _exchange("join_grads", join_cores(
        [sum_chips(f"sum_chips_{n}{l}", a, b, chip, core)
         for (n, l), a, b in zip(first + late, sums + Hooks.sums, list(recv) + Hooks.recv)]))
    reduced = dict(zip(first + late, [_as2d(r) for r in reduced]))
    red = _unpack(sum_slots("sum_small", Hooks.small), Hooks.small_shapes)
    loss_out = red[0][0]
    gs = dict(zip(early_small, red[1:]))
    gs['meta'] = _unpack(sum_slots("sum_meta", meta_slots), [g['meta'].shape])[0]
    gs['meta'] = lax.dynamic_slice_in_dim(gs['meta'], chip * meta.shape[1], meta.shape[1], axis=1)
    gs['hgrn_norm_g'] = lax.dynamic_slice_in_dim(gs['hgrn_norm_g'].reshape(1, -1), chip * meta.shape[1],
                                                 meta.shape[1], axis=1)

    grads, deltas, new_m, new_v = {}, {}, {}, {}
    for n in BIG:
        r = jnp.concatenate([reduced[n, l] for l in range(wts[n].shape[0])], axis=0)
        res = adamw("adamw_" + n, _as2d(wts[n]), [r], _as2d(ms[n]), _as2d(vs[n]))
        grads[n], deltas[n], new_m[n], new_v[n] = [r.reshape(wts[n].shape) for r in res]
    shapes = [wts[n].shape for n in SMALL]
    res = adamw("adamw_small", _pack([wts[n] for n in SMALL]), [_pack([gs[n] for n in SMALL])],
                _pack([ms[n] for n in SMALL]), _pack([vs[n] for n in SMALL]))
    for d, r in zip((grads, deltas, new_m, new_v), res):
        d.update(zip(SMALL, _unpack(r, shapes)))
    return (loss_out, grad_x, *[grads[n] for n in WEIGHTS], *[deltas[n] for n in WEIGHTS],
            *[new_m[n] for n in WEIGHTS], *[new_v[n] for n in WEIGHTS])
```

```python
import functools
import math

import jax
import jax.numpy as jnp
from jax import lax
from jax.experimental import pallas as pl
from jax.experimental.pallas import tpu as pltpu

F32 = jnp.float32
BF16 = jnp.bfloat16

D_MODEL = 1024
N_META = 16
DEPTH = 2
S5_WIDTH = 512
S5_GROUP = 16
S5_GROUPS = 32
S5_STATE = 64
S5_CHUNK = 16
SB_WIDTH = 512
SB_BLOCK = 128
SB_DEAD = -110.0
HG_HEADS = 8
HG_DK = 128
HG_CHUNK = 64
HG_UNROLL = 2
D_FF = 4096
ALPHA = (2.0 * DEPTH) ** 0.25
LN_EPS = 1e-5
RMS_EPS = 1e-6
ADAM_LR = 0.001
ADAM_B1 = 0.9
ADAM_B2 = 0.999
ADAM_EPS = 1e-08
ADAM_WD = 0.01
ADAM_STEP = 10
N_CHIPS = 4
N_DEV = 8
LANES = 128
MESH = pl.DeviceIdType.MESH
VMEM_LIMIT = 56 * 1024 * 1024
WHOLE_K_BYTES = 42 * 1024 * 1024
HGRN_BWD_VMEM = 60 * 1024 * 1024

WEIGHTS = ['meta', 'w_in_ab', 's5_lam_re', 's5_lam_im', 's5_log_dt', 's5_b_re', 's5_b_im', 's5_c_re', 's5_c_im',
           's5_d', 's5_w_glu', 's5_b_glu', 'w_out_ab', 'w_in_c', 'hgrn_gamma', 'hgrn_norm_g', 'w_out_c',
           'ln_mix_g', 'ln_mix_b', 'mlp_w_up', 'mlp_b_up', 'mlp_w_down', 'mlp_b_down', 'ln_mlp_g', 'ln_mlp_b']
BIG = ['w_in_ab', 's5_w_glu', 'w_out_ab', 'w_in_c', 'w_out_c', 'mlp_w_up', 'mlp_w_down']
SMALL = [n for n in WEIGHTS if n not in BIG]


def _params(sem=None):
    return pltpu.CompilerParams(dimension_semantics=sem, vmem_limit_bytes=VMEM_LIMIT)


def _pick(n, cands):
    for c in cands:
        if n % c == 0:
            return c
    return n


def _dot(a, b, ca, cb):
    return lax.dot_general(a, b, (((ca,), (cb,)), ((), ())), preferred_element_type=F32)


def _split3(x):
    hi = x.astype(BF16)
    r = x - hi.astype(F32)
    mid = r.astype(BF16)
    lo = (r - mid.astype(F32)).astype(BF16)
    return hi, mid, lo


def _dot_exact_rhs(x, m, cx, cm):
    hi = x.astype(BF16)
    lo = (x - hi.astype(F32)).astype(BF16)
    return _dot(hi, m, cx, cm) + _dot(lo, m, cx, cm)


def _dot3(a, b, ca, cb):
    ah = a.astype(BF16)
    al = (a - ah.astype(F32)).astype(BF16)
    bh = b.astype(BF16)
    bl = (b - bh.astype(F32)).astype(BF16)
    return _dot(ah, bh, ca, cb) + (_dot(ah, bl, ca, cb) + _dot(al, bh, ca, cb))


def rowwise(name, fn, row_ins, bc_ins, out_widths, sum_widths=(), out_dtypes=None, tm=None):
    t = row_ins[0].shape[0]
    if tm is None:
        wmax = max([a.shape[1] for a in row_ins] + list(out_widths))
        tm = _pick(t, (272, 256, 128, 64, 16, 8)) if wmax > 1024 else _pick(t, (544, 512, 256, 128, 64, 16, 8))
    nr, nb, no, ns = len(row_ins), len(bc_ins), len(out_widths), len(sum_widths)
    out_dtypes = out_dtypes or [F32] * no

    def body(*refs):
        i = pl.program_id(0)
        rows = [r[...] for r in refs[:nr]]
        bcs = [r[...] for r in refs[nr:nr + nb]]
        outs, sums = fn(i, tm, rows, bcs)
        for r, v in zip(refs[nr + nb:nr + nb + no], outs):
            r[...] = v.astype(r.dtype)
        if ns:
            srefs = refs[nr + nb + no:]

            @pl.when(i == 0)
            def _():
                for r in srefs:
                    r[...] = jnp.zeros(r.shape, r.dtype)

            for r, v in zip(srefs, sums):
                r[...] += v

    in_specs = [pl.BlockSpec((tm, a.shape[1]), lambda i: (i, 0)) for a in row_ins]
    in_specs += [pl.BlockSpec(a.shape, lambda i: (0, 0)) for a in bc_ins]
    out_specs = [pl.BlockSpec((tm, w), lambda i: (i, 0)) for w in out_widths]
    out_specs += [pl.BlockSpec((1, w), lambda i: (0, 0)) for w in sum_widths]
    out_shape = [jax.ShapeDtypeStruct((t, w), dt) for w, dt in zip(out_widths, out_dtypes)]
    out_shape += [jax.ShapeDtypeStruct((1, w), F32) for w in sum_widths]
    res = pl.pallas_call(body, name=name, grid=(t // tm,), in_specs=in_specs, out_specs=out_specs,
                         out_shape=out_shape, compiler_params=_params(("arbitrary",)))(*row_ins, *bc_ins)
    return res


def _colsum(v):
    return jnp.sum(v, axis=0, keepdims=True)


def matmul(name, a, b, mode, *, bias=None, act=None, add=None, add_scale=1.0, relu2_of=None, colsum=False,
           out_dtype=F32, out_slots=0, tm=None, tn=None, tk=None):
    slotted = b.ndim == 3
    if mode == 'nn':
        m, k = a.shape
        n = b.shape[-1] * (b.shape[0] if slotted else 1)
    elif mode == 'nt':
        m, k = a.shape
        n = b.shape[-2]
    else:
        k, m = a.shape
        n = b.shape[-1]
    ns = b.shape[-1] if slotted else None
    if mode == 'tn':
        tm = tm or _pick(m, (512, 256, 128))
        nw = n if not out_slots else n // out_slots
        if tn is None and tk is None:
            for cand in (512, 256):
                if nw % cand == 0 and (2 * k * (tm * a.dtype.itemsize + cand * b.dtype.itemsize)
                                       + 2 * tm * cand * 4 <= WHOLE_K_BYTES):
                    tn, tk = cand, k
                    break
        tn = tn or _pick(nw, (1024, 512, 256, 128))
        tk = tk or _pick(k, (1088, 1024, 768, 512, 384, 256, 128))
    else:
        tm = tm or _pick(m, (1088, 1024, 768, 512, 384, 256, 128))
        tn = tn or _pick(n if not (slotted and mode == 'nn') else ns, (1024, 512, 256, 128))
        extra = sum(x is not None for x in (add, relu2_of)) * 4
        if tk is None and not slotted and (2 * (tm * k * a.dtype.itemsize + k * tn * b.dtype.itemsize)
                                           + 2 * tm * tn * (4 + extra) <= WHOLE_K_BYTES):
            tk = k
        tk = tk or _pick(k if not (slotted and mode == 'nt') else ns, (1024, 512, 256, 128))
    gm, gn, gk = m // tm, n // tn, k // tk

    if mode == 'nn':
        a_spec = pl.BlockSpec((tm, tk), lambda i, j, kk: (i, kk))
        if slotted:
            per = ns // tn
            b_spec = pl.BlockSpec((None, tk, tn), lambda i, j, kk: (j // per, kk, j % per))
        else:
            b_spec = pl.BlockSpec((tk, tn), lambda i, j, kk: (kk, j))
        ca, cb = 1, 0
    elif mode == 'nt':
        a_spec = pl.BlockSpec((tm, tk), lambda i, j, kk: (i, kk))
        if slotted:
            per = ns // tk
            b_spec = pl.BlockSpec((None, tn, tk), lambda i, j, kk: (kk // per, j, kk % per))
        else:
            b_spec = pl.BlockSpec((tn, tk), lambda i, j, kk: (j, kk))
        ca, cb = 1, 1
    else:
        a_spec = pl.BlockSpec((tk, tm), lambda i, j, kk: (kk, i))
        b_spec = pl.BlockSpec((tk, tn), lambda i, j, kk: (kk, j))
        ca, cb = 0, 0
    in_specs = [a_spec, b_spec]
    args = [a, b]
    if bias is not None:
        in_specs.append(pl.BlockSpec((1, tn), lambda i, j, kk: (0, j)))
        args.append(bias)
    if add is not None:
        in_specs.append(pl.BlockSpec((tm, tn), lambda i, j, kk: (i, j)))
        args.append(add)
    if relu2_of is not None:
        in_specs.append(pl.BlockSpec((tm, tn), lambda i, j, kk: (i, j)))
        args.append(relu2_of)
    if out_slots:
        per_o = (n // out_slots) // tn
        out_spec = pl.BlockSpec((None, tm, tn), lambda i, j, kk: (j // per_o, i, j % per_o))
        out_shape = jax.ShapeDtypeStruct((out_slots, m, n // out_slots), out_dtype)
    else:
        out_spec = pl.BlockSpec((tm, tn), lambda i, j, kk: (i, j))
        out_shape = jax.ShapeDtypeStruct((m, n), out_dtype)
    grid = (gm, gn, gk)
    if colsum:
        assert mode == 'tn'
        out_spec = [out_spec, pl.BlockSpec((1, tn), lambda i, j, kk: (0, j))]
        out_shape = [out_shape, jax.ShapeDtypeStruct((1, n), F32)]

        def swapped(spec):
            return pl.BlockSpec(spec.block_shape, functools.partial(lambda j, i, kk, f: f(i, j, kk), f=spec.index_map))
        in_specs = [swapped(sp) for sp in in_specs]
        out_spec = [swapped(sp) for sp in out_spec]
        grid = (gn, gm, gk)

    def body(*refs):
        a_ref, b_ref = refs[0], refs[1]
        pos = 2
        bias_ref = add_ref = hid_ref = cs_ref = None
        if bias is not None:
            bias_ref = refs[pos]
            pos += 1
        if add is not None:
            add_ref = refs[pos]
            pos += 1
        if relu2_of is not None:
            hid_ref = refs[pos]
            pos += 1
        o_ref = refs[pos]
        pos += 1
        if colsum:
            cs_ref = refs[pos]
            pos += 1
        acc_ref = refs[pos] if gk > 1 else None
        bt = b_ref[...]
        p = _dot(a_ref[...].astype(BF16), bt.astype(BF16), ca, cb)
        kk = pl.program_id(2)

        if colsum:
            @pl.when(jnp.logical_and(pl.program_id(1) == 0, kk == 0))
            def _():
                cs_ref[...] = _colsum(bt.astype(F32))

            @pl.when(jnp.logical_and(pl.program_id(1) == 0, kk > 0))
            def _():
                cs_ref[...] += _colsum(bt.astype(F32))

        def finish(r):
            if bias_ref is not None:
                r = r + bias_ref[...]
            if act == 'relu2':
                r = jnp.square(jnp.maximum(r, 0.0))
            if hid_ref is not None:
                r = r * (2.0 * jnp.sqrt(hid_ref[...].astype(F32)))
            if add_ref is not None:
                r = r + add_scale * add_ref[...]
            o_ref[...] = r.astype(o_ref.dtype)

        if gk == 1:
            finish(p)
        else:
            @pl.when(kk == 0)
            def _():
                acc_ref[...] = p

            @pl.when(kk > 0)
            def _():
                acc_ref[...] += p

            @pl.when(kk == gk - 1)
            def _():
                finish(acc_ref[...])

    scratch = [pltpu.VMEM((tm, tn), F32)] if gk > 1 else []
    sem = ("arbitrary",) * 3 if colsum else ("parallel", "parallel", "arbitrary")
    return pl.pallas_call(body, name=name, grid=grid, in_specs=in_specs, out_specs=out_spec,
                          out_shape=out_shape, scratch_shapes=scratch, compiler_params=_params(sem))(*args)


def ln_fwd(name, h, mix, g, b):
    def fn(i, tm, rows, bcs):
        pre = ALPHA * rows[0] + rows[1]
        mu = jnp.mean(pre, axis=1, keepdims=True)
        xc = pre - mu
        var = jnp.mean(xc * xc, axis=1, keepdims=True)
        rstd = lax.rsqrt(var + LN_EPS)
        xhat = xc * rstd
        return (xhat * bcs[0] + bcs[1], xhat, rstd), ()
    return rowwise(name, fn, [h, mix], [g, b], [D_MODEL, D_MODEL, 1])


def ln_bwd(name, dy, xhat, rstd, g):
    def fn(i, tm, rows, bcs):
        dy_, xh, rs = rows
        dyg = dy_ * bcs[0]
        m1 = jnp.mean(dyg, axis=1, keepdims=True)
        m2 = jnp.mean(dyg * xh, axis=1, keepdims=True)
        dpre = rs * (dyg - m1 - xh * m2)
        return (dpre,), (_colsum(dy_ * xh), _colsum(dy_), _colsum(dpre))
    return rowwise(name, fn, [dy, xhat, rstd], [g], [D_MODEL], [D_MODEL] * 3)


_GELU_C = math.sqrt(2.0 / math.pi)


def gelu_fwd(name, x):
    def fn(i, tm, rows, bcs):
        v = rows[0]
        u = _GELU_C * (v + 0.044715 * (v * v * v))
        return (0.5 * v * (1.0 + jnp.tanh(u)),), ()
    return rowwise(name, fn, [x], [], [x.shape[1]])[0]


def gelu_bwd(name, dy, x):
    def fn(i, tm, rows, bcs):
        v = rows[1]
        th = jnp.tanh(_GELU_C * (v + 0.044715 * (v * v * v)))
        dg = 0.5 * (1.0 + th) + 0.5 * v * (1.0 - th * th) * (_GELU_C * (1.0 + 3.0 * 0.044715 * v * v))
        return (rows[0] * dg,), ()
    return rowwise(name, fn, [dy, x], [], [x.shape[1]])[0]


def glu_gate(name, y, gp):
    def fn(i, tm, rows, bcs):
        return (rows[0] * jax.nn.sigmoid(rows[1]),), ()
    return rowwise(name, fn, [y, gp], [], [y.shape[1]], out_dtypes=[BF16])[0]


def glu_gate_bwd(name, da, y, gp):
    def fn(i, tm, rows, bcs):
        s = jax.nn.sigmoid(rows[2])
        dgp = rows[0] * rows[1] * s * (1.0 - s)
        return (dgp, rows[0] * s), (_colsum(dgp),)
    w = y.shape[1]
    return rowwise(name, fn, [da, y, gp], [], [w, w], [w])


def loss_head(name, h, tgt, lp, seq):
    def fn(i, tm, rows, bcs):
        pos = (i * tm + lax.broadcasted_iota(jnp.int32, (tm, 1), 0)) % lp
        valid = jnp.logical_and(pos >= N_META, pos < N_META + seq)
        err = jnp.where(valid, rows[0] - rows[1], 0.0)
        part = 0.5 * jnp.sum(jnp.mean(err * err, axis=1, keepdims=True), axis=0, keepdims=True)
        return (err * (1.0 / D_MODEL),), (jnp.broadcast_to(part, (1, LANES)),)
    return rowwise(name, fn, [h, tgt], [], [D_MODEL], [LANES])


def adamw(name, w, gs, m, v):
    ng = len(gs)

    def fn(i, tm, rows, bcs):
        w_ = rows[0]
        g = rows[1] if ng == 1 else rows[1] + rows[2]
        m_, v_ = rows[1 + ng], rows[2 + ng]
        m_ = ADAM_B1 * m_ + (1.0 - ADAM_B1) * g
        v_ = ADAM_B2 * v_ + (1.0 - ADAM_B2) * jnp.square(g)
        m_hat = m_ / (1.0 - ADAM_B1 ** ADAM_STEP)
        v_hat = v_ / (1.0 - ADAM_B2 ** ADAM_STEP)
        delta = -ADAM_LR * (m_hat / (jnp.sqrt(v_hat) + ADAM_EPS) + ADAM_WD * w_)
        return (g, delta, m_, v_), ()
    wd = w.shape[1]
    return rowwise(name, fn, [w] + list(gs) + [m, v], [], [wd] * 4)


def s5_matrices(lam_re, lam_im, log_dt, b_re, b_im, c_re, c_im, d):
    c, hh, gg, pp = S5_CHUNK, S5_GROUP, S5_GROUPS, S5_STATE
    dt = jnp.exp(log_dt)[:, None]
    tau = jnp.arange(c + 1, dtype=F32)[None, :, None]
    mag = jnp.exp(tau * (lam_re * dt)[:, None, :])
    ang = tau * (lam_im * dt)[:, None, :]
    pw_re, pw_im = mag * jnp.cos(ang), mag * jnp.sin(ang)
    nr, ni = pw_re[:, 1] - 1.0, pw_im[:, 1]
    den = lam_re * lam_re + lam_im * lam_im
    cf_re = (nr * lam_re + ni * lam_im) / den
    cf_im = (ni * lam_re - nr * lam_im) / den
    bb_re = cf_re[:, :, None] * b_re - cf_im[:, :, None] * b_im
    bb_im = cf_re[:, :, None] * b_im + cf_im[:, :, None] * b_re
    ct_re, ct_im = c_re.transpose(0, 2, 1)[:, :, None, :], c_im.transpose(0, 2, 1)[:, :, None, :]
    pt_re, pt_im = pw_re.transpose(0, 2, 1)[:, :, :, None], pw_im.transpose(0, 2, 1)[:, :, :, None]
    cp_re = ct_re * pt_re - ct_im * pt_im
    cp_im = ct_re * pt_im + ct_im * pt_re
    hp = lax.Precision.HIGHEST
    kmat = (jnp.einsum('gpk,gpn->gkn', bb_re, cp_re[:, :, :c].reshape(gg, pp, c * hh), precision=hp)
            - jnp.einsum('gpk,gpn->gkn', bb_im, cp_im[:, :, :c].reshape(gg, pp, c * hh), precision=hp))
    rows = [jnp.pad(kmat[:, :, :(c - j) * hh], ((0, 0), (0, 0), (j * hh, 0))) for j in range(c)]
    a1 = jnp.stack(rows, axis=1).reshape(gg, c * hh, c * hh)
    a1 = a1 + jnp.eye(c * hh, dtype=F32)[None] * jnp.tile(d, (1, c))[:, None, :]
    a2 = jnp.concatenate([cp_re[:, :, 1:].reshape(gg, pp, c * hh), -cp_im[:, :, 1:].reshape(gg, pp, c * hh)], axis=1)
    rv_re, rv_im = pw_re[:, :c][:, ::-1, None, :], pw_im[:, :c][:, ::-1, None, :]
    bt_re, bt_im = bb_re.transpose(0, 2, 1)[:, None], bb_im.transpose(0, 2, 1)[:, None]
    a3 = jnp.concatenate([rv_re * bt_re - rv_im * bt_im, rv_re * bt_im + rv_im * bt_re], axis=-1)
    a3 = a3.reshape(gg, c * hh, 2 * pp)
    are = jnp.concatenate([pw_re[:, c], pw_re[:, c]], axis=-1)[:, None, :]
    aim = jnp.concatenate([-pw_im[:, c], pw_im[:, c]], axis=-1)[:, None, :]
    return a1, a2, a3, are, aim


S5_LEVELS = 8


def s5_scan_powers(lam_re, lam_im, log_dt):
    dt = jnp.exp(log_dt)[:, None]
    e = (S5_CHUNK * 2.0 ** jnp.arange(S5_LEVELS, dtype=F32))[:, None, None]
    mag = jnp.exp(e * (lam_re * dt)[None])
    ang = e * (lam_im * dt)[None]
    pr, pi = mag * jnp.cos(ang), mag * jnp.sin(ang)
    pre = jnp.concatenate([pr, pr], axis=-1).transpose(1, 0, 2)
    pim = jnp.concatenate([-pi, pi], axis=-1).transpose(1, 0, 2)
    return pre, pim


def _s5_scan(x, pre, pim, reverse):
    n = x.shape[0]
    rowi = lax.broadcasted_iota(jnp.int32, (n, 1), 0)
    t = x
    for k in range(S5_LEVELS):
        shift = 2 ** k
        if shift >= n:
            break
        p_re, p_im = pre[k:k + 1, :], pim[k:k + 1, :]
        if reverse:
            far = jnp.where(rowi < n - shift, pltpu.roll(t, n - shift, 0), 0.0)
            t = t + (p_re * far + pltpu.roll(p_im * far, S5_STATE, 1))
        else:
            far = jnp.where(rowi >= shift, pltpu.roll(t, shift, 0), 0.0)
            t = t + (p_re * far + p_im * pltpu.roll(far, S5_STATE, 1))
    return t


def s5_fwd(ug, a1, a2, a3, pre, pim, nseq):
    g, nc, cw = ug.shape
    per = nc // nseq
    sw = 2 * S5_STATE
    assert per <= 2 ** S5_LEVELS

    def body(u_ref, a1_ref, a2_ref, a3_ref, pre_ref, pim_ref, y_ref, s_ref):
        u = u_ref[...]
        x = _dot3(u, a3_ref[...], 1, 0)
        first = lax.broadcasted_iota(jnp.int32, (per, 1), 0) == 0
        for b in range(nseq):
            t = _s5_scan(x[b * per:(b + 1) * per], pre_ref[...], pim_ref[...], False)
            s_ref[pl.ds(b * per, per), :] = jnp.where(first, 0.0, pltpu.roll(t, 1, 0))
        y_ref[...] = _dot3(u, a1_ref[...], 1, 0) + _dot3(s_ref[...], a2_ref[...], 1, 0)

    def spec(shape):
        return pl.BlockSpec((None,) + shape, lambda i: (i, 0, 0))
    lv = (S5_LEVELS, sw)
    return pl.pallas_call(
        body, name="s5_fwd", grid=(g,),
        in_specs=[spec((nc, cw)), spec((cw, cw)), spec((sw, cw)), spec((cw, sw)), spec(lv), spec(lv)],
        out_specs=[spec((nc, cw)), spec((nc, sw))],
        out_shape=[jax.ShapeDtypeStruct((g, nc, cw), F32), jax.ShapeDtypeStruct((g, nc, sw), F32)],
        compiler_params=_params(("parallel",)))(ug, a1, a2, a3, pre, pim)


def s5_bwd(dyg, ug, sst, a1, a2, a3, pre, pim, nseq, ex=None):
    g, nc, cw = ug.shape
    per = nc // nseq
    sw = 2 * S5_STATE
    split, ex_start, ex_finish, exkw = _hidden_exchange(ex, 8, 6, (g,))

    def body(*refs):
        ((dy_ref, u_ref, s_ref, a1_ref, a2_ref, a3_ref, pre_ref, pim_ref),
         (du_ref, da1_ref, da2_ref, da3_ref, dare_ref, daim_ref), ex_parts, (dx_ref,)) = split(refs)
        ex_start(ex_parts)
        dy = dy_ref[...]
        u = u_ref[...]
        gd = _dot3(dy, a2_ref[...], 1, 1)
        s_all = s_ref[...]
        da2_ref[...] = _dot3(s_all, dy, 0, 0)
        last = lax.broadcasted_iota(jnp.int32, (per, 1), 0) == per - 1
        for b in range(nseq):
            gs = _s5_scan(gd[b * per:(b + 1) * per], pre_ref[...], pim_ref[...], True)
            dx_ref[pl.ds(b * per, per), :] = jnp.where(last, 0.0, pltpu.roll(gs, per - 1, 0))
        dx = dx_ref[...]
        dare_ref[...] = _colsum(dx * s_all)
        daim_ref[...] = _colsum(dx * pltpu.roll(s_all, S5_STATE, 1))
        du_ref[...] = _dot3(dy, a1_ref[...], 1, 1) + _dot3(dx, a3_ref[...], 1, 1)
        da1_ref[...] = _dot3(u, dy, 0, 0)
        da3_ref[...] = _dot3(u, dx, 0, 0)
        ex_finish(ex_parts)

    def spec(shape):
        return pl.BlockSpec((None,) + shape, lambda i: (i, 0, 0))
    sds = jax.ShapeDtypeStruct
    return pl.pallas_call(
        body, name="s5_bwd", grid=(g,),
        in_specs=[spec((nc, cw)), spec((nc, cw)), spec((nc, sw)), spec((cw, cw)), spec((sw, cw)), spec((cw, sw)),
                  spec((S5_LEVELS, sw)), spec((S5_LEVELS, sw))] + exkw['in_specs'],
        out_specs=[spec((nc, cw)), spec((cw, cw)), spec((sw, cw)), spec((cw, sw)), spec((1, sw)), spec((1, sw))]
        + exkw['out_specs'],
        out_shape=[sds((g, nc, cw), F32), sds((g, cw, cw), F32), sds((g, sw, cw), F32), sds((g, cw, sw), F32),
                   sds((g, 1, sw), F32), sds((g, 1, sw), F32)] + exkw['out_shape'],
        input_output_aliases=exkw['aliases'], scratch_shapes=[pltpu.VMEM((nc, sw), F32)] + exkw['scratch'],
        compiler_params=_params(("arbitrary",)))(dyg, ug, sst, a1, a2, a3, pre, pim, *exkw['ins'])


def _to_groups(u):
    t = u.shape[0]
    nc = t // S5_CHUNK
    return u.reshape(nc, S5_CHUNK, S5_GROUPS, S5_GROUP).transpose(2, 0, 1, 3).reshape(S5_GROUPS, nc, -1)


def _from_groups(yg):
    g, nc, _ = yg.shape
    return yg.reshape(g, nc, S5_CHUNK, S5_GROUP).transpose(1, 2, 0, 3).reshape(nc * S5_CHUNK, g * S5_GROUP)


def _sb_masks():
    row = lax.broadcasted_iota(jnp.int32, (SB_BLOCK, SB_BLOCK), 0)
    col = lax.broadcasted_iota(jnp.int32, (SB_BLOCK, SB_BLOCK), 1)
    return row, col


def _sb_weights(z, vis, later_of, after):
    sp = jnp.maximum(z, 0.0) + jnp.log1p(jnp.exp(-jnp.abs(z)))
    lk = jnp.where(vis, -sp, 0.0)
    rs = jnp.sum(lk, axis=1, keepdims=True)
    later = _dot_exact_rhs(lk, after, 1, 0) + later_of(rs)
    lb = z - sp
    w = jnp.where(vis, jnp.exp(lb + later), 0.0)
    return w, rs, lb


def _hidden_exchange(ex, n_in, n_out, grid):
    ni, no = (len(ex.ins), len(ex.out_shapes)) if ex else (0, 0)

    def split(refs):
        a, b = n_in + ni, n_in + ni + n_out
        end = len(refs) - (2 if ex else 0)
        return refs[:n_in], refs[a:b], (refs[n_in:a], refs[b:b + no], refs[end:]), refs[b + no:end]

    def at_step(which, fn, parts):
        if ex is not None:
            ids = [pl.program_id(d) == (0 if which == 'first' else g - 1) for d, g in enumerate(grid)]
            cond = ids[0]
            for c in ids[1:]:
                cond = jnp.logical_and(cond, c)
            pl.when(cond)(lambda: fn(parts[0], parts[1], *parts[2]))

    anyspec = pl.BlockSpec(memory_space=pl.ANY)
    kw = dict(in_specs=[anyspec] * ni, out_specs=[anyspec] * no, out_shape=list(ex.out_shapes) if ex else [],
              aliases={n_in + i: n_out + j for i, j in (ex.aliases().items() if ex else ())},
              scratch=ex.sems() if ex else [], ins=list(ex.ins) if ex else [])
    start = functools.partial(at_step, 'first', ex.start if ex else None)
    finish = functools.partial(at_step, 'last', ex.finish if ex else None)
    return split, start, finish, kw


SB_TILES = 2
SB_HEADS = 2 * SB_TILES
SB_WD = SB_TILES * LANES


def _stack_heads(x):
    return jnp.concatenate([x] * SB_HEADS, axis=0)


def _stack_masks():
    row = lax.broadcasted_iota(jnp.int32, (SB_HEADS * SB_BLOCK, SB_WD), 0)
    col = lax.broadcasted_iota(jnp.int32, (SB_HEADS * SB_BLOCK, SB_WD), 1)
    rowk = lax.broadcasted_iota(jnp.int32, (SB_HEADS * SB_BLOCK, SB_BLOCK), 0)
    colk = lax.broadcasted_iota(jnp.int32, (SB_HEADS * SB_BLOCK, SB_BLOCK), 1)
    return (col // 64) == (row // SB_BLOCK), colk < (rowk % SB_BLOCK)


def _own_lanes(r):
    head = lax.broadcasted_iota(jnp.int32, (SB_BLOCK, SB_WD), 1) // 64
    out = r[:SB_BLOCK]
    for h in range(1, SB_HEADS):
        out = jnp.where(head == h, r[h * SB_BLOCK:(h + 1) * SB_BLOCK], out)
    return out


def _sb_specs(lp, nseq):
    wd = SB_TILES * LANES
    off = [(S5_WIDTH + i * SB_WIDTH) // wd for i in range(3)]
    blk = (lp, wd)
    qkv = [pl.BlockSpec(blk, functools.partial(lambda b, h, o: (b, o + h), o=o)) for o in off]
    return (blk, qkv, pl.BlockSpec(blk, lambda b, h: (b, h)),
            pl.BlockSpec((None, None, 8, LANES), lambda b, h: (b, h, 0, 0)))


def attn_fwd(proj, nseq, lp, ex=None):
    nqb = lp // SB_BLOCK
    t = proj.shape[0]
    nstep = SB_WIDTH // (SB_TILES * LANES)
    split, ex_start, ex_finish, exkw = _hidden_exchange(ex, 3, 3, (nseq, nstep))

    def body(*refs):
        (q_ref, k_ref, v_ref), (o_ref, tot_ref, first_ref), ex_parts, _ = split(refs)
        ex_start(ex_parts)
        row, col = _sb_masks()
        after = (row > col).astype(BF16)
        lane8 = lax.broadcasted_iota(jnp.int32, (8, LANES), 1)
        hms, tris = _stack_masks()

        def qloop(qi, firsts):
            q0 = pl.multiple_of(qi * SB_BLOCK, SB_BLOCK)
            qm = jnp.where(hms, _stack_heads(q_ref[pl.ds(q0, SB_BLOCK), :] * 0.125), 0.0).astype(BF16)

            def alive(carry):
                return jnp.logical_and(carry[0] <= qi, carry[1] > 0)

            def kstep(carry):
                s, _, acc, lat = carry
                kj = qi - s
                k0 = pl.multiple_of(kj * SB_BLOCK, SB_BLOCK)
                k = k_ref[pl.ds(k0, SB_BLOCK), :].astype(BF16)
                v = v_ref[pl.ds(k0, SB_BLOCK), :].astype(BF16)
                vis = jnp.logical_or(kj < qi, tris)
                w, rs, _ = _sb_weights(_dot(qm, k, 1, 1), vis, functools.partial(lambda rs, lat: lat, lat=lat), after)
                acc = acc + _own_lanes(_dot(w.astype(BF16), v, 1, 0))
                lat = lat + rs
                return s + 1, (jnp.max(lat) > SB_DEAD).astype(jnp.int32), acc, lat

            res = lax.while_loop(alive, kstep, (jnp.int32(0), jnp.int32(1), jnp.zeros((SB_BLOCK, SB_WD), F32),
                                                jnp.zeros((SB_HEADS * SB_BLOCK, 1), F32)))
            o_ref[pl.ds(q0, SB_BLOCK), :] = res[2].astype(o_ref.dtype)
            tot_ref[pl.ds(q0, SB_BLOCK), :] = _own_lanes(jnp.broadcast_to(res[3], (SB_HEADS * SB_BLOCK, SB_WD)))
            return jnp.where(lane8 == qi, (qi - res[0] + 1).astype(F32), firsts)

        first_ref[...] = lax.fori_loop(0, nqb, qloop, jnp.zeros((8, LANES), F32))
        ex_finish(ex_parts)

    blk, qkv, out, vec = _sb_specs(lp, nseq)
    return pl.pallas_call(
        body, name="attn_fwd", grid=(nseq, nstep),
        in_specs=qkv + exkw['in_specs'], out_specs=[out, out, vec] + exkw['out_specs'],
        out_shape=[jax.ShapeDtypeStruct((t, SB_WIDTH), BF16), jax.ShapeDtypeStruct((t, SB_WIDTH), F32)]
        + [jax.ShapeDtypeStruct((nseq, nstep, 8, LANES), F32)] + exkw['out_shape'],
        input_output_aliases=exkw['aliases'], scratch_shapes=exkw['scratch'],
        compiler_params=_params(("arbitrary", "arbitrary")))(proj, proj, proj, *exkw['ins'])


def attn_bwd(proj, tot, first, do, nseq, lp, ex=None):
    nqb = lp // SB_BLOCK
    t = proj.shape[0]
    nstep = SB_WIDTH // (SB_TILES * LANES)
    split, ex_start, ex_finish, exkw = _hidden_exchange(ex, 6, 3, (nseq, nstep))

    def body(*refs):
        (q_ref, k_ref, v_ref, tot_ref, first_ref, do_ref), (dq_ref, dk_ref, dv_ref), ex_parts, _ = split(refs)
        ex_start(ex_parts)
        row, col = _sb_masks()
        after = (row > col).astype(BF16)
        before = (row < col).astype(BF16)
        lane8 = lax.broadcasted_iota(jnp.int32, (8, LANES), 1)
        firsts = first_ref[...]
        dk_ref[...] = jnp.zeros(dk_ref.shape, F32)
        dv_ref[...] = jnp.zeros(dv_ref.shape, F32)
        hms, tris = _stack_masks()

        def qloop(qi, _):
            first = jnp.max(jnp.where(lane8 == qi, firsts, 0.0)).astype(jnp.int32)
            first = jnp.clip(first, 0, qi)
            q0 = pl.multiple_of(qi * SB_BLOCK, SB_BLOCK)
            qm = jnp.where(hms, _stack_heads(q_ref[pl.ds(q0, SB_BLOCK), :] * 0.125), 0.0).astype(BF16)
            dom = jnp.where(hms, _stack_heads(do_ref[pl.ds(q0, SB_BLOCK), :]), 0.0).astype(BF16)
            tot = jnp.max(jnp.where(hms, _stack_heads(tot_ref[pl.ds(q0, SB_BLOCK), :]), -jnp.inf),
                          axis=1, keepdims=True)

            def kloop(kj, carry):
                dq, pre, gpre = carry
                k0 = pl.multiple_of(kj * SB_BLOCK, SB_BLOCK)
                k = k_ref[pl.ds(k0, SB_BLOCK), :].astype(BF16)
                v = v_ref[pl.ds(k0, SB_BLOCK), :].astype(BF16)
                vis = jnp.logical_or(kj < qi, tris)
                later_of = functools.partial(lambda rs, t_, p_: t_ - p_ - rs, t_=tot, p_=pre)
                w, rs, lb = _sb_weights(_dot(qm, k, 1, 1), vis, later_of, after)
                g = _dot(dom, v, 1, 1) * w
                dlk = _dot_exact_rhs(g, before, 1, 0) + gpre
                sig = jnp.exp(lb)
                dz = jnp.where(vis, g * (1.0 - sig) - dlk * sig, 0.0).astype(BF16)
                dk_ref[pl.ds(k0, SB_BLOCK), :] += _dot(dz, qm, 0, 0)
                dv_ref[pl.ds(k0, SB_BLOCK), :] += _dot(w.astype(BF16), dom, 0, 0)
                return dq + _own_lanes(_dot(dz, k, 1, 0)), pre + rs, gpre + jnp.sum(g, axis=1, keepdims=True)

            z1 = jnp.zeros((SB_HEADS * SB_BLOCK, 1), F32)
            res = lax.fori_loop(first, qi + 1, kloop, (jnp.zeros((SB_BLOCK, SB_WD), F32), z1, z1))
            dq_ref[pl.ds(q0, SB_BLOCK), :] = res[0] * 0.125
            return 0

        lax.fori_loop(0, nqb, qloop, 0)
        ex_finish(ex_parts)

    blk, qkv, out, vec = _sb_specs(lp, nseq)
    return pl.pallas_call(
        body, name="attn_bwd", grid=(nseq, nstep),
        in_specs=qkv + [out, vec, out] + exkw['in_specs'], out_specs=[out] * 3 + exkw['out_specs'],
        out_shape=[jax.ShapeDtypeStruct((t, SB_WIDTH), F32)] * 3 + exkw['out_shape'],
        input_output_aliases=exkw['aliases'], scratch_shapes=exkw['scratch'],
        compiler_params=_params(("arbitrary", "arbitrary")))(proj, proj, proj, tot, first, do, *exkw['ins'])


def _hg_loop(nch, step, init):
    assert nch % HG_UNROLL == 0

    def trip(i, carry):
        for u in range(HG_UNROLL):
            carry = step(i * HG_UNROLL + u, carry)
        return carry
    return lax.fori_loop(0, nch // HG_UNROLL, trip, init)


def _hg_gates(fc, lb):
    sg = jax.nn.sigmoid(fc)
    f = lb + (1.0 - lb) * sg
    return sg, f


def _dot_exact_lhs(m, x):
    hi, mid, lo = _split3(x)
    return _dot(m, hi, 1, 0) + (_dot(m, mid, 1, 0) + _dot(m, lo, 1, 0))


def hgrn_fwd(proj, lb, ng, nseq, lp, ex=None):
    nch = lp // HG_CHUNK
    t = proj.shape[0]
    c = HG_CHUNK

    split, ex_start, ex_finish, exkw = _hidden_exchange(ex, 6, 3, (HG_HEADS,))

    def body(*refs):
        (q_ref, f_ref, v_ref, g_ref, lb_ref, ng_ref), (o_ref, og_ref, st_ref), ex_parts, _ = split(refs)
        ex_start(ex_parts)
        lbv, ngv = lb_ref[...], ng_ref[...]

        rr = nseq * c
        seq_r = lax.broadcasted_iota(jnp.int32, (rr, 1), 0) // c
        bd_row = lax.broadcasted_iota(jnp.int32, (rr, rr), 0)
        bd_col = lax.broadcasted_iota(jnp.int32, (rr, rr), 1)
        causal = jnp.logical_and(bd_row // c == bd_col // c, bd_col <= bd_row)
        incl = causal.astype(BF16)

        def stacked(ref, ci):
            return jnp.concatenate([ref[pl.ds(pl.multiple_of(b * lp + ci * c, c), c), :] for b in range(nseq)], axis=0)

        def wide(x):
            return jnp.concatenate([jnp.where(seq_r == b, x, jnp.zeros_like(x)) for b in range(nseq)], axis=1)

        def step(ci, st):
            for b in range(nseq):
                st_ref[b, ci] = st[:, b * HG_DK:(b + 1) * HG_DK]
            _, f = _hg_gates(stacked(f_ref, ci), lbv)
            bcum = _dot_exact_lhs(incl, jnp.log(f))
            lasts = [bcum[b * c + c - 1:b * c + c, :] for b in range(nseq)]
            blast = jnp.concatenate([jnp.broadcast_to(x, (c, LANES)) for x in lasts], axis=0)
            kk = 1.0 - f
            vc = stacked(v_ref, ci).astype(BF16)
            qd = (stacked(q_ref, ci) * jnp.exp(bcum)).astype(BF16)
            kd = (kk * jnp.exp(-bcum)).astype(BF16)
            a = jnp.where(causal, _dot(qd, kd, 1, 1), 0.0)
            o = _dot(a.astype(BF16), vc, 1, 0) + _dot(wide(qd), st.astype(BF16), 1, 1)
            kend = (kk * jnp.exp(blast - bcum)).astype(BF16)
            st = st * jnp.exp(jnp.concatenate(lasts, axis=1)) + _dot(vc, wide(kend), 0, 0)
            r = lax.rsqrt(jnp.mean(o * o, axis=1, keepdims=True) + RMS_EPS)
            gc = stacked(g_ref, ci)
            og = (o * r * ngv * (gc * jax.nn.sigmoid(gc))).astype(og_ref.dtype)
            for b in range(nseq):
                rows = pl.ds(pl.multiple_of(b * lp + ci * c, c), c)
                o_ref[rows, :] = o[b * c:(b + 1) * c]
                og_ref[rows, :] = og[b * c:(b + 1) * c]
            return st

        _hg_loop(nch, step, jnp.zeros((HG_DK, nseq * HG_DK), F32))
        ex_finish(ex_parts)

    blk = (nseq * lp, LANES)
    h = HG_HEADS
    return pl.pallas_call(
        body, name="hgrn_fwd", grid=(h,),
        in_specs=[pl.BlockSpec(blk, lambda hh: (0, hh)),
                  pl.BlockSpec(blk, lambda hh: (0, h + hh)),
                  pl.BlockSpec(blk, lambda hh: (0, 2 * h + hh)),
                  pl.BlockSpec(blk, lambda hh: (0, 3 * h + hh)),
                  pl.BlockSpec((1, LANES), lambda hh: (0, hh)),
                  pl.BlockSpec((1, LANES), lambda hh: (0, hh))] + exkw['in_specs'],
        out_specs=[pl.BlockSpec(blk, lambda hh: (0, hh)),
                   pl.BlockSpec(blk, lambda hh: (0, hh)),
                   pl.BlockSpec((nseq, None, nch, HG_DK, HG_DK), lambda hh: (0, hh, 0, 0, 0))] + exkw['out_specs'],
        out_shape=[jax.ShapeDtypeStruct((t, D_MODEL), F32), jax.ShapeDtypeStruct((t, D_MODEL), BF16),
                   jax.ShapeDtypeStruct((nseq, h, nch, HG_DK, HG_DK), F32)] + exkw['out_shape'],
        input_output_aliases=exkw['aliases'], scratch_shapes=exkw['scratch'],
        compiler_params=_params(("arbitrary",)))(proj, proj, proj, proj, lb, ng, *exkw['ins'])


def hgrn_bwd(proj, o, dog, states, lb, ng, nseq, lp):
    nch = lp // HG_CHUNK
    t = proj.shape[0]
    c = HG_CHUNK
    rr = nseq * c

    def body(q_ref, f_ref, v_ref, g_ref, o_ref, dog_ref, st_ref, lb_ref, ng_ref,
             dq_ref, df_ref, dv_ref, dg_ref, dlb_ref, dng_ref):
        seq_r = lax.broadcasted_iota(jnp.int32, (rr, 1), 0) // c
        last = lax.broadcasted_iota(jnp.int32, (rr, 1), 0) % c == c - 1
        bd_row = lax.broadcasted_iota(jnp.int32, (rr, rr), 0)
        bd_col = lax.broadcasted_iota(jnp.int32, (rr, rr), 1)
        same = bd_row // c == bd_col // c
        causal = jnp.logical_and(same, bd_col <= bd_row)
        incl = causal.astype(BF16)
        from_here = jnp.logical_and(same, bd_col >= bd_row).astype(BF16)
        lbv, ngv = lb_ref[...], ng_ref[...]

        def stacked(ref, ci):
            return jnp.concatenate([ref[pl.ds(pl.multiple_of(b * lp + ci * c, c), c), :] for b in range(nseq)], axis=0)

        def wide(x):
            return jnp.concatenate([jnp.where(seq_r == b, x, jnp.zeros_like(x)) for b in range(nseq)], axis=1)

        def own(x):
            return functools.reduce(jnp.add, [jnp.where(seq_r == b, x[:, b * LANES:(b + 1) * LANES], 0.0)
                                              for b in range(nseq)])

        def per_seq_rows(vals):
            return jnp.concatenate([jnp.broadcast_to(x, (c, LANES)) for x in vals], axis=0)

        def step(s, carry):
            dst, dlb, dng = carry
            ci = nch - 1 - s
            oc, gc, dogc = stacked(o_ref, ci), stacked(g_ref, ci), stacked(dog_ref, ci)
            r = lax.rsqrt(jnp.mean(oc * oc, axis=1, keepdims=True) + RMS_EPS)
            sgg = jax.nn.sigmoid(gc)
            dgc = dogc * (oc * r * ngv) * (sgg * (1.0 + gc * (1.0 - sgg)))
            don = dogc * (gc * sgg)
            dng = dng + _colsum(don * oc * r)
            tt = don * ngv
            do = r * (tt - oc * (r * r) * jnp.mean(tt * oc, axis=1, keepdims=True))
            st = jnp.concatenate([st_ref[b, ci] for b in range(nseq)], axis=1)
            sg, f = _hg_gates(stacked(f_ref, ci), lbv)
            bcum = _dot_exact_lhs(incl, jnp.log(f))
            lasts = [bcum[b * c + c - 1:b * c + c, :] for b in range(nseq)]
            blast = per_seq_rows(lasts)
            kk = 1.0 - f
            qc, vf = stacked(q_ref, ci), stacked(v_ref, ci)
            pdec = jnp.exp(bcum)
            ndec = jnp.exp(-bcum)
            edec = jnp.exp(blast - bcum)
            eb = jnp.exp(jnp.concatenate(lasts, axis=1))
            qd_f, kd_f, kend_f = qc * pdec, kk * ndec, kk * edec
            qd, kd, kend = qd_f.astype(BF16), kd_f.astype(BF16), kend_f.astype(BF16)
            vc, dob, stb, dstb = vf.astype(BF16), do.astype(BF16), st.astype(BF16), dst.astype(BF16)
            a = jnp.where(causal, _dot(qd, kd, 1, 1), 0.0).astype(BF16)
            da = jnp.where(causal, _dot(dob, vc, 1, 1), 0.0).astype(BF16)
            dv = _dot(a, dob, 0, 0) + _dot(wide(kend), dstb, 1, 1)
            dqd = _dot(da, kd, 1, 0) + own(_dot(dob, stb, 1, 0))
            dkd = _dot(da, qd, 0, 0)
            dkend = own(_dot(vc, dstb, 1, 0))
            sts = _colsum(dst * st)
            dkk = dkend * kend_f
            dblast = per_seq_rows([_colsum(jnp.where(seq_r == b, dkk, 0.0))
                                   + eb[:, b * LANES:(b + 1) * LANES] * sts[:, b * LANES:(b + 1) * LANES]
                                   for b in range(nseq)])
            dbcum = dqd * qd_f - dkd * kd_f - dkk
            dbcum = dbcum + jnp.where(last, dblast, 0.0)
            dlf = _dot_exact_lhs(from_here, dbcum)
            df = dlf / f - (dkd * ndec + dkend * edec)
            dfp = df * (1.0 - lbv) * sg * (1.0 - sg)
            dqc = dqd * pdec
            for b in range(nseq):
                rows = pl.ds(pl.multiple_of(b * lp + ci * c, c), c)
                blk_rows = slice(b * c, (b + 1) * c)
                dg_ref[rows, :], dv_ref[rows, :] = dgc[blk_rows], dv[blk_rows]
                dq_ref[rows, :], df_ref[rows, :] = dqc[blk_rows], dfp[blk_rows]
            dlb = dlb + _colsum(df * (1.0 - sg))
            dst = dst * eb + _dot(dob, wide(qd), 0, 0)
            return dst, dlb, dng

        z = jnp.zeros((1, LANES), F32)
        _, dlb, dng = _hg_loop(nch, step, (jnp.zeros((HG_DK, nseq * HG_DK), F32), z, z))
        dlb_ref[...] = dlb
        dng_ref[...] = dng

    blk = (nseq * lp, LANES)
    h = HG_HEADS
    vec = pl.BlockSpec((1, LANES), lambda hh: (0, hh))
    col = pl.BlockSpec(blk, lambda hh: (0, hh))
    return pl.pallas_call(
        body, name="hgrn_bwd", grid=(h,),
        in_specs=[col,
                  pl.BlockSpec(blk, lambda hh: (0, h + hh)),
                  pl.BlockSpec(blk, lambda hh: (0, 2 * h + hh)),
                  pl.BlockSpec(blk, lambda hh: (0, 3 * h + hh)),
                  col, col,
                  pl.BlockSpec((nseq, None, nch, HG_DK, HG_DK), lambda hh: (0, hh, 0, 0, 0)),
                  vec, vec],
        out_specs=[col] * 4 + [vec, vec],
        out_shape=[jax.ShapeDtypeStruct((t, D_MODEL), F32)] * 4 + [jax.ShapeDtypeStruct((1, D_MODEL), F32)] * 2,
        compiler_params=pltpu.CompilerParams(dimension_semantics=("parallel",), vmem_limit_bytes=HGRN_BWD_VMEM))(
            proj, proj, proj, proj, o, dog, states, lb, ng)


def _lower_bound(gamma):
    p = jax.nn.softmax(gamma, axis=0)
    return (jnp.cumsum(p, axis=0) - p[0])[1][None, :]


def local_step(x, tgt, w, hooks=None):
    nseq, seq, _ = x.shape
    lp = -(-(N_META + seq) // SB_BLOCK) * SB_BLOCK
    t = nseq * lp
    pad = lp - N_META - seq
    h0 = jnp.concatenate([jnp.broadcast_to(w['meta'][None], (nseq, N_META, D_MODEL)), x,
                          jnp.zeros((nseq, pad, D_MODEL), F32)], axis=1).reshape(t, D_MODEL)
    tgt_p = jnp.pad(tgt, ((0, 0), (N_META, pad), (0, 0))).reshape(t, D_MODEL)
    row = lambda v: v.reshape(1, -1)
    g = {}

    proj = matmul("in_ab", h0, w['w_in_ab'], 'nn')
    att = attn_fwd(proj, nseq, lp, ex=hooks.gather_late if hooks else None)
    b_out, sb_tot, sb_first = att[:3]
    if hooks:
        w = {**w, **hooks.late_weights(att[3:])}
    s5_names = ['s5_lam_re', 's5_lam_im', 's5_log_dt', 's5_b_re', 's5_b_im', 's5_c_re', 's5_c_im', 's5_d']
    mats, mats_vjp = jax.vjp(s5_matrices, *[w[n][0] for n in s5_names])
    ug = _to_groups(proj[:, :S5_WIDTH])
    s5_pows = s5_scan_powers(*[w[n][0] for n in s5_names[:3]])
    yg, sst = s5_fwd(ug, *mats[:3], *s5_pows, nseq)
    ys5 = _from_groups(yg)
    y = gelu_fwd("gelu", ys5)
    gp = matmul("glu", y, w['s5_w_glu'], 'nn', bias=w['s5_b_glu'])
    a_out = glu_gate("glu_gate", y, gp)
    cat = jnp.concatenate([a_out, b_out], axis=1)
    mix0 = matmul("out_ab", cat, w['w_out_ab'], 'nn')
    h1, xh1, rs1 = ln_fwd("ln_mix0", h0, mix0, row(w['ln_mix_g'][0]), row(w['ln_mix_b'][0]))
    hid0 = matmul("up0", h1, w['mlp_w_up'][0], 'nn', bias=row(w['mlp_b_up'][0]), act='relu2', out_dtype=BF16)
    dn0 = matmul("down0", hid0, w['mlp_w_down'][0], 'nn', bias=row(w['mlp_b_down'][0]))
    h2, xh2, rs2 = ln_fwd("ln_mlp0", h1, dn0, row(w['ln_mlp_g'][0]), row(w['ln_mlp_b'][0]))

    projc = matmul("in_c", h2, w['w_in_c'], 'nn')
    lb, lb_vjp = jax.vjp(_lower_bound, w['hgrn_gamma'])
    ng = w['hgrn_norm_g']
    res = hgrn_fwd(projc, lb, ng, nseq, lp, ex=hooks.gather_last if hooks else None)
    o, og, states = res[:3]
    if hooks:
        w = {**w, **hooks.last_weights(res[3:])}
    mix1 = matmul("out_c", og, w['w_out_c'], 'nn')
    h3, xh3, rs3 = ln_fwd("ln_mix1", h2, mix1, row(w['ln_mix_g'][1]), row(w['ln_mix_b'][1]))
    hid1 = matmul("up1", h3, w['mlp_w_up'][1], 'nn', bias=row(w['mlp_b_up'][1]), act='relu2', out_dtype=BF16)
    dn1 = matmul("down1", hid1, w['mlp_w_down'][1], 'nn', bias=row(w['mlp_b_down'][1]))
    h4, xh4, rs4 = ln_fwd("ln_mlp1", h3, dn1, row(w['ln_mlp_g'][1]), row(w['ln_mlp_b'][1]))
    dy, loss = loss_head("loss", h4, tgt_p, lp, seq)

    def mlp_bwd(l, dh_out, xh_out, rs_out, hid, h_in):
        dpre, dg_, db_, dbd = ln_bwd(f"ln_mlp{l}_b", dh_out, xh_out, rs_out, row(w['ln_mlp_g'][l]))
        dpu = matmul(f"down{l}_dx", dpre, w['mlp_w_down'][l], 'nt', relu2_of=hid, out_dtype=BF16)
        dwd = matmul(f"down{l}_dw", hid, dpre, 'tn')
        dh_in = matmul(f"up{l}_dx", dpu, w['mlp_w_up'][l], 'nt', add=dpre, add_scale=ALPHA)
        dwu, dbu = matmul(f"up{l}_dw", h_in, dpu, 'tn', out_slots=N_CHIPS, colsum=True)
        return dh_in, dict(ln_mlp_g=dg_, ln_mlp_b=db_, mlp_b_down=dbd, mlp_b_up=dbu, mlp_w_up=dwu, mlp_w_down=dwd)

    dh3, gm1 = mlp_bwd(1, dy, xh4, rs4, hid1, h3)
    dpre, dg1, db1, _ = ln_bwd("ln_mix1_b", dh3, xh3, rs3, row(w['ln_mix_g'][1]))
    g['w_out_c'] = matmul("out_c_dw", og, dpre, 'tn')
    dog = matmul("out_c_dx", dpre, w['w_out_c'], 'nt')
    dq, df, di, dgg, dlb, dng = hgrn_bwd(projc, o, dog, states, lb, ng, nseq, lp)
    dprojc = jnp.concatenate([dq, df, di, dgg], axis=1)
    dh2 = matmul("in_c_dx", dprojc, w['w_in_c'], 'nt', add=dpre, add_scale=ALPHA)
    g['w_in_c'] = matmul("in_c_dw", h2, dprojc, 'tn', out_slots=N_CHIPS)
    g['hgrn_gamma'] = lb_vjp(dlb)[0]
    g['hgrn_norm_g'] = dng

    dh1, gm0 = mlp_bwd(0, dh2, xh2, rs2, hid0, h1)
    dpre, dg0, db0, _ = ln_bwd("ln_mix0_b", dh1, xh1, rs1, row(w['ln_mix_g'][0]))
    g['w_out_ab'] = matmul("out_ab_dw", cat, dpre, 'tn')
    dcat = matmul("out_ab_dx", dpre, w['w_out_ab'], 'nt')
    dgp, da_s, dbglu = glu_gate_bwd("glu_gate_b", dcat[:, :S5_WIDTH], y, gp)
    g['s5_w_glu'] = matmul("glu_dw", y, dgp, 'tn')
    g['s5_b_glu'] = dbglu
    dy5 = matmul("glu_dx", dgp, w['s5_w_glu'], 'nt', add=da_s)
    dys5 = gelu_bwd("gelu_b", dy5, ys5)
    for n in ('mlp_w_up', 'mlp_w_down'):
        g[n] = [gm0[n], gm1[n]]
    g['ln_mix_g'] = jnp.concatenate([dg0, dg1], axis=0)
    g['ln_mix_b'] = jnp.concatenate([db0, db1], axis=0)
    for n in ('ln_mlp_g', 'ln_mlp_b', 'mlp_b_down', 'mlp_b_up'):
        g[n] = jnp.concatenate([gm0[n], gm1[n]], axis=0)
    res = s5_bwd(_to_groups(dys5), ug, sst, *mats[:3], *s5_pows, nseq, ex=hooks.fold_early(g) if hooks else None)
    dug, da1, da2, da3, dare, daim = res[:6]
    for n, v in zip(s5_names, mats_vjp((da1, da2, da3, dare, daim))):
        g[n] = v[None]
    ex = hooks.scatter_early(g, loss, res[6:]) if hooks else None
    res = attn_bwd(proj, sb_tot, sb_first, dcat[:, S5_WIDTH:], nseq, lp, ex=ex)
    dqa, dka, dva = res[:3]
    if hooks:
        hooks.scattered(res[3:])
    dproj = jnp.concatenate([_from_groups(dug), dqa, dka, dva], axis=1)
    dh0 = matmul("in_ab_dx", dproj, w['w_in_ab'], 'nt', add=dpre, add_scale=ALPHA)
    g['w_in_ab'] = matmul("in_ab_dw", h0, dproj, 'tn', out_slots=N_CHIPS)

    dh0 = dh0.reshape(nseq, lp, D_MODEL)
    grad_x = dh0[:, N_META:N_META + seq]
    g['meta'] = jnp.sum(dh0[:, :N_META], axis=0)
    return loss, grad_x, g


class Exchange:
    def __init__(self, ins, out_shapes, n_remote, build, in_place):
        self.ins, self.out_shapes, self.n_remote, self.build, self.in_place = ins, out_shapes, n_remote, build, in_place

    def sems(self):
        return [pltpu.SemaphoreType.DMA((self.n_remote,)), pltpu.SemaphoreType.DMA((self.n_remote,))]

    def aliases(self):
        return self.in_place if isinstance(self.in_place, dict) else \
            {i: i for i in range(len(self.ins) if self.in_place else 0)}

    def beside(self, other):
        na, nao = len(self.ins), len(self.out_shapes)

        def build(in_refs, out_refs, me):
            pa = self.build(in_refs[:na], out_refs[:nao], me)
            pb = other.build(in_refs[na:], out_refs[nao:], me)
            n = max(len(pa), len(pb))
            return [(pa[i] if i < len(pa) else []) + (pb[i] if i < len(pb) else []) for i in range(n)]
        al = dict(self.aliases())
        al.update({na + i: nao + j for i, j in other.aliases().items()})
        return Exchange(list(self.ins) + list(other.ins), list(self.out_shapes) + list(other.out_shapes),
                        self.n_remote + other.n_remote, build, al)

    def phases(self, in_refs, out_refs, ssem, rsem):
        me = (lax.axis_index("x"), lax.axis_index("y"), lax.axis_index("c"))
        out, k = [], 0
        for phase in self.build(in_refs, out_refs, me):
            out.append(functools.partial(
                lambda phase, k: [pltpu.make_async_remote_copy(src_ref=s, dst_ref=d, send_sem=ssem.at[k + i],
                                                               recv_sem=rsem.at[k + i], device_id=p,
                                                               device_id_type=MESH)
                                  for i, (s, d, p) in enumerate(phase)], phase, k))
            k += len(phase)
        return out

    def start(self, in_refs, out_refs, ssem, rsem):
        for cp in self.phases(in_refs, out_refs, ssem, rsem)[0]():
            cp.start()

    def finish(self, in_refs, out_refs, ssem, rsem):
        phases = self.phases(in_refs, out_refs, ssem, rsem)
        for cp in phases[0]():
            cp.wait()
        for make in phases[1:]:
            copies = make()
            for cp in copies:
                cp.start()
            for cp in copies:
                cp.wait()


def _exchange(name, ex):
    ni, no = len(ex.ins), len(ex.out_shapes)

    def body(*refs):
        in_refs, out_refs, sems = refs[:ni], refs[ni:ni + no], refs[ni + no:]
        ex.start(in_refs, out_refs, *sems)
        ex.finish(in_refs, out_refs, *sems)

    anyspec = pl.BlockSpec(memory_space=pl.ANY)
    return pl.pallas_call(
        body, name=name, in_specs=[anyspec] * ni, out_specs=[anyspec] * no, out_shape=ex.out_shapes,
        input_output_aliases=ex.aliases(), scratch_shapes=ex.sems())(*ex.ins)


def _chip_peers(me):
    x, y, c = me
    return [(x ^ (m >> 1), y ^ (m & 1), c) for m in (1, 2, 3)]


def _half(ref, c, axis):
    h = ref.shape[axis] // 2
    idx = [slice(None)] * axis + [pl.ds(c * h, h)]
    return ref.at[tuple(idx)]


def _like(arrs):
    return [jax.ShapeDtypeStruct(a.shape, a.dtype) for a in arrs]


def gather_over_chips(bufs):
    def build(in_refs, out_refs, me):
        x, y, c = me
        j = 2 * x + y
        sib = (x, y, 1 - c)
        ici = [(_half(o.at[j], c, 0), _half(o.at[j], c, 0), p) for o in out_refs for p in _chip_peers(me)]
        d2d = [(_half(o.at[2 * p[0] + p[1]], c, 0), _half(o.at[2 * p[0] + p[1]], c, 0), sib)
               for o in out_refs for p in _chip_peers(me)]
        return [ici, d2d]
    return Exchange(bufs, _like(bufs), 6 * len(bufs), build, True)


def fold_cores(fulls):
    def build(in_refs, out_refs, me):
        x, y, c = me
        return [[(_half(s, 1 - c, 1), o, (x, y, 1 - c)) for s, o in zip(in_refs, out_refs)]]
    shapes = [jax.ShapeDtypeStruct((f.shape[0], f.shape[1] // 2, f.shape[2]), f.dtype) for f in fulls]
    return Exchange(fulls, shapes, len(fulls), build, False)


def scatter_over_chips(parts):
    def build(in_refs, out_refs, me):
        j = 2 * me[0] + me[1]
        return [[(s.at[2 * p[0] + p[1]], o.at[j], p) for s, o in zip(in_refs, out_refs) for p in _chip_peers(me)]]
    return Exchange(parts, _like(parts), 3 * len(parts), build, False)


def join_cores(bufs):
    def build(in_refs, out_refs, me):
        x, y, c = me
        return [[(o.at[c], o.at[c], (x, y, 1 - c)) for o in out_refs]]
    return Exchange(bufs, _like(bufs), len(bufs), build, True)


def gather_over_devices(buf):
    def build(in_refs, out_refs, me):
        x, y, c = me
        i = 4 * x + 2 * y + c
        out = out_refs[0]
        return [[(out.at[i], out.at[i], (x ^ (m >> 2), y ^ ((m >> 1) & 1), c ^ (m & 1))) for m in range(1, N_DEV)]]
    return Exchange([buf], _like([buf]), N_DEV - 1, build, True)


def _slot_call(name, body, scalars, ins, in_maps, out_shape, out_map, blk, grid):
    gs = pltpu.PrefetchScalarGridSpec(
        num_scalar_prefetch=1, grid=grid,
        in_specs=[pl.BlockSpec((None,) + blk, m) for m in in_maps],
        out_specs=pl.BlockSpec((None,) + blk, out_map))
    return pl.pallas_call(body, name=name, grid_spec=gs, out_shape=out_shape,
                          compiler_params=_params(("arbitrary",) * len(grid)))(scalars, *ins)


def cast_into_slot(name, w2d, chip, dtype):
    r, wd = w2d.shape
    tm = _pick(r, (512, 256, 128, 64, 32, 16))

    def body(s_ref, w_ref, o_ref):
        o_ref[...] = w_ref[...].astype(o_ref.dtype)

    gs = pltpu.PrefetchScalarGridSpec(
        num_scalar_prefetch=1, grid=(r // tm,),
        in_specs=[pl.BlockSpec((tm, wd), lambda i, s: (i, 0))],
        out_specs=pl.BlockSpec((None, tm, wd), lambda i, s: (s[0], i, 0)))
    return pl.pallas_call(body, name=name, grid_spec=gs,
                          out_shape=jax.ShapeDtypeStruct((N_CHIPS, r, wd), dtype),
                          compiler_params=_params(("arbitrary",)))(chip.reshape(1), w2d)


def sum_cores(name, full, theirs, core):
    s, r, wd = full.shape
    h = r // 2
    tm = _pick(h, (512, 256, 128, 64, 32, 16))
    nt = h // tm

    def body(s_ref, a_ref, b_ref, o_ref):
        o_ref[...] = (a_ref[...] + b_ref[...]).astype(o_ref.dtype)

    return _slot_call(name, body, core.reshape(1), [full, theirs],
                      [lambda k, i, s_: (k, s_[0] * nt + i, 0), lambda k, i, s_: (k, i, 0)],
                      jax.ShapeDtypeStruct((s, h, wd), BF16), lambda k, i, s_: (k, i, 0), (tm, wd), (s, nt))


def sum_chips(name, own, recv, chip, core):
    s, r, wd = own.shape
    tm = _pick(r, (512, 256, 128, 64, 32, 16))

    def body(s_ref, a_ref, b1_ref, b2_ref, b3_ref, o_ref):
        acc = a_ref[...].astype(F32)
        for ref in (b1_ref, b2_ref, b3_ref):
            acc = acc + ref[...].astype(F32)
        o_ref[...] = acc

    maps = [lambda i, s_: (s_[0], i, 0)]
    maps += [functools.partial(lambda i, s_, m: (s_[0] ^ m, i, 0), m=m) for m in (1, 2, 3)]
    return _slot_call(name, body, jnp.stack([chip, core]), [own, recv, recv, recv], maps,
                      jax.ShapeDtypeStruct((2, r, wd), F32), lambda i, s_: (s_[1], i, 0), (tm, wd), (r // tm,))


def sum_slots(name, arr):
    s, r, wd = arr.shape
    tm = _pick(r, (512, 256, 128, 64, 32, 16, 8))

    def body(*refs):
        acc = refs[0][...].astype(F32)
        for ref in refs[1:s]:
            acc = acc + ref[...].astype(F32)
        refs[s][...] = acc

    specs = [pl.BlockSpec((None, tm, wd), functools.partial(lambda i, k: (k, i, 0), k=k)) for k in range(s)]
    return pl.pallas_call(body, name=name, grid=(r // tm,), in_specs=specs,
                          out_specs=pl.BlockSpec((tm, wd), lambda i: (i, 0)),
                          out_shape=jax.ShapeDtypeStruct((r, wd), F32),
                          compiler_params=_params(("parallel",)))(*([arr] * s))


def _pack(arrs):
    flat = jnp.concatenate([a.reshape(-1) for a in arrs])
    n = flat.shape[0]
    padded = -(-n // (512 * LANES)) * (512 * LANES)
    return jnp.pad(flat, (0, padded - n)).reshape(-1, LANES)


def _unpack(packed, shapes):
    flat = packed.reshape(-1)
    out, pos = [], 0
    for s in shapes:
        n = math.prod(s)
        out.append(flat[pos:pos + n].reshape(s))
        pos += n
    return out


def _as2d(a):
    return a.reshape(-1, a.shape[-1])


def kernel(x, meta, w_in_ab, s5_lam_re, s5_lam_im, s5_log_dt, s5_b_re, s5_b_im, s5_c_re, s5_c_im, s5_d, s5_w_glu, s5_b_glu, w_out_ab, w_in_c, hgrn_gamma, hgrn_norm_g, w_out_c, ln_mix_g, ln_mix_b, mlp_w_up, mlp_b_up, mlp_w_down, mlp_b_down, ln_mlp_g, ln_mlp_b, loss_target, m_meta, m_w_in_ab, m_s5_lam_re, m_s5_lam_im, m_s5_log_dt, m_s5_b_re, m_s5_b_im, m_s5_c_re, m_s5_c_im, m_s5_d, m_s5_w_glu, m_s5_b_glu, m_w_out_ab, m_w_in_c, m_hgrn_gamma, m_hgrn_norm_g, m_w_out_c, m_ln_mix_g, m_ln_mix_b, m_mlp_w_up, m_mlp_b_up, m_mlp_w_down, m_mlp_b_down, m_ln_mlp_g, m_ln_mlp_b, v_meta, v_w_in_ab, v_s5_lam_re, v_s5_lam_im, v_s5_log_dt, v_s5_b_re, v_s5_b_im, v_s5_c_re, v_s5_c_im, v_s5_d, v_s5_w_glu, v_s5_b_glu, v_w_out_ab, v_w_in_c, v_hgrn_gamma, v_hgrn_norm_g, v_w_out_c, v_ln_mix_g, v_ln_mix_b, v_mlp_w_up, v_mlp_b_up, v_mlp_w_down, v_mlp_b_down, v_ln_mlp_g, v_ln_mlp_b):
    given = dict(locals())
    wts = {n: given[n] for n in WEIGHTS}
    ms = {n: given['m_' + n] for n in WEIGHTS}
    vs = {n: given['v_' + n] for n in WEIGHTS}
    chip = 2 * lax.axis_index("x") + lax.axis_index("y")

    core = lax.axis_index("c")
    parts = [(n, l) for n in BIG for l in (range(DEPTH) if n.startswith('mlp_') else [0])]
    tags = [f"{n}{l}" for n, l in parts]
    shards = {p: cast_into_slot("cast_" + t, wts[p[0]][p[1]], chip, BF16) for t, p in zip(tags, parts)}
    small_sh = jnp.concatenate([meta, hgrn_norm_g, jnp.zeros((15, meta.shape[1]), F32)], axis=0)
    first, late = parts[:1], parts[1:]
    w_in_ab_all, sm = _exchange("gather_first", gather_over_chips(
        [shards[p] for p in first] + [cast_into_slot("cast_small", small_sh, chip, F32)]))
    w = {n: wts[n] for n in SMALL}
    w['meta'] = sm[:, :N_META].transpose(1, 0, 2).reshape(N_META, D_MODEL)
    w['hgrn_norm_g'] = sm[:, N_META:N_META + 1].transpose(1, 0, 2).reshape(1, D_MODEL)
    w['w_in_ab'] = w_in_ab_all

    def slot_major(v):
        return v if v.ndim == 3 else v.reshape(N_CHIPS, -1, v.shape[-1])

    def whole_grads(ps, g):
        return [slot_major(g[n][l] if n.startswith('mlp_') else g[n]) for n, l in ps]

    def chip_sums_of(ps, gfull, theirs):
        return [sum_cores(f"sum_cores_{n}{l}", a, b, core) for (n, l), a, b in zip(ps, gfull, theirs)]

    def all_devices(arrs):
        packed = _pack(arrs)
        return gather_over_devices(lax.dynamic_update_slice(
            jnp.zeros((N_DEV,) + packed.shape, F32), packed[None], (2 * chip + core, 0, 0)))

    early_small = [n for n in SMALL if n != 'meta']
    last = [('w_out_c', 0), ('mlp_w_up', 1), ('mlp_w_down', 1)]
    mid = [p for p in late if p not in last]

    class Hooks:
        gather_late = gather_over_chips([shards[p] for p in mid])
        gather_last = gather_over_chips([shards[p] for p in last])

        @staticmethod
        def late_weights(filled):
            fw = Hooks.fw = dict(zip(mid, filled))
            return {'s5_w_glu': fw['s5_w_glu', 0].reshape(S5_WIDTH, S5_WIDTH),
                    'w_out_ab': fw['w_out_ab', 0].reshape(D_MODEL, D_MODEL),
                    'w_in_c': fw['w_in_c', 0],
                    'mlp_w_up': [fw['mlp_w_up', 0]],
                    'mlp_w_down': [fw['mlp_w_down', 0].reshape(D_FF, D_MODEL)]}

        @staticmethod
        def last_weights(filled):
            fw = dict(zip(last, filled))
            return {'w_out_c': fw['w_out_c', 0].reshape(D_MODEL, D_MODEL),
                    'mlp_w_up': [Hooks.fw['mlp_w_up', 0], fw['mlp_w_up', 1]],
                    'mlp_w_down': [Hooks.fw['mlp_w_down', 0].reshape(D_FF, D_MODEL),
                                   fw['mlp_w_down', 1].reshape(D_FF, D_MODEL)]}

        @staticmethod
        def fold_early(g):
            Hooks.whole = whole_grads(late, g)
            return fold_cores(Hooks.whole)

        @staticmethod
        def scatter_early(g, loss, theirs):
            Hooks.sums = chip_sums_of(late, Hooks.whole, theirs)
            Hooks.small_shapes = [(LANES,)] + [g[n].shape for n in early_small]
            return scatter_over_chips(Hooks.sums).beside(
                all_devices([loss.reshape(-1)] + [g[n] for n in early_small]))

        @staticmethod
        def scattered(outs):
            Hooks.recv, Hooks.small = list(outs[:-1]), outs[-1]

    loss, grad_x, g = local_step(x, loss_target, w, Hooks)

    whole = whole_grads(first, g)
    sums = chip_sums_of(first, whole, _exchange("fold_last", fold_cores(whole)))
    *recv, meta_slots = _exchange("scatter_last", scatter_over_chips(sums).beside(all_devices([g['meta']])))
    reduced = _exchange("join_grads", join_cores(
        [sum_chips(f"sum_chips_{n}{l}", a, b, chip, core)
         for (n, l), a, b in zip(first + late, sums + Hooks.sums, list(recv) + Hooks.recv)]))
    reduced = dict(zip(first + late, [_as2d(r) for r in reduced]))
    red = _unpack(sum_slots("sum_small", Hooks.small), Hooks.small_shapes)
    loss_out = red[0][0]
    gs = dict(zip(early_small, red[1:]))
    gs['meta'] = _unpack(sum_slots("sum_meta", meta_slots), [g['meta'].shape])[0]
    gs['meta'] = lax.dynamic_slice_in_dim(gs['meta'], chip * meta.shape[1], meta.shape[1], axis=1)
    gs['hgrn_norm_g'] = lax.dynamic_slice_in_dim(gs['hgrn_norm_g'].reshape(1, -1), chip * meta.shape[1],
                                                 meta.shape[1], axis=1)

    grads, deltas, new_m, new_v = {}, {}, {}, {}
    for n in BIG:
        r = jnp.concatenate([reduced[n, l] for l in range(wts[n].shape[0])], axis=0)
        res = adamw("adamw_" + n, _as2d(wts[n]), [r], _as2d(ms[n]), _as2d(vs[n]))
        grads[n], deltas[n], new_m[n], new_v[n] = [r.reshape(wts[n].shape) for r in res]
    shapes = [wts[n].shape for n in SMALL]
    res = adamw("adamw_small", _pack([wts[n] for n in SMALL]), [_pack([gs[n] for n in SMALL])],
                _pack([ms[n] for n in SMALL]), _pack([vs[n] for n in SMALL]))
    for d, r in zip((grads, deltas, new_m, new_v), res):
        d.update(zip(SMALL, _unpack(r, shapes)))
    return (loss_out, grad_x, *[grads[n] for n in WEIGHTS], *[deltas[n] for n in WEIGHTS],
            *[new_m[n] for n in WEIGHTS], *[new_v[n] for n in WEIGHTS])
```

```python
import functools
import math

import jax
import jax.numpy as jnp
from jax import lax
from jax.experimental import pallas as pl
from jax.experimental.pallas import tpu as pltpu

F32 = jnp.float32
BF16 = jnp.bfloat16

D_MODEL = 1024
N_META = 16
DEPTH = 2
S5_WIDTH = 512
S5_GROUP = 16
S5_GROUPS = 32
S5_STATE = 64
S5_CHUNK = 16
SB_WIDTH = 512
SB_BLOCK = 128
SB_DEAD = -110.0
HG_HEADS = 8
HG_DK = 128
HG_CHUNK = 64
HG_UNROLL = 2
D_FF = 4096
ALPHA = (2.0 * DEPTH) ** 0.25
LN_EPS = 1e-5
RMS_EPS = 1e-6
ADAM_LR = 0.001
ADAM_B1 = 0.9
ADAM_B2 = 0.999
ADAM_EPS = 1e-08
ADAM_WD = 0.01
ADAM_STEP = 10
N_CHIPS = 4
N_DEV = 8
LANES = 128
MESH = pl.DeviceIdType.MESH
VMEM_LIMIT = 56 * 1024 * 1024
WHOLE_K_BYTES = 42 * 1024 * 1024
HGRN_BWD_VMEM = 60 * 1024 * 1024

WEIGHTS = ['meta', 'w_in_ab', 's5_lam_re', 's5_lam_im', 's5_log_dt', 's5_b_re', 's5_b_im', 's5_c_re', 's5_c_im',
           's5_d', 's5_w_glu', 's5_b_glu', 'w_out_ab', 'w_in_c', 'hgrn_gamma', 'hgrn_norm_g', 'w_out_c',
           'ln_mix_g', 'ln_mix_b', 'mlp_w_up', 'mlp_b_up', 'mlp_w_down', 'mlp_b_down', 'ln_mlp_g', 'ln_mlp_b']
BIG = ['w_in_ab', 's5_w_glu', 'w_out_ab', 'w_in_c', 'w_out_c', 'mlp_w_up', 'mlp_w_down']
SMALL = [n for n in WEIGHTS if n not in BIG]


def _params(sem=None):
    return pltpu.CompilerParams(dimension_semantics=sem, vmem_limit_bytes=VMEM_LIMIT)


def _pick(n, cands):
    for c in cands:
        if n % c == 0:
            return c
    return n


def _dot(a, b, ca, cb):
    return lax.dot_general(a, b, (((ca,), (cb,)), ((), ())), preferred_element_type=F32)


def _split3(x):
    hi = x.astype(BF16)
    r = x - hi.astype(F32)
    mid = r.astype(BF16)
    lo = (r - mid.astype(F32)).astype(BF16)
    return hi, mid, lo


def _dot_exact_rhs(x, m, cx, cm):
    hi = x.astype(BF16)
    lo = (x - hi.astype(F32)).astype(BF16)
    return _dot(hi, m, cx, cm) + _dot(lo, m, cx, cm)


def _dot3(a, b, ca, cb):
    ah = a.astype(BF16)
    al = (a - ah.astype(F32)).astype(BF16)
    bh = b.astype(BF16)
    bl = (b - bh.astype(F32)).astype(BF16)
    return _dot(ah, bh, ca, cb) + (_dot(ah, bl, ca, cb) + _dot(al, bh, ca, cb))


def rowwise(name, fn, row_ins, bc_ins, out_widths, sum_widths=(), out_dtypes=None, tm=None):
    t = row_ins[0].shape[0]
    if tm is None:
        wmax = max([a.shape[1] for a in row_ins] + list(out_widths))
        tm = _pick(t, (272, 256, 128, 64, 16, 8)) if wmax > 1024 else _pick(t, (544, 512, 256, 128, 64, 16, 8))
    nr, nb, no, ns = len(row_ins), len(bc_ins), len(out_widths), len(sum_widths)
    out_dtypes = out_dtypes or [F32] * no

    def body(*refs):
        i = pl.program_id(0)
        rows = [r[...] for r in refs[:nr]]
        bcs = [r[...] for r in refs[nr:nr + nb]]
        outs, sums = fn(i, tm, rows, bcs)
        for r, v in zip(refs[nr + nb:nr + nb + no], outs):
            r[...] = v.astype(r.dtype)
        if ns:
            srefs = refs[nr + nb + no:]

            @pl.when(i == 0)
            def _():
                for r in srefs:
                    r[...] = jnp.zeros(r.shape, r.dtype)

            for r, v in zip(srefs, sums):
                r[...] += v

    in_specs = [pl.BlockSpec((tm, a.shape[1]), lambda i: (i, 0)) for a in row_ins]
    in_specs += [pl.BlockSpec(a.shape, lambda i: (0, 0)) for a in bc_ins]
    out_specs = [pl.BlockSpec((tm, w), lambda i: (i, 0)) for w in out_widths]
    out_specs += [pl.BlockSpec((1, w), lambda i: (0, 0)) for w in sum_widths]
    out_shape = [jax.ShapeDtypeStruct((t, w), dt) for w, dt in zip(out_widths, out_dtypes)]
    out_shape += [jax.ShapeDtypeStruct((1, w), F32) for w in sum_widths]
    res = pl.pallas_call(body, name=name, grid=(t // tm,), in_specs=in_specs, out_specs=out_specs,
                         out_shape=out_shape, compiler_params=_params(("arbitrary",)))(*row_ins, *bc_ins)
    return res


def _colsum(v):
    return jnp.sum(v, axis=0, keepdims=True)


def matmul(name, a, b, mode, *, bias=None, act=None, add=None, add_scale=1.0, relu2_of=None, colsum=False,
           out_dtype=F32, out_slots=0, tm=None, tn=None, tk=None):
    slotted = b.ndim == 3
    if mode == 'nn':
        m, k = a.shape
        n = b.shape[-1] * (b.shape[0] if slotted else 1)
    elif mode == 'nt':
        m, k = a.shape
        n = b.shape[-2]
    else:
        k, m = a.shape
        n = b.shape[-1]
    ns = b.shape[-1] if slotted else None
    if mode == 'tn':
        tm = tm or _pick(m, (512, 256, 128))
        nw = n if not out_slots else n // out_slots
        if tn is None and tk is None:
            for cand in (512, 256):
                if nw % cand == 0 and (2 * k * (tm * a.dtype.itemsize + cand * b.dtype.itemsize)
                                       + 2 * tm * cand * 4 <= WHOLE_K_BYTES):
                    tn, tk = cand, k
                    break
        tn = tn or _pick(nw, (1024, 512, 256, 128))
        tk = tk or _pick(k, (1088, 1024, 768, 512, 384, 256, 128))
    else:
        tm = tm or _pick(m, (1088, 1024, 768, 512, 384, 256, 128))
        tn = tn or _pick(n if not (slotted and mode == 'nn') else ns, (1024, 512, 256, 128))
        extra = sum(x is not None for x in (add, relu2_of)) * 4
        if tk is None and not slotted and (2 * (tm * k * a.dtype.itemsize + k * tn * b.dtype.itemsize)
                                           + 2 * tm * tn * (4 + extra) <= WHOLE_K_BYTES):
            tk = k
        tk = tk or _pick(k if not (slotted and mode == 'nt') else ns, (1024, 512, 256, 128))
    gm, gn, gk = m // tm, n // tn, k // tk

    if mode == 'nn':
        a_spec = pl.BlockSpec((tm, tk), lambda i, j, kk: (i, kk))
        if slotted:
            per = ns // tn
            b_spec = pl.BlockSpec((None, tk, tn), lambda i, j, kk: (j // per, kk, j % per))
        else:
            b_spec = pl.BlockSpec((tk, tn), lambda i, j, kk: (kk, j))
        ca, cb = 1, 0
    elif mode == 'nt':
        a_spec = pl.BlockSpec((tm, tk), lambda i, j, kk: (i, kk))
        if slotted:
            per = ns // tk
            b_spec = pl.BlockSpec((None, tn, tk), lambda i, j, kk: (kk // per, j, kk % per))
        else:
            b_spec = pl.BlockSpec((tn, tk), lambda i, j, kk: (j, kk))
        ca, cb = 1, 1
    else:
        a_spec = pl.BlockSpec((tk, tm), lambda i, j, kk: (kk, i))
        b_spec = pl.BlockSpec((tk, tn), lambda i, j, kk: (kk, j))
        ca, cb = 0, 0
    in_specs = [a_spec, b_spec]
    args = [a, b]
    if bias is not None:
        in_specs.append(pl.BlockSpec((1, tn), lambda i, j, kk: (0, j)))
        args.append(bias)
    if add is not None:
        in_specs.append(pl.BlockSpec((tm, tn), lambda i, j, kk: (i, j)))
        args.append(add)
    if relu2_of is not None:
        in_specs.append(pl.BlockSpec((tm, tn), lambda i, j, kk: (i, j)))
        args.append(relu2_of)
    if out_slots:
        per_o = (n // out_slots) // tn
        out_spec = pl.BlockSpec((None, tm, tn), lambda i, j, kk: (j // per_o, i, j % per_o))
        out_shape = jax.ShapeDtypeStruct((out_slots, m, n // out_slots), out_dtype)
    else:
        out_spec = pl.BlockSpec((tm, tn), lambda i, j, kk: (i, j))
        out_shape = jax.ShapeDtypeStruct((m, n), out_dtype)
    grid = (gm, gn, gk)
    if colsum:
        assert mode == 'tn'
        out_spec = [out_spec, pl.BlockSpec((1, tn), lambda i, j, kk: (0, j))]
        out_shape = [out_shape, jax.ShapeDtypeStruct((1, n), F32)]

        def swapped(spec):
            return pl.BlockSpec(spec.block_shape, functools.partial(lambda j, i, kk, f: f(i, j, kk), f=spec.index_map))
        in_specs = [swapped(sp) for sp in in_specs]
        out_spec = [swapped(sp) for sp in out_spec]
        grid = (gn, gm, gk)

    def body(*refs):
        a_ref, b_ref = refs[0], refs[1]
        pos = 2
        bias_ref = add_ref = hid_ref = cs_ref = None
        if bias is not None:
            bias_ref = refs[pos]
            pos += 1
        if add is not None:
            add_ref = refs[pos]
            pos += 1
        if relu2_of is not None:
            hid_ref = refs[pos]
            pos += 1
        o_ref = refs[pos]
        pos += 1
        if colsum:
            cs_ref = refs[pos]
            pos += 1
        acc_ref = refs[pos] if gk > 1 else None
        bt = b_ref[...]
        p = _dot(a_ref[...].astype(BF16), bt.astype(BF16), ca, cb)
        kk = pl.program_id(2)

        if colsum:
            @pl.when(jnp.logical_and(pl.program_id(1) == 0, kk == 0))
            def _():
                cs_ref[...] = _colsum(bt.astype(F32))

            @pl.when(jnp.logical_and(pl.program_id(1) == 0, kk > 0))
            def _():
                cs_ref[...] += _colsum(bt.astype(F32))

        def finish(r):
            if bias_ref is not None:
                r = r + bias_ref[...]
            if act == 'relu2':
                r = jnp.square(jnp.maximum(r, 0.0))
            if hid_ref is not None:
                r = r * (2.0 * jnp.sqrt(hid_ref[...].astype(F32)))
            if add_ref is not None:
                r = r + add_scale * add_ref[...]
            o_ref[...] = r.astype(o_ref.dtype)

        if gk == 1:
            finish(p)
        else:
            @pl.when(kk == 0)
            def _():
                acc_ref[...] = p

            @pl.when(kk > 0)
            def _():
                acc_ref[...] += p

            @pl.when(kk == gk - 1)
            def _():
                finish(acc_ref[...])

    scratch = [pltpu.VMEM((tm, tn), F32)] if gk > 1 else []
    sem = ("arbitrary",) * 3 if colsum else ("parallel", "parallel", "arbitrary")
    return pl.pallas_call(body, name=name, grid=grid, in_specs=in_specs, out_specs=out_spec,
                          out_shape=out_shape, scratch_shapes=scratch, compiler_params=_params(sem))(*args)


def ln_fwd(name, h, mix, g, b):
    def fn(i, tm, rows, bcs):
        pre = ALPHA * rows[0] + rows[1]
        mu = jnp.mean(pre, axis=1, keepdims=True)
        xc = pre - mu
        var = jnp.mean(xc * xc, axis=1, keepdims=True)
        rstd = lax.rsqrt(var + LN_EPS)
        xhat = xc * rstd
        return (xhat * bcs[0] + bcs[1], xhat, rstd), ()
    return rowwise(name, fn, [h, mix], [g, b], [D_MODEL, D_MODEL, 1])


def ln_bwd(name, dy, xhat, rstd, g):
    def fn(i, tm, rows, bcs):
        dy_, xh, rs = rows
        dyg = dy_ * bcs[0]
        m1 = jnp.mean(dyg, axis=1, keepdims=True)
        m2 = jnp.mean(dyg * xh, axis=1, keepdims=True)
        dpre = rs * (dyg - m1 - xh * m2)
        return (dpre,), (_colsum(dy_ * xh), _colsum(dy_), _colsum(dpre))
    return rowwise(name, fn, [dy, xhat, rstd], [g], [D_MODEL], [D_MODEL] * 3)


_GELU_C = math.sqrt(2.0 / math.pi)


def gelu_fwd(name, x):
    def fn(i, tm, rows, bcs):
        v = rows[0]
        u = _GELU_C * (v + 0.044715 * (v * v * v))
        return (0.5 * v * (1.0 + jnp.tanh(u)),), ()
    return rowwise(name, fn, [x], [], [x.shape[1]])[0]


def gelu_bwd(name, dy, x):
    def fn(i, tm, rows, bcs):
        v = rows[1]
        th = jnp.tanh(_GELU_C * (v + 0.044715 * (v * v * v)))
        dg = 0.5 * (1.0 + th) + 0.5 * v * (1.0 - th * th) * (_GELU_C * (1.0 + 3.0 * 0.044715 * v * v))
        return (rows[0] * dg,), ()
    return rowwise(name, fn, [dy, x], [], [x.shape[1]])[0]


def glu_gate(name, y, gp):
    def fn(i, tm, rows, bcs):
        return (rows[0] * jax.nn.sigmoid(rows[1]),), ()
    return rowwise(name, fn, [y, gp], [], [y.shape[1]], out_dtypes=[BF16])[0]


def glu_gate_bwd(name, da, y, gp):
    def fn(i, tm, rows, bcs):
        s = jax.nn.sigmoid(rows[2])
        dgp = rows[0] * rows[1] * s * (1.0 - s)
        return (dgp, rows[0] * s), (_colsum(dgp),)
    w = y.shape[1]
    return rowwise(name, fn, [da, y, gp], [], [w, w], [w])


def loss_head(name, h, tgt, lp, seq):
    def fn(i, tm, rows, bcs):
        pos = (i * tm + lax.broadcasted_iota(jnp.int32, (tm, 1), 0)) % lp
        valid = jnp.logical_and(pos >= N_META, pos < N_META + seq)
        err = jnp.where(valid, rows[0] - rows[1], 0.0)
        part = 0.5 * jnp.sum(jnp.mean(err * err, axis=1, keepdims=True), axis=0, keepdims=True)
        return (err * (1.0 / D_MODEL),), (jnp.broadcast_to(part, (1, LANES)),)
    return rowwise(name, fn, [h, tgt], [], [D_MODEL], [LANES])


def adamw(name, w, gs, m, v):
    ng = len(gs)

    def fn(i, tm, rows, bcs):
        w_ = rows[0]
        g = rows[1] if ng == 1 else rows[1] + rows[2]
        m_, v_ = rows[1 + ng], rows[2 + ng]
        m_ = ADAM_B1 * m_ + (1.0 - ADAM_B1) * g
        v_ = ADAM_B2 * v_ + (1.0 - ADAM_B2) * jnp.square(g)
        m_hat = m_ / (1.0 - ADAM_B1 ** ADAM_STEP)
        v_hat = v_ / (1.0 - ADAM_B2 ** ADAM_STEP)
        delta = -ADAM_LR * (m_hat / (jnp.sqrt(v_hat) + ADAM_EPS) + ADAM_WD * w_)
        return (g, delta, m_, v_), ()
    wd = w.shape[1]
    return rowwise(name, fn, [w] + list(gs) + [m, v], [], [wd] * 4)


def s5_matrices(lam_re, lam_im, log_dt, b_re, b_im, c_re, c_im, d):
    c, hh, gg, pp = S5_CHUNK, S5_GROUP, S5_GROUPS, S5_STATE
    dt = jnp.exp(log_dt)[:, None]
    tau = jnp.arange(c + 1, dtype=F32)[None, :, None]
    mag = jnp.exp(tau * (lam_re * dt)[:, None, :])
    ang = tau * (lam_im * dt)[:, None, :]
    pw_re, pw_im = mag * jnp.cos(ang), mag * jnp.sin(ang)
    nr, ni = pw_re[:, 1] - 1.0, pw_im[:, 1]
    den = lam_re * lam_re + lam_im * lam_im
    cf_re = (nr * lam_re + ni * lam_im) / den
    cf_im = (ni * lam_re - nr * lam_im) / den
    bb_re = cf_re[:, :, None] * b_re - cf_im[:, :, None] * b_im
    bb_im = cf_re[:, :, None] * b_im + cf_im[:, :, None] * b_re
    ct_re, ct_im = c_re.transpose(0, 2, 1)[:, :, None, :], c_im.transpose(0, 2, 1)[:, :, None, :]
    pt_re, pt_im = pw_re.transpose(0, 2, 1)[:, :, :, None], pw_im.transpose(0, 2, 1)[:, :, :, None]
    cp_re = ct_re * pt_re - ct_im * pt_im
    cp_im = ct_re * pt_im + ct_im * pt_re
    hp = lax.Precision.HIGHEST
    kmat = (jnp.einsum('gpk,gpn->gkn', bb_re, cp_re[:, :, :c].reshape(gg, pp, c * hh), precision=hp)
            - jnp.einsum('gpk,gpn->gkn', bb_im, cp_im[:, :, :c].reshape(gg, pp, c * hh), precision=hp))
    rows = [jnp.pad(kmat[:, :, :(c - j) * hh], ((0, 0), (0, 0), (j * hh, 0))) for j in range(c)]
    a1 = jnp.stack(rows, axis=1).reshape(gg, c * hh, c * hh)
    a1 = a1 + jnp.eye(c * hh, dtype=F32)[None] * jnp.tile(d, (1, c))[:, None, :]
    a2 = jnp.concatenate([cp_re[:, :, 1:].reshape(gg, pp, c * hh), -cp_im[:, :, 1:].reshape(gg, pp, c * hh)], axis=1)
    rv_re, rv_im = pw_re[:, :c][:, ::-1, None, :], pw_im[:, :c][:, ::-1, None, :]
    bt_re, bt_im = bb_re.transpose(0, 2, 1)[:, None], bb_im.transpose(0, 2, 1)[:, None]
    a3 = jnp.concatenate([rv_re * bt_re - rv_im * bt_im, rv_re * bt_im + rv_im * bt_re], axis=-1)
    a3 = a3.reshape(gg, c * hh, 2 * pp)
    are = jnp.concatenate([pw_re[:, c], pw_re[:, c]], axis=-1)[:, None, :]
    aim = jnp.concatenate([-pw_im[:, c], pw_im[:, c]], axis=-1)[:, None, :]
    return a1, a2, a3, are, aim


S5_LEVELS = 8


def s5_scan_powers(lam_re, lam_im, log_dt):
    dt = jnp.exp(log_dt)[:, None]
    e = (S5_CHUNK * 2.0 ** jnp.arange(S5_LEVELS, dtype=F32))[:, None, None]
    mag = jnp.exp(e * (lam_re * dt)[None])
    ang = e * (lam_im * dt)[None]
    pr, pi = mag * jnp.cos(ang), mag * jnp.sin(ang)
    pre = jnp.concatenate([pr, pr], axis=-1).transpose(1, 0, 2)
    pim = jnp.concatenate([-pi, pi], axis=-1).transpose(1, 0, 2)
    return pre, pim


def _s5_scan(x, pre, pim, reverse):
    n = x.shape[0]
    rowi = lax.broadcasted_iota(jnp.int32, (n, 1), 0)
    t = x
    for k in range(S5_LEVELS):
        shift = 2 ** k
        if shift >= n:
            break
        p_re, p_im = pre[k:k + 1, :], pim[k:k + 1, :]
        if reverse:
            far = jnp.where(rowi < n - shift, pltpu.roll(t, n - shift, 0), 0.0)
            t = t + (p_re * far + pltpu.roll(p_im * far, S5_STATE, 1))
        else:
            far = jnp.where(rowi >= shift, pltpu.roll(t, shift, 0), 0.0)
            t = t + (p_re * far + p_im * pltpu.roll(far, S5_STATE, 1))
    return t


def s5_fwd(ug, a1, a2, a3, pre, pim, nseq):
    g, nc, cw = ug.shape
    per = nc // nseq
    sw = 2 * S5_STATE
    assert per <= 2 ** S5_LEVELS

    def body(u_ref, a1_ref, a2_ref, a3_ref, pre_ref, pim_ref, y_ref, s_ref):
        u = u_ref[...]
        x = _dot3(u, a3_ref[...], 1, 0)
        first = lax.broadcasted_iota(jnp.int32, (per, 1), 0) == 0
        for b in range(nseq):
            t = _s5_scan(x[b * per:(b + 1) * per], pre_ref[...], pim_ref[...], False)
            s_ref[pl.ds(b * per, per), :] = jnp.where(first, 0.0, pltpu.roll(t, 1, 0))
        y_ref[...] = _dot3(u, a1_ref[...], 1, 0) + _dot3(s_ref[...], a2_ref[...], 1, 0)

    def spec(shape):
        return pl.BlockSpec((None,) + shape, lambda i: (i, 0, 0))
    lv = (S5_LEVELS, sw)
    return pl.pallas_call(
        body, name="s5_fwd", grid=(g,),
        in_specs=[spec((nc, cw)), spec((cw, cw)), spec((sw, cw)), spec((cw, sw)), spec(lv), spec(lv)],
        out_specs=[spec((nc, cw)), spec((nc, sw))],
        out_shape=[jax.ShapeDtypeStruct((g, nc, cw), F32), jax.ShapeDtypeStruct((g, nc, sw), F32)],
        compiler_params=_params(("parallel",)))(ug, a1, a2, a3, pre, pim)


def s5_bwd(dyg, ug, sst, a1, a2, a3, pre, pim, nseq, ex=None):
    g, nc, cw = ug.shape
    per = nc // nseq
    sw = 2 * S5_STATE
    split, ex_start, ex_finish, exkw = _hidden_exchange(ex, 8, 6, (g,))

    def body(*refs):
        ((dy_ref, u_ref, s_ref, a1_ref, a2_ref, a3_ref, pre_ref, pim_ref),
         (du_ref, da1_ref, da2_ref, da3_ref, dare_ref, daim_ref), ex_parts, (dx_ref,)) = split(refs)
        ex_start(ex_parts)
        dy = dy_ref[...]
        u = u_ref[...]
        gd = _dot3(dy, a2_ref[...], 1, 1)
        s_all = s_ref[...]
        da2_ref[...] = _dot3(s_all, dy, 0, 0)
        last = lax.broadcasted_iota(jnp.int32, (per, 1), 0) == per - 1
        for b in range(nseq):
            gs = _s5_scan(gd[b * per:(b + 1) * per], pre_ref[...], pim_ref[...], True)
            dx_ref[pl.ds(b * per, per), :] = jnp.where(last, 0.0, pltpu.roll(gs, per - 1, 0))
        dx = dx_ref[...]
        dare_ref[...] = _colsum(dx * s_all)
        daim_ref[...] = _colsum(dx * pltpu.roll(s_all, S5_STATE, 1))
        du_ref[...] = _dot3(dy, a1_ref[...], 1, 1) + _dot3(dx, a3_ref[...], 1, 1)
        da1_ref[...] = _dot3(u, dy, 0, 0)
        da3_ref[...] = _dot3(u, dx, 0, 0)
        ex_finish(ex_parts)

    def spec(shape):
        return pl.BlockSpec((None,) + shape, lambda i: (i, 0, 0))
    sds = jax.ShapeDtypeStruct
    return pl.pallas_call(
        body, name="s5_bwd", grid=(g,),
        in_specs=[spec((nc, cw)), spec((nc, cw)), spec((nc, sw)), spec((cw, cw)), spec((sw, cw)), spec((cw, sw)),
                  spec((S5_LEVELS, sw)), spec((S5_LEVELS, sw))] + exkw['in_specs'],
        out_specs=[spec((nc, cw)), spec((cw, cw)), spec((sw, cw)), spec((cw, sw)), spec((1, sw)), spec((1, sw))]
        + exkw['out_specs'],
        out_shape=[sds((g, nc, cw), F32), sds((g, cw, cw), F32), sds((g, sw, cw), F32), sds((g, cw, sw), F32),
                   sds((g, 1, sw), F32), sds((g, 1, sw), F32)] + exkw['out_shape'],
        input_output_aliases=exkw['aliases'], scratch_shapes=[pltpu.VMEM((nc, sw), F32)] + exkw['scratch'],
        compiler_params=_params(("arbitrary",)))(dyg, ug, sst, a1, a2, a3, pre, pim, *exkw['ins'])


def _to_groups(u):
    t = u.shape[0]
    nc = t // S5_CHUNK
    return u.reshape(nc, S5_CHUNK, S5_GROUPS, S5_GROUP).transpose(2, 0, 1, 3).reshape(S5_GROUPS, nc, -1)


def _from_groups(yg):
    g, nc, _ = yg.shape
    return yg.reshape(g, nc, S5_CHUNK, S5_GROUP).transpose(1, 2, 0, 3).reshape(nc * S5_CHUNK, g * S5_GROUP)


def _sb_masks():
    row = lax.broadcasted_iota(jnp.int32, (SB_BLOCK, SB_BLOCK), 0)
    col = lax.broadcasted_iota(jnp.int32, (SB_BLOCK, SB_BLOCK), 1)
    return row, col


def _sb_weights(z, vis, later_of, after):
    sp = jnp.maximum(z, 0.0) + jnp.log1p(jnp.exp(-jnp.abs(z)))
    lk = jnp.where(vis, -sp, 0.0)
    rs = jnp.sum(lk, axis=1, keepdims=True)
    later = _dot_exact_rhs(lk, after, 1, 0) + later_of(rs)
    lb = z - sp
    w = jnp.where(vis, jnp.exp(lb + later), 0.0)
    return w, rs, lb


def _hidden_exchange(ex, n_in, n_out, grid):
    ni, no = (len(ex.ins), len(ex.out_shapes)) if ex else (0, 0)

    def split(refs):
        a, b = n_in + ni, n_in + ni + n_out
        end = len(refs) - (2 if ex else 0)
        return refs[:n_in], refs[a:b], (refs[n_in:a], refs[b:b + no], refs[end:]), refs[b + no:end]

    def at_step(which, fn, parts):
        if ex is not None:
            ids = [pl.program_id(d) == (0 if which == 'first' else g - 1) for d, g in enumerate(grid)]
            cond = ids[0]
            for c in ids[1:]:
                cond = jnp.logical_and(cond, c)
            pl.when(cond)(lambda: fn(parts[0], parts[1], *parts[2]))

    anyspec = pl.BlockSpec(memory_space=pl.ANY)
    kw = dict(in_specs=[anyspec] * ni, out_specs=[anyspec] * no, out_shape=list(ex.out_shapes) if ex else [],
              aliases={n_in + i: n_out + j for i, j in (ex.aliases().items() if ex else ())},
              scratch=ex.sems() if ex else [], ins=list(ex.ins) if ex else [])
    start = functools.partial(at_step, 'first', ex.start if ex else None)
    finish = functools.partial(at_step, 'last', ex.finish if ex else None)
    return split, start, finish, kw


SB_TILES = 2
SB_HEADS = 2 * SB_TILES
SB_WD = SB_TILES * LANES


def _stack_heads(x):
    return jnp.concatenate([x] * SB_HEADS, axis=0)


def _stack_masks():
    row = lax.broadcasted_iota(jnp.int32, (SB_HEADS * SB_BLOCK, SB_WD), 0)
    col = lax.broadcasted_iota(jnp.int32, (SB_HEADS * SB_BLOCK, SB_WD), 1)
    rowk = lax.broadcasted_iota(jnp.int32, (SB_HEADS * SB_BLOCK, SB_BLOCK), 0)
    colk = lax.broadcasted_iota(jnp.int32, (SB_HEADS * SB_BLOCK, SB_BLOCK), 1)
    return (col // 64) == (row // SB_BLOCK), colk < (rowk % SB_BLOCK)


def _own_lanes(r):
    head = lax.broadcasted_iota(jnp.int32, (SB_BLOCK, SB_WD), 1) // 64
    out = r[:SB_BLOCK]
    for h in range(1, SB_HEADS):
        out = jnp.where(head == h, r[h * SB_BLOCK:(h + 1) * SB_BLOCK], out)
    return out


def _sb_specs(lp, nseq):
    wd = SB_TILES * LANES
    off = [(S5_WIDTH + i * SB_WIDTH) // wd for i in range(3)]
    blk = (lp, wd)
    qkv = [pl.BlockSpec(blk, functools.partial(lambda b, h, o: (b, o + h), o=o)) for o in off]
    return (blk, qkv, pl.BlockSpec(blk, lambda b, h: (b, h)),
            pl.BlockSpec((None, None, 8, LANES), lambda b, h: (b, h, 0, 0)))


def attn_fwd(proj, nseq, lp, ex=None):
    nqb = lp // SB_BLOCK
    t = proj.shape[0]
    nstep = SB_WIDTH // (SB_TILES * LANES)
    split, ex_start, ex_finish, exkw = _hidden_exchange(ex, 3, 3, (nseq, nstep))

    def body(*refs):
        (q_ref, k_ref, v_ref), (o_ref, tot_ref, first_ref), ex_parts, _ = split(refs)
        ex_start(ex_parts)
        row, col = _sb_masks()
        after = (row > col).astype(BF16)
        lane8 = lax.broadcasted_iota(jnp.int32, (8, LANES), 1)
        hms, tris = _stack_masks()

        def qloop(qi, firsts):
            q0 = pl.multiple_of(qi * SB_BLOCK, SB_BLOCK)
            qm = jnp.where(hms, _stack_heads(q_ref[pl.ds(q0, SB_BLOCK), :] * 0.125), 0.0).astype(BF16)

            def alive(carry):
                return jnp.logical_and(carry[0] <= qi, carry[1] > 0)

            def kstep(carry):
                s, _, acc, lat = carry
                kj = qi - s
                k0 = pl.multiple_of(kj * SB_BLOCK, SB_BLOCK)
                k = k_ref[pl.ds(k0, SB_BLOCK), :].astype(BF16)
                v = v_ref[pl.ds(k0, SB_BLOCK), :].astype(BF16)
                vis = jnp.logical_or(kj < qi, tris)
                w, rs, _ = _sb_weights(_dot(qm, k, 1, 1), vis, functools.partial(lambda rs, lat: lat, lat=lat), after)
                acc = acc + _own_lanes(_dot(w.astype(BF16), v, 1, 0))
                lat = lat + rs
                return s + 1, (jnp.max(lat) > SB_DEAD).astype(jnp.int32), acc, lat

            res = lax.while_loop(alive, kstep, (jnp.int32(0), jnp.int32(1), jnp.zeros((SB_BLOCK, SB_WD), F32),
                                                jnp.zeros((SB_HEADS * SB_BLOCK, 1), F32)))
            o_ref[pl.ds(q0, SB_BLOCK), :] = res[2].astype(o_ref.dtype)
            tot_ref[pl.ds(q0, SB_BLOCK), :] = _own_lanes(jnp.broadcast_to(res[3], (SB_HEADS * SB_BLOCK, SB_WD)))
            return jnp.where(lane8 == qi, (qi - res[0] + 1).astype(F32), firsts)

        first_ref[...] = lax.fori_loop(0, nqb, qloop, jnp.zeros((8, LANES), F32))
        ex_finish(ex_parts)

    blk, qkv, out, vec = _sb_specs(lp, nseq)
    return pl.pallas_call(
        body, name="attn_fwd", grid=(nseq, nstep),
        in_specs=qkv + exkw['in_specs'], out_specs=[out, out, vec] + exkw['out_specs'],
        out_shape=[jax.ShapeDtypeStruct((t, SB_WIDTH), BF16), jax.ShapeDtypeStruct((t, SB_WIDTH), F32)]
        + [jax.ShapeDtypeStruct((nseq, nstep, 8, LANES), F32)] + exkw['out_shape'],
        input_output_aliases=exkw['aliases'], scratch_shapes=exkw['scratch'],
        compiler_params=_params(("arbitrary", "arbitrary")))(proj, proj, proj, *exkw['ins'])


def attn_bwd(proj, tot, first, do, nseq, lp, ex=None):
    nqb = lp // SB_BLOCK
    t = proj.shape[0]
    nstep = SB_WIDTH // (SB_TILES * LANES)
    split, ex_start, ex_finish, exkw = _hidden_exchange(ex, 6, 3, (nseq, nstep))

    def body(*refs):
        (q_ref, k_ref, v_ref, tot_ref, first_ref, do_ref), (dq_ref, dk_ref, dv_ref), ex_parts, _ = split(refs)
        ex_start(ex_parts)
        row, col = _sb_masks()
        after = (row > col).astype(BF16)
        before = (row < col).astype(BF16)
        lane8 = lax.broadcasted_iota(jnp.int32, (8, LANES), 1)
        firsts = first_ref[...]
        dk_ref[...] = jnp.zeros(dk_ref.shape, F32)
        dv_ref[...] = jnp.zeros(dv_ref.shape, F32)
        hms, tris = _stack_masks()

        def qloop(qi, _):
            first = jnp.max(jnp.where(lane8 == qi, firsts, 0.0)).astype(jnp.int32)
            first = jnp.clip(first, 0, qi)
            q0 = pl.multiple_of(qi * SB_BLOCK, SB_BLOCK)
            qm = jnp.where(hms, _stack_heads(q_ref[pl.ds(q0, SB_BLOCK), :] * 0.125), 0.0).astype(BF16)
            dom = jnp.where(hms, _stack_heads(do_ref[pl.ds(q0, SB_BLOCK), :]), 0.0).astype(BF16)
            tot = jnp.max(jnp.where(hms, _stack_heads(tot_ref[pl.ds(q0, SB_BLOCK), :]), -jnp.inf),
                          axis=1, keepdims=True)

            def kloop(kj, carry):
                dq, pre, gpre = carry
                k0 = pl.multiple_of(kj * SB_BLOCK, SB_BLOCK)
                k = k_ref[pl.ds(k0, SB_BLOCK), :].astype(BF16)
                v = v_ref[pl.ds(k0, SB_BLOCK), :].astype(BF16)
                vis = jnp.logical_or(kj < qi, tris)
                later_of = functools.partial(lambda rs, t_, p_: t_ - p_ - rs, t_=tot, p_=pre)
                w, rs, lb = _sb_weights(_dot(qm, k, 1, 1), vis, later_of, after)
                g = _dot(dom, v, 1, 1) * w
                dlk = _dot_exact_rhs(g, before, 1, 0) + gpre
                sig = jnp.exp(lb)
                dz = jnp.where(vis, g * (1.0 - sig) - dlk * sig, 0.0).astype(BF16)
                dk_ref[pl.ds(k0, SB_BLOCK), :] += _dot(dz, qm, 0, 0)
                dv_ref[pl.ds(k0, SB_BLOCK), :] += _dot(w.astype(BF16), dom, 0, 0)
                return dq + _own_lanes(_dot(dz, k, 1, 0)), pre + rs, gpre + jnp.sum(g, axis=1, keepdims=True)

            z1 = jnp.zeros((SB_HEADS * SB_BLOCK, 1), F32)
            res = lax.fori_loop(first, qi + 1, kloop, (jnp.zeros((SB_BLOCK, SB_WD), F32), z1, z1))
            dq_ref[pl.ds(q0, SB_BLOCK), :] = res[0] * 0.125
            return 0

        lax.fori_loop(0, nqb, qloop, 0)
        ex_finish(ex_parts)

    blk, qkv, out, vec = _sb_specs(lp, nseq)
    return pl.pallas_call(
        body, name="attn_bwd", grid=(nseq, nstep),
        in_specs=qkv + [out, vec, out] + exkw['in_specs'], out_specs=[out] * 3 + exkw['out_specs'],
        out_shape=[jax.ShapeDtypeStruct((t, SB_WIDTH), F32)] * 3 + exkw['out_shape'],
        input_output_aliases=exkw['aliases'], scratch_shapes=exkw['scratch'],
        compiler_params=_params(("arbitrary", "arbitrary")))(proj, proj, proj, tot, first, do, *exkw['ins'])


def _hg_loop(nch, step, init):
    assert nch % HG_UNROLL == 0

    def trip(i, carry):
        for u in range(HG_UNROLL):
            carry = step(i * HG_UNROLL + u, carry)
        return carry
    return lax.fori_loop(0, nch // HG_UNROLL, trip, init)


def _hg_gates(fc, lb):
    sg = jax.nn.sigmoid(fc)
    f = lb + (1.0 - lb) * sg
    return sg, f


def _dot_exact_lhs(m, x):
    hi, mid, lo = _split3(x)
    return _dot(m, hi, 1, 0) + (_dot(m, mid, 1, 0) + _dot(m, lo, 1, 0))


def hgrn_fwd(proj, lb, ng, nseq, lp, ex=None):
    nch = lp // HG_CHUNK
    t = proj.shape[0]
    c = HG_CHUNK

    split, ex_start, ex_finish, exkw = _hidden_exchange(ex, 6, 3, (HG_HEADS,))

    def body(*refs):
        (q_ref, f_ref, v_ref, g_ref, lb_ref, ng_ref), (o_ref, og_ref, st_ref), ex_parts, _ = split(refs)
        ex_start(ex_parts)
        lbv, ngv = lb_ref[...], ng_ref[...]

        rr = nseq * c
        seq_r = lax.broadcasted_iota(jnp.int32, (rr, 1), 0) // c
        bd_row = lax.broadcasted_iota(jnp.int32, (rr, rr), 0)
        bd_col = lax.broadcasted_iota(jnp.int32, (rr, rr), 1)
        causal = jnp.logical_and(bd_row // c == bd_col // c, bd_col <= bd_row)
        incl = causal.astype(BF16)

        def stacked(ref, ci):
            return jnp.concatenate([ref[pl.ds(pl.multiple_of(b * lp + ci * c, c), c), :] for b in range(nseq)], axis=0)

        def wide(x):
            return jnp.concatenate([jnp.where(seq_r == b, x, jnp.zeros_like(x)) for b in range(nseq)], axis=1)

        def step(ci, st):
            for b in range(nseq):
                st_ref[b, ci] = st[:, b * HG_DK:(b + 1) * HG_DK]
            _, f = _hg_gates(stacked(f_ref, ci), lbv)
            bcum = _dot_exact_lhs(incl, jnp.log(f))
            lasts = [bcum[b * c + c - 1:b * c + c, :] for b in range(nseq)]
            blast = jnp.concatenate([jnp.broadcast_to(x, (c, LANES)) for x in lasts], axis=0)
            kk = 1.0 - f
            vc = stacked(v_ref, ci).astype(BF16)
            qd = (stacked(q_ref, ci) * jnp.exp(bcum)).astype(BF16)
            kd = (kk * jnp.exp(-bcum)).astype(BF16)
            a = jnp.where(causal, _dot(qd, kd, 1, 1), 0.0)
            o = _dot(a.astype(BF16), vc, 1, 0) + _dot(wide(qd), st.astype(BF16), 1, 1)
            kend = (kk * jnp.exp(blast - bcum)).astype(BF16)
            st = st * jnp.exp(jnp.concatenate(lasts, axis=1)) + _dot(vc, wide(kend), 0, 0)
            r = lax.rsqrt(jnp.mean(o * o, axis=1, keepdims=True) + RMS_EPS)
            gc = stacked(g_ref, ci)
            og = (o * r * ngv * (gc * jax.nn.sigmoid(gc))).astype(og_ref.dtype)
            for b in range(nseq):
                rows = pl.ds(pl.multiple_of(b * lp + ci * c, c), c)
                o_ref[rows, :] = o[b * c:(b + 1) * c]
                og_ref[rows, :] = og[b * c:(b + 1) * c]
            return st

        _hg_loop(nch, step, jnp.zeros((HG_DK, nseq * HG_DK), F32))
        ex_finish(ex_parts)

    blk = (nseq * lp, LANES)
    h = HG_HEADS
    return pl.pallas_call(
        body, name="hgrn_fwd", grid=(h,),
        in_specs=[pl.BlockSpec(blk, lambda hh: (0, hh)),
                  pl.BlockSpec(blk, lambda hh: (0, h + hh)),
                  pl.BlockSpec(blk, lambda hh: (0, 2 * h + hh)),
                  pl.BlockSpec(blk, lambda hh: (0, 3 * h + hh)),
                  pl.BlockSpec((1, LANES), lambda hh: (0, hh)),
                  pl.BlockSpec((1, LANES), lambda hh: (0, hh))] + exkw['in_specs'],
        out_specs=[pl.BlockSpec(blk, lambda hh: (0, hh)),
                   pl.BlockSpec(blk, lambda hh: (0, hh)),
                   pl.BlockSpec((nseq, None, nch, HG_DK, HG_DK), lambda hh: (0, hh, 0, 0, 0))] + exkw['out_specs'],
        out_shape=[jax.ShapeDtypeStruct((t, D_MODEL), F32), jax.ShapeDtypeStruct((t, D_MODEL), BF16),
                   jax.ShapeDtypeStruct((nseq, h, nch, HG_DK, HG_DK), F32)] + exkw['out_shape'],
        input_output_aliases=exkw['aliases'], scratch_shapes=exkw['scratch'],
        compiler_params=_params(("arbitrary",)))(proj, proj, proj, proj, lb, ng, *exkw['ins'])


def hgrn_bwd(proj, o, dog, states, lb, ng, nseq, lp):
    nch = lp // HG_CHUNK
    t = proj.shape[0]
    c = HG_CHUNK
    rr = nseq * c

    def body(q_ref, f_ref, v_ref, g_ref, o_ref, dog_ref, st_ref, lb_ref, ng_ref,
             dq_ref, df_ref, dv_ref, dg_ref, dlb_ref, dng_ref):
        seq_r = lax.broadcasted_iota(jnp.int32, (rr, 1), 0) // c
        last = lax.broadcasted_iota(jnp.int32, (rr, 1), 0) % c == c - 1
        bd_row = lax.broadcasted_iota(jnp.int32, (rr, rr), 0)
        bd_col = lax.broadcasted_iota(jnp.int32, (rr, rr), 1)
        same = bd_row // c == bd_col // c
        causal = jnp.logical_and(same, bd_col <= bd_row)
        incl = causal.astype(BF16)
        from_here = jnp.logical_and(same, bd_col >= bd_row).astype(BF16)
        lbv, ngv = lb_ref[...], ng_ref[...]

        def stacked(ref, ci):
            return jnp.concatenate([ref[pl.ds(pl.multiple_of(b * lp + ci * c, c), c), :] for b in range(nseq)], axis=0)

        def wide(x):
            return jnp.concatenate([jnp.where(seq_r == b, x, jnp.zeros_like(x)) for b in range(nseq)], axis=1)

        def own(x):
            return functools.reduce(jnp.add, [jnp.where(seq_r == b, x[:, b * LANES:(b + 1) * LANES], 0.0)
                                              for b in range(nseq)])

        def per_seq_rows(vals):
            return jnp.concatenate([jnp.broadcast_to(x, (c, LANES)) for x in vals], axis=0)

        def step(s, carry):
            dst, dlb, dng = carry
            ci = nch - 1 - s
            oc, gc, dogc = stacked(o_ref, ci), stacked(g_ref, ci), stacked(dog_ref, ci)
            r = lax.rsqrt(jnp.mean(oc * oc, axis=1, keepdims=True) + RMS_EPS)
            sgg = jax.nn.sigmoid(gc)
            dgc = dogc * (oc * r * ngv) * (sgg * (1.0 + gc * (1.0 - sgg)))
            don = dogc * (gc * sgg)
            dng = dng + _colsum(don * oc * r)
            tt = don * ngv
            do = r * (tt - oc * (r * r) * jnp.mean(tt * oc, axis=1, keepdims=True))
            st = jnp.concatenate([st_ref[b, ci] for b in range(nseq)], axis=1)
            sg, f = _hg_gates(stacked(f_ref, ci), lbv)
            bcum = _dot_exact_lhs(incl, jnp.log(f))
            lasts = [bcum[b * c + c - 1:b * c + c, :] for b in range(nseq)]
            blast = per_seq_rows(lasts)
            kk = 1.0 - f
            qc, vf = stacked(q_ref, ci), stacked(v_ref, ci)
            pdec = jnp.exp(bcum)
            ndec = jnp.exp(-bcum)
            edec = jnp.exp(blast - bcum)
            eb = jnp.exp(jnp.concatenate(lasts, axis=1))
            qd_f, kd_f, kend_f = qc * pdec, kk * ndec, kk * edec
            qd, kd, kend = qd_f.astype(BF16), kd_f.astype(BF16), kend_f.astype(BF16)
            vc, dob, stb, dstb = vf.astype(BF16), do.astype(BF16), st.astype(BF16), dst.astype(BF16)
            a = jnp.where(causal, _dot(qd, kd, 1, 1), 0.0).astype(BF16)
            da = jnp.where(causal, _dot(dob, vc, 1, 1), 0.0).astype(BF16)
            dv = _dot(a, dob, 0, 0) + _dot(wide(kend), dstb, 1, 1)
            dqd = _dot(da, kd, 1, 0) + own(_dot(dob, stb, 1, 0))
            dkd = _dot(da, qd, 0, 0)
            dkend = own(_dot(vc, dstb, 1, 0))
            sts = _colsum(dst * st)
            dkk = dkend * kend_f
            dblast = per_seq_rows([_colsum(jnp.where(seq_r == b, dkk, 0.0))
                                   + eb[:, b * LANES:(b + 1) * LANES] * sts[:, b * LANES:(b + 1) * LANES]
                                   for b in range(nseq)])
            dbcum = dqd * qd_f - dkd * kd_f - dkk
            dbcum = dbcum + jnp.where(last, dblast, 0.0)
            dlf = _dot_exact_lhs(from_here, dbcum)
            df = dlf / f - (dkd * ndec + dkend * edec)
            dfp = df * (1.0 - lbv) * sg * (1.0 - sg)
            dqc = dqd * pdec
            for b in range(nseq):
                rows = pl.ds(pl.multiple_of(b * lp + ci * c, c), c)
                blk_rows = slice(b * c, (b + 1) * c)
                dg_ref[rows, :], dv_ref[rows, :] = dgc[blk_rows], dv[blk_rows]
                dq_ref[rows, :], df_ref[rows, :] = dqc[blk_rows], dfp[blk_rows]
            dlb = dlb + _colsum(df * (1.0 - sg))
            dst = dst * eb + _dot(dob, wide(qd), 0, 0)
            return dst, dlb, dng

        z = jnp.zeros((1, LANES), F32)
        _, dlb, dng = _hg_loop(nch, step, (jnp.zeros((HG_DK, nseq * HG_DK), F32), z, z))
        dlb_ref[...] = dlb
        dng_ref[...] = dng

    blk = (nseq * lp, LANES)
    h = HG_HEADS
    vec = pl.BlockSpec((1, LANES), lambda hh: (0, hh))
    col = pl.BlockSpec(blk, lambda hh: (0, hh))
    return pl.pallas_call(
        body, name="hgrn_bwd", grid=(h,),
        in_specs=[col,
                  pl.BlockSpec(blk, lambda hh: (0, h + hh)),
                  pl.BlockSpec(blk, lambda hh: (0, 2 * h + hh)),
                  pl.BlockSpec(blk, lambda hh: (0, 3 * h + hh)),
                  col, col,
                  pl.BlockSpec((nseq, None, nch, HG_DK, HG_DK), lambda hh: (0, hh, 0, 0, 0)),
                  vec, vec],
        out_specs=[col] * 4 + [vec, vec],
        out_shape=[jax.ShapeDtypeStruct((t, D_MODEL), F32)] * 4 + [jax.ShapeDtypeStruct((1, D_MODEL), F32)] * 2,
        compiler_params=pltpu.CompilerParams(dimension_semantics=("parallel",), vmem_limit_bytes=HGRN_BWD_VMEM))(
            proj, proj, proj, proj, o, dog, states, lb, ng)


def _lower_bound(gamma):
    p = jax.nn.softmax(gamma, axis=0)
    return (jnp.cumsum(p, axis=0) - p[0])[1][None, :]


def local_step(x, tgt, w, hooks=None):
    nseq, seq, _ = x.shape
    lp = -(-(N_META + seq) // SB_BLOCK) * SB_BLOCK
    t = nseq * lp
    pad = lp - N_META - seq
    h0 = jnp.concatenate([jnp.broadcast_to(w['meta'][None], (nseq, N_META, D_MODEL)), x,
                          jnp.zeros((nseq, pad, D_MODEL), F32)], axis=1).reshape(t, D_MODEL)
    tgt_p = jnp.pad(tgt, ((0, 0), (N_META, pad), (0, 0))).reshape(t, D_MODEL)
    row = lambda v: v.reshape(1, -1)
    g = {}

    proj = matmul("in_ab", h0, w['w_in_ab'], 'nn')
    att = attn_fwd(proj, nseq, lp, ex=hooks.gather_late if hooks else None)
    b_out, sb_tot, sb_first = att[:3]
    if hooks:
        w = {**w, **hooks.late_weights(att[3:])}
    s5_names = ['s5_lam_re', 's5_lam_im', 's5_log_dt', 's5_b_re', 's5_b_im', 's5_c_re', 's5_c_im', 's5_d']
    mats, mats_vjp = jax.vjp(s5_matrices, *[w[n][0] for n in s5_names])
    ug = _to_groups(proj[:, :S5_WIDTH])
    s5_pows = s5_scan_powers(*[w[n][0] for n in s5_names[:3]])
    yg, sst = s5_fwd(ug, *mats[:3], *s5_pows, nseq)
    ys5 = _from_groups(yg)
    y = gelu_fwd("gelu", ys5)
    gp = matmul("glu", y, w['s5_w_glu'], 'nn', bias=w['s5_b_glu'])
    a_out = glu_gate("glu_gate", y, gp)
    cat = jnp.concatenate([a_out, b_out], axis=1)
    mix0 = matmul("out_ab", cat, w['w_out_ab'], 'nn')
    h1, xh1, rs1 = ln_fwd("ln_mix0", h0, mix0, row(w['ln_mix_g'][0]), row(w['ln_mix_b'][0]))
    hid0 = matmul("up0", h1, w['mlp_w_up'][0], 'nn', bias=row(w['mlp_b_up'][0]), act='relu2', out_dtype=BF16)
    dn0 = matmul("down0", hid0, w['mlp_w_down'][0], 'nn', bias=row(w['mlp_b_down'][0]))
    h2, xh2, rs2 = ln_fwd("ln_mlp0", h1, dn0, row(w['ln_mlp_g'][0]), row(w['ln_mlp_b'][0]))

    projc = matmul("in_c", h2, w['w_in_c'], 'nn')
    lb, lb_vjp = jax.vjp(_lower_bound, w['hgrn_gamma'])
    ng = w['hgrn_norm_g']
    res = hgrn_fwd(projc, lb, ng, nseq, lp, ex=hooks.gather_last if hooks else None)
    o, og, states = res[:3]
    if hooks:
        w = {**w, **hooks.last_weights(res[3:])}
    mix1 = matmul("out_c", og, w['w_out_c'], 'nn')
    h3, xh3, rs3 = ln_fwd("ln_mix1", h2, mix1, row(w['ln_mix_g'][1]), row(w['ln_mix_b'][1]))
    hid1 = matmul("up1", h3, w['mlp_w_up'][1], 'nn', bias=row(w['mlp_b_up'][1]), act='relu2', out_dtype=BF16)
    dn1 = matmul("down1", hid1, w['mlp_w_down'][1], 'nn', bias=row(w['mlp_b_down'][1]))
    h4, xh4, rs4 = ln_fwd("ln_mlp1", h3, dn1, row(w['ln_mlp_g'][1]), row(w['ln_mlp_b'][1]))
    dy, loss = loss_head("loss", h4, tgt_p, lp, seq)

    def mlp_bwd(l, dh_out, xh_out, rs_out, hid, h_in):
        dpre, dg_, db_, dbd = ln_bwd(f"ln_mlp{l}_b", dh_out, xh_out, rs_out, row(w['ln_mlp_g'][l]))
        dpu = matmul(f"down{l}_dx", dpre, w['mlp_w_down'][l], 'nt', relu2_of=hid, out_dtype=BF16)
        dwd = matmul(f"down{l}_dw", hid, dpre, 'tn')
        dh_in = matmul(f"up{l}_dx", dpu, w['mlp_w_up'][l], 'nt', add=dpre, add_scale=ALPHA)
        dwu, dbu = matmul(f"up{l}_dw", h_in, dpu, 'tn', out_slots=N_CHIPS, colsum=True)
        return dh_in, dict(ln_mlp_g=dg_, ln_mlp_b=db_, mlp_b_down=dbd, mlp_b_up=dbu, mlp_w_up=dwu, mlp_w_down=dwd)

    dh3, gm1 = mlp_bwd(1, dy, xh4, rs4, hid1, h3)
    dpre, dg1, db1, _ = ln_bwd("ln_mix1_b", dh3, xh3, rs3, row(w['ln_mix_g'][1]))
    g['w_out_c'] = matmul("out_c_dw", og, dpre, 'tn')
    dog = matmul("out_c_dx", dpre, w['w_out_c'], 'nt')
    dq, df, di, dgg, dlb, dng = hgrn_bwd(projc, o, dog, states, lb, ng, nseq, lp)
    dprojc = jnp.concatenate([dq, df, di, dgg], axis=1)
    dh2 = matmul("in_c_dx", dprojc, w['w_in_c'], 'nt', add=dpre, add_scale=ALPHA)
    g['w_in_c'] = matmul("in_c_dw", h2, dprojc, 'tn', out_slots=N_CHIPS)
    g['hgrn_gamma'] = lb_vjp(dlb)[0]
    g['hgrn_norm_g'] = dng

    dh1, gm0 = mlp_bwd(0, dh2, xh2, rs2, hid0, h1)
    dpre, dg0, db0, _ = ln_bwd("ln_mix0_b", dh1, xh1, rs1, row(w['ln_mix_g'][0]))
    g['w_out_ab'] = matmul("out_ab_dw", cat, dpre, 'tn')
    dcat = matmul("out_ab_dx", dpre, w['w_out_ab'], 'nt')
    dgp, da_s, dbglu = glu_gate_bwd("glu_gate_b", dcat[:, :S5_WIDTH], y, gp)
    g['s5_w_glu'] = matmul("glu_dw", y, dgp, 'tn')
    g['s5_b_glu'] = dbglu
    dy5 = matmul("glu_dx", dgp, w['s5_w_glu'], 'nt', add=da_s)
    dys5 = gelu_bwd("gelu_b", dy5, ys5)
    for n in ('mlp_w_up', 'mlp_w_down'):
        g[n] = [gm0[n], gm1[n]]
    g['ln_mix_g'] = jnp.concatenate([dg0, dg1], axis=0)
    g['ln_mix_b'] = jnp.concatenate([db0, db1], axis=0)
    for n in ('ln_mlp_g', 'ln_mlp_b', 'mlp_b_down', 'mlp_b_up'):
        g[n] = jnp.concatenate([gm0[n], gm1[n]], axis=0)
    res = s5_bwd(_to_groups(dys5), ug, sst, *mats[:3], *s5_pows, nseq, ex=hooks.fold_early(g) if hooks else None)
    dug, da1, da2, da3, dare, daim = res[:6]
    for n, v in zip(s5_names, mats_vjp((da1, da2, da3, dare, daim))):
        g[n] = v[None]
    ex = hooks.scatter_early(g, loss, res[6:]) if hooks else None
    res = attn_bwd(proj, sb_tot, sb_first, dcat[:, S5_WIDTH:], nseq, lp, ex=ex)
    dqa, dka, dva = res[:3]
    if hooks:
        hooks.scattered(res[3:])
    dproj = jnp.concatenate([_from_groups(dug), dqa, dka, dva], axis=1)
    dh0 = matmul("in_ab_dx", dproj, w['w_in_ab'], 'nt', add=dpre, add_scale=ALPHA)
    g['w_in_ab'] = matmul("in_ab_dw", h0, dproj, 'tn', out_slots=N_CHIPS)

    dh0 = dh0.reshape(nseq, lp, D_MODEL)
    grad_x = dh0[:, N_META:N_META + seq]
    g['meta'] = jnp.sum(dh0[:, :N_META], axis=0)
    return loss, grad_x, g


class Exchange:
    def __init__(self, ins, out_shapes, n_remote, build, in_place):
        self.ins, self.out_shapes, self.n_remote, self.build, self.in_place = ins, out_shapes, n_remote, build, in_place

    def sems(self):
        return [pltpu.SemaphoreType.DMA((self.n_remote,)), pltpu.SemaphoreType.DMA((self.n_remote,))]

    def aliases(self):
        return self.in_place if isinstance(self.in_place, dict) else \
            {i: i for i in range(len(self.ins) if self.in_place else 0)}

    def beside(self, other):
        na, nao = len(self.ins), len(self.out_shapes)

        def build(in_refs, out_refs, me):
            pa = self.build(in_refs[:na], out_refs[:nao], me)
            pb = other.build(in_refs[na:], out_refs[nao:], me)
            n = max(len(pa), len(pb))
            return [(pa[i] if i < len(pa) else []) + (pb[i] if i < len(pb) else []) for i in range(n)]
        al = dict(self.aliases())
        al.update({na + i: nao + j for i, j in other.aliases().items()})
        return Exchange(list(self.ins) + list(other.ins), list(self.out_shapes) + list(other.out_shapes),
                        self.n_remote + other.n_remote, build, al)

    def phases(self, in_refs, out_refs, ssem, rsem):
        me = (lax.axis_index("x"), lax.axis_index("y"), lax.axis_index("c"))
        out, k = [], 0
        for phase in self.build(in_refs, out_refs, me):
            out.append(functools.partial(
                lambda phase, k: [pltpu.make_async_remote_copy(src_ref=s, dst_ref=d, send_sem=ssem.at[k + i],
                                                               recv_sem=rsem.at[k + i], device_id=p,
                                                               device_id_type=MESH)
                                  for i, (s, d, p) in enumerate(phase)], phase, k))
            k += len(phase)
        return out

    def start(self, in_refs, out_refs, ssem, rsem):
        for cp in self.phases(in_refs, out_refs, ssem, rsem)[0]():
            cp.start()

    def finish(self, in_refs, out_refs, ssem, rsem):
        phases = self.phases(in_refs, out_refs, ssem, rsem)
        for cp in phases[0]():
            cp.wait()
        for make in phases[1:]:
            copies = make()
            for cp in copies:
                cp.start()
            for cp in copies:
                cp.wait()


def _exchange(name, ex):
    ni, no = len(ex.ins), len(ex.out_shapes)

    def body(*refs):
        in_refs, out_refs, sems = refs[:ni], refs[ni:ni + no], refs[ni + no:]
        ex.start(in_refs, out_refs, *sems)
        ex.finish(in_refs, out_refs, *sems)

    anyspec = pl.BlockSpec(memory_space=pl.ANY)
    return pl.pallas_call(
        body, name=name, in_specs=[anyspec] * ni, out_specs=[anyspec] * no, out_shape=ex.out_shapes,
        input_output_aliases=ex.aliases(), scratch_shapes=ex.sems())(*ex.ins)


def _chip_peers(me):
    x, y, c = me
    return [(x ^ (m >> 1), y ^ (m & 1), c) for m in (1, 2, 3)]


def _half(ref, c, axis):
    h = ref.shape[axis] // 2
    idx = [slice(None)] * axis + [pl.ds(c * h, h)]
    return ref.at[tuple(idx)]


def _like(arrs):
    return [jax.ShapeDtypeStruct(a.shape, a.dtype) for a in arrs]


def gather_over_chips(bufs):
    def build(in_refs, out_refs, me):
        x, y, c = me
        j = 2 * x + y
        sib = (x, y, 1 - c)
        ici = [(_half(o.at[j], c, 0), _half(o.at[j], c, 0), p) for o in out_refs for p in _chip_peers(me)]
        d2d = [(_half(o.at[2 * p[0] + p[1]], c, 0), _half(o.at[2 * p[0] + p[1]], c, 0), sib)
               for o in out_refs for p in _chip_peers(me)]
        return [ici, d2d]
    return Exchange(bufs, _like(bufs), 6 * len(bufs), build, True)


def fold_cores(fulls):
    def build(in_refs, out_refs, me):
        x, y, c = me
        return [[(_half(s, 1 - c, 1), o, (x, y, 1 - c)) for s, o in zip(in_refs, out_refs)]]
    shapes = [jax.ShapeDtypeStruct((f.shape[0], f.shape[1] // 2, f.shape[2]), f.dtype) for f in fulls]
    return Exchange(fulls, shapes, len(fulls), build, False)


def scatter_over_chips(parts):
    def build(in_refs, out_refs, me):
        j = 2 * me[0] + me[1]
        return [[(s.at[2 * p[0] + p[1]], o.at[j], p) for s, o in zip(in_refs, out_refs) for p in _chip_peers(me)]]
    return Exchange(parts, _like(parts), 3 * len(parts), build, False)


def join_cores(bufs):
    def build(in_refs, out_refs, me):
        x, y, c = me
        return [[(o.at[c], o.at[c], (x, y, 1 - c)) for o in out_refs]]
    return Exchange(bufs, _like(bufs), len(bufs), build, True)


def gather_over_devices(buf):
    def build(in_refs, out_refs, me):
        x, y, c = me
        out = out_refs[0]
        sib = (x, y, 1 - c)
        mine = out.at[4 * x + 2 * y + c]
        first = [(mine, mine, sib)] + [(mine, mine, p) for p in _chip_peers(me)]
        second = [(out.at[4 * p[0] + 2 * p[1] + c], out.at[4 * p[0] + 2 * p[1] + c], sib) for p in _chip_peers(me)]
        return [first, second]
    return Exchange([buf], _like([buf]), N_DEV - 1, build, True)


def _slot_call(name, body, scalars, ins, in_maps, out_shape, out_map, blk, grid):
    gs = pltpu.PrefetchScalarGridSpec(
        num_scalar_prefetch=1, grid=grid,
        in_specs=[pl.BlockSpec((None,) + blk, m) for m in in_maps],
        out_specs=pl.BlockSpec((None,) + blk, out_map))
    return pl.pallas_call(body, name=name, grid_spec=gs, out_shape=out_shape,
                          compiler_params=_params(("arbitrary",) * len(grid)))(scalars, *ins)


def cast_into_slot(name, w2d, chip, dtype):
    r, wd = w2d.shape
    tm = _pick(r, (512, 256, 128, 64, 32, 16))

    def body(s_ref, w_ref, o_ref):
        o_ref[...] = w_ref[...].astype(o_ref.dtype)

    gs = pltpu.PrefetchScalarGridSpec(
        num_scalar_prefetch=1, grid=(r // tm,),
        in_specs=[pl.BlockSpec((tm, wd), lambda i, s: (i, 0))],
        out_specs=pl.BlockSpec((None, tm, wd), lambda i, s: (s[0], i, 0)))
    return pl.pallas_call(body, name=name, grid_spec=gs,
                          out_shape=jax.ShapeDtypeStruct((N_CHIPS, r, wd), dtype),
                          compiler_params=_params(("arbitrary",)))(chip.reshape(1), w2d)


def sum_cores(name, full, theirs, core):
    s, r, wd = full.shape
    h = r // 2
    tm = _pick(h, (512, 256, 128, 64, 32, 16))
    nt = h // tm

    def body(s_ref, a_ref, b_ref, o_ref):
        o_ref[...] = (a_ref[...] + b_ref[...]).astype(o_ref.dtype)

    return _slot_call(name, body, core.reshape(1), [full, theirs],
                      [lambda k, i, s_: (k, s_[0] * nt + i, 0), lambda k, i, s_: (k, i, 0)],
                      jax.ShapeDtypeStruct((s, h, wd), BF16), lambda k, i, s_: (k, i, 0), (tm, wd), (s, nt))


def sum_chips(name, own, recv, chip, core):
    s, r, wd = own.shape
    tm = _pick(r, (512, 256, 128, 64, 32, 16))

    def body(s_ref, a_ref, b1_ref, b2_ref, b3_ref, o_ref):
        acc = a_ref[...].astype(F32)
        for ref in (b1_ref, b2_ref, b3_ref):
            acc = acc + ref[...].astype(F32)
        o_ref[...] = acc

    maps = [lambda i, s_: (s_[0], i, 0)]
    maps += [functools.partial(lambda i, s_, m: (s_[0] ^ m, i, 0), m=m) for m in (1, 2, 3)]
    return _slot_call(name, body, jnp.stack([chip, core]), [own, recv, recv, recv], maps,
                      jax.ShapeDtypeStruct((2, r, wd), F32), lambda i, s_: (s_[1], i, 0), (tm, wd), (r // tm,))


def sum_slots(name, arr):
    s, r, wd = arr.shape
    tm = _pick(r, (512, 256, 128, 64, 32, 16, 8))

    def body(*refs):
        acc = refs[0][...].astype(F32)
        for ref in refs[1:s]:
            acc = acc + ref[...].astype(F32)
        refs[s][...] = acc

    specs = [pl.BlockSpec((None, tm, wd), functools.partial(lambda i, k: (k, i, 0), k=k)) for k in range(s)]
    return pl.pallas_call(body, name=name, grid=(r // tm,), in_specs=specs,
                          out_specs=pl.BlockSpec((tm, wd), lambda i: (i, 0)),
                          out_shape=jax.ShapeDtypeStruct((r, wd), F32),
                          compiler_params=_params(("parallel",)))(*([arr] * s))


def _pack(arrs):
    flat = jnp.concatenate([a.reshape(-1) for a in arrs])
    n = flat.shape[0]
    padded = -(-n // (512 * LANES)) * (512 * LANES)
    return jnp.pad(flat, (0, padded - n)).reshape(-1, LANES)


def _unpack(packed, shapes):
    flat = packed.reshape(-1)
    out, pos = [], 0
    for s in shapes:
        n = math.prod(s)
        out.append(flat[pos:pos + n].reshape(s))
        pos += n
    return out


def _as2d(a):
    return a.reshape(-1, a.shape[-1])


def kernel(x, meta, w_in_ab, s5_lam_re, s5_lam_im, s5_log_dt, s5_b_re, s5_b_im, s5_c_re, s5_c_im, s5_d, s5_w_glu, s5_b_glu, w_out_ab, w_in_c, hgrn_gamma, hgrn_norm_g, w_out_c, ln_mix_g, ln_mix_b, mlp_w_up, mlp_b_up, mlp_w_down, mlp_b_down, ln_mlp_g, ln_mlp_b, loss_target, m_meta, m_w_in_ab, m_s5_lam_re, m_s5_lam_im, m_s5_log_dt, m_s5_b_re, m_s5_b_im, m_s5_c_re, m_s5_c_im, m_s5_d, m_s5_w_glu, m_s5_b_glu, m_w_out_ab, m_w_in_c, m_hgrn_gamma, m_hgrn_norm_g, m_w_out_c, m_ln_mix_g, m_ln_mix_b, m_mlp_w_up, m_mlp_b_up, m_mlp_w_down, m_mlp_b_down, m_ln_mlp_g, m_ln_mlp_b, v_meta, v_w_in_ab, v_s5_lam_re, v_s5_lam_im, v_s5_log_dt, v_s5_b_re, v_s5_b_im, v_s5_c_re, v_s5_c_im, v_s5_d, v_s5_w_glu, v_s5_b_glu, v_w_out_ab, v_w_in_c, v_hgrn_gamma, v_hgrn_norm_g, v_w_out_c, v_ln_mix_g, v_ln_mix_b, v_mlp_w_up, v_mlp_b_up, v_mlp_w_down, v_mlp_b_down, v_ln_mlp_g, v_ln_mlp_b):
    given = dict(locals())
    wts = {n: given[n] for n in WEIGHTS}
    ms = {n: given['m_' + n] for n in WEIGHTS}
    vs = {n: given['v_' + n] for n in WEIGHTS}
    chip = 2 * lax.axis_index("x") + lax.axis_index("y")

    core = lax.axis_index("c")
    parts = [(n, l) for n in BIG for l in (range(DEPTH) if n.startswith('mlp_') else [0])]
    tags = [f"{n}{l}" for n, l in parts]
    shards = {p: cast_into_slot("cast_" + t, wts[p[0]][p[1]], chip, BF16) for t, p in zip(tags, parts)}
    small_sh = jnp.concatenate([meta, hgrn_norm_g, jnp.zeros((15, meta.shape[1]), F32)], axis=0)
    first, late = parts[:1], parts[1:]
    w_in_ab_all, sm = _exchange("gather_first", gather_over_chips(
        [shards[p] for p in first] + [cast_into_slot("cast_small", small_sh, chip, F32)]))
    w = {n: wts[n] for n in SMALL}
    w['meta'] = sm[:, :N_META].transpose(1, 0, 2).reshape(N_META, D_MODEL)
    w['hgrn_norm_g'] = sm[:, N_META:N_META + 1].transpose(1, 0, 2).reshape(1, D_MODEL)
    w['w_in_ab'] = w_in_ab_all

    def slot_major(v):
        return v if v.ndim == 3 else v.reshape(N_CHIPS, -1, v.shape[-1])

    def whole_grads(ps, g):
        return [slot_major(g[n][l] if n.startswith('mlp_') else g[n]) for n, l in ps]

    def chip_sums_of(ps, gfull, theirs):
        return [sum_cores(f"sum_cores_{n}{l}", a, b, core) for (n, l), a, b in zip(ps, gfull, theirs)]

    def all_devices(arrs):
        packed = _pack(arrs)
        return gather_over_devices(lax.dynamic_update_slice(
            jnp.zeros((N_DEV,) + packed.shape, F32), packed[None], (2 * chip + core, 0, 0)))

    early_small = [n for n in SMALL if n != 'meta']
    last = [('w_out_c', 0), ('mlp_w_up', 1), ('mlp_w_down', 1)]
    mid = [p for p in late if p not in last]

    class Hooks:
        gather_late = gather_over_chips([shards[p] for p in mid])
        gather_last = gather_over_chips([shards[p] for p in last])

        @staticmethod
        def late_weights(filled):
            fw = Hooks.fw = dict(zip(mid, filled))
            return {'s5_w_glu': fw['s5_w_glu', 0].reshape(S5_WIDTH, S5_WIDTH),
                    'w_out_ab': fw['w_out_ab', 0].reshape(D_MODEL, D_MODEL),
                    'w_in_c': fw['w_in_c', 0],
                    'mlp_w_up': [fw['mlp_w_up', 0]],
                    'mlp_w_down': [fw['mlp_w_down', 0].reshape(D_FF, D_MODEL)]}

        @staticmethod
        def last_weights(filled):
            fw = dict(zip(last, filled))
            return {'w_out_c': fw['w_out_c', 0].reshape(D_MODEL, D_MODEL),
                    'mlp_w_up': [Hooks.fw['mlp_w_up', 0], fw['mlp_w_up', 1]],
                    'mlp_w_down': [Hooks.fw['mlp_w_down', 0].reshape(D_FF, D_MODEL),
                                   fw['mlp_w_down', 1].reshape(D_FF, D_MODEL)]}

        @staticmethod
        def fold_early(g):
            Hooks.whole = whole_grads(late, g)
            return fold_cores(Hooks.whole)

        @staticmethod
        def scatter_early(g, loss, theirs):
            Hooks.sums = chip_sums_of(late, Hooks.whole, theirs)
            Hooks.small_shapes = [(LANES,)] + [g[n].shape for n in early_small]
            return scatter_over_chips(Hooks.sums).beside(
                all_devices([loss.reshape(-1)] + [g[n] for n in early_small]))

        @staticmethod
        def scattered(outs):
            Hooks.recv, Hooks.small = list(outs[:-1]), outs[-1]

    loss, grad_x, g = local_step(x, loss_target, w, Hooks)

    whole = whole_grads(first, g)
    sums = chip_sums_of(first, whole, _exchange("fold_last", fold_cores(whole)))
    *recv, meta_slots = _exchange("scatter_last", scatter_over_chips(sums).beside(all_devices([g['meta']])))
    reduced = _exchange("join_grads", join_cores(
        [sum_chips(f"sum_chips_{n}{l}", a, b, chip, core)
         for (n, l), a, b in zip(first + late, sums + Hooks.sums, list(recv) + Hooks.recv)]))
    reduced = dict(zip(first + late, [_as2d(r) for r in reduced]))
    red = _unpack(sum_slots("sum_small", Hooks.small), Hooks.small_shapes)
    loss_out = red[0][0]
    gs = dict(zip(early_small, red[1:]))
    gs['meta'] = _unpack(sum_slots("sum_meta", meta_slots), [g['meta'].shape])[0]
    gs['meta'] = lax.dynamic_slice_in_dim(gs['meta'], chip * meta.shape[1], meta.shape[1], axis=1)
    gs['hgrn_norm_g'] = lax.dynamic_slice_in_dim(gs['hgrn_norm_g'].reshape(1, -1), chip * meta.shape[1],
                                                 meta.shape[1], axis=1)

    grads, deltas, new_m, new_v = {}, {}, {}, {}
    for n in BIG:
        r = jnp.concatenate([reduced[n, l] for l in range(wts[n].shape[0])], axis=0)
        res = adamw("adamw_" + n, _as2d(wts[n]), [r], _as2d(ms[n]), _as2d(vs[n]))
        grads[n], deltas[n], new_m[n], new_v[n] = [r.reshape(wts[n].shape) for r in res]
    shapes = [wts[n].shape for n in SMALL]
    res = adamw("adamw_small", _pack([wts[n] for n in SMALL]), [_pack([gs[n] for n in SMALL])],
                _pack([ms[n] for n in SMALL]), _pack([vs[n] for n in SMALL]))
    for d, r in zip((grads, deltas, new_m, new_v), res):
        d.update(zip(SMALL, _unpack(r, shapes)))
    return (loss_out, grad_x, *[grads[n] for n in WEIGHTS], *[deltas[n] for n in WEIGHTS],
            *[new_m[n] for n in WEIGHTS], *[new_v[n] for n in WEIGHTS])
```

```python
import functools
import math

import jax
import jax.numpy as jnp
from jax import lax
from jax.experimental import pallas as pl
from jax.experimental.pallas import tpu as pltpu

F32 = jnp.float32
BF16 = jnp.bfloat16

D_MODEL = 1024
N_META = 16
DEPTH = 2
S5_WIDTH = 512
S5_GROUP = 16
S5_GROUPS = 32
S5_STATE = 64
S5_CHUNK = 16
SB_WIDTH = 512
SB_BLOCK = 128
SB_DEAD = -110.0
HG_HEADS = 8
HG_DK = 128
HG_CHUNK = 64
HG_UNROLL = 2
D_FF = 4096
ALPHA = (2.0 * DEPTH) ** 0.25
LN_EPS = 1e-5
RMS_EPS = 1e-6
ADAM_LR = 0.001
ADAM_B1 = 0.9
ADAM_B2 = 0.999
ADAM_EPS = 1e-08
ADAM_WD = 0.01
ADAM_STEP = 10
N_CHIPS = 4
N_DEV = 8
LANES = 128
MESH = pl.DeviceIdType.MESH
VMEM_LIMIT = 56 * 1024 * 1024
WHOLE_K_BYTES = 42 * 1024 * 1024
HGRN_BWD_VMEM = 60 * 1024 * 1024

WEIGHTS = ['meta', 'w_in_ab', 's5_lam_re', 's5_lam_im', 's5_log_dt', 's5_b_re', 's5_b_im', 's5_c_re', 's5_c_im',
           's5_d', 's5_w_glu', 's5_b_glu', 'w_out_ab', 'w_in_c', 'hgrn_gamma', 'hgrn_norm_g', 'w_out_c',
           'ln_mix_g', 'ln_mix_b', 'mlp_w_up', 'mlp_b_up', 'mlp_w_down', 'mlp_b_down', 'ln_mlp_g', 'ln_mlp_b']
BIG = ['w_in_ab', 's5_w_glu', 'w_out_ab', 'w_in_c', 'w_out_c', 'mlp_w_up', 'mlp_w_down']
SMALL = [n for n in WEIGHTS if n not in BIG]


def _params(sem=None):
    return pltpu.CompilerParams(dimension_semantics=sem, vmem_limit_bytes=VMEM_LIMIT)


def _pick(n, cands):
    for c in cands:
        if n % c == 0:
            return c
    return n


def _dot(a, b, ca, cb):
    return lax.dot_general(a, b, (((ca,), (cb,)), ((), ())), preferred_element_type=F32)


def _split3(x):
    hi = x.astype(BF16)
    r = x - hi.astype(F32)
    mid = r.astype(BF16)
    lo = (r - mid.astype(F32)).astype(BF16)
    return hi, mid, lo


def _dot_exact_rhs(x, m, cx, cm):
    hi = x.astype(BF16)
    lo = (x - hi.astype(F32)).astype(BF16)
    return _dot(hi, m, cx, cm) + _dot(lo, m, cx, cm)


def _dot3(a, b, ca, cb):
    ah = a.astype(BF16)
    al = (a - ah.astype(F32)).astype(BF16)
    bh = b.astype(BF16)
    bl = (b - bh.astype(F32)).astype(BF16)
    return _dot(ah, bh, ca, cb) + (_dot(ah, bl, ca, cb) + _dot(al, bh, ca, cb))


def rowwise(name, fn, row_ins, bc_ins, out_widths, sum_widths=(), out_dtypes=None, tm=None):
    t = row_ins[0].shape[0]
    if tm is None:
        wmax = max([a.shape[1] for a in row_ins] + list(out_widths))
        tm = _pick(t, (272, 256, 128, 64, 16, 8)) if wmax > 1024 else _pick(t, (544, 512, 256, 128, 64, 16, 8))
    nr, nb, no, ns = len(row_ins), len(bc_ins), len(out_widths), len(sum_widths)
    out_dtypes = out_dtypes or [F32] * no

    def body(*refs):
        i = pl.program_id(0)
        rows = [r[...] for r in refs[:nr]]
        bcs = [r[...] for r in refs[nr:nr + nb]]
        outs, sums = fn(i, tm, rows, bcs)
        for r, v in zip(refs[nr + nb:nr + nb + no], outs):
            r[...] = v.astype(r.dtype)
        if ns:
            srefs = refs[nr + nb + no:]

            @pl.when(i == 0)
            def _():
                for r in srefs:
                    r[...] = jnp.zeros(r.shape, r.dtype)

            for r, v in zip(srefs, sums):
                r[...] += v

    in_specs = [pl.BlockSpec((tm, a.shape[1]), lambda i: (i, 0)) for a in row_ins]
    in_specs += [pl.BlockSpec(a.shape, lambda i: (0, 0)) for a in bc_ins]
    out_specs = [pl.BlockSpec((tm, w), lambda i: (i, 0)) for w in out_widths]
    out_specs += [pl.BlockSpec((1, w), lambda i: (0, 0)) for w in sum_widths]
    out_shape = [jax.ShapeDtypeStruct((t, w), dt) for w, dt in zip(out_widths, out_dtypes)]
    out_shape += [jax.ShapeDtypeStruct((1, w), F32) for w in sum_widths]
    res = pl.pallas_call(body, name=name, grid=(t // tm,), in_specs=in_specs, out_specs=out_specs,
                         out_shape=out_shape, compiler_params=_params(("arbitrary",)))(*row_ins, *bc_ins)
    return res


def _colsum(v):
    return jnp.sum(v, axis=0, keepdims=True)


def matmul(name, a, b, mode, *, bias=None, act=None, add=None, add_scale=1.0, relu2_of=None, colsum=False,
           out_dtype=F32, out_slots=0, tm=None, tn=None, tk=None):
    slotted = b.ndim == 3
    if mode == 'nn':
        m, k = a.shape
        n = b.shape[-1] * (b.shape[0] if slotted else 1)
    elif mode == 'nt':
        m, k = a.shape
        n = b.shape[-2]
    else:
        k, m = a.shape
        n = b.shape[-1]
    ns = b.shape[-1] if slotted else None
    if mode == 'tn':
        tm = tm or _pick(m, (512, 256, 128))
        nw = n if not out_slots else n // out_slots
        if tn is None and tk is None:
            for cand in (512, 256):
                if nw % cand == 0 and (2 * k * (tm * a.dtype.itemsize + cand * b.dtype.itemsize)
                                       + 2 * tm * cand * 4 <= WHOLE_K_BYTES):
                    tn, tk = cand, k
                    break
        tn = tn or _pick(nw, (1024, 512, 256, 128))
        tk = tk or _pick(k, (1088, 1024, 768, 512, 384, 256, 128))
    else:
        tm = tm or _pick(m, (1088, 1024, 768, 512, 384, 256, 128))
        tn = tn or _pick(n if not (slotted and mode == 'nn') else ns, (1024, 512, 256, 128))
        extra = sum(x is not None for x in (add, relu2_of)) * 4
        if tk is None and not slotted and (2 * (tm * k * a.dtype.itemsize + k * tn * b.dtype.itemsize)
                                           + 2 * tm * tn * (4 + extra) <= WHOLE_K_BYTES):
            tk = k
        tk = tk or _pick(k if not (slotted and mode == 'nt') else ns, (1024, 512, 256, 128))
    gm, gn, gk = m // tm, n // tn, k // tk

    if mode == 'nn':
        a_spec = pl.BlockSpec((tm, tk), lambda i, j, kk: (i, kk))
        if slotted:
            per = ns // tn
            b_spec = pl.BlockSpec((None, tk, tn), lambda i, j, kk: (j // per, kk, j % per))
        else:
            b_spec = pl.BlockSpec((tk, tn), lambda i, j, kk: (kk, j))
        ca, cb = 1, 0
    elif mode == 'nt':
        a_spec = pl.BlockSpec((tm, tk), lambda i, j, kk: (i, kk))
        if slotted:
            per = ns // tk
            b_spec = pl.BlockSpec((None, tn, tk), lambda i, j, kk: (kk // per, j, kk % per))
        else:
            b_spec = pl.BlockSpec((tn, tk), lambda i, j, kk: (j, kk))
        ca, cb = 1, 1
    else:
        a_spec = pl.BlockSpec((tk, tm), lambda i, j, kk: (kk, i))
        b_spec = pl.BlockSpec((tk, tn), lambda i, j, kk: (kk, j))
        ca, cb = 0, 0
    in_specs = [a_spec, b_spec]
    args = [a, b]
    if bias is not None:
        in_specs.append(pl.BlockSpec((1, tn), lambda i, j, kk: (0, j)))
        args.append(bias)
    if add is not None:
        in_specs.append(pl.BlockSpec((tm, tn), lambda i, j, kk: (i, j)))
        args.append(add)
    if relu2_of is not None:
        in_specs.append(pl.BlockSpec((tm, tn), lambda i, j, kk: (i, j)))
        args.append(relu2_of)
    if out_slots:
        per_o = (n // out_slots) // tn
        out_spec = pl.BlockSpec((None, tm, tn), lambda i, j, kk: (j // per_o, i, j % per_o))
        out_shape = jax.ShapeDtypeStruct((out_slots, m, n // out_slots), out_dtype)
    else:
        out_spec = pl.BlockSpec((tm, tn), lambda i, j, kk: (i, j))
        out_shape = jax.ShapeDtypeStruct((m, n), out_dtype)
    grid = (gm, gn, gk)
    if colsum:
        assert mode == 'tn'
        out_spec = [out_spec, pl.BlockSpec((1, tn), lambda i, j, kk: (0, j))]
        out_shape = [out_shape, jax.ShapeDtypeStruct((1, n), F32)]

        def swapped(spec):
            return pl.BlockSpec(spec.block_shape, functools.partial(lambda j, i, kk, f: f(i, j, kk), f=spec.index_map))
        in_specs = [swapped(sp) for sp in in_specs]
        out_spec = [swapped(sp) for sp in out_spec]
        grid = (gn, gm, gk)

    def body(*refs):
        a_ref, b_ref = refs[0], refs[1]
        pos = 2
        bias_ref = add_ref = hid_ref = cs_ref = None
        if bias is not None:
            bias_ref = refs[pos]
            pos += 1
        if add is not None:
            add_ref = refs[pos]
            pos += 1
        if relu2_of is not None:
            hid_ref = refs[pos]
            pos += 1
        o_ref = refs[pos]
        pos += 1
        if colsum:
            cs_ref = refs[pos]
            pos += 1
        acc_ref = refs[pos] if gk > 1 else None
        bt = b_ref[...]
        p = _dot(a_ref[...].astype(BF16), bt.astype(BF16), ca, cb)
        kk = pl.program_id(2)

        if colsum:
            @pl.when(jnp.logical_and(pl.program_id(1) == 0, kk == 0))
            def _():
                cs_ref[...] = _colsum(bt.astype(F32))

            @pl.when(jnp.logical_and(pl.program_id(1) == 0, kk > 0))
            def _():
                cs_ref[...] += _colsum(bt.astype(F32))

        def finish(r):
            if bias_ref is not None:
                r = r + bias_ref[...]
            if act == 'relu2':
                r = jnp.square(jnp.maximum(r, 0.0))
            if hid_ref is not None:
                r = r * (2.0 * jnp.sqrt(hid_ref[...].astype(F32)))
            if add_ref is not None:
                r = r + add_scale * add_ref[...]
            o_ref[...] = r.astype(o_ref.dtype)

        if gk == 1:
            finish(p)
        else:
            @pl.when(kk == 0)
            def _():
                acc_ref[...] = p

            @pl.when(kk > 0)
            def _():
                acc_ref[...] += p

            @pl.when(kk == gk - 1)
            def _():
                finish(acc_ref[...])

    scratch = [pltpu.VMEM((tm, tn), F32)] if gk > 1 else []
    sem = ("arbitrary",) * 3 if colsum else ("parallel", "parallel", "arbitrary")
    return pl.pallas_call(body, name=name, grid=grid, in_specs=in_specs, out_specs=out_spec,
                          out_shape=out_shape, scratch_shapes=scratch, compiler_params=_params(sem))(*args)


def ln_fwd(name, h, mix, g, b):
    def fn(i, tm, rows, bcs):
        pre = ALPHA * rows[0] + rows[1]
        mu = jnp.mean(pre, axis=1, keepdims=True)
        xc = pre - mu
        var = jnp.mean(xc * xc, axis=1, keepdims=True)
        rstd = lax.rsqrt(var + LN_EPS)
        xhat = xc * rstd
        return (xhat * bcs[0] + bcs[1], xhat, rstd), ()
    return rowwise(name, fn, [h, mix], [g, b], [D_MODEL, D_MODEL, 1])


def ln_bwd(name, dy, xhat, rstd, g):
    def fn(i, tm, rows, bcs):
        dy_, xh, rs = rows
        dyg = dy_ * bcs[0]
        m1 = jnp.mean(dyg, axis=1, keepdims=True)
        m2 = jnp.mean(dyg * xh, axis=1, keepdims=True)
        dpre = rs * (dyg - m1 - xh * m2)
        return (dpre,), (_colsum(dy_ * xh), _colsum(dy_), _colsum(dpre))
    return rowwise(name, fn, [dy, xhat, rstd], [g], [D_MODEL], [D_MODEL] * 3)


_GELU_C = math.sqrt(2.0 / math.pi)


def gelu_fwd(name, x):
    def fn(i, tm, rows, bcs):
        v = rows[0]
        u = _GELU_C * (v + 0.044715 * (v * v * v))
        return (0.5 * v * (1.0 + jnp.tanh(u)),), ()
    return rowwise(name, fn, [x], [], [x.shape[1]])[0]


def gelu_bwd(name, dy, x):
    def fn(i, tm, rows, bcs):
        v = rows[1]
        th = jnp.tanh(_GELU_C * (v + 0.044715 * (v * v * v)))
        dg = 0.5 * (1.0 + th) + 0.5 * v * (1.0 - th * th) * (_GELU_C * (1.0 + 3.0 * 0.044715 * v * v))
        return (rows[0] * dg,), ()
    return rowwise(name, fn, [dy, x], [], [x.shape[1]])[0]


def glu_gate(name, y, gp):
    def fn(i, tm, rows, bcs):
        return (rows[0] * jax.nn.sigmoid(rows[1]),), ()
    return rowwise(name, fn, [y, gp], [], [y.shape[1]], out_dtypes=[BF16])[0]


def glu_gate_bwd(name, da, y, gp):
    def fn(i, tm, rows, bcs):
        s = jax.nn.sigmoid(rows[2])
        dgp = rows[0] * rows[1] * s * (1.0 - s)
        return (dgp, rows[0] * s), (_colsum(dgp),)
    w = y.shape[1]
    return rowwise(name, fn, [da, y, gp], [], [w, w], [w])


def loss_head(name, h, tgt, lp, seq):
    def fn(i, tm, rows, bcs):
        pos = (i * tm + lax.broadcasted_iota(jnp.int32, (tm, 1), 0)) % lp
        valid = jnp.logical_and(pos >= N_META, pos < N_META + seq)
        err = jnp.where(valid, rows[0] - rows[1], 0.0)
        part = 0.5 * jnp.sum(jnp.mean(err * err, axis=1, keepdims=True), axis=0, keepdims=True)
        return (err * (1.0 / D_MODEL),), (jnp.broadcast_to(part, (1, LANES)),)
    return rowwise(name, fn, [h, tgt], [], [D_MODEL], [LANES])


def adamw(name, w, gs, m, v):
    ng = len(gs)

    def fn(i, tm, rows, bcs):
        w_ = rows[0]
        g = rows[1] if ng == 1 else rows[1] + rows[2]
        m_, v_ = rows[1 + ng], rows[2 + ng]
        m_ = ADAM_B1 * m_ + (1.0 - ADAM_B1) * g
        v_ = ADAM_B2 * v_ + (1.0 - ADAM_B2) * jnp.square(g)
        m_hat = m_ / (1.0 - ADAM_B1 ** ADAM_STEP)
        v_hat = v_ / (1.0 - ADAM_B2 ** ADAM_STEP)
        delta = -ADAM_LR * (m_hat / (jnp.sqrt(v_hat) + ADAM_EPS) + ADAM_WD * w_)
        return (g, delta, m_, v_), ()
    wd = w.shape[1]
    return rowwise(name, fn, [w] + list(gs) + [m, v], [], [wd] * 4)


def s5_matrices(lam_re, lam_im, log_dt, b_re, b_im, c_re, c_im, d):
    c, hh, gg, pp = S5_CHUNK, S5_GROUP, S5_GROUPS, S5_STATE
    dt = jnp.exp(log_dt)[:, None]
    tau = jnp.arange(c + 1, dtype=F32)[None, :, None]
    mag = jnp.exp(tau * (lam_re * dt)[:, None, :])
    ang = tau * (lam_im * dt)[:, None, :]
    pw_re, pw_im = mag * jnp.cos(ang), mag * jnp.sin(ang)
    nr, ni = pw_re[:, 1] - 1.0, pw_im[:, 1]
    den = lam_re * lam_re + lam_im * lam_im
    cf_re = (nr * lam_re + ni * lam_im) / den
    cf_im = (ni * lam_re - nr * lam_im) / den
    bb_re = cf_re[:, :, None] * b_re - cf_im[:, :, None] * b_im
    bb_im = cf_re[:, :, None] * b_im + cf_im[:, :, None] * b_re
    ct_re, ct_im = c_re.transpose(0, 2, 1)[:, :, None, :], c_im.transpose(0, 2, 1)[:, :, None, :]
    pt_re, pt_im = pw_re.transpose(0, 2, 1)[:, :, :, None], pw_im.transpose(0, 2, 1)[:, :, :, None]
    cp_re = ct_re * pt_re - ct_im * pt_im
    cp_im = ct_re * pt_im + ct_im * pt_re
    hp = lax.Precision.HIGHEST
    kmat = (jnp.einsum('gpk,gpn->gkn', bb_re, cp_re[:, :, :c].reshape(gg, pp, c * hh), precision=hp)
            - jnp.einsum('gpk,gpn->gkn', bb_im, cp_im[:, :, :c].reshape(gg, pp, c * hh), precision=hp))
    rows = [jnp.pad(kmat[:, :, :(c - j) * hh], ((0, 0), (0, 0), (j * hh, 0))) for j in range(c)]
    a1 = jnp.stack(rows, axis=1).reshape(gg, c * hh, c * hh)
    a1 = a1 + jnp.eye(c * hh, dtype=F32)[None] * jnp.tile(d, (1, c))[:, None, :]
    a2 = jnp.concatenate([cp_re[:, :, 1:].reshape(gg, pp, c * hh), -cp_im[:, :, 1:].reshape(gg, pp, c * hh)], axis=1)
    rv_re, rv_im = pw_re[:, :c][:, ::-1, None, :], pw_im[:, :c][:, ::-1, None, :]
    bt_re, bt_im = bb_re.transpose(0, 2, 1)[:, None], bb_im.transpose(0, 2, 1)[:, None]
    a3 = jnp.concatenate([rv_re * bt_re - rv_im * bt_im, rv_re * bt_im + rv_im * bt_re], axis=-1)
    a3 = a3.reshape(gg, c * hh, 2 * pp)
    are = jnp.concatenate([pw_re[:, c], pw_re[:, c]], axis=-1)[:, None, :]
    aim = jnp.concatenate([-pw_im[:, c], pw_im[:, c]], axis=-1)[:, None, :]
    return a1, a2, a3, are, aim


S5_LEVELS = 8


def s5_scan_powers(lam_re, lam_im, log_dt):
    dt = jnp.exp(log_dt)[:, None]
    e = (S5_CHUNK * 2.0 ** jnp.arange(S5_LEVELS, dtype=F32))[:, None, None]
    mag = jnp.exp(e * (lam_re * dt)[None])
    ang = e * (lam_im * dt)[None]
    pr, pi = mag * jnp.cos(ang), mag * jnp.sin(ang)
    pre = jnp.concatenate([pr, pr], axis=-1).transpose(1, 0, 2)
    pim = jnp.concatenate([-pi, pi], axis=-1).transpose(1, 0, 2)
    return pre, pim


def _s5_scan(x, pre, pim, reverse):
    n = x.shape[0]
    rowi = lax.broadcasted_iota(jnp.int32, (n, 1), 0)
    t = x
    for k in range(S5_LEVELS):
        shift = 2 ** k
        if shift >= n:
            break
        p_re, p_im = pre[k:k + 1, :], pim[k:k + 1, :]
        if reverse:
            far = jnp.where(rowi < n - shift, pltpu.roll(t, n - shift, 0), 0.0)
            t = t + (p_re * far + pltpu.roll(p_im * far, S5_STATE, 1))
        else:
            far = jnp.where(rowi >= shift, pltpu.roll(t, shift, 0), 0.0)
            t = t + (p_re * far + p_im * pltpu.roll(far, S5_STATE, 1))
    return t


def s5_fwd(ug, a1, a2, a3, pre, pim, nseq):
    g, nc, cw = ug.shape
    per = nc // nseq
    sw = 2 * S5_STATE
    assert per <= 2 ** S5_LEVELS

    def body(u_ref, a1_ref, a2_ref, a3_ref, pre_ref, pim_ref, y_ref, s_ref):
        u = u_ref[...]
        x = _dot3(u, a3_ref[...], 1, 0)
        first = lax.broadcasted_iota(jnp.int32, (per, 1), 0) == 0
        for b in range(nseq):
            t = _s5_scan(x[b * per:(b + 1) * per], pre_ref[...], pim_ref[...], False)
            s_ref[pl.ds(b * per, per), :] = jnp.where(first, 0.0, pltpu.roll(t, 1, 0))
        y_ref[...] = _dot3(u, a1_ref[...], 1, 0) + _dot3(s_ref[...], a2_ref[...], 1, 0)

    def spec(shape):
        return pl.BlockSpec((None,) + shape, lambda i: (i, 0, 0))
    lv = (S5_LEVELS, sw)
    return pl.pallas_call(
        body, name="s5_fwd", grid=(g,),
        in_specs=[spec((nc, cw)), spec((cw, cw)), spec((sw, cw)), spec((cw, sw)), spec(lv), spec(lv)],
        out_specs=[spec((nc, cw)), spec((nc, sw))],
        out_shape=[jax.ShapeDtypeStruct((g, nc, cw), F32), jax.ShapeDtypeStruct((g, nc, sw), F32)],
        compiler_params=_params(("parallel",)))(ug, a1, a2, a3, pre, pim)


def s5_bwd(dyg, ug, sst, a1, a2, a3, pre, pim, nseq, ex=None):
    g, nc, cw = ug.shape
    per = nc // nseq
    sw = 2 * S5_STATE
    split, ex_start, ex_finish, exkw = _hidden_exchange(ex, 8, 6, (g,))

    def body(*refs):
        ((dy_ref, u_ref, s_ref, a1_ref, a2_ref, a3_ref, pre_ref, pim_ref),
         (du_ref, da1_ref, da2_ref, da3_ref, dare_ref, daim_ref), ex_parts, (dx_ref,)) = split(refs)
        ex_start(ex_parts)
        dy = dy_ref[...]
        u = u_ref[...]
        gd = _dot3(dy, a2_ref[...], 1, 1)
        s_all = s_ref[...]
        da2_ref[...] = _dot3(s_all, dy, 0, 0)
        last = lax.broadcasted_iota(jnp.int32, (per, 1), 0) == per - 1
        for b in range(nseq):
            gs = _s5_scan(gd[b * per:(b + 1) * per], pre_ref[...], pim_ref[...], True)
            dx_ref[pl.ds(b * per, per), :] = jnp.where(last, 0.0, pltpu.roll(gs, per - 1, 0))
        dx = dx_ref[...]
        dare_ref[...] = _colsum(dx * s_all)
        daim_ref[...] = _colsum(dx * pltpu.roll(s_all, S5_STATE, 1))
        du_ref[...] = _dot3(dy, a1_ref[...], 1, 1) + _dot3(dx, a3_ref[...], 1, 1)
        da1_ref[...] = _dot3(u, dy, 0, 0)
        da3_ref[...] = _dot3(u, dx, 0, 0)
        ex_finish(ex_parts)

    def spec(shape):
        return pl.BlockSpec((None,) + shape, lambda i: (i, 0, 0))
    sds = jax.ShapeDtypeStruct
    return pl.pallas_call(
        body, name="s5_bwd", grid=(g,),
        in_specs=[spec((nc, cw)), spec((nc, cw)), spec((nc, sw)), spec((cw, cw)), spec((sw, cw)), spec((cw, sw)),
                  spec((S5_LEVELS, sw)), spec((S5_LEVELS, sw))] + exkw['in_specs'],
        out_specs=[spec((nc, cw)), spec((cw, cw)), spec((sw, cw)), spec((cw, sw)), spec((1, sw)), spec((1, sw))]
        + exkw['out_specs'],
        out_shape=[sds((g, nc, cw), F32), sds((g, cw, cw), F32), sds((g, sw, cw), F32), sds((g, cw, sw), F32),
                   sds((g, 1, sw), F32), sds((g, 1, sw), F32)] + exkw['out_shape'],
        input_output_aliases=exkw['aliases'], scratch_shapes=[pltpu.VMEM((nc, sw), F32)] + exkw['scratch'],
        compiler_params=_params(("arbitrary",)))(dyg, ug, sst, a1, a2, a3, pre, pim, *exkw['ins'])


def _to_groups(u):
    t = u.shape[0]
    nc = t // S5_CHUNK
    return u.reshape(nc, S5_CHUNK, S5_GROUPS, S5_GROUP).transpose(2, 0, 1, 3).reshape(S5_GROUPS, nc, -1)


def _from_groups(yg):
    g, nc, _ = yg.shape
    return yg.reshape(g, nc, S5_CHUNK, S5_GROUP).transpose(1, 2, 0, 3).reshape(nc * S5_CHUNK, g * S5_GROUP)


def _sb_masks():
    row = lax.broadcasted_iota(jnp.int32, (SB_BLOCK, SB_BLOCK), 0)
    col = lax.broadcasted_iota(jnp.int32, (SB_BLOCK, SB_BLOCK), 1)
    return row, col


def _sb_weights(z, vis, later_of, after):
    sp = jnp.maximum(z, 0.0) + jnp.log1p(jnp.exp(-jnp.abs(z)))
    lk = jnp.where(vis, -sp, 0.0)
    rs = jnp.sum(lk, axis=1, keepdims=True)
    later = _dot_exact_rhs(lk, after, 1, 0) + later_of(rs)
    lb = z - sp
    w = jnp.where(vis, jnp.exp(lb + later), 0.0)
    return w, rs, lb


def _hidden_exchange(ex, n_in, n_out, grid):
    ni, no = (len(ex.ins), len(ex.out_shapes)) if ex else (0, 0)

    def split(refs):
        a, b = n_in + ni, n_in + ni + n_out
        end = len(refs) - (2 if ex else 0)
        return refs[:n_in], refs[a:b], (refs[n_in:a], refs[b:b + no], refs[end:]), refs[b + no:end]

    def at_step(which, fn, parts):
        if ex is not None:
            ids = [pl.program_id(d) == (0 if which == 'first' else g - 1) for d, g in enumerate(grid)]
            cond = ids[0]
            for c in ids[1:]:
                cond = jnp.logical_and(cond, c)
            pl.when(cond)(lambda: fn(parts[0], parts[1], *parts[2]))

    anyspec = pl.BlockSpec(memory_space=pl.ANY)
    kw = dict(in_specs=[anyspec] * ni, out_specs=[anyspec] * no, out_shape=list(ex.out_shapes) if ex else [],
              aliases={n_in + i: n_out + j for i, j in (ex.aliases().items() if ex else ())},
              scratch=ex.sems() if ex else [], ins=list(ex.ins) if ex else [])
    start = functools.partial(at_step, 'first', ex.start if ex else None)
    finish = functools.partial(at_step, 'last', ex.finish if ex else None)
    return split, start, finish, kw


SB_TILES = 2
SB_HEADS = 2 * SB_TILES
SB_WD = SB_TILES * LANES


def _stack_heads(x):
    return jnp.concatenate([x] * SB_HEADS, axis=0)


def _stack_masks():
    row = lax.broadcasted_iota(jnp.int32, (SB_HEADS * SB_BLOCK, SB_WD), 0)
    col = lax.broadcasted_iota(jnp.int32, (SB_HEADS * SB_BLOCK, SB_WD), 1)
    rowk = lax.broadcasted_iota(jnp.int32, (SB_HEADS * SB_BLOCK, SB_BLOCK), 0)
    colk = lax.broadcasted_iota(jnp.int32, (SB_HEADS * SB_BLOCK, SB_BLOCK), 1)
    return (col // 64) == (row // SB_BLOCK), colk < (rowk % SB_BLOCK)


def _own_lanes(r):
    head = lax.broadcasted_iota(jnp.int32, (SB_BLOCK, SB_WD), 1) // 64
    out = r[:SB_BLOCK]
    for h in range(1, SB_HEADS):
        out = jnp.where(head == h, r[h * SB_BLOCK:(h + 1) * SB_BLOCK], out)
    return out


def _sb_specs(lp, nseq):
    wd = SB_TILES * LANES
    off = [(S5_WIDTH + i * SB_WIDTH) // wd for i in range(3)]
    blk = (lp, wd)
    qkv = [pl.BlockSpec(blk, functools.partial(lambda b, h, o: (b, o + h), o=o)) for o in off]
    return (blk, qkv, pl.BlockSpec(blk, lambda b, h: (b, h)),
            pl.BlockSpec((None, None, 8, LANES), lambda b, h: (b, h, 0, 0)))


def attn_fwd(proj, nseq, lp, ex=None):
    nqb = lp // SB_BLOCK
    t = proj.shape[0]
    nstep = SB_WIDTH // (SB_TILES * LANES)
    split, ex_start, ex_finish, exkw = _hidden_exchange(ex, 3, 3, (nseq, nstep))

    def body(*refs):
        (q_ref, k_ref, v_ref), (o_ref, tot_ref, first_ref), ex_parts, _ = split(refs)
        ex_start(ex_parts)
        row, col = _sb_masks()
        after = (row > col).astype(BF16)
        lane8 = lax.broadcasted_iota(jnp.int32, (8, LANES), 1)
        hms, tris = _stack_masks()

        def qloop(qi, firsts):
            q0 = pl.multiple_of(qi * SB_BLOCK, SB_BLOCK)
            qm = jnp.where(hms, _stack_heads(q_ref[pl.ds(q0, SB_BLOCK), :] * 0.125), 0.0).astype(BF16)

            def alive(carry):
                return jnp.logical_and(carry[0] <= qi, carry[1] > 0)

            def kstep(carry):
                s, _, acc, lat = carry
                kj = qi - s
                k0 = pl.multiple_of(kj * SB_BLOCK, SB_BLOCK)
                k = k_ref[pl.ds(k0, SB_BLOCK), :].astype(BF16)
                v = v_ref[pl.ds(k0, SB_BLOCK), :].astype(BF16)
                vis = jnp.logical_or(kj < qi, tris)
                w, rs, _ = _sb_weights(_dot(qm, k, 1, 1), vis, functools.partial(lambda rs, lat: lat, lat=lat), after)
                acc = acc + _own_lanes(_dot(w.astype(BF16), v, 1, 0))
                lat = lat + rs
                return s + 1, (jnp.max(lat) > SB_DEAD).astype(jnp.int32), acc, lat

            res = lax.while_loop(alive, kstep, (jnp.int32(0), jnp.int32(1), jnp.zeros((SB_BLOCK, SB_WD), F32),
                                                jnp.zeros((SB_HEADS * SB_BLOCK, 1), F32)))
            o_ref[pl.ds(q0, SB_BLOCK), :] = res[2].astype(o_ref.dtype)
            tot_ref[pl.ds(q0, SB_BLOCK), :] = _own_lanes(jnp.broadcast_to(res[3], (SB_HEADS * SB_BLOCK, SB_WD)))
            return jnp.where(lane8 == qi, (qi - res[0] + 1).astype(F32), firsts)

        first_ref[...] = lax.fori_loop(0, nqb, qloop, jnp.zeros((8, LANES), F32))
        ex_finish(ex_parts)

    blk, qkv, out, vec = _sb_specs(lp, nseq)
    return pl.pallas_call(
        body, name="attn_fwd", grid=(nseq, nstep),
        in_specs=qkv + exkw['in_specs'], out_specs=[out, out, vec] + exkw['out_specs'],
        out_shape=[jax.ShapeDtypeStruct((t, SB_WIDTH), BF16), jax.ShapeDtypeStruct((t, SB_WIDTH), F32)]
        + [jax.ShapeDtypeStruct((nseq, nstep, 8, LANES), F32)] + exkw['out_shape'],
        input_output_aliases=exkw['aliases'], scratch_shapes=exkw['scratch'],
        compiler_params=_params(("arbitrary", "arbitrary")))(proj, proj, proj, *exkw['ins'])


def attn_bwd(proj, tot, first, do, nseq, lp, ex=None):
    nqb = lp // SB_BLOCK
    t = proj.shape[0]
    nstep = SB_WIDTH // (SB_TILES * LANES)
    split, ex_start, ex_finish, exkw = _hidden_exchange(ex, 6, 3, (nseq, nstep))

    def body(*refs):
        (q_ref, k_ref, v_ref, tot_ref, first_ref, do_ref), (dq_ref, dk_ref, dv_ref), ex_parts, _ = split(refs)
        ex_start(ex_parts)
        row, col = _sb_masks()
        after = (row > col).astype(BF16)
        before = (row < col).astype(BF16)
        lane8 = lax.broadcasted_iota(jnp.int32, (8, LANES), 1)
        firsts = first_ref[...]
        dk_ref[...] = jnp.zeros(dk_ref.shape, F32)
        dv_ref[...] = jnp.zeros(dv_ref.shape, F32)
        hms, tris = _stack_masks()

        def qloop(qi, _):
            first = jnp.max(jnp.where(lane8 == qi, firsts, 0.0)).astype(jnp.int32)
            first = jnp.clip(first, 0, qi)
            q0 = pl.multiple_of(qi * SB_BLOCK, SB_BLOCK)
            qm = jnp.where(hms, _stack_heads(q_ref[pl.ds(q0, SB_BLOCK), :] * 0.125), 0.0).astype(BF16)
            dom = jnp.where(hms, _stack_heads(do_ref[pl.ds(q0, SB_BLOCK), :]), 0.0).astype(BF16)
            tot = jnp.max(jnp.where(hms, _stack_heads(tot_ref[pl.ds(q0, SB_BLOCK), :]), -jnp.inf),
                          axis=1, keepdims=True)

            def kloop(kj, carry):
                dq, pre, gpre = carry
                k0 = pl.multiple_of(kj * SB_BLOCK, SB_BLOCK)
                k = k_ref[pl.ds(k0, SB_BLOCK), :].astype(BF16)
                v = v_ref[pl.ds(k0, SB_BLOCK), :].astype(BF16)
                vis = jnp.logical_or(kj < qi, tris)
                later_of = functools.partial(lambda rs, t_, p_: t_ - p_ - rs, t_=tot, p_=pre)
                w, rs, lb = _sb_weights(_dot(qm, k, 1, 1), vis, later_of, after)
                g = _dot(dom, v, 1, 1) * w
                dlk = _dot_exact_rhs(g, before, 1, 0) + gpre
                sig = jnp.exp(lb)
                dz = jnp.where(vis, g * (1.0 - sig) - dlk * sig, 0.0).astype(BF16)
                dk_ref[pl.ds(k0, SB_BLOCK), :] += _dot(dz, qm, 0, 0)
                dv_ref[pl.ds(k0, SB_BLOCK), :] += _dot(w.astype(BF16), dom, 0, 0)
                return dq + _own_lanes(_dot(dz, k, 1, 0)), pre + rs, gpre + jnp.sum(g, axis=1, keepdims=True)

            z1 = jnp.zeros((SB_HEADS * SB_BLOCK, 1), F32)
            res = lax.fori_loop(first, qi + 1, kloop, (jnp.zeros((SB_BLOCK, SB_WD), F32), z1, z1))
            dq_ref[pl.ds(q0, SB_BLOCK), :] = res[0] * 0.125
            return 0

        lax.fori_loop(0, nqb, qloop, 0)
        ex_finish(ex_parts)

    blk, qkv, out, vec = _sb_specs(lp, nseq)
    return pl.pallas_call(
        body, name="attn_bwd", grid=(nseq, nstep),
        in_specs=qkv + [out, vec, out] + exkw['in_specs'], out_specs=[out] * 3 + exkw['out_specs'],
        out_shape=[jax.ShapeDtypeStruct((t, SB_WIDTH), F32)] * 3 + exkw['out_shape'],
        input_output_aliases=exkw['aliases'], scratch_shapes=exkw['scratch'],
        compiler_params=_params(("arbitrary", "arbitrary")))(proj, proj, proj, tot, first, do, *exkw['ins'])


def _hg_loop(nch, step, init):
    assert nch % HG_UNROLL == 0

    def trip(i, carry):
        for u in range(HG_UNROLL):
            carry = step(i * HG_UNROLL + u, carry)
        return carry
    return lax.fori_loop(0, nch // HG_UNROLL, trip, init)


def _hg_gates(fc, lb):
    sg = jax.nn.sigmoid(fc)
    f = lb + (1.0 - lb) * sg
    return sg, f


def _dot_exact_lhs(m, x):
    hi, mid, lo = _split3(x)
    return _dot(m, hi, 1, 0) + (_dot(m, mid, 1, 0) + _dot(m, lo, 1, 0))


def hgrn_fwd(proj, lb, ng, nseq, lp, ex=None):
    nch = lp // HG_CHUNK
    t = proj.shape[0]
    c = HG_CHUNK

    split, ex_start, ex_finish, exkw = _hidden_exchange(ex, 6, 3, (HG_HEADS,))

    def body(*refs):
        (q_ref, f_ref, v_ref, g_ref, lb_ref, ng_ref), (o_ref, og_ref, st_ref), ex_parts, _ = split(refs)
        ex_start(ex_parts)
        lbv, ngv = lb_ref[...], ng_ref[...]

        rr = nseq * c
        seq_r = lax.broadcasted_iota(jnp.int32, (rr, 1), 0) // c
        bd_row = lax.broadcasted_iota(jnp.int32, (rr, rr), 0)
        bd_col = lax.broadcasted_iota(jnp.int32, (rr, rr), 1)
        causal = jnp.logical_and(bd_row // c == bd_col // c, bd_col <= bd_row)
        incl = causal.astype(BF16)

        def stacked(ref, ci):
            return jnp.concatenate([ref[pl.ds(pl.multiple_of(b * lp + ci * c, c), c), :] for b in range(nseq)], axis=0)

        def wide(x):
            return jnp.concatenate([jnp.where(seq_r == b, x, jnp.zeros_like(x)) for b in range(nseq)], axis=1)

        def step(ci, st):
            for b in range(nseq):
                st_ref[b, ci] = st[:, b * HG_DK:(b + 1) * HG_DK]
            _, f = _hg_gates(stacked(f_ref, ci), lbv)
            bcum = _dot_exact_lhs(incl, jnp.log(f))
            lasts = [bcum[b * c + c - 1:b * c + c, :] for b in range(nseq)]
            blast = jnp.concatenate([jnp.broadcast_to(x, (c, LANES)) for x in lasts], axis=0)
            kk = 1.0 - f
            vc = stacked(v_ref, ci).astype(BF16)
            qd = (stacked(q_ref, ci) * jnp.exp(bcum)).astype(BF16)
            kd = (kk * jnp.exp(-bcum)).astype(BF16)
            a = jnp.where(causal, _dot(qd, kd, 1, 1), 0.0)
            o = _dot(a.astype(BF16), vc, 1, 0) + _dot(wide(qd), st.astype(BF16), 1, 1)
            kend = (kk * jnp.exp(blast - bcum)).astype(BF16)
            st = st * jnp.exp(jnp.concatenate(lasts, axis=1)) + _dot(vc, wide(kend), 0, 0)
            r = lax.rsqrt(jnp.mean(o * o, axis=1, keepdims=True) + RMS_EPS)
            gc = stacked(g_ref, ci)
            og = (o * r * ngv * (gc * jax.nn.sigmoid(gc))).astype(og_ref.dtype)
            for b in range(nseq):
                rows = pl.ds(pl.multiple_of(b * lp + ci * c, c), c)
                o_ref[rows, :] = o[b * c:(b + 1) * c]
                og_ref[rows, :] = og[b * c:(b + 1) * c]
            return st

        _hg_loop(nch, step, jnp.zeros((HG_DK, nseq * HG_DK), F32))
        ex_finish(ex_parts)

    blk = (nseq * lp, LANES)
    h = HG_HEADS
    return pl.pallas_call(
        body, name="hgrn_fwd", grid=(h,),
        in_specs=[pl.BlockSpec(blk, lambda hh: (0, hh)),
                  pl.BlockSpec(blk, lambda hh: (0, h + hh)),
                  pl.BlockSpec(blk, lambda hh: (0, 2 * h + hh)),
                  pl.BlockSpec(blk, lambda hh: (0, 3 * h + hh)),
                  pl.BlockSpec((1, LANES), lambda hh: (0, hh)),
                  pl.BlockSpec((1, LANES), lambda hh: (0, hh))] + exkw['in_specs'],
        out_specs=[pl.BlockSpec(blk, lambda hh: (0, hh)),
                   pl.BlockSpec(blk, lambda hh: (0, hh)),
                   pl.BlockSpec((nseq, None, nch, HG_DK, HG_DK), lambda hh: (0, hh, 0, 0, 0))] + exkw['out_specs'],
        out_shape=[jax.ShapeDtypeStruct((t, D_MODEL), F32), jax.ShapeDtypeStruct((t, D_MODEL), BF16),
                   jax.ShapeDtypeStruct((nseq, h, nch, HG_DK, HG_DK), F32)] + exkw['out_shape'],
        input_output_aliases=exkw['aliases'], scratch_shapes=exkw['scratch'],
        compiler_params=_params(("arbitrary",)))(proj, proj, proj, proj, lb, ng, *exkw['ins'])


def hgrn_bwd(proj, o, dog, states, lb, ng, nseq, lp):
    nch = lp // HG_CHUNK
    t = proj.shape[0]
    c = HG_CHUNK
    rr = nseq * c

    def body(q_ref, f_ref, v_ref, g_ref, o_ref, dog_ref, st_ref, lb_ref, ng_ref,
             dq_ref, df_ref, dv_ref, dg_ref, dlb_ref, dng_ref):
        seq_r = lax.broadcasted_iota(jnp.int32, (rr, 1), 0) // c
        last = lax.broadcasted_iota(jnp.int32, (rr, 1), 0) % c == c - 1
        bd_row = lax.broadcasted_iota(jnp.int32, (rr, rr), 0)
        bd_col = lax.broadcasted_iota(jnp.int32, (rr, rr), 1)
        same = bd_row // c == bd_col // c
        causal = jnp.logical_and(same, bd_col <= bd_row)
        incl = causal.astype(BF16)
        from_here = jnp.logical_and(same, bd_col >= bd_row).astype(BF16)
        lbv, ngv = lb_ref[...], ng_ref[...]

        def stacked(ref, ci):
            return jnp.concatenate([ref[pl.ds(pl.multiple_of(b * lp + ci * c, c), c), :] for b in range(nseq)], axis=0)

        def wide(x):
            return jnp.concatenate([jnp.where(seq_r == b, x, jnp.zeros_like(x)) for b in range(nseq)], axis=1)

        def own(x):
            return functools.reduce(jnp.add, [jnp.where(seq_r == b, x[:, b * LANES:(b + 1) * LANES], 0.0)
                                              for b in range(nseq)])

        def per_seq_rows(vals):
            return jnp.concatenate([jnp.broadcast_to(x, (c, LANES)) for x in vals], axis=0)

        def step(s, carry):
            dst, dlb, dng = carry
            ci = nch - 1 - s
            oc, gc, dogc = stacked(o_ref, ci), stacked(g_ref, ci), stacked(dog_ref, ci)
            r = lax.rsqrt(jnp.mean(oc * oc, axis=1, keepdims=True) + RMS_EPS)
            sgg = jax.nn.sigmoid(gc)
            dgc = dogc * (oc * r * ngv) * (sgg * (1.0 + gc * (1.0 - sgg)))
            don = dogc * (gc * sgg)
            dng = dng + _colsum(don * oc * r)
            tt = don * ngv
            do = r * (tt - oc * (r * r) * jnp.mean(tt * oc, axis=1, keepdims=True))
            st = jnp.concatenate([st_ref[b, ci] for b in range(nseq)], axis=1)
            sg, f = _hg_gates(stacked(f_ref, ci), lbv)
            bcum = _dot_exact_lhs(incl, jnp.log(f))
            lasts = [bcum[b * c + c - 1:b * c + c, :] for b in range(nseq)]
            blast = per_seq_rows(lasts)
            kk = 1.0 - f
            qc, vf = stacked(q_ref, ci), stacked(v_ref, ci)
            pdec = jnp.exp(bcum)
            ndec = jnp.exp(-bcum)
            edec = jnp.exp(blast - bcum)
            eb = jnp.exp(jnp.concatenate(lasts, axis=1))
            qd_f, kd_f, kend_f = qc * pdec, kk * ndec, kk * edec
            qd, kd, kend = qd_f.astype(BF16), kd_f.astype(BF16), kend_f.astype(BF16)
            vc, dob, stb, dstb = vf.astype(BF16), do.astype(BF16), st.astype(BF16), dst.astype(BF16)
            a = jnp.where(causal, _dot(qd, kd, 1, 1), 0.0).astype(BF16)
            da = jnp.where(causal, _dot(dob, vc, 1, 1), 0.0).astype(BF16)
            dv = _dot(a, dob, 0, 0) + _dot(wide(kend), dstb, 1, 1)
            dqd = _dot(da, kd, 1, 0) + own(_dot(dob, stb, 1, 0))
            dkd = _dot(da, qd, 0, 0)
            dkend = own(_dot(vc, dstb, 1, 0))
            sts = _colsum(dst * st)
            dkk = dkend * kend_f
            dblast = per_seq_rows([_colsum(jnp.where(seq_r == b, dkk, 0.0))
                                   + eb[:, b * LANES:(b + 1) * LANES] * sts[:, b * LANES:(b + 1) * LANES]
                                   for b in range(nseq)])
            dbcum = dqd * qd_f - dkd * kd_f - dkk
            dbcum = dbcum + jnp.where(last, dblast, 0.0)
            dlf = _dot_exact_lhs(from_here, dbcum)
            df = dlf / f - (dkd * ndec + dkend * edec)
            dfp = df * (1.0 - lbv) * sg * (1.0 - sg)
            dqc = dqd * pdec
            for b in range(nseq):
                rows = pl.ds(pl.multiple_of(b * lp + ci * c, c), c)
                blk_rows = slice(b * c, (b + 1) * c)
                dg_ref[rows, :], dv_ref[rows, :] = dgc[blk_rows], dv[blk_rows]
                dq_ref[rows, :], df_ref[rows, :] = dqc[blk_rows], dfp[blk_rows]
            dlb = dlb + _colsum(df * (1.0 - sg))
            dst = dst * eb + _dot(dob, wide(qd), 0, 0)
            return dst, dlb, dng

        z = jnp.zeros((1, LANES), F32)
        _, dlb, dng = _hg_loop(nch, step, (jnp.zeros((HG_DK, nseq * HG_DK), F32), z, z))
        dlb_ref[...] = dlb
        dng_ref[...] = dng

    blk = (nseq * lp, LANES)
    h = HG_HEADS
    vec = pl.BlockSpec((1, LANES), lambda hh: (0, hh))
    col = pl.BlockSpec(blk, lambda hh: (0, hh))
    return pl.pallas_call(
        body, name="hgrn_bwd", grid=(h,),
        in_specs=[col,
                  pl.BlockSpec(blk, lambda hh: (0, h + hh)),
                  pl.BlockSpec(blk, lambda hh: (0, 2 * h + hh)),
                  pl.BlockSpec(blk, lambda hh: (0, 3 * h + hh)),
                  col, col,
                  pl.BlockSpec((nseq, None, nch, HG_DK, HG_DK), lambda hh: (0, hh, 0, 0, 0)),
                  vec, vec],
        out_specs=[col] * 4 + [vec, vec],
        out_shape=[jax.ShapeDtypeStruct((t, D_MODEL), F32)] * 4 + [jax.ShapeDtypeStruct((1, D_MODEL), F32)] * 2,
        compiler_params=pltpu.CompilerParams(dimension_semantics=("parallel",), vmem_limit_bytes=HGRN_BWD_VMEM))(
            proj, proj, proj, proj, o, dog, states, lb, ng)


def _lower_bound(gamma):
    p = jax.nn.softmax(gamma, axis=0)
    return (jnp.cumsum(p, axis=0) - p[0])[1][None, :]


def local_step(x, tgt, w, hooks=None):
    nseq, seq, _ = x.shape
    lp = -(-(N_META + seq) // SB_BLOCK) * SB_BLOCK
    t = nseq * lp
    pad = lp - N_META - seq
    h0 = jnp.concatenate([jnp.broadcast_to(w['meta'][None], (nseq, N_META, D_MODEL)), x,
                          jnp.zeros((nseq, pad, D_MODEL), F32)], axis=1).reshape(t, D_MODEL)
    tgt_p = jnp.pad(tgt, ((0, 0), (N_META, pad), (0, 0))).reshape(t, D_MODEL)
    row = lambda v: v.reshape(1, -1)
    g = {}

    proj = matmul("in_ab", h0, w['w_in_ab'], 'nn')
    att = attn_fwd(proj, nseq, lp, ex=hooks.gather_late if hooks else None)
    b_out, sb_tot, sb_first = att[:3]
    if hooks:
        w = {**w, **hooks.late_weights(att[3:])}
    s5_names = ['s5_lam_re', 's5_lam_im', 's5_log_dt', 's5_b_re', 's5_b_im', 's5_c_re', 's5_c_im', 's5_d']
    mats, mats_vjp = jax.vjp(s5_matrices, *[w[n][0] for n in s5_names])
    ug = _to_groups(proj[:, :S5_WIDTH])
    s5_pows = s5_scan_powers(*[w[n][0] for n in s5_names[:3]])
    yg, sst = s5_fwd(ug, *mats[:3], *s5_pows, nseq)
    ys5 = _from_groups(yg)
    y = gelu_fwd("gelu", ys5)
    gp = matmul("glu", y, w['s5_w_glu'], 'nn', bias=w['s5_b_glu'])
    a_out = glu_gate("glu_gate", y, gp)
    cat = jnp.concatenate([a_out, b_out], axis=1)
    mix0 = matmul("out_ab", cat, w['w_out_ab'], 'nn')
    h1, xh1, rs1 = ln_fwd("ln_mix0", h0, mix0, row(w['ln_mix_g'][0]), row(w['ln_mix_b'][0]))
    hid0 = matmul("up0", h1, w['mlp_w_up'][0], 'nn', bias=row(w['mlp_b_up'][0]), act='relu2', out_dtype=BF16)
    dn0 = matmul("down0", hid0, w['mlp_w_down'][0], 'nn', bias=row(w['mlp_b_down'][0]))
    h2, xh2, rs2 = ln_fwd("ln_mlp0", h1, dn0, row(w['ln_mlp_g'][0]), row(w['ln_mlp_b'][0]))

    projc = matmul("in_c", h2, w['w_in_c'], 'nn')
    lb, lb_vjp = jax.vjp(_lower_bound, w['hgrn_gamma'])
    ng = w['hgrn_norm_g']
    res = hgrn_fwd(projc, lb, ng, nseq, lp, ex=hooks.gather_last if hooks else None)
    o, og, states = res[:3]
    if hooks:
        w = {**w, **hooks.last_weights(res[3:])}
    mix1 = matmul("out_c", og, w['w_out_c'], 'nn')
    h3, xh3, rs3 = ln_fwd("ln_mix1", h2, mix1, row(w['ln_mix_g'][1]), row(w['ln_mix_b'][1]))
    hid1 = matmul("up1", h3, w['mlp_w_up'][1], 'nn', bias=row(w['mlp_b_up'][1]), act='relu2', out_dtype=BF16)
    dn1 = matmul("down1", hid1, w['mlp_w_down'][1], 'nn', bias=row(w['mlp_b_down'][1]))
    h4, xh4, rs4 = ln_fwd("ln_mlp1", h3, dn1, row(w['ln_mlp_g'][1]), row(w['ln_mlp_b'][1]))
    dy, loss = loss_head("loss", h4, tgt_p, lp, seq)

    def mlp_bwd(l, dh_out, xh_out, rs_out, hid, h_in):
        dpre, dg_, db_, dbd = ln_bwd(f"ln_mlp{l}_b", dh_out, xh_out, rs_out, row(w['ln_mlp_g'][l]))
        dpu = matmul(f"down{l}_dx", dpre, w['mlp_w_down'][l], 'nt', relu2_of=hid, out_dtype=BF16)
        dwd = matmul(f"down{l}_dw", hid, dpre, 'tn')
        dh_in = matmul(f"up{l}_dx", dpu, w['mlp_w_up'][l], 'nt', add=dpre, add_scale=ALPHA)
        dwu, dbu = matmul(f"up{l}_dw", h_in, dpu, 'tn', out_slots=N_CHIPS, colsum=True)
        return dh_in, dict(ln_mlp_g=dg_, ln_mlp_b=db_, mlp_b_down=dbd, mlp_b_up=dbu, mlp_w_up=dwu, mlp_w_down=dwd)

    dh3, gm1 = mlp_bwd(1, dy, xh4, rs4, hid1, h3)
    dpre, dg1, db1, _ = ln_bwd("ln_mix1_b", dh3, xh3, rs3, row(w['ln_mix_g'][1]))
    g['w_out_c'] = matmul("out_c_dw", og, dpre, 'tn')
    dog = matmul("out_c_dx", dpre, w['w_out_c'], 'nt')
    dq, df, di, dgg, dlb, dng = hgrn_bwd(projc, o, dog, states, lb, ng, nseq, lp)
    dprojc = jnp.concatenate([dq, df, di, dgg], axis=1)
    dh2 = matmul("in_c_dx", dprojc, w['w_in_c'], 'nt', add=dpre, add_scale=ALPHA)
    g['w_in_c'] = matmul("in_c_dw", h2, dprojc, 'tn', out_slots=N_CHIPS)
    g['hgrn_gamma'] = lb_vjp(dlb)[0]
    g['hgrn_norm_g'] = dng

    dh1, gm0 = mlp_bwd(0, dh2, xh2, rs2, hid0, h1)
    dpre, dg0, db0, _ = ln_bwd("ln_mix0_b", dh1, xh1, rs1, row(w['ln_mix_g'][0]))
    g['w_out_ab'] = matmul("out_ab_dw", cat, dpre, 'tn')
    dcat = matmul("out_ab_dx", dpre, w['w_out_ab'], 'nt')
    dgp, da_s, dbglu = glu_gate_bwd("glu_gate_b", dcat[:, :S5_WIDTH], y, gp)
    g['s5_w_glu'] = matmul("glu_dw", y, dgp, 'tn')
    g['s5_b_glu'] = dbglu
    dy5 = matmul("glu_dx", dgp, w['s5_w_glu'], 'nt', add=da_s)
    dys5 = gelu_bwd("gelu_b", dy5, ys5)
    for n in ('mlp_w_up', 'mlp_w_down'):
        g[n] = [gm0[n], gm1[n]]
    g['ln_mix_g'] = jnp.concatenate([dg0, dg1], axis=0)
    g['ln_mix_b'] = jnp.concatenate([db0, db1], axis=0)
    for n in ('ln_mlp_g', 'ln_mlp_b', 'mlp_b_down', 'mlp_b_up'):
        g[n] = jnp.concatenate([gm0[n], gm1[n]], axis=0)
    res = s5_bwd(_to_groups(dys5), ug, sst, *mats[:3], *s5_pows, nseq, ex=hooks.fold_early(g) if hooks else None)
    dug, da1, da2, da3, dare, daim = res[:6]
    for n, v in zip(s5_names, mats_vjp((da1, da2, da3, dare, daim))):
        g[n] = v[None]
    ex = hooks.scatter_early(g, loss, res[6:]) if hooks else None
    res = attn_bwd(proj, sb_tot, sb_first, dcat[:, S5_WIDTH:], nseq, lp, ex=ex)
    dqa, dka, dva = res[:3]
    if hooks:
        hooks.scattered(res[3:])
    dproj = jnp.concatenate([_from_groups(dug), dqa, dka, dva], axis=1)
    dh0 = matmul("in_ab_dx", dproj, w['w_in_ab'], 'nt', add=dpre, add_scale=ALPHA)
    g['w_in_ab'] = matmul("in_ab_dw", h0, dproj, 'tn', out_slots=N_CHIPS)

    dh0 = dh0.reshape(nseq, lp, D_MODEL)
    grad_x = dh0[:, N_META:N_META + seq]
    g['meta'] = jnp.sum(dh0[:, :N_META], axis=0)
    return loss, grad_x, g


class Exchange:
    def __init__(self, ins, out_shapes, n_remote, build, in_place):
        self.ins, self.out_shapes, self.n_remote, self.build, self.in_place = ins, out_shapes, n_remote, build, in_place

    def sems(self):
        return [pltpu.SemaphoreType.DMA((self.n_remote,)), pltpu.SemaphoreType.DMA((self.n_remote,))]

    def aliases(self):
        return self.in_place if isinstance(self.in_place, dict) else \
            {i: i for i in range(len(self.ins) if self.in_place else 0)}

    def beside(self, other):
        na, nao = len(self.ins), len(self.out_shapes)

        def build(in_refs, out_refs, me):
            pa = self.build(in_refs[:na], out_refs[:nao], me)
            pb = other.build(in_refs[na:], out_refs[nao:], me)
            n = max(len(pa), len(pb))
            return [(pa[i] if i < len(pa) else []) + (pb[i] if i < len(pb) else []) for i in range(n)]
        al = dict(self.aliases())
        al.update({na + i: nao + j for i, j in other.aliases().items()})
        return Exchange(list(self.ins) + list(other.ins), list(self.out_shapes) + list(other.out_shapes),
                        self.n_remote + other.n_remote, build, al)

    def phases(self, in_refs, out_refs, ssem, rsem):
        me = (lax.axis_index("x"), lax.axis_index("y"), lax.axis_index("c"))
        out, k = [], 0
        for phase in self.build(in_refs, out_refs, me):
            out.append(functools.partial(
                lambda phase, k: [pltpu.make_async_remote_copy(src_ref=s, dst_ref=d, send_sem=ssem.at[k + i],
                                                               recv_sem=rsem.at[k + i], device_id=p,
                                                               device_id_type=MESH)
                                  for i, (s, d, p) in enumerate(phase)], phase, k))
            k += len(phase)
        return out

    def start(self, in_refs, out_refs, ssem, rsem):
        for cp in self.phases(in_refs, out_refs, ssem, rsem)[0]():
            cp.start()

    def finish(self, in_refs, out_refs, ssem, rsem):
        phases = self.phases(in_refs, out_refs, ssem, rsem)
        for cp in phases[0]():
            cp.wait()
        for make in phases[1:]:
            copies = make()
            for cp in copies:
                cp.start()
            for cp in copies:
                cp.wait()


def _exchange(name, ex):
    ni, no = len(ex.ins), len(ex.out_shapes)

    def body(*refs):
        in_refs, out_refs, sems = refs[:ni], refs[ni:ni + no], refs[ni + no:]
        ex.start(in_refs, out_refs, *sems)
        ex.finish(in_refs, out_refs, *sems)

    anyspec = pl.BlockSpec(memory_space=pl.ANY)
    return pl.pallas_call(
        body, name=name, in_specs=[anyspec] * ni, out_specs=[anyspec] * no, out_shape=ex.out_shapes,
        input_output_aliases=ex.aliases(), scratch_shapes=ex.sems())(*ex.ins)


def _chip_peers(me):
    x, y, c = me
    return [(x ^ (m >> 1), y ^ (m & 1), c) for m in (1, 2, 3)]


def _half(ref, c, axis):
    h = ref.shape[axis] // 2
    idx = [slice(None)] * axis + [pl.ds(c * h, h)]
    return ref.at[tuple(idx)]


def _like(arrs):
    return [jax.ShapeDtypeStruct(a.shape, a.dtype) for a in arrs]


def gather_over_chips(bufs):
    def build(in_refs, out_refs, me):
        x, y, c = me
        j = 2 * x + y
        sib = (x, y, 1 - c)
        ici = [(_half(o.at[j], c, 0), _half(o.at[j], c, 0), p) for o in out_refs for p in _chip_peers(me)]
        d2d = [(_half(o.at[2 * p[0] + p[1]], c, 0), _half(o.at[2 * p[0] + p[1]], c, 0), sib)
               for o in out_refs for p in _chip_peers(me)]
        return [ici, d2d]
    return Exchange(bufs, _like(bufs), 6 * len(bufs), build, True)


def fold_cores(fulls):
    def build(in_refs, out_refs, me):
        x, y, c = me
        return [[(_half(s, 1 - c, 1), o, (x, y, 1 - c)) for s, o in zip(in_refs, out_refs)]]
    shapes = [jax.ShapeDtypeStruct((f.shape[0], f.shape[1] // 2, f.shape[2]), f.dtype) for f in fulls]
    return Exchange(fulls, shapes, len(fulls), build, False)


def scatter_over_chips(parts):
    def build(in_refs, out_refs, me):
        j = 2 * me[0] + me[1]
        return [[(s.at[2 * p[0] + p[1]], o.at[j], p) for s, o in zip(in_refs, out_refs) for p in _chip_peers(me)]]
    return Exchange(parts, _like(parts), 3 * len(parts), build, False)


def join_cores(bufs):
    def build(in_refs, out_refs, me):
        x, y, c = me
        return [[(o.at[c], o.at[c], (x, y, 1 - c)) for o in out_refs]]
    return Exchange(bufs, _like(bufs), len(bufs), build, True)


def gather_over_devices(buf):
    def build(in_refs, out_refs, me):
        x, y, c = me
        out = out_refs[0]
        sib = (x, y, 1 - c)
        mine = out.at[4 * x + 2 * y + c]
        first = [(mine, mine, sib)] + [(mine, mine, p) for p in _chip_peers(me)]
        second = [(out.at[4 * p[0] + 2 * p[1] + c], out.at[4 * p[0] + 2 * p[1] + c], sib) for p in _chip_peers(me)]
        return [first, second]
    return Exchange([buf], _like([buf]), N_DEV - 1, build, True)


def _slot_call(name, body, scalars, ins, in_maps, out_shapes, out_maps, blk, grid):
    gs = pltpu.PrefetchScalarGridSpec(
        num_scalar_prefetch=1, grid=grid,
        in_specs=[pl.BlockSpec((None,) + blk, m) for m in in_maps],
        out_specs=[pl.BlockSpec((None,) + blk, m) for m in out_maps])
    return pl.pallas_call(body, name=name, grid_spec=gs, out_shape=out_shapes,
                          compiler_params=_params(("arbitrary",) * len(grid)))(scalars, *ins)


def _group_rows(r, n):
    return _pick(r, (512, 256, 128, 64, 32, 16) if n <= 2 else (256, 128, 64, 32, 16))


def cast_into_slot(name, ws, chip, dtype):
    n = len(ws)
    r, wd = ws[0].shape
    tm = _group_rows(r, n)

    def body(s_ref, *refs):
        for w_ref, o_ref in zip(refs[:n], refs[n:]):
            o_ref[...] = w_ref[...].astype(o_ref.dtype)

    gs = pltpu.PrefetchScalarGridSpec(
        num_scalar_prefetch=1, grid=(r // tm,),
        in_specs=[pl.BlockSpec((tm, wd), lambda i, s: (i, 0))] * n,
        out_specs=[pl.BlockSpec((None, tm, wd), lambda i, s: (s[0], i, 0))] * n)
    return pl.pallas_call(body, name=name, grid_spec=gs,
                          out_shape=[jax.ShapeDtypeStruct((N_CHIPS, r, wd), dtype)] * n,
                          compiler_params=_params(("arbitrary",)))(chip.reshape(1), *ws)


def sum_cores(name, fulls, theirs, core):
    n = len(fulls)
    s, r, wd = fulls[0].shape
    h = r // 2
    tm = _group_rows(h, n)
    nt = h // tm

    def body(s_ref, *refs):
        for k in range(n):
            refs[2 * n + k][...] = (refs[2 * k][...] + refs[2 * k + 1][...]).astype(BF16)

    ins = [x for pair in zip(fulls, theirs) for x in pair]
    return _slot_call(name, body, core.reshape(1), ins,
                      [lambda k, i, s_: (k, s_[0] * nt + i, 0), lambda k, i, s_: (k, i, 0)] * n,
                      [jax.ShapeDtypeStruct((s, h, wd), BF16)] * n, [lambda k, i, s_: (k, i, 0)] * n,
                      (tm, wd), (s, nt))


def sum_chips(name, owns, recvs, chip, core):
    n = len(owns)
    s, r, wd = owns[0].shape
    tm = _group_rows(r, n)

    def body(s_ref, *refs):
        for k in range(n):
            acc = refs[4 * k][...].astype(F32)
            for ref in refs[4 * k + 1:4 * k + 4]:
                acc = acc + ref[...].astype(F32)
            refs[4 * n + k][...] = acc

    maps = [lambda i, s_: (s_[0], i, 0)]
    maps += [functools.partial(lambda i, s_, m: (s_[0] ^ m, i, 0), m=m) for m in (1, 2, 3)]
    ins = [x for own, recv in zip(owns, recvs) for x in (own, recv, recv, recv)]
    return _slot_call(name, body, jnp.stack([chip, core]), ins, maps * n,
                      [jax.ShapeDtypeStruct((2, r, wd), F32)] * n, [lambda i, s_: (s_[1], i, 0)] * n,
                      (tm, wd), (r // tm,))


def sum_slots(name, arr):
    s, r, wd = arr.shape
    tm = _pick(r, (512, 256, 128, 64, 32, 16, 8))

    def body(*refs):
        acc = refs[0][...].astype(F32)
        for ref in refs[1:s]:
            acc = acc + ref[...].astype(F32)
        refs[s][...] = acc

    specs = [pl.BlockSpec((None, tm, wd), functools.partial(lambda i, k: (k, i, 0), k=k)) for k in range(s)]
    return pl.pallas_call(body, name=name, grid=(r // tm,), in_specs=specs,
                          out_specs=pl.BlockSpec((tm, wd), lambda i: (i, 0)),
                          out_shape=jax.ShapeDtypeStruct((r, wd), F32),
                          compiler_params=_params(("parallel",)))(*([arr] * s))


def _pack(arrs):
    flat = jnp.concatenate([a.reshape(-1) for a in arrs])
    n = flat.shape[0]
    padded = -(-n // (512 * LANES)) * (512 * LANES)
    return jnp.pad(flat, (0, padded - n)).reshape(-1, LANES)


def _unpack(packed, shapes):
    flat = packed.reshape(-1)
    out, pos = [], 0
    for s in shapes:
        n = math.prod(s)
        out.append(flat[pos:pos + n].reshape(s))
        pos += n
    return out


def _as2d(a):
    return a.reshape(-1, a.shape[-1])


def kernel(x, meta, w_in_ab, s5_lam_re, s5_lam_im, s5_log_dt, s5_b_re, s5_b_im, s5_c_re, s5_c_im, s5_d, s5_w_glu, s5_b_glu, w_out_ab, w_in_c, hgrn_gamma, hgrn_norm_g, w_out_c, ln_mix_g, ln_mix_b, mlp_w_up, mlp_b_up, mlp_w_down, mlp_b_down, ln_mlp_g, ln_mlp_b, loss_target, m_meta, m_w_in_ab, m_s5_lam_re, m_s5_lam_im, m_s5_log_dt, m_s5_b_re, m_s5_b_im, m_s5_c_re, m_s5_c_im, m_s5_d, m_s5_w_glu, m_s5_b_glu, m_w_out_ab, m_w_in_c, m_hgrn_gamma, m_hgrn_norm_g, m_w_out_c, m_ln_mix_g, m_ln_mix_b, m_mlp_w_up, m_mlp_b_up, m_mlp_w_down, m_mlp_b_down, m_ln_mlp_g, m_ln_mlp_b, v_meta, v_w_in_ab, v_s5_lam_re, v_s5_lam_im, v_s5_log_dt, v_s5_b_re, v_s5_b_im, v_s5_c_re, v_s5_c_im, v_s5_d, v_s5_w_glu, v_s5_b_glu, v_w_out_ab, v_w_in_c, v_hgrn_gamma, v_hgrn_norm_g, v_w_out_c, v_ln_mix_g, v_ln_mix_b, v_mlp_w_up, v_mlp_b_up, v_mlp_w_down, v_mlp_b_down, v_ln_mlp_g, v_ln_mlp_b):
    given = dict(locals())
    wts = {n: given[n] for n in WEIGHTS}
    ms = {n: given['m_' + n] for n in WEIGHTS}
    vs = {n: given['v_' + n] for n in WEIGHTS}
    chip = 2 * lax.axis_index("x") + lax.axis_index("y")

    core = lax.axis_index("c")
    parts = [(n, l) for n in BIG for l in (range(DEPTH) if n.startswith('mlp_') else [0])]

    def by_shape(tag, fn, keys, *lists):
        out = {}
        for shape in dict.fromkeys(a.shape for a in lists[0]):
            idx = [i for i, a in enumerate(lists[0]) if a.shape == shape]
            res = fn(f"{tag}_{keys[idx[0]][0]}{keys[idx[0]][1]}_x{len(idx)}", *[[lst[i] for i in idx] for lst in lists])
            out.update({keys[i]: r for i, r in zip(idx, res)})
        return [out[k] for k in keys]

    shards = dict(zip(parts, by_shape("cast", lambda nm, ws: cast_into_slot(nm, ws, chip, BF16), parts,
                                      [wts[n][l] for n, l in parts])))
    small_sh = jnp.concatenate([meta, hgrn_norm_g, jnp.zeros((15, meta.shape[1]), F32)], axis=0)
    first, late = parts[:1], parts[1:]
    w_in_ab_all, sm = _exchange("gather_first", gather_over_chips(
        [shards[p] for p in first] + list(cast_into_slot("cast_small", [small_sh], chip, F32))))
    w = {n: wts[n] for n in SMALL}
    w['meta'] = sm[:, :N_META].transpose(1, 0, 2).reshape(N_META, D_MODEL)
    w['hgrn_norm_g'] = sm[:, N_META:N_META + 1].transpose(1, 0, 2).reshape(1, D_MODEL)
    w['w_in_ab'] = w_in_ab_all

    def slot_major(v):
        return v if v.ndim == 3 else v.reshape(N_CHIPS, -1, v.shape[-1])

    def whole_grads(ps, g):
        return [slot_major(g[n][l] if n.startswith('mlp_') else g[n]) for n, l in ps]

    def chip_sums_of(ps, gfull, theirs):
        return by_shape("sum_cores", lambda nm, a, b: sum_cores(nm, a, b, core), ps, gfull, list(theirs))

    def all_devices(arrs):
        packed = _pack(arrs)
        return gather_over_devices(lax.dynamic_update_slice(
            jnp.zeros((N_DEV,) + packed.shape, F32), packed[None], (2 * chip + core, 0, 0)))

    early_small = [n for n in SMALL if n != 'meta']
    last = [('w_out_c', 0), ('mlp_w_up', 1), ('mlp_w_down', 1)]
    mid = [p for p in late if p not in last]

    class Hooks:
        gather_late = gather_over_chips([shards[p] for p in mid])
        gather_last = gather_over_chips([shards[p] for p in last])

        @staticmethod
        def late_weights(filled):
            fw = Hooks.fw = dict(zip(mid, filled))
            return {'s5_w_glu': fw['s5_w_glu', 0].reshape(S5_WIDTH, S5_WIDTH),
                    'w_out_ab': fw['w_out_ab', 0].reshape(D_MODEL, D_MODEL),
                    'w_in_c': fw['w_in_c', 0],
                    'mlp_w_up': [fw['mlp_w_up', 0]],
                    'mlp_w_down': [fw['mlp_w_down', 0].reshape(D_FF, D_MODEL)]}

        @staticmethod
        def last_weights(filled):
            fw = dict(zip(last, filled))
            return {'w_out_c': fw['w_out_c', 0].reshape(D_MODEL, D_MODEL),
                    'mlp_w_up': [Hooks.fw['mlp_w_up', 0], fw['mlp_w_up', 1]],
                    'mlp_w_down': [Hooks.fw['mlp_w_down', 0].reshape(D_FF, D_MODEL),
                                   fw['mlp_w_down', 1].reshape(D_FF, D_MODEL)]}

        @staticmethod
        def fold_early(g):
            Hooks.whole = whole_grads(late, g)
            return fold_cores(Hooks.whole)

        @staticmethod
        def scatter_early(g, loss, theirs):
            Hooks.sums = chip_sums_of(late, Hooks.whole, theirs)
            Hooks.small_shapes = [(LANES,)] + [g[n].shape for n in early_small]
            return scatter_over_chips(Hooks.sums).beside(
                all_devices([loss.reshape(-1)] + [g[n] for n in early_small]))

        @staticmethod
        def scattered(outs):
            Hooks.recv, Hooks.small = list(outs[:-1]), outs[-1]

    loss, grad_x, g = local_step(x, loss_target, w, Hooks)

    whole = whole_grads(first, g)
    sums = chip_sums_of(first, whole, _exchange("fold_last", fold_cores(whole)))
    *recv, meta_slots = _exchange("scatter_last", scatter_over_chips(sums).beside(all_devices([g['meta']])))
    reduced = _exchange("join_grads", join_cores(
        by_shape("sum_chips", lambda nm, a, b: sum_chips(nm, a, b, chip, core), first + late,
                 sums + Hooks.sums, list(recv) + Hooks.recv)))
    reduced = dict(zip(first + late, [_as2d(r) for r in reduced]))
    red = _unpack(sum_slots("sum_small", Hooks.small), Hooks.small_shapes)
    loss_out = red[0][0]
    gs = dict(zip(early_small, red[1:]))
    gs['meta'] = _unpack(sum_slots("sum_meta", meta_slots), [g['meta'].shape])[0]
    gs['meta'] = lax.dynamic_slice_in_dim(gs['meta'], chip * meta.shape[1], meta.shape[1], axis=1)
    gs['hgrn_norm_g'] = lax.dynamic_slice_in_dim(gs['hgrn_norm_g'].reshape(1, -1), chip * meta.shape[1],
                                                 meta.shape[1], axis=1)

    grads, deltas, new_m, new_v = {}, {}, {}, {}
    for n in BIG:
        r = jnp.concatenate([reduced[n, l] for l in range(wts[n].shape[0])], axis=0)
        res = adamw("adamw_" + n, _as2d(wts[n]), [r], _as2d(ms[n]), _as2d(vs[n]))
        grads[n], deltas[n], new_m[n], new_v[n] = [r.reshape(wts[n].shape) for r in res]
    shapes = [wts[n].shape for n in SMALL]
    res = adamw("adamw_small", _pack([wts[n] for n in SMALL]), [_pack([gs[n] for n in SMALL])],
                _pack([ms[n] for n in SMALL]), _pack([vs[n] for n in SMALL]))
    for d, r in zip((grads, deltas, new_m, new_v), res):
        d.update(zip(SMALL, _unpack(r, shapes)))
    return (loss_out, grad_x, *[grads[n] for n in WEIGHTS], *[deltas[n] for n in WEIGHTS],
            *[new_m[n] for n in WEIGHTS], *[new_v[n] for n in WEIGHTS])
```

```python
import functools
import math

import jax
import jax.numpy as jnp
from jax import lax
from jax.experimental import pallas as pl
from jax.experimental.pallas import tpu as pltpu

F32 = jnp.float32
BF16 = jnp.bfloat16

D_MODEL = 1024
N_META = 16
DEPTH = 2
S5_WIDTH = 512
S5_GROUP = 16
S5_GROUPS = 32
S5_STATE = 64
S5_CHUNK = 16
SB_WIDTH = 512
SB_BLOCK = 128
SB_DEAD = -110.0
HG_HEADS = 8
HG_DK = 128
HG_CHUNK = 64
HG_UNROLL = 2
D_FF = 4096
ALPHA = (2.0 * DEPTH) ** 0.25
LN_EPS = 1e-5
RMS_EPS = 1e-6
ADAM_LR = 0.001
ADAM_B1 = 0.9
ADAM_B2 = 0.999
ADAM_EPS = 1e-08
ADAM_WD = 0.01
ADAM_STEP = 10
N_CHIPS = 4
N_DEV = 8
LANES = 128
MESH = pl.DeviceIdType.MESH
VMEM_LIMIT = 56 * 1024 * 1024
WHOLE_K_BYTES = 42 * 1024 * 1024
HGRN_BWD_VMEM = 60 * 1024 * 1024

WEIGHTS = ['meta', 'w_in_ab', 's5_lam_re', 's5_lam_im', 's5_log_dt', 's5_b_re', 's5_b_im', 's5_c_re', 's5_c_im',
           's5_d', 's5_w_glu', 's5_b_glu', 'w_out_ab', 'w_in_c', 'hgrn_gamma', 'hgrn_norm_g', 'w_out_c',
           'ln_mix_g', 'ln_mix_b', 'mlp_w_up', 'mlp_b_up', 'mlp_w_down', 'mlp_b_down', 'ln_mlp_g', 'ln_mlp_b']
BIG = ['w_in_ab', 's5_w_glu', 'w_out_ab', 'w_in_c', 'w_out_c', 'mlp_w_up', 'mlp_w_down']
SMALL = [n for n in WEIGHTS if n not in BIG]


def _params(sem=None):
    return pltpu.CompilerParams(dimension_semantics=sem, vmem_limit_bytes=VMEM_LIMIT)


def _pick(n, cands):
    for c in cands:
        if n % c == 0:
            return c
    return n


def _dot(a, b, ca, cb):
    return lax.dot_general(a, b, (((ca,), (cb,)), ((), ())), preferred_element_type=F32)


def _split3(x):
    hi = x.astype(BF16)
    r = x - hi.astype(F32)
    mid = r.astype(BF16)
    lo = (r - mid.astype(F32)).astype(BF16)
    return hi, mid, lo


def _dot_exact_rhs(x, m, cx, cm):
    hi = x.astype(BF16)
    lo = (x - hi.astype(F32)).astype(BF16)
    return _dot(hi, m, cx, cm) + _dot(lo, m, cx, cm)


def _dot3(a, b, ca, cb):
    ah = a.astype(BF16)
    al = (a - ah.astype(F32)).astype(BF16)
    bh = b.astype(BF16)
    bl = (b - bh.astype(F32)).astype(BF16)
    return _dot(ah, bh, ca, cb) + (_dot(ah, bl, ca, cb) + _dot(al, bh, ca, cb))


def rowwise(name, fn, row_ins, bc_ins, out_widths, sum_widths=(), out_dtypes=None, tm=None):
    t = row_ins[0].shape[0]
    if tm is None:
        wmax = max([a.shape[1] for a in row_ins] + list(out_widths))
        tm = _pick(t, (272, 256, 128, 64, 16, 8)) if wmax > 1024 else _pick(t, (544, 512, 256, 128, 64, 16, 8))
    nr, nb, no, ns = len(row_ins), len(bc_ins), len(out_widths), len(sum_widths)
    out_dtypes = out_dtypes or [F32] * no

    def body(*refs):
        i = pl.program_id(0)
        rows = [r[...] for r in refs[:nr]]
        bcs = [r[...] for r in refs[nr:nr + nb]]
        outs, sums = fn(i, tm, rows, bcs)
        for r, v in zip(refs[nr + nb:nr + nb + no], outs):
            r[...] = v.astype(r.dtype)
        if ns:
            srefs = refs[nr + nb + no:]

            @pl.when(i == 0)
            def _():
                for r in srefs:
                    r[...] = jnp.zeros(r.shape, r.dtype)

            for r, v in zip(srefs, sums):
                r[...] += v

    in_specs = [pl.BlockSpec((tm, a.shape[1]), lambda i: (i, 0)) for a in row_ins]
    in_specs += [pl.BlockSpec(a.shape, lambda i: (0, 0)) for a in bc_ins]
    out_specs = [pl.BlockSpec((tm, w), lambda i: (i, 0)) for w in out_widths]
    out_specs += [pl.BlockSpec((1, w), lambda i: (0, 0)) for w in sum_widths]
    out_shape = [jax.ShapeDtypeStruct((t, w), dt) for w, dt in zip(out_widths, out_dtypes)]
    out_shape += [jax.ShapeDtypeStruct((1, w), F32) for w in sum_widths]
    res = pl.pallas_call(body, name=name, grid=(t // tm,), in_specs=in_specs, out_specs=out_specs,
                         out_shape=out_shape, compiler_params=_params(("arbitrary",)))(*row_ins, *bc_ins)
    return res


def _colsum(v):
    return jnp.sum(v, axis=0, keepdims=True)


def matmul(name, a, b, mode, *, bias=None, act=None, add=None, add_scale=1.0, relu2_of=None, colsum=False,
           out_dtype=F32, out_slots=0, tm=None, tn=None, tk=None):
    slotted = b.ndim == 3
    if mode == 'nn':
        m, k = a.shape
        n = b.shape[-1] * (b.shape[0] if slotted else 1)
    elif mode == 'nt':
        m, k = a.shape
        n = b.shape[-2]
    else:
        k, m = a.shape
        n = b.shape[-1]
    ns = b.shape[-1] if slotted else None
    if mode == 'tn':
        tm = tm or _pick(m, (512, 256, 128))
        nw = n if not out_slots else n // out_slots
        if tn is None and tk is None:
            for cand in (512, 256):
                if nw % cand == 0 and (2 * k * (tm * a.dtype.itemsize + cand * b.dtype.itemsize)
                                       + 2 * tm * cand * 4 <= WHOLE_K_BYTES):
                    tn, tk = cand, k
                    break
        tn = tn or _pick(nw, (1024, 512, 256, 128))
        tk = tk or _pick(k, (1088, 1024, 768, 512, 384, 256, 128))
    else:
        tm = tm or _pick(m, (1088, 1024, 768, 512, 384, 256, 128))
        tn = tn or _pick(n if not (slotted and mode == 'nn') else ns, (1024, 512, 256, 128))
        extra = sum(x is not None for x in (add, relu2_of)) * 4
        if tk is None and not slotted and (2 * (tm * k * a.dtype.itemsize + k * tn * b.dtype.itemsize)
                                           + 2 * tm * tn * (4 + extra) <= WHOLE_K_BYTES):
            tk = k
        tk = tk or _pick(k if not (slotted and mode == 'nt') else ns, (1024, 512, 256, 128))
    gm, gn, gk = m // tm, n // tn, k // tk

    if mode == 'nn':
        a_spec = pl.BlockSpec((tm, tk), lambda i, j, kk: (i, kk))
        if slotted:
            per = ns // tn
            b_spec = pl.BlockSpec((None, tk, tn), lambda i, j, kk: (j // per, kk, j % per))
        else:
            b_spec = pl.BlockSpec((tk, tn), lambda i, j, kk: (kk, j))
        ca, cb = 1, 0
    elif mode == 'nt':
        a_spec = pl.BlockSpec((tm, tk), lambda i, j, kk: (i, kk))
        if slotted:
            per = ns // tk
            b_spec = pl.BlockSpec((None, tn, tk), lambda i, j, kk: (kk // per, j, kk % per))
        else:
            b_spec = pl.BlockSpec((tn, tk), lambda i, j, kk: (j, kk))
        ca, cb = 1, 1
    else:
        a_spec = pl.BlockSpec((tk, tm), lambda i, j, kk: (kk, i))
        b_spec = pl.BlockSpec((tk, tn), lambda i, j, kk: (kk, j))
        ca, cb = 0, 0
    in_specs = [a_spec, b_spec]
    args = [a, b]
    if bias is not None:
        in_specs.append(pl.BlockSpec((1, tn), lambda i, j, kk: (0, j)))
        args.append(bias)
    if add is not None:
        in_specs.append(pl.BlockSpec((tm, tn), lambda i, j, kk: (i, j)))
        args.append(add)
    if relu2_of is not None:
        in_specs.append(pl.BlockSpec((tm, tn), lambda i, j, kk: (i, j)))
        args.append(relu2_of)
    if out_slots:
        per_o = (n // out_slots) // tn
        out_spec = pl.BlockSpec((None, tm, tn), lambda i, j, kk: (j // per_o, i, j % per_o))
        out_shape = jax.ShapeDtypeStruct((out_slots, m, n // out_slots), out_dtype)
    else:
        out_spec = pl.BlockSpec((tm, tn), lambda i, j, kk: (i, j))
        out_shape = jax.ShapeDtypeStruct((m, n), out_dtype)
    grid = (gm, gn, gk)
    if colsum:
        assert mode == 'tn'
        out_spec = [out_spec, pl.BlockSpec((1, tn), lambda i, j, kk: (0, j))]
        out_shape = [out_shape, jax.ShapeDtypeStruct((1, n), F32)]

        def swapped(spec):
            return pl.BlockSpec(spec.block_shape, functools.partial(lambda j, i, kk, f: f(i, j, kk), f=spec.index_map))
        in_specs = [swapped(sp) for sp in in_specs]
        out_spec = [swapped(sp) for sp in out_spec]
        grid = (gn, gm, gk)

    def body(*refs):
        a_ref, b_ref = refs[0], refs[1]
        pos = 2
        bias_ref = add_ref = hid_ref = cs_ref = None
        if bias is not None:
            bias_ref = refs[pos]
            pos += 1
        if add is not None:
            add_ref = refs[pos]
            pos += 1
        if relu2_of is not None:
            hid_ref = refs[pos]
            pos += 1
        o_ref = refs[pos]
        pos += 1
        if colsum:
            cs_ref = refs[pos]
            pos += 1
        acc_ref = refs[pos] if gk > 1 else None
        bt = b_ref[...]
        p = _dot(a_ref[...].astype(BF16), bt.astype(BF16), ca, cb)
        kk = pl.program_id(2)

        if colsum:
            @pl.when(jnp.logical_and(pl.program_id(1) == 0, kk == 0))
            def _():
                cs_ref[...] = _colsum(bt.astype(F32))

            @pl.when(jnp.logical_and(pl.program_id(1) == 0, kk > 0))
            def _():
                cs_ref[...] += _colsum(bt.astype(F32))

        def finish(r):
            if bias_ref is not None:
                r = r + bias_ref[...]
            if act == 'relu2':
                r = jnp.square(jnp.maximum(r, 0.0))
            if hid_ref is not None:
                r = r * (2.0 * jnp.sqrt(hid_ref[...].astype(F32)))
            if add_ref is not None:
                r = r + add_scale * add_ref[...]
            o_ref[...] = r.astype(o_ref.dtype)

        if gk == 1:
            finish(p)
        else:
            @pl.when(kk == 0)
            def _():
                acc_ref[...] = p

            @pl.when(kk > 0)
            def _():
                acc_ref[...] += p

            @pl.when(kk == gk - 1)
            def _():
                finish(acc_ref[...])

    scratch = [pltpu.VMEM((tm, tn), F32)] if gk > 1 else []
    sem = ("arbitrary",) * 3 if colsum else ("parallel", "parallel", "arbitrary")
    return pl.pallas_call(body, name=name, grid=grid, in_specs=in_specs, out_specs=out_spec,
                          out_shape=out_shape, scratch_shapes=scratch, compiler_params=_params(sem))(*args)


def ln_fwd(name, h, mix, g, b):
    def fn(i, tm, rows, bcs):
        pre = ALPHA * rows[0] + rows[1]
        mu = jnp.mean(pre, axis=1, keepdims=True)
        xc = pre - mu
        var = jnp.mean(xc * xc, axis=1, keepdims=True)
        rstd = lax.rsqrt(var + LN_EPS)
        xhat = xc * rstd
        return (xhat * bcs[0] + bcs[1], xhat, rstd), ()
    return rowwise(name, fn, [h, mix], [g, b], [D_MODEL, D_MODEL, 1])


def ln_bwd(name, dy, xhat, rstd, g):
    def fn(i, tm, rows, bcs):
        dy_, xh, rs = rows
        dyg = dy_ * bcs[0]
        m1 = jnp.mean(dyg, axis=1, keepdims=True)
        m2 = jnp.mean(dyg * xh, axis=1, keepdims=True)
        dpre = rs * (dyg - m1 - xh * m2)
        return (dpre,), (_colsum(dy_ * xh), _colsum(dy_), _colsum(dpre))
    return rowwise(name, fn, [dy, xhat, rstd], [g], [D_MODEL], [D_MODEL] * 3)


_GELU_C = math.sqrt(2.0 / math.pi)


def gelu_fwd(name, x):
    def fn(i, tm, rows, bcs):
        v = rows[0]
        u = _GELU_C * (v + 0.044715 * (v * v * v))
        return (0.5 * v * (1.0 + jnp.tanh(u)),), ()
    return rowwise(name, fn, [x], [], [x.shape[1]])[0]


def gelu_bwd(name, dy, x):
    def fn(i, tm, rows, bcs):
        v = rows[1]
        th = jnp.tanh(_GELU_C * (v + 0.044715 * (v * v * v)))
        dg = 0.5 * (1.0 + th) + 0.5 * v * (1.0 - th * th) * (_GELU_C * (1.0 + 3.0 * 0.044715 * v * v))
        return (rows[0] * dg,), ()
    return rowwise(name, fn, [dy, x], [], [x.shape[1]])[0]


def glu_gate(name, y, gp):
    def fn(i, tm, rows, bcs):
        return (rows[0] * jax.nn.sigmoid(rows[1]),), ()
    return rowwise(name, fn, [y, gp], [], [y.shape[1]], out_dtypes=[BF16])[0]


def glu_gate_bwd(name, da, y, gp):
    def fn(i, tm, rows, bcs):
        s = jax.nn.sigmoid(rows[2])
        dgp = rows[0] * rows[1] * s * (1.0 - s)
        return (dgp, rows[0] * s), (_colsum(dgp),)
    w = y.shape[1]
    return rowwise(name, fn, [da, y, gp], [], [w, w], [w])


def ln_loss(name, h, mix, g, b, tgt, lp, seq):
    def fn(i, tm, rows, bcs):
        pre = ALPHA * rows[0] + rows[1]
        mu = jnp.mean(pre, axis=1, keepdims=True)
        xc = pre - mu
        var = jnp.mean(xc * xc, axis=1, keepdims=True)
        rstd = lax.rsqrt(var + LN_EPS)
        xhat = xc * rstd
        pos = (i * tm + lax.broadcasted_iota(jnp.int32, (tm, 1), 0)) % lp
        valid = jnp.logical_and(pos >= N_META, pos < N_META + seq)
        err = jnp.where(valid, (xhat * bcs[0] + bcs[1]) - rows[2], 0.0)
        part = 0.5 * jnp.sum(jnp.mean(err * err, axis=1, keepdims=True), axis=0, keepdims=True)
        return (xhat, rstd, err * (1.0 / D_MODEL)), (jnp.broadcast_to(part, (1, LANES)),)
    return rowwise(name, fn, [h, mix, tgt], [g, b], [D_MODEL, 1, D_MODEL], [LANES])


def adamw(name, w, gs, m, v):
    ng = len(gs)

    def fn(i, tm, rows, bcs):
        w_ = rows[0]
        g = rows[1] if ng == 1 else rows[1] + rows[2]
        m_, v_ = rows[1 + ng], rows[2 + ng]
        m_ = ADAM_B1 * m_ + (1.0 - ADAM_B1) * g
        v_ = ADAM_B2 * v_ + (1.0 - ADAM_B2) * jnp.square(g)
        m_hat = m_ / (1.0 - ADAM_B1 ** ADAM_STEP)
        v_hat = v_ / (1.0 - ADAM_B2 ** ADAM_STEP)
        delta = -ADAM_LR * (m_hat / (jnp.sqrt(v_hat) + ADAM_EPS) + ADAM_WD * w_)
        return (g, delta, m_, v_), ()
    wd = w.shape[1]
    return rowwise(name, fn, [w] + list(gs) + [m, v], [], [wd] * 4)


def s5_matrices(lam_re, lam_im, log_dt, b_re, b_im, c_re, c_im, d):
    c, hh, gg, pp = S5_CHUNK, S5_GROUP, S5_GROUPS, S5_STATE
    dt = jnp.exp(log_dt)[:, None]
    tau = jnp.arange(c + 1, dtype=F32)[None, :, None]
    mag = jnp.exp(tau * (lam_re * dt)[:, None, :])
    ang = tau * (lam_im * dt)[:, None, :]
    pw_re, pw_im = mag * jnp.cos(ang), mag * jnp.sin(ang)
    nr, ni = pw_re[:, 1] - 1.0, pw_im[:, 1]
    den = lam_re * lam_re + lam_im * lam_im
    cf_re = (nr * lam_re + ni * lam_im) / den
    cf_im = (ni * lam_re - nr * lam_im) / den
    bb_re = cf_re[:, :, None] * b_re - cf_im[:, :, None] * b_im
    bb_im = cf_re[:, :, None] * b_im + cf_im[:, :, None] * b_re
    ct_re, ct_im = c_re.transpose(0, 2, 1)[:, :, None, :], c_im.transpose(0, 2, 1)[:, :, None, :]
    pt_re, pt_im = pw_re.transpose(0, 2, 1)[:, :, :, None], pw_im.transpose(0, 2, 1)[:, :, :, None]
    cp_re = ct_re * pt_re - ct_im * pt_im
    cp_im = ct_re * pt_im + ct_im * pt_re
    hp = lax.Precision.HIGHEST
    kmat = (jnp.einsum('gpk,gpn->gkn', bb_re, cp_re[:, :, :c].reshape(gg, pp, c * hh), precision=hp)
            - jnp.einsum('gpk,gpn->gkn', bb_im, cp_im[:, :, :c].reshape(gg, pp, c * hh), precision=hp))
    rows = [jnp.pad(kmat[:, :, :(c - j) * hh], ((0, 0), (0, 0), (j * hh, 0))) for j in range(c)]
    a1 = jnp.stack(rows, axis=1).reshape(gg, c * hh, c * hh)
    a1 = a1 + jnp.eye(c * hh, dtype=F32)[None] * jnp.tile(d, (1, c))[:, None, :]
    a2 = jnp.concatenate([cp_re[:, :, 1:].reshape(gg, pp, c * hh), -cp_im[:, :, 1:].reshape(gg, pp, c * hh)], axis=1)
    rv_re, rv_im = pw_re[:, :c][:, ::-1, None, :], pw_im[:, :c][:, ::-1, None, :]
    bt_re, bt_im = bb_re.transpose(0, 2, 1)[:, None], bb_im.transpose(0, 2, 1)[:, None]
    a3 = jnp.concatenate([rv_re * bt_re - rv_im * bt_im, rv_re * bt_im + rv_im * bt_re], axis=-1)
    a3 = a3.reshape(gg, c * hh, 2 * pp)
    are = jnp.concatenate([pw_re[:, c], pw_re[:, c]], axis=-1)[:, None, :]
    aim = jnp.concatenate([-pw_im[:, c], pw_im[:, c]], axis=-1)[:, None, :]
    return a1, a2, a3, are, aim


S5_LEVELS = 8


def s5_scan_powers(lam_re, lam_im, log_dt):
    dt = jnp.exp(log_dt)[:, None]
    e = (S5_CHUNK * 2.0 ** jnp.arange(S5_LEVELS, dtype=F32))[:, None, None]
    mag = jnp.exp(e * (lam_re * dt)[None])
    ang = e * (lam_im * dt)[None]
    pr, pi = mag * jnp.cos(ang), mag * jnp.sin(ang)
    pre = jnp.concatenate([pr, pr], axis=-1).transpose(1, 0, 2)
    pim = jnp.concatenate([-pi, pi], axis=-1).transpose(1, 0, 2)
    return pre, pim


def _s5_scan(x, pre, pim, reverse):
    n = x.shape[0]
    rowi = lax.broadcasted_iota(jnp.int32, (n, 1), 0)
    t = x
    for k in range(S5_LEVELS):
        shift = 2 ** k
        if shift >= n:
            break
        p_re, p_im = pre[k:k + 1, :], pim[k:k + 1, :]
        if reverse:
            far = jnp.where(rowi < n - shift, pltpu.roll(t, n - shift, 0), 0.0)
            t = t + (p_re * far + pltpu.roll(p_im * far, S5_STATE, 1))
        else:
            far = jnp.where(rowi >= shift, pltpu.roll(t, shift, 0), 0.0)
            t = t + (p_re * far + p_im * pltpu.roll(far, S5_STATE, 1))
    return t


def s5_fwd(ug, a1, a2, a3, pre, pim, nseq):
    g, nc, cw = ug.shape
    per = nc // nseq
    sw = 2 * S5_STATE
    assert per <= 2 ** S5_LEVELS

    def body(u_ref, a1_ref, a2_ref, a3_ref, pre_ref, pim_ref, y_ref, s_ref):
        u = u_ref[...]
        x = _dot3(u, a3_ref[...], 1, 0)
        first = lax.broadcasted_iota(jnp.int32, (per, 1), 0) == 0
        for b in range(nseq):
            t = _s5_scan(x[b * per:(b + 1) * per], pre_ref[...], pim_ref[...], False)
            s_ref[pl.ds(b * per, per), :] = jnp.where(first, 0.0, pltpu.roll(t, 1, 0))
        y_ref[...] = _dot3(u, a1_ref[...], 1, 0) + _dot3(s_ref[...], a2_ref[...], 1, 0)

    def spec(shape):
        return pl.BlockSpec((None,) + shape, lambda i: (i, 0, 0))
    lv = (S5_LEVELS, sw)
    return pl.pallas_call(
        body, name="s5_fwd", grid=(g,),
        in_specs=[spec((nc, cw)), spec((cw, cw)), spec((sw, cw)), spec((cw, sw)), spec(lv), spec(lv)],
        out_specs=[spec((nc, cw)), spec((nc, sw))],
        out_shape=[jax.ShapeDtypeStruct((g, nc, cw), F32), jax.ShapeDtypeStruct((g, nc, sw), F32)],
        compiler_params=_params(("parallel",)))(ug, a1, a2, a3, pre, pim)


def s5_bwd(dyg, ug, sst, a1, a2, a3, pre, pim, nseq, ex=None):
    g, nc, cw = ug.shape
    per = nc // nseq
    sw = 2 * S5_STATE
    split, ex_start, ex_finish, exkw = _hidden_exchange(ex, 8, 6, (g,))

    def body(*refs):
        ((dy_ref, u_ref, s_ref, a1_ref, a2_ref, a3_ref, pre_ref, pim_ref),
         (du_ref, da1_ref, da2_ref, da3_ref, dare_ref, daim_ref), ex_parts, (dx_ref,)) = split(refs)
        ex_start(ex_parts)
        dy = dy_ref[...]
        u = u_ref[...]
        gd = _dot3(dy, a2_ref[...], 1, 1)
        s_all = s_ref[...]
        da2_ref[...] = _dot3(s_all, dy, 0, 0)
        last = lax.broadcasted_iota(jnp.int32, (per, 1), 0) == per - 1
        for b in range(nseq):
            gs = _s5_scan(gd[b * per:(b + 1) * per], pre_ref[...], pim_ref[...], True)
            dx_ref[pl.ds(b * per, per), :] = jnp.where(last, 0.0, pltpu.roll(gs, per - 1, 0))
        dx = dx_ref[...]
        dare_ref[...] = _colsum(dx * s_all)
        daim_ref[...] = _colsum(dx * pltpu.roll(s_all, S5_STATE, 1))
        du_ref[...] = _dot3(dy, a1_ref[...], 1, 1) + _dot3(dx, a3_ref[...], 1, 1)
        da1_ref[...] = _dot3(u, dy, 0, 0)
        da3_ref[...] = _dot3(u, dx, 0, 0)
        ex_finish(ex_parts)

    def spec(shape):
        return pl.BlockSpec((None,) + shape, lambda i: (i, 0, 0))
    sds = jax.ShapeDtypeStruct
    return pl.pallas_call(
        body, name="s5_bwd", grid=(g,),
        in_specs=[spec((nc, cw)), spec((nc, cw)), spec((nc, sw)), spec((cw, cw)), spec((sw, cw)), spec((cw, sw)),
                  spec((S5_LEVELS, sw)), spec((S5_LEVELS, sw))] + exkw['in_specs'],
        out_specs=[spec((nc, cw)), spec((cw, cw)), spec((sw, cw)), spec((cw, sw)), spec((1, sw)), spec((1, sw))]
        + exkw['out_specs'],
        out_shape=[sds((g, nc, cw), F32), sds((g, cw, cw), F32), sds((g, sw, cw), F32), sds((g, cw, sw), F32),
                   sds((g, 1, sw), F32), sds((g, 1, sw), F32)] + exkw['out_shape'],
        input_output_aliases=exkw['aliases'], scratch_shapes=[pltpu.VMEM((nc, sw), F32)] + exkw['scratch'],
        compiler_params=_params(("arbitrary",)))(dyg, ug, sst, a1, a2, a3, pre, pim, *exkw['ins'])


def _to_groups(u):
    t = u.shape[0]
    nc = t // S5_CHUNK
    return u.reshape(nc, S5_CHUNK, S5_GROUPS, S5_GROUP).transpose(2, 0, 1, 3).reshape(S5_GROUPS, nc, -1)


def _from_groups(yg):
    g, nc, _ = yg.shape
    return yg.reshape(g, nc, S5_CHUNK, S5_GROUP).transpose(1, 2, 0, 3).reshape(nc * S5_CHUNK, g * S5_GROUP)


def _sb_masks():
    row = lax.broadcasted_iota(jnp.int32, (SB_BLOCK, SB_BLOCK), 0)
    col = lax.broadcasted_iota(jnp.int32, (SB_BLOCK, SB_BLOCK), 1)
    return row, col


def _sb_weights(z, vis, later_of, after):
    sp = jnp.maximum(z, 0.0) + jnp.log1p(jnp.exp(-jnp.abs(z)))
    lk = jnp.where(vis, -sp, 0.0)
    rs = jnp.sum(lk, axis=1, keepdims=True)
    later = _dot_exact_rhs(lk, after, 1, 0) + later_of(rs)
    lb = z - sp
    w = jnp.where(vis, jnp.exp(lb + later), 0.0)
    return w, rs, lb


def _hidden_exchange(ex, n_in, n_out, grid):
    ni, no = (len(ex.ins), len(ex.out_shapes)) if ex else (0, 0)

    def split(refs):
        a, b = n_in + ni, n_in + ni + n_out
        end = len(refs) - (2 if ex else 0)
        return refs[:n_in], refs[a:b], (refs[n_in:a], refs[b:b + no], refs[end:]), refs[b + no:end]

    def at_step(which, fn, parts):
        if ex is not None:
            ids = [pl.program_id(d) == (0 if which == 'first' else g - 1) for d, g in enumerate(grid)]
            cond = ids[0]
            for c in ids[1:]:
                cond = jnp.logical_and(cond, c)
            pl.when(cond)(lambda: fn(parts[0], parts[1], *parts[2]))

    anyspec = pl.BlockSpec(memory_space=pl.ANY)
    kw = dict(in_specs=[anyspec] * ni, out_specs=[anyspec] * no, out_shape=list(ex.out_shapes) if ex else [],
              aliases={n_in + i: n_out + j for i, j in (ex.aliases().items() if ex else ())},
              scratch=ex.sems() if ex else [], ins=list(ex.ins) if ex else [])
    start = functools.partial(at_step, 'first', ex.start if ex else None)
    finish = functools.partial(at_step, 'last', ex.finish if ex else None)
    return split, start, finish, kw


SB_TILES = 2
SB_HEADS = 2 * SB_TILES
SB_WD = SB_TILES * LANES


def _stack_heads(x):
    return jnp.concatenate([x] * SB_HEADS, axis=0)


def _stack_masks():
    row = lax.broadcasted_iota(jnp.int32, (SB_HEADS * SB_BLOCK, SB_WD), 0)
    col = lax.broadcasted_iota(jnp.int32, (SB_HEADS * SB_BLOCK, SB_WD), 1)
    rowk = lax.broadcasted_iota(jnp.int32, (SB_HEADS * SB_BLOCK, SB_BLOCK), 0)
    colk = lax.broadcasted_iota(jnp.int32, (SB_HEADS * SB_BLOCK, SB_BLOCK), 1)
    return (col // 64) == (row // SB_BLOCK), colk < (rowk % SB_BLOCK)


def _own_lanes(r):
    head = lax.broadcasted_iota(jnp.int32, (SB_BLOCK, SB_WD), 1) // 64
    out = r[:SB_BLOCK]
    for h in range(1, SB_HEADS):
        out = jnp.where(head == h, r[h * SB_BLOCK:(h + 1) * SB_BLOCK], out)
    return out


def _sb_specs(lp, nseq):
    wd = SB_TILES * LANES
    off = [(S5_WIDTH + i * SB_WIDTH) // wd for i in range(3)]
    blk = (lp, wd)
    qkv = [pl.BlockSpec(blk, functools.partial(lambda b, h, o: (b, o + h), o=o)) for o in off]
    return (blk, qkv, pl.BlockSpec(blk, lambda b, h: (b, h)),
            pl.BlockSpec((None, None, 8, LANES), lambda b, h: (b, h, 0, 0)))


def attn_fwd(proj, nseq, lp, ex=None):
    nqb = lp // SB_BLOCK
    t = proj.shape[0]
    nstep = SB_WIDTH // (SB_TILES * LANES)
    split, ex_start, ex_finish, exkw = _hidden_exchange(ex, 3, 3, (nseq, nstep))

    def body(*refs):
        (q_ref, k_ref, v_ref), (o_ref, tot_ref, first_ref), ex_parts, _ = split(refs)
        ex_start(ex_parts)
        row, col = _sb_masks()
        after = (row > col).astype(BF16)
        lane8 = lax.broadcasted_iota(jnp.int32, (8, LANES), 1)
        hms, tris = _stack_masks()

        def qloop(qi, firsts):
            q0 = pl.multiple_of(qi * SB_BLOCK, SB_BLOCK)
            qm = jnp.where(hms, _stack_heads(q_ref[pl.ds(q0, SB_BLOCK), :] * 0.125), 0.0).astype(BF16)

            def alive(carry):
                return jnp.logical_and(carry[0] <= qi, carry[1] > 0)

            def kstep(carry):
                s, _, acc, lat = carry
                kj = qi - s
                k0 = pl.multiple_of(kj * SB_BLOCK, SB_BLOCK)
                k = k_ref[pl.ds(k0, SB_BLOCK), :].astype(BF16)
                v = v_ref[pl.ds(k0, SB_BLOCK), :].astype(BF16)
                vis = jnp.logical_or(kj < qi, tris)
                w, rs, _ = _sb_weights(_dot(qm, k, 1, 1), vis, functools.partial(lambda rs, lat: lat, lat=lat), after)
                acc = acc + _own_lanes(_dot(w.astype(BF16), v, 1, 0))
                lat = lat + rs
                return s + 1, (jnp.max(lat) > SB_DEAD).astype(jnp.int32), acc, lat

            res = lax.while_loop(alive, kstep, (jnp.int32(0), jnp.int32(1), jnp.zeros((SB_BLOCK, SB_WD), F32),
                                                jnp.zeros((SB_HEADS * SB_BLOCK, 1), F32)))
            o_ref[pl.ds(q0, SB_BLOCK), :] = res[2].astype(o_ref.dtype)
            tot_ref[pl.ds(q0, SB_BLOCK), :] = _own_lanes(jnp.broadcast_to(res[3], (SB_HEADS * SB_BLOCK, SB_WD)))
            return jnp.where(lane8 == qi, (qi - res[0] + 1).astype(F32), firsts)

        first_ref[...] = lax.fori_loop(0, nqb, qloop, jnp.zeros((8, LANES), F32))
        ex_finish(ex_parts)

    blk, qkv, out, vec = _sb_specs(lp, nseq)
    return pl.pallas_call(
        body, name="attn_fwd", grid=(nseq, nstep),
        in_specs=qkv + exkw['in_specs'], out_specs=[out, out, vec] + exkw['out_specs'],
        out_shape=[jax.ShapeDtypeStruct((t, SB_WIDTH), BF16), jax.ShapeDtypeStruct((t, SB_WIDTH), F32)]
        + [jax.ShapeDtypeStruct((nseq, nstep, 8, LANES), F32)] + exkw['out_shape'],
        input_output_aliases=exkw['aliases'], scratch_shapes=exkw['scratch'],
        compiler_params=_params(("arbitrary", "arbitrary")))(proj, proj, proj, *exkw['ins'])


def attn_bwd(proj, tot, first, do, nseq, lp, ex=None):
    nqb = lp // SB_BLOCK
    t = proj.shape[0]
    nstep = SB_WIDTH // (SB_TILES * LANES)
    split, ex_start, ex_finish, exkw = _hidden_exchange(ex, 6, 3, (nseq, nstep))

    def body(*refs):
        (q_ref, k_ref, v_ref, tot_ref, first_ref, do_ref), (dq_ref, dk_ref, dv_ref), ex_parts, _ = split(refs)
        ex_start(ex_parts)
        row, col = _sb_masks()
        after = (row > col).astype(BF16)
        before = (row < col).astype(BF16)
        lane8 = lax.broadcasted_iota(jnp.int32, (8, LANES), 1)
        firsts = first_ref[...]
        dk_ref[...] = jnp.zeros(dk_ref.shape, F32)
        dv_ref[...] = jnp.zeros(dv_ref.shape, F32)
        hms, tris = _stack_masks()

        def qloop(qi, _):
            first = jnp.max(jnp.where(lane8 == qi, firsts, 0.0)).astype(jnp.int32)
            first = jnp.clip(first, 0, qi)
            q0 = pl.multiple_of(qi * SB_BLOCK, SB_BLOCK)
            qm = jnp.where(hms, _stack_heads(q_ref[pl.ds(q0, SB_BLOCK), :] * 0.125), 0.0).astype(BF16)
            dom = jnp.where(hms, _stack_heads(do_ref[pl.ds(q0, SB_BLOCK), :]), 0.0).astype(BF16)
            tot = jnp.max(jnp.where(hms, _stack_heads(tot_ref[pl.ds(q0, SB_BLOCK), :]), -jnp.inf),
                          axis=1, keepdims=True)

            def kloop(kj, carry):
                dq, pre, gpre = carry
                k0 = pl.multiple_of(kj * SB_BLOCK, SB_BLOCK)
                k = k_ref[pl.ds(k0, SB_BLOCK), :].astype(BF16)
                v = v_ref[pl.ds(k0, SB_BLOCK), :].astype(BF16)
                vis = jnp.logical_or(kj < qi, tris)
                later_of = functools.partial(lambda rs, t_, p_: t_ - p_ - rs, t_=tot, p_=pre)
                w, rs, lb = _sb_weights(_dot(qm, k, 1, 1), vis, later_of, after)
                g = _dot(dom, v, 1, 1) * w
                dlk = _dot_exact_rhs(g, before, 1, 0) + gpre
                sig = jnp.exp(lb)
                dz = jnp.where(vis, g * (1.0 - sig) - dlk * sig, 0.0).astype(BF16)
                dk_ref[pl.ds(k0, SB_BLOCK), :] += _dot(dz, qm, 0, 0)
                dv_ref[pl.ds(k0, SB_BLOCK), :] += _dot(w.astype(BF16), dom, 0, 0)
                return dq + _own_lanes(_dot(dz, k, 1, 0)), pre + rs, gpre + jnp.sum(g, axis=1, keepdims=True)

            z1 = jnp.zeros((SB_HEADS * SB_BLOCK, 1), F32)
            res = lax.fori_loop(first, qi + 1, kloop, (jnp.zeros((SB_BLOCK, SB_WD), F32), z1, z1))
            dq_ref[pl.ds(q0, SB_BLOCK), :] = res[0] * 0.125
            return 0

        lax.fori_loop(0, nqb, qloop, 0)
        ex_finish(ex_parts)

    blk, qkv, out, vec = _sb_specs(lp, nseq)
    return pl.pallas_call(
        body, name="attn_bwd", grid=(nseq, nstep),
        in_specs=qkv + [out, vec, out] + exkw['in_specs'], out_specs=[out] * 3 + exkw['out_specs'],
        out_shape=[jax.ShapeDtypeStruct((t, SB_WIDTH), F32)] * 3 + exkw['out_shape'],
        input_output_aliases=exkw['aliases'], scratch_shapes=exkw['scratch'],
        compiler_params=_params(("arbitrary", "arbitrary")))(proj, proj, proj, tot, first, do, *exkw['ins'])


def _hg_loop(nch, step, init):
    assert nch % HG_UNROLL == 0

    def trip(i, carry):
        for u in range(HG_UNROLL):
            carry = step(i * HG_UNROLL + u, carry)
        return carry
    return lax.fori_loop(0, nch // HG_UNROLL, trip, init)


def _hg_gates(fc, lb):
    sg = jax.nn.sigmoid(fc)
    f = lb + (1.0 - lb) * sg
    return sg, f


def _dot_exact_lhs(m, x):
    hi, mid, lo = _split3(x)
    return _dot(m, hi, 1, 0) + (_dot(m, mid, 1, 0) + _dot(m, lo, 1, 0))


def hgrn_fwd(proj, lb, ng, nseq, lp, ex=None):
    nch = lp // HG_CHUNK
    t = proj.shape[0]
    c = HG_CHUNK

    split, ex_start, ex_finish, exkw = _hidden_exchange(ex, 6, 3, (HG_HEADS,))

    def body(*refs):
        (q_ref, f_ref, v_ref, g_ref, lb_ref, ng_ref), (o_ref, og_ref, st_ref), ex_parts, _ = split(refs)
        ex_start(ex_parts)
        lbv, ngv = lb_ref[...], ng_ref[...]

        rr = nseq * c
        seq_r = lax.broadcasted_iota(jnp.int32, (rr, 1), 0) // c
        bd_row = lax.broadcasted_iota(jnp.int32, (rr, rr), 0)
        bd_col = lax.broadcasted_iota(jnp.int32, (rr, rr), 1)
        causal = jnp.logical_and(bd_row // c == bd_col // c, bd_col <= bd_row)
        incl = causal.astype(BF16)

        def stacked(ref, ci):
            return jnp.concatenate([ref[pl.ds(pl.multiple_of(b * lp + ci * c, c), c), :] for b in range(nseq)], axis=0)

        def wide(x):
            return jnp.concatenate([jnp.where(seq_r == b, x, jnp.zeros_like(x)) for b in range(nseq)], axis=1)

        def step(ci, st):
            for b in range(nseq):
                st_ref[b, ci] = st[:, b * HG_DK:(b + 1) * HG_DK]
            _, f = _hg_gates(stacked(f_ref, ci), lbv)
            bcum = _dot_exact_lhs(incl, jnp.log(f))
            lasts = [bcum[b * c + c - 1:b * c + c, :] for b in range(nseq)]
            blast = jnp.concatenate([jnp.broadcast_to(x, (c, LANES)) for x in lasts], axis=0)
            kk = 1.0 - f
            vc = stacked(v_ref, ci).astype(BF16)
            qd = (stacked(q_ref, ci) * jnp.exp(bcum)).astype(BF16)
            kd = (kk * jnp.exp(-bcum)).astype(BF16)
            a = jnp.where(causal, _dot(qd, kd, 1, 1), 0.0)
            o = _dot(a.astype(BF16), vc, 1, 0) + _dot(wide(qd), st.astype(BF16), 1, 1)
            kend = (kk * jnp.exp(blast - bcum)).astype(BF16)
            st = st * jnp.exp(jnp.concatenate(lasts, axis=1)) + _dot(vc, wide(kend), 0, 0)
            r = lax.rsqrt(jnp.mean(o * o, axis=1, keepdims=True) + RMS_EPS)
            gc = stacked(g_ref, ci)
            og = (o * r * ngv * (gc * jax.nn.sigmoid(gc))).astype(og_ref.dtype)
            for b in range(nseq):
                rows = pl.ds(pl.multiple_of(b * lp + ci * c, c), c)
                o_ref[rows, :] = o[b * c:(b + 1) * c]
                og_ref[rows, :] = og[b * c:(b + 1) * c]
            return st

        _hg_loop(nch, step, jnp.zeros((HG_DK, nseq * HG_DK), F32))
        ex_finish(ex_parts)

    blk = (nseq * lp, LANES)
    h = HG_HEADS
    return pl.pallas_call(
        body, name="hgrn_fwd", grid=(h,),
        in_specs=[pl.BlockSpec(blk, lambda hh: (0, hh)),
                  pl.BlockSpec(blk, lambda hh: (0, h + hh)),
                  pl.BlockSpec(blk, lambda hh: (0, 2 * h + hh)),
                  pl.BlockSpec(blk, lambda hh: (0, 3 * h + hh)),
                  pl.BlockSpec((1, LANES), lambda hh: (0, hh)),
                  pl.BlockSpec((1, LANES), lambda hh: (0, hh))] + exkw['in_specs'],
        out_specs=[pl.BlockSpec(blk, lambda hh: (0, hh)),
                   pl.BlockSpec(blk, lambda hh: (0, hh)),
                   pl.BlockSpec((nseq, None, nch, HG_DK, HG_DK), lambda hh: (0, hh, 0, 0, 0))] + exkw['out_specs'],
        out_shape=[jax.ShapeDtypeStruct((t, D_MODEL), F32), jax.ShapeDtypeStruct((t, D_MODEL), BF16),
                   jax.ShapeDtypeStruct((nseq, h, nch, HG_DK, HG_DK), F32)] + exkw['out_shape'],
        input_output_aliases=exkw['aliases'], scratch_shapes=exkw['scratch'],
        compiler_params=_params(("arbitrary",)))(proj, proj, proj, proj, lb, ng, *exkw['ins'])


def hgrn_bwd(proj, o, dog, states, lb, ng, nseq, lp):
    nch = lp // HG_CHUNK
    t = proj.shape[0]
    c = HG_CHUNK
    rr = nseq * c

    def body(q_ref, f_ref, v_ref, g_ref, o_ref, dog_ref, st_ref, lb_ref, ng_ref,
             dq_ref, df_ref, dv_ref, dg_ref, dlb_ref, dng_ref):
        seq_r = lax.broadcasted_iota(jnp.int32, (rr, 1), 0) // c
        last = lax.broadcasted_iota(jnp.int32, (rr, 1), 0) % c == c - 1
        bd_row = lax.broadcasted_iota(jnp.int32, (rr, rr), 0)
        bd_col = lax.broadcasted_iota(jnp.int32, (rr, rr), 1)
        same = bd_row // c == bd_col // c
        causal = jnp.logical_and(same, bd_col <= bd_row)
        incl = causal.astype(BF16)
        from_here = jnp.logical_and(same, bd_col >= bd_row).astype(BF16)
        lbv, ngv = lb_ref[...], ng_ref[...]

        def stacked(ref, ci):
            return jnp.concatenate([ref[pl.ds(pl.multiple_of(b * lp + ci * c, c), c), :] for b in range(nseq)], axis=0)

        def wide(x):
            return jnp.concatenate([jnp.where(seq_r == b, x, jnp.zeros_like(x)) for b in range(nseq)], axis=1)

        def own(x):
            return functools.reduce(jnp.add, [jnp.where(seq_r == b, x[:, b * LANES:(b + 1) * LANES], 0.0)
                                              for b in range(nseq)])

        def per_seq_rows(vals):
            return jnp.concatenate([jnp.broadcast_to(x, (c, LANES)) for x in vals], axis=0)

        def step(s, carry):
            dst, dlb, dng = carry
            ci = nch - 1 - s
            oc, gc, dogc = stacked(o_ref, ci), stacked(g_ref, ci), stacked(dog_ref, ci)
            r = lax.rsqrt(jnp.mean(oc * oc, axis=1, keepdims=True) + RMS_EPS)
            sgg = jax.nn.sigmoid(gc)
            dgc = dogc * (oc * r * ngv) * (sgg * (1.0 + gc * (1.0 - sgg)))
            don = dogc * (gc * sgg)
            dng = dng + _colsum(don * oc * r)
            tt = don * ngv
            do = r * (tt - oc * (r * r) * jnp.mean(tt * oc, axis=1, keepdims=True))
            st = jnp.concatenate([st_ref[b, ci] for b in range(nseq)], axis=1)
            sg, f = _hg_gates(stacked(f_ref, ci), lbv)
            bcum = _dot_exact_lhs(incl, jnp.log(f))
            lasts = [bcum[b * c + c - 1:b * c + c, :] for b in range(nseq)]
            blast = per_seq_rows(lasts)
            kk = 1.0 - f
            qc, vf = stacked(q_ref, ci), stacked(v_ref, ci)
            pdec = jnp.exp(bcum)
            ndec = jnp.exp(-bcum)
            edec = jnp.exp(blast - bcum)
            eb = jnp.exp(jnp.concatenate(lasts, axis=1))
            qd_f, kd_f, kend_f = qc * pdec, kk * ndec, kk * edec
            qd, kd, kend = qd_f.astype(BF16), kd_f.astype(BF16), kend_f.astype(BF16)
            vc, dob, stb, dstb = vf.astype(BF16), do.astype(BF16), st.astype(BF16), dst.astype(BF16)
            a = jnp.where(causal, _dot(qd, kd, 1, 1), 0.0).astype(BF16)
            da = jnp.where(causal, _dot(dob, vc, 1, 1), 0.0).astype(BF16)
            dv = _dot(a, dob, 0, 0) + _dot(wide(kend), dstb, 1, 1)
            dqd = _dot(da, kd, 1, 0) + own(_dot(dob, stb, 1, 0))
            dkd = _dot(da, qd, 0, 0)
            dkend = own(_dot(vc, dstb, 1, 0))
            sts = _colsum(dst * st)
            dkk = dkend * kend_f
            dblast = per_seq_rows([_colsum(jnp.where(seq_r == b, dkk, 0.0))
                                   + eb[:, b * LANES:(b + 1) * LANES] * sts[:, b * LANES:(b + 1) * LANES]
                                   for b in range(nseq)])
            dbcum = dqd * qd_f - dkd * kd_f - dkk
            dbcum = dbcum + jnp.where(last, dblast, 0.0)
            dlf = _dot_exact_lhs(from_here, dbcum)
            df = dlf / f - (dkd * ndec + dkend * edec)
            dfp = df * (1.0 - lbv) * sg * (1.0 - sg)
            dqc = dqd * pdec
            for b in range(nseq):
                rows = pl.ds(pl.multiple_of(b * lp + ci * c, c), c)
                blk_rows = slice(b * c, (b + 1) * c)
                dg_ref[rows, :], dv_ref[rows, :] = dgc[blk_rows], dv[blk_rows]
                dq_ref[rows, :], df_ref[rows, :] = dqc[blk_rows], dfp[blk_rows]
            dlb = dlb + _colsum(df * (1.0 - sg))
            dst = dst * eb + _dot(dob, wide(qd), 0, 0)
            return dst, dlb, dng

        z = jnp.zeros((1, LANES), F32)
        _, dlb, dng = _hg_loop(nch, step, (jnp.zeros((HG_DK, nseq * HG_DK), F32), z, z))
        dlb_ref[...] = dlb
        dng_ref[...] = dng

    blk = (nseq * lp, LANES)
    h = HG_HEADS
    vec = pl.BlockSpec((1, LANES), lambda hh: (0, hh))
    col = pl.BlockSpec(blk, lambda hh: (0, hh))
    return pl.pallas_call(
        body, name="hgrn_bwd", grid=(h,),
        in_specs=[col,
                  pl.BlockSpec(blk, lambda hh: (0, h + hh)),
                  pl.BlockSpec(blk, lambda hh: (0, 2 * h + hh)),
                  pl.BlockSpec(blk, lambda hh: (0, 3 * h + hh)),
                  col, col,
                  pl.BlockSpec((nseq, None, nch, HG_DK, HG_DK), lambda hh: (0, hh, 0, 0, 0)),
                  vec, vec],
        out_specs=[col] * 4 + [vec, vec],
        out_shape=[jax.ShapeDtypeStruct((t, D_MODEL), F32)] * 4 + [jax.ShapeDtypeStruct((1, D_MODEL), F32)] * 2,
        compiler_params=pltpu.CompilerParams(dimension_semantics=("parallel",), vmem_limit_bytes=HGRN_BWD_VMEM))(
            proj, proj, proj, proj, o, dog, states, lb, ng)


def _lower_bound(gamma):
    p = jax.nn.softmax(gamma, axis=0)
    return (jnp.cumsum(p, axis=0) - p[0])[1][None, :]


def local_step(x, tgt, w, hooks=None):
    nseq, seq, _ = x.shape
    lp = -(-(N_META + seq) // SB_BLOCK) * SB_BLOCK
    t = nseq * lp
    pad = lp - N_META - seq
    h0 = jnp.concatenate([jnp.broadcast_to(w['meta'][None], (nseq, N_META, D_MODEL)), x,
                          jnp.zeros((nseq, pad, D_MODEL), F32)], axis=1).reshape(t, D_MODEL)
    tgt_p = jnp.pad(tgt, ((0, 0), (N_META, pad), (0, 0))).reshape(t, D_MODEL)
    row = lambda v: v.reshape(1, -1)
    g = {}

    proj = matmul("in_ab", h0, w['w_in_ab'], 'nn')
    att = attn_fwd(proj, nseq, lp, ex=hooks.gather_late if hooks else None)
    b_out, sb_tot, sb_first = att[:3]
    if hooks:
        w = {**w, **hooks.late_weights(att[3:])}
    s5_names = ['s5_lam_re', 's5_lam_im', 's5_log_dt', 's5_b_re', 's5_b_im', 's5_c_re', 's5_c_im', 's5_d']
    mats, mats_vjp = jax.vjp(s5_matrices, *[w[n][0] for n in s5_names])
    ug = _to_groups(proj[:, :S5_WIDTH])
    s5_pows = s5_scan_powers(*[w[n][0] for n in s5_names[:3]])
    yg, sst = s5_fwd(ug, *mats[:3], *s5_pows, nseq)
    ys5 = _from_groups(yg)
    y = gelu_fwd("gelu", ys5)
    gp = matmul("glu", y, w['s5_w_glu'], 'nn', bias=w['s5_b_glu'])
    a_out = glu_gate("glu_gate", y, gp)
    cat = jnp.concatenate([a_out, b_out], axis=1)
    mix0 = matmul("out_ab", cat, w['w_out_ab'], 'nn')
    h1, xh1, rs1 = ln_fwd("ln_mix0", h0, mix0, row(w['ln_mix_g'][0]), row(w['ln_mix_b'][0]))
    hid0 = matmul("up0", h1, w['mlp_w_up'][0], 'nn', bias=row(w['mlp_b_up'][0]), act='relu2', out_dtype=BF16)
    dn0 = matmul("down0", hid0, w['mlp_w_down'][0], 'nn', bias=row(w['mlp_b_down'][0]))
    h2, xh2, rs2 = ln_fwd("ln_mlp0", h1, dn0, row(w['ln_mlp_g'][0]), row(w['ln_mlp_b'][0]))

    projc = matmul("in_c", h2, w['w_in_c'], 'nn')
    lb, lb_vjp = jax.vjp(_lower_bound, w['hgrn_gamma'])
    ng = w['hgrn_norm_g']
    res = hgrn_fwd(projc, lb, ng, nseq, lp, ex=hooks.gather_last if hooks else None)
    o, og, states = res[:3]
    if hooks:
        w = {**w, **hooks.last_weights(res[3:])}
    mix1 = matmul("out_c", og, w['w_out_c'], 'nn')
    h3, xh3, rs3 = ln_fwd("ln_mix1", h2, mix1, row(w['ln_mix_g'][1]), row(w['ln_mix_b'][1]))
    hid1 = matmul("up1", h3, w['mlp_w_up'][1], 'nn', bias=row(w['mlp_b_up'][1]), act='relu2', out_dtype=BF16)
    dn1 = matmul("down1", hid1, w['mlp_w_down'][1], 'nn', bias=row(w['mlp_b_down'][1]))
    xh4, rs4, dy, loss = ln_loss("ln_mlp1_loss", h3, dn1, row(w['ln_mlp_g'][1]), row(w['ln_mlp_b'][1]), tgt_p, lp, seq)

    def mlp_bwd(l, dh_out, xh_out, rs_out, hid, h_in):
        dpre, dg_, db_, dbd = ln_bwd(f"ln_mlp{l}_b", dh_out, xh_out, rs_out, row(w['ln_mlp_g'][l]))
        dpu = matmul(f"down{l}_dx", dpre, w['mlp_w_down'][l], 'nt', relu2_of=hid, out_dtype=BF16)
        dwd = matmul(f"down{l}_dw", hid, dpre, 'tn')
        dh_in = matmul(f"up{l}_dx", dpu, w['mlp_w_up'][l], 'nt', add=dpre, add_scale=ALPHA)
        dwu, dbu = matmul(f"up{l}_dw", h_in, dpu, 'tn', out_slots=N_CHIPS, colsum=True)
        return dh_in, dict(ln_mlp_g=dg_, ln_mlp_b=db_, mlp_b_down=dbd, mlp_b_up=dbu, mlp_w_up=dwu, mlp_w_down=dwd)

    dh3, gm1 = mlp_bwd(1, dy, xh4, rs4, hid1, h3)
    dpre, dg1, db1, _ = ln_bwd("ln_mix1_b", dh3, xh3, rs3, row(w['ln_mix_g'][1]))
    g['w_out_c'] = matmul("out_c_dw", og, dpre, 'tn')
    dog = matmul("out_c_dx", dpre, w['w_out_c'], 'nt')
    dq, df, di, dgg, dlb, dng = hgrn_bwd(projc, o, dog, states, lb, ng, nseq, lp)
    dprojc = jnp.concatenate([dq, df, di, dgg], axis=1)
    dh2 = matmul("in_c_dx", dprojc, w['w_in_c'], 'nt', add=dpre, add_scale=ALPHA)
    g['w_in_c'] = matmul("in_c_dw", h2, dprojc, 'tn', out_slots=N_CHIPS)
    g['hgrn_gamma'] = lb_vjp(dlb)[0]
    g['hgrn_norm_g'] = dng

    dh1, gm0 = mlp_bwd(0, dh2, xh2, rs2, hid0, h1)
    dpre, dg0, db0, _ = ln_bwd("ln_mix0_b", dh1, xh1, rs1, row(w['ln_mix_g'][0]))
    g['w_out_ab'] = matmul("out_ab_dw", cat, dpre, 'tn')
    dcat = matmul("out_ab_dx", dpre, w['w_out_ab'], 'nt')
    dgp, da_s, dbglu = glu_gate_bwd("glu_gate_b", dcat[:, :S5_WIDTH], y, gp)
    g['s5_w_glu'] = matmul("glu_dw", y, dgp, 'tn')
    g['s5_b_glu'] = dbglu
    dy5 = matmul("glu_dx", dgp, w['s5_w_glu'], 'nt', add=da_s)
    dys5 = gelu_bwd("gelu_b", dy5, ys5)
    for n in ('mlp_w_up', 'mlp_w_down'):
        g[n] = [gm0[n], gm1[n]]
    g['ln_mix_g'] = jnp.concatenate([dg0, dg1], axis=0)
    g['ln_mix_b'] = jnp.concatenate([db0, db1], axis=0)
    for n in ('ln_mlp_g', 'ln_mlp_b', 'mlp_b_down', 'mlp_b_up'):
        g[n] = jnp.concatenate([gm0[n], gm1[n]], axis=0)
    res = s5_bwd(_to_groups(dys5), ug, sst, *mats[:3], *s5_pows, nseq, ex=hooks.fold_early(g) if hooks else None)
    dug, da1, da2, da3, dare, daim = res[:6]
    for n, v in zip(s5_names, mats_vjp((da1, da2, da3, dare, daim))):
        g[n] = v[None]
    ex = hooks.scatter_early(g, loss, res[6:]) if hooks else None
    res = attn_bwd(proj, sb_tot, sb_first, dcat[:, S5_WIDTH:], nseq, lp, ex=ex)
    dqa, dka, dva = res[:3]
    if hooks:
        hooks.scattered(res[3:])
    dproj = jnp.concatenate([_from_groups(dug), dqa, dka, dva], axis=1)
    dh0 = matmul("in_ab_dx", dproj, w['w_in_ab'], 'nt', add=dpre, add_scale=ALPHA)
    g['w_in_ab'] = matmul("in_ab_dw", h0, dproj, 'tn', out_slots=N_CHIPS)

    dh0 = dh0.reshape(nseq, lp, D_MODEL)
    grad_x = dh0[:, N_META:N_META + seq]
    g['meta'] = jnp.sum(dh0[:, :N_META], axis=0)
    return loss, grad_x, g


class Exchange:
    def __init__(self, ins, out_shapes, n_remote, build, in_place):
        self.ins, self.out_shapes, self.n_remote, self.build, self.in_place = ins, out_shapes, n_remote, build, in_place

    def sems(self):
        return [pltpu.SemaphoreType.DMA((self.n_remote,)), pltpu.SemaphoreType.DMA((self.n_remote,))]

    def aliases(self):
        return self.in_place if isinstance(self.in_place, dict) else \
            {i: i for i in range(len(self.ins) if self.in_place else 0)}

    def beside(self, other):
        na, nao = len(self.ins), len(self.out_shapes)

        def build(in_refs, out_refs, me):
            pa = self.build(in_refs[:na], out_refs[:nao], me)
            pb = other.build(in_refs[na:], out_refs[nao:], me)
            n = max(len(pa), len(pb))
            return [(pa[i] if i < len(pa) else []) + (pb[i] if i < len(pb) else []) for i in range(n)]
        al = dict(self.aliases())
        al.update({na + i: nao + j for i, j in other.aliases().items()})
        return Exchange(list(self.ins) + list(other.ins), list(self.out_shapes) + list(other.out_shapes),
                        self.n_remote + other.n_remote, build, al)

    def phases(self, in_refs, out_refs, ssem, rsem):
        me = (lax.axis_index("x"), lax.axis_index("y"), lax.axis_index("c"))
        out, k = [], 0
        for phase in self.build(in_refs, out_refs, me):
            out.append(functools.partial(
                lambda phase, k: [pltpu.make_async_remote_copy(src_ref=s, dst_ref=d, send_sem=ssem.at[k + i],
                                                               recv_sem=rsem.at[k + i], device_id=p,
                                                               device_id_type=MESH)
                                  for i, (s, d, p) in enumerate(phase)], phase, k))
            k += len(phase)
        return out

    def start(self, in_refs, out_refs, ssem, rsem):
        for cp in self.phases(in_refs, out_refs, ssem, rsem)[0]():
            cp.start()

    def finish(self, in_refs, out_refs, ssem, rsem):
        phases = self.phases(in_refs, out_refs, ssem, rsem)
        for cp in phases[0]():
            cp.wait()
        for make in phases[1:]:
            copies = make()
            for cp in copies:
                cp.start()
            for cp in copies:
                cp.wait()


def _exchange(name, ex):
    ni, no = len(ex.ins), len(ex.out_shapes)

    def body(*refs):
        in_refs, out_refs, sems = refs[:ni], refs[ni:ni + no], refs[ni + no:]
        ex.start(in_refs, out_refs, *sems)
        ex.finish(in_refs, out_refs, *sems)

    anyspec = pl.BlockSpec(memory_space=pl.ANY)
    return pl.pallas_call(
        body, name=name, in_specs=[anyspec] * ni, out_specs=[anyspec] * no, out_shape=ex.out_shapes,
        input_output_aliases=ex.aliases(), scratch_shapes=ex.sems())(*ex.ins)


def _chip_peers(me):
    x, y, c = me
    return [(x ^ (m >> 1), y ^ (m & 1), c) for m in (1, 2, 3)]


def _half(ref, c, axis):
    h = ref.shape[axis] // 2
    idx = [slice(None)] * axis + [pl.ds(c * h, h)]
    return ref.at[tuple(idx)]


def _like(arrs):
    return [jax.ShapeDtypeStruct(a.shape, a.dtype) for a in arrs]


def gather_over_chips(bufs):
    def build(in_refs, out_refs, me):
        x, y, c = me
        j = 2 * x + y
        sib = (x, y, 1 - c)
        ici = [(_half(o.at[j], c, 0), _half(o.at[j], c, 0), p) for o in out_refs for p in _chip_peers(me)]
        d2d = [(_half(o.at[2 * p[0] + p[1]], c, 0), _half(o.at[2 * p[0] + p[1]], c, 0), sib)
               for o in out_refs for p in _chip_peers(me)]
        return [ici, d2d]
    return Exchange(bufs, _like(bufs), 6 * len(bufs), build, True)


def fold_cores(fulls):
    def build(in_refs, out_refs, me):
        x, y, c = me
        return [[(_half(s, 1 - c, 1), o, (x, y, 1 - c)) for s, o in zip(in_refs, out_refs)]]
    shapes = [jax.ShapeDtypeStruct((f.shape[0], f.shape[1] // 2, f.shape[2]), f.dtype) for f in fulls]
    return Exchange(fulls, shapes, len(fulls), build, False)


def scatter_over_chips(parts):
    def build(in_refs, out_refs, me):
        j = 2 * me[0] + me[1]
        return [[(s.at[2 * p[0] + p[1]], o.at[j], p) for s, o in zip(in_refs, out_refs) for p in _chip_peers(me)]]
    return Exchange(parts, _like(parts), 3 * len(parts), build, False)


def join_cores(bufs):
    def build(in_refs, out_refs, me):
        x, y, c = me
        return [[(o.at[c], o.at[c], (x, y, 1 - c)) for o in out_refs]]
    return Exchange(bufs, _like(bufs), len(bufs), build, True)


def gather_over_devices(buf):
    def build(in_refs, out_refs, me):
        x, y, c = me
        out = out_refs[0]
        sib = (x, y, 1 - c)
        mine = out.at[4 * x + 2 * y + c]
        first = [(mine, mine, sib)] + [(mine, mine, p) for p in _chip_peers(me)]
        second = [(out.at[4 * p[0] + 2 * p[1] + c], out.at[4 * p[0] + 2 * p[1] + c], sib) for p in _chip_peers(me)]
        return [first, second]
    return Exchange([buf], _like([buf]), N_DEV - 1, build, True)


def _slot_call(name, body, scalars, ins, in_maps, out_shapes, out_maps, blk, grid):
    gs = pltpu.PrefetchScalarGridSpec(
        num_scalar_prefetch=1, grid=grid,
        in_specs=[pl.BlockSpec((None,) + blk, m) for m in in_maps],
        out_specs=[pl.BlockSpec((None,) + blk, m) for m in out_maps])
    return pl.pallas_call(body, name=name, grid_spec=gs, out_shape=out_shapes,
                          compiler_params=_params(("arbitrary",) * len(grid)))(scalars, *ins)


def _group_rows(r, n):
    return _pick(r, (512, 256, 128, 64, 32, 16) if n <= 2 else (256, 128, 64, 32, 16))


def cast_into_slot(name, ws, chip, dtype):
    n = len(ws)
    r, wd = ws[0].shape
    tm = _group_rows(r, n)

    def body(s_ref, *refs):
        for w_ref, o_ref in zip(refs[:n], refs[n:]):
            o_ref[...] = w_ref[...].astype(o_ref.dtype)

    gs = pltpu.PrefetchScalarGridSpec(
        num_scalar_prefetch=1, grid=(r // tm,),
        in_specs=[pl.BlockSpec((tm, wd), lambda i, s: (i, 0))] * n,
        out_specs=[pl.BlockSpec((None, tm, wd), lambda i, s: (s[0], i, 0))] * n)
    return pl.pallas_call(body, name=name, grid_spec=gs,
                          out_shape=[jax.ShapeDtypeStruct((N_CHIPS, r, wd), dtype)] * n,
                          compiler_params=_params(("arbitrary",)))(chip.reshape(1), *ws)


def sum_cores(name, fulls, theirs, core):
    n = len(fulls)
    s, r, wd = fulls[0].shape
    h = r // 2
    tm = _group_rows(h, n)
    nt = h // tm

    def body(s_ref, *refs):
        for k in range(n):
            refs[2 * n + k][...] = (refs[2 * k][...] + refs[2 * k + 1][...]).astype(BF16)

    ins = [x for pair in zip(fulls, theirs) for x in pair]
    return _slot_call(name, body, core.reshape(1), ins,
                      [lambda k, i, s_: (k, s_[0] * nt + i, 0), lambda k, i, s_: (k, i, 0)] * n,
                      [jax.ShapeDtypeStruct((s, h, wd), BF16)] * n, [lambda k, i, s_: (k, i, 0)] * n,
                      (tm, wd), (s, nt))


def sum_chips(name, owns, recvs, chip, core):
    n = len(owns)
    s, r, wd = owns[0].shape
    tm = _group_rows(r, n)

    def body(s_ref, *refs):
        for k in range(n):
            acc = refs[4 * k][...].astype(F32)
            for ref in refs[4 * k + 1:4 * k + 4]:
                acc = acc + ref[...].astype(F32)
            refs[4 * n + k][...] = acc

    maps = [lambda i, s_: (s_[0], i, 0)]
    maps += [functools.partial(lambda i, s_, m: (s_[0] ^ m, i, 0), m=m) for m in (1, 2, 3)]
    ins = [x for own, recv in zip(owns, recvs) for x in (own, recv, recv, recv)]
    return _slot_call(name, body, jnp.stack([chip, core]), ins, maps * n,
                      [jax.ShapeDtypeStruct((2, r, wd), F32)] * n, [lambda i, s_: (s_[1], i, 0)] * n,
                      (tm, wd), (r // tm,))


def sum_slots(name, arr):
    s, r, wd = arr.shape
    tm = _pick(r, (512, 256, 128, 64, 32, 16, 8))

    def body(*refs):
        acc = refs[0][...].astype(F32)
        for ref in refs[1:s]:
            acc = acc + ref[...].astype(F32)
        refs[s][...] = acc

    specs = [pl.BlockSpec((None, tm, wd), functools.partial(lambda i, k: (k, i, 0), k=k)) for k in range(s)]
    return pl.pallas_call(body, name=name, grid=(r // tm,), in_specs=specs,
                          out_specs=pl.BlockSpec((tm, wd), lambda i: (i, 0)),
                          out_shape=jax.ShapeDtypeStruct((r, wd), F32),
                          compiler_params=_params(("parallel",)))(*([arr] * s))


def _pack(arrs):
    flat = jnp.concatenate([a.reshape(-1) for a in arrs])
    n = flat.shape[0]
    padded = -(-n // (512 * LANES)) * (512 * LANES)
    return jnp.pad(flat, (0, padded - n)).reshape(-1, LANES)


def _unpack(packed, shapes):
    flat = packed.reshape(-1)
    out, pos = [], 0
    for s in shapes:
        n = math.prod(s)
        out.append(flat[pos:pos + n].reshape(s))
        pos += n
    return out


def _as2d(a):
    return a.reshape(-1, a.shape[-1])


def kernel(x, meta, w_in_ab, s5_lam_re, s5_lam_im, s5_log_dt, s5_b_re, s5_b_im, s5_c_re, s5_c_im, s5_d, s5_w_glu, s5_b_glu, w_out_ab, w_in_c, hgrn_gamma, hgrn_norm_g, w_out_c, ln_mix_g, ln_mix_b, mlp_w_up, mlp_b_up, mlp_w_down, mlp_b_down, ln_mlp_g, ln_mlp_b, loss_target, m_meta, m_w_in_ab, m_s5_lam_re, m_s5_lam_im, m_s5_log_dt, m_s5_b_re, m_s5_b_im, m_s5_c_re, m_s5_c_im, m_s5_d, m_s5_w_glu, m_s5_b_glu, m_w_out_ab, m_w_in_c, m_hgrn_gamma, m_hgrn_norm_g, m_w_out_c, m_ln_mix_g, m_ln_mix_b, m_mlp_w_up, m_mlp_b_up, m_mlp_w_down, m_mlp_b_down, m_ln_mlp_g, m_ln_mlp_b, v_meta, v_w_in_ab, v_s5_lam_re, v_s5_lam_im, v_s5_log_dt, v_s5_b_re, v_s5_b_im, v_s5_c_re, v_s5_c_im, v_s5_d, v_s5_w_glu, v_s5_b_glu, v_w_out_ab, v_w_in_c, v_hgrn_gamma, v_hgrn_norm_g, v_w_out_c, v_ln_mix_g, v_ln_mix_b, v_mlp_w_up, v_mlp_b_up, v_mlp_w_down, v_mlp_b_down, v_ln_mlp_g, v_ln_mlp_b):
    given = dict(locals())
    wts = {n: given[n] for n in WEIGHTS}
    ms = {n: given['m_' + n] for n in WEIGHTS}
    vs = {n: given['v_' + n] for n in WEIGHTS}
    chip = 2 * lax.axis_index("x") + lax.axis_index("y")

    core = lax.axis_index("c")
    parts = [(n, l) for n in BIG for l in (range(DEPTH) if n.startswith('mlp_') else [0])]

    def by_shape(tag, fn, keys, *lists):
        out = {}
        for shape in dict.fromkeys(a.shape for a in lists[0]):
            idx = [i for i, a in enumerate(lists[0]) if a.shape == shape]
            res = fn(f"{tag}_{keys[idx[0]][0]}{keys[idx[0]][1]}_x{len(idx)}", *[[lst[i] for i in idx] for lst in lists])
            out.update({keys[i]: r for i, r in zip(idx, res)})
        return [out[k] for k in keys]

    shards = dict(zip(parts, by_shape("cast", lambda nm, ws: cast_into_slot(nm, ws, chip, BF16), parts,
                                      [wts[n][l] for n, l in parts])))
    small_sh = jnp.concatenate([meta, hgrn_norm_g, jnp.zeros((15, meta.shape[1]), F32)], axis=0)
    first, late = parts[:1], parts[1:]
    w_in_ab_all, sm = _exchange("gather_first", gather_over_chips(
        [shards[p] for p in first] + list(cast_into_slot("cast_small", [small_sh], chip, F32))))
    w = {n: wts[n] for n in SMALL}
    w['meta'] = sm[:, :N_META].transpose(1, 0, 2).reshape(N_META, D_MODEL)
    w['hgrn_norm_g'] = sm[:, N_META:N_META + 1].transpose(1, 0, 2).reshape(1, D_MODEL)
    w['w_in_ab'] = w_in_ab_all

    def slot_major(v):
        return v if v.ndim == 3 else v.reshape(N_CHIPS, -1, v.shape[-1])

    def whole_grads(ps, g):
        return [slot_major(g[n][l] if n.startswith('mlp_') else g[n]) for n, l in ps]

    def chip_sums_of(ps, gfull, theirs):
        return by_shape("sum_cores", lambda nm, a, b: sum_cores(nm, a, b, core), ps, gfull, list(theirs))

    def all_devices(arrs):
        packed = _pack(arrs)
        return gather_over_devices(lax.dynamic_update_slice(
            jnp.zeros((N_DEV,) + packed.shape, F32), packed[None], (2 * chip + core, 0, 0)))

    early_small = [n for n in SMALL if n != 'meta']
    last = [('w_out_c', 0), ('mlp_w_up', 1), ('mlp_w_down', 1)]
    mid = [p for p in late if p not in last]

    class Hooks:
        gather_late = gather_over_chips([shards[p] for p in mid])
        gather_last = gather_over_chips([shards[p] for p in last])

        @staticmethod
        def late_weights(filled):
            fw = Hooks.fw = dict(zip(mid, filled))
            return {'s5_w_glu': fw['s5_w_glu', 0].reshape(S5_WIDTH, S5_WIDTH),
                    'w_out_ab': fw['w_out_ab', 0].reshape(D_MODEL, D_MODEL),
                    'w_in_c': fw['w_in_c', 0],
                    'mlp_w_up': [fw['mlp_w_up', 0]],
                    'mlp_w_down': [fw['mlp_w_down', 0].reshape(D_FF, D_MODEL)]}

        @staticmethod
        def last_weights(filled):
            fw = dict(zip(last, filled))
            return {'w_out_c': fw['w_out_c', 0].reshape(D_MODEL, D_MODEL),
                    'mlp_w_up': [Hooks.fw['mlp_w_up', 0], fw['mlp_w_up', 1]],
                    'mlp_w_down': [Hooks.fw['mlp_w_down', 0].reshape(D_FF, D_MODEL),
                                   fw['mlp_w_down', 1].reshape(D_FF, D_MODEL)]}

        @staticmethod
        def fold_early(g):
            Hooks.whole = whole_grads(late, g)
            return fold_cores(Hooks.whole)

        @staticmethod
        def scatter_early(g, loss, theirs):
            Hooks.sums = chip_sums_of(late, Hooks.whole, theirs)
            Hooks.small_shapes = [(LANES,)] + [g[n].shape for n in early_small]
            return scatter_over_chips(Hooks.sums).beside(
                all_devices([loss.reshape(-1)] + [g[n] for n in early_small]))

        @staticmethod
        def scattered(outs):
            Hooks.recv, Hooks.small = list(outs[:-1]), outs[-1]

    loss, grad_x, g = local_step(x, loss_target, w, Hooks)

    whole = whole_grads(first, g)
    sums = chip_sums_of(first, whole, _exchange("fold_last", fold_cores(whole)))
    *recv, meta_slots = _exchange("scatter_last", scatter_over_chips(sums).beside(all_devices([g['meta']])))
    reduced = _exchange("join_grads", join_cores(
        by_shape("sum_chips", lambda nm, a, b: sum_chips(nm, a, b, chip, core), first + late,
                 sums + Hooks.sums, list(recv) + Hooks.recv)))
    reduced = dict(zip(first + late, [_as2d(r) for r in reduced]))
    red = _unpack(sum_slots("sum_small", Hooks.small), Hooks.small_shapes)
    loss_out = red[0][0]
    gs = dict(zip(early_small, red[1:]))
    gs['meta'] = _unpack(sum_slots("sum_meta", meta_slots), [g['meta'].shape])[0]
    gs['meta'] = lax.dynamic_slice_in_dim(gs['meta'], chip * meta.shape[1], meta.shape[1], axis=1)
    gs['hgrn_norm_g'] = lax.dynamic_slice_in_dim(gs['hgrn_norm_g'].reshape(1, -1), chip * meta.shape[1],
                                                 meta.shape[1], axis=1)

    grads, deltas, new_m, new_v = {}, {}, {}, {}
    for n in BIG:
        r = jnp.concatenate([reduced[n, l] for l in range(wts[n].shape[0])], axis=0)
        res = adamw("adamw_" + n, _as2d(wts[n]), [r], _as2d(ms[n]), _as2d(vs[n]))
        grads[n], deltas[n], new_m[n], new_v[n] = [r.reshape(wts[n].shape) for r in res]
    shapes = [wts[n].shape for n in SMALL]
    res = adamw("adamw_small", _pack([wts[n] for n in SMALL]), [_pack([gs[n] for n in SMALL])],
                _pack([ms[n] for n in SMALL]), _pack([vs[n] for n in SMALL]))
    for d, r in zip((grads, deltas, new_m, new_v), res):
        d.update(zip(SMALL, _unpack(r, shapes)))
    return (loss_out, grad_x, *[grads[n] for n in WEIGHTS], *[deltas[n] for n in WEIGHTS],
            *[new_m[n] for n in WEIGHTS], *[new_v[n] for n in WEIGHTS])
```

```python
import functools
import math

import jax
import jax.numpy as jnp
from jax import lax
from jax.experimental import pallas as pl
from jax.experimental.pallas import tpu as pltpu

F32 = jnp.float32
BF16 = jnp.bfloat16

D_MODEL = 1024
N_META = 16
DEPTH = 2
S5_WIDTH = 512
S5_GROUP = 16
S5_GROUPS = 32
S5_STATE = 64
S5_CHUNK = 16
SB_WIDTH = 512
SB_BLOCK = 128
SB_DEAD = -110.0
HG_HEADS = 8
HG_DK = 128
HG_CHUNK = 64
HG_UNROLL = 2
D_FF = 4096
ALPHA = (2.0 * DEPTH) ** 0.25
LN_EPS = 1e-5
RMS_EPS = 1e-6
ADAM_LR = 0.001
ADAM_B1 = 0.9
ADAM_B2 = 0.999
ADAM_EPS = 1e-08
ADAM_WD = 0.01
ADAM_STEP = 10
N_CHIPS = 4
N_DEV = 8
LANES = 128
MESH = pl.DeviceIdType.MESH
VMEM_LIMIT = 56 * 1024 * 1024
WHOLE_K_BYTES = 42 * 1024 * 1024
HGRN_BWD_VMEM = 60 * 1024 * 1024

WEIGHTS = ['meta', 'w_in_ab', 's5_lam_re', 's5_lam_im', 's5_log_dt', 's5_b_re', 's5_b_im', 's5_c_re', 's5_c_im',
           's5_d', 's5_w_glu', 's5_b_glu', 'w_out_ab', 'w_in_c', 'hgrn_gamma', 'hgrn_norm_g', 'w_out_c',
           'ln_mix_g', 'ln_mix_b', 'mlp_w_up', 'mlp_b_up', 'mlp_w_down', 'mlp_b_down', 'ln_mlp_g', 'ln_mlp_b']
BIG = ['w_in_ab', 's5_w_glu', 'w_out_ab', 'w_in_c', 'w_out_c', 'mlp_w_up', 'mlp_w_down']
SMALL = [n for n in WEIGHTS if n not in BIG]


def _params(sem=None):
    return pltpu.CompilerParams(dimension_semantics=sem, vmem_limit_bytes=VMEM_LIMIT)


def _pick(n, cands):
    for c in cands:
        if n % c == 0:
            return c
    return n


def _dot(a, b, ca, cb):
    return lax.dot_general(a, b, (((ca,), (cb,)), ((), ())), preferred_element_type=F32)


def _split3(x):
    hi = x.astype(BF16)
    r = x - hi.astype(F32)
    mid = r.astype(BF16)
    lo = (r - mid.astype(F32)).astype(BF16)
    return hi, mid, lo


def _dot_exact_rhs(x, m, cx, cm):
    hi = x.astype(BF16)
    lo = (x - hi.astype(F32)).astype(BF16)
    return _dot(hi, m, cx, cm) + _dot(lo, m, cx, cm)


def _dot3(a, b, ca, cb):
    ah = a.astype(BF16)
    al = (a - ah.astype(F32)).astype(BF16)
    bh = b.astype(BF16)
    bl = (b - bh.astype(F32)).astype(BF16)
    return _dot(ah, bh, ca, cb) + (_dot(ah, bl, ca, cb) + _dot(al, bh, ca, cb))


def rowwise(name, fn, row_ins, bc_ins, out_widths, sum_widths=(), out_dtypes=None, tm=None):
    t = row_ins[0].shape[0]
    if tm is None:
        wmax = max([a.shape[1] for a in row_ins] + list(out_widths))
        tm = _pick(t, (272, 256, 128, 64, 16, 8)) if wmax > 1024 else _pick(t, (544, 512, 256, 128, 64, 16, 8))
    nr, nb, no, ns = len(row_ins), len(bc_ins), len(out_widths), len(sum_widths)
    out_dtypes = out_dtypes or [F32] * no

    def body(*refs):
        i = pl.program_id(0)
        rows = [r[...] for r in refs[:nr]]
        bcs = [r[...] for r in refs[nr:nr + nb]]
        outs, sums = fn(i, tm, rows, bcs)
        for r, v in zip(refs[nr + nb:nr + nb + no], outs):
            r[...] = v.astype(r.dtype)
        if ns:
            srefs = refs[nr + nb + no:]

            @pl.when(i == 0)
            def _():
                for r in srefs:
                    r[...] = jnp.zeros(r.shape, r.dtype)

            for r, v in zip(srefs, sums):
                r[...] += v

    in_specs = [pl.BlockSpec((tm, a.shape[1]), lambda i: (i, 0)) for a in row_ins]
    in_specs += [pl.BlockSpec(a.shape, lambda i: (0, 0)) for a in bc_ins]
    out_specs = [pl.BlockSpec((tm, w), lambda i: (i, 0)) for w in out_widths]
    out_specs += [pl.BlockSpec((1, w), lambda i: (0, 0)) for w in sum_widths]
    out_shape = [jax.ShapeDtypeStruct((t, w), dt) for w, dt in zip(out_widths, out_dtypes)]
    out_shape += [jax.ShapeDtypeStruct((1, w), F32) for w in sum_widths]
    res = pl.pallas_call(body, name=name, grid=(t // tm,), in_specs=in_specs, out_specs=out_specs,
                         out_shape=out_shape, compiler_params=_params(("arbitrary",)))(*row_ins, *bc_ins)
    return res


def _colsum(v):
    return jnp.sum(v, axis=0, keepdims=True)


def matmul(name, a, b, mode, *, bias=None, act=None, add=None, add_scale=1.0, relu2_of=None, colsum=False,
           out_dtype=F32, out_slots=0, tm=None, tn=None, tk=None):
    slotted = b.ndim == 3
    if mode == 'nn':
        m, k = a.shape
        n = b.shape[-1] * (b.shape[0] if slotted else 1)
    elif mode == 'nt':
        m, k = a.shape
        n = b.shape[-2]
    else:
        k, m = a.shape
        n = b.shape[-1]
    ns = b.shape[-1] if slotted else None
    if mode == 'tn':
        tm = tm or _pick(m, (512, 256, 128))
        nw = n if not out_slots else n // out_slots
        if tn is None and tk is None:
            for cand in (512, 256):
                if nw % cand == 0 and (2 * k * (tm * a.dtype.itemsize + cand * b.dtype.itemsize)
                                       + 2 * tm * cand * 4 <= WHOLE_K_BYTES):
                    tn, tk = cand, k
                    break
        tn = tn or _pick(nw, (1024, 512, 256, 128))
        tk = tk or _pick(k, (1088, 1024, 768, 512, 384, 256, 128))
    else:
        tm = tm or _pick(m, (1088, 1024, 768, 512, 384, 256, 128))
        tn = tn or _pick(n if not (slotted and mode == 'nn') else ns, (1024, 512, 256, 128))
        extra = sum(x is not None for x in (add, relu2_of)) * 4
        if tk is None and not slotted and (2 * (tm * k * a.dtype.itemsize + k * tn * b.dtype.itemsize)
                                           + 2 * tm * tn * (4 + extra) <= WHOLE_K_BYTES):
            tk = k
        tk = tk or _pick(k if not (slotted and mode == 'nt') else ns, (1024, 512, 256, 128))
    gm, gn, gk = m // tm, n // tn, k // tk

    if mode == 'nn':
        a_spec = pl.BlockSpec((tm, tk), lambda i, j, kk: (i, kk))
        if slotted:
            per = ns // tn
            b_spec = pl.BlockSpec((None, tk, tn), lambda i, j, kk: (j // per, kk, j % per))
        else:
            b_spec = pl.BlockSpec((tk, tn), lambda i, j, kk: (kk, j))
        ca, cb = 1, 0
    elif mode == 'nt':
        a_spec = pl.BlockSpec((tm, tk), lambda i, j, kk: (i, kk))
        if slotted:
            per = ns // tk
            b_spec = pl.BlockSpec((None, tn, tk), lambda i, j, kk: (kk // per, j, kk % per))
        else:
            b_spec = pl.BlockSpec((tn, tk), lambda i, j, kk: (j, kk))
        ca, cb = 1, 1
    else:
        a_spec = pl.BlockSpec((tk, tm), lambda i, j, kk: (kk, i))
        b_spec = pl.BlockSpec((tk, tn), lambda i, j, kk: (kk, j))
        ca, cb = 0, 0
    in_specs = [a_spec, b_spec]
    args = [a, b]
    if bias is not None:
        in_specs.append(pl.BlockSpec((1, tn), lambda i, j, kk: (0, j)))
        args.append(bias)
    if add is not None:
        in_specs.append(pl.BlockSpec((tm, tn), lambda i, j, kk: (i, j)))
        args.append(add)
    if relu2_of is not None:
        in_specs.append(pl.BlockSpec((tm, tn), lambda i, j, kk: (i, j)))
        args.append(relu2_of)
    if out_slots:
        per_o = (n // out_slots) // tn
        out_spec = pl.BlockSpec((None, tm, tn), lambda i, j, kk: (j // per_o, i, j % per_o))
        out_shape = jax.ShapeDtypeStruct((out_slots, m, n // out_slots), out_dtype)
    else:
        out_spec = pl.BlockSpec((tm, tn), lambda i, j, kk: (i, j))
        out_shape = jax.ShapeDtypeStruct((m, n), out_dtype)
    grid = (gm, gn, gk)
    if colsum:
        assert mode == 'tn'
        out_spec = [out_spec, pl.BlockSpec((1, tn), lambda i, j, kk: (0, j))]
        out_shape = [out_shape, jax.ShapeDtypeStruct((1, n), F32)]

        def swapped(spec):
            return pl.BlockSpec(spec.block_shape, functools.partial(lambda j, i, kk, f: f(i, j, kk), f=spec.index_map))
        in_specs = [swapped(sp) for sp in in_specs]
        out_spec = [swapped(sp) for sp in out_spec]
        grid = (gn, gm, gk)

    def body(*refs):
        a_ref, b_ref = refs[0], refs[1]
        pos = 2
        bias_ref = add_ref = hid_ref = cs_ref = None
        if bias is not None:
            bias_ref = refs[pos]
            pos += 1
        if add is not None:
            add_ref = refs[pos]
            pos += 1
        if relu2_of is not None:
            hid_ref = refs[pos]
            pos += 1
        o_ref = refs[pos]
        pos += 1
        if colsum:
            cs_ref = refs[pos]
            pos += 1
        acc_ref = refs[pos] if gk > 1 else None
        bt = b_ref[...]
        p = _dot(a_ref[...].astype(BF16), bt.astype(BF16), ca, cb)
        kk = pl.program_id(2)

        if colsum:
            @pl.when(jnp.logical_and(pl.program_id(1) == 0, kk == 0))
            def _():
                cs_ref[...] = _colsum(bt.astype(F32))

            @pl.when(jnp.logical_and(pl.program_id(1) == 0, kk > 0))
            def _():
                cs_ref[...] += _colsum(bt.astype(F32))

        def finish(r):
            if bias_ref is not None:
                r = r + bias_ref[...]
            if act == 'relu2':
                r = jnp.square(jnp.maximum(r, 0.0))
            if hid_ref is not None:
                r = r * (2.0 * jnp.sqrt(hid_ref[...].astype(F32)))
            if add_ref is not None:
                r = r + add_scale * add_ref[...]
            o_ref[...] = r.astype(o_ref.dtype)

        if gk == 1:
            finish(p)
        else:
            @pl.when(kk == 0)
            def _():
                acc_ref[...] = p

            @pl.when(kk > 0)
            def _():
                acc_ref[...] += p

            @pl.when(kk == gk - 1)
            def _():
                finish(acc_ref[...])

    scratch = [pltpu.VMEM((tm, tn), F32)] if gk > 1 else []
    sem = ("arbitrary",) * 3 if colsum else ("parallel", "parallel", "arbitrary")
    return pl.pallas_call(body, name=name, grid=grid, in_specs=in_specs, out_specs=out_spec,
                          out_shape=out_shape, scratch_shapes=scratch, compiler_params=_params(sem))(*args)


def ln_fwd(name, h, mix, g, b):
    def fn(i, tm, rows, bcs):
        pre = ALPHA * rows[0] + rows[1]
        mu = jnp.mean(pre, axis=1, keepdims=True)
        xc = pre - mu
        var = jnp.mean(xc * xc, axis=1, keepdims=True)
        rstd = lax.rsqrt(var + LN_EPS)
        xhat = xc * rstd
        return (xhat * bcs[0] + bcs[1], xhat, rstd), ()
    return rowwise(name, fn, [h, mix], [g, b], [D_MODEL, D_MODEL, 1])


def ln_bwd(name, dy, xhat, rstd, g):
    def fn(i, tm, rows, bcs):
        dy_, xh, rs = rows
        dyg = dy_ * bcs[0]
        m1 = jnp.mean(dyg, axis=1, keepdims=True)
        m2 = jnp.mean(dyg * xh, axis=1, keepdims=True)
        dpre = rs * (dyg - m1 - xh * m2)
        return (dpre,), (_colsum(dy_ * xh), _colsum(dy_), _colsum(dpre))
    return rowwise(name, fn, [dy, xhat, rstd], [g], [D_MODEL], [D_MODEL] * 3)


_GELU_C = math.sqrt(2.0 / math.pi)


def gelu_fwd(name, x):
    def fn(i, tm, rows, bcs):
        v = rows[0]
        u = _GELU_C * (v + 0.044715 * (v * v * v))
        return (0.5 * v * (1.0 + jnp.tanh(u)),), ()
    return rowwise(name, fn, [x], [], [x.shape[1]])[0]


def gelu_bwd(name, dy, x):
    def fn(i, tm, rows, bcs):
        v = rows[1]
        th = jnp.tanh(_GELU_C * (v + 0.044715 * (v * v * v)))
        dg = 0.5 * (1.0 + th) + 0.5 * v * (1.0 - th * th) * (_GELU_C * (1.0 + 3.0 * 0.044715 * v * v))
        return (rows[0] * dg,), ()
    return rowwise(name, fn, [dy, x], [], [x.shape[1]])[0]


def glu_gate(name, y, gp):
    def fn(i, tm, rows, bcs):
        return (rows[0] * jax.nn.sigmoid(rows[1]),), ()
    return rowwise(name, fn, [y, gp], [], [y.shape[1]], out_dtypes=[BF16])[0]


def glu_gate_bwd(name, da, y, gp):
    def fn(i, tm, rows, bcs):
        s = jax.nn.sigmoid(rows[2])
        dgp = rows[0] * rows[1] * s * (1.0 - s)
        return (dgp, rows[0] * s), (_colsum(dgp),)
    w = y.shape[1]
    return rowwise(name, fn, [da, y, gp], [], [w, w], [w])


def ln_loss(name, h, mix, g, b, tgt, lp, seq):
    def fn(i, tm, rows, bcs):
        pre = ALPHA * rows[0] + rows[1]
        mu = jnp.mean(pre, axis=1, keepdims=True)
        xc = pre - mu
        var = jnp.mean(xc * xc, axis=1, keepdims=True)
        rstd = lax.rsqrt(var + LN_EPS)
        xhat = xc * rstd
        pos = (i * tm + lax.broadcasted_iota(jnp.int32, (tm, 1), 0)) % lp
        valid = jnp.logical_and(pos >= N_META, pos < N_META + seq)
        err = jnp.where(valid, (xhat * bcs[0] + bcs[1]) - rows[2], 0.0)
        part = 0.5 * jnp.sum(jnp.mean(err * err, axis=1, keepdims=True), axis=0, keepdims=True)
        dy = err * (1.0 / D_MODEL)
        dyg = dy * bcs[0]
        m1 = jnp.mean(dyg, axis=1, keepdims=True)
        m2 = jnp.mean(dyg * xhat, axis=1, keepdims=True)
        dpre = rstd * (dyg - m1 - xhat * m2)
        return (dpre,), (_colsum(dy * xhat), _colsum(dy), _colsum(dpre), jnp.broadcast_to(part, (1, LANES)))
    return rowwise(name, fn, [h, mix, tgt], [g, b], [D_MODEL], [D_MODEL] * 3 + [LANES])


def adamw(name, w, gs, m, v):
    ng = len(gs)

    def fn(i, tm, rows, bcs):
        w_ = rows[0]
        g = rows[1] if ng == 1 else rows[1] + rows[2]
        m_, v_ = rows[1 + ng], rows[2 + ng]
        m_ = ADAM_B1 * m_ + (1.0 - ADAM_B1) * g
        v_ = ADAM_B2 * v_ + (1.0 - ADAM_B2) * jnp.square(g)
        m_hat = m_ / (1.0 - ADAM_B1 ** ADAM_STEP)
        v_hat = v_ / (1.0 - ADAM_B2 ** ADAM_STEP)
        delta = -ADAM_LR * (m_hat / (jnp.sqrt(v_hat) + ADAM_EPS) + ADAM_WD * w_)
        return (g, delta, m_, v_), ()
    wd = w.shape[1]
    return rowwise(name, fn, [w] + list(gs) + [m, v], [], [wd] * 4)


def s5_matrices(lam_re, lam_im, log_dt, b_re, b_im, c_re, c_im, d):
    c, hh, gg, pp = S5_CHUNK, S5_GROUP, S5_GROUPS, S5_STATE
    dt = jnp.exp(log_dt)[:, None]
    tau = jnp.arange(c + 1, dtype=F32)[None, :, None]
    mag = jnp.exp(tau * (lam_re * dt)[:, None, :])
    ang = tau * (lam_im * dt)[:, None, :]
    pw_re, pw_im = mag * jnp.cos(ang), mag * jnp.sin(ang)
    nr, ni = pw_re[:, 1] - 1.0, pw_im[:, 1]
    den = lam_re * lam_re + lam_im * lam_im
    cf_re = (nr * lam_re + ni * lam_im) / den
    cf_im = (ni * lam_re - nr * lam_im) / den
    bb_re = cf_re[:, :, None] * b_re - cf_im[:, :, None] * b_im
    bb_im = cf_re[:, :, None] * b_im + cf_im[:, :, None] * b_re
    ct_re, ct_im = c_re.transpose(0, 2, 1)[:, :, None, :], c_im.transpose(0, 2, 1)[:, :, None, :]
    pt_re, pt_im = pw_re.transpose(0, 2, 1)[:, :, :, None], pw_im.transpose(0, 2, 1)[:, :, :, None]
    cp_re = ct_re * pt_re - ct_im * pt_im
    cp_im = ct_re * pt_im + ct_im * pt_re
    hp = lax.Precision.HIGHEST
    kmat = (jnp.einsum('gpk,gpn->gkn', bb_re, cp_re[:, :, :c].reshape(gg, pp, c * hh), precision=hp)
            - jnp.einsum('gpk,gpn->gkn', bb_im, cp_im[:, :, :c].reshape(gg, pp, c * hh), precision=hp))
    rows = [jnp.pad(kmat[:, :, :(c - j) * hh], ((0, 0), (0, 0), (j * hh, 0))) for j in range(c)]
    a1 = jnp.stack(rows, axis=1).reshape(gg, c * hh, c * hh)
    a1 = a1 + jnp.eye(c * hh, dtype=F32)[None] * jnp.tile(d, (1, c))[:, None, :]
    a2 = jnp.concatenate([cp_re[:, :, 1:].reshape(gg, pp, c * hh), -cp_im[:, :, 1:].reshape(gg, pp, c * hh)], axis=1)
    rv_re, rv_im = pw_re[:, :c][:, ::-1, None, :], pw_im[:, :c][:, ::-1, None, :]
    bt_re, bt_im = bb_re.transpose(0, 2, 1)[:, None], bb_im.transpose(0, 2, 1)[:, None]
    a3 = jnp.concatenate([rv_re * bt_re - rv_im * bt_im, rv_re * bt_im + rv_im * bt_re], axis=-1)
    a3 = a3.reshape(gg, c * hh, 2 * pp)
    are = jnp.concatenate([pw_re[:, c], pw_re[:, c]], axis=-1)[:, None, :]
    aim = jnp.concatenate([-pw_im[:, c], pw_im[:, c]], axis=-1)[:, None, :]
    return a1, a2, a3, are, aim


S5_LEVELS = 8


def s5_scan_powers(lam_re, lam_im, log_dt):
    dt = jnp.exp(log_dt)[:, None]
    e = (S5_CHUNK * 2.0 ** jnp.arange(S5_LEVELS, dtype=F32))[:, None, None]
    mag = jnp.exp(e * (lam_re * dt)[None])
    ang = e * (lam_im * dt)[None]
    pr, pi = mag * jnp.cos(ang), mag * jnp.sin(ang)
    pre = jnp.concatenate([pr, pr], axis=-1).transpose(1, 0, 2)
    pim = jnp.concatenate([-pi, pi], axis=-1).transpose(1, 0, 2)
    return pre, pim


def _s5_scan(x, pre, pim, reverse):
    n = x.shape[0]
    rowi = lax.broadcasted_iota(jnp.int32, (n, 1), 0)
    t = x
    for k in range(S5_LEVELS):
        shift = 2 ** k
        if shift >= n:
            break
        p_re, p_im = pre[k:k + 1, :], pim[k:k + 1, :]
        if reverse:
            far = jnp.where(rowi < n - shift, pltpu.roll(t, n - shift, 0), 0.0)
            t = t + (p_re * far + pltpu.roll(p_im * far, S5_STATE, 1))
        else:
            far = jnp.where(rowi >= shift, pltpu.roll(t, shift, 0), 0.0)
            t = t + (p_re * far + p_im * pltpu.roll(far, S5_STATE, 1))
    return t


def s5_fwd(ug, a1, a2, a3, pre, pim, nseq):
    g, nc, cw = ug.shape
    per = nc // nseq
    sw = 2 * S5_STATE
    assert per <= 2 ** S5_LEVELS

    def body(u_ref, a1_ref, a2_ref, a3_ref, pre_ref, pim_ref, y_ref, s_ref):
        u = u_ref[...]
        x = _dot3(u, a3_ref[...], 1, 0)
        first = lax.broadcasted_iota(jnp.int32, (per, 1), 0) == 0
        for b in range(nseq):
            t = _s5_scan(x[b * per:(b + 1) * per], pre_ref[...], pim_ref[...], False)
            s_ref[pl.ds(b * per, per), :] = jnp.where(first, 0.0, pltpu.roll(t, 1, 0))
        y_ref[...] = _dot3(u, a1_ref[...], 1, 0) + _dot3(s_ref[...], a2_ref[...], 1, 0)

    def spec(shape):
        return pl.BlockSpec((None,) + shape, lambda i: (i, 0, 0))
    lv = (S5_LEVELS, sw)
    return pl.pallas_call(
        body, name="s5_fwd", grid=(g,),
        in_specs=[spec((nc, cw)), spec((cw, cw)), spec((sw, cw)), spec((cw, sw)), spec(lv), spec(lv)],
        out_specs=[spec((nc, cw)), spec((nc, sw))],
        out_shape=[jax.ShapeDtypeStruct((g, nc, cw), F32), jax.ShapeDtypeStruct((g, nc, sw), F32)],
        compiler_params=_params(("parallel",)))(ug, a1, a2, a3, pre, pim)


def s5_bwd(dyg, ug, sst, a1, a2, a3, pre, pim, nseq, ex=None):
    g, nc, cw = ug.shape
    per = nc // nseq
    sw = 2 * S5_STATE
    split, ex_start, ex_finish, exkw = _hidden_exchange(ex, 8, 6, (g,))

    def body(*refs):
        ((dy_ref, u_ref, s_ref, a1_ref, a2_ref, a3_ref, pre_ref, pim_ref),
         (du_ref, da1_ref, da2_ref, da3_ref, dare_ref, daim_ref), ex_parts, (dx_ref,)) = split(refs)
        ex_start(ex_parts)
        dy = dy_ref[...]
        u = u_ref[...]
        gd = _dot3(dy, a2_ref[...], 1, 1)
        s_all = s_ref[...]
        da2_ref[...] = _dot3(s_all, dy, 0, 0)
        last = lax.broadcasted_iota(jnp.int32, (per, 1), 0) == per - 1
        for b in range(nseq):
            gs = _s5_scan(gd[b * per:(b + 1) * per], pre_ref[...], pim_ref[...], True)
            dx_ref[pl.ds(b * per, per), :] = jnp.where(last, 0.0, pltpu.roll(gs, per - 1, 0))
        dx = dx_ref[...]
        dare_ref[...] = _colsum(dx * s_all)
        daim_ref[...] = _colsum(dx * pltpu.roll(s_all, S5_STATE, 1))
        du_ref[...] = _dot3(dy, a1_ref[...], 1, 1) + _dot3(dx, a3_ref[...], 1, 1)
        da1_ref[...] = _dot3(u, dy, 0, 0)
        da3_ref[...] = _dot3(u, dx, 0, 0)
        ex_finish(ex_parts)

    def spec(shape):
        return pl.BlockSpec((None,) + shape, lambda i: (i, 0, 0))
    sds = jax.ShapeDtypeStruct
    return pl.pallas_call(
        body, name="s5_bwd", grid=(g,),
        in_specs=[spec((nc, cw)), spec((nc, cw)), spec((nc, sw)), spec((cw, cw)), spec((sw, cw)), spec((cw, sw)),
                  spec((S5_LEVELS, sw)), spec((S5_LEVELS, sw))] + exkw['in_specs'],
        out_specs=[spec((nc, cw)), spec((cw, cw)), spec((sw, cw)), spec((cw, sw)), spec((1, sw)), spec((1, sw))]
        + exkw['out_specs'],
        out_shape=[sds((g, nc, cw), F32), sds((g, cw, cw), F32), sds((g, sw, cw), F32), sds((g, cw, sw), F32),
                   sds((g, 1, sw), F32), sds((g, 1, sw), F32)] + exkw['out_shape'],
        input_output_aliases=exkw['aliases'], scratch_shapes=[pltpu.VMEM((nc, sw), F32)] + exkw['scratch'],
        compiler_params=_params(("arbitrary",)))(dyg, ug, sst, a1, a2, a3, pre, pim, *exkw['ins'])


def _to_groups(u):
    t = u.shape[0]
    nc = t // S5_CHUNK
    return u.reshape(nc, S5_CHUNK, S5_GROUPS, S5_GROUP).transpose(2, 0, 1, 3).reshape(S5_GROUPS, nc, -1)


def _from_groups(yg):
    g, nc, _ = yg.shape
    return yg.reshape(g, nc, S5_CHUNK, S5_GROUP).transpose(1, 2, 0, 3).reshape(nc * S5_CHUNK, g * S5_GROUP)


def _sb_masks():
    row = lax.broadcasted_iota(jnp.int32, (SB_BLOCK, SB_BLOCK), 0)
    col = lax.broadcasted_iota(jnp.int32, (SB_BLOCK, SB_BLOCK), 1)
    return row, col


def _sb_weights(z, vis, later_of, after):
    sp = jnp.maximum(z, 0.0) + jnp.log1p(jnp.exp(-jnp.abs(z)))
    lk = jnp.where(vis, -sp, 0.0)
    rs = jnp.sum(lk, axis=1, keepdims=True)
    later = _dot_exact_rhs(lk, after, 1, 0) + later_of(rs)
    lb = z - sp
    w = jnp.where(vis, jnp.exp(lb + later), 0.0)
    return w, rs, lb


def _hidden_exchange(ex, n_in, n_out, grid):
    ni, no = (len(ex.ins), len(ex.out_shapes)) if ex else (0, 0)

    def split(refs):
        a, b = n_in + ni, n_in + ni + n_out
        end = len(refs) - (2 if ex else 0)
        return refs[:n_in], refs[a:b], (refs[n_in:a], refs[b:b + no], refs[end:]), refs[b + no:end]

    def at_step(which, fn, parts):
        if ex is not None:
            ids = [pl.program_id(d) == (0 if which == 'first' else g - 1) for d, g in enumerate(grid)]
            cond = ids[0]
            for c in ids[1:]:
                cond = jnp.logical_and(cond, c)
            pl.when(cond)(lambda: fn(parts[0], parts[1], *parts[2]))

    anyspec = pl.BlockSpec(memory_space=pl.ANY)
    kw = dict(in_specs=[anyspec] * ni, out_specs=[anyspec] * no, out_shape=list(ex.out_shapes) if ex else [],
              aliases={n_in + i: n_out + j for i, j in (ex.aliases().items() if ex else ())},
              scratch=ex.sems() if ex else [], ins=list(ex.ins) if ex else [])
    start = functools.partial(at_step, 'first', ex.start if ex else None)
    finish = functools.partial(at_step, 'last', ex.finish if ex else None)
    return split, start, finish, kw


SB_TILES = 2
SB_HEADS = 2 * SB_TILES
SB_WD = SB_TILES * LANES


def _stack_heads(x):
    return jnp.concatenate([x] * SB_HEADS, axis=0)


def _stack_masks():
    row = lax.broadcasted_iota(jnp.int32, (SB_HEADS * SB_BLOCK, SB_WD), 0)
    col = lax.broadcasted_iota(jnp.int32, (SB_HEADS * SB_BLOCK, SB_WD), 1)
    rowk = lax.broadcasted_iota(jnp.int32, (SB_HEADS * SB_BLOCK, SB_BLOCK), 0)
    colk = lax.broadcasted_iota(jnp.int32, (SB_HEADS * SB_BLOCK, SB_BLOCK), 1)
    return (col // 64) == (row // SB_BLOCK), colk < (rowk % SB_BLOCK)


def _own_lanes(r):
    head = lax.broadcasted_iota(jnp.int32, (SB_BLOCK, SB_WD), 1) // 64
    out = r[:SB_BLOCK]
    for h in range(1, SB_HEADS):
        out = jnp.where(head == h, r[h * SB_BLOCK:(h + 1) * SB_BLOCK], out)
    return out


def _sb_specs(lp, nseq):
    wd = SB_TILES * LANES
    off = [(S5_WIDTH + i * SB_WIDTH) // wd for i in range(3)]
    blk = (lp, wd)
    qkv = [pl.BlockSpec(blk, functools.partial(lambda b, h, o: (b, o + h), o=o)) for o in off]
    return (blk, qkv, pl.BlockSpec(blk, lambda b, h: (b, h)),
            pl.BlockSpec((None, None, 8, LANES), lambda b, h: (b, h, 0, 0)))


def attn_fwd(proj, nseq, lp, ex=None):
    nqb = lp // SB_BLOCK
    t = proj.shape[0]
    nstep = SB_WIDTH // (SB_TILES * LANES)
    split, ex_start, ex_finish, exkw = _hidden_exchange(ex, 3, 3, (nseq, nstep))

    def body(*refs):
        (q_ref, k_ref, v_ref), (o_ref, tot_ref, first_ref), ex_parts, _ = split(refs)
        ex_start(ex_parts)
        row, col = _sb_masks()
        after = (row > col).astype(BF16)
        lane8 = lax.broadcasted_iota(jnp.int32, (8, LANES), 1)
        hms, tris = _stack_masks()

        def qloop(qi, firsts):
            q0 = pl.multiple_of(qi * SB_BLOCK, SB_BLOCK)
            qm = jnp.where(hms, _stack_heads(q_ref[pl.ds(q0, SB_BLOCK), :] * 0.125), 0.0).astype(BF16)

            def alive(carry):
                return jnp.logical_and(carry[0] <= qi, carry[1] > 0)

            def kstep(carry):
                s, _, acc, lat = carry
                kj = qi - s
                k0 = pl.multiple_of(kj * SB_BLOCK, SB_BLOCK)
                k = k_ref[pl.ds(k0, SB_BLOCK), :].astype(BF16)
                v = v_ref[pl.ds(k0, SB_BLOCK), :].astype(BF16)
                vis = jnp.logical_or(kj < qi, tris)
                w, rs, _ = _sb_weights(_dot(qm, k, 1, 1), vis, functools.partial(lambda rs, lat: lat, lat=lat), after)
                acc = acc + _own_lanes(_dot(w.astype(BF16), v, 1, 0))
                lat = lat + rs
                return s + 1, (jnp.max(lat) > SB_DEAD).astype(jnp.int32), acc, lat

            res = lax.while_loop(alive, kstep, (jnp.int32(0), jnp.int32(1), jnp.zeros((SB_BLOCK, SB_WD), F32),
                                                jnp.zeros((SB_HEADS * SB_BLOCK, 1), F32)))
            o_ref[pl.ds(q0, SB_BLOCK), :] = res[2].astype(o_ref.dtype)
            tot_ref[pl.ds(q0, SB_BLOCK), :] = _own_lanes(jnp.broadcast_to(res[3], (SB_HEADS * SB_BLOCK, SB_WD)))
            return jnp.where(lane8 == qi, (qi - res[0] + 1).astype(F32), firsts)

        first_ref[...] = lax.fori_loop(0, nqb, qloop, jnp.zeros((8, LANES), F32))
        ex_finish(ex_parts)

    blk, qkv, out, vec = _sb_specs(lp, nseq)
    return pl.pallas_call(
        body, name="attn_fwd", grid=(nseq, nstep),
        in_specs=qkv + exkw['in_specs'], out_specs=[out, out, vec] + exkw['out_specs'],
        out_shape=[jax.ShapeDtypeStruct((t, SB_WIDTH), BF16), jax.ShapeDtypeStruct((t, SB_WIDTH), F32)]
        + [jax.ShapeDtypeStruct((nseq, nstep, 8, LANES), F32)] + exkw['out_shape'],
        input_output_aliases=exkw['aliases'], scratch_shapes=exkw['scratch'],
        compiler_params=_params(("arbitrary", "arbitrary")))(proj, proj, proj, *exkw['ins'])


def attn_bwd(proj, tot, first, do, nseq, lp, ex=None):
    nqb = lp // SB_BLOCK
    t = proj.shape[0]
    nstep = SB_WIDTH // (SB_TILES * LANES)
    split, ex_start, ex_finish, exkw = _hidden_exchange(ex, 6, 3, (nseq, nstep))

    def body(*refs):
        (q_ref, k_ref, v_ref, tot_ref, first_ref, do_ref), (dq_ref, dk_ref, dv_ref), ex_parts, _ = split(refs)
        ex_start(ex_parts)
        row, col = _sb_masks()
        after = (row > col).astype(BF16)
        before = (row < col).astype(BF16)
        lane8 = lax.broadcasted_iota(jnp.int32, (8, LANES), 1)
        firsts = first_ref[...]
        dk_ref[...] = jnp.zeros(dk_ref.shape, F32)
        dv_ref[...] = jnp.zeros(dv_ref.shape, F32)
        hms, tris = _stack_masks()

        def qloop(qi, _):
            first = jnp.max(jnp.where(lane8 == qi, firsts, 0.0)).astype(jnp.int32)
            first = jnp.clip(first, 0, qi)
            q0 = pl.multiple_of(qi * SB_BLOCK, SB_BLOCK)
            qm = jnp.where(hms, _stack_heads(q_ref[pl.ds(q0, SB_BLOCK), :] * 0.125), 0.0).astype(BF16)
            dom = jnp.where(hms, _stack_heads(do_ref[pl.ds(q0, SB_BLOCK), :]), 0.0).astype(BF16)
            tot = jnp.max(jnp.where(hms, _stack_heads(tot_ref[pl.ds(q0, SB_BLOCK), :]), -jnp.inf),
                          axis=1, keepdims=True)

            def kloop(kj, carry):
                dq, pre, gpre = carry
                k0 = pl.multiple_of(kj * SB_BLOCK, SB_BLOCK)
                k = k_ref[pl.ds(k0, SB_BLOCK), :].astype(BF16)
                v = v_ref[pl.ds(k0, SB_BLOCK), :].astype(BF16)
                vis = jnp.logical_or(kj < qi, tris)
                later_of = functools.partial(lambda rs, t_, p_: t_ - p_ - rs, t_=tot, p_=pre)
                w, rs, lb = _sb_weights(_dot(qm, k, 1, 1), vis, later_of, after)
                g = _dot(dom, v, 1, 1) * w
                dlk = _dot_exact_rhs(g, before, 1, 0) + gpre
                sig = jnp.exp(lb)
                dz = jnp.where(vis, g * (1.0 - sig) - dlk * sig, 0.0).astype(BF16)
                dk_ref[pl.ds(k0, SB_BLOCK), :] += _dot(dz, qm, 0, 0)
                dv_ref[pl.ds(k0, SB_BLOCK), :] += _dot(w.astype(BF16), dom, 0, 0)
                return dq + _own_lanes(_dot(dz, k, 1, 0)), pre + rs, gpre + jnp.sum(g, axis=1, keepdims=True)

            z1 = jnp.zeros((SB_HEADS * SB_BLOCK, 1), F32)
            res = lax.fori_loop(first, qi + 1, kloop, (jnp.zeros((SB_BLOCK, SB_WD), F32), z1, z1))
            dq_ref[pl.ds(q0, SB_BLOCK), :] = res[0] * 0.125
            return 0

        lax.fori_loop(0, nqb, qloop, 0)
        ex_finish(ex_parts)

    blk, qkv, out, vec = _sb_specs(lp, nseq)
    return pl.pallas_call(
        body, name="attn_bwd", grid=(nseq, nstep),
        in_specs=qkv + [out, vec, out] + exkw['in_specs'], out_specs=[out] * 3 + exkw['out_specs'],
        out_shape=[jax.ShapeDtypeStruct((t, SB_WIDTH), F32)] * 3 + exkw['out_shape'],
        input_output_aliases=exkw['aliases'], scratch_shapes=exkw['scratch'],
        compiler_params=_params(("arbitrary", "arbitrary")))(proj, proj, proj, tot, first, do, *exkw['ins'])


def _hg_loop(nch, step, init):
    assert nch % HG_UNROLL == 0

    def trip(i, carry):
        for u in range(HG_UNROLL):
            carry = step(i * HG_UNROLL + u, carry)
        return carry
    return lax.fori_loop(0, nch // HG_UNROLL, trip, init)


def _hg_gates(fc, lb):
    sg = jax.nn.sigmoid(fc)
    f = lb + (1.0 - lb) * sg
    return sg, f


def _dot_exact_lhs(m, x):
    hi, mid, lo = _split3(x)
    return _dot(m, hi, 1, 0) + (_dot(m, mid, 1, 0) + _dot(m, lo, 1, 0))


def hgrn_fwd(proj, lb, ng, nseq, lp, ex=None):
    nch = lp // HG_CHUNK
    t = proj.shape[0]
    c = HG_CHUNK

    split, ex_start, ex_finish, exkw = _hidden_exchange(ex, 6, 3, (HG_HEADS,))

    def body(*refs):
        (q_ref, f_ref, v_ref, g_ref, lb_ref, ng_ref), (o_ref, og_ref, st_ref), ex_parts, _ = split(refs)
        ex_start(ex_parts)
        lbv, ngv = lb_ref[...], ng_ref[...]

        rr = nseq * c
        seq_r = lax.broadcasted_iota(jnp.int32, (rr, 1), 0) // c
        bd_row = lax.broadcasted_iota(jnp.int32, (rr, rr), 0)
        bd_col = lax.broadcasted_iota(jnp.int32, (rr, rr), 1)
        causal = jnp.logical_and(bd_row // c == bd_col // c, bd_col <= bd_row)
        incl = causal.astype(BF16)

        def stacked(ref, ci):
            return jnp.concatenate([ref[pl.ds(pl.multiple_of(b * lp + ci * c, c), c), :] for b in range(nseq)], axis=0)

        def wide(x):
            return jnp.concatenate([jnp.where(seq_r == b, x, jnp.zeros_like(x)) for b in range(nseq)], axis=1)

        def step(ci, st):
            for b in range(nseq):
                st_ref[b, ci] = st[:, b * HG_DK:(b + 1) * HG_DK]
            _, f = _hg_gates(stacked(f_ref, ci), lbv)
            bcum = _dot_exact_lhs(incl, jnp.log(f))
            lasts = [bcum[b * c + c - 1:b * c + c, :] for b in range(nseq)]
            blast = jnp.concatenate([jnp.broadcast_to(x, (c, LANES)) for x in lasts], axis=0)
            kk = 1.0 - f
            vc = stacked(v_ref, ci).astype(BF16)
            qd = (stacked(q_ref, ci) * jnp.exp(bcum)).astype(BF16)
            kd = (kk * jnp.exp(-bcum)).astype(BF16)
            a = jnp.where(causal, _dot(qd, kd, 1, 1), 0.0)
            o = _dot(a.astype(BF16), vc, 1, 0) + _dot(wide(qd), st.astype(BF16), 1, 1)
            kend = (kk * jnp.exp(blast - bcum)).astype(BF16)
            st = st * jnp.exp(jnp.concatenate(lasts, axis=1)) + _dot(vc, wide(kend), 0, 0)
            r = lax.rsqrt(jnp.mean(o * o, axis=1, keepdims=True) + RMS_EPS)
            gc = stacked(g_ref, ci)
            og = (o * r * ngv * (gc * jax.nn.sigmoid(gc))).astype(og_ref.dtype)
            for b in range(nseq):
                rows = pl.ds(pl.multiple_of(b * lp + ci * c, c), c)
                o_ref[rows, :] = o[b * c:(b + 1) * c]
                og_ref[rows, :] = og[b * c:(b + 1) * c]
            return st

        _hg_loop(nch, step, jnp.zeros((HG_DK, nseq * HG_DK), F32))
        ex_finish(ex_parts)

    blk = (nseq * lp, LANES)
    h = HG_HEADS
    return pl.pallas_call(
        body, name="hgrn_fwd", grid=(h,),
        in_specs=[pl.BlockSpec(blk, lambda hh: (0, hh)),
                  pl.BlockSpec(blk, lambda hh: (0, h + hh)),
                  pl.BlockSpec(blk, lambda hh: (0, 2 * h + hh)),
                  pl.BlockSpec(blk, lambda hh: (0, 3 * h + hh)),
                  pl.BlockSpec((1, LANES), lambda hh: (0, hh)),
                  pl.BlockSpec((1, LANES), lambda hh: (0, hh))] + exkw['in_specs'],
        out_specs=[pl.BlockSpec(blk, lambda hh: (0, hh)),
                   pl.BlockSpec(blk, lambda hh: (0, hh)),
                   pl.BlockSpec((nseq, None, nch, HG_DK, HG_DK), lambda hh: (0, hh, 0, 0, 0))] + exkw['out_specs'],
        out_shape=[jax.ShapeDtypeStruct((t, D_MODEL), F32), jax.ShapeDtypeStruct((t, D_MODEL), BF16),
                   jax.ShapeDtypeStruct((nseq, h, nch, HG_DK, HG_DK), F32)] + exkw['out_shape'],
        input_output_aliases=exkw['aliases'], scratch_shapes=exkw['scratch'],
        compiler_params=_params(("arbitrary",)))(proj, proj, proj, proj, lb, ng, *exkw['ins'])


def hgrn_bwd(proj, o, dog, states, lb, ng, nseq, lp):
    nch = lp // HG_CHUNK
    t = proj.shape[0]
    c = HG_CHUNK
    rr = nseq * c

    def body(q_ref, f_ref, v_ref, g_ref, o_ref, dog_ref, st_ref, lb_ref, ng_ref,
             dq_ref, df_ref, dv_ref, dg_ref, dlb_ref, dng_ref):
        seq_r = lax.broadcasted_iota(jnp.int32, (rr, 1), 0) // c
        last = lax.broadcasted_iota(jnp.int32, (rr, 1), 0) % c == c - 1
        bd_row = lax.broadcasted_iota(jnp.int32, (rr, rr), 0)
        bd_col = lax.broadcasted_iota(jnp.int32, (rr, rr), 1)
        same = bd_row // c == bd_col // c
        causal = jnp.logical_and(same, bd_col <= bd_row)
        incl = causal.astype(BF16)
        from_here = jnp.logical_and(same, bd_col >= bd_row).astype(BF16)
        lbv, ngv = lb_ref[...], ng_ref[...]

        def stacked(ref, ci):
            return jnp.concatenate([ref[pl.ds(pl.multiple_of(b * lp + ci * c, c), c), :] for b in range(nseq)], axis=0)

        def wide(x):
            return jnp.concatenate([jnp.where(seq_r == b, x, jnp.zeros_like(x)) for b in range(nseq)], axis=1)

        def own(x):
            return functools.reduce(jnp.add, [jnp.where(seq_r == b, x[:, b * LANES:(b + 1) * LANES], 0.0)
                                              for b in range(nseq)])

        def per_seq_rows(vals):
            return jnp.concatenate([jnp.broadcast_to(x, (c, LANES)) for x in vals], axis=0)

        def step(s, carry):
            dst, dlb, dng = carry
            ci = nch - 1 - s
            oc, gc, dogc = stacked(o_ref, ci), stacked(g_ref, ci), stacked(dog_ref, ci)
            r = lax.rsqrt(jnp.mean(oc * oc, axis=1, keepdims=True) + RMS_EPS)
            sgg = jax.nn.sigmoid(gc)
            dgc = dogc * (oc * r * ngv) * (sgg * (1.0 + gc * (1.0 - sgg)))
            don = dogc * (gc * sgg)
            dng = dng + _colsum(don * oc * r)
            tt = don * ngv
            do = r * (tt - oc * (r * r) * jnp.mean(tt * oc, axis=1, keepdims=True))
            st = jnp.concatenate([st_ref[b, ci] for b in range(nseq)], axis=1)
            sg, f = _hg_gates(stacked(f_ref, ci), lbv)
            bcum = _dot_exact_lhs(incl, jnp.log(f))
            lasts = [bcum[b * c + c - 1:b * c + c, :] for b in range(nseq)]
            blast = per_seq_rows(lasts)
            kk = 1.0 - f
            qc, vf = stacked(q_ref, ci), stacked(v_ref, ci)
            pdec = jnp.exp(bcum)
            ndec = jnp.exp(-bcum)
            edec = jnp.exp(blast - bcum)
            eb = jnp.exp(jnp.concatenate(lasts, axis=1))
            qd_f, kd_f, kend_f = qc * pdec, kk * ndec, kk * edec
            qd, kd, kend = qd_f.astype(BF16), kd_f.astype(BF16), kend_f.astype(BF16)
            vc, dob, stb, dstb = vf.astype(BF16), do.astype(BF16), st.astype(BF16), dst.astype(BF16)
            a = jnp.where(causal, _dot(qd, kd, 1, 1), 0.0).astype(BF16)
            da = jnp.where(causal, _dot(dob, vc, 1, 1), 0.0).astype(BF16)
            dv = _dot(a, dob, 0, 0) + _dot(wide(kend), dstb, 1, 1)
            dqd = _dot(da, kd, 1, 0) + own(_dot(dob, stb, 1, 0))
            dkd = _dot(da, qd, 0, 0)
            dkend = own(_dot(vc, dstb, 1, 0))
            sts = _colsum(dst * st)
            dkk = dkend * kend_f
            dblast = per_seq_rows([_colsum(jnp.where(seq_r == b, dkk, 0.0))
                                   + eb[:, b * LANES:(b + 1) * LANES] * sts[:, b * LANES:(b + 1) * LANES]
                                   for b in range(nseq)])
            dbcum = dqd * qd_f - dkd * kd_f - dkk
            dbcum = dbcum + jnp.where(last, dblast, 0.0)
            dlf = _dot_exact_lhs(from_here, dbcum)
            df = dlf / f - (dkd * ndec + dkend * edec)
            dfp = df * (1.0 - lbv) * sg * (1.0 - sg)
            dqc = dqd * pdec
            for b in range(nseq):
                rows = pl.ds(pl.multiple_of(b * lp + ci * c, c), c)
                blk_rows = slice(b * c, (b + 1) * c)
                dg_ref[rows, :], dv_ref[rows, :] = dgc[blk_rows], dv[blk_rows]
                dq_ref[rows, :], df_ref[rows, :] = dqc[blk_rows], dfp[blk_rows]
            dlb = dlb + _colsum(df * (1.0 - sg))
            dst = dst * eb + _dot(dob, wide(qd), 0, 0)
            return dst, dlb, dng

        z = jnp.zeros((1, LANES), F32)
        _, dlb, dng = _hg_loop(nch, step, (jnp.zeros((HG_DK, nseq * HG_DK), F32), z, z))
        dlb_ref[...] = dlb
        dng_ref[...] = dng

    blk = (nseq * lp, LANES)
    h = HG_HEADS
    vec = pl.BlockSpec((1, LANES), lambda hh: (0, hh))
    col = pl.BlockSpec(blk, lambda hh: (0, hh))
    return pl.pallas_call(
        body, name="hgrn_bwd", grid=(h,),
        in_specs=[col,
                  pl.BlockSpec(blk, lambda hh: (0, h + hh)),
                  pl.BlockSpec(blk, lambda hh: (0, 2 * h + hh)),
                  pl.BlockSpec(blk, lambda hh: (0, 3 * h + hh)),
                  col, col,
                  pl.BlockSpec((nseq, None, nch, HG_DK, HG_DK), lambda hh: (0, hh, 0, 0, 0)),
                  vec, vec],
        out_specs=[col] * 4 + [vec, vec],
        out_shape=[jax.ShapeDtypeStruct((t, D_MODEL), F32)] * 4 + [jax.ShapeDtypeStruct((1, D_MODEL), F32)] * 2,
        compiler_params=pltpu.CompilerParams(dimension_semantics=("parallel",), vmem_limit_bytes=HGRN_BWD_VMEM))(
            proj, proj, proj, proj, o, dog, states, lb, ng)


def _lower_bound(gamma):
    p = jax.nn.softmax(gamma, axis=0)
    return (jnp.cumsum(p, axis=0) - p[0])[1][None, :]


def local_step(x, tgt, w, hooks=None):
    nseq, seq, _ = x.shape
    lp = -(-(N_META + seq) // SB_BLOCK) * SB_BLOCK
    t = nseq * lp
    pad = lp - N_META - seq
    h0 = jnp.concatenate([jnp.broadcast_to(w['meta'][None], (nseq, N_META, D_MODEL)), x,
                          jnp.zeros((nseq, pad, D_MODEL), F32)], axis=1).reshape(t, D_MODEL)
    tgt_p = jnp.pad(tgt, ((0, 0), (N_META, pad), (0, 0))).reshape(t, D_MODEL)
    row = lambda v: v.reshape(1, -1)
    g = {}

    proj = matmul("in_ab", h0, w['w_in_ab'], 'nn')
    att = attn_fwd(proj, nseq, lp, ex=hooks.gather_late if hooks else None)
    b_out, sb_tot, sb_first = att[:3]
    if hooks:
        w = {**w, **hooks.late_weights(att[3:])}
    s5_names = ['s5_lam_re', 's5_lam_im', 's5_log_dt', 's5_b_re', 's5_b_im', 's5_c_re', 's5_c_im', 's5_d']
    mats, mats_vjp = jax.vjp(s5_matrices, *[w[n][0] for n in s5_names])
    ug = _to_groups(proj[:, :S5_WIDTH])
    s5_pows = s5_scan_powers(*[w[n][0] for n in s5_names[:3]])
    yg, sst = s5_fwd(ug, *mats[:3], *s5_pows, nseq)
    ys5 = _from_groups(yg)
    y = gelu_fwd("gelu", ys5)
    gp = matmul("glu", y, w['s5_w_glu'], 'nn', bias=w['s5_b_glu'])
    a_out = glu_gate("glu_gate", y, gp)
    cat = jnp.concatenate([a_out, b_out], axis=1)
    mix0 = matmul("out_ab", cat, w['w_out_ab'], 'nn')
    h1, xh1, rs1 = ln_fwd("ln_mix0", h0, mix0, row(w['ln_mix_g'][0]), row(w['ln_mix_b'][0]))
    hid0 = matmul("up0", h1, w['mlp_w_up'][0], 'nn', bias=row(w['mlp_b_up'][0]), act='relu2', out_dtype=BF16)
    dn0 = matmul("down0", hid0, w['mlp_w_down'][0], 'nn', bias=row(w['mlp_b_down'][0]))
    h2, xh2, rs2 = ln_fwd("ln_mlp0", h1, dn0, row(w['ln_mlp_g'][0]), row(w['ln_mlp_b'][0]))

    projc = matmul("in_c", h2, w['w_in_c'], 'nn')
    lb, lb_vjp = jax.vjp(_lower_bound, w['hgrn_gamma'])
    ng = w['hgrn_norm_g']
    res = hgrn_fwd(projc, lb, ng, nseq, lp, ex=hooks.gather_last if hooks else None)
    o, og, states = res[:3]
    if hooks:
        w = {**w, **hooks.last_weights(res[3:])}
    mix1 = matmul("out_c", og, w['w_out_c'], 'nn')
    h3, xh3, rs3 = ln_fwd("ln_mix1", h2, mix1, row(w['ln_mix_g'][1]), row(w['ln_mix_b'][1]))
    hid1 = matmul("up1", h3, w['mlp_w_up'][1], 'nn', bias=row(w['mlp_b_up'][1]), act='relu2', out_dtype=BF16)
    dn1 = matmul("down1", hid1, w['mlp_w_down'][1], 'nn', bias=row(w['mlp_b_down'][1]))
    last_ln = ln_loss("ln_mlp1_loss", h3, dn1, row(w['ln_mlp_g'][1]), row(w['ln_mlp_b'][1]), tgt_p, lp, seq)
    loss = last_ln[4]

    def mlp_bwd(l, ln_back, hid, h_in):
        dpre, dg_, db_, dbd = ln_back
        dpu = matmul(f"down{l}_dx", dpre, w['mlp_w_down'][l], 'nt', relu2_of=hid, out_dtype=BF16)
        dwd = matmul(f"down{l}_dw", hid, dpre, 'tn')
        dh_in = matmul(f"up{l}_dx", dpu, w['mlp_w_up'][l], 'nt', add=dpre, add_scale=ALPHA)
        dwu, dbu = matmul(f"up{l}_dw", h_in, dpu, 'tn', out_slots=N_CHIPS, colsum=True)
        return dh_in, dict(ln_mlp_g=dg_, ln_mlp_b=db_, mlp_b_down=dbd, mlp_b_up=dbu, mlp_w_up=dwu, mlp_w_down=dwd)

    dh3, gm1 = mlp_bwd(1, last_ln[:4], hid1, h3)
    dpre, dg1, db1, _ = ln_bwd("ln_mix1_b", dh3, xh3, rs3, row(w['ln_mix_g'][1]))
    g['w_out_c'] = matmul("out_c_dw", og, dpre, 'tn')
    dog = matmul("out_c_dx", dpre, w['w_out_c'], 'nt')
    dq, df, di, dgg, dlb, dng = hgrn_bwd(projc, o, dog, states, lb, ng, nseq, lp)
    dprojc = jnp.concatenate([dq, df, di, dgg], axis=1)
    dh2 = matmul("in_c_dx", dprojc, w['w_in_c'], 'nt', add=dpre, add_scale=ALPHA)
    g['w_in_c'] = matmul("in_c_dw", h2, dprojc, 'tn', out_slots=N_CHIPS)
    g['hgrn_gamma'] = lb_vjp(dlb)[0]
    g['hgrn_norm_g'] = dng

    dh1, gm0 = mlp_bwd(0, ln_bwd("ln_mlp0_b", dh2, xh2, rs2, row(w['ln_mlp_g'][0])), hid0, h1)
    dpre, dg0, db0, _ = ln_bwd("ln_mix0_b", dh1, xh1, rs1, row(w['ln_mix_g'][0]))
    g['w_out_ab'] = matmul("out_ab_dw", cat, dpre, 'tn')
    dcat = matmul("out_ab_dx", dpre, w['w_out_ab'], 'nt')
    dgp, da_s, dbglu = glu_gate_bwd("glu_gate_b", dcat[:, :S5_WIDTH], y, gp)
    g['s5_w_glu'] = matmul("glu_dw", y, dgp, 'tn')
    g['s5_b_glu'] = dbglu
    dy5 = matmul("glu_dx", dgp, w['s5_w_glu'], 'nt', add=da_s)
    dys5 = gelu_bwd("gelu_b", dy5, ys5)
    for n in ('mlp_w_up', 'mlp_w_down'):
        g[n] = [gm0[n], gm1[n]]
    g['ln_mix_g'] = jnp.concatenate([dg0, dg1], axis=0)
    g['ln_mix_b'] = jnp.concatenate([db0, db1], axis=0)
    for n in ('ln_mlp_g', 'ln_mlp_b', 'mlp_b_down', 'mlp_b_up'):
        g[n] = jnp.concatenate([gm0[n], gm1[n]], axis=0)
    res = s5_bwd(_to_groups(dys5), ug, sst, *mats[:3], *s5_pows, nseq, ex=hooks.fold_early(g) if hooks else None)
    dug, da1, da2, da3, dare, daim = res[:6]
    for n, v in zip(s5_names, mats_vjp((da1, da2, da3, dare, daim))):
        g[n] = v[None]
    ex = hooks.scatter_early(g, loss, res[6:]) if hooks else None
    res = attn_bwd(proj, sb_tot, sb_first, dcat[:, S5_WIDTH:], nseq, lp, ex=ex)
    dqa, dka, dva = res[:3]
    if hooks:
        hooks.scattered(res[3:])
    dproj = jnp.concatenate([_from_groups(dug), dqa, dka, dva], axis=1)
    dh0 = matmul("in_ab_dx", dproj, w['w_in_ab'], 'nt', add=dpre, add_scale=ALPHA)
    g['w_in_ab'] = matmul("in_ab_dw", h0, dproj, 'tn', out_slots=N_CHIPS)

    dh0 = dh0.reshape(nseq, lp, D_MODEL)
    grad_x = dh0[:, N_META:N_META + seq]
    g['meta'] = jnp.sum(dh0[:, :N_META], axis=0)
    return loss, grad_x, g


class Exchange:
    def __init__(self, ins, out_shapes, n_remote, build, in_place):
        self.ins, self.out_shapes, self.n_remote, self.build, self.in_place = ins, out_shapes, n_remote, build, in_place

    def sems(self):
        return [pltpu.SemaphoreType.DMA((self.n_remote,)), pltpu.SemaphoreType.DMA((self.n_remote,))]

    def aliases(self):
        return self.in_place if isinstance(self.in_place, dict) else \
            {i: i for i in range(len(self.ins) if self.in_place else 0)}

    def beside(self, other):
        na, nao = len(self.ins), len(self.out_shapes)

        def build(in_refs, out_refs, me):
            pa = self.build(in_refs[:na], out_refs[:nao], me)
            pb = other.build(in_refs[na:], out_refs[nao:], me)
            n = max(len(pa), len(pb))
            return [(pa[i] if i < len(pa) else []) + (pb[i] if i < len(pb) else []) for i in range(n)]
        al = dict(self.aliases())
        al.update({na + i: nao + j for i, j in other.aliases().items()})
        return Exchange(list(self.ins) + list(other.ins), list(self.out_shapes) + list(other.out_shapes),
                        self.n_remote + other.n_remote, build, al)

    def phases(self, in_refs, out_refs, ssem, rsem):
        me = (lax.axis_index("x"), lax.axis_index("y"), lax.axis_index("c"))
        out, k = [], 0
        for phase in self.build(in_refs, out_refs, me):
            out.append(functools.partial(
                lambda phase, k: [pltpu.make_async_remote_copy(src_ref=s, dst_ref=d, send_sem=ssem.at[k + i],
                                                               recv_sem=rsem.at[k + i], device_id=p,
                                                               device_id_type=MESH)
                                  for i, (s, d, p) in enumerate(phase)], phase, k))
            k += len(phase)
        return out

    def start(self, in_refs, out_refs, ssem, rsem):
        for cp in self.phases(in_refs, out_refs, ssem, rsem)[0]():
            cp.start()

    def finish(self, in_refs, out_refs, ssem, rsem):
        phases = self.phases(in_refs, out_refs, ssem, rsem)
        for cp in phases[0]():
            cp.wait()
        for make in phases[1:]:
            copies = make()
            for cp in copies:
                cp.start()
            for cp in copies:
                cp.wait()


def _exchange(name, ex):
    ni, no = len(ex.ins), len(ex.out_shapes)

    def body(*refs):
        in_refs, out_refs, sems = refs[:ni], refs[ni:ni + no], refs[ni + no:]
        ex.start(in_refs, out_refs, *sems)
        ex.finish(in_refs, out_refs, *sems)

    anyspec = pl.BlockSpec(memory_space=pl.ANY)
    return pl.pallas_call(
        body, name=name, in_specs=[anyspec] * ni, out_specs=[anyspec] * no, out_shape=ex.out_shapes,
        input_output_aliases=ex.aliases(), scratch_shapes=ex.sems())(*ex.ins)


def _chip_peers(me):
    x, y, c = me
    return [(x ^ (m >> 1), y ^ (m & 1), c) for m in (1, 2, 3)]


def _half(ref, c, axis):
    h = ref.shape[axis] // 2
    idx = [slice(None)] * axis + [pl.ds(c * h, h)]
    return ref.at[tuple(idx)]


def _like(arrs):
    return [jax.ShapeDtypeStruct(a.shape, a.dtype) for a in arrs]


def gather_over_chips(bufs):
    def build(in_refs, out_refs, me):
        x, y, c = me
        j = 2 * x + y
        sib = (x, y, 1 - c)
        ici = [(_half(o.at[j], c, 0), _half(o.at[j], c, 0), p) for o in out_refs for p in _chip_peers(me)]
        d2d = [(_half(o.at[2 * p[0] + p[1]], c, 0), _half(o.at[2 * p[0] + p[1]], c, 0), sib)
               for o in out_refs for p in _chip_peers(me)]
        return [ici, d2d]
    return Exchange(bufs, _like(bufs), 6 * len(bufs), build, True)


def fold_cores(fulls):
    def build(in_refs, out_refs, me):
        x, y, c = me
        return [[(_half(s, 1 - c, 1), o, (x, y, 1 - c)) for s, o in zip(in_refs, out_refs)]]
    shapes = [jax.ShapeDtypeStruct((f.shape[0], f.shape[1] // 2, f.shape[2]), f.dtype) for f in fulls]
    return Exchange(fulls, shapes, len(fulls), build, False)


def scatter_over_chips(parts):
    def build(in_refs, out_refs, me):
        j = 2 * me[0] + me[1]
        return [[(s.at[2 * p[0] + p[1]], o.at[j], p) for s, o in zip(in_refs, out_refs) for p in _chip_peers(me)]]
    return Exchange(parts, _like(parts), 3 * len(parts), build, False)


def join_cores(bufs):
    def build(in_refs, out_refs, me):
        x, y, c = me
        return [[(o.at[c], o.at[c], (x, y, 1 - c)) for o in out_refs]]
    return Exchange(bufs, _like(bufs), len(bufs), build, True)


def gather_over_devices(buf):
    def build(in_refs, out_refs, me):
        x, y, c = me
        out = out_refs[0]
        sib = (x, y, 1 - c)
        mine = out.at[4 * x + 2 * y + c]
        first = [(mine, mine, sib)] + [(mine, mine, p) for p in _chip_peers(me)]
        second = [(out.at[4 * p[0] + 2 * p[1] + c], out.at[4 * p[0] + 2 * p[1] + c], sib) for p in _chip_peers(me)]
        return [first, second]
    return Exchange([buf], _like([buf]), N_DEV - 1, build, True)


def _slot_call(name, body, scalars, ins, in_maps, out_shapes, out_maps, blk, grid):
    gs = pltpu.PrefetchScalarGridSpec(
        num_scalar_prefetch=1, grid=grid,
        in_specs=[pl.BlockSpec((None,) + blk, m) for m in in_maps],
        out_specs=[pl.BlockSpec((None,) + blk, m) for m in out_maps])
    return pl.pallas_call(body, name=name, grid_spec=gs, out_shape=out_shapes,
                          compiler_params=_params(("arbitrary",) * len(grid)))(scalars, *ins)


def _group_rows(r, n):
    return _pick(r, (512, 256, 128, 64, 32, 16) if n <= 2 else (256, 128, 64, 32, 16))


def cast_into_slot(name, ws, chip, dtype):
    n = len(ws)
    r, wd = ws[0].shape
    tm = _group_rows(r, n)

    def body(s_ref, *refs):
        for w_ref, o_ref in zip(refs[:n], refs[n:]):
            o_ref[...] = w_ref[...].astype(o_ref.dtype)

    gs = pltpu.PrefetchScalarGridSpec(
        num_scalar_prefetch=1, grid=(r // tm,),
        in_specs=[pl.BlockSpec((tm, wd), lambda i, s: (i, 0))] * n,
        out_specs=[pl.BlockSpec((None, tm, wd), lambda i, s: (s[0], i, 0))] * n)
    return pl.pallas_call(body, name=name, grid_spec=gs,
                          out_shape=[jax.ShapeDtypeStruct((N_CHIPS, r, wd), dtype)] * n,
                          compiler_params=_params(("arbitrary",)))(chip.reshape(1), *ws)


def sum_cores(name, fulls, theirs, core):
    n = len(fulls)
    s, r, wd = fulls[0].shape
    h = r // 2
    tm = _group_rows(h, n)
    nt = h // tm

    def body(s_ref, *refs):
        for k in range(n):
            refs[2 * n + k][...] = (refs[2 * k][...] + refs[2 * k + 1][...]).astype(BF16)

    ins = [x for pair in zip(fulls, theirs) for x in pair]
    return _slot_call(name, body, core.reshape(1), ins,
                      [lambda k, i, s_: (k, s_[0] * nt + i, 0), lambda k, i, s_: (k, i, 0)] * n,
                      [jax.ShapeDtypeStruct((s, h, wd), BF16)] * n, [lambda k, i, s_: (k, i, 0)] * n,
                      (tm, wd), (s, nt))


def sum_chips(name, owns, recvs, chip, core):
    n = len(owns)
    s, r, wd = owns[0].shape
    tm = _group_rows(r, n)

    def body(s_ref, *refs):
        for k in range(n):
            acc = refs[4 * k][...].astype(F32)
            for ref in refs[4 * k + 1:4 * k + 4]:
                acc = acc + ref[...].astype(F32)
            refs[4 * n + k][...] = acc

    maps = [lambda i, s_: (s_[0], i, 0)]
    maps += [functools.partial(lambda i, s_, m: (s_[0] ^ m, i, 0), m=m) for m in (1, 2, 3)]
    ins = [x for own, recv in zip(owns, recvs) for x in (own, recv, recv, recv)]
    return _slot_call(name, body, jnp.stack([chip, core]), ins, maps * n,
                      [jax.ShapeDtypeStruct((2, r, wd), F32)] * n, [lambda i, s_: (s_[1], i, 0)] * n,
                      (tm, wd), (r // tm,))


def sum_slots(name, arr):
    s, r, wd = arr.shape
    tm = _pick(r, (512, 256, 128, 64, 32, 16, 8))

    def body(*refs):
        acc = refs[0][...].astype(F32)
        for ref in refs[1:s]:
            acc = acc + ref[...].astype(F32)
        refs[s][...] = acc

    specs = [pl.BlockSpec((None, tm, wd), functools.partial(lambda i, k: (k, i, 0), k=k)) for k in range(s)]
    return pl.pallas_call(body, name=name, grid=(r // tm,), in_specs=specs,
                          out_specs=pl.BlockSpec((tm, wd), lambda i: (i, 0)),
                          out_shape=jax.ShapeDtypeStruct((r, wd), F32),
                          compiler_params=_params(("parallel",)))(*([arr] * s))


def _pack(arrs):
    flat = jnp.concatenate([a.reshape(-1) for a in arrs])
    n = flat.shape[0]
    padded = -(-n // (512 * LANES)) * (512 * LANES)
    return jnp.pad(flat, (0, padded - n)).reshape(-1, LANES)


def _unpack(packed, shapes):
    flat = packed.reshape(-1)
    out, pos = [], 0
    for s in shapes:
        n = math.prod(s)
        out.append(flat[pos:pos + n].reshape(s))
        pos += n
    return out


def _as2d(a):
    return a.reshape(-1, a.shape[-1])


def kernel(x, meta, w_in_ab, s5_lam_re, s5_lam_im, s5_log_dt, s5_b_re, s5_b_im, s5_c_re, s5_c_im, s5_d, s5_w_glu, s5_b_glu, w_out_ab, w_in_c, hgrn_gamma, hgrn_norm_g, w_out_c, ln_mix_g, ln_mix_b, mlp_w_up, mlp_b_up, mlp_w_down, mlp_b_down, ln_mlp_g, ln_mlp_b, loss_target, m_meta, m_w_in_ab, m_s5_lam_re, m_s5_lam_im, m_s5_log_dt, m_s5_b_re, m_s5_b_im, m_s5_c_re, m_s5_c_im, m_s5_d, m_s5_w_glu, m_s5_b_glu, m_w_out_ab, m_w_in_c, m_hgrn_gamma, m_hgrn_norm_g, m_w_out_c, m_ln_mix_g, m_ln_mix_b, m_mlp_w_up, m_mlp_b_up, m_mlp_w_down, m_mlp_b_down, m_ln_mlp_g, m_ln_mlp_b, v_meta, v_w_in_ab, v_s5_lam_re, v_s5_lam_im, v_s5_log_dt, v_s5_b_re, v_s5_b_im, v_s5_c_re, v_s5_c_im, v_s5_d, v_s5_w_glu, v_s5_b_glu, v_w_out_ab, v_w_in_c, v_hgrn_gamma, v_hgrn_norm_g, v_w_out_c, v_ln_mix_g, v_ln_mix_b, v_mlp_w_up, v_mlp_b_up, v_mlp_w_down, v_mlp_b_down, v_ln_mlp_g, v_ln_mlp_b):
    given = dict(locals())
    wts = {n: given[n] for n in WEIGHTS}
    ms = {n: given['m_' + n] for n in WEIGHTS}
    vs = {n: given['v_' + n] for n in WEIGHTS}
    chip = 2 * lax.axis_index("x") + lax.axis_index("y")

    core = lax.axis_index("c")
    parts = [(n, l) for n in BIG for l in (range(DEPTH) if n.startswith('mlp_') else [0])]

    def by_shape(tag, fn, keys, *lists):
        out = {}
        for shape in dict.fromkeys(a.shape for a in lists[0]):
            idx = [i for i, a in enumerate(lists[0]) if a.shape == shape]
            res = fn(f"{tag}_{keys[idx[0]][0]}{keys[idx[0]][1]}_x{len(idx)}", *[[lst[i] for i in idx] for lst in lists])
            out.update({keys[i]: r for i, r in zip(idx, res)})
        return [out[k] for k in keys]

    shards = dict(zip(parts, by_shape("cast", lambda nm, ws: cast_into_slot(nm, ws, chip, BF16), parts,
                                      [wts[n][l] for n, l in parts])))
    small_sh = jnp.concatenate([meta, hgrn_norm_g, jnp.zeros((15, meta.shape[1]), F32)], axis=0)
    first, late = parts[:1], parts[1:]
    w_in_ab_all, sm = _exchange("gather_first", gather_over_chips(
        [shards[p] for p in first] + list(cast_into_slot("cast_small", [small_sh], chip, F32))))
    w = {n: wts[n] for n in SMALL}
    w['meta'] = sm[:, :N_META].transpose(1, 0, 2).reshape(N_META, D_MODEL)
    w['hgrn_norm_g'] = sm[:, N_META:N_META + 1].transpose(1, 0, 2).reshape(1, D_MODEL)
    w['w_in_ab'] = w_in_ab_all

    def slot_major(v):
        return v if v.ndim == 3 else v.reshape(N_CHIPS, -1, v.shape[-1])

    def whole_grads(ps, g):
        return [slot_major(g[n][l] if n.startswith('mlp_') else g[n]) for n, l in ps]

    def chip_sums_of(ps, gfull, theirs):
        return by_shape("sum_cores", lambda nm, a, b: sum_cores(nm, a, b, core), ps, gfull, list(theirs))

    def all_devices(arrs):
        packed = _pack(arrs)
        return gather_over_devices(lax.dynamic_update_slice(
            jnp.zeros((N_DEV,) + packed.shape, F32), packed[None], (2 * chip + core, 0, 0)))

    early_small = [n for n in SMALL if n != 'meta']
    last = [('w_out_c', 0), ('mlp_w_up', 1), ('mlp_w_down', 1)]
    mid = [p for p in late if p not in last]

    class Hooks:
        gather_late = gather_over_chips([shards[p] for p in mid])
        gather_last = gather_over_chips([shards[p] for p in last])

        @staticmethod
        def late_weights(filled):
            fw = Hooks.fw = dict(zip(mid, filled))
            return {'s5_w_glu': fw['s5_w_glu', 0].reshape(S5_WIDTH, S5_WIDTH),
                    'w_out_ab': fw['w_out_ab', 0].reshape(D_MODEL, D_MODEL),
                    'w_in_c': fw['w_in_c', 0],
                    'mlp_w_up': [fw['mlp_w_up', 0]],
                    'mlp_w_down': [fw['mlp_w_down', 0].reshape(D_FF, D_MODEL)]}

        @staticmethod
        def last_weights(filled):
            fw = dict(zip(last, filled))
            return {'w_out_c': fw['w_out_c', 0].reshape(D_MODEL, D_MODEL),
                    'mlp_w_up': [Hooks.fw['mlp_w_up', 0], fw['mlp_w_up', 1]],
                    'mlp_w_down': [Hooks.fw['mlp_w_down', 0].reshape(D_FF, D_MODEL),
                                   fw['mlp_w_down', 1].reshape(D_FF, D_MODEL)]}

        @staticmethod
        def fold_early(g):
            Hooks.whole = whole_grads(late, g)
            return fold_cores(Hooks.whole)

        @staticmethod
        def scatter_early(g, loss, theirs):
            Hooks.sums = chip_sums_of(late, Hooks.whole, theirs)
            Hooks.small_shapes = [(LANES,)] + [g[n].shape for n in early_small]
            return scatter_over_chips(Hooks.sums).beside(
                all_devices([loss.reshape(-1)] + [g[n] for n in early_small]))

        @staticmethod
        def scattered(outs):
            Hooks.recv, Hooks.small = list(outs[:-1]), outs[-1]

    loss, grad_x, g = local_step(x, loss_target, w, Hooks)

    whole = whole_grads(first, g)
    sums = chip_sums_of(first, whole, _exchange("fold_last", fold_cores(whole)))
    *recv, meta_slots = _exchange("scatter_last", scatter_over_chips(sums).beside(all_devices([g['meta']])))
    reduced = _exchange("join_grads", join_cores(
        by_shape("sum_chips", lambda nm, a, b: sum_chips(nm, a, b, chip, core), first + late,
                 sums + Hooks.sums, list(recv) + Hooks.recv)))
    reduced = dict(zip(first + late, [_as2d(r) for r in reduced]))
    red = _unpack(sum_slots("sum_small", Hooks.small), Hooks.small_shapes)
    loss_out = red[0][0]
    gs = dict(zip(early_small, red[1:]))
    gs['meta'] = _unpack(sum_slots("sum_meta", meta_slots), [g['meta'].shape])[0]
    gs['meta'] = lax.dynamic_slice_in_dim(gs['meta'], chip * meta.shape[1], meta.shape[1], axis=1)
    gs['hgrn_norm_g'] = lax.dynamic_slice_in_dim(gs['hgrn_norm_g'].reshape(1, -1), chip * meta.shape[1],
                                                 meta.shape[1], axis=1)

    grads, deltas, new_m, new_v = {}, {}, {}, {}
    for n in BIG:
        r = jnp.concatenate([reduced[n, l] for l in range(wts[n].shape[0])], axis=0)
        res = adamw("adamw_" + n, _as2d(wts[n]), [r], _as2d(ms[n]), _as2d(vs[n]))
        grads[n], deltas[n], new_m[n], new_v[n] = [r.reshape(wts[n].shape) for r in res]
    shapes = [wts[n].shape for n in SMALL]
    res = adamw("adamw_small", _pack([wts[n] for n in SMALL]), [_pack([gs[n] for n in SMALL])],
                _pack([ms[n] for n in SMALL]), _pack([vs[n] for n in SMALL]))
    for d, r in zip((grads, deltas, new_m, new_v), res):
        d.update(zip(SMALL, _unpack(r, shapes)))
    return (loss_out, grad_x, *[grads[n] for n in WEIGHTS], *[deltas[n] for n in WEIGHTS],
            *[new_m[n] for n in WEIGHTS], *[new_v[n] for n in WEIGHTS])
```

```python
import functools
import math

import jax
import jax.numpy as jnp
from jax import lax
from jax.experimental import pallas as pl
from jax.experimental.pallas import tpu as pltpu

F32 = jnp.float32
BF16 = jnp.bfloat16

D_MODEL = 1024
N_META = 16
DEPTH = 2
S5_WIDTH = 512
S5_GROUP = 16
S5_GROUPS = 32
S5_STATE = 64
S5_CHUNK = 16
SB_WIDTH = 512
SB_BLOCK = 128
SB_DEAD = -110.0
HG_HEADS = 8
HG_DK = 128
HG_CHUNK = 64
HG_UNROLL = 2
D_FF = 4096
ALPHA = (2.0 * DEPTH) ** 0.25
LN_EPS = 1e-5
RMS_EPS = 1e-6
ADAM_LR = 0.001
ADAM_B1 = 0.9
ADAM_B2 = 0.999
ADAM_EPS = 1e-08
ADAM_WD = 0.01
ADAM_STEP = 10
N_CHIPS = 4
N_DEV = 8
LANES = 128
MESH = pl.DeviceIdType.MESH
VMEM_LIMIT = 56 * 1024 * 1024
WHOLE_K_BYTES = 42 * 1024 * 1024
HGRN_BWD_VMEM = 60 * 1024 * 1024

WEIGHTS = ['meta', 'w_in_ab', 's5_lam_re', 's5_lam_im', 's5_log_dt', 's5_b_re', 's5_b_im', 's5_c_re', 's5_c_im',
           's5_d', 's5_w_glu', 's5_b_glu', 'w_out_ab', 'w_in_c', 'hgrn_gamma', 'hgrn_norm_g', 'w_out_c',
           'ln_mix_g', 'ln_mix_b', 'mlp_w_up', 'mlp_b_up', 'mlp_w_down', 'mlp_b_down', 'ln_mlp_g', 'ln_mlp_b']
BIG = ['w_in_ab', 's5_w_glu', 'w_out_ab', 'w_in_c', 'w_out_c', 'mlp_w_up', 'mlp_w_down']
SMALL = [n for n in WEIGHTS if n not in BIG]


def _params(sem=None):
    return pltpu.CompilerParams(dimension_semantics=sem, vmem_limit_bytes=VMEM_LIMIT)


def _pick(n, cands):
    for c in cands:
        if n % c == 0:
            return c
    return n


def _dot(a, b, ca, cb):
    return lax.dot_general(a, b, (((ca,), (cb,)), ((), ())), preferred_element_type=F32)


def _split3(x):
    hi = x.astype(BF16)
    r = x - hi.astype(F32)
    mid = r.astype(BF16)
    lo = (r - mid.astype(F32)).astype(BF16)
    return hi, mid, lo


def _dot_exact_rhs(x, m, cx, cm):
    hi = x.astype(BF16)
    lo = (x - hi.astype(F32)).astype(BF16)
    return _dot(hi, m, cx, cm) + _dot(lo, m, cx, cm)


def _dot3(a, b, ca, cb):
    ah = a.astype(BF16)
    al = (a - ah.astype(F32)).astype(BF16)
    bh = b.astype(BF16)
    bl = (b - bh.astype(F32)).astype(BF16)
    return _dot(ah, bh, ca, cb) + (_dot(ah, bl, ca, cb) + _dot(al, bh, ca, cb))


def rowwise(name, fn, row_ins, bc_ins, out_widths, sum_widths=(), out_dtypes=None, tm=None):
    t = row_ins[0].shape[0]
    if tm is None:
        wmax = max([a.shape[1] for a in row_ins] + list(out_widths))
        tm = _pick(t, (272, 256, 128, 64, 16, 8)) if wmax > 1024 else _pick(t, (544, 512, 256, 128, 64, 16, 8))
    nr, nb, no, ns = len(row_ins), len(bc_ins), len(out_widths), len(sum_widths)
    out_dtypes = out_dtypes or [F32] * no

    def body(*refs):
        i = pl.program_id(0)
        rows = [r[...] for r in refs[:nr]]
        bcs = [r[...] for r in refs[nr:nr + nb]]
        outs, sums = fn(i, tm, rows, bcs)
        for r, v in zip(refs[nr + nb:nr + nb + no], outs):
            r[...] = v.astype(r.dtype)
        if ns:
            srefs = refs[nr + nb + no:]

            @pl.when(i == 0)
            def _():
                for r in srefs:
                    r[...] = jnp.zeros(r.shape, r.dtype)

            for r, v in zip(srefs, sums):
                r[...] += v

    in_specs = [pl.BlockSpec((tm, a.shape[1]), lambda i: (i, 0)) for a in row_ins]
    in_specs += [pl.BlockSpec(a.shape, lambda i: (0, 0)) for a in bc_ins]
    out_specs = [pl.BlockSpec((tm, w), lambda i: (i, 0)) for w in out_widths]
    out_specs += [pl.BlockSpec((1, w), lambda i: (0, 0)) for w in sum_widths]
    out_shape = [jax.ShapeDtypeStruct((t, w), dt) for w, dt in zip(out_widths, out_dtypes)]
    out_shape += [jax.ShapeDtypeStruct((1, w), F32) for w in sum_widths]
    res = pl.pallas_call(body, name=name, grid=(t // tm,), in_specs=in_specs, out_specs=out_specs,
                         out_shape=out_shape, compiler_params=_params(("arbitrary",)))(*row_ins, *bc_ins)
    return res


def _colsum(v):
    return jnp.sum(v, axis=0, keepdims=True)


def matmul(name, a, b, mode, *, bias=None, act=None, add=None, add_scale=1.0, relu2_of=None, colsum=False,
           out_dtype=F32, out_slots=0, tm=None, tn=None, tk=None):
    slotted = b.ndim == 3
    if mode == 'nn':
        m, k = a.shape
        n = b.shape[-1] * (b.shape[0] if slotted else 1)
    elif mode == 'nt':
        m, k = a.shape
        n = b.shape[-2]
    else:
        k, m = a.shape
        n = b.shape[-1]
    ns = b.shape[-1] if slotted else None
    if mode == 'tn':
        tm = tm or _pick(m, (512, 256, 128))
        nw = n if not out_slots else n // out_slots
        if tn is None and tk is None:
            for cand in (512, 256):
                if nw % cand == 0 and (2 * k * (tm * a.dtype.itemsize + cand * b.dtype.itemsize)
                                       + 2 * tm * cand * 4 <= WHOLE_K_BYTES):
                    tn, tk = cand, k
                    break
        tn = tn or _pick(nw, (1024, 512, 256, 128))
        tk = tk or _pick(k, (1088, 1024, 768, 512, 384, 256, 128))
    else:
        tm = tm or _pick(m, (1088, 1024, 768, 512, 384, 256, 128))
        tn = tn or _pick(n if not (slotted and mode == 'nn') else ns, (1024, 512, 256, 128))
        extra = sum(x is not None for x in (add, relu2_of)) * 4
        if tk is None and not slotted and (2 * (tm * k * a.dtype.itemsize + k * tn * b.dtype.itemsize)
                                           + 2 * tm * tn * (4 + extra) <= WHOLE_K_BYTES):
            tk = k
        tk = tk or _pick(k if not (slotted and mode == 'nt') else ns, (1024, 512, 256, 128))
    gm, gn, gk = m // tm, n // tn, k // tk

    if mode == 'nn':
        a_spec = pl.BlockSpec((tm, tk), lambda i, j, kk: (i, kk))
        if slotted:
            per = ns // tn
            b_spec = pl.BlockSpec((None, tk, tn), lambda i, j, kk: (j // per, kk, j % per))
        else:
            b_spec = pl.BlockSpec((tk, tn), lambda i, j, kk: (kk, j))
        ca, cb = 1, 0
    elif mode == 'nt':
        a_spec = pl.BlockSpec((tm, tk), lambda i, j, kk: (i, kk))
        if slotted:
            per = ns // tk
            b_spec = pl.BlockSpec((None, tn, tk), lambda i, j, kk: (kk // per, j, kk % per))
        else:
            b_spec = pl.BlockSpec((tn, tk), lambda i, j, kk: (j, kk))
        ca, cb = 1, 1
    else:
        a_spec = pl.BlockSpec((tk, tm), lambda i, j, kk: (kk, i))
        b_spec = pl.BlockSpec((tk, tn), lambda i, j, kk: (kk, j))
        ca, cb = 0, 0
    in_specs = [a_spec, b_spec]
    args = [a, b]
    if bias is not None:
        in_specs.append(pl.BlockSpec((1, tn), lambda i, j, kk: (0, j)))
        args.append(bias)
    if add is not None:
        in_specs.append(pl.BlockSpec((tm, tn), lambda i, j, kk: (i, j)))
        args.append(add)
    if relu2_of is not None:
        in_specs.append(pl.BlockSpec((tm, tn), lambda i, j, kk: (i, j)))
        args.append(relu2_of)
    if out_slots:
        per_o = (n // out_slots) // tn
        out_spec = pl.BlockSpec((None, tm, tn), lambda i, j, kk: (j // per_o, i, j % per_o))
        out_shape = jax.ShapeDtypeStruct((out_slots, m, n // out_slots), out_dtype)
    else:
        out_spec = pl.BlockSpec((tm, tn), lambda i, j, kk: (i, j))
        out_shape = jax.ShapeDtypeStruct((m, n), out_dtype)
    grid = (gm, gn, gk)
    if colsum:
        assert mode == 'tn'
        out_spec = [out_spec, pl.BlockSpec((1, tn), lambda i, j, kk: (0, j))]
        out_shape = [out_shape, jax.ShapeDtypeStruct((1, n), F32)]

        def swapped(spec):
            return pl.BlockSpec(spec.block_shape, functools.partial(lambda j, i, kk, f: f(i, j, kk), f=spec.index_map))
        in_specs = [swapped(sp) for sp in in_specs]
        out_spec = [swapped(sp) for sp in out_spec]
        grid = (gn, gm, gk)

    def body(*refs):
        a_ref, b_ref = refs[0], refs[1]
        pos = 2
        bias_ref = add_ref = hid_ref = cs_ref = None
        if bias is not None:
            bias_ref = refs[pos]
            pos += 1
        if add is not None:
            add_ref = refs[pos]
            pos += 1
        if relu2_of is not None:
            hid_ref = refs[pos]
            pos += 1
        o_ref = refs[pos]
        pos += 1
        if colsum:
            cs_ref = refs[pos]
            pos += 1
        acc_ref = refs[pos] if gk > 1 else None
        bt = b_ref[...]
        p = _dot(a_ref[...].astype(BF16), bt.astype(BF16), ca, cb)
        kk = pl.program_id(2)

        if colsum:
            @pl.when(jnp.logical_and(pl.program_id(1) == 0, kk == 0))
            def _():
                cs_ref[...] = _colsum(bt.astype(F32))

            @pl.when(jnp.logical_and(pl.program_id(1) == 0, kk > 0))
            def _():
                cs_ref[...] += _colsum(bt.astype(F32))

        def finish(r):
            if bias_ref is not None:
                r = r + bias_ref[...]
            if act == 'relu2':
                r = jnp.square(jnp.maximum(r, 0.0))
            if hid_ref is not None:
                r = r * (2.0 * jnp.sqrt(hid_ref[...].astype(F32)))
            if add_ref is not None:
                r = r + add_scale * add_ref[...]
            o_ref[...] = r.astype(o_ref.dtype)

        if gk == 1:
            finish(p)
        else:
            @pl.when(kk == 0)
            def _():
                acc_ref[...] = p

            @pl.when(kk > 0)
            def _():
                acc_ref[...] += p

            @pl.when(kk == gk - 1)
            def _():
                finish(acc_ref[...])

    scratch = [pltpu.VMEM((tm, tn), F32)] if gk > 1 else []
    sem = ("arbitrary",) * 3 if colsum else ("parallel", "parallel", "arbitrary")
    return pl.pallas_call(body, name=name, grid=grid, in_specs=in_specs, out_specs=out_spec,
                          out_shape=out_shape, scratch_shapes=scratch, compiler_params=_params(sem))(*args)


def ln_fwd(name, h, mix, g, b):
    def fn(i, tm, rows, bcs):
        pre = ALPHA * rows[0] + rows[1]
        mu = jnp.mean(pre, axis=1, keepdims=True)
        xc = pre - mu
        var = jnp.mean(xc * xc, axis=1, keepdims=True)
        rstd = lax.rsqrt(var + LN_EPS)
        xhat = xc * rstd
        return (xhat * bcs[0] + bcs[1], xhat, rstd), ()
    return rowwise(name, fn, [h, mix], [g, b], [D_MODEL, D_MODEL, 1])


def ln_bwd(name, dy, xhat, rstd, g):
    def fn(i, tm, rows, bcs):
        dy_, xh, rs = rows
        dyg = dy_ * bcs[0]
        m1 = jnp.mean(dyg, axis=1, keepdims=True)
        m2 = jnp.mean(dyg * xh, axis=1, keepdims=True)
        dpre = rs * (dyg - m1 - xh * m2)
        return (dpre,), (_colsum(dy_ * xh), _colsum(dy_), _colsum(dpre))
    return rowwise(name, fn, [dy, xhat, rstd], [g], [D_MODEL], [D_MODEL] * 3)


_GELU_C = math.sqrt(2.0 / math.pi)


def gelu_fwd(name, x):
    def fn(i, tm, rows, bcs):
        v = rows[0]
        u = _GELU_C * (v + 0.044715 * (v * v * v))
        return (0.5 * v * (1.0 + jnp.tanh(u)),), ()
    return rowwise(name, fn, [x], [], [x.shape[1]])[0]


def gelu_bwd(name, dy, x):
    def fn(i, tm, rows, bcs):
        v = rows[1]
        th = jnp.tanh(_GELU_C * (v + 0.044715 * (v * v * v)))
        dg = 0.5 * (1.0 + th) + 0.5 * v * (1.0 - th * th) * (_GELU_C * (1.0 + 3.0 * 0.044715 * v * v))
        return (rows[0] * dg,), ()
    return rowwise(name, fn, [dy, x], [], [x.shape[1]])[0]


def glu_gate(name, y, gp):
    def fn(i, tm, rows, bcs):
        return (rows[0] * jax.nn.sigmoid(rows[1]),), ()
    return rowwise(name, fn, [y, gp], [], [y.shape[1]], out_dtypes=[BF16])[0]


def glu_gate_bwd(name, da, y, gp):
    def fn(i, tm, rows, bcs):
        s = jax.nn.sigmoid(rows[2])
        dgp = rows[0] * rows[1] * s * (1.0 - s)
        return (dgp, rows[0] * s), (_colsum(dgp),)
    w = y.shape[1]
    return rowwise(name, fn, [da, y, gp], [], [w, w], [w])


def ln_loss(name, h, mix, g, b, tgt, lp, seq):
    def fn(i, tm, rows, bcs):
        pre = ALPHA * rows[0] + rows[1]
        mu = jnp.mean(pre, axis=1, keepdims=True)
        xc = pre - mu
        var = jnp.mean(xc * xc, axis=1, keepdims=True)
        rstd = lax.rsqrt(var + LN_EPS)
        xhat = xc * rstd
        pos = (i * tm + lax.broadcasted_iota(jnp.int32, (tm, 1), 0)) % lp
        valid = jnp.logical_and(pos >= N_META, pos < N_META + seq)
        err = jnp.where(valid, (xhat * bcs[0] + bcs[1]) - rows[2], 0.0)
        part = 0.5 * jnp.sum(jnp.mean(err * err, axis=1, keepdims=True), axis=0, keepdims=True)
        dy = err * (1.0 / D_MODEL)
        dyg = dy * bcs[0]
        m1 = jnp.mean(dyg, axis=1, keepdims=True)
        m2 = jnp.mean(dyg * xhat, axis=1, keepdims=True)
        dpre = rstd * (dyg - m1 - xhat * m2)
        return (dpre,), (_colsum(dy * xhat), _colsum(dy), _colsum(dpre), jnp.broadcast_to(part, (1, LANES)))
    return rowwise(name, fn, [h, mix, tgt], [g, b], [D_MODEL], [D_MODEL] * 3 + [LANES])


def adamw(name, w, gs, m, v):
    ng = len(gs)

    def fn(i, tm, rows, bcs):
        w_ = rows[0]
        g = rows[1] if ng == 1 else rows[1] + rows[2]
        m_, v_ = rows[1 + ng], rows[2 + ng]
        m_ = ADAM_B1 * m_ + (1.0 - ADAM_B1) * g
        v_ = ADAM_B2 * v_ + (1.0 - ADAM_B2) * jnp.square(g)
        m_hat = m_ / (1.0 - ADAM_B1 ** ADAM_STEP)
        v_hat = v_ / (1.0 - ADAM_B2 ** ADAM_STEP)
        delta = -ADAM_LR * (m_hat / (jnp.sqrt(v_hat) + ADAM_EPS) + ADAM_WD * w_)
        return (g, delta, m_, v_), ()
    wd = w.shape[1]
    return rowwise(name, fn, [w] + list(gs) + [m, v], [], [wd] * 4)


def s5_matrices(lam_re, lam_im, log_dt, b_re, b_im, c_re, c_im, d):
    c, hh, gg, pp = S5_CHUNK, S5_GROUP, S5_GROUPS, S5_STATE
    dt = jnp.exp(log_dt)[:, None]
    tau = jnp.arange(c + 1, dtype=F32)[None, :, None]
    mag = jnp.exp(tau * (lam_re * dt)[:, None, :])
    ang = tau * (lam_im * dt)[:, None, :]
    pw_re, pw_im = mag * jnp.cos(ang), mag * jnp.sin(ang)
    nr, ni = pw_re[:, 1] - 1.0, pw_im[:, 1]
    den = lam_re * lam_re + lam_im * lam_im
    cf_re = (nr * lam_re + ni * lam_im) / den
    cf_im = (ni * lam_re - nr * lam_im) / den
    bb_re = cf_re[:, :, None] * b_re - cf_im[:, :, None] * b_im
    bb_im = cf_re[:, :, None] * b_im + cf_im[:, :, None] * b_re
    ct_re, ct_im = c_re.transpose(0, 2, 1)[:, :, None, :], c_im.transpose(0, 2, 1)[:, :, None, :]
    pt_re, pt_im = pw_re.transpose(0, 2, 1)[:, :, :, None], pw_im.transpose(0, 2, 1)[:, :, :, None]
    cp_re = ct_re * pt_re - ct_im * pt_im
    cp_im = ct_re * pt_im + ct_im * pt_re
    hp = lax.Precision.HIGHEST
    kmat = (jnp.einsum('gpk,gpn->gkn', bb_re, cp_re[:, :, :c].reshape(gg, pp, c * hh), precision=hp)
            - jnp.einsum('gpk,gpn->gkn', bb_im, cp_im[:, :, :c].reshape(gg, pp, c * hh), precision=hp))
    rows = [jnp.pad(kmat[:, :, :(c - j) * hh], ((0, 0), (0, 0), (j * hh, 0))) for j in range(c)]
    a1 = jnp.stack(rows, axis=1).reshape(gg, c * hh, c * hh)
    a1 = a1 + jnp.eye(c * hh, dtype=F32)[None] * jnp.tile(d, (1, c))[:, None, :]
    a2 = jnp.concatenate([cp_re[:, :, 1:].reshape(gg, pp, c * hh), -cp_im[:, :, 1:].reshape(gg, pp, c * hh)], axis=1)
    rv_re, rv_im = pw_re[:, :c][:, ::-1, None, :], pw_im[:, :c][:, ::-1, None, :]
    bt_re, bt_im = bb_re.transpose(0, 2, 1)[:, None], bb_im.transpose(0, 2, 1)[:, None]
    a3 = jnp.concatenate([rv_re * bt_re - rv_im * bt_im, rv_re * bt_im + rv_im * bt_re], axis=-1)
    a3 = a3.reshape(gg, c * hh, 2 * pp)
    are = jnp.concatenate([pw_re[:, c], pw_re[:, c]], axis=-1)[:, None, :]
    aim = jnp.concatenate([-pw_im[:, c], pw_im[:, c]], axis=-1)[:, None, :]
    return a1, a2, a3, are, aim


S5_LEVELS = 8


def s5_scan_powers(lam_re, lam_im, log_dt):
    dt = jnp.exp(log_dt)[:, None]
    e = (S5_CHUNK * 2.0 ** jnp.arange(S5_LEVELS, dtype=F32))[:, None, None]
    mag = jnp.exp(e * (lam_re * dt)[None])
    ang = e * (lam_im * dt)[None]
    pr, pi = mag * jnp.cos(ang), mag * jnp.sin(ang)
    pre = jnp.concatenate([pr, pr], axis=-1).transpose(1, 0, 2)
    pim = jnp.concatenate([-pi, pi], axis=-1).transpose(1, 0, 2)
    return pre, pim


def _s5_scan(x, pre, pim, reverse):
    n = x.shape[0]
    rowi = lax.broadcasted_iota(jnp.int32, (n, 1), 0)
    t = x
    for k in range(S5_LEVELS):
        shift = 2 ** k
        if shift >= n:
            break
        p_re, p_im = pre[k:k + 1, :], pim[k:k + 1, :]
        if reverse:
            far = jnp.where(rowi < n - shift, pltpu.roll(t, n - shift, 0), 0.0)
            t = t + (p_re * far + pltpu.roll(p_im * far, S5_STATE, 1))
        else:
            far = jnp.where(rowi >= shift, pltpu.roll(t, shift, 0), 0.0)
            t = t + (p_re * far + p_im * pltpu.roll(far, S5_STATE, 1))
    return t


def s5_fwd(ug, a1, a2, a3, pre, pim, nseq):
    g, nc, cw = ug.shape
    per = nc // nseq
    sw = 2 * S5_STATE
    assert per <= 2 ** S5_LEVELS

    def body(u_ref, a1_ref, a2_ref, a3_ref, pre_ref, pim_ref, y_ref, s_ref):
        u = u_ref[...]
        x = _dot3(u, a3_ref[...], 1, 0)
        first = lax.broadcasted_iota(jnp.int32, (per, 1), 0) == 0
        for b in range(nseq):
            t = _s5_scan(x[b * per:(b + 1) * per], pre_ref[...], pim_ref[...], False)
            s_ref[pl.ds(b * per, per), :] = jnp.where(first, 0.0, pltpu.roll(t, 1, 0))
        y_ref[...] = _dot3(u, a1_ref[...], 1, 0) + _dot3(s_ref[...], a2_ref[...], 1, 0)

    def spec(shape):
        return pl.BlockSpec((None,) + shape, lambda i: (i, 0, 0))
    lv = (S5_LEVELS, sw)
    return pl.pallas_call(
        body, name="s5_fwd", grid=(g,),
        in_specs=[spec((nc, cw)), spec((cw, cw)), spec((sw, cw)), spec((cw, sw)), spec(lv), spec(lv)],
        out_specs=[spec((nc, cw)), spec((nc, sw))],
        out_shape=[jax.ShapeDtypeStruct((g, nc, cw), F32), jax.ShapeDtypeStruct((g, nc, sw), F32)],
        compiler_params=_params(("parallel",)))(ug, a1, a2, a3, pre, pim)


def s5_bwd(dyg, ug, sst, a1, a2, a3, pre, pim, nseq, ex=None):
    g, nc, cw = ug.shape
    per = nc // nseq
    sw = 2 * S5_STATE
    split, ex_start, ex_finish, exkw = _hidden_exchange(ex, 8, 6, (g,))

    def body(*refs):
        ((dy_ref, u_ref, s_ref, a1_ref, a2_ref, a3_ref, pre_ref, pim_ref),
         (du_ref, da1_ref, da2_ref, da3_ref, dare_ref, daim_ref), ex_parts, (dx_ref,)) = split(refs)
        ex_start(ex_parts)
        dy = dy_ref[...]
        u = u_ref[...]
        gd = _dot3(dy, a2_ref[...], 1, 1)
        s_all = s_ref[...]
        da2_ref[...] = _dot3(s_all, dy, 0, 0)
        last = lax.broadcasted_iota(jnp.int32, (per, 1), 0) == per - 1
        for b in range(nseq):
            gs = _s5_scan(gd[b * per:(b + 1) * per], pre_ref[...], pim_ref[...], True)
            dx_ref[pl.ds(b * per, per), :] = jnp.where(last, 0.0, pltpu.roll(gs, per - 1, 0))
        dx = dx_ref[...]
        dare_ref[...] = _colsum(dx * s_all)
        daim_ref[...] = _colsum(dx * pltpu.roll(s_all, S5_STATE, 1))
        du_ref[...] = _dot3(dy, a1_ref[...], 1, 1) + _dot3(dx, a3_ref[...], 1, 1)
        da1_ref[...] = _dot3(u, dy, 0, 0)
        da3_ref[...] = _dot3(u, dx, 0, 0)
        ex_finish(ex_parts)

    def spec(shape):
        return pl.BlockSpec((None,) + shape, lambda i: (i, 0, 0))
    sds = jax.ShapeDtypeStruct
    return pl.pallas_call(
        body, name="s5_bwd", grid=(g,),
        in_specs=[spec((nc, cw)), spec((nc, cw)), spec((nc, sw)), spec((cw, cw)), spec((sw, cw)), spec((cw, sw)),
                  spec((S5_LEVELS, sw)), spec((S5_LEVELS, sw))] + exkw['in_specs'],
        out_specs=[spec((nc, cw)), spec((cw, cw)), spec((sw, cw)), spec((cw, sw)), spec((1, sw)), spec((1, sw))]
        + exkw['out_specs'],
        out_shape=[sds((g, nc, cw), F32), sds((g, cw, cw), F32), sds((g, sw, cw), F32), sds((g, cw, sw), F32),
                   sds((g, 1, sw), F32), sds((g, 1, sw), F32)] + exkw['out_shape'],
        input_output_aliases=exkw['aliases'], scratch_shapes=[pltpu.VMEM((nc, sw), F32)] + exkw['scratch'],
        compiler_params=_params(("arbitrary",)))(dyg, ug, sst, a1, a2, a3, pre, pim, *exkw['ins'])


def _to_groups(u):
    t = u.shape[0]
    nc = t // S5_CHUNK
    return u.reshape(nc, S5_CHUNK, S5_GROUPS, S5_GROUP).transpose(2, 0, 1, 3).reshape(S5_GROUPS, nc, -1)


def _from_groups(yg):
    g, nc, _ = yg.shape
    return yg.reshape(g, nc, S5_CHUNK, S5_GROUP).transpose(1, 2, 0, 3).reshape(nc * S5_CHUNK, g * S5_GROUP)


def _sb_masks():
    row = lax.broadcasted_iota(jnp.int32, (SB_BLOCK, SB_BLOCK), 0)
    col = lax.broadcasted_iota(jnp.int32, (SB_BLOCK, SB_BLOCK), 1)
    return row, col


def _sb_weights(z, vis, later_of, after):
    sp = jnp.maximum(z, 0.0) + jnp.log1p(jnp.exp(-jnp.abs(z)))
    lk = jnp.where(vis, -sp, 0.0)
    rs = jnp.sum(lk, axis=1, keepdims=True)
    later = _dot_exact_rhs(lk, after, 1, 0) + later_of(rs)
    lb = z - sp
    w = jnp.where(vis, jnp.exp(lb + later), 0.0)
    return w, rs, lb


def _hidden_exchange(ex, n_in, n_out, grid):
    ni, no = (len(ex.ins), len(ex.out_shapes)) if ex else (0, 0)

    def split(refs):
        a, b = n_in + ni, n_in + ni + n_out
        end = len(refs) - (2 if ex else 0)
        return refs[:n_in], refs[a:b], (refs[n_in:a], refs[b:b + no], refs[end:]), refs[b + no:end]

    def at_step(which, fn, parts):
        if ex is not None:
            ids = [pl.program_id(d) == (0 if which == 'first' else g - 1) for d, g in enumerate(grid)]
            cond = ids[0]
            for c in ids[1:]:
                cond = jnp.logical_and(cond, c)
            pl.when(cond)(lambda: fn(parts[0], parts[1], *parts[2]))

    anyspec = pl.BlockSpec(memory_space=pl.ANY)
    kw = dict(in_specs=[anyspec] * ni, out_specs=[anyspec] * no, out_shape=list(ex.out_shapes) if ex else [],
              aliases={n_in + i: n_out + j for i, j in (ex.aliases().items() if ex else ())},
              scratch=ex.sems() if ex else [], ins=list(ex.ins) if ex else [])
    start = functools.partial(at_step, 'first', ex.start if ex else None)
    finish = functools.partial(at_step, 'last', ex.finish if ex else None)
    return split, start, finish, kw


SB_TILES = 2
SB_HEADS = 2 * SB_TILES
SB_WD = SB_TILES * LANES


def _stack_heads(x):
    return jnp.concatenate([x] * SB_HEADS, axis=0)


def _stack_masks():
    row = lax.broadcasted_iota(jnp.int32, (SB_HEADS * SB_BLOCK, SB_WD), 0)
    col = lax.broadcasted_iota(jnp.int32, (SB_HEADS * SB_BLOCK, SB_WD), 1)
    rowk = lax.broadcasted_iota(jnp.int32, (SB_HEADS * SB_BLOCK, SB_BLOCK), 0)
    colk = lax.broadcasted_iota(jnp.int32, (SB_HEADS * SB_BLOCK, SB_BLOCK), 1)
    return (col // 64) == (row // SB_BLOCK), colk < (rowk % SB_BLOCK)


def _own_lanes(r):
    head = lax.broadcasted_iota(jnp.int32, (SB_BLOCK, SB_WD), 1) // 64
    out = r[:SB_BLOCK]
    for h in range(1, SB_HEADS):
        out = jnp.where(head == h, r[h * SB_BLOCK:(h + 1) * SB_BLOCK], out)
    return out


def _sb_specs(lp, nseq):
    wd = SB_TILES * LANES
    off = [(S5_WIDTH + i * SB_WIDTH) // wd for i in range(3)]
    blk = (lp, wd)
    qkv = [pl.BlockSpec(blk, functools.partial(lambda b, h, o: (b, o + h), o=o)) for o in off]
    return (blk, qkv, pl.BlockSpec(blk, lambda b, h: (b, h)),
            pl.BlockSpec((None, None, 8, LANES), lambda b, h: (b, h, 0, 0)))


def attn_fwd(proj, nseq, lp, ex=None):
    nqb = lp // SB_BLOCK
    t = proj.shape[0]
    nstep = SB_WIDTH // (SB_TILES * LANES)
    split, ex_start, ex_finish, exkw = _hidden_exchange(ex, 3, 3, (nseq, nstep))

    def body(*refs):
        (q_ref, k_ref, v_ref), (o_ref, tot_ref, first_ref), ex_parts, _ = split(refs)
        ex_start(ex_parts)
        row, col = _sb_masks()
        after = (row > col).astype(BF16)
        lane8 = lax.broadcasted_iota(jnp.int32, (8, LANES), 1)
        hms, tris = _stack_masks()

        def qloop(qi, firsts):
            q0 = pl.multiple_of(qi * SB_BLOCK, SB_BLOCK)
            qm = jnp.where(hms, _stack_heads(q_ref[pl.ds(q0, SB_BLOCK), :] * 0.125), 0.0).astype(BF16)

            def alive(carry):
                return jnp.logical_and(carry[0] <= qi, carry[1] > 0)

            def kstep(carry):
                s, _, acc, lat = carry
                kj = qi - s
                k0 = pl.multiple_of(kj * SB_BLOCK, SB_BLOCK)
                k = k_ref[pl.ds(k0, SB_BLOCK), :].astype(BF16)
                v = v_ref[pl.ds(k0, SB_BLOCK), :].astype(BF16)
                vis = jnp.logical_or(kj < qi, tris)
                w, rs, _ = _sb_weights(_dot(qm, k, 1, 1), vis, functools.partial(lambda rs, lat: lat, lat=lat), after)
                acc = acc + _own_lanes(_dot(w.astype(BF16), v, 1, 0))
                lat = lat + rs
                return s + 1, (jnp.max(lat) > SB_DEAD).astype(jnp.int32), acc, lat

            res = lax.while_loop(alive, kstep, (jnp.int32(0), jnp.int32(1), jnp.zeros((SB_BLOCK, SB_WD), F32),
                                                jnp.zeros((SB_HEADS * SB_BLOCK, 1), F32)))
            o_ref[pl.ds(q0, SB_BLOCK), :] = res[2].astype(o_ref.dtype)
            tot_ref[pl.ds(q0, SB_BLOCK), :] = _own_lanes(jnp.broadcast_to(res[3], (SB_HEADS * SB_BLOCK, SB_WD)))
            return jnp.where(lane8 == qi, (qi - res[0] + 1).astype(F32), firsts)

        first_ref[...] = lax.fori_loop(0, nqb, qloop, jnp.zeros((8, LANES), F32))
        ex_finish(ex_parts)

    blk, qkv, out, vec = _sb_specs(lp, nseq)
    return pl.pallas_call(
        body, name="attn_fwd", grid=(nseq, nstep),
        in_specs=qkv + exkw['in_specs'], out_specs=[out, out, vec] + exkw['out_specs'],
        out_shape=[jax.ShapeDtypeStruct((t, SB_WIDTH), BF16), jax.ShapeDtypeStruct((t, SB_WIDTH), F32)]
        + [jax.ShapeDtypeStruct((nseq, nstep, 8, LANES), F32)] + exkw['out_shape'],
        input_output_aliases=exkw['aliases'], scratch_shapes=exkw['scratch'],
        compiler_params=_params(("arbitrary", "arbitrary")))(proj, proj, proj, *exkw['ins'])


def attn_bwd(proj, tot, first, do, nseq, lp, ex=None):
    nqb = lp // SB_BLOCK
    t = proj.shape[0]
    nstep = SB_WIDTH // (SB_TILES * LANES)
    split, ex_start, ex_finish, exkw = _hidden_exchange(ex, 6, 3, (nseq, nstep))

    def body(*refs):
        (q_ref, k_ref, v_ref, tot_ref, first_ref, do_ref), (dq_ref, dk_ref, dv_ref), ex_parts, _ = split(refs)
        ex_start(ex_parts)
        row, col = _sb_masks()
        after = (row > col).astype(BF16)
        before = (row < col).astype(BF16)
        lane8 = lax.broadcasted_iota(jnp.int32, (8, LANES), 1)
        firsts = first_ref[...]
        dk_ref[...] = jnp.zeros(dk_ref.shape, F32)
        dv_ref[...] = jnp.zeros(dv_ref.shape, F32)
        hms, tris = _stack_masks()

        def qloop(qi, _):
            first = jnp.max(jnp.where(lane8 == qi, firsts, 0.0)).astype(jnp.int32)
            first = jnp.clip(first, 0, qi)
            q0 = pl.multiple_of(qi * SB_BLOCK, SB_BLOCK)
            qm = jnp.where(hms, _stack_heads(q_ref[pl.ds(q0, SB_BLOCK), :] * 0.125), 0.0).astype(BF16)
            dom = jnp.where(hms, _stack_heads(do_ref[pl.ds(q0, SB_BLOCK), :]), 0.0).astype(BF16)
            tot = jnp.max(jnp.where(hms, _stack_heads(tot_ref[pl.ds(q0, SB_BLOCK), :]), -jnp.inf),
                          axis=1, keepdims=True)

            def kloop(kj, carry):
                dq, pre, gpre = carry
                k0 = pl.multiple_of(kj * SB_BLOCK, SB_BLOCK)
                k = k_ref[pl.ds(k0, SB_BLOCK), :].astype(BF16)
                v = v_ref[pl.ds(k0, SB_BLOCK), :].astype(BF16)
                vis = jnp.logical_or(kj < qi, tris)
                later_of = functools.partial(lambda rs, t_, p_: t_ - p_ - rs, t_=tot, p_=pre)
                w, rs, lb = _sb_weights(_dot(qm, k, 1, 1), vis, later_of, after)
                g = _dot(dom, v, 1, 1) * w
                dlk = _dot_exact_rhs(g, before, 1, 0) + gpre
                sig = jnp.exp(lb)
                dz = jnp.where(vis, g * (1.0 - sig) - dlk * sig, 0.0).astype(BF16)
                dk_ref[pl.ds(k0, SB_BLOCK), :] += _dot(dz, qm, 0, 0)
                dv_ref[pl.ds(k0, SB_BLOCK), :] += _dot(w.astype(BF16), dom, 0, 0)
                return dq + _own_lanes(_dot(dz, k, 1, 0)), pre + rs, gpre + jnp.sum(g, axis=1, keepdims=True)

            z1 = jnp.zeros((SB_HEADS * SB_BLOCK, 1), F32)
            res = lax.fori_loop(first, qi + 1, kloop, (jnp.zeros((SB_BLOCK, SB_WD), F32), z1, z1))
            dq_ref[pl.ds(q0, SB_BLOCK), :] = res[0] * 0.125
            return 0

        lax.fori_loop(0, nqb, qloop, 0)
        ex_finish(ex_parts)

    blk, qkv, out, vec = _sb_specs(lp, nseq)
    return pl.pallas_call(
        body, name="attn_bwd", grid=(nseq, nstep),
        in_specs=qkv + [out, vec, out] + exkw['in_specs'], out_specs=[out] * 3 + exkw['out_specs'],
        out_shape=[jax.ShapeDtypeStruct((t, SB_WIDTH), F32)] * 3 + exkw['out_shape'],
        input_output_aliases=exkw['aliases'], scratch_shapes=exkw['scratch'],
        compiler_params=_params(("arbitrary", "arbitrary")))(proj, proj, proj, tot, first, do, *exkw['ins'])


def _hg_loop(nch, step, init):
    assert nch % HG_UNROLL == 0

    def trip(i, carry):
        for u in range(HG_UNROLL):
            carry = step(i * HG_UNROLL + u, carry)
        return carry
    return lax.fori_loop(0, nch // HG_UNROLL, trip, init)


def _hg_gates(fc, lb):
    sg = jax.nn.sigmoid(fc)
    f = lb + (1.0 - lb) * sg
    return sg, f


def _dot_exact_lhs(m, x):
    hi, mid, lo = _split3(x)
    return _dot(m, hi, 1, 0) + (_dot(m, mid, 1, 0) + _dot(m, lo, 1, 0))


def hgrn_fwd(proj, lb, ng, nseq, lp, ex=None):
    nch = lp // HG_CHUNK
    t = proj.shape[0]
    c = HG_CHUNK

    split, ex_start, ex_finish, exkw = _hidden_exchange(ex, 6, 3, (HG_HEADS,))

    def body(*refs):
        (q_ref, f_ref, v_ref, g_ref, lb_ref, ng_ref), (o_ref, og_ref, st_ref), ex_parts, _ = split(refs)
        ex_start(ex_parts)
        lbv, ngv = lb_ref[...], ng_ref[...]

        rr = nseq * c
        seq_r = lax.broadcasted_iota(jnp.int32, (rr, 1), 0) // c
        bd_row = lax.broadcasted_iota(jnp.int32, (rr, rr), 0)
        bd_col = lax.broadcasted_iota(jnp.int32, (rr, rr), 1)
        causal = jnp.logical_and(bd_row // c == bd_col // c, bd_col <= bd_row)
        incl = causal.astype(BF16)

        def stacked(ref, ci):
            return jnp.concatenate([ref[pl.ds(pl.multiple_of(b * lp + ci * c, c), c), :] for b in range(nseq)], axis=0)

        def wide(x):
            return jnp.concatenate([jnp.where(seq_r == b, x, jnp.zeros_like(x)) for b in range(nseq)], axis=1)

        def step(ci, st):
            for b in range(nseq):
                st_ref[b, ci] = st[:, b * HG_DK:(b + 1) * HG_DK]
            _, f = _hg_gates(stacked(f_ref, ci), lbv)
            bcum = _dot_exact_lhs(incl, jnp.log(f))
            lasts = [bcum[b * c + c - 1:b * c + c, :] for b in range(nseq)]
            blast = jnp.concatenate([jnp.broadcast_to(x, (c, LANES)) for x in lasts], axis=0)
            kk = 1.0 - f
            vc = stacked(v_ref, ci).astype(BF16)
            qd = (stacked(q_ref, ci) * jnp.exp(bcum)).astype(BF16)
            kd = (kk * jnp.exp(-bcum)).astype(BF16)
            a = jnp.where(causal, _dot(qd, kd, 1, 1), 0.0)
            o = _dot(a.astype(BF16), vc, 1, 0) + _dot(wide(qd), st.astype(BF16), 1, 1)
            kend = (kk * jnp.exp(blast - bcum)).astype(BF16)
            st = st * jnp.exp(jnp.concatenate(lasts, axis=1)) + _dot(vc, wide(kend), 0, 0)
            r = lax.rsqrt(jnp.mean(o * o, axis=1, keepdims=True) + RMS_EPS)
            gc = stacked(g_ref, ci)
            og = (o * r * ngv * (gc * jax.nn.sigmoid(gc))).astype(og_ref.dtype)
            for b in range(nseq):
                rows = pl.ds(pl.multiple_of(b * lp + ci * c, c), c)
                o_ref[rows, :] = o[b * c:(b + 1) * c]
                og_ref[rows, :] = og[b * c:(b + 1) * c]
            return st

        _hg_loop(nch, step, jnp.zeros((HG_DK, nseq * HG_DK), F32))
        ex_finish(ex_parts)

    blk = (nseq * lp, LANES)
    h = HG_HEADS
    return pl.pallas_call(
        body, name="hgrn_fwd", grid=(h,),
        in_specs=[pl.BlockSpec(blk, lambda hh: (0, hh)),
                  pl.BlockSpec(blk, lambda hh: (0, h + hh)),
                  pl.BlockSpec(blk, lambda hh: (0, 2 * h + hh)),
                  pl.BlockSpec(blk, lambda hh: (0, 3 * h + hh)),
                  pl.BlockSpec((1, LANES), lambda hh: (0, hh)),
                  pl.BlockSpec((1, LANES), lambda hh: (0, hh))] + exkw['in_specs'],
        out_specs=[pl.BlockSpec(blk, lambda hh: (0, hh)),
                   pl.BlockSpec(blk, lambda hh: (0, hh)),
                   pl.BlockSpec((nseq, None, nch, HG_DK, HG_DK), lambda hh: (0, hh, 0, 0, 0))] + exkw['out_specs'],
        out_shape=[jax.ShapeDtypeStruct((t, D_MODEL), F32), jax.ShapeDtypeStruct((t, D_MODEL), BF16),
                   jax.ShapeDtypeStruct((nseq, h, nch, HG_DK, HG_DK), F32)] + exkw['out_shape'],
        input_output_aliases=exkw['aliases'], scratch_shapes=exkw['scratch'],
        compiler_params=_params(("arbitrary",)))(proj, proj, proj, proj, lb, ng, *exkw['ins'])


def hgrn_bwd(proj, o, dog, states, lb, ng, nseq, lp):
    nch = lp // HG_CHUNK
    t = proj.shape[0]
    c = HG_CHUNK
    rr = nseq * c

    def body(q_ref, f_ref, v_ref, g_ref, o_ref, dog_ref, st_ref, lb_ref, ng_ref,
             dq_ref, df_ref, dv_ref, dg_ref, dlb_ref, dng_ref):
        seq_r = lax.broadcasted_iota(jnp.int32, (rr, 1), 0) // c
        last = lax.broadcasted_iota(jnp.int32, (rr, 1), 0) % c == c - 1
        bd_row = lax.broadcasted_iota(jnp.int32, (rr, rr), 0)
        bd_col = lax.broadcasted_iota(jnp.int32, (rr, rr), 1)
        same = bd_row // c == bd_col // c
        causal = jnp.logical_and(same, bd_col <= bd_row)
        incl = causal.astype(BF16)
        from_here = jnp.logical_and(same, bd_col >= bd_row).astype(BF16)
        lbv, ngv = lb_ref[...], ng_ref[...]

        def stacked(ref, ci):
            return jnp.concatenate([ref[pl.ds(pl.multiple_of(b * lp + ci * c, c), c), :] for b in range(nseq)], axis=0)

        def wide(x):
            return jnp.concatenate([jnp.where(seq_r == b, x, jnp.zeros_like(x)) for b in range(nseq)], axis=1)

        def own(x):
            return functools.reduce(jnp.add, [jnp.where(seq_r == b, x[:, b * LANES:(b + 1) * LANES], 0.0)
                                              for b in range(nseq)])

        def per_seq_rows(vals):
            return jnp.concatenate([jnp.broadcast_to(x, (c, LANES)) for x in vals], axis=0)

        def step(s, carry):
            dst, dlb, dng = carry
            ci = nch - 1 - s
            oc, gc, dogc = stacked(o_ref, ci), stacked(g_ref, ci), stacked(dog_ref, ci)
            r = lax.rsqrt(jnp.mean(oc * oc, axis=1, keepdims=True) + RMS_EPS)
            sgg = jax.nn.sigmoid(gc)
            dgc = dogc * (oc * r * ngv) * (sgg * (1.0 + gc * (1.0 - sgg)))
            don = dogc * (gc * sgg)
            dng = dng + _colsum(don * oc * r)
            tt = don * ngv
            do = r * (tt - oc * (r * r) * jnp.mean(tt * oc, axis=1, keepdims=True))
            st = jnp.concatenate([st_ref[b, ci] for b in range(nseq)], axis=1)
            sg, f = _hg_gates(stacked(f_ref, ci), lbv)
            bcum = _dot_exact_lhs(incl, jnp.log(f))
            lasts = [bcum[b * c + c - 1:b * c + c, :] for b in range(nseq)]
            blast = per_seq_rows(lasts)
            kk = 1.0 - f
            qc, vf = stacked(q_ref, ci), stacked(v_ref, ci)
            pdec = jnp.exp(bcum)
            ndec = jnp.exp(-bcum)
            edec = jnp.exp(blast - bcum)
            eb = jnp.exp(jnp.concatenate(lasts, axis=1))
            qd_f, kd_f, kend_f = qc * pdec, kk * ndec, kk * edec
            qd, kd, kend = qd_f.astype(BF16), kd_f.astype(BF16), kend_f.astype(BF16)
            vc, dob, stb, dstb = vf.astype(BF16), do.astype(BF16), st.astype(BF16), dst.astype(BF16)
            a = jnp.where(causal, _dot(qd, kd, 1, 1), 0.0).astype(BF16)
            da = jnp.where(causal, _dot(dob, vc, 1, 1), 0.0).astype(BF16)
            dv = _dot(a, dob, 0, 0) + _dot(wide(kend), dstb, 1, 1)
            dqd = _dot(da, kd, 1, 0) + own(_dot(dob, stb, 1, 0))
            dkd = _dot(da, qd, 0, 0)
            dkend = own(_dot(vc, dstb, 1, 0))
            sts = _colsum(dst * st)
            dkk = dkend * kend_f
            dblast = per_seq_rows([_colsum(jnp.where(seq_r == b, dkk, 0.0))
                                   + eb[:, b * LANES:(b + 1) * LANES] * sts[:, b * LANES:(b + 1) * LANES]
                                   for b in range(nseq)])
            dbcum = dqd * qd_f - dkd * kd_f - dkk
            dbcum = dbcum + jnp.where(last, dblast, 0.0)
            dlf = _dot_exact_lhs(from_here, dbcum)
            df = dlf / f - (dkd * ndec + dkend * edec)
            dfp = df * (1.0 - lbv) * sg * (1.0 - sg)
            dqc = dqd * pdec
            for b in range(nseq):
                rows = pl.ds(pl.multiple_of(b * lp + ci * c, c), c)
                blk_rows = slice(b * c, (b + 1) * c)
                dg_ref[rows, :], dv_ref[rows, :] = dgc[blk_rows], dv[blk_rows]
                dq_ref[rows, :], df_ref[rows, :] = dqc[blk_rows], dfp[blk_rows]
            dlb = dlb + _colsum(df * (1.0 - sg))
            dst = dst * eb + _dot(dob, wide(qd), 0, 0)
            return dst, dlb, dng

        z = jnp.zeros((1, LANES), F32)
        _, dlb, dng = _hg_loop(nch, step, (jnp.zeros((HG_DK, nseq * HG_DK), F32), z, z))
        dlb_ref[...] = dlb
        dng_ref[...] = dng

    blk = (nseq * lp, LANES)
    h = HG_HEADS
    vec = pl.BlockSpec((1, LANES), lambda hh: (0, hh))
    col = pl.BlockSpec(blk, lambda hh: (0, hh))
    return pl.pallas_call(
        body, name="hgrn_bwd", grid=(h,),
        in_specs=[col,
                  pl.BlockSpec(blk, lambda hh: (0, h + hh)),
                  pl.BlockSpec(blk, lambda hh: (0, 2 * h + hh)),
                  pl.BlockSpec(blk, lambda hh: (0, 3 * h + hh)),
                  col, col,
                  pl.BlockSpec((nseq, None, nch, HG_DK, HG_DK), lambda hh: (0, hh, 0, 0, 0)),
                  vec, vec],
        out_specs=[col] * 4 + [vec, vec],
        out_shape=[jax.ShapeDtypeStruct((t, D_MODEL), F32)] * 4 + [jax.ShapeDtypeStruct((1, D_MODEL), F32)] * 2,
        compiler_params=pltpu.CompilerParams(dimension_semantics=("parallel",), vmem_limit_bytes=HGRN_BWD_VMEM))(
            proj, proj, proj, proj, o, dog, states, lb, ng)


def _lower_bound(gamma):
    p = jax.nn.softmax(gamma, axis=0)
    return (jnp.cumsum(p, axis=0) - p[0])[1][None, :]


def local_step(x, tgt, w, hooks=None):
    nseq, seq, _ = x.shape
    lp = -(-(N_META + seq) // SB_BLOCK) * SB_BLOCK
    t = nseq * lp
    pad = lp - N_META - seq
    h0 = jnp.concatenate([jnp.broadcast_to(w['meta'][None], (nseq, N_META, D_MODEL)), x,
                          jnp.zeros((nseq, pad, D_MODEL), F32)], axis=1).reshape(t, D_MODEL)
    tgt_p = jnp.pad(tgt, ((0, 0), (N_META, pad), (0, 0))).reshape(t, D_MODEL)
    row = lambda v: v.reshape(1, -1)
    g = {}

    proj = matmul("in_ab", h0, w['w_in_ab'], 'nn')
    att = attn_fwd(proj, nseq, lp, ex=hooks.gather_late if hooks else None)
    b_out, sb_tot, sb_first = att[:3]
    if hooks:
        w = {**w, **hooks.late_weights(att[3:])}
    s5_names = ['s5_lam_re', 's5_lam_im', 's5_log_dt', 's5_b_re', 's5_b_im', 's5_c_re', 's5_c_im', 's5_d']
    mats, mats_vjp = jax.vjp(s5_matrices, *[w[n][0] for n in s5_names])
    ug = _to_groups(proj[:, :S5_WIDTH])
    s5_pows = s5_scan_powers(*[w[n][0] for n in s5_names[:3]])
    yg, sst = s5_fwd(ug, *mats[:3], *s5_pows, nseq)
    ys5 = _from_groups(yg)
    y = gelu_fwd("gelu", ys5)
    gp = matmul("glu", y, w['s5_w_glu'], 'nn', bias=w['s5_b_glu'])
    a_out = glu_gate("glu_gate", y, gp)
    cat = jnp.concatenate([a_out, b_out], axis=1)
    mix0 = matmul("out_ab", cat, w['w_out_ab'], 'nn')
    h1, xh1, rs1 = ln_fwd("ln_mix0", h0, mix0, row(w['ln_mix_g'][0]), row(w['ln_mix_b'][0]))
    hid0 = matmul("up0", h1, w['mlp_w_up'][0], 'nn', bias=row(w['mlp_b_up'][0]), act='relu2', out_dtype=BF16)
    dn0 = matmul("down0", hid0, w['mlp_w_down'][0], 'nn', bias=row(w['mlp_b_down'][0]))
    h2, xh2, rs2 = ln_fwd("ln_mlp0", h1, dn0, row(w['ln_mlp_g'][0]), row(w['ln_mlp_b'][0]))

    projc = matmul("in_c", h2, w['w_in_c'], 'nn')
    lb, lb_vjp = jax.vjp(_lower_bound, w['hgrn_gamma'])
    ng = w['hgrn_norm_g']
    res = hgrn_fwd(projc, lb, ng, nseq, lp, ex=hooks.gather_last if hooks else None)
    o, og, states = res[:3]
    if hooks:
        w = {**w, **hooks.last_weights(res[3:])}
    mix1 = matmul("out_c", og, w['w_out_c'], 'nn')
    h3, xh3, rs3 = ln_fwd("ln_mix1", h2, mix1, row(w['ln_mix_g'][1]), row(w['ln_mix_b'][1]))
    hid1 = matmul("up1", h3, w['mlp_w_up'][1], 'nn', bias=row(w['mlp_b_up'][1]), act='relu2', out_dtype=BF16)
    dn1 = matmul("down1", hid1, w['mlp_w_down'][1], 'nn', bias=row(w['mlp_b_down'][1]))
    last_ln = ln_loss("ln_mlp1_loss", h3, dn1, row(w['ln_mlp_g'][1]), row(w['ln_mlp_b'][1]), tgt_p, lp, seq)
    loss = last_ln[4]

    def by_parts(tag, parts, w_slots, h_in, dres):
        dh = dres
        for j, p in enumerate(parts):
            dh = matmul(f"{tag}_dx{j}", p, w_slots[j], 'nt', add=dh, add_scale=ALPHA if j == 0 else 1.0)
        return dh, jnp.stack([matmul(f"{tag}_dw{j}", h_in, p, 'tn') for j, p in enumerate(parts)])

    def mlp_bwd(l, ln_back, hid, h_in):
        dpre, dg_, db_, dbd = ln_back
        dpu = matmul(f"down{l}_dx", dpre, w['mlp_w_down'][l], 'nt', relu2_of=hid, out_dtype=BF16)
        dwd = matmul(f"down{l}_dw", hid, dpre, 'tn')
        dh_in = matmul(f"up{l}_dx", dpu, w['mlp_w_up'][l], 'nt', add=dpre, add_scale=ALPHA)
        dwu, dbu = matmul(f"up{l}_dw", h_in, dpu, 'tn', out_slots=N_CHIPS, colsum=True)
        return dh_in, dict(ln_mlp_g=dg_, ln_mlp_b=db_, mlp_b_down=dbd, mlp_b_up=dbu, mlp_w_up=dwu, mlp_w_down=dwd)

    dh3, gm1 = mlp_bwd(1, last_ln[:4], hid1, h3)
    dpre, dg1, db1, _ = ln_bwd("ln_mix1_b", dh3, xh3, rs3, row(w['ln_mix_g'][1]))
    g['w_out_c'] = matmul("out_c_dw", og, dpre, 'tn')
    dog = matmul("out_c_dx", dpre, w['w_out_c'], 'nt')
    dq, df, di, dgg, dlb, dng = hgrn_bwd(projc, o, dog, states, lb, ng, nseq, lp)
    dh2, g['w_in_c'] = by_parts("in_c", [dq, df, di, dgg], w['w_in_c'], h2, dpre)
    g['hgrn_gamma'] = lb_vjp(dlb)[0]
    g['hgrn_norm_g'] = dng

    dh1, gm0 = mlp_bwd(0, ln_bwd("ln_mlp0_b", dh2, xh2, rs2, row(w['ln_mlp_g'][0])), hid0, h1)
    dpre, dg0, db0, _ = ln_bwd("ln_mix0_b", dh1, xh1, rs1, row(w['ln_mix_g'][0]))
    g['w_out_ab'] = matmul("out_ab_dw", cat, dpre, 'tn')
    dcat = matmul("out_ab_dx", dpre, w['w_out_ab'], 'nt')
    dgp, da_s, dbglu = glu_gate_bwd("glu_gate_b", dcat[:, :S5_WIDTH], y, gp)
    g['s5_w_glu'] = matmul("glu_dw", y, dgp, 'tn')
    g['s5_b_glu'] = dbglu
    dy5 = matmul("glu_dx", dgp, w['s5_w_glu'], 'nt', add=da_s)
    dys5 = gelu_bwd("gelu_b", dy5, ys5)
    for n in ('mlp_w_up', 'mlp_w_down'):
        g[n] = [gm0[n], gm1[n]]
    g['ln_mix_g'] = jnp.concatenate([dg0, dg1], axis=0)
    g['ln_mix_b'] = jnp.concatenate([db0, db1], axis=0)
    for n in ('ln_mlp_g', 'ln_mlp_b', 'mlp_b_down', 'mlp_b_up'):
        g[n] = jnp.concatenate([gm0[n], gm1[n]], axis=0)
    res = s5_bwd(_to_groups(dys5), ug, sst, *mats[:3], *s5_pows, nseq, ex=hooks.fold_early(g) if hooks else None)
    dug, da1, da2, da3, dare, daim = res[:6]
    for n, v in zip(s5_names, mats_vjp((da1, da2, da3, dare, daim))):
        g[n] = v[None]
    ex = hooks.scatter_early(g, loss, res[6:]) if hooks else None
    res = attn_bwd(proj, sb_tot, sb_first, dcat[:, S5_WIDTH:], nseq, lp, ex=ex)
    dqa, dka, dva = res[:3]
    if hooks:
        hooks.scattered(res[3:])
    dh0, g['w_in_ab'] = by_parts("in_ab", [_from_groups(dug), dqa, dka, dva], w['w_in_ab'], h0, dpre)

    dh0 = dh0.reshape(nseq, lp, D_MODEL)
    grad_x = dh0[:, N_META:N_META + seq]
    g['meta'] = jnp.sum(dh0[:, :N_META], axis=0)
    return loss, grad_x, g


class Exchange:
    def __init__(self, ins, out_shapes, n_remote, build, in_place):
        self.ins, self.out_shapes, self.n_remote, self.build, self.in_place = ins, out_shapes, n_remote, build, in_place

    def sems(self):
        return [pltpu.SemaphoreType.DMA((self.n_remote,)), pltpu.SemaphoreType.DMA((self.n_remote,))]

    def aliases(self):
        return self.in_place if isinstance(self.in_place, dict) else \
            {i: i for i in range(len(self.ins) if self.in_place else 0)}

    def beside(self, other):
        na, nao = len(self.ins), len(self.out_shapes)

        def build(in_refs, out_refs, me):
            pa = self.build(in_refs[:na], out_refs[:nao], me)
            pb = other.build(in_refs[na:], out_refs[nao:], me)
            n = max(len(pa), len(pb))
            return [(pa[i] if i < len(pa) else []) + (pb[i] if i < len(pb) else []) for i in range(n)]
        al = dict(self.aliases())
        al.update({na + i: nao + j for i, j in other.aliases().items()})
        return Exchange(list(self.ins) + list(other.ins), list(self.out_shapes) + list(other.out_shapes),
                        self.n_remote + other.n_remote, build, al)

    def phases(self, in_refs, out_refs, ssem, rsem):
        me = (lax.axis_index("x"), lax.axis_index("y"), lax.axis_index("c"))
        out, k = [], 0
        for phase in self.build(in_refs, out_refs, me):
            out.append(functools.partial(
                lambda phase, k: [pltpu.make_async_remote_copy(src_ref=s, dst_ref=d, send_sem=ssem.at[k + i],
                                                               recv_sem=rsem.at[k + i], device_id=p,
                                                               device_id_type=MESH)
                                  for i, (s, d, p) in enumerate(phase)], phase, k))
            k += len(phase)
        return out

    def start(self, in_refs, out_refs, ssem, rsem):
        for cp in self.phases(in_refs, out_refs, ssem, rsem)[0]():
            cp.start()

    def finish(self, in_refs, out_refs, ssem, rsem):
        phases = self.phases(in_refs, out_refs, ssem, rsem)
        for cp in phases[0]():
            cp.wait()
        for make in phases[1:]:
            copies = make()
            for cp in copies:
                cp.start()
            for cp in copies:
                cp.wait()


def _exchange(name, ex):
    ni, no = len(ex.ins), len(ex.out_shapes)

    def body(*refs):
        in_refs, out_refs, sems = refs[:ni], refs[ni:ni + no], refs[ni + no:]
        ex.start(in_refs, out_refs, *sems)
        ex.finish(in_refs, out_refs, *sems)

    anyspec = pl.BlockSpec(memory_space=pl.ANY)
    return pl.pallas_call(
        body, name=name, in_specs=[anyspec] * ni, out_specs=[anyspec] * no, out_shape=ex.out_shapes,
        input_output_aliases=ex.aliases(), scratch_shapes=ex.sems())(*ex.ins)


def _chip_peers(me):
    x, y, c = me
    return [(x ^ (m >> 1), y ^ (m & 1), c) for m in (1, 2, 3)]


def _half(ref, c, axis):
    h = ref.shape[axis] // 2
    idx = [slice(None)] * axis + [pl.ds(c * h, h)]
    return ref.at[tuple(idx)]


def _like(arrs):
    return [jax.ShapeDtypeStruct(a.shape, a.dtype) for a in arrs]


def gather_over_chips(bufs):
    def build(in_refs, out_refs, me):
        x, y, c = me
        j = 2 * x + y
        sib = (x, y, 1 - c)
        ici = [(_half(o.at[j], c, 0), _half(o.at[j], c, 0), p) for o in out_refs for p in _chip_peers(me)]
        d2d = [(_half(o.at[2 * p[0] + p[1]], c, 0), _half(o.at[2 * p[0] + p[1]], c, 0), sib)
               for o in out_refs for p in _chip_peers(me)]
        return [ici, d2d]
    return Exchange(bufs, _like(bufs), 6 * len(bufs), build, True)


def fold_cores(fulls):
    def build(in_refs, out_refs, me):
        x, y, c = me
        return [[(_half(s, 1 - c, 1), o, (x, y, 1 - c)) for s, o in zip(in_refs, out_refs)]]
    shapes = [jax.ShapeDtypeStruct((f.shape[0], f.shape[1] // 2, f.shape[2]), f.dtype) for f in fulls]
    return Exchange(fulls, shapes, len(fulls), build, False)


def scatter_over_chips(parts):
    def build(in_refs, out_refs, me):
        j = 2 * me[0] + me[1]
        return [[(s.at[2 * p[0] + p[1]], o.at[j], p) for s, o in zip(in_refs, out_refs) for p in _chip_peers(me)]]
    return Exchange(parts, _like(parts), 3 * len(parts), build, False)


def join_cores(bufs):
    def build(in_refs, out_refs, me):
        x, y, c = me
        return [[(o.at[c], o.at[c], (x, y, 1 - c)) for o in out_refs]]
    return Exchange(bufs, _like(bufs), len(bufs), build, True)


def gather_over_devices(buf):
    def build(in_refs, out_refs, me):
        x, y, c = me
        out = out_refs[0]
        sib = (x, y, 1 - c)
        mine = out.at[4 * x + 2 * y + c]
        first = [(mine, mine, sib)] + [(mine, mine, p) for p in _chip_peers(me)]
        second = [(out.at[4 * p[0] + 2 * p[1] + c], out.at[4 * p[0] + 2 * p[1] + c], sib) for p in _chip_peers(me)]
        return [first, second]
    return Exchange([buf], _like([buf]), N_DEV - 1, build, True)


def _slot_call(name, body, scalars, ins, in_maps, out_shapes, out_maps, blk, grid):
    gs = pltpu.PrefetchScalarGridSpec(
        num_scalar_prefetch=1, grid=grid,
        in_specs=[pl.BlockSpec((None,) + blk, m) for m in in_maps],
        out_specs=[pl.BlockSpec((None,) + blk, m) for m in out_maps])
    return pl.pallas_call(body, name=name, grid_spec=gs, out_shape=out_shapes,
                          compiler_params=_params(("arbitrary",) * len(grid)))(scalars, *ins)


def _group_rows(r, n):
    return _pick(r, (512, 256, 128, 64, 32, 16) if n <= 2 else (256, 128, 64, 32, 16))


def cast_into_slot(name, ws, chip, dtype):
    n = len(ws)
    r, wd = ws[0].shape
    tm = _group_rows(r, n)

    def body(s_ref, *refs):
        for w_ref, o_ref in zip(refs[:n], refs[n:]):
            o_ref[...] = w_ref[...].astype(o_ref.dtype)

    gs = pltpu.PrefetchScalarGridSpec(
        num_scalar_prefetch=1, grid=(r // tm,),
        in_specs=[pl.BlockSpec((tm, wd), lambda i, s: (i, 0))] * n,
        out_specs=[pl.BlockSpec((None, tm, wd), lambda i, s: (s[0], i, 0))] * n)
    return pl.pallas_call(body, name=name, grid_spec=gs,
                          out_shape=[jax.ShapeDtypeStruct((N_CHIPS, r, wd), dtype)] * n,
                          compiler_params=_params(("arbitrary",)))(chip.reshape(1), *ws)


def sum_cores(name, fulls, theirs, core):
    n = len(fulls)
    s, r, wd = fulls[0].shape
    h = r // 2
    tm = _group_rows(h, n)
    nt = h // tm

    def body(s_ref, *refs):
        for k in range(n):
            refs[2 * n + k][...] = (refs[2 * k][...] + refs[2 * k + 1][...]).astype(BF16)

    ins = [x for pair in zip(fulls, theirs) for x in pair]
    return _slot_call(name, body, core.reshape(1), ins,
                      [lambda k, i, s_: (k, s_[0] * nt + i, 0), lambda k, i, s_: (k, i, 0)] * n,
                      [jax.ShapeDtypeStruct((s, h, wd), BF16)] * n, [lambda k, i, s_: (k, i, 0)] * n,
                      (tm, wd), (s, nt))


def sum_chips(name, owns, recvs, chip, core):
    n = len(owns)
    s, r, wd = owns[0].shape
    tm = _group_rows(r, n)

    def body(s_ref, *refs):
        for k in range(n):
            acc = refs[4 * k][...].astype(F32)
            for ref in refs[4 * k + 1:4 * k + 4]:
                acc = acc + ref[...].astype(F32)
            refs[4 * n + k][...] = acc

    maps = [lambda i, s_: (s_[0], i, 0)]
    maps += [functools.partial(lambda i, s_, m: (s_[0] ^ m, i, 0), m=m) for m in (1, 2, 3)]
    ins = [x for own, recv in zip(owns, recvs) for x in (own, recv, recv, recv)]
    return _slot_call(name, body, jnp.stack([chip, core]), ins, maps * n,
                      [jax.ShapeDtypeStruct((2, r, wd), F32)] * n, [lambda i, s_: (s_[1], i, 0)] * n,
                      (tm, wd), (r // tm,))


def sum_slots(name, arr):
    s, r, wd = arr.shape
    tm = _pick(r, (512, 256, 128, 64, 32, 16, 8))

    def body(*refs):
        acc = refs[0][...].astype(F32)
        for ref in refs[1:s]:
            acc = acc + ref[...].astype(F32)
        refs[s][...] = acc

    specs = [pl.BlockSpec((None, tm, wd), functools.partial(lambda i, k: (k, i, 0), k=k)) for k in range(s)]
    return pl.pallas_call(body, name=name, grid=(r // tm,), in_specs=specs,
                          out_specs=pl.BlockSpec((tm, wd), lambda i: (i, 0)),
                          out_shape=jax.ShapeDtypeStruct((r, wd), F32),
                          compiler_params=_params(("parallel",)))(*([arr] * s))


def _pack(arrs):
    flat = jnp.concatenate([a.reshape(-1) for a in arrs])
    n = flat.shape[0]
    padded = -(-n // (512 * LANES)) * (512 * LANES)
    return jnp.pad(flat, (0, padded - n)).reshape(-1, LANES)


def _unpack(packed, shapes):
    flat = packed.reshape(-1)
    out, pos = [], 0
    for s in shapes:
        n = math.prod(s)
        out.append(flat[pos:pos + n].reshape(s))
        pos += n
    return out


def _as2d(a):
    return a.reshape(-1, a.shape[-1])


def kernel(x, meta, w_in_ab, s5_lam_re, s5_lam_im, s5_log_dt, s5_b_re, s5_b_im, s5_c_re, s5_c_im, s5_d, s5_w_glu, s5_b_glu, w_out_ab, w_in_c, hgrn_gamma, hgrn_norm_g, w_out_c, ln_mix_g, ln_mix_b, mlp_w_up, mlp_b_up, mlp_w_down, mlp_b_down, ln_mlp_g, ln_mlp_b, loss_target, m_meta, m_w_in_ab, m_s5_lam_re, m_s5_lam_im, m_s5_log_dt, m_s5_b_re, m_s5_b_im, m_s5_c_re, m_s5_c_im, m_s5_d, m_s5_w_glu, m_s5_b_glu, m_w_out_ab, m_w_in_c, m_hgrn_gamma, m_hgrn_norm_g, m_w_out_c, m_ln_mix_g, m_ln_mix_b, m_mlp_w_up, m_mlp_b_up, m_mlp_w_down, m_mlp_b_down, m_ln_mlp_g, m_ln_mlp_b, v_meta, v_w_in_ab, v_s5_lam_re, v_s5_lam_im, v_s5_log_dt, v_s5_b_re, v_s5_b_im, v_s5_c_re, v_s5_c_im, v_s5_d, v_s5_w_glu, v_s5_b_glu, v_w_out_ab, v_w_in_c, v_hgrn_gamma, v_hgrn_norm_g, v_w_out_c, v_ln_mix_g, v_ln_mix_b, v_mlp_w_up, v_mlp_b_up, v_mlp_w_down, v_mlp_b_down, v_ln_mlp_g, v_ln_mlp_b):
    given = dict(locals())
    wts = {n: given[n] for n in WEIGHTS}
    ms = {n: given['m_' + n] for n in WEIGHTS}
    vs = {n: given['v_' + n] for n in WEIGHTS}
    chip = 2 * lax.axis_index("x") + lax.axis_index("y")

    core = lax.axis_index("c")
    parts = [(n, l) for n in BIG for l in (range(DEPTH) if n.startswith('mlp_') else [0])]

    def by_shape(tag, fn, keys, *lists):
        out = {}
        for shape in dict.fromkeys(a.shape for a in lists[0]):
            idx = [i for i, a in enumerate(lists[0]) if a.shape == shape]
            res = fn(f"{tag}_{keys[idx[0]][0]}{keys[idx[0]][1]}_x{len(idx)}", *[[lst[i] for i in idx] for lst in lists])
            out.update({keys[i]: r for i, r in zip(idx, res)})
        return [out[k] for k in keys]

    shards = dict(zip(parts, by_shape("cast", lambda nm, ws: cast_into_slot(nm, ws, chip, BF16), parts,
                                      [wts[n][l] for n, l in parts])))
    small_sh = jnp.concatenate([meta, hgrn_norm_g, jnp.zeros((15, meta.shape[1]), F32)], axis=0)
    first, late = parts[:1], parts[1:]
    w_in_ab_all, sm = _exchange("gather_first", gather_over_chips(
        [shards[p] for p in first] + list(cast_into_slot("cast_small", [small_sh], chip, F32))))
    w = {n: wts[n] for n in SMALL}
    w['meta'] = sm[:, :N_META].transpose(1, 0, 2).reshape(N_META, D_MODEL)
    w['hgrn_norm_g'] = sm[:, N_META:N_META + 1].transpose(1, 0, 2).reshape(1, D_MODEL)
    w['w_in_ab'] = w_in_ab_all

    def slot_major(v):
        return v if v.ndim == 3 else v.reshape(N_CHIPS, -1, v.shape[-1])

    def whole_grads(ps, g):
        return [slot_major(g[n][l] if n.startswith('mlp_') else g[n]) for n, l in ps]

    def chip_sums_of(ps, gfull, theirs):
        return by_shape("sum_cores", lambda nm, a, b: sum_cores(nm, a, b, core), ps, gfull, list(theirs))

    def all_devices(arrs):
        packed = _pack(arrs)
        return gather_over_devices(lax.dynamic_update_slice(
            jnp.zeros((N_DEV,) + packed.shape, F32), packed[None], (2 * chip + core, 0, 0)))

    early_small = [n for n in SMALL if n != 'meta']
    last = [('w_out_c', 0), ('mlp_w_up', 1), ('mlp_w_down', 1)]
    mid = [p for p in late if p not in last]

    class Hooks:
        gather_late = gather_over_chips([shards[p] for p in mid])
        gather_last = gather_over_chips([shards[p] for p in last])

        @staticmethod
        def late_weights(filled):
            fw = Hooks.fw = dict(zip(mid, filled))
            return {'s5_w_glu': fw['s5_w_glu', 0].reshape(S5_WIDTH, S5_WIDTH),
                    'w_out_ab': fw['w_out_ab', 0].reshape(D_MODEL, D_MODEL),
                    'w_in_c': fw['w_in_c', 0],
                    'mlp_w_up': [fw['mlp_w_up', 0]],
                    'mlp_w_down': [fw['mlp_w_down', 0].reshape(D_FF, D_MODEL)]}

        @staticmethod
        def last_weights(filled):
            fw = dict(zip(last, filled))
            return {'w_out_c': fw['w_out_c', 0].reshape(D_MODEL, D_MODEL),
                    'mlp_w_up': [Hooks.fw['mlp_w_up', 0], fw['mlp_w_up', 1]],
                    'mlp_w_down': [Hooks.fw['mlp_w_down', 0].reshape(D_FF, D_MODEL),
                                   fw['mlp_w_down', 1].reshape(D_FF, D_MODEL)]}

        @staticmethod
        def fold_early(g):
            Hooks.whole = whole_grads(late, g)
            return fold_cores(Hooks.whole)

        @staticmethod
        def scatter_early(g, loss, theirs):
            Hooks.sums = chip_sums_of(late, Hooks.whole, theirs)
            Hooks.small_shapes = [(LANES,)] + [g[n].shape for n in early_small]
            return scatter_over_chips(Hooks.sums).beside(
                all_devices([loss.reshape(-1)] + [g[n] for n in early_small]))

        @staticmethod
        def scattered(outs):
            Hooks.recv, Hooks.small = list(outs[:-1]), outs[-1]

    loss, grad_x, g = local_step(x, loss_target, w, Hooks)

    whole = whole_grads(first, g)
    sums = chip_sums_of(first, whole, _exchange("fold_last", fold_cores(whole)))
    *recv, meta_slots = _exchange("scatter_last", scatter_over_chips(sums).beside(all_devices([g['meta']])))
    reduced = _exchange("join_grads", join_cores(
        by_shape("sum_chips", lambda nm, a, b: sum_chips(nm, a, b, chip, core), first + late,
                 sums + Hooks.sums, list(recv) + Hooks.recv)))
    reduced = dict(zip(first + late, [_as2d(r) for r in reduced]))
    red = _unpack(sum_slots("sum_small", Hooks.small), Hooks.small_shapes)
    loss_out = red[0][0]
    gs = dict(zip(early_small, red[1:]))
    gs['meta'] = _unpack(sum_slots("sum_meta", meta_slots), [g['meta'].shape])[0]
    gs['meta'] = lax.dynamic_slice_in_dim(gs['meta'], chip * meta.shape[1], meta.shape[1], axis=1)
    gs['hgrn_norm_g'] = lax.dynamic_slice_in_dim(gs['hgrn_norm_g'].reshape(1, -1), chip * meta.shape[1],
                                                 meta.shape[1], axis=1)

    grads, deltas, new_m, new_v = {}, {}, {}, {}
    for n in BIG:
        r = jnp.concatenate([reduced[n, l] for l in range(wts[n].shape[0])], axis=0)
        res = adamw("adamw_" + n, _as2d(wts[n]), [r], _as2d(ms[n]), _as2d(vs[n]))
        grads[n], deltas[n], new_m[n], new_v[n] = [r.reshape(wts[n].shape) for r in res]
    shapes = [wts[n].shape for n in SMALL]
    res = adamw("adamw_small", _pack([wts[n] for n in SMALL]), [_pack([gs[n] for n in SMALL])],
                _pack([ms[n] for n in SMALL]), _pack([vs[n] for n in SMALL]))
    for d, r in zip((grads, deltas, new_m, new_v), res):
        d.update(zip(SMALL, _unpack(r, shapes)))
    return (loss_out, grad_x, *[grads[n] for n in WEIGHTS], *[deltas[n] for n in WEIGHTS],
            *[new_m[n] for n in WEIGHTS], *[new_v[n] for n in WEIGHTS])
```

```python
import functools
import math

import jax
import jax.numpy as jnp
from jax import lax
from jax.experimental import pallas as pl
from jax.experimental.pallas import tpu as pltpu

F32 = jnp.float32
BF16 = jnp.bfloat16

D_MODEL = 1024
N_META = 16
DEPTH = 2
S5_WIDTH = 512
S5_GROUP = 16
S5_GROUPS = 32
S5_STATE = 64
S5_CHUNK = 16
SB_WIDTH = 512
SB_BLOCK = 128
SB_DEAD = -110.0
HG_HEADS = 8
HG_DK = 128
HG_CHUNK = 64
HG_UNROLL = 2
D_FF = 4096
ALPHA = (2.0 * DEPTH) ** 0.25
LN_EPS = 1e-5
RMS_EPS = 1e-6
ADAM_LR = 0.001
ADAM_B1 = 0.9
ADAM_B2 = 0.999
ADAM_EPS = 1e-08
ADAM_WD = 0.01
ADAM_STEP = 10
N_CHIPS = 4
N_DEV = 8
LANES = 128
MESH = pl.DeviceIdType.MESH
VMEM_LIMIT = 56 * 1024 * 1024
WHOLE_K_BYTES = 42 * 1024 * 1024
HGRN_BWD_VMEM = 60 * 1024 * 1024

WEIGHTS = ['meta', 'w_in_ab', 's5_lam_re', 's5_lam_im', 's5_log_dt', 's5_b_re', 's5_b_im', 's5_c_re', 's5_c_im',
           's5_d', 's5_w_glu', 's5_b_glu', 'w_out_ab', 'w_in_c', 'hgrn_gamma', 'hgrn_norm_g', 'w_out_c',
           'ln_mix_g', 'ln_mix_b', 'mlp_w_up', 'mlp_b_up', 'mlp_w_down', 'mlp_b_down', 'ln_mlp_g', 'ln_mlp_b']
BIG = ['w_in_ab', 's5_w_glu', 'w_out_ab', 'w_in_c', 'w_out_c', 'mlp_w_up', 'mlp_w_down']
SMALL = [n for n in WEIGHTS if n not in BIG]


def _params(sem=None):
    return pltpu.CompilerParams(dimension_semantics=sem, vmem_limit_bytes=VMEM_LIMIT)


def _pick(n, cands):
    for c in cands:
        if n % c == 0:
            return c
    return n


def _dot(a, b, ca, cb):
    return lax.dot_general(a, b, (((ca,), (cb,)), ((), ())), preferred_element_type=F32)


def _split3(x):
    hi = x.astype(BF16)
    r = x - hi.astype(F32)
    mid = r.astype(BF16)
    lo = (r - mid.astype(F32)).astype(BF16)
    return hi, mid, lo


def _dot_exact_rhs(x, m, cx, cm):
    hi = x.astype(BF16)
    lo = (x - hi.astype(F32)).astype(BF16)
    return _dot(hi, m, cx, cm) + _dot(lo, m, cx, cm)


def _dot3(a, b, ca, cb):
    ah = a.astype(BF16)
    al = (a - ah.astype(F32)).astype(BF16)
    bh = b.astype(BF16)
    bl = (b - bh.astype(F32)).astype(BF16)
    return _dot(ah, bh, ca, cb) + (_dot(ah, bl, ca, cb) + _dot(al, bh, ca, cb))


def rowwise(name, fn, row_ins, bc_ins, out_widths, sum_widths=(), out_dtypes=None, tm=None):
    t = row_ins[0].shape[0]
    if tm is None:
        wmax = max([a.shape[1] for a in row_ins] + list(out_widths))
        tm = _pick(t, (272, 256, 128, 64, 16, 8)) if wmax > 1024 else _pick(t, (544, 512, 256, 128, 64, 16, 8))
    nr, nb, no, ns = len(row_ins), len(bc_ins), len(out_widths), len(sum_widths)
    out_dtypes = out_dtypes or [F32] * no

    def body(*refs):
        i = pl.program_id(0)
        rows = [r[...] for r in refs[:nr]]
        bcs = [r[...] for r in refs[nr:nr + nb]]
        outs, sums = fn(i, tm, rows, bcs)
        for r, v in zip(refs[nr + nb:nr + nb + no], outs):
            r[...] = v.astype(r.dtype)
        if ns:
            srefs = refs[nr + nb + no:]

            @pl.when(i == 0)
            def _():
                for r in srefs:
                    r[...] = jnp.zeros(r.shape, r.dtype)

            for r, v in zip(srefs, sums):
                r[...] += v

    in_specs = [pl.BlockSpec((tm, a.shape[1]), lambda i: (i, 0)) for a in row_ins]
    in_specs += [pl.BlockSpec(a.shape, lambda i: (0, 0)) for a in bc_ins]
    out_specs = [pl.BlockSpec((tm, w), lambda i: (i, 0)) for w in out_widths]
    out_specs += [pl.BlockSpec((1, w), lambda i: (0, 0)) for w in sum_widths]
    out_shape = [jax.ShapeDtypeStruct((t, w), dt) for w, dt in zip(out_widths, out_dtypes)]
    out_shape += [jax.ShapeDtypeStruct((1, w), F32) for w in sum_widths]
    res = pl.pallas_call(body, name=name, grid=(t // tm,), in_specs=in_specs, out_specs=out_specs,
                         out_shape=out_shape, compiler_params=_params(("arbitrary",)))(*row_ins, *bc_ins)
    return res


def _colsum(v):
    return jnp.sum(v, axis=0, keepdims=True)


def matmul(name, a, b, mode, *, bias=None, act=None, add=None, add_scale=1.0, relu2_of=None, colsum=False,
           out_dtype=F32, out_slots=0, tm=None, tn=None, tk=None, ex=None):
    slotted = b.ndim == 3
    if mode == 'nn':
        m, k = a.shape
        n = b.shape[-1] * (b.shape[0] if slotted else 1)
    elif mode == 'nt':
        m, k = a.shape
        n = b.shape[-2]
    else:
        k, m = a.shape
        n = b.shape[-1]
    ns = b.shape[-1] if slotted else None
    if mode == 'tn':
        tm = tm or _pick(m, (512, 256, 128))
        nw = n if not out_slots else n // out_slots
        if tn is None and tk is None:
            for cand in (512, 256):
                if nw % cand == 0 and (2 * k * (tm * a.dtype.itemsize + cand * b.dtype.itemsize)
                                       + 2 * tm * cand * 4 <= WHOLE_K_BYTES):
                    tn, tk = cand, k
                    break
        tn = tn or _pick(nw, (1024, 512, 256, 128))
        tk = tk or _pick(k, (1088, 1024, 768, 512, 384, 256, 128))
    else:
        tm = tm or _pick(m, (1088, 1024, 768, 512, 384, 256, 128))
        tn = tn or _pick(n if not (slotted and mode == 'nn') else ns, (1024, 512, 256, 128))
        extra = sum(x is not None for x in (add, relu2_of)) * 4
        if tk is None and not slotted and (2 * (tm * k * a.dtype.itemsize + k * tn * b.dtype.itemsize)
                                           + 2 * tm * tn * (4 + extra) <= WHOLE_K_BYTES):
            tk = k
        tk = tk or _pick(k if not (slotted and mode == 'nt') else ns, (1024, 512, 256, 128))
    gm, gn, gk = m // tm, n // tn, k // tk

    if mode == 'nn':
        a_spec = pl.BlockSpec((tm, tk), lambda i, j, kk: (i, kk))
        if slotted:
            per = ns // tn
            b_spec = pl.BlockSpec((None, tk, tn), lambda i, j, kk: (j // per, kk, j % per))
        else:
            b_spec = pl.BlockSpec((tk, tn), lambda i, j, kk: (kk, j))
        ca, cb = 1, 0
    elif mode == 'nt':
        a_spec = pl.BlockSpec((tm, tk), lambda i, j, kk: (i, kk))
        if slotted:
            per = ns // tk
            b_spec = pl.BlockSpec((None, tn, tk), lambda i, j, kk: (kk // per, j, kk % per))
        else:
            b_spec = pl.BlockSpec((tn, tk), lambda i, j, kk: (j, kk))
        ca, cb = 1, 1
    else:
        a_spec = pl.BlockSpec((tk, tm), lambda i, j, kk: (kk, i))
        b_spec = pl.BlockSpec((tk, tn), lambda i, j, kk: (kk, j))
        ca, cb = 0, 0
    in_specs = [a_spec, b_spec]
    args = [a, b]
    if bias is not None:
        in_specs.append(pl.BlockSpec((1, tn), lambda i, j, kk: (0, j)))
        args.append(bias)
    if add is not None:
        in_specs.append(pl.BlockSpec((tm, tn), lambda i, j, kk: (i, j)))
        args.append(add)
    if relu2_of is not None:
        in_specs.append(pl.BlockSpec((tm, tn), lambda i, j, kk: (i, j)))
        args.append(relu2_of)
    if out_slots:
        per_o = (n // out_slots) // tn
        out_spec = pl.BlockSpec((None, tm, tn), lambda i, j, kk: (j // per_o, i, j % per_o))
        out_shape = jax.ShapeDtypeStruct((out_slots, m, n // out_slots), out_dtype)
    else:
        out_spec = pl.BlockSpec((tm, tn), lambda i, j, kk: (i, j))
        out_shape = jax.ShapeDtypeStruct((m, n), out_dtype)
    grid = (gm, gn, gk)
    if colsum:
        assert mode == 'tn'
        out_spec = [out_spec, pl.BlockSpec((1, tn), lambda i, j, kk: (0, j))]
        out_shape = [out_shape, jax.ShapeDtypeStruct((1, n), F32)]

        def swapped(spec):
            return pl.BlockSpec(spec.block_shape, functools.partial(lambda j, i, kk, f: f(i, j, kk), f=spec.index_map))
        in_specs = [swapped(sp) for sp in in_specs]
        out_spec = [swapped(sp) for sp in out_spec]
        grid = (gn, gm, gk)

    split, ex_start, ex_finish, exkw = _hidden_exchange(ex, len(args), 2 if colsum else 1, grid)

    def body(*refs):
        ins, outs, ex_parts, own = split(refs)
        refs = (*ins, *outs, *own)
        ex_start(ex_parts)
        a_ref, b_ref = refs[0], refs[1]
        pos = 2
        bias_ref = add_ref = hid_ref = cs_ref = None
        if bias is not None:
            bias_ref = refs[pos]
            pos += 1
        if add is not None:
            add_ref = refs[pos]
            pos += 1
        if relu2_of is not None:
            hid_ref = refs[pos]
            pos += 1
        o_ref = refs[pos]
        pos += 1
        if colsum:
            cs_ref = refs[pos]
            pos += 1
        acc_ref = refs[pos] if gk > 1 else None
        bt = b_ref[...]
        p = _dot(a_ref[...].astype(BF16), bt.astype(BF16), ca, cb)
        kk = pl.program_id(2)

        if colsum:
            @pl.when(jnp.logical_and(pl.program_id(1) == 0, kk == 0))
            def _():
                cs_ref[...] = _colsum(bt.astype(F32))

            @pl.when(jnp.logical_and(pl.program_id(1) == 0, kk > 0))
            def _():
                cs_ref[...] += _colsum(bt.astype(F32))

        def finish(r):
            if bias_ref is not None:
                r = r + bias_ref[...]
            if act == 'relu2':
                r = jnp.square(jnp.maximum(r, 0.0))
            if hid_ref is not None:
                r = r * (2.0 * jnp.sqrt(hid_ref[...].astype(F32)))
            if add_ref is not None:
                r = r + add_scale * add_ref[...]
            o_ref[...] = r.astype(o_ref.dtype)

        if gk == 1:
            finish(p)
        else:
            @pl.when(kk == 0)
            def _():
                acc_ref[...] = p

            @pl.when(kk > 0)
            def _():
                acc_ref[...] += p

            @pl.when(kk == gk - 1)
            def _():
                finish(acc_ref[...])
        ex_finish(ex_parts)

    scratch = [pltpu.VMEM((tm, tn), F32)] if gk > 1 else []
    sem = ("arbitrary",) * 3 if colsum or ex is not None else ("parallel", "parallel", "arbitrary")
    if ex is not None:
        out_spec = (out_spec if colsum else [out_spec]) + exkw['out_specs']
        out_shape = (out_shape if colsum else [out_shape]) + exkw['out_shape']
    return pl.pallas_call(body, name=name, grid=grid, in_specs=in_specs + exkw['in_specs'], out_specs=out_spec,
                          out_shape=out_shape, scratch_shapes=scratch + exkw['scratch'],
                          input_output_aliases=exkw['aliases'], compiler_params=_params(sem))(*args, *exkw['ins'])


def ln_fwd(name, h, mix, g, b):
    def fn(i, tm, rows, bcs):
        pre = ALPHA * rows[0] + rows[1]
        mu = jnp.mean(pre, axis=1, keepdims=True)
        xc = pre - mu
        var = jnp.mean(xc * xc, axis=1, keepdims=True)
        rstd = lax.rsqrt(var + LN_EPS)
        xhat = xc * rstd
        return (xhat * bcs[0] + bcs[1], xhat, rstd), ()
    return rowwise(name, fn, [h, mix], [g, b], [D_MODEL, D_MODEL, 1])


def ln_bwd(name, dy, xhat, rstd, g):
    def fn(i, tm, rows, bcs):
        dy_, xh, rs = rows
        dyg = dy_ * bcs[0]
        m1 = jnp.mean(dyg, axis=1, keepdims=True)
        m2 = jnp.mean(dyg * xh, axis=1, keepdims=True)
        dpre = rs * (dyg - m1 - xh * m2)
        return (dpre,), (_colsum(dy_ * xh), _colsum(dy_), _colsum(dpre))
    return rowwise(name, fn, [dy, xhat, rstd], [g], [D_MODEL], [D_MODEL] * 3)


_GELU_C = math.sqrt(2.0 / math.pi)


def gelu_fwd(name, x):
    def fn(i, tm, rows, bcs):
        v = rows[0]
        u = _GELU_C * (v + 0.044715 * (v * v * v))
        return (0.5 * v * (1.0 + jnp.tanh(u)),), ()
    return rowwise(name, fn, [x], [], [x.shape[1]])[0]


def gelu_bwd(name, dy, x):
    def fn(i, tm, rows, bcs):
        v = rows[1]
        th = jnp.tanh(_GELU_C * (v + 0.044715 * (v * v * v)))
        dg = 0.5 * (1.0 + th) + 0.5 * v * (1.0 - th * th) * (_GELU_C * (1.0 + 3.0 * 0.044715 * v * v))
        return (rows[0] * dg,), ()
    return rowwise(name, fn, [dy, x], [], [x.shape[1]])[0]


def glu_gate(name, y, gp):
    def fn(i, tm, rows, bcs):
        return (rows[0] * jax.nn.sigmoid(rows[1]),), ()
    return rowwise(name, fn, [y, gp], [], [y.shape[1]], out_dtypes=[BF16])[0]


def glu_gate_bwd(name, da, y, gp):
    def fn(i, tm, rows, bcs):
        s = jax.nn.sigmoid(rows[2])
        dgp = rows[0] * rows[1] * s * (1.0 - s)
        return (dgp, rows[0] * s), (_colsum(dgp),)
    w = y.shape[1]
    return rowwise(name, fn, [da, y, gp], [], [w, w], [w])


def ln_loss(name, h, mix, g, b, tgt, lp, seq):
    def fn(i, tm, rows, bcs):
        pre = ALPHA * rows[0] + rows[1]
        mu = jnp.mean(pre, axis=1, keepdims=True)
        xc = pre - mu
        var = jnp.mean(xc * xc, axis=1, keepdims=True)
        rstd = lax.rsqrt(var + LN_EPS)
        xhat = xc * rstd
        pos = (i * tm + lax.broadcasted_iota(jnp.int32, (tm, 1), 0)) % lp
        valid = jnp.logical_and(pos >= N_META, pos < N_META + seq)
        err = jnp.where(valid, (xhat * bcs[0] + bcs[1]) - rows[2], 0.0)
        part = 0.5 * jnp.sum(jnp.mean(err * err, axis=1, keepdims=True), axis=0, keepdims=True)
        dy = err * (1.0 / D_MODEL)
        dyg = dy * bcs[0]
        m1 = jnp.mean(dyg, axis=1, keepdims=True)
        m2 = jnp.mean(dyg * xhat, axis=1, keepdims=True)
        dpre = rstd * (dyg - m1 - xhat * m2)
        return (dpre,), (_colsum(dy * xhat), _colsum(dy), _colsum(dpre), jnp.broadcast_to(part, (1, LANES)))
    return rowwise(name, fn, [h, mix, tgt], [g, b], [D_MODEL], [D_MODEL] * 3 + [LANES])


def adamw(name, w, gs, m, v):
    ng = len(gs)

    def fn(i, tm, rows, bcs):
        w_ = rows[0]
        g = rows[1] if ng == 1 else rows[1] + rows[2]
        m_, v_ = rows[1 + ng], rows[2 + ng]
        m_ = ADAM_B1 * m_ + (1.0 - ADAM_B1) * g
        v_ = ADAM_B2 * v_ + (1.0 - ADAM_B2) * jnp.square(g)
        m_hat = m_ / (1.0 - ADAM_B1 ** ADAM_STEP)
        v_hat = v_ / (1.0 - ADAM_B2 ** ADAM_STEP)
        delta = -ADAM_LR * (m_hat / (jnp.sqrt(v_hat) + ADAM_EPS) + ADAM_WD * w_)
        return (g, delta, m_, v_), ()
    wd = w.shape[1]
    return rowwise(name, fn, [w] + list(gs) + [m, v], [], [wd] * 4)


def s5_matrices(lam_re, lam_im, log_dt, b_re, b_im, c_re, c_im, d):
    c, hh, gg, pp = S5_CHUNK, S5_GROUP, S5_GROUPS, S5_STATE
    dt = jnp.exp(log_dt)[:, None]
    tau = jnp.arange(c + 1, dtype=F32)[None, :, None]
    mag = jnp.exp(tau * (lam_re * dt)[:, None, :])
    ang = tau * (lam_im * dt)[:, None, :]
    pw_re, pw_im = mag * jnp.cos(ang), mag * jnp.sin(ang)
    nr, ni = pw_re[:, 1] - 1.0, pw_im[:, 1]
    den = lam_re * lam_re + lam_im * lam_im
    cf_re = (nr * lam_re + ni * lam_im) / den
    cf_im = (ni * lam_re - nr * lam_im) / den
    bb_re = cf_re[:, :, None] * b_re - cf_im[:, :, None] * b_im
    bb_im = cf_re[:, :, None] * b_im + cf_im[:, :, None] * b_re
    ct_re, ct_im = c_re.transpose(0, 2, 1)[:, :, None, :], c_im.transpose(0, 2, 1)[:, :, None, :]
    pt_re, pt_im = pw_re.transpose(0, 2, 1)[:, :, :, None], pw_im.transpose(0, 2, 1)[:, :, :, None]
    cp_re = ct_re * pt_re - ct_im * pt_im
    cp_im = ct_re * pt_im + ct_im * pt_re
    hp = lax.Precision.HIGHEST
    kmat = (jnp.einsum('gpk,gpn->gkn', bb_re, cp_re[:, :, :c].reshape(gg, pp, c * hh), precision=hp)
            - jnp.einsum('gpk,gpn->gkn', bb_im, cp_im[:, :, :c].reshape(gg, pp, c * hh), precision=hp))
    rows = [jnp.pad(kmat[:, :, :(c - j) * hh], ((0, 0), (0, 0), (j * hh, 0))) for j in range(c)]
    a1 = jnp.stack(rows, axis=1).reshape(gg, c * hh, c * hh)
    a1 = a1 + jnp.eye(c * hh, dtype=F32)[None] * jnp.tile(d, (1, c))[:, None, :]
    a2 = jnp.concatenate([cp_re[:, :, 1:].reshape(gg, pp, c * hh), -cp_im[:, :, 1:].reshape(gg, pp, c * hh)], axis=1)
    rv_re, rv_im = pw_re[:, :c][:, ::-1, None, :], pw_im[:, :c][:, ::-1, None, :]
    bt_re, bt_im = bb_re.transpose(0, 2, 1)[:, None], bb_im.transpose(0, 2, 1)[:, None]
    a3 = jnp.concatenate([rv_re * bt_re - rv_im * bt_im, rv_re * bt_im + rv_im * bt_re], axis=-1)
    a3 = a3.reshape(gg, c * hh, 2 * pp)
    are = jnp.concatenate([pw_re[:, c], pw_re[:, c]], axis=-1)[:, None, :]
    aim = jnp.concatenate([-pw_im[:, c], pw_im[:, c]], axis=-1)[:, None, :]
    return a1, a2, a3, are, aim


S5_LEVELS = 8


def s5_scan_powers(lam_re, lam_im, log_dt):
    dt = jnp.exp(log_dt)[:, None]
    e = (S5_CHUNK * 2.0 ** jnp.arange(S5_LEVELS, dtype=F32))[:, None, None]
    mag = jnp.exp(e * (lam_re * dt)[None])
    ang = e * (lam_im * dt)[None]
    pr, pi = mag * jnp.cos(ang), mag * jnp.sin(ang)
    pre = jnp.concatenate([pr, pr], axis=-1).transpose(1, 0, 2)
    pim = jnp.concatenate([-pi, pi], axis=-1).transpose(1, 0, 2)
    return pre, pim


def _s5_scan(x, pre, pim, reverse):
    n = x.shape[0]
    rowi = lax.broadcasted_iota(jnp.int32, (n, 1), 0)
    t = x
    for k in range(S5_LEVELS):
        shift = 2 ** k
        if shift >= n:
            break
        p_re, p_im = pre[k:k + 1, :], pim[k:k + 1, :]
        if reverse:
            far = jnp.where(rowi < n - shift, pltpu.roll(t, n - shift, 0), 0.0)
            t = t + (p_re * far + pltpu.roll(p_im * far, S5_STATE, 1))
        else:
            far = jnp.where(rowi >= shift, pltpu.roll(t, shift, 0), 0.0)
            t = t + (p_re * far + p_im * pltpu.roll(far, S5_STATE, 1))
    return t


def s5_fwd(ug, a1, a2, a3, pre, pim, nseq):
    g, nc, cw = ug.shape
    per = nc // nseq
    sw = 2 * S5_STATE
    assert per <= 2 ** S5_LEVELS

    def body(u_ref, a1_ref, a2_ref, a3_ref, pre_ref, pim_ref, y_ref, s_ref):
        u = u_ref[...]
        x = _dot3(u, a3_ref[...], 1, 0)
        first = lax.broadcasted_iota(jnp.int32, (per, 1), 0) == 0
        for b in range(nseq):
            t = _s5_scan(x[b * per:(b + 1) * per], pre_ref[...], pim_ref[...], False)
            s_ref[pl.ds(b * per, per), :] = jnp.where(first, 0.0, pltpu.roll(t, 1, 0))
        y_ref[...] = _dot3(u, a1_ref[...], 1, 0) + _dot3(s_ref[...], a2_ref[...], 1, 0)

    def spec(shape):
        return pl.BlockSpec((None,) + shape, lambda i: (i, 0, 0))
    lv = (S5_LEVELS, sw)
    return pl.pallas_call(
        body, name="s5_fwd", grid=(g,),
        in_specs=[spec((nc, cw)), spec((cw, cw)), spec((sw, cw)), spec((cw, sw)), spec(lv), spec(lv)],
        out_specs=[spec((nc, cw)), spec((nc, sw))],
        out_shape=[jax.ShapeDtypeStruct((g, nc, cw), F32), jax.ShapeDtypeStruct((g, nc, sw), F32)],
        compiler_params=_params(("parallel",)))(ug, a1, a2, a3, pre, pim)


def s5_bwd(dyg, ug, sst, a1, a2, a3, pre, pim, nseq, ex=None):
    g, nc, cw = ug.shape
    per = nc // nseq
    sw = 2 * S5_STATE
    split, ex_start, ex_finish, exkw = _hidden_exchange(ex, 8, 6, (g,))

    def body(*refs):
        ((dy_ref, u_ref, s_ref, a1_ref, a2_ref, a3_ref, pre_ref, pim_ref),
         (du_ref, da1_ref, da2_ref, da3_ref, dare_ref, daim_ref), ex_parts, (dx_ref,)) = split(refs)
        ex_start(ex_parts)
        dy = dy_ref[...]
        u = u_ref[...]
        gd = _dot3(dy, a2_ref[...], 1, 1)
        s_all = s_ref[...]
        da2_ref[...] = _dot3(s_all, dy, 0, 0)
        last = lax.broadcasted_iota(jnp.int32, (per, 1), 0) == per - 1
        for b in range(nseq):
            gs = _s5_scan(gd[b * per:(b + 1) * per], pre_ref[...], pim_ref[...], True)
            dx_ref[pl.ds(b * per, per), :] = jnp.where(last, 0.0, pltpu.roll(gs, per - 1, 0))
        dx = dx_ref[...]
        dare_ref[...] = _colsum(dx * s_all)
        daim_ref[...] = _colsum(dx * pltpu.roll(s_all, S5_STATE, 1))
        du_ref[...] = _dot3(dy, a1_ref[...], 1, 1) + _dot3(dx, a3_ref[...], 1, 1)
        da1_ref[...] = _dot3(u, dy, 0, 0)
        da3_ref[...] = _dot3(u, dx, 0, 0)
        ex_finish(ex_parts)

    def spec(shape):
        return pl.BlockSpec((None,) + shape, lambda i: (i, 0, 0))
    sds = jax.ShapeDtypeStruct
    return pl.pallas_call(
        body, name="s5_bwd", grid=(g,),
        in_specs=[spec((nc, cw)), spec((nc, cw)), spec((nc, sw)), spec((cw, cw)), spec((sw, cw)), spec((cw, sw)),
                  spec((S5_LEVELS, sw)), spec((S5_LEVELS, sw))] + exkw['in_specs'],
        out_specs=[spec((nc, cw)), spec((cw, cw)), spec((sw, cw)), spec((cw, sw)), spec((1, sw)), spec((1, sw))]
        + exkw['out_specs'],
        out_shape=[sds((g, nc, cw), F32), sds((g, cw, cw), F32), sds((g, sw, cw), F32), sds((g, cw, sw), F32),
                   sds((g, 1, sw), F32), sds((g, 1, sw), F32)] + exkw['out_shape'],
        input_output_aliases=exkw['aliases'], scratch_shapes=[pltpu.VMEM((nc, sw), F32)] + exkw['scratch'],
        compiler_params=_params(("arbitrary",)))(dyg, ug, sst, a1, a2, a3, pre, pim, *exkw['ins'])


def _to_groups(u):
    t = u.shape[0]
    nc = t // S5_CHUNK
    return u.reshape(nc, S5_CHUNK, S5_GROUPS, S5_GROUP).transpose(2, 0, 1, 3).reshape(S5_GROUPS, nc, -1)


def _from_groups(yg):
    g, nc, _ = yg.shape
    return yg.reshape(g, nc, S5_CHUNK, S5_GROUP).transpose(1, 2, 0, 3).reshape(nc * S5_CHUNK, g * S5_GROUP)


def _sb_masks():
    row = lax.broadcasted_iota(jnp.int32, (SB_BLOCK, SB_BLOCK), 0)
    col = lax.broadcasted_iota(jnp.int32, (SB_BLOCK, SB_BLOCK), 1)
    return row, col


def _sb_weights(z, vis, later_of, after):
    sp = jnp.maximum(z, 0.0) + jnp.log1p(jnp.exp(-jnp.abs(z)))
    lk = jnp.where(vis, -sp, 0.0)
    rs = jnp.sum(lk, axis=1, keepdims=True)
    later = _dot_exact_rhs(lk, after, 1, 0) + later_of(rs)
    lb = z - sp
    w = jnp.where(vis, jnp.exp(lb + later), 0.0)
    return w, rs, lb


def _hidden_exchange(ex, n_in, n_out, grid):
    ni, no = (len(ex.ins), len(ex.out_shapes)) if ex else (0, 0)

    def split(refs):
        a, b = n_in + ni, n_in + ni + n_out
        end = len(refs) - (2 if ex else 0)
        return refs[:n_in], refs[a:b], (refs[n_in:a], refs[b:b + no], refs[end:]), refs[b + no:end]

    def at_step(which, fn, parts):
        if ex is not None:
            ids = [pl.program_id(d) == (0 if which == 'first' else g - 1) for d, g in enumerate(grid)]
            cond = ids[0]
            for c in ids[1:]:
                cond = jnp.logical_and(cond, c)
            pl.when(cond)(lambda: fn(parts[0], parts[1], *parts[2]))

    anyspec = pl.BlockSpec(memory_space=pl.ANY)
    kw = dict(in_specs=[anyspec] * ni, out_specs=[anyspec] * no, out_shape=list(ex.out_shapes) if ex else [],
              aliases={n_in + i: n_out + j for i, j in (ex.aliases().items() if ex else ())},
              scratch=ex.sems() if ex else [], ins=list(ex.ins) if ex else [])
    start = functools.partial(at_step, 'first', ex.start if ex else None)
    finish = functools.partial(at_step, 'last', ex.finish if ex else None)
    return split, start, finish, kw


SB_TILES = 2
SB_HEADS = 2 * SB_TILES
SB_WD = SB_TILES * LANES


def _stack_heads(x):
    return jnp.concatenate([x] * SB_HEADS, axis=0)


def _stack_masks():
    row = lax.broadcasted_iota(jnp.int32, (SB_HEADS * SB_BLOCK, SB_WD), 0)
    col = lax.broadcasted_iota(jnp.int32, (SB_HEADS * SB_BLOCK, SB_WD), 1)
    rowk = lax.broadcasted_iota(jnp.int32, (SB_HEADS * SB_BLOCK, SB_BLOCK), 0)
    colk = lax.broadcasted_iota(jnp.int32, (SB_HEADS * SB_BLOCK, SB_BLOCK), 1)
    return (col // 64) == (row // SB_BLOCK), colk < (rowk % SB_BLOCK)


def _own_lanes(r):
    head = lax.broadcasted_iota(jnp.int32, (SB_BLOCK, SB_WD), 1) // 64
    out = r[:SB_BLOCK]
    for h in range(1, SB_HEADS):
        out = jnp.where(head == h, r[h * SB_BLOCK:(h + 1) * SB_BLOCK], out)
    return out


def _sb_specs(lp, nseq):
    wd = SB_TILES * LANES
    off = [(S5_WIDTH + i * SB_WIDTH) // wd for i in range(3)]
    blk = (lp, wd)
    qkv = [pl.BlockSpec(blk, functools.partial(lambda b, h, o: (b, o + h), o=o)) for o in off]
    return (blk, qkv, pl.BlockSpec(blk, lambda b, h: (b, h)),
            pl.BlockSpec((None, None, 8, LANES), lambda b, h: (b, h, 0, 0)))


def attn_fwd(proj, nseq, lp, ex=None):
    nqb = lp // SB_BLOCK
    t = proj.shape[0]
    nstep = SB_WIDTH // (SB_TILES * LANES)
    split, ex_start, ex_finish, exkw = _hidden_exchange(ex, 3, 3, (nseq, nstep))

    def body(*refs):
        (q_ref, k_ref, v_ref), (o_ref, tot_ref, first_ref), ex_parts, _ = split(refs)
        ex_start(ex_parts)
        row, col = _sb_masks()
        after = (row > col).astype(BF16)
        lane8 = lax.broadcasted_iota(jnp.int32, (8, LANES), 1)
        hms, tris = _stack_masks()

        def qloop(qi, firsts):
            q0 = pl.multiple_of(qi * SB_BLOCK, SB_BLOCK)
            qm = jnp.where(hms, _stack_heads(q_ref[pl.ds(q0, SB_BLOCK), :] * 0.125), 0.0).astype(BF16)

            def alive(carry):
                return jnp.logical_and(carry[0] <= qi, carry[1] > 0)

            def kstep(carry):
                s, _, acc, lat = carry
                kj = qi - s
                k0 = pl.multiple_of(kj * SB_BLOCK, SB_BLOCK)
                k = k_ref[pl.ds(k0, SB_BLOCK), :].astype(BF16)
                v = v_ref[pl.ds(k0, SB_BLOCK), :].astype(BF16)
                vis = jnp.logical_or(kj < qi, tris)
                w, rs, _ = _sb_weights(_dot(qm, k, 1, 1), vis, functools.partial(lambda rs, lat: lat, lat=lat), after)
                acc = acc + _own_lanes(_dot(w.astype(BF16), v, 1, 0))
                lat = lat + rs
                return s + 1, (jnp.max(lat) > SB_DEAD).astype(jnp.int32), acc, lat

            res = lax.while_loop(alive, kstep, (jnp.int32(0), jnp.int32(1), jnp.zeros((SB_BLOCK, SB_WD), F32),
                                                jnp.zeros((SB_HEADS * SB_BLOCK, 1), F32)))
            o_ref[pl.ds(q0, SB_BLOCK), :] = res[2].astype(o_ref.dtype)
            tot_ref[pl.ds(q0, SB_BLOCK), :] = _own_lanes(jnp.broadcast_to(res[3], (SB_HEADS * SB_BLOCK, SB_WD)))
            return jnp.where(lane8 == qi, (qi - res[0] + 1).astype(F32), firsts)

        first_ref[...] = lax.fori_loop(0, nqb, qloop, jnp.zeros((8, LANES), F32))
        ex_finish(ex_parts)

    blk, qkv, out, vec = _sb_specs(lp, nseq)
    return pl.pallas_call(
        body, name="attn_fwd", grid=(nseq, nstep),
        in_specs=qkv + exkw['in_specs'], out_specs=[out, out, vec] + exkw['out_specs'],
        out_shape=[jax.ShapeDtypeStruct((t, SB_WIDTH), BF16), jax.ShapeDtypeStruct((t, SB_WIDTH), F32)]
        + [jax.ShapeDtypeStruct((nseq, nstep, 8, LANES), F32)] + exkw['out_shape'],
        input_output_aliases=exkw['aliases'], scratch_shapes=exkw['scratch'],
        compiler_params=_params(("arbitrary", "arbitrary")))(proj, proj, proj, *exkw['ins'])


def attn_bwd(proj, tot, first, do, nseq, lp, ex=None):
    nqb = lp // SB_BLOCK
    t = proj.shape[0]
    nstep = SB_WIDTH // (SB_TILES * LANES)
    split, ex_start, ex_finish, exkw = _hidden_exchange(ex, 6, 3, (nseq, nstep))

    def body(*refs):
        (q_ref, k_ref, v_ref, tot_ref, first_ref, do_ref), (dq_ref, dk_ref, dv_ref), ex_parts, _ = split(refs)
        ex_start(ex_parts)
        row, col = _sb_masks()
        after = (row > col).astype(BF16)
        before = (row < col).astype(BF16)
        lane8 = lax.broadcasted_iota(jnp.int32, (8, LANES), 1)
        firsts = first_ref[...]
        dk_ref[...] = jnp.zeros(dk_ref.shape, F32)
        dv_ref[...] = jnp.zeros(dv_ref.shape, F32)
        hms, tris = _stack_masks()

        def qloop(qi, _):
            first = jnp.max(jnp.where(lane8 == qi, firsts, 0.0)).astype(jnp.int32)
            first = jnp.clip(first, 0, qi)
            q0 = pl.multiple_of(qi * SB_BLOCK, SB_BLOCK)
            qm = jnp.where(hms, _stack_heads(q_ref[pl.ds(q0, SB_BLOCK), :] * 0.125), 0.0).astype(BF16)
            dom = jnp.where(hms, _stack_heads(do_ref[pl.ds(q0, SB_BLOCK), :]), 0.0).astype(BF16)
            tot = jnp.max(jnp.where(hms, _stack_heads(tot_ref[pl.ds(q0, SB_BLOCK), :]), -jnp.inf),
                          axis=1, keepdims=True)

            def kloop(kj, carry):
                dq, pre, gpre = carry
                k0 = pl.multiple_of(kj * SB_BLOCK, SB_BLOCK)
                k = k_ref[pl.ds(k0, SB_BLOCK), :].astype(BF16)
                v = v_ref[pl.ds(k0, SB_BLOCK), :].astype(BF16)
                vis = jnp.logical_or(kj < qi, tris)
                later_of = functools.partial(lambda rs, t_, p_: t_ - p_ - rs, t_=tot, p_=pre)
                w, rs, lb = _sb_weights(_dot(qm, k, 1, 1), vis, later_of, after)
                g = _dot(dom, v, 1, 1) * w
                dlk = _dot_exact_rhs(g, before, 1, 0) + gpre
                sig = jnp.exp(lb)
                dz = jnp.where(vis, g * (1.0 - sig) - dlk * sig, 0.0).astype(BF16)
                dk_ref[pl.ds(k0, SB_BLOCK), :] += _dot(dz, qm, 0, 0)
                dv_ref[pl.ds(k0, SB_BLOCK), :] += _dot(w.astype(BF16), dom, 0, 0)
                return dq + _own_lanes(_dot(dz, k, 1, 0)), pre + rs, gpre + jnp.sum(g, axis=1, keepdims=True)

            z1 = jnp.zeros((SB_HEADS * SB_BLOCK, 1), F32)
            res = lax.fori_loop(first, qi + 1, kloop, (jnp.zeros((SB_BLOCK, SB_WD), F32), z1, z1))
            dq_ref[pl.ds(q0, SB_BLOCK), :] = res[0] * 0.125
            return 0

        lax.fori_loop(0, nqb, qloop, 0)
        ex_finish(ex_parts)

    blk, qkv, out, vec = _sb_specs(lp, nseq)
    return pl.pallas_call(
        body, name="attn_bwd", grid=(nseq, nstep),
        in_specs=qkv + [out, vec, out] + exkw['in_specs'], out_specs=[out] * 3 + exkw['out_specs'],
        out_shape=[jax.ShapeDtypeStruct((t, SB_WIDTH), F32)] * 3 + exkw['out_shape'],
        input_output_aliases=exkw['aliases'], scratch_shapes=exkw['scratch'],
        compiler_params=_params(("arbitrary", "arbitrary")))(proj, proj, proj, tot, first, do, *exkw['ins'])


def _hg_loop(nch, step, init):
    assert nch % HG_UNROLL == 0

    def trip(i, carry):
        for u in range(HG_UNROLL):
            carry = step(i * HG_UNROLL + u, carry)
        return carry
    return lax.fori_loop(0, nch // HG_UNROLL, trip, init)


def _hg_gates(fc, lb):
    sg = jax.nn.sigmoid(fc)
    f = lb + (1.0 - lb) * sg
    return sg, f


def _dot_exact_lhs(m, x):
    hi, mid, lo = _split3(x)
    return _dot(m, hi, 1, 0) + (_dot(m, mid, 1, 0) + _dot(m, lo, 1, 0))


def hgrn_fwd(proj, lb, ng, nseq, lp, ex=None):
    nch = lp // HG_CHUNK
    t = proj.shape[0]
    c = HG_CHUNK

    split, ex_start, ex_finish, exkw = _hidden_exchange(ex, 6, 3, (HG_HEADS,))

    def body(*refs):
        (q_ref, f_ref, v_ref, g_ref, lb_ref, ng_ref), (o_ref, og_ref, st_ref), ex_parts, _ = split(refs)
        ex_start(ex_parts)
        lbv, ngv = lb_ref[...], ng_ref[...]

        rr = nseq * c
        seq_r = lax.broadcasted_iota(jnp.int32, (rr, 1), 0) // c
        bd_row = lax.broadcasted_iota(jnp.int32, (rr, rr), 0)
        bd_col = lax.broadcasted_iota(jnp.int32, (rr, rr), 1)
        causal = jnp.logical_and(bd_row // c == bd_col // c, bd_col <= bd_row)
        incl = causal.astype(BF16)

        def stacked(ref, ci):
            return jnp.concatenate([ref[pl.ds(pl.multiple_of(b * lp + ci * c, c), c), :] for b in range(nseq)], axis=0)

        def wide(x):
            return jnp.concatenate([jnp.where(seq_r == b, x, jnp.zeros_like(x)) for b in range(nseq)], axis=1)

        def step(ci, st):
            for b in range(nseq):
                st_ref[b, ci] = st[:, b * HG_DK:(b + 1) * HG_DK]
            _, f = _hg_gates(stacked(f_ref, ci), lbv)
            bcum = _dot_exact_lhs(incl, jnp.log(f))
            lasts = [bcum[b * c + c - 1:b * c + c, :] for b in range(nseq)]
            blast = jnp.concatenate([jnp.broadcast_to(x, (c, LANES)) for x in lasts], axis=0)
            kk = 1.0 - f
            vc = stacked(v_ref, ci).astype(BF16)
            qd = (stacked(q_ref, ci) * jnp.exp(bcum)).astype(BF16)
            kd = (kk * jnp.exp(-bcum)).astype(BF16)
            a = jnp.where(causal, _dot(qd, kd, 1, 1), 0.0)
            o = _dot(a.astype(BF16), vc, 1, 0) + _dot(wide(qd), st.astype(BF16), 1, 1)
            kend = (kk * jnp.exp(blast - bcum)).astype(BF16)
            st = st * jnp.exp(jnp.concatenate(lasts, axis=1)) + _dot(vc, wide(kend), 0, 0)
            r = lax.rsqrt(jnp.mean(o * o, axis=1, keepdims=True) + RMS_EPS)
            gc = stacked(g_ref, ci)
            og = (o * r * ngv * (gc * jax.nn.sigmoid(gc))).astype(og_ref.dtype)
            for b in range(nseq):
                rows = pl.ds(pl.multiple_of(b * lp + ci * c, c), c)
                o_ref[rows, :] = o[b * c:(b + 1) * c]
                og_ref[rows, :] = og[b * c:(b + 1) * c]
            return st

        _hg_loop(nch, step, jnp.zeros((HG_DK, nseq * HG_DK), F32))
        ex_finish(ex_parts)

    blk = (nseq * lp, LANES)
    h = HG_HEADS
    return pl.pallas_call(
        body, name="hgrn_fwd", grid=(h,),
        in_specs=[pl.BlockSpec(blk, lambda hh: (0, hh)),
                  pl.BlockSpec(blk, lambda hh: (0, h + hh)),
                  pl.BlockSpec(blk, lambda hh: (0, 2 * h + hh)),
                  pl.BlockSpec(blk, lambda hh: (0, 3 * h + hh)),
                  pl.BlockSpec((1, LANES), lambda hh: (0, hh)),
                  pl.BlockSpec((1, LANES), lambda hh: (0, hh))] + exkw['in_specs'],
        out_specs=[pl.BlockSpec(blk, lambda hh: (0, hh)),
                   pl.BlockSpec(blk, lambda hh: (0, hh)),
                   pl.BlockSpec((nseq, None, nch, HG_DK, HG_DK), lambda hh: (0, hh, 0, 0, 0))] + exkw['out_specs'],
        out_shape=[jax.ShapeDtypeStruct((t, D_MODEL), F32), jax.ShapeDtypeStruct((t, D_MODEL), BF16),
                   jax.ShapeDtypeStruct((nseq, h, nch, HG_DK, HG_DK), F32)] + exkw['out_shape'],
        input_output_aliases=exkw['aliases'], scratch_shapes=exkw['scratch'],
        compiler_params=_params(("arbitrary",)))(proj, proj, proj, proj, lb, ng, *exkw['ins'])


def hgrn_bwd(proj, o, dog, states, lb, ng, nseq, lp):
    nch = lp // HG_CHUNK
    t = proj.shape[0]
    c = HG_CHUNK
    rr = nseq * c

    def body(q_ref, f_ref, v_ref, g_ref, o_ref, dog_ref, st_ref, lb_ref, ng_ref,
             dq_ref, df_ref, dv_ref, dg_ref, dlb_ref, dng_ref):
        seq_r = lax.broadcasted_iota(jnp.int32, (rr, 1), 0) // c
        last = lax.broadcasted_iota(jnp.int32, (rr, 1), 0) % c == c - 1
        bd_row = lax.broadcasted_iota(jnp.int32, (rr, rr), 0)
        bd_col = lax.broadcasted_iota(jnp.int32, (rr, rr), 1)
        same = bd_row // c == bd_col // c
        causal = jnp.logical_and(same, bd_col <= bd_row)
        incl = causal.astype(BF16)
        from_here = jnp.logical_and(same, bd_col >= bd_row).astype(BF16)
        lbv, ngv = lb_ref[...], ng_ref[...]

        def stacked(ref, ci):
            return jnp.concatenate([ref[pl.ds(pl.multiple_of(b * lp + ci * c, c), c), :] for b in range(nseq)], axis=0)

        def wide(x):
            return jnp.concatenate([jnp.where(seq_r == b, x, jnp.zeros_like(x)) for b in range(nseq)], axis=1)

        def own(x):
            return functools.reduce(jnp.add, [jnp.where(seq_r == b, x[:, b * LANES:(b + 1) * LANES], 0.0)
                                              for b in range(nseq)])

        def per_seq_rows(vals):
            return jnp.concatenate([jnp.broadcast_to(x, (c, LANES)) for x in vals], axis=0)

        def step(s, carry):
            dst, dlb, dng = carry
            ci = nch - 1 - s
            oc, gc, dogc = stacked(o_ref, ci), stacked(g_ref, ci), stacked(dog_ref, ci)
            r = lax.rsqrt(jnp.mean(oc * oc, axis=1, keepdims=True) + RMS_EPS)
            sgg = jax.nn.sigmoid(gc)
            dgc = dogc * (oc * r * ngv) * (sgg * (1.0 + gc * (1.0 - sgg)))
            don = dogc * (gc * sgg)
            dng = dng + _colsum(don * oc * r)
            tt = don * ngv
            do = r * (tt - oc * (r * r) * jnp.mean(tt * oc, axis=1, keepdims=True))
            st = jnp.concatenate([st_ref[b, ci] for b in range(nseq)], axis=1)
            sg, f = _hg_gates(stacked(f_ref, ci), lbv)
            bcum = _dot_exact_lhs(incl, jnp.log(f))
            lasts = [bcum[b * c + c - 1:b * c + c, :] for b in range(nseq)]
            blast = per_seq_rows(lasts)
            kk = 1.0 - f
            qc, vf = stacked(q_ref, ci), stacked(v_ref, ci)
            pdec = jnp.exp(bcum)
            ndec = jnp.exp(-bcum)
            edec = jnp.exp(blast - bcum)
            eb = jnp.exp(jnp.concatenate(lasts, axis=1))
            qd_f, kd_f, kend_f = qc * pdec, kk * ndec, kk * edec
            qd, kd, kend = qd_f.astype(BF16), kd_f.astype(BF16), kend_f.astype(BF16)
            vc, dob, stb, dstb = vf.astype(BF16), do.astype(BF16), st.astype(BF16), dst.astype(BF16)
            a = jnp.where(causal, _dot(qd, kd, 1, 1), 0.0).astype(BF16)
            da = jnp.where(causal, _dot(dob, vc, 1, 1), 0.0).astype(BF16)
            dv = _dot(a, dob, 0, 0) + _dot(wide(kend), dstb, 1, 1)
            dqd = _dot(da, kd, 1, 0) + own(_dot(dob, stb, 1, 0))
            dkd = _dot(da, qd, 0, 0)
            dkend = own(_dot(vc, dstb, 1, 0))
            sts = _colsum(dst * st)
            dkk = dkend * kend_f
            dblast = per_seq_rows([_colsum(jnp.where(seq_r == b, dkk, 0.0))
                                   + eb[:, b * LANES:(b + 1) * LANES] * sts[:, b * LANES:(b + 1) * LANES]
                                   for b in range(nseq)])
            dbcum = dqd * qd_f - dkd * kd_f - dkk
            dbcum = dbcum + jnp.where(last, dblast, 0.0)
            dlf = _dot_exact_lhs(from_here, dbcum)
            df = dlf / f - (dkd * ndec + dkend * edec)
            dfp = df * (1.0 - lbv) * sg * (1.0 - sg)
            dqc = dqd * pdec
            for b in range(nseq):
                rows = pl.ds(pl.multiple_of(b * lp + ci * c, c), c)
                blk_rows = slice(b * c, (b + 1) * c)
                dg_ref[rows, :], dv_ref[rows, :] = dgc[blk_rows], dv[blk_rows]
                dq_ref[rows, :], df_ref[rows, :] = dqc[blk_rows], dfp[blk_rows]
            dlb = dlb + _colsum(df * (1.0 - sg))
            dst = dst * eb + _dot(dob, wide(qd), 0, 0)
            return dst, dlb, dng

        z = jnp.zeros((1, LANES), F32)
        _, dlb, dng = _hg_loop(nch, step, (jnp.zeros((HG_DK, nseq * HG_DK), F32), z, z))
        dlb_ref[...] = dlb
        dng_ref[...] = dng

    blk = (nseq * lp, LANES)
    h = HG_HEADS
    vec = pl.BlockSpec((1, LANES), lambda hh: (0, hh))
    col = pl.BlockSpec(blk, lambda hh: (0, hh))
    return pl.pallas_call(
        body, name="hgrn_bwd", grid=(h,),
        in_specs=[col,
                  pl.BlockSpec(blk, lambda hh: (0, h + hh)),
                  pl.BlockSpec(blk, lambda hh: (0, 2 * h + hh)),
                  pl.BlockSpec(blk, lambda hh: (0, 3 * h + hh)),
                  col, col,
                  pl.BlockSpec((nseq, None, nch, HG_DK, HG_DK), lambda hh: (0, hh, 0, 0, 0)),
                  vec, vec],
        out_specs=[col] * 4 + [vec, vec],
        out_shape=[jax.ShapeDtypeStruct((t, D_MODEL), F32)] * 4 + [jax.ShapeDtypeStruct((1, D_MODEL), F32)] * 2,
        compiler_params=pltpu.CompilerParams(dimension_semantics=("parallel",), vmem_limit_bytes=HGRN_BWD_VMEM))(
            proj, proj, proj, proj, o, dog, states, lb, ng)


def _lower_bound(gamma):
    p = jax.nn.softmax(gamma, axis=0)
    return (jnp.cumsum(p, axis=0) - p[0])[1][None, :]


def local_step(x, tgt, w, hooks=None):
    nseq, seq, _ = x.shape
    lp = -(-(N_META + seq) // SB_BLOCK) * SB_BLOCK
    t = nseq * lp
    pad = lp - N_META - seq
    h0 = jnp.concatenate([jnp.broadcast_to(w['meta'][None], (nseq, N_META, D_MODEL)), x,
                          jnp.zeros((nseq, pad, D_MODEL), F32)], axis=1).reshape(t, D_MODEL)
    tgt_p = jnp.pad(tgt, ((0, 0), (N_META, pad), (0, 0))).reshape(t, D_MODEL)
    row = lambda v: v.reshape(1, -1)
    g = {}

    proj = matmul("in_ab", h0, w['w_in_ab'], 'nn')
    att = attn_fwd(proj, nseq, lp, ex=hooks.gather_late if hooks else None)
    b_out, sb_tot, sb_first = att[:3]
    if hooks:
        w = {**w, **hooks.late_weights(att[3:])}
    s5_names = ['s5_lam_re', 's5_lam_im', 's5_log_dt', 's5_b_re', 's5_b_im', 's5_c_re', 's5_c_im', 's5_d']
    mats, mats_vjp = jax.vjp(s5_matrices, *[w[n][0] for n in s5_names])
    ug = _to_groups(proj[:, :S5_WIDTH])
    s5_pows = s5_scan_powers(*[w[n][0] for n in s5_names[:3]])
    yg, sst = s5_fwd(ug, *mats[:3], *s5_pows, nseq)
    ys5 = _from_groups(yg)
    y = gelu_fwd("gelu", ys5)
    gp = matmul("glu", y, w['s5_w_glu'], 'nn', bias=w['s5_b_glu'])
    a_out = glu_gate("glu_gate", y, gp)
    cat = jnp.concatenate([a_out, b_out], axis=1)
    mix0 = matmul("out_ab", cat, w['w_out_ab'], 'nn')
    h1, xh1, rs1 = ln_fwd("ln_mix0", h0, mix0, row(w['ln_mix_g'][0]), row(w['ln_mix_b'][0]))
    hid0 = matmul("up0", h1, w['mlp_w_up'][0], 'nn', bias=row(w['mlp_b_up'][0]), act='relu2', out_dtype=BF16)
    dn0 = matmul("down0", hid0, w['mlp_w_down'][0], 'nn', bias=row(w['mlp_b_down'][0]))
    h2, xh2, rs2 = ln_fwd("ln_mlp0", h1, dn0, row(w['ln_mlp_g'][0]), row(w['ln_mlp_b'][0]))

    projc = matmul("in_c", h2, w['w_in_c'], 'nn')
    lb, lb_vjp = jax.vjp(_lower_bound, w['hgrn_gamma'])
    ng = w['hgrn_norm_g']
    res = hgrn_fwd(projc, lb, ng, nseq, lp, ex=hooks.gather_last if hooks else None)
    o, og, states = res[:3]
    if hooks:
        w = {**w, **hooks.last_weights(res[3:])}
    mix1 = matmul("out_c", og, w['w_out_c'], 'nn')
    h3, xh3, rs3 = ln_fwd("ln_mix1", h2, mix1, row(w['ln_mix_g'][1]), row(w['ln_mix_b'][1]))
    hid1 = matmul("up1", h3, w['mlp_w_up'][1], 'nn', bias=row(w['mlp_b_up'][1]), act='relu2', out_dtype=BF16)
    dn1 = matmul("down1", hid1, w['mlp_w_down'][1], 'nn', bias=row(w['mlp_b_down'][1]))
    last_ln = ln_loss("ln_mlp1_loss", h3, dn1, row(w['ln_mlp_g'][1]), row(w['ln_mlp_b'][1]), tgt_p, lp, seq)
    loss = last_ln[4]

    def mlp_bwd(l, ln_back, hid, h_in):
        dpre, dg_, db_, dbd = ln_back
        dpu = matmul(f"down{l}_dx", dpre, w['mlp_w_down'][l], 'nt', relu2_of=hid, out_dtype=BF16)
        dwd = matmul(f"down{l}_dw", hid, dpre, 'tn')
        dh_in = matmul(f"up{l}_dx", dpu, w['mlp_w_up'][l], 'nt', add=dpre, add_scale=ALPHA)
        dwu, dbu = matmul(f"up{l}_dw", h_in, dpu, 'tn', out_slots=N_CHIPS, colsum=True)
        return dh_in, dict(ln_mlp_g=dg_, ln_mlp_b=db_, mlp_b_down=dbd, mlp_b_up=dbu, mlp_w_up=dwu, mlp_w_down=dwd)

    dh3, gm1 = mlp_bwd(1, last_ln[:4], hid1, h3)
    dpre, dg1, db1, _ = ln_bwd("ln_mix1_b", dh3, xh3, rs3, row(w['ln_mix_g'][1]))
    g['w_out_c'] = matmul("out_c_dw", og, dpre, 'tn')
    dog = matmul("out_c_dx", dpre, w['w_out_c'], 'nt')
    dq, df, di, dgg, dlb, dng = hgrn_bwd(projc, o, dog, states, lb, ng, nseq, lp)
    dprojc = jnp.concatenate([dq, df, di, dgg], axis=1)
    dh2 = matmul("in_c_dx", dprojc, w['w_in_c'], 'nt', add=dpre, add_scale=ALPHA)
    g['w_in_c'] = matmul("in_c_dw", h2, dprojc, 'tn', out_slots=N_CHIPS)
    g['hgrn_gamma'] = lb_vjp(dlb)[0]
    g['hgrn_norm_g'] = dng

    dh1, gm0 = mlp_bwd(0, ln_bwd("ln_mlp0_b", dh2, xh2, rs2, row(w['ln_mlp_g'][0])), hid0, h1)
    dpre, dg0, db0, _ = ln_bwd("ln_mix0_b", dh1, xh1, rs1, row(w['ln_mix_g'][0]))
    g['w_out_ab'] = matmul("out_ab_dw", cat, dpre, 'tn')
    dcat = matmul("out_ab_dx", dpre, w['w_out_ab'], 'nt')
    dgp, da_s, dbglu = glu_gate_bwd("glu_gate_b", dcat[:, :S5_WIDTH], y, gp)
    g['s5_w_glu'] = matmul("glu_dw", y, dgp, 'tn')
    g['s5_b_glu'] = dbglu
    dy5 = matmul("glu_dx", dgp, w['s5_w_glu'], 'nt', add=da_s)
    dys5 = gelu_bwd("gelu_b", dy5, ys5)
    for n in ('mlp_w_up', 'mlp_w_down'):
        g[n] = [gm0[n], gm1[n]]
    g['ln_mix_g'] = jnp.concatenate([dg0, dg1], axis=0)
    g['ln_mix_b'] = jnp.concatenate([db0, db1], axis=0)
    for n in ('ln_mlp_g', 'ln_mlp_b', 'mlp_b_down', 'mlp_b_up'):
        g[n] = jnp.concatenate([gm0[n], gm1[n]], axis=0)
    res = s5_bwd(_to_groups(dys5), ug, sst, *mats[:3], *s5_pows, nseq, ex=hooks.fold_early(g) if hooks else None)
    dug, da1, da2, da3, dare, daim = res[:6]
    for n, v in zip(s5_names, mats_vjp((da1, da2, da3, dare, daim))):
        g[n] = v[None]
    ex = hooks.scatter_early(g, loss, res[6:]) if hooks else None
    res = attn_bwd(proj, sb_tot, sb_first, dcat[:, S5_WIDTH:], nseq, lp, ex=ex)
    dqa, dka, dva = res[:3]
    if hooks:
        hooks.scattered(res[3:])
    dproj = jnp.concatenate([_from_groups(dug), dqa, dka, dva], axis=1)
    dh0 = matmul("in_ab_dx", dproj, w['w_in_ab'], 'nt', add=dpre, add_scale=ALPHA,
                 ex=hooks.gather_small if hooks else None)
    if hooks:
        dh0 = hooks.small_gathered(dh0)
    g['w_in_ab'] = matmul("in_ab_dw", h0, dproj, 'tn', out_slots=N_CHIPS)

    dh0 = dh0.reshape(nseq, lp, D_MODEL)
    grad_x = dh0[:, N_META:N_META + seq]
    g['meta'] = jnp.sum(dh0[:, :N_META], axis=0)
    return loss, grad_x, g


class Exchange:
    def __init__(self, ins, out_shapes, n_remote, build, in_place):
        self.ins, self.out_shapes, self.n_remote, self.build, self.in_place = ins, out_shapes, n_remote, build, in_place

    def sems(self):
        return [pltpu.SemaphoreType.DMA((self.n_remote,)), pltpu.SemaphoreType.DMA((self.n_remote,))]

    def aliases(self):
        return self.in_place if isinstance(self.in_place, dict) else \
            {i: i for i in range(len(self.ins) if self.in_place else 0)}

    def beside(self, other):
        na, nao = len(self.ins), len(self.out_shapes)

        def build(in_refs, out_refs, me):
            pa = self.build(in_refs[:na], out_refs[:nao], me)
            pb = other.build(in_refs[na:], out_refs[nao:], me)
            n = max(len(pa), len(pb))
            return [(pa[i] if i < len(pa) else []) + (pb[i] if i < len(pb) else []) for i in range(n)]
        al = dict(self.aliases())
        al.update({na + i: nao + j for i, j in other.aliases().items()})
        return Exchange(list(self.ins) + list(other.ins), list(self.out_shapes) + list(other.out_shapes),
                        self.n_remote + other.n_remote, build, al)

    def phases(self, in_refs, out_refs, ssem, rsem):
        me = (lax.axis_index("x"), lax.axis_index("y"), lax.axis_index("c"))
        out, k = [], 0
        for phase in self.build(in_refs, out_refs, me):
            out.append(functools.partial(
                lambda phase, k: [pltpu.make_async_remote_copy(src_ref=s, dst_ref=d, send_sem=ssem.at[k + i],
                                                               recv_sem=rsem.at[k + i], device_id=p,
                                                               device_id_type=MESH)
                                  for i, (s, d, p) in enumerate(phase)], phase, k))
            k += len(phase)
        return out

    def start(self, in_refs, out_refs, ssem, rsem):
        for cp in self.phases(in_refs, out_refs, ssem, rsem)[0]():
            cp.start()

    def finish(self, in_refs, out_refs, ssem, rsem):
        phases = self.phases(in_refs, out_refs, ssem, rsem)
        for cp in phases[0]():
            cp.wait()
        for make in phases[1:]:
            copies = make()
            for cp in copies:
                cp.start()
            for cp in copies:
                cp.wait()


def _exchange(name, ex):
    ni, no = len(ex.ins), len(ex.out_shapes)

    def body(*refs):
        in_refs, out_refs, sems = refs[:ni], refs[ni:ni + no], refs[ni + no:]
        ex.start(in_refs, out_refs, *sems)
        ex.finish(in_refs, out_refs, *sems)

    anyspec = pl.BlockSpec(memory_space=pl.ANY)
    return pl.pallas_call(
        body, name=name, in_specs=[anyspec] * ni, out_specs=[anyspec] * no, out_shape=ex.out_shapes,
        input_output_aliases=ex.aliases(), scratch_shapes=ex.sems())(*ex.ins)


def _chip_peers(me):
    x, y, c = me
    return [(x ^ (m >> 1), y ^ (m & 1), c) for m in (1, 2, 3)]


def _half(ref, c, axis):
    h = ref.shape[axis] // 2
    idx = [slice(None)] * axis + [pl.ds(c * h, h)]
    return ref.at[tuple(idx)]


def _like(arrs):
    return [jax.ShapeDtypeStruct(a.shape, a.dtype) for a in arrs]


def gather_over_chips(bufs):
    def build(in_refs, out_refs, me):
        x, y, c = me
        j = 2 * x + y
        sib = (x, y, 1 - c)
        ici = [(_half(o.at[j], c, 0), _half(o.at[j], c, 0), p) for o in out_refs for p in _chip_peers(me)]
        d2d = [(_half(o.at[2 * p[0] + p[1]], c, 0), _half(o.at[2 * p[0] + p[1]], c, 0), sib)
               for o in out_refs for p in _chip_peers(me)]
        return [ici, d2d]
    return Exchange(bufs, _like(bufs), 6 * len(bufs), build, True)


def fold_cores(fulls):
    def build(in_refs, out_refs, me):
        x, y, c = me
        return [[(_half(s, 1 - c, 1), o, (x, y, 1 - c)) for s, o in zip(in_refs, out_refs)]]
    shapes = [jax.ShapeDtypeStruct((f.shape[0], f.shape[1] // 2, f.shape[2]), f.dtype) for f in fulls]
    return Exchange(fulls, shapes, len(fulls), build, False)


def scatter_over_chips(parts):
    def build(in_refs, out_refs, me):
        j = 2 * me[0] + me[1]
        return [[(s.at[2 * p[0] + p[1]], o.at[j], p) for s, o in zip(in_refs, out_refs) for p in _chip_peers(me)]]
    return Exchange(parts, _like(parts), 3 * len(parts), build, False)


def join_cores(bufs):
    def build(in_refs, out_refs, me):
        x, y, c = me
        return [[(o.at[c], o.at[c], (x, y, 1 - c)) for o in out_refs]]
    return Exchange(bufs, _like(bufs), len(bufs), build, True)


def gather_over_devices(buf):
    def build(in_refs, out_refs, me):
        x, y, c = me
        out = out_refs[0]
        sib = (x, y, 1 - c)
        mine = out.at[4 * x + 2 * y + c]
        first = [(mine, mine, sib)] + [(mine, mine, p) for p in _chip_peers(me)]
        second = [(out.at[4 * p[0] + 2 * p[1] + c], out.at[4 * p[0] + 2 * p[1] + c], sib) for p in _chip_peers(me)]
        return [first, second]
    return Exchange([buf], _like([buf]), N_DEV - 1, build, True)


def _slot_call(name, body, scalars, ins, in_maps, out_shapes, out_maps, blk, grid):
    gs = pltpu.PrefetchScalarGridSpec(
        num_scalar_prefetch=1, grid=grid,
        in_specs=[pl.BlockSpec((None,) + blk, m) for m in in_maps],
        out_specs=[pl.BlockSpec((None,) + blk, m) for m in out_maps])
    return pl.pallas_call(body, name=name, grid_spec=gs, out_shape=out_shapes,
                          compiler_params=_params(("arbitrary",) * len(grid)))(scalars, *ins)


def _group_rows(r, n):
    return _pick(r, (512, 256, 128, 64, 32, 16) if n <= 2 else (256, 128, 64, 32, 16))


def cast_into_slot(name, ws, chip, dtype):
    n = len(ws)
    r, wd = ws[0].shape
    tm = _group_rows(r, n)

    def body(s_ref, *refs):
        for w_ref, o_ref in zip(refs[:n], refs[n:]):
            o_ref[...] = w_ref[...].astype(o_ref.dtype)

    gs = pltpu.PrefetchScalarGridSpec(
        num_scalar_prefetch=1, grid=(r // tm,),
        in_specs=[pl.BlockSpec((tm, wd), lambda i, s: (i, 0))] * n,
        out_specs=[pl.BlockSpec((None, tm, wd), lambda i, s: (s[0], i, 0))] * n)
    return pl.pallas_call(body, name=name, grid_spec=gs,
                          out_shape=[jax.ShapeDtypeStruct((N_CHIPS, r, wd), dtype)] * n,
                          compiler_params=_params(("arbitrary",)))(chip.reshape(1), *ws)


def sum_cores(name, fulls, theirs, core):
    n = len(fulls)
    s, r, wd = fulls[0].shape
    h = r // 2
    tm = _group_rows(h, n)
    nt = h // tm

    def body(s_ref, *refs):
        for k in range(n):
            refs[2 * n + k][...] = (refs[2 * k][...] + refs[2 * k + 1][...]).astype(BF16)

    ins = [x for pair in zip(fulls, theirs) for x in pair]
    return _slot_call(name, body, core.reshape(1), ins,
                      [lambda k, i, s_: (k, s_[0] * nt + i, 0), lambda k, i, s_: (k, i, 0)] * n,
                      [jax.ShapeDtypeStruct((s, h, wd), BF16)] * n, [lambda k, i, s_: (k, i, 0)] * n,
                      (tm, wd), (s, nt))


def sum_chips(name, owns, recvs, chip, core):
    n = len(owns)
    s, r, wd = owns[0].shape
    tm = _group_rows(r, n)

    def body(s_ref, *refs):
        for k in range(n):
            acc = refs[4 * k][...].astype(F32)
            for ref in refs[4 * k + 1:4 * k + 4]:
                acc = acc + ref[...].astype(F32)
            refs[4 * n + k][...] = acc

    maps = [lambda i, s_: (s_[0], i, 0)]
    maps += [functools.partial(lambda i, s_, m: (s_[0] ^ m, i, 0), m=m) for m in (1, 2, 3)]
    ins = [x for own, recv in zip(owns, recvs) for x in (own, recv, recv, recv)]
    return _slot_call(name, body, jnp.stack([chip, core]), ins, maps * n,
                      [jax.ShapeDtypeStruct((2, r, wd), F32)] * n, [lambda i, s_: (s_[1], i, 0)] * n,
                      (tm, wd), (r // tm,))


def sum_slots(name, arr):
    s, r, wd = arr.shape
    tm = _pick(r, (512, 256, 128, 64, 32, 16, 8))

    def body(*refs):
        acc = refs[0][...].astype(F32)
        for ref in refs[1:s]:
            acc = acc + ref[...].astype(F32)
        refs[s][...] = acc

    specs = [pl.BlockSpec((None, tm, wd), functools.partial(lambda i, k: (k, i, 0), k=k)) for k in range(s)]
    return pl.pallas_call(body, name=name, grid=(r // tm,), in_specs=specs,
                          out_specs=pl.BlockSpec((tm, wd), lambda i: (i, 0)),
                          out_shape=jax.ShapeDtypeStruct((r, wd), F32),
                          compiler_params=_params(("parallel",)))(*([arr] * s))


def _pack(arrs):
    flat = jnp.concatenate([a.reshape(-1) for a in arrs])
    n = flat.shape[0]
    padded = -(-n // (512 * LANES)) * (512 * LANES)
    return jnp.pad(flat, (0, padded - n)).reshape(-1, LANES)


def _unpack(packed, shapes):
    flat = packed.reshape(-1)
    out, pos = [], 0
    for s in shapes:
        n = math.prod(s)
        out.append(flat[pos:pos + n].reshape(s))
        pos += n
    return out


def _as2d(a):
    return a.reshape(-1, a.shape[-1])


def kernel(x, meta, w_in_ab, s5_lam_re, s5_lam_im, s5_log_dt, s5_b_re, s5_b_im, s5_c_re, s5_c_im, s5_d, s5_w_glu, s5_b_glu, w_out_ab, w_in_c, hgrn_gamma, hgrn_norm_g, w_out_c, ln_mix_g, ln_mix_b, mlp_w_up, mlp_b_up, mlp_w_down, mlp_b_down, ln_mlp_g, ln_mlp_b, loss_target, m_meta, m_w_in_ab, m_s5_lam_re, m_s5_lam_im, m_s5_log_dt, m_s5_b_re, m_s5_b_im, m_s5_c_re, m_s5_c_im, m_s5_d, m_s5_w_glu, m_s5_b_glu, m_w_out_ab, m_w_in_c, m_hgrn_gamma, m_hgrn_norm_g, m_w_out_c, m_ln_mix_g, m_ln_mix_b, m_mlp_w_up, m_mlp_b_up, m_mlp_w_down, m_mlp_b_down, m_ln_mlp_g, m_ln_mlp_b, v_meta, v_w_in_ab, v_s5_lam_re, v_s5_lam_im, v_s5_log_dt, v_s5_b_re, v_s5_b_im, v_s5_c_re, v_s5_c_im, v_s5_d, v_s5_w_glu, v_s5_b_glu, v_w_out_ab, v_w_in_c, v_hgrn_gamma, v_hgrn_norm_g, v_w_out_c, v_ln_mix_g, v_ln_mix_b, v_mlp_w_up, v_mlp_b_up, v_mlp_w_down, v_mlp_b_down, v_ln_mlp_g, v_ln_mlp_b):
    given = dict(locals())
    wts = {n: given[n] for n in WEIGHTS}
    ms = {n: given['m_' + n] for n in WEIGHTS}
    vs = {n: given['v_' + n] for n in WEIGHTS}
    chip = 2 * lax.axis_index("x") + lax.axis_index("y")

    core = lax.axis_index("c")
    parts = [(n, l) for n in BIG for l in (range(DEPTH) if n.startswith('mlp_') else [0])]

    def by_shape(tag, fn, keys, *lists):
        out = {}
        for shape in dict.fromkeys(a.shape for a in lists[0]):
            idx = [i for i, a in enumerate(lists[0]) if a.shape == shape]
            res = fn(f"{tag}_{keys[idx[0]][0]}{keys[idx[0]][1]}_x{len(idx)}", *[[lst[i] for i in idx] for lst in lists])
            out.update({keys[i]: r for i, r in zip(idx, res)})
        return [out[k] for k in keys]

    shards = dict(zip(parts, by_shape("cast", lambda nm, ws: cast_into_slot(nm, ws, chip, BF16), parts,
                                      [wts[n][l] for n, l in parts])))
    small_sh = jnp.concatenate([meta, hgrn_norm_g, jnp.zeros((15, meta.shape[1]), F32)], axis=0)
    first, late = parts[:1], parts[1:]
    w_in_ab_all, sm = _exchange("gather_first", gather_over_chips(
        [shards[p] for p in first] + list(cast_into_slot("cast_small", [small_sh], chip, F32))))
    w = {n: wts[n] for n in SMALL}
    w['meta'] = sm[:, :N_META].transpose(1, 0, 2).reshape(N_META, D_MODEL)
    w['hgrn_norm_g'] = sm[:, N_META:N_META + 1].transpose(1, 0, 2).reshape(1, D_MODEL)
    w['w_in_ab'] = w_in_ab_all

    def slot_major(v):
        return v if v.ndim == 3 else v.reshape(N_CHIPS, -1, v.shape[-1])

    def whole_grads(ps, g):
        return [slot_major(g[n][l] if n.startswith('mlp_') else g[n]) for n, l in ps]

    def chip_sums_of(ps, gfull, theirs):
        return by_shape("sum_cores", lambda nm, a, b: sum_cores(nm, a, b, core), ps, gfull, list(theirs))

    def all_devices(arrs):
        packed = _pack(arrs)
        return gather_over_devices(lax.dynamic_update_slice(
            jnp.zeros((N_DEV,) + packed.shape, F32), packed[None], (2 * chip + core, 0, 0)))

    early_small = [n for n in SMALL if n != 'meta']
    last = [('w_out_c', 0), ('mlp_w_up', 1), ('mlp_w_down', 1)]
    mid = [p for p in late if p not in last]

    class Hooks:
        gather_late = gather_over_chips([shards[p] for p in mid])
        gather_last = gather_over_chips([shards[p] for p in last])

        @staticmethod
        def late_weights(filled):
            fw = Hooks.fw = dict(zip(mid, filled))
            return {'s5_w_glu': fw['s5_w_glu', 0].reshape(S5_WIDTH, S5_WIDTH),
                    'w_out_ab': fw['w_out_ab', 0].reshape(D_MODEL, D_MODEL),
                    'w_in_c': fw['w_in_c', 0],
                    'mlp_w_up': [fw['mlp_w_up', 0]],
                    'mlp_w_down': [fw['mlp_w_down', 0].reshape(D_FF, D_MODEL)]}

        @staticmethod
        def last_weights(filled):
            fw = dict(zip(last, filled))
            return {'w_out_c': fw['w_out_c', 0].reshape(D_MODEL, D_MODEL),
                    'mlp_w_up': [Hooks.fw['mlp_w_up', 0], fw['mlp_w_up', 1]],
                    'mlp_w_down': [Hooks.fw['mlp_w_down', 0].reshape(D_FF, D_MODEL),
                                   fw['mlp_w_down', 1].reshape(D_FF, D_MODEL)]}

        @staticmethod
        def fold_early(g):
            Hooks.whole = whole_grads(late, g)
            return fold_cores(Hooks.whole)

        @staticmethod
        def scatter_early(g, loss, theirs):
            Hooks.sums = chip_sums_of(late, Hooks.whole, theirs)
            Hooks.small_shapes = [(LANES,)] + [g[n].shape for n in early_small]
            Hooks.gather_small = all_devices([loss.reshape(-1)] + [g[n] for n in early_small])
            return scatter_over_chips(Hooks.sums)

        @staticmethod
        def scattered(outs):
            Hooks.recv = list(outs)

        @staticmethod
        def small_gathered(res):
            Hooks.small = res[1]
            return res[0]

    loss, grad_x, g = local_step(x, loss_target, w, Hooks)

    whole = whole_grads(first, g)
    sums = chip_sums_of(first, whole, _exchange("fold_last", fold_cores(whole)))
    *recv, meta_slots = _exchange("scatter_last", scatter_over_chips(sums).beside(all_devices([g['meta']])))
    reduced = _exchange("join_grads", join_cores(
        by_shape("sum_chips", lambda nm, a, b: sum_chips(nm, a, b, chip, core), first + late,
                 sums + Hooks.sums, list(recv) + Hooks.recv)))
    reduced = dict(zip(first + late, [_as2d(r) for r in reduced]))
    red = _unpack(sum_slots("sum_small", Hooks.small), Hooks.small_shapes)
    loss_out = red[0][0]
    gs = dict(zip(early_small, red[1:]))
    gs['meta'] = _unpack(sum_slots("sum_meta", meta_slots), [g['meta'].shape])[0]
    gs['meta'] = lax.dynamic_slice_in_dim(gs['meta'], chip * meta.shape[1], meta.shape[1], axis=1)
    gs['hgrn_norm_g'] = lax.dynamic_slice_in_dim(gs['hgrn_norm_g'].reshape(1, -1), chip * meta.shape[1],
                                                 meta.shape[1], axis=1)

    grads, deltas, new_m, new_v = {}, {}, {}, {}
    for n in BIG:
        r = jnp.concatenate([reduced[n, l] for l in range(wts[n].shape[0])], axis=0)
        res = adamw("adamw_" + n, _as2d(wts[n]), [r], _as2d(ms[n]), _as2d(vs[n]))
        grads[n], deltas[n], new_m[n], new_v[n] = [r.reshape(wts[n].shape) for r in res]
    shapes = [wts[n].shape for n in SMALL]
    res = adamw("adamw_small", _pack([wts[n] for n in SMALL]), [_pack([gs[n] for n in SMALL])],
                _pack([ms[n] for n in SMALL]), _pack([vs[n] for n in SMALL]))
    for d, r in zip((grads, deltas, new_m, new_v), res):
        d.update(zip(SMALL, _unpack(r, shapes)))
    return (loss_out, grad_x, *[grads[n] for n in WEIGHTS], *[deltas[n] for n in WEIGHTS],
            *[new_m[n] for n in WEIGHTS], *[new_v[n] for n in WEIGHTS])
```

```python
import functools
import math

import jax
import jax.numpy as jnp
from jax import lax
from jax.experimental import pallas as pl
from jax.experimental.pallas import tpu as pltpu

F32 = jnp.float32
BF16 = jnp.bfloat16

D_MODEL = 1024
N_META = 16
DEPTH = 2
S5_WIDTH = 512
S5_GROUP = 16
S5_GROUPS = 32
S5_STATE = 64
S5_CHUNK = 16
SB_WIDTH = 512
SB_BLOCK = 128
SB_DEAD = -110.0
HG_HEADS = 8
HG_DK = 128
HG_CHUNK = 64
HG_UNROLL = 2
D_FF = 4096
ALPHA = (2.0 * DEPTH) ** 0.25
LN_EPS = 1e-5
RMS_EPS = 1e-6
ADAM_LR = 0.001
ADAM_B1 = 0.9
ADAM_B2 = 0.999
ADAM_EPS = 1e-08
ADAM_WD = 0.01
ADAM_STEP = 10
N_CHIPS = 4
N_DEV = 8
LANES = 128
MESH = pl.DeviceIdType.MESH
VMEM_LIMIT = 56 * 1024 * 1024
WHOLE_K_BYTES = 42 * 1024 * 1024
HGRN_BWD_VMEM = 60 * 1024 * 1024

WEIGHTS = ['meta', 'w_in_ab', 's5_lam_re', 's5_lam_im', 's5_log_dt', 's5_b_re', 's5_b_im', 's5_c_re', 's5_c_im',
           's5_d', 's5_w_glu', 's5_b_glu', 'w_out_ab', 'w_in_c', 'hgrn_gamma', 'hgrn_norm_g', 'w_out_c',
           'ln_mix_g', 'ln_mix_b', 'mlp_w_up', 'mlp_b_up', 'mlp_w_down', 'mlp_b_down', 'ln_mlp_g', 'ln_mlp_b']
BIG = ['w_in_ab', 's5_w_glu', 'w_out_ab', 'w_in_c', 'w_out_c', 'mlp_w_up', 'mlp_w_down']
SMALL = [n for n in WEIGHTS if n not in BIG]


def _params(sem=None):
    return pltpu.CompilerParams(dimension_semantics=sem, vmem_limit_bytes=VMEM_LIMIT)


def _pick(n, cands):
    for c in cands:
        if n % c == 0:
            return c
    return n


def _dot(a, b, ca, cb):
    return lax.dot_general(a, b, (((ca,), (cb,)), ((), ())), preferred_element_type=F32)


def _split3(x):
    hi = x.astype(BF16)
    r = x - hi.astype(F32)
    mid = r.astype(BF16)
    lo = (r - mid.astype(F32)).astype(BF16)
    return hi, mid, lo


def _dot_exact_rhs(x, m, cx, cm):
    hi = x.astype(BF16)
    lo = (x - hi.astype(F32)).astype(BF16)
    return _dot(hi, m, cx, cm) + _dot(lo, m, cx, cm)


def _dot3(a, b, ca, cb):
    ah = a.astype(BF16)
    al = (a - ah.astype(F32)).astype(BF16)
    bh = b.astype(BF16)
    bl = (b - bh.astype(F32)).astype(BF16)
    return _dot(ah, bh, ca, cb) + (_dot(ah, bl, ca, cb) + _dot(al, bh, ca, cb))


def rowwise(name, fn, row_ins, bc_ins, out_widths, sum_widths=(), out_dtypes=None, tm=None):
    t = row_ins[0].shape[0]
    if tm is None:
        wmax = max([a.shape[1] for a in row_ins] + list(out_widths))
        tm = _pick(t, (272, 256, 128, 64, 16, 8)) if wmax > 1024 else _pick(t, (544, 512, 256, 128, 64, 16, 8))
    nr, nb, no, ns = len(row_ins), len(bc_ins), len(out_widths), len(sum_widths)
    out_dtypes = out_dtypes or [F32] * no

    def body(*refs):
        i = pl.program_id(0)
        rows = [r[...] for r in refs[:nr]]
        bcs = [r[...] for r in refs[nr:nr + nb]]
        outs, sums = fn(i, tm, rows, bcs)
        for r, v in zip(refs[nr + nb:nr + nb + no], outs):
            r[...] = v.astype(r.dtype)
        if ns:
            srefs = refs[nr + nb + no:]

            @pl.when(i == 0)
            def _():
                for r in srefs:
                    r[...] = jnp.zeros(r.shape, r.dtype)

            for r, v in zip(srefs, sums):
                r[...] += v

    in_specs = [pl.BlockSpec((tm, a.shape[1]), lambda i: (i, 0)) for a in row_ins]
    in_specs += [pl.BlockSpec(a.shape, lambda i: (0, 0)) for a in bc_ins]
    out_specs = [pl.BlockSpec((tm, w), lambda i: (i, 0)) for w in out_widths]
    out_specs += [pl.BlockSpec((1, w), lambda i: (0, 0)) for w in sum_widths]
    out_shape = [jax.ShapeDtypeStruct((t, w), dt) for w, dt in zip(out_widths, out_dtypes)]
    out_shape += [jax.ShapeDtypeStruct((1, w), F32) for w in sum_widths]
    res = pl.pallas_call(body, name=name, grid=(t // tm,), in_specs=in_specs, out_specs=out_specs,
                         out_shape=out_shape, compiler_params=_params(("arbitrary",)))(*row_ins, *bc_ins)
    return res


def _colsum(v):
    return jnp.sum(v, axis=0, keepdims=True)


def matmul(name, a, b, mode, *, bias=None, act=None, add=None, add_scale=1.0, relu2_of=None, colsum=False,
           out_dtype=F32, out_slots=0, tm=None, tn=None, tk=None, ex=None):
    slotted = b.ndim == 3
    if mode == 'nn':
        m, k = a.shape
        n = b.shape[-1] * (b.shape[0] if slotted else 1)
    elif mode == 'nt':
        m, k = a.shape
        n = b.shape[-2]
    else:
        k, m = a.shape
        n = b.shape[-1]
    ns = b.shape[-1] if slotted else None
    if mode == 'tn':
        tm = tm or _pick(m, (512, 256, 128))
        nw = n if not out_slots else n // out_slots
        if tn is None and tk is None:
            for cand in (512, 256):
                if nw % cand == 0 and (2 * k * (tm * a.dtype.itemsize + cand * b.dtype.itemsize)
                                       + 2 * tm * cand * 4 <= WHOLE_K_BYTES):
                    tn, tk = cand, k
                    break
        tn = tn or _pick(nw, (1024, 512, 256, 128))
        tk = tk or _pick(k, (1088, 1024, 768, 512, 384, 256, 128))
    else:
        tm = tm or _pick(m, (1088, 1024, 768, 512, 384, 256, 128))
        tn = tn or _pick(n if not (slotted and mode == 'nn') else ns, (1024, 512, 256, 128))
        extra = sum(x is not None for x in (add, relu2_of)) * 4
        if tk is None and not slotted and (2 * (tm * k * a.dtype.itemsize + k * tn * b.dtype.itemsize)
                                           + 2 * tm * tn * (4 + extra) <= WHOLE_K_BYTES):
            tk = k
        tk = tk or _pick(k if not (slotted and mode == 'nt') else ns, (1024, 512, 256, 128))
    gm, gn, gk = m // tm, n // tn, k // tk

    if mode == 'nn':
        a_spec = pl.BlockSpec((tm, tk), lambda i, j, kk: (i, kk))
        if slotted:
            per = ns // tn
            b_spec = pl.BlockSpec((None, tk, tn), lambda i, j, kk: (j // per, kk, j % per))
        else:
            b_spec = pl.BlockSpec((tk, tn), lambda i, j, kk: (kk, j))
        ca, cb = 1, 0
    elif mode == 'nt':
        a_spec = pl.BlockSpec((tm, tk), lambda i, j, kk: (i, kk))
        if slotted:
            per = ns // tk
            b_spec = pl.BlockSpec((None, tn, tk), lambda i, j, kk: (kk // per, j, kk % per))
        else:
            b_spec = pl.BlockSpec((tn, tk), lambda i, j, kk: (j, kk))
        ca, cb = 1, 1
    else:
        a_spec = pl.BlockSpec((tk, tm), lambda i, j, kk: (kk, i))
        b_spec = pl.BlockSpec((tk, tn), lambda i, j, kk: (kk, j))
        ca, cb = 0, 0
    in_specs = [a_spec, b_spec]
    args = [a, b]
    if bias is not None:
        in_specs.append(pl.BlockSpec((1, tn), lambda i, j, kk: (0, j)))
        args.append(bias)
    if add is not None:
        in_specs.append(pl.BlockSpec((tm, tn), lambda i, j, kk: (i, j)))
        args.append(add)
    if relu2_of is not None:
        in_specs.append(pl.BlockSpec((tm, tn), lambda i, j, kk: (i, j)))
        args.append(relu2_of)
    if out_slots:
        per_o = (n // out_slots) // tn
        out_spec = pl.BlockSpec((None, tm, tn), lambda i, j, kk: (j // per_o, i, j % per_o))
        out_shape = jax.ShapeDtypeStruct((out_slots, m, n // out_slots), out_dtype)
    else:
        out_spec = pl.BlockSpec((tm, tn), lambda i, j, kk: (i, j))
        out_shape = jax.ShapeDtypeStruct((m, n), out_dtype)
    grid = (gm, gn, gk)
    if colsum:
        assert mode == 'tn'
        out_spec = [out_spec, pl.BlockSpec((1, tn), lambda i, j, kk: (0, j))]
        out_shape = [out_shape, jax.ShapeDtypeStruct((1, n), F32)]

        def swapped(spec):
            return pl.BlockSpec(spec.block_shape, functools.partial(lambda j, i, kk, f: f(i, j, kk), f=spec.index_map))
        in_specs = [swapped(sp) for sp in in_specs]
        out_spec = [swapped(sp) for sp in out_spec]
        grid = (gn, gm, gk)

    split, ex_start, ex_finish, exkw = _hidden_exchange(ex, len(args), 2 if colsum else 1, grid)

    def body(*refs):
        ins, outs, ex_parts, own = split(refs)
        refs = (*ins, *outs, *own)
        ex_start(ex_parts)
        a_ref, b_ref = refs[0], refs[1]
        pos = 2
        bias_ref = add_ref = hid_ref = cs_ref = None
        if bias is not None:
            bias_ref = refs[pos]
            pos += 1
        if add is not None:
            add_ref = refs[pos]
            pos += 1
        if relu2_of is not None:
            hid_ref = refs[pos]
            pos += 1
        o_ref = refs[pos]
        pos += 1
        if colsum:
            cs_ref = refs[pos]
            pos += 1
        acc_ref = refs[pos] if gk > 1 else None
        bt = b_ref[...]
        p = _dot(a_ref[...].astype(BF16), bt.astype(BF16), ca, cb)
        kk = pl.program_id(2)

        if colsum:
            @pl.when(jnp.logical_and(pl.program_id(1) == 0, kk == 0))
            def _():
                cs_ref[...] = _colsum(bt.astype(F32))

            @pl.when(jnp.logical_and(pl.program_id(1) == 0, kk > 0))
            def _():
                cs_ref[...] += _colsum(bt.astype(F32))

        def finish(r):
            if bias_ref is not None:
                r = r + bias_ref[...]
            if act == 'relu2':
                r = jnp.square(jnp.maximum(r, 0.0))
            if hid_ref is not None:
                r = r * (2.0 * jnp.sqrt(hid_ref[...].astype(F32)))
            if add_ref is not None:
                r = r + add_scale * add_ref[...]
            o_ref[...] = r.astype(o_ref.dtype)

        if gk == 1:
            finish(p)
        else:
            @pl.when(kk == 0)
            def _():
                acc_ref[...] = p

            @pl.when(kk > 0)
            def _():
                acc_ref[...] += p

            @pl.when(kk == gk - 1)
            def _():
                finish(acc_ref[...])
        ex_finish(ex_parts)

    scratch = [pltpu.VMEM((tm, tn), F32)] if gk > 1 else []
    sem = ("arbitrary",) * 3 if colsum or ex is not None else ("parallel", "parallel", "arbitrary")
    if ex is not None:
        out_spec = (out_spec if colsum else [out_spec]) + exkw['out_specs']
        out_shape = (out_shape if colsum else [out_shape]) + exkw['out_shape']
    return pl.pallas_call(body, name=name, grid=grid, in_specs=in_specs + exkw['in_specs'], out_specs=out_spec,
                          out_shape=out_shape, scratch_shapes=scratch + exkw['scratch'],
                          input_output_aliases=exkw['aliases'], compiler_params=_params(sem))(*args, *exkw['ins'])


def ln_fwd(name, h, mix, g, b):
    def fn(i, tm, rows, bcs):
        pre = ALPHA * rows[0] + rows[1]
        mu = jnp.mean(pre, axis=1, keepdims=True)
        xc = pre - mu
        var = jnp.mean(xc * xc, axis=1, keepdims=True)
        rstd = lax.rsqrt(var + LN_EPS)
        xhat = xc * rstd
        return (xhat * bcs[0] + bcs[1], xhat, rstd), ()
    return rowwise(name, fn, [h, mix], [g, b], [D_MODEL, D_MODEL, 1])


def ln_bwd(name, dy, xhat, rstd, g):
    def fn(i, tm, rows, bcs):
        dy_, xh, rs = rows
        dyg = dy_ * bcs[0]
        m1 = jnp.mean(dyg, axis=1, keepdims=True)
        m2 = jnp.mean(dyg * xh, axis=1, keepdims=True)
        dpre = rs * (dyg - m1 - xh * m2)
        return (dpre,), (_colsum(dy_ * xh), _colsum(dy_), _colsum(dpre))
    return rowwise(name, fn, [dy, xhat, rstd], [g], [D_MODEL], [D_MODEL] * 3)


_GELU_C = math.sqrt(2.0 / math.pi)


def gelu_fwd(name, x):
    def fn(i, tm, rows, bcs):
        v = rows[0]
        u = _GELU_C * (v + 0.044715 * (v * v * v))
        return (0.5 * v * (1.0 + jnp.tanh(u)),), ()
    return rowwise(name, fn, [x], [], [x.shape[1]])[0]


def gelu_bwd(name, dy, x):
    def fn(i, tm, rows, bcs):
        v = rows[1]
        th = jnp.tanh(_GELU_C * (v + 0.044715 * (v * v * v)))
        dg = 0.5 * (1.0 + th) + 0.5 * v * (1.0 - th * th) * (_GELU_C * (1.0 + 3.0 * 0.044715 * v * v))
        return (rows[0] * dg,), ()
    return rowwise(name, fn, [dy, x], [], [x.shape[1]])[0]


def glu_gate(name, y, gp):
    def fn(i, tm, rows, bcs):
        return (rows[0] * jax.nn.sigmoid(rows[1]),), ()
    return rowwise(name, fn, [y, gp], [], [y.shape[1]], out_dtypes=[BF16])[0]


def glu_gate_bwd(name, da, y, gp):
    def fn(i, tm, rows, bcs):
        s = jax.nn.sigmoid(rows[2])
        dgp = rows[0] * rows[1] * s * (1.0 - s)
        return (dgp, rows[0] * s), (_colsum(dgp),)
    w = y.shape[1]
    return rowwise(name, fn, [da, y, gp], [], [w, w], [w])


def ln_loss(name, h, mix, g, b, tgt, lp, seq):
    def fn(i, tm, rows, bcs):
        pre = ALPHA * rows[0] + rows[1]
        mu = jnp.mean(pre, axis=1, keepdims=True)
        xc = pre - mu
        var = jnp.mean(xc * xc, axis=1, keepdims=True)
        rstd = lax.rsqrt(var + LN_EPS)
        xhat = xc * rstd
        pos = (i * tm + lax.broadcasted_iota(jnp.int32, (tm, 1), 0)) % lp
        valid = jnp.logical_and(pos >= N_META, pos < N_META + seq)
        err = jnp.where(valid, (xhat * bcs[0] + bcs[1]) - rows[2], 0.0)
        part = 0.5 * jnp.sum(jnp.mean(err * err, axis=1, keepdims=True), axis=0, keepdims=True)
        dy = err * (1.0 / D_MODEL)
        dyg = dy * bcs[0]
        m1 = jnp.mean(dyg, axis=1, keepdims=True)
        m2 = jnp.mean(dyg * xhat, axis=1, keepdims=True)
        dpre = rstd * (dyg - m1 - xhat * m2)
        return (dpre,), (_colsum(dy * xhat), _colsum(dy), _colsum(dpre), jnp.broadcast_to(part, (1, LANES)))
    return rowwise(name, fn, [h, mix, tgt], [g, b], [D_MODEL], [D_MODEL] * 3 + [LANES])


def adamw(name, w, gs, m, v):
    ng = len(gs)

    def fn(i, tm, rows, bcs):
        w_ = rows[0]
        g = rows[1] if ng == 1 else rows[1] + rows[2]
        m_, v_ = rows[1 + ng], rows[2 + ng]
        m_ = ADAM_B1 * m_ + (1.0 - ADAM_B1) * g
        v_ = ADAM_B2 * v_ + (1.0 - ADAM_B2) * jnp.square(g)
        m_hat = m_ / (1.0 - ADAM_B1 ** ADAM_STEP)
        v_hat = v_ / (1.0 - ADAM_B2 ** ADAM_STEP)
        delta = -ADAM_LR * (m_hat / (jnp.sqrt(v_hat) + ADAM_EPS) + ADAM_WD * w_)
        return (g, delta, m_, v_), ()
    wd = w.shape[1]
    return rowwise(name, fn, [w] + list(gs) + [m, v], [], [wd] * 4)


def s5_matrices(lam_re, lam_im, log_dt, b_re, b_im, c_re, c_im, d):
    c, hh, gg, pp = S5_CHUNK, S5_GROUP, S5_GROUPS, S5_STATE
    dt = jnp.exp(log_dt)[:, None]
    tau = jnp.arange(c + 1, dtype=F32)[None, :, None]
    mag = jnp.exp(tau * (lam_re * dt)[:, None, :])
    ang = tau * (lam_im * dt)[:, None, :]
    pw_re, pw_im = mag * jnp.cos(ang), mag * jnp.sin(ang)
    nr, ni = pw_re[:, 1] - 1.0, pw_im[:, 1]
    den = lam_re * lam_re + lam_im * lam_im
    cf_re = (nr * lam_re + ni * lam_im) / den
    cf_im = (ni * lam_re - nr * lam_im) / den
    bb_re = cf_re[:, :, None] * b_re - cf_im[:, :, None] * b_im
    bb_im = cf_re[:, :, None] * b_im + cf_im[:, :, None] * b_re
    ct_re, ct_im = c_re.transpose(0, 2, 1)[:, :, None, :], c_im.transpose(0, 2, 1)[:, :, None, :]
    pt_re, pt_im = pw_re.transpose(0, 2, 1)[:, :, :, None], pw_im.transpose(0, 2, 1)[:, :, :, None]
    cp_re = ct_re * pt_re - ct_im * pt_im
    cp_im = ct_re * pt_im + ct_im * pt_re
    hp = lax.Precision.HIGHEST
    kmat = (jnp.einsum('gpk,gpn->gkn', bb_re, cp_re[:, :, :c].reshape(gg, pp, c * hh), precision=hp)
            - jnp.einsum('gpk,gpn->gkn', bb_im, cp_im[:, :, :c].reshape(gg, pp, c * hh), precision=hp))
    rows = [jnp.pad(kmat[:, :, :(c - j) * hh], ((0, 0), (0, 0), (j * hh, 0))) for j in range(c)]
    a1 = jnp.stack(rows, axis=1).reshape(gg, c * hh, c * hh)
    a1 = a1 + jnp.eye(c * hh, dtype=F32)[None] * jnp.tile(d, (1, c))[:, None, :]
    a2 = jnp.concatenate([cp_re[:, :, 1:].reshape(gg, pp, c * hh), -cp_im[:, :, 1:].reshape(gg, pp, c * hh)], axis=1)
    rv_re, rv_im = pw_re[:, :c][:, ::-1, None, :], pw_im[:, :c][:, ::-1, None, :]
    bt_re, bt_im = bb_re.transpose(0, 2, 1)[:, None], bb_im.transpose(0, 2, 1)[:, None]
    a3 = jnp.concatenate([rv_re * bt_re - rv_im * bt_im, rv_re * bt_im + rv_im * bt_re], axis=-1)
    a3 = a3.reshape(gg, c * hh, 2 * pp)
    are = jnp.concatenate([pw_re[:, c], pw_re[:, c]], axis=-1)[:, None, :]
    aim = jnp.concatenate([-pw_im[:, c], pw_im[:, c]], axis=-1)[:, None, :]
    return a1, a2, a3, are, aim


S5_LEVELS = 8


def s5_scan_powers(lam_re, lam_im, log_dt):
    dt = jnp.exp(log_dt)[:, None]
    e = (S5_CHUNK * 2.0 ** jnp.arange(S5_LEVELS, dtype=F32))[:, None, None]
    mag = jnp.exp(e * (lam_re * dt)[None])
    ang = e * (lam_im * dt)[None]
    pr, pi = mag * jnp.cos(ang), mag * jnp.sin(ang)
    pre = jnp.concatenate([pr, pr], axis=-1).transpose(1, 0, 2)
    pim = jnp.concatenate([-pi, pi], axis=-1).transpose(1, 0, 2)
    return pre, pim


def _s5_scan(x, pre, pim, reverse):
    n = x.shape[0]
    rowi = lax.broadcasted_iota(jnp.int32, (n, 1), 0)
    t = x
    for k in range(S5_LEVELS):
        shift = 2 ** k
        if shift >= n:
            break
        p_re, p_im = pre[k:k + 1, :], pim[k:k + 1, :]
        if reverse:
            far = jnp.where(rowi < n - shift, pltpu.roll(t, n - shift, 0), 0.0)
            t = t + (p_re * far + pltpu.roll(p_im * far, S5_STATE, 1))
        else:
            far = jnp.where(rowi >= shift, pltpu.roll(t, shift, 0), 0.0)
            t = t + (p_re * far + p_im * pltpu.roll(far, S5_STATE, 1))
    return t


def s5_fwd(ug, a1, a2, a3, pre, pim, nseq):
    g, nc, cw = ug.shape
    per = nc // nseq
    sw = 2 * S5_STATE
    assert per <= 2 ** S5_LEVELS

    def body(u_ref, a1_ref, a2_ref, a3_ref, pre_ref, pim_ref, y_ref, s_ref):
        u = u_ref[...]
        x = _dot3(u, a3_ref[...], 1, 0)
        first = lax.broadcasted_iota(jnp.int32, (per, 1), 0) == 0
        for b in range(nseq):
            t = _s5_scan(x[b * per:(b + 1) * per], pre_ref[...], pim_ref[...], False)
            s_ref[pl.ds(b * per, per), :] = jnp.where(first, 0.0, pltpu.roll(t, 1, 0))
        y_ref[...] = _dot3(u, a1_ref[...], 1, 0) + _dot3(s_ref[...], a2_ref[...], 1, 0)

    def spec(shape):
        return pl.BlockSpec((None,) + shape, lambda i: (i, 0, 0))
    lv = (S5_LEVELS, sw)
    return pl.pallas_call(
        body, name="s5_fwd", grid=(g,),
        in_specs=[spec((nc, cw)), spec((cw, cw)), spec((sw, cw)), spec((cw, sw)), spec(lv), spec(lv)],
        out_specs=[spec((nc, cw)), spec((nc, sw))],
        out_shape=[jax.ShapeDtypeStruct((g, nc, cw), F32), jax.ShapeDtypeStruct((g, nc, sw), F32)],
        compiler_params=_params(("parallel",)))(ug, a1, a2, a3, pre, pim)


def s5_bwd(dyg, ug, sst, a1, a2, a3, pre, pim, nseq, ex=None):
    g, nc, cw = ug.shape
    per = nc // nseq
    sw = 2 * S5_STATE
    split, ex_start, ex_finish, exkw = _hidden_exchange(ex, 8, 6, (g,))

    def body(*refs):
        ((dy_ref, u_ref, s_ref, a1_ref, a2_ref, a3_ref, pre_ref, pim_ref),
         (du_ref, da1_ref, da2_ref, da3_ref, dare_ref, daim_ref), ex_parts, (dx_ref,)) = split(refs)
        ex_start(ex_parts)
        dy = dy_ref[...]
        u = u_ref[...]
        gd = _dot3(dy, a2_ref[...], 1, 1)
        s_all = s_ref[...]
        da2_ref[...] = _dot3(s_all, dy, 0, 0)
        last = lax.broadcasted_iota(jnp.int32, (per, 1), 0) == per - 1
        for b in range(nseq):
            gs = _s5_scan(gd[b * per:(b + 1) * per], pre_ref[...], pim_ref[...], True)
            dx_ref[pl.ds(b * per, per), :] = jnp.where(last, 0.0, pltpu.roll(gs, per - 1, 0))
        dx = dx_ref[...]
        dare_ref[...] = _colsum(dx * s_all)
        daim_ref[...] = _colsum(dx * pltpu.roll(s_all, S5_STATE, 1))
        du_ref[...] = _dot3(dy, a1_ref[...], 1, 1) + _dot3(dx, a3_ref[...], 1, 1)
        da1_ref[...] = _dot3(u, dy, 0, 0)
        da3_ref[...] = _dot3(u, dx, 0, 0)
        ex_finish(ex_parts)

    def spec(shape):
        return pl.BlockSpec((None,) + shape, lambda i: (i, 0, 0))
    sds = jax.ShapeDtypeStruct
    return pl.pallas_call(
        body, name="s5_bwd", grid=(g,),
        in_specs=[spec((nc, cw)), spec((nc, cw)), spec((nc, sw)), spec((cw, cw)), spec((sw, cw)), spec((cw, sw)),
                  spec((S5_LEVELS, sw)), spec((S5_LEVELS, sw))] + exkw['in_specs'],
        out_specs=[spec((nc, cw)), spec((cw, cw)), spec((sw, cw)), spec((cw, sw)), spec((1, sw)), spec((1, sw))]
        + exkw['out_specs'],
        out_shape=[sds((g, nc, cw), F32), sds((g, cw, cw), F32), sds((g, sw, cw), F32), sds((g, cw, sw), F32),
                   sds((g, 1, sw), F32), sds((g, 1, sw), F32)] + exkw['out_shape'],
        input_output_aliases=exkw['aliases'], scratch_shapes=[pltpu.VMEM((nc, sw), F32)] + exkw['scratch'],
        compiler_params=_params(("arbitrary",)))(dyg, ug, sst, a1, a2, a3, pre, pim, *exkw['ins'])


def _to_groups(u):
    t = u.shape[0]
    nc = t // S5_CHUNK
    return u.reshape(nc, S5_CHUNK, S5_GROUPS, S5_GROUP).transpose(2, 0, 1, 3).reshape(S5_GROUPS, nc, -1)


def _from_groups(yg):
    g, nc, _ = yg.shape
    return yg.reshape(g, nc, S5_CHUNK, S5_GROUP).transpose(1, 2, 0, 3).reshape(nc * S5_CHUNK, g * S5_GROUP)


def _sb_masks():
    row = lax.broadcasted_iota(jnp.int32, (SB_BLOCK, SB_BLOCK), 0)
    col = lax.broadcasted_iota(jnp.int32, (SB_BLOCK, SB_BLOCK), 1)
    return row, col


def _sb_weights(z, vis, later_of, after):
    sp = jnp.maximum(z, 0.0) + jnp.log1p(jnp.exp(-jnp.abs(z)))
    lk = jnp.where(vis, -sp, 0.0)
    rs = jnp.sum(lk, axis=1, keepdims=True)
    later = _dot_exact_rhs(lk, after, 1, 0) + later_of(rs)
    lb = z - sp
    w = jnp.where(vis, jnp.exp(lb + later), 0.0)
    return w, rs, lb


def _hidden_exchange(ex, n_in, n_out, grid):
    ni, no = (len(ex.ins), len(ex.out_shapes)) if ex else (0, 0)

    def split(refs):
        a, b = n_in + ni, n_in + ni + n_out
        end = len(refs) - (2 if ex else 0)
        return refs[:n_in], refs[a:b], (refs[n_in:a], refs[b:b + no], refs[end:]), refs[b + no:end]

    def at_step(which, fn, parts):
        if ex is not None:
            ids = [pl.program_id(d) == (0 if which == 'first' else g - 1) for d, g in enumerate(grid)]
            cond = ids[0]
            for c in ids[1:]:
                cond = jnp.logical_and(cond, c)
            pl.when(cond)(lambda: fn(parts[0], parts[1], *parts[2]))

    anyspec = pl.BlockSpec(memory_space=pl.ANY)
    kw = dict(in_specs=[anyspec] * ni, out_specs=[anyspec] * no, out_shape=list(ex.out_shapes) if ex else [],
              aliases={n_in + i: n_out + j for i, j in (ex.aliases().items() if ex else ())},
              scratch=ex.sems() if ex else [], ins=list(ex.ins) if ex else [])
    start = functools.partial(at_step, 'first', ex.start if ex else None)
    finish = functools.partial(at_step, 'last', ex.finish if ex else None)
    return split, start, finish, kw


SB_TILES = 2
SB_HEADS = 2 * SB_TILES
SB_WD = SB_TILES * LANES


def _stack_heads(x):
    return jnp.concatenate([x] * SB_HEADS, axis=0)


def _stack_masks():
    row = lax.broadcasted_iota(jnp.int32, (SB_HEADS * SB_BLOCK, SB_WD), 0)
    col = lax.broadcasted_iota(jnp.int32, (SB_HEADS * SB_BLOCK, SB_WD), 1)
    rowk = lax.broadcasted_iota(jnp.int32, (SB_HEADS * SB_BLOCK, SB_BLOCK), 0)
    colk = lax.broadcasted_iota(jnp.int32, (SB_HEADS * SB_BLOCK, SB_BLOCK), 1)
    return (col // 64) == (row // SB_BLOCK), colk < (rowk % SB_BLOCK)


def _own_lanes(r):
    head = lax.broadcasted_iota(jnp.int32, (SB_BLOCK, SB_WD), 1) // 64
    out = r[:SB_BLOCK]
    for h in range(1, SB_HEADS):
        out = jnp.where(head == h, r[h * SB_BLOCK:(h + 1) * SB_BLOCK], out)
    return out


def _sb_specs(lp, nseq):
    wd = SB_TILES * LANES
    off = [(S5_WIDTH + i * SB_WIDTH) // wd for i in range(3)]
    blk = (lp, wd)
    qkv = [pl.BlockSpec(blk, functools.partial(lambda b, h, o: (b, o + h), o=o)) for o in off]
    return (blk, qkv, pl.BlockSpec(blk, lambda b, h: (b, h)),
            pl.BlockSpec((None, None, 8, LANES), lambda b, h: (b, h, 0, 0)))


def attn_fwd(proj, nseq, lp, ex=None):
    nqb = lp // SB_BLOCK
    t = proj.shape[0]
    nstep = SB_WIDTH // (SB_TILES * LANES)
    split, ex_start, ex_finish, exkw = _hidden_exchange(ex, 3, 3, (nseq, nstep))

    def body(*refs):
        (q_ref, k_ref, v_ref), (o_ref, tot_ref, first_ref), ex_parts, _ = split(refs)
        ex_start(ex_parts)
        row, col = _sb_masks()
        after = (row > col).astype(BF16)
        lane8 = lax.broadcasted_iota(jnp.int32, (8, LANES), 1)
        hms, tris = _stack_masks()

        def qloop(qi, firsts):
            q0 = pl.multiple_of(qi * SB_BLOCK, SB_BLOCK)
            qm = jnp.where(hms, _stack_heads(q_ref[pl.ds(q0, SB_BLOCK), :] * 0.125), 0.0).astype(BF16)

            def alive(carry):
                return jnp.logical_and(carry[0] <= qi, carry[1] > 0)

            def kstep(carry):
                s, _, acc, lat = carry
                kj = qi - s
                k0 = pl.multiple_of(kj * SB_BLOCK, SB_BLOCK)
                k = k_ref[pl.ds(k0, SB_BLOCK), :].astype(BF16)
                v = v_ref[pl.ds(k0, SB_BLOCK), :].astype(BF16)
                vis = jnp.logical_or(kj < qi, tris)
                w, rs, _ = _sb_weights(_dot(qm, k, 1, 1), vis, functools.partial(lambda rs, lat: lat, lat=lat), after)
                acc = acc + _own_lanes(_dot(w.astype(BF16), v, 1, 0))
                lat = lat + rs
                return s + 1, (jnp.max(lat) > SB_DEAD).astype(jnp.int32), acc, lat

            res = lax.while_loop(alive, kstep, (jnp.int32(0), jnp.int32(1), jnp.zeros((SB_BLOCK, SB_WD), F32),
                                                jnp.zeros((SB_HEADS * SB_BLOCK, 1), F32)))
            o_ref[pl.ds(q0, SB_BLOCK), :] = res[2].astype(o_ref.dtype)
            tot_ref[pl.ds(q0, SB_BLOCK), :] = _own_lanes(jnp.broadcast_to(res[3], (SB_HEADS * SB_BLOCK, SB_WD)))
            return jnp.where(lane8 == qi, (qi - res[0] + 1).astype(F32), firsts)

        first_ref[...] = lax.fori_loop(0, nqb, qloop, jnp.zeros((8, LANES), F32))
        ex_finish(ex_parts)

    blk, qkv, out, vec = _sb_specs(lp, nseq)
    return pl.pallas_call(
        body, name="attn_fwd", grid=(nseq, nstep),
        in_specs=qkv + exkw['in_specs'], out_specs=[out, out, vec] + exkw['out_specs'],
        out_shape=[jax.ShapeDtypeStruct((t, SB_WIDTH), BF16), jax.ShapeDtypeStruct((t, SB_WIDTH), F32)]
        + [jax.ShapeDtypeStruct((nseq, nstep, 8, LANES), F32)] + exkw['out_shape'],
        input_output_aliases=exkw['aliases'], scratch_shapes=exkw['scratch'],
        compiler_params=_params(("arbitrary", "arbitrary")))(proj, proj, proj, *exkw['ins'])


def attn_bwd(proj, tot, first, do, nseq, lp, ex=None):
    nqb = lp // SB_BLOCK
    t = proj.shape[0]
    nstep = SB_WIDTH // (SB_TILES * LANES)
    split, ex_start, ex_finish, exkw = _hidden_exchange(ex, 6, 3, (nseq, nstep))

    def body(*refs):
        (q_ref, k_ref, v_ref, tot_ref, first_ref, do_ref), (dq_ref, dk_ref, dv_ref), ex_parts, _ = split(refs)
        ex_start(ex_parts)
        row, col = _sb_masks()
        after = (row > col).astype(BF16)
        before = (row < col).astype(BF16)
        lane8 = lax.broadcasted_iota(jnp.int32, (8, LANES), 1)
        firsts = first_ref[...]
        dk_ref[...] = jnp.zeros(dk_ref.shape, F32)
        dv_ref[...] = jnp.zeros(dv_ref.shape, F32)
        hms, tris = _stack_masks()

        def qloop(qi, _):
            first = jnp.max(jnp.where(lane8 == qi, firsts, 0.0)).astype(jnp.int32)
            first = jnp.clip(first, 0, qi)
            q0 = pl.multiple_of(qi * SB_BLOCK, SB_BLOCK)
            qm = jnp.where(hms, _stack_heads(q_ref[pl.ds(q0, SB_BLOCK), :] * 0.125), 0.0).astype(BF16)
            dom = jnp.where(hms, _stack_heads(do_ref[pl.ds(q0, SB_BLOCK), :]), 0.0).astype(BF16)
            tot = jnp.max(jnp.where(hms, _stack_heads(tot_ref[pl.ds(q0, SB_BLOCK), :]), -jnp.inf),
                          axis=1, keepdims=True)

            def kloop(kj, carry):
                dq, pre, gpre = carry
                k0 = pl.multiple_of(kj * SB_BLOCK, SB_BLOCK)
                k = k_ref[pl.ds(k0, SB_BLOCK), :].astype(BF16)
                v = v_ref[pl.ds(k0, SB_BLOCK), :].astype(BF16)
                vis = jnp.logical_or(kj < qi, tris)
                later_of = functools.partial(lambda rs, t_, p_: t_ - p_ - rs, t_=tot, p_=pre)
                w, rs, lb = _sb_weights(_dot(qm, k, 1, 1), vis, later_of, after)
                g = _dot(dom, v, 1, 1) * w
                dlk = _dot_exact_rhs(g, before, 1, 0) + gpre
                sig = jnp.exp(lb)
                dz = jnp.where(vis, g * (1.0 - sig) - dlk * sig, 0.0).astype(BF16)
                dk_ref[pl.ds(k0, SB_BLOCK), :] += _dot(dz, qm, 0, 0)
                dv_ref[pl.ds(k0, SB_BLOCK), :] += _dot(w.astype(BF16), dom, 0, 0)
                return dq + _own_lanes(_dot(dz, k, 1, 0)), pre + rs, gpre + jnp.sum(g, axis=1, keepdims=True)

            z1 = jnp.zeros((SB_HEADS * SB_BLOCK, 1), F32)
            res = lax.fori_loop(first, qi + 1, kloop, (jnp.zeros((SB_BLOCK, SB_WD), F32), z1, z1))
            dq_ref[pl.ds(q0, SB_BLOCK), :] = res[0] * 0.125
            return 0

        lax.fori_loop(0, nqb, qloop, 0)
        ex_finish(ex_parts)

    blk, qkv, out, vec = _sb_specs(lp, nseq)
    return pl.pallas_call(
        body, name="attn_bwd", grid=(nseq, nstep),
        in_specs=qkv + [out, vec, out] + exkw['in_specs'], out_specs=[out] * 3 + exkw['out_specs'],
        out_shape=[jax.ShapeDtypeStruct((t, SB_WIDTH), F32)] * 3 + exkw['out_shape'],
        input_output_aliases=exkw['aliases'], scratch_shapes=exkw['scratch'],
        compiler_params=_params(("arbitrary", "arbitrary")))(proj, proj, proj, tot, first, do, *exkw['ins'])


def _hg_loop(nch, step, init):
    assert nch % HG_UNROLL == 0

    def trip(i, carry):
        for u in range(HG_UNROLL):
            carry = step(i * HG_UNROLL + u, carry)
        return carry
    return lax.fori_loop(0, nch // HG_UNROLL, trip, init)


def _hg_gates(fc, lb):
    sg = jax.nn.sigmoid(fc)
    f = lb + (1.0 - lb) * sg
    return sg, f


def _dot_exact_lhs(m, x):
    hi, mid, lo = _split3(x)
    return _dot(m, hi, 1, 0) + (_dot(m, mid, 1, 0) + _dot(m, lo, 1, 0))


def hgrn_fwd(proj, lb, ng, nseq, lp, ex=None):
    nch = lp // HG_CHUNK
    t = proj.shape[0]
    c = HG_CHUNK

    split, ex_start, ex_finish, exkw = _hidden_exchange(ex, 6, 3, (HG_HEADS,))

    def body(*refs):
        (q_ref, f_ref, v_ref, g_ref, lb_ref, ng_ref), (o_ref, og_ref, st_ref), ex_parts, _ = split(refs)
        ex_start(ex_parts)
        lbv, ngv = lb_ref[...], ng_ref[...]

        rr = nseq * c
        seq_r = lax.broadcasted_iota(jnp.int32, (rr, 1), 0) // c
        bd_row = lax.broadcasted_iota(jnp.int32, (rr, rr), 0)
        bd_col = lax.broadcasted_iota(jnp.int32, (rr, rr), 1)
        causal = jnp.logical_and(bd_row // c == bd_col // c, bd_col <= bd_row)
        incl = causal.astype(BF16)

        def stacked(ref, ci):
            return jnp.concatenate([ref[pl.ds(pl.multiple_of(b * lp + ci * c, c), c), :] for b in range(nseq)], axis=0)

        def wide(x):
            return jnp.concatenate([jnp.where(seq_r == b, x, jnp.zeros_like(x)) for b in range(nseq)], axis=1)

        def step(ci, st):
            for b in range(nseq):
                st_ref[b, ci] = st[:, b * HG_DK:(b + 1) * HG_DK]
            _, f = _hg_gates(stacked(f_ref, ci), lbv)
            bcum = _dot_exact_lhs(incl, jnp.log(f))
            lasts = [bcum[b * c + c - 1:b * c + c, :] for b in range(nseq)]
            blast = jnp.concatenate([jnp.broadcast_to(x, (c, LANES)) for x in lasts], axis=0)
            kk = 1.0 - f
            vc = stacked(v_ref, ci).astype(BF16)
            qd = (stacked(q_ref, ci) * jnp.exp(bcum)).astype(BF16)
            kd = (kk * jnp.exp(-bcum)).astype(BF16)
            a = jnp.where(causal, _dot(qd, kd, 1, 1), 0.0)
            o = _dot(a.astype(BF16), vc, 1, 0) + _dot(wide(qd), st.astype(BF16), 1, 1)
            kend = (kk * jnp.exp(blast - bcum)).astype(BF16)
            st = st * jnp.exp(jnp.concatenate(lasts, axis=1)) + _dot(vc, wide(kend), 0, 0)
            r = lax.rsqrt(jnp.mean(o * o, axis=1, keepdims=True) + RMS_EPS)
            gc = stacked(g_ref, ci)
            og = (o * r * ngv * (gc * jax.nn.sigmoid(gc))).astype(og_ref.dtype)
            for b in range(nseq):
                rows = pl.ds(pl.multiple_of(b * lp + ci * c, c), c)
                o_ref[rows, :] = o[b * c:(b + 1) * c]
                og_ref[rows, :] = og[b * c:(b + 1) * c]
            return st

        _hg_loop(nch, step, jnp.zeros((HG_DK, nseq * HG_DK), F32))
        ex_finish(ex_parts)

    blk = (nseq * lp, LANES)
    h = HG_HEADS
    return pl.pallas_call(
        body, name="hgrn_fwd", grid=(h,),
        in_specs=[pl.BlockSpec(blk, lambda hh: (0, hh)),
                  pl.BlockSpec(blk, lambda hh: (0, h + hh)),
                  pl.BlockSpec(blk, lambda hh: (0, 2 * h + hh)),
                  pl.BlockSpec(blk, lambda hh: (0, 3 * h + hh)),
                  pl.BlockSpec((1, LANES), lambda hh: (0, hh)),
                  pl.BlockSpec((1, LANES), lambda hh: (0, hh))] + exkw['in_specs'],
        out_specs=[pl.BlockSpec(blk, lambda hh: (0, hh)),
                   pl.BlockSpec(blk, lambda hh: (0, hh)),
                   pl.BlockSpec((nseq, None, nch, HG_DK, HG_DK), lambda hh: (0, hh, 0, 0, 0))] + exkw['out_specs'],
        out_shape=[jax.ShapeDtypeStruct((t, D_MODEL), F32), jax.ShapeDtypeStruct((t, D_MODEL), BF16),
                   jax.ShapeDtypeStruct((nseq, h, nch, HG_DK, HG_DK), F32)] + exkw['out_shape'],
        input_output_aliases=exkw['aliases'], scratch_shapes=exkw['scratch'],
        compiler_params=_params(("arbitrary",)))(proj, proj, proj, proj, lb, ng, *exkw['ins'])


def hgrn_bwd(proj, o, dog, states, lb, ng, nseq, lp):
    nch = lp // HG_CHUNK
    t = proj.shape[0]
    c = HG_CHUNK
    rr = nseq * c

    def body(q_ref, f_ref, v_ref, g_ref, o_ref, dog_ref, st_ref, lb_ref, ng_ref,
             dq_ref, df_ref, dv_ref, dg_ref, dlb_ref, dng_ref):
        seq_r = lax.broadcasted_iota(jnp.int32, (rr, 1), 0) // c
        last = lax.broadcasted_iota(jnp.int32, (rr, 1), 0) % c == c - 1
        bd_row = lax.broadcasted_iota(jnp.int32, (rr, rr), 0)
        bd_col = lax.broadcasted_iota(jnp.int32, (rr, rr), 1)
        same = bd_row // c == bd_col // c
        causal = jnp.logical_and(same, bd_col <= bd_row)
        incl = causal.astype(BF16)
        from_here = jnp.logical_and(same, bd_col >= bd_row).astype(BF16)
        lbv, ngv = lb_ref[...], ng_ref[...]

        def stacked(ref, ci):
            return jnp.concatenate([ref[pl.ds(pl.multiple_of(b * lp + ci * c, c), c), :] for b in range(nseq)], axis=0)

        def wide(x):
            return jnp.concatenate([jnp.where(seq_r == b, x, jnp.zeros_like(x)) for b in range(nseq)], axis=1)

        def own(x):
            return functools.reduce(jnp.add, [jnp.where(seq_r == b, x[:, b * LANES:(b + 1) * LANES], 0.0)
                                              for b in range(nseq)])

        def per_seq_rows(vals):
            return jnp.concatenate([jnp.broadcast_to(x, (c, LANES)) for x in vals], axis=0)

        def step(s, carry):
            dst, dlb, dng = carry
            ci = nch - 1 - s
            oc, gc, dogc = stacked(o_ref, ci), stacked(g_ref, ci), stacked(dog_ref, ci)
            r = lax.rsqrt(jnp.mean(oc * oc, axis=1, keepdims=True) + RMS_EPS)
            sgg = jax.nn.sigmoid(gc)
            dgc = dogc * (oc * r * ngv) * (sgg * (1.0 + gc * (1.0 - sgg)))
            don = dogc * (gc * sgg)
            dng = dng + _colsum(don * oc * r)
            tt = don * ngv
            do = r * (tt - oc * (r * r) * jnp.mean(tt * oc, axis=1, keepdims=True))
            st = jnp.concatenate([st_ref[b, ci] for b in range(nseq)], axis=1)
            sg, f = _hg_gates(stacked(f_ref, ci), lbv)
            bcum = _dot_exact_lhs(incl, jnp.log(f))
            lasts = [bcum[b * c + c - 1:b * c + c, :] for b in range(nseq)]
            blast = per_seq_rows(lasts)
            kk = 1.0 - f
            qc, vf = stacked(q_ref, ci), stacked(v_ref, ci)
            pdec = jnp.exp(bcum)
            ndec = jnp.exp(-bcum)
            edec = jnp.exp(blast - bcum)
            eb = jnp.exp(jnp.concatenate(lasts, axis=1))
            qd_f, kd_f, kend_f = qc * pdec, kk * ndec, kk * edec
            qd, kd, kend = qd_f.astype(BF16), kd_f.astype(BF16), kend_f.astype(BF16)
            vc, dob, stb, dstb = vf.astype(BF16), do.astype(BF16), st.astype(BF16), dst.astype(BF16)
            a = jnp.where(causal, _dot(qd, kd, 1, 1), 0.0).astype(BF16)
            da = jnp.where(causal, _dot(dob, vc, 1, 1), 0.0).astype(BF16)
            dv = _dot(a, dob, 0, 0) + _dot(wide(kend), dstb, 1, 1)
            dqd = _dot(da, kd, 1, 0) + own(_dot(dob, stb, 1, 0))
            dkd = _dot(da, qd, 0, 0)
            dkend = own(_dot(vc, dstb, 1, 0))
            sts = _colsum(dst * st)
            dkk = dkend * kend_f
            dblast = per_seq_rows([_colsum(jnp.where(seq_r == b, dkk, 0.0))
                                   + eb[:, b * LANES:(b + 1) * LANES] * sts[:, b * LANES:(b + 1) * LANES]
                                   for b in range(nseq)])
            dbcum = dqd * qd_f - dkd * kd_f - dkk
            dbcum = dbcum + jnp.where(last, dblast, 0.0)
            dlf = _dot_exact_lhs(from_here, dbcum)
            df = dlf / f - (dkd * ndec + dkend * edec)
            dfp = df * (1.0 - lbv) * sg * (1.0 - sg)
            dqc = dqd * pdec
            for b in range(nseq):
                rows = pl.ds(pl.multiple_of(b * lp + ci * c, c), c)
                blk_rows = slice(b * c, (b + 1) * c)
                dg_ref[rows, :], dv_ref[rows, :] = dgc[blk_rows], dv[blk_rows]
                dq_ref[rows, :], df_ref[rows, :] = dqc[blk_rows], dfp[blk_rows]
            dlb = dlb + _colsum(df * (1.0 - sg))
            dst = dst * eb + _dot(dob, wide(qd), 0, 0)
            return dst, dlb, dng

        z = jnp.zeros((1, LANES), F32)
        _, dlb, dng = _hg_loop(nch, step, (jnp.zeros((HG_DK, nseq * HG_DK), F32), z, z))
        dlb_ref[...] = dlb
        dng_ref[...] = dng

    blk = (nseq * lp, LANES)
    h = HG_HEADS
    vec = pl.BlockSpec((1, LANES), lambda hh: (0, hh))
    col = pl.BlockSpec(blk, lambda hh: (0, hh))
    return pl.pallas_call(
        body, name="hgrn_bwd", grid=(h,),
        in_specs=[col,
                  pl.BlockSpec(blk, lambda hh: (0, h + hh)),
                  pl.BlockSpec(blk, lambda hh: (0, 2 * h + hh)),
                  pl.BlockSpec(blk, lambda hh: (0, 3 * h + hh)),
                  col, col,
                  pl.BlockSpec((nseq, None, nch, HG_DK, HG_DK), lambda hh: (0, hh, 0, 0, 0)),
                  vec, vec],
        out_specs=[col] * 4 + [vec, vec],
        out_shape=[jax.ShapeDtypeStruct((t, D_MODEL), F32)] * 4 + [jax.ShapeDtypeStruct((1, D_MODEL), F32)] * 2,
        compiler_params=pltpu.CompilerParams(dimension_semantics=("parallel",), vmem_limit_bytes=HGRN_BWD_VMEM))(
            proj, proj, proj, proj, o, dog, states, lb, ng)


def _lower_bound(gamma):
    p = jax.nn.softmax(gamma, axis=0)
    return (jnp.cumsum(p, axis=0) - p[0])[1][None, :]


def local_step(x, tgt, w, hooks=None):
    nseq, seq, _ = x.shape
    lp = -(-(N_META + seq) // SB_BLOCK) * SB_BLOCK
    t = nseq * lp
    pad = lp - N_META - seq
    h0 = jnp.concatenate([jnp.broadcast_to(w['meta'][None], (nseq, N_META, D_MODEL)), x,
                          jnp.zeros((nseq, pad, D_MODEL), F32)], axis=1).reshape(t, D_MODEL)
    tgt_p = jnp.pad(tgt, ((0, 0), (N_META, pad), (0, 0))).reshape(t, D_MODEL)
    row = lambda v: v.reshape(1, -1)
    g = {}

    proj = matmul("in_ab", h0, w['w_in_ab'], 'nn', ex=hooks.gather_soon if hooks else None)
    if hooks:
        proj, soon = proj[0], proj[1:]
    att = attn_fwd(proj, nseq, lp, ex=hooks.gather_late if hooks else None)
    b_out, sb_tot, sb_first = att[:3]
    if hooks:
        w = {**w, **hooks.late_weights(list(soon) + list(att[3:]))}
    s5_names = ['s5_lam_re', 's5_lam_im', 's5_log_dt', 's5_b_re', 's5_b_im', 's5_c_re', 's5_c_im', 's5_d']
    mats, mats_vjp = jax.vjp(s5_matrices, *[w[n][0] for n in s5_names])
    ug = _to_groups(proj[:, :S5_WIDTH])
    s5_pows = s5_scan_powers(*[w[n][0] for n in s5_names[:3]])
    yg, sst = s5_fwd(ug, *mats[:3], *s5_pows, nseq)
    ys5 = _from_groups(yg)
    y = gelu_fwd("gelu", ys5)
    gp = matmul("glu", y, w['s5_w_glu'], 'nn', bias=w['s5_b_glu'])
    a_out = glu_gate("glu_gate", y, gp)
    cat = jnp.concatenate([a_out, b_out], axis=1)
    mix0 = matmul("out_ab", cat, w['w_out_ab'], 'nn')
    h1, xh1, rs1 = ln_fwd("ln_mix0", h0, mix0, row(w['ln_mix_g'][0]), row(w['ln_mix_b'][0]))
    hid0 = matmul("up0", h1, w['mlp_w_up'][0], 'nn', bias=row(w['mlp_b_up'][0]), act='relu2', out_dtype=BF16)
    dn0 = matmul("down0", hid0, w['mlp_w_down'][0], 'nn', bias=row(w['mlp_b_down'][0]))
    h2, xh2, rs2 = ln_fwd("ln_mlp0", h1, dn0, row(w['ln_mlp_g'][0]), row(w['ln_mlp_b'][0]))

    projc = matmul("in_c", h2, w['w_in_c'], 'nn')
    lb, lb_vjp = jax.vjp(_lower_bound, w['hgrn_gamma'])
    ng = w['hgrn_norm_g']
    res = hgrn_fwd(projc, lb, ng, nseq, lp, ex=hooks.gather_last if hooks else None)
    o, og, states = res[:3]
    if hooks:
        w = {**w, **hooks.last_weights(res[3:])}
    mix1 = matmul("out_c", og, w['w_out_c'], 'nn')
    h3, xh3, rs3 = ln_fwd("ln_mix1", h2, mix1, row(w['ln_mix_g'][1]), row(w['ln_mix_b'][1]))
    hid1 = matmul("up1", h3, w['mlp_w_up'][1], 'nn', bias=row(w['mlp_b_up'][1]), act='relu2', out_dtype=BF16)
    dn1 = matmul("down1", hid1, w['mlp_w_down'][1], 'nn', bias=row(w['mlp_b_down'][1]))
    last_ln = ln_loss("ln_mlp1_loss", h3, dn1, row(w['ln_mlp_g'][1]), row(w['ln_mlp_b'][1]), tgt_p, lp, seq)
    loss = last_ln[4]

    def mlp_bwd(l, ln_back, hid, h_in):
        dpre, dg_, db_, dbd = ln_back
        dpu = matmul(f"down{l}_dx", dpre, w['mlp_w_down'][l], 'nt', relu2_of=hid, out_dtype=BF16)
        dwd = matmul(f"down{l}_dw", hid, dpre, 'tn')
        dh_in = matmul(f"up{l}_dx", dpu, w['mlp_w_up'][l], 'nt', add=dpre, add_scale=ALPHA)
        dwu, dbu = matmul(f"up{l}_dw", h_in, dpu, 'tn', out_slots=N_CHIPS, colsum=True)
        return dh_in, dict(ln_mlp_g=dg_, ln_mlp_b=db_, mlp_b_down=dbd, mlp_b_up=dbu, mlp_w_up=dwu, mlp_w_down=dwd)

    dh3, gm1 = mlp_bwd(1, last_ln[:4], hid1, h3)
    dpre, dg1, db1, _ = ln_bwd("ln_mix1_b", dh3, xh3, rs3, row(w['ln_mix_g'][1]))
    g['w_out_c'] = matmul("out_c_dw", og, dpre, 'tn')
    dog = matmul("out_c_dx", dpre, w['w_out_c'], 'nt')
    dq, df, di, dgg, dlb, dng = hgrn_bwd(projc, o, dog, states, lb, ng, nseq, lp)
    dprojc = jnp.concatenate([dq, df, di, dgg], axis=1)
    dh2 = matmul("in_c_dx", dprojc, w['w_in_c'], 'nt', add=dpre, add_scale=ALPHA)
    g['w_in_c'] = matmul("in_c_dw", h2, dprojc, 'tn', out_slots=N_CHIPS)
    g['hgrn_gamma'] = lb_vjp(dlb)[0]
    g['hgrn_norm_g'] = dng

    dh1, gm0 = mlp_bwd(0, ln_bwd("ln_mlp0_b", dh2, xh2, rs2, row(w['ln_mlp_g'][0])), hid0, h1)
    dpre, dg0, db0, _ = ln_bwd("ln_mix0_b", dh1, xh1, rs1, row(w['ln_mix_g'][0]))
    g['w_out_ab'] = matmul("out_ab_dw", cat, dpre, 'tn')
    dcat = matmul("out_ab_dx", dpre, w['w_out_ab'], 'nt')
    dgp, da_s, dbglu = glu_gate_bwd("glu_gate_b", dcat[:, :S5_WIDTH], y, gp)
    g['s5_w_glu'] = matmul("glu_dw", y, dgp, 'tn')
    g['s5_b_glu'] = dbglu
    dy5 = matmul("glu_dx", dgp, w['s5_w_glu'], 'nt', add=da_s)
    dys5 = gelu_bwd("gelu_b", dy5, ys5)
    for n in ('mlp_w_up', 'mlp_w_down'):
        g[n] = [gm0[n], gm1[n]]
    g['ln_mix_g'] = jnp.concatenate([dg0, dg1], axis=0)
    g['ln_mix_b'] = jnp.concatenate([db0, db1], axis=0)
    for n in ('ln_mlp_g', 'ln_mlp_b', 'mlp_b_down', 'mlp_b_up'):
        g[n] = jnp.concatenate([gm0[n], gm1[n]], axis=0)
    res = s5_bwd(_to_groups(dys5), ug, sst, *mats[:3], *s5_pows, nseq, ex=hooks.fold_early(g) if hooks else None)
    dug, da1, da2, da3, dare, daim = res[:6]
    for n, v in zip(s5_names, mats_vjp((da1, da2, da3, dare, daim))):
        g[n] = v[None]
    ex = hooks.scatter_early(g, loss, res[6:]) if hooks else None
    res = attn_bwd(proj, sb_tot, sb_first, dcat[:, S5_WIDTH:], nseq, lp, ex=ex)
    dqa, dka, dva = res[:3]
    if hooks:
        hooks.scattered(res[3:])
    dproj = jnp.concatenate([_from_groups(dug), dqa, dka, dva], axis=1)
    dh0 = matmul("in_ab_dx", dproj, w['w_in_ab'], 'nt', add=dpre, add_scale=ALPHA,
                 ex=hooks.gather_small if hooks else None)
    if hooks:
        dh0 = hooks.small_gathered(dh0)
    g['w_in_ab'] = matmul("in_ab_dw", h0, dproj, 'tn', out_slots=N_CHIPS)

    dh0 = dh0.reshape(nseq, lp, D_MODEL)
    grad_x = dh0[:, N_META:N_META + seq]
    g['meta'] = jnp.sum(dh0[:, :N_META], axis=0)
    return loss, grad_x, g


class Exchange:
    def __init__(self, ins, out_shapes, n_remote, build, in_place):
        self.ins, self.out_shapes, self.n_remote, self.build, self.in_place = ins, out_shapes, n_remote, build, in_place

    def sems(self):
        return [pltpu.SemaphoreType.DMA((self.n_remote,)), pltpu.SemaphoreType.DMA((self.n_remote,))]

    def aliases(self):
        return self.in_place if isinstance(self.in_place, dict) else \
            {i: i for i in range(len(self.ins) if self.in_place else 0)}

    def beside(self, other):
        na, nao = len(self.ins), len(self.out_shapes)

        def build(in_refs, out_refs, me):
            pa = self.build(in_refs[:na], out_refs[:nao], me)
            pb = other.build(in_refs[na:], out_refs[nao:], me)
            n = max(len(pa), len(pb))
            return [(pa[i] if i < len(pa) else []) + (pb[i] if i < len(pb) else []) for i in range(n)]
        al = dict(self.aliases())
        al.update({na + i: nao + j for i, j in other.aliases().items()})
        return Exchange(list(self.ins) + list(other.ins), list(self.out_shapes) + list(other.out_shapes),
                        self.n_remote + other.n_remote, build, al)

    def phases(self, in_refs, out_refs, ssem, rsem):
        me = (lax.axis_index("x"), lax.axis_index("y"), lax.axis_index("c"))
        out, k = [], 0
        for phase in self.build(in_refs, out_refs, me):
            out.append(functools.partial(
                lambda phase, k: [pltpu.make_async_remote_copy(src_ref=s, dst_ref=d, send_sem=ssem.at[k + i],
                                                               recv_sem=rsem.at[k + i], device_id=p,
                                                               device_id_type=MESH)
                                  for i, (s, d, p) in enumerate(phase)], phase, k))
            k += len(phase)
        return out

    def start(self, in_refs, out_refs, ssem, rsem):
        for cp in self.phases(in_refs, out_refs, ssem, rsem)[0]():
            cp.start()

    def finish(self, in_refs, out_refs, ssem, rsem):
        phases = self.phases(in_refs, out_refs, ssem, rsem)
        for cp in phases[0]():
            cp.wait()
        for make in phases[1:]:
            copies = make()
            for cp in copies:
                cp.start()
            for cp in copies:
                cp.wait()


def _exchange(name, ex):
    ni, no = len(ex.ins), len(ex.out_shapes)

    def body(*refs):
        in_refs, out_refs, sems = refs[:ni], refs[ni:ni + no], refs[ni + no:]
        ex.start(in_refs, out_refs, *sems)
        ex.finish(in_refs, out_refs, *sems)

    anyspec = pl.BlockSpec(memory_space=pl.ANY)
    return pl.pallas_call(
        body, name=name, in_specs=[anyspec] * ni, out_specs=[anyspec] * no, out_shape=ex.out_shapes,
        input_output_aliases=ex.aliases(), scratch_shapes=ex.sems())(*ex.ins)


def _chip_peers(me):
    x, y, c = me
    return [(x ^ (m >> 1), y ^ (m & 1), c) for m in (1, 2, 3)]


def _half(ref, c, axis):
    h = ref.shape[axis] // 2
    idx = [slice(None)] * axis + [pl.ds(c * h, h)]
    return ref.at[tuple(idx)]


def _like(arrs):
    return [jax.ShapeDtypeStruct(a.shape, a.dtype) for a in arrs]


def gather_over_chips(bufs):
    def build(in_refs, out_refs, me):
        x, y, c = me
        j = 2 * x + y
        sib = (x, y, 1 - c)
        ici = [(_half(o.at[j], c, 0), _half(o.at[j], c, 0), p) for o in out_refs for p in _chip_peers(me)]
        d2d = [(_half(o.at[2 * p[0] + p[1]], c, 0), _half(o.at[2 * p[0] + p[1]], c, 0), sib)
               for o in out_refs for p in _chip_peers(me)]
        return [ici, d2d]
    return Exchange(bufs, _like(bufs), 6 * len(bufs), build, True)


def fold_cores(fulls):
    def build(in_refs, out_refs, me):
        x, y, c = me
        return [[(_half(s, 1 - c, 1), o, (x, y, 1 - c)) for s, o in zip(in_refs, out_refs)]]
    shapes = [jax.ShapeDtypeStruct((f.shape[0], f.shape[1] // 2, f.shape[2]), f.dtype) for f in fulls]
    return Exchange(fulls, shapes, len(fulls), build, False)


def scatter_over_chips(parts):
    def build(in_refs, out_refs, me):
        j = 2 * me[0] + me[1]
        return [[(s.at[2 * p[0] + p[1]], o.at[j], p) for s, o in zip(in_refs, out_refs) for p in _chip_peers(me)]]
    return Exchange(parts, _like(parts), 3 * len(parts), build, False)


def join_cores(bufs):
    def build(in_refs, out_refs, me):
        x, y, c = me
        return [[(o.at[c], o.at[c], (x, y, 1 - c)) for o in out_refs]]
    return Exchange(bufs, _like(bufs), len(bufs), build, True)


def gather_over_devices(buf):
    def build(in_refs, out_refs, me):
        x, y, c = me
        out = out_refs[0]
        sib = (x, y, 1 - c)
        mine = out.at[4 * x + 2 * y + c]
        first = [(mine, mine, sib)] + [(mine, mine, p) for p in _chip_peers(me)]
        second = [(out.at[4 * p[0] + 2 * p[1] + c], out.at[4 * p[0] + 2 * p[1] + c], sib) for p in _chip_peers(me)]
        return [first, second]
    return Exchange([buf], _like([buf]), N_DEV - 1, build, True)


def _slot_call(name, body, scalars, ins, in_maps, out_shapes, out_maps, blk, grid):
    gs = pltpu.PrefetchScalarGridSpec(
        num_scalar_prefetch=1, grid=grid,
        in_specs=[pl.BlockSpec((None,) + blk, m) for m in in_maps],
        out_specs=[pl.BlockSpec((None,) + blk, m) for m in out_maps])
    return pl.pallas_call(body, name=name, grid_spec=gs, out_shape=out_shapes,
                          compiler_params=_params(("arbitrary",) * len(grid)))(scalars, *ins)


def _group_rows(r, n):
    return _pick(r, (512, 256, 128, 64, 32, 16) if n <= 2 else (256, 128, 64, 32, 16))


def cast_into_slot(name, ws, chip, dtype):
    n = len(ws)
    r, wd = ws[0].shape
    tm = _group_rows(r, n)

    def body(s_ref, *refs):
        for w_ref, o_ref in zip(refs[:n], refs[n:]):
            o_ref[...] = w_ref[...].astype(o_ref.dtype)

    gs = pltpu.PrefetchScalarGridSpec(
        num_scalar_prefetch=1, grid=(r // tm,),
        in_specs=[pl.BlockSpec((tm, wd), lambda i, s: (i, 0))] * n,
        out_specs=[pl.BlockSpec((None, tm, wd), lambda i, s: (s[0], i, 0))] * n)
    return pl.pallas_call(body, name=name, grid_spec=gs,
                          out_shape=[jax.ShapeDtypeStruct((N_CHIPS, r, wd), dtype)] * n,
                          compiler_params=_params(("arbitrary",)))(chip.reshape(1), *ws)


def sum_cores(name, fulls, theirs, core):
    n = len(fulls)
    s, r, wd = fulls[0].shape
    h = r // 2
    tm = _group_rows(h, n)
    nt = h // tm

    def body(s_ref, *refs):
        for k in range(n):
            refs[2 * n + k][...] = (refs[2 * k][...] + refs[2 * k + 1][...]).astype(BF16)

    ins = [x for pair in zip(fulls, theirs) for x in pair]
    return _slot_call(name, body, core.reshape(1), ins,
                      [lambda k, i, s_: (k, s_[0] * nt + i, 0), lambda k, i, s_: (k, i, 0)] * n,
                      [jax.ShapeDtypeStruct((s, h, wd), BF16)] * n, [lambda k, i, s_: (k, i, 0)] * n,
                      (tm, wd), (s, nt))


def sum_chips(name, owns, recvs, chip, core):
    n = len(owns)
    s, r, wd = owns[0].shape
    tm = _group_rows(r, n)

    def body(s_ref, *refs):
        for k in range(n):
            acc = refs[4 * k][...].astype(F32)
            for ref in refs[4 * k + 1:4 * k + 4]:
                acc = acc + ref[...].astype(F32)
            refs[4 * n + k][...] = acc

    maps = [lambda i, s_: (s_[0], i, 0)]
    maps += [functools.partial(lambda i, s_, m: (s_[0] ^ m, i, 0), m=m) for m in (1, 2, 3)]
    ins = [x for own, recv in zip(owns, recvs) for x in (own, recv, recv, recv)]
    return _slot_call(name, body, jnp.stack([chip, core]), ins, maps * n,
                      [jax.ShapeDtypeStruct((2, r, wd), F32)] * n, [lambda i, s_: (s_[1], i, 0)] * n,
                      (tm, wd), (r // tm,))


def sum_slots(name, arr):
    s, r, wd = arr.shape
    tm = _pick(r, (512, 256, 128, 64, 32, 16, 8))

    def body(*refs):
        acc = refs[0][...].astype(F32)
        for ref in refs[1:s]:
            acc = acc + ref[...].astype(F32)
        refs[s][...] = acc

    specs = [pl.BlockSpec((None, tm, wd), functools.partial(lambda i, k: (k, i, 0), k=k)) for k in range(s)]
    return pl.pallas_call(body, name=name, grid=(r // tm,), in_specs=specs,
                          out_specs=pl.BlockSpec((tm, wd), lambda i: (i, 0)),
                          out_shape=jax.ShapeDtypeStruct((r, wd), F32),
                          compiler_params=_params(("parallel",)))(*([arr] * s))


def _pack(arrs):
    flat = jnp.concatenate([a.reshape(-1) for a in arrs])
    n = flat.shape[0]
    padded = -(-n // (512 * LANES)) * (512 * LANES)
    return jnp.pad(flat, (0, padded - n)).reshape(-1, LANES)


def _unpack(packed, shapes):
    flat = packed.reshape(-1)
    out, pos = [], 0
    for s in shapes:
        n = math.prod(s)
        out.append(flat[pos:pos + n].reshape(s))
        pos += n
    return out


def _as2d(a):
    return a.reshape(-1, a.shape[-1])


def kernel(x, meta, w_in_ab, s5_lam_re, s5_lam_im, s5_log_dt, s5_b_re, s5_b_im, s5_c_re, s5_c_im, s5_d, s5_w_glu, s5_b_glu, w_out_ab, w_in_c, hgrn_gamma, hgrn_norm_g, w_out_c, ln_mix_g, ln_mix_b, mlp_w_up, mlp_b_up, mlp_w_down, mlp_b_down, ln_mlp_g, ln_mlp_b, loss_target, m_meta, m_w_in_ab, m_s5_lam_re, m_s5_lam_im, m_s5_log_dt, m_s5_b_re, m_s5_b_im, m_s5_c_re, m_s5_c_im, m_s5_d, m_s5_w_glu, m_s5_b_glu, m_w_out_ab, m_w_in_c, m_hgrn_gamma, m_hgrn_norm_g, m_w_out_c, m_ln_mix_g, m_ln_mix_b, m_mlp_w_up, m_mlp_b_up, m_mlp_w_down, m_mlp_b_down, m_ln_mlp_g, m_ln_mlp_b, v_meta, v_w_in_ab, v_s5_lam_re, v_s5_lam_im, v_s5_log_dt, v_s5_b_re, v_s5_b_im, v_s5_c_re, v_s5_c_im, v_s5_d, v_s5_w_glu, v_s5_b_glu, v_w_out_ab, v_w_in_c, v_hgrn_gamma, v_hgrn_norm_g, v_w_out_c, v_ln_mix_g, v_ln_mix_b, v_mlp_w_up, v_mlp_b_up, v_mlp_w_down, v_mlp_b_down, v_ln_mlp_g, v_ln_mlp_b):
    given = dict(locals())
    wts = {n: given[n] for n in WEIGHTS}
    ms = {n: given['m_' + n] for n in WEIGHTS}
    vs = {n: given['v_' + n] for n in WEIGHTS}
    chip = 2 * lax.axis_index("x") + lax.axis_index("y")

    core = lax.axis_index("c")
    parts = [(n, l) for n in BIG for l in (range(DEPTH) if n.startswith('mlp_') else [0])]

    def by_shape(tag, fn, keys, *lists):
        out = {}
        for shape in dict.fromkeys(a.shape for a in lists[0]):
            idx = [i for i, a in enumerate(lists[0]) if a.shape == shape]
            res = fn(f"{tag}_{keys[idx[0]][0]}{keys[idx[0]][1]}_x{len(idx)}", *[[lst[i] for i in idx] for lst in lists])
            out.update({keys[i]: r for i, r in zip(idx, res)})
        return [out[k] for k in keys]

    shards = dict(zip(parts, by_shape("cast", lambda nm, ws: cast_into_slot(nm, ws, chip, BF16), parts,
                                      [wts[n][l] for n, l in parts])))
    small_sh = jnp.concatenate([meta, hgrn_norm_g, jnp.zeros((15, meta.shape[1]), F32)], axis=0)
    first, late = parts[:1], parts[1:]
    w_in_ab_all, sm = _exchange("gather_first", gather_over_chips(
        [shards[p] for p in first] + list(cast_into_slot("cast_small", [small_sh], chip, F32))))
    w = {n: wts[n] for n in SMALL}
    w['meta'] = sm[:, :N_META].transpose(1, 0, 2).reshape(N_META, D_MODEL)
    w['hgrn_norm_g'] = sm[:, N_META:N_META + 1].transpose(1, 0, 2).reshape(1, D_MODEL)
    w['w_in_ab'] = w_in_ab_all

    def slot_major(v):
        return v if v.ndim == 3 else v.reshape(N_CHIPS, -1, v.shape[-1])

    def whole_grads(ps, g):
        return [slot_major(g[n][l] if n.startswith('mlp_') else g[n]) for n, l in ps]

    def chip_sums_of(ps, gfull, theirs):
        return by_shape("sum_cores", lambda nm, a, b: sum_cores(nm, a, b, core), ps, gfull, list(theirs))

    def all_devices(arrs):
        packed = _pack(arrs)
        return gather_over_devices(lax.dynamic_update_slice(
            jnp.zeros((N_DEV,) + packed.shape, F32), packed[None], (2 * chip + core, 0, 0)))

    early_small = [n for n in SMALL if n != 'meta']
    last = [('w_out_c', 0), ('mlp_w_up', 1), ('mlp_w_down', 1)]
    mid = [p for p in late if p not in last]

    class Hooks:
        gather_soon = gather_over_chips([shards[p] for p in mid[:2]])
        gather_late = gather_over_chips([shards[p] for p in mid[2:]])
        gather_last = gather_over_chips([shards[p] for p in last])

        @staticmethod
        def late_weights(filled):
            fw = Hooks.fw = dict(zip(mid, filled))
            return {'s5_w_glu': fw['s5_w_glu', 0].reshape(S5_WIDTH, S5_WIDTH),
                    'w_out_ab': fw['w_out_ab', 0].reshape(D_MODEL, D_MODEL),
                    'w_in_c': fw['w_in_c', 0],
                    'mlp_w_up': [fw['mlp_w_up', 0]],
                    'mlp_w_down': [fw['mlp_w_down', 0].reshape(D_FF, D_MODEL)]}

        @staticmethod
        def last_weights(filled):
            fw = dict(zip(last, filled))
            return {'w_out_c': fw['w_out_c', 0].reshape(D_MODEL, D_MODEL),
                    'mlp_w_up': [Hooks.fw['mlp_w_up', 0], fw['mlp_w_up', 1]],
                    'mlp_w_down': [Hooks.fw['mlp_w_down', 0].reshape(D_FF, D_MODEL),
                                   fw['mlp_w_down', 1].reshape(D_FF, D_MODEL)]}

        @staticmethod
        def fold_early(g):
            Hooks.whole = whole_grads(late, g)
            return fold_cores(Hooks.whole)

        @staticmethod
        def scatter_early(g, loss, theirs):
            Hooks.sums = chip_sums_of(late, Hooks.whole, theirs)
            Hooks.small_shapes = [(LANES,)] + [g[n].shape for n in early_small]
            Hooks.gather_small = all_devices([loss.reshape(-1)] + [g[n] for n in early_small])
            return scatter_over_chips(Hooks.sums)

        @staticmethod
        def scattered(outs):
            Hooks.recv = list(outs)

        @staticmethod
        def small_gathered(res):
            Hooks.small = res[1]
            return res[0]

    loss, grad_x, g = local_step(x, loss_target, w, Hooks)

    whole = whole_grads(first, g)
    sums = chip_sums_of(first, whole, _exchange("fold_last", fold_cores(whole)))
    *recv, meta_slots = _exchange("scatter_last", scatter_over_chips(sums).beside(all_devices([g['meta']])))
    reduced = _exchange("join_grads", join_cores(
        by_shape("sum_chips", lambda nm, a, b: sum_chips(nm, a, b, chip, core), first + late,
                 sums + Hooks.sums, list(recv) + Hooks.recv)))
    reduced = dict(zip(first + late, [_as2d(r) for r in reduced]))
    red = _unpack(sum_slots("sum_small", Hooks.small), Hooks.small_shapes)
    loss_out = red[0][0]
    gs = dict(zip(early_small, red[1:]))
    gs['meta'] = _unpack(sum_slots("sum_meta", meta_slots), [g['meta'].shape])[0]
    gs['meta'] = lax.dynamic_slice_in_dim(gs['meta'], chip * meta.shape[1], meta.shape[1], axis=1)
    gs['hgrn_norm_g'] = lax.dynamic_slice_in_dim(gs['hgrn_norm_g'].reshape(1, -1), chip * meta.shape[1],
                                                 meta.shape[1], axis=1)

    grads, deltas, new_m, new_v = {}, {}, {}, {}
    for n in BIG:
        r = jnp.concatenate([reduced[n, l] for l in range(wts[n].shape[0])], axis=0)
        res = adamw("adamw_" + n, _as2d(wts[n]), [r], _as2d(ms[n]), _as2d(vs[n]))
        grads[n], deltas[n], new_m[n], new_v[n] = [r.reshape(wts[n].shape) for r in res]
    shapes = [wts[n].shape for n in SMALL]
    res = adamw("adamw_small", _pack([wts[n] for n in SMALL]), [_pack([gs[n] for n in SMALL])],
                _pack([ms[n] for n in SMALL]), _pack([vs[n] for n in SMALL]))
    for d, r in zip((grads, deltas, new_m, new_v), res):
        d.update(zip(SMALL, _unpack(r, shapes)))
    return (loss_out, grad_x, *[grads[n] for n in WEIGHTS], *[deltas[n] for n in WEIGHTS],
            *[new_m[n] for n in WEIGHTS], *[new_v[n] for n in WEIGHTS])
```
